```python
import jax, jax.numpy as jnp
from jax import lax
import numpy as np

D_MODEL = 1024
BATCH = 8
SEQ = 4096
DEPTH = 2

N_MIXERS = 2
N_ATTN_LAYERS = (DEPTH + 1) // 2
N_POOL_LAYERS = DEPTH // 2
GRID_W = 64
EPS = 1e-6

HEAD_DIM = 64
N_HEADS = D_MODEL // HEAD_DIM
N_KV_HEADS = 4
GQA_GROUP = N_HEADS // N_KV_HEADS
Q_BLOCK = 128
ROPE_THETA = 10000.0
ROPE_PAIRS = HEAD_DIM // 4
QKV_DIM = (N_HEADS + 2 * N_KV_HEADS) * HEAD_DIM

POOL_WINDOWS = (2, 4, 8, 16)
N_POOL_GROUPS = len(POOL_WINDOWS)
POOL_GROUP_W = D_MODEL // N_POOL_GROUPS

MEM_LEN = 256
X_HEADS = 4
X_HEAD_DIM = D_MODEL // X_HEADS

D_FF = 2816
CONV_W = 3

kernel_name = "hybrid_axial_gqa_pool_encoder"


def rmsnorm(x, gain):
    x32 = x.astype(jnp.float32)
    y = x32 * lax.rsqrt(jnp.mean(x32 * x32, axis=-1, keepdims=True) + EPS)
    return (y * gain.astype(jnp.float32)).astype(x.dtype)


def axial_rope_tables(seq_len):
    n_rows = seq_len // GRID_W
    row = jnp.repeat(jnp.arange(n_rows, dtype=jnp.float32), GRID_W)
    col = jnp.tile(jnp.arange(GRID_W, dtype=jnp.float32), n_rows)
    inv_freq = ROPE_THETA ** (-jnp.arange(ROPE_PAIRS, dtype=jnp.float32) / ROPE_PAIRS)
    ang = jnp.stack([row[:, None] * inv_freq, col[:, None] * inv_freq], axis=1)
    return jnp.cos(ang), jnp.sin(ang)


def apply_axial_rope(x, cos, sin):
    b, s, h, d = x.shape
    xs = x.astype(jnp.float32).reshape(b, s, h, 2, 2, ROPE_PAIRS)
    x1, x2 = xs[..., 0, :], xs[..., 1, :]
    c = cos[None, :, None]
    sn = sin[None, :, None]
    out = jnp.stack([x1 * c - x2 * sn, x2 * c + x1 * sn], axis=-2)
    return out.reshape(b, s, h, d).astype(x.dtype)


def axial_gqa_attention(h, w_qkv, q_gain, k_gain, w_o):
    b, s, _ = h.shape
    qkv = h @ w_qkv
    q_end = N_HEADS * HEAD_DIM
    k_end = q_end + N_KV_HEADS * HEAD_DIM
    q = qkv[..., :q_end].reshape(b, s, N_HEADS, HEAD_DIM)
    k = qkv[..., q_end:k_end].reshape(b, s, N_KV_HEADS, HEAD_DIM)
    v = qkv[..., k_end:].reshape(b, s, N_KV_HEADS, HEAD_DIM)
    q = rmsnorm(q, q_gain)
    k = rmsnorm(k, k_gain)
    cos, sin = axial_rope_tables(s)
    q = apply_axial_rope(q, cos, sin) * (HEAD_DIM ** -0.5)
    k = apply_axial_rope(k, cos, sin)
    n_blk = s // Q_BLOCK
    qb = q.reshape(b, n_blk, Q_BLOCK, N_KV_HEADS, GQA_GROUP, HEAD_DIM).transpose(1, 0, 3, 4, 2, 5)
    kt = k.transpose(0, 2, 1, 3)
    vt = v.transpose(0, 2, 1, 3)

    def attend_block(q_blk):
        scores = jnp.einsum('bkgqd,bksd->bkgqs', q_blk, kt).astype(jnp.float32)
        p = jax.nn.softmax(scores, axis=-1).astype(vt.dtype)
        return jnp.einsum('bkgqs,bksd->bkgqd', p, vt)

    o = lax.map(attend_block, qb)
    o = o.transpose(1, 0, 4, 2, 3, 5).reshape(b, s, N_HEADS * HEAD_DIM)
    return o @ w_o


def multiscale_pool_mixer(h, w_grp, scale):
    b, s, d = h.shape
    hg = h.astype(jnp.float32).reshape(b, s, N_POOL_GROUPS, POOL_GROUP_W)
    csum = jnp.concatenate(
        [jnp.zeros((b, 1, N_POOL_GROUPS, POOL_GROUP_W), jnp.float32), jnp.cumsum(hg, axis=1)], axis=1)
    t = jnp.arange(s)[:, None]
    win = jnp.array(POOL_WINDOWS, dtype=jnp.int32)[None, :]
    lo = jnp.clip(t - win // 2, 0, s)
    hi = jnp.clip(t + win - win // 2, 0, s)
    s_hi = jnp.take_along_axis(csum, hi[None, :, :, None], axis=1)
    s_lo = jnp.take_along_axis(csum, lo[None, :, :, None], axis=1)
    count = (hi - lo).astype(jnp.float32)[None, :, :, None]
    mixed = (s_hi - s_lo) / count - hg
    y = jnp.einsum('bsgc,gcd->bsgd', mixed, w_grp.astype(jnp.float32)).reshape(b, s, d)
    return (y * scale.astype(jnp.float32)).astype(h.dtype)


def memory_cross_attention(h, mem_n, w_q, w_kv, w_o):
    b, s, _ = h.shape
    m = mem_n.shape[1]
    q = (h @ w_q).reshape(b, s, X_HEADS, X_HEAD_DIM) * (X_HEAD_DIM ** -0.5)
    kv = mem_n @ w_kv
    k = kv[..., :D_MODEL].reshape(b, m, X_HEADS, X_HEAD_DIM)
    v = kv[..., D_MODEL:].reshape(b, m, X_HEADS, X_HEAD_DIM)
    scores = jnp.einsum('bshd,bmhd->bhsm', q, k).astype(jnp.float32)
    p = jax.nn.softmax(scores, axis=-1).astype(v.dtype)
    o = jnp.einsum('bhsm,bmhd->bshd', p, v).reshape(b, s, D_MODEL)
    return o @ w_o


def conv_gated_ffn(h, w_up, conv_w, conv_b, w_down):
    u = h @ w_up
    up = jnp.pad(u, ((0, 0), (1, 1), (0, 0)))
    u = up[:, :-2] * conv_w[0] + up[:, 1:-1] * conv_w[1] + up[:, 2:] * conv_w[2] + conv_b
    gate, val = u[..., :D_FF], u[..., D_FF:]
    return (jax.nn.silu(gate) * val) @ w_down


def _fwd_setup_inputs(seed: int = 0) -> dict:
    key = jax.random.key(seed)
    ks = jax.random.split(key, 24)
    f32 = jnp.float32

    def nrm(k, shape, scale):
        return jax.random.normal(k, shape, f32) * scale

    def gain(k, shape):
        return 1.0 + 0.05 * jax.random.normal(k, shape, f32)

    na, nb = N_ATTN_LAYERS, N_POOL_LAYERS
    return {
        "x": nrm(ks[0], (BATCH, SEQ, D_MODEL), 1.0),
        "mem": nrm(ks[1], (BATCH, MEM_LEN, D_MODEL), 1.0),
        "attn_norm": gain(ks[2], (na, D_MODEL)),
        "attn_w_qkv": nrm(ks[3], (na, D_MODEL, QKV_DIM), D_MODEL ** -0.5),
        "attn_q_gain": gain(ks[4], (na, HEAD_DIM)),
        "attn_k_gain": gain(ks[5], (na, HEAD_DIM)),
        "attn_w_o": nrm(ks[6], (na, N_HEADS * HEAD_DIM, D_MODEL), (N_HEADS * HEAD_DIM) ** -0.5),
        "pool_norm": gain(ks[7], (nb, D_MODEL)),
        "pool_w": nrm(ks[8], (nb, N_POOL_GROUPS, POOL_GROUP_W, POOL_GROUP_W), POOL_GROUP_W ** -0.5),
        "pool_scale": gain(ks[9], (nb, D_MODEL)),
        "xattn_norm": gain(ks[10], (DEPTH, D_MODEL)),
        "mem_norm": gain(ks[11], (DEPTH, D_MODEL)),
        "xattn_w_q": nrm(ks[12], (DEPTH, D_MODEL, D_MODEL), D_MODEL ** -0.5),
        "xattn_w_kv": nrm(ks[13], (DEPTH, D_MODEL, 2 * D_MODEL), D_MODEL ** -0.5),
        "xattn_w_o": nrm(ks[14], (DEPTH, D_MODEL, D_MODEL), D_MODEL ** -0.5),
        "ffn_norm": gain(ks[15], (DEPTH, D_MODEL)),
        "ffn_w_up": nrm(ks[16], (DEPTH, D_MODEL, 2 * D_FF), D_MODEL ** -0.5),
        "ffn_conv_w": nrm(ks[17], (DEPTH, CONV_W, 2 * D_FF), CONV_W ** -0.5),
        "ffn_conv_b": nrm(ks[18], (DEPTH, 2 * D_FF), 0.02),
        "ffn_w_down": nrm(ks[19], (DEPTH, D_FF, D_MODEL), D_FF ** -0.5),
        "final_norm": gain(ks[20], (D_MODEL,)),
    }


def _fwd_reference(x, mem, attn_norm, attn_w_qkv, attn_q_gain, attn_k_gain, attn_w_o,
              pool_norm, pool_w, pool_scale,
              xattn_norm, mem_norm, xattn_w_q, xattn_w_kv, xattn_w_o,
              ffn_norm, ffn_w_up, ffn_conv_w, ffn_conv_b, ffn_w_down,
              final_norm):
    ia = 0
    ib = 0
    for i in range(DEPTH):
        if i % N_MIXERS == 0:
            x = x + axial_gqa_attention(rmsnorm(x, attn_norm[ia]), attn_w_qkv[ia],
                                        attn_q_gain[ia], attn_k_gain[ia], attn_w_o[ia])
            ia += 1
        else:
            x = x + multiscale_pool_mixer(rmsnorm(x, pool_norm[ib]), pool_w[ib], pool_scale[ib])
            ib += 1
        x = x + memory_cross_attention(rmsnorm(x, xattn_norm[i]), rmsnorm(mem, mem_norm[i]),
                                       xattn_w_q[i], xattn_w_kv[i], xattn_w_o[i])
        x = x + conv_gated_ffn(rmsnorm(x, ffn_norm[i]), ffn_w_up[i], ffn_conv_w[i],
                               ffn_conv_b[i], ffn_w_down[i])
    return rmsnorm(x, final_norm)


import jax as _jax
import jax.numpy as _jnp

TWIN_FORMAT = 'train_step'
FWD_PARAMS = ['x', 'mem', 'attn_norm', 'attn_w_qkv', 'attn_q_gain', 'attn_k_gain', 'attn_w_o', 'pool_norm', 'pool_w', 'pool_scale', 'xattn_norm', 'mem_norm', 'xattn_w_q', 'xattn_w_kv', 'xattn_w_o', 'ffn_norm', 'ffn_w_up', 'ffn_conv_w', 'ffn_conv_b', 'ffn_w_down', 'final_norm']
TWIN_WEIGHTS = ['attn_norm', 'attn_w_qkv', 'attn_q_gain', 'attn_k_gain', 'attn_w_o', 'pool_norm', 'pool_w', 'pool_scale', 'xattn_norm', 'mem_norm', 'xattn_w_q', 'xattn_w_kv', 'xattn_w_o', 'ffn_norm', 'ffn_w_up', 'ffn_conv_w', 'ffn_conv_b', 'ffn_w_down', 'final_norm']
TWIN_DIFF_INPUT = 'x'
TWIN_INPUTS = ['x', 'mem', 'attn_norm', 'attn_w_qkv', 'attn_q_gain', 'attn_k_gain', 'attn_w_o', 'pool_norm', 'pool_w', 'pool_scale', 'xattn_norm', 'mem_norm', 'xattn_w_q', 'xattn_w_kv', 'xattn_w_o', 'ffn_norm', 'ffn_w_up', 'ffn_conv_w', 'ffn_conv_b', 'ffn_w_down', 'final_norm', 'loss_target', 'm_attn_norm', 'm_attn_w_qkv', 'm_attn_q_gain', 'm_attn_k_gain', 'm_attn_w_o', 'm_pool_norm', 'm_pool_w', 'm_pool_scale', 'm_xattn_norm', 'm_mem_norm', 'm_xattn_w_q', 'm_xattn_w_kv', 'm_xattn_w_o', 'm_ffn_norm', 'm_ffn_w_up', 'm_ffn_conv_w', 'm_ffn_conv_b', 'm_ffn_w_down', 'm_final_norm', 'v_attn_norm', 'v_attn_w_qkv', 'v_attn_q_gain', 'v_attn_k_gain', 'v_attn_w_o', 'v_pool_norm', 'v_pool_w', 'v_pool_scale', 'v_xattn_norm', 'v_mem_norm', 'v_xattn_w_q', 'v_xattn_w_kv', 'v_xattn_w_o', 'v_ffn_norm', 'v_ffn_w_up', 'v_ffn_conv_w', 'v_ffn_conv_b', 'v_ffn_w_down', 'v_final_norm']
TWIN_OUTPUTS = ['loss', 'grad_x', 'grad_attn_norm', 'grad_attn_w_qkv', 'grad_attn_q_gain', 'grad_attn_k_gain', 'grad_attn_w_o', 'grad_pool_norm', 'grad_pool_w', 'grad_pool_scale', 'grad_xattn_norm', 'grad_mem_norm', 'grad_xattn_w_q', 'grad_xattn_w_kv', 'grad_xattn_w_o', 'grad_ffn_norm', 'grad_ffn_w_up', 'grad_ffn_conv_w', 'grad_ffn_conv_b', 'grad_ffn_w_down', 'grad_final_norm', 'delta_attn_norm', 'delta_attn_w_qkv', 'delta_attn_q_gain', 'delta_attn_k_gain', 'delta_attn_w_o', 'delta_pool_norm', 'delta_pool_w', 'delta_pool_scale', 'delta_xattn_norm', 'delta_mem_norm', 'delta_xattn_w_q', 'delta_xattn_w_kv', 'delta_xattn_w_o', 'delta_ffn_norm', 'delta_ffn_w_up', 'delta_ffn_conv_w', 'delta_ffn_conv_b', 'delta_ffn_w_down', 'delta_final_norm', 'new_m_attn_norm', 'new_m_attn_w_qkv', 'new_m_attn_q_gain', 'new_m_attn_k_gain', 'new_m_attn_w_o', 'new_m_pool_norm', 'new_m_pool_w', 'new_m_pool_scale', 'new_m_xattn_norm', 'new_m_mem_norm', 'new_m_xattn_w_q', 'new_m_xattn_w_kv', 'new_m_xattn_w_o', 'new_m_ffn_norm', 'new_m_ffn_w_up', 'new_m_ffn_conv_w', 'new_m_ffn_conv_b', 'new_m_ffn_w_down', 'new_m_final_norm', 'new_v_attn_norm', 'new_v_attn_w_qkv', 'new_v_attn_q_gain', 'new_v_attn_k_gain', 'new_v_attn_w_o', 'new_v_pool_norm', 'new_v_pool_w', 'new_v_pool_scale', 'new_v_xattn_norm', 'new_v_mem_norm', 'new_v_xattn_w_q', 'new_v_xattn_w_kv', 'new_v_xattn_w_o', 'new_v_ffn_norm', 'new_v_ffn_w_up', 'new_v_ffn_conv_w', 'new_v_ffn_conv_b', 'new_v_ffn_w_down', 'new_v_final_norm']
TWIN_LEAF_KINDS = {'loss': 'loss', 'grad_x': 'grad_x', 'grad_attn_norm': 'grad_w', 'grad_attn_w_qkv': 'grad_w', 'grad_attn_q_gain': 'grad_w', 'grad_attn_k_gain': 'grad_w', 'grad_attn_w_o': 'grad_w', 'grad_pool_norm': 'grad_w', 'grad_pool_w': 'grad_w', 'grad_pool_scale': 'grad_w', 'grad_xattn_norm': 'grad_w', 'grad_mem_norm': 'grad_w', 'grad_xattn_w_q': 'grad_w', 'grad_xattn_w_kv': 'grad_w', 'grad_xattn_w_o': 'grad_w', 'grad_ffn_norm': 'grad_w', 'grad_ffn_w_up': 'grad_w', 'grad_ffn_conv_w': 'grad_w', 'grad_ffn_conv_b': 'grad_w', 'grad_ffn_w_down': 'grad_w', 'grad_final_norm': 'grad_w', 'delta_attn_norm': 'delta_w', 'delta_attn_w_qkv': 'delta_w', 'delta_attn_q_gain': 'delta_w', 'delta_attn_k_gain': 'delta_w', 'delta_attn_w_o': 'delta_w', 'delta_pool_norm': 'delta_w', 'delta_pool_w': 'delta_w', 'delta_pool_scale': 'delta_w', 'delta_xattn_norm': 'delta_w', 'delta_mem_norm': 'delta_w', 'delta_xattn_w_q': 'delta_w', 'delta_xattn_w_kv': 'delta_w', 'delta_xattn_w_o': 'delta_w', 'delta_ffn_norm': 'delta_w', 'delta_ffn_w_up': 'delta_w', 'delta_ffn_conv_w': 'delta_w', 'delta_ffn_conv_b': 'delta_w', 'delta_ffn_w_down': 'delta_w', 'delta_final_norm': 'delta_w', 'new_m_attn_norm': 'new_m', 'new_m_attn_w_qkv': 'new_m', 'new_m_attn_q_gain': 'new_m', 'new_m_attn_k_gain': 'new_m', 'new_m_attn_w_o': 'new_m', 'new_m_pool_norm': 'new_m', 'new_m_pool_w': 'new_m', 'new_m_pool_scale': 'new_m', 'new_m_xattn_norm': 'new_m', 'new_m_mem_norm': 'new_m', 'new_m_xattn_w_q': 'new_m', 'new_m_xattn_w_kv': 'new_m', 'new_m_xattn_w_o': 'new_m', 'new_m_ffn_norm': 'new_m', 'new_m_ffn_w_up': 'new_m', 'new_m_ffn_conv_w': 'new_m', 'new_m_ffn_conv_b': 'new_m', 'new_m_ffn_w_down': 'new_m', 'new_m_final_norm': 'new_m', 'new_v_attn_norm': 'new_v', 'new_v_attn_w_qkv': 'new_v', 'new_v_attn_q_gain': 'new_v', 'new_v_attn_k_gain': 'new_v', 'new_v_attn_w_o': 'new_v', 'new_v_pool_norm': 'new_v', 'new_v_pool_w': 'new_v', 'new_v_pool_scale': 'new_v', 'new_v_xattn_norm': 'new_v', 'new_v_mem_norm': 'new_v', 'new_v_xattn_w_q': 'new_v', 'new_v_xattn_w_kv': 'new_v', 'new_v_xattn_w_o': 'new_v', 'new_v_ffn_norm': 'new_v', 'new_v_ffn_w_up': 'new_v', 'new_v_ffn_conv_w': 'new_v', 'new_v_ffn_conv_b': 'new_v', 'new_v_ffn_w_down': 'new_v', 'new_v_final_norm': 'new_v'}


def _forward(args):
    return _fwd_reference(*[args[k] for k in FWD_PARAMS])


def _output_shape():
    out = _jax.eval_shape(lambda: _forward(_fwd_setup_inputs(0)))
    return out.shape, out.dtype

N_MICROBATCH = 1
ADAM_LR = 0.001
ADAM_B1 = 0.9
ADAM_B2 = 0.999
ADAM_EPS = 1e-08
ADAM_WD = 0.01
ADAM_STEP = 10
PER_EXAMPLE_BATCH_AXIS = {'x': 0, 'mem': 0, 'loss_target': 0}
SHARED_INPUTS = []
_WEIGHT_DTYPES = {'attn_norm': _jnp.float32, 'attn_w_qkv': _jnp.float32, 'attn_q_gain': _jnp.float32, 'attn_k_gain': _jnp.float32, 'attn_w_o': _jnp.float32, 'pool_norm': _jnp.float32, 'pool_w': _jnp.float32, 'pool_scale': _jnp.float32, 'xattn_norm': _jnp.float32, 'mem_norm': _jnp.float32, 'xattn_w_q': _jnp.float32, 'xattn_w_kv': _jnp.float32, 'xattn_w_o': _jnp.float32, 'ffn_norm': _jnp.float32, 'ffn_w_up': _jnp.float32, 'ffn_conv_w': _jnp.float32, 'ffn_conv_b': _jnp.float32, 'ffn_w_down': _jnp.float32, 'final_norm': _jnp.float32}
MOMENT_SCALE = {'attn_norm': 3.785281e-02, 'attn_w_qkv': 3.092940e-02, 'attn_q_gain': 9.353551e-02, 'attn_k_gain': 8.523991e-02, 'attn_w_o': 2.155892e-02, 'pool_norm': 1.160813e-01, 'pool_w': 1.237522e-01, 'pool_scale': 7.905832e-01, 'xattn_norm': 1.912923e-02, 'mem_norm': 2.766156e-02, 'xattn_w_q': 1.879714e-02, 'xattn_w_kv': 1.892282e-02, 'xattn_w_o': 1.862993e-02, 'ffn_norm': 1.231980e-01, 'ffn_w_up': 5.270758e-02, 'ffn_conv_w': 5.243679e-02, 'ffn_conv_b': 5.052559e-02, 'ffn_w_down': 8.634536e-02, 'final_norm': 3.217653e+01}


def _to_microbatches(a, axis):
    t = _jnp.moveaxis(a, axis, 0)
    t = t.reshape((N_MICROBATCH, t.shape[0] // N_MICROBATCH) + t.shape[1:])
    return _jnp.moveaxis(t, 1, axis + 1)


def setup_inputs(seed: int = 0) -> dict:
    inp = _fwd_setup_inputs(seed)
    key = _jax.random.fold_in(_jax.random.key(seed), 7919)
    shape, _ = _output_shape()
    out = dict(inp)
    out["loss_target"] = _jax.random.normal(_jax.random.fold_in(key, 0), shape, _jnp.float32)
    for i, name in enumerate(TWIN_WEIGHTS):
        w = inp[name].astype(_jnp.float32)
        if MOMENT_SCALE is None:
            s = _jnp.sqrt(_jnp.mean(_jnp.square(w)) + 1e-30)
        else:
            s = MOMENT_SCALE[name]
        km, kv = _jax.random.split(_jax.random.fold_in(key, i + 1))
        out[name] = w
        out["m_" + name] = s * _jax.random.normal(km, w.shape, _jnp.float32)
        out["v_" + name] = (s * s) * _jax.random.uniform(kv, w.shape, _jnp.float32, 0.5, 1.5)
    if N_MICROBATCH > 1:
        for name, axis in PER_EXAMPLE_BATCH_AXIS.items():
            out[name] = _to_microbatches(out[name], axis)
    return {'x': out['x'], 'mem': out['mem'], 'attn_norm': out['attn_norm'], 'attn_w_qkv': out['attn_w_qkv'], 'attn_q_gain': out['attn_q_gain'], 'attn_k_gain': out['attn_k_gain'], 'attn_w_o': out['attn_w_o'], 'pool_norm': out['pool_norm'], 'pool_w': out['pool_w'], 'pool_scale': out['pool_scale'], 'xattn_norm': out['xattn_norm'], 'mem_norm': out['mem_norm'], 'xattn_w_q': out['xattn_w_q'], 'xattn_w_kv': out['xattn_w_kv'], 'xattn_w_o': out['xattn_w_o'], 'ffn_norm': out['ffn_norm'], 'ffn_w_up': out['ffn_w_up'], 'ffn_conv_w': out['ffn_conv_w'], 'ffn_conv_b': out['ffn_conv_b'], 'ffn_w_down': out['ffn_w_down'], 'final_norm': out['final_norm'], 'loss_target': out['loss_target'], 'm_attn_norm': out['m_attn_norm'], 'm_attn_w_qkv': out['m_attn_w_qkv'], 'm_attn_q_gain': out['m_attn_q_gain'], 'm_attn_k_gain': out['m_attn_k_gain'], 'm_attn_w_o': out['m_attn_w_o'], 'm_pool_norm': out['m_pool_norm'], 'm_pool_w': out['m_pool_w'], 'm_pool_scale': out['m_pool_scale'], 'm_xattn_norm': out['m_xattn_norm'], 'm_mem_norm': out['m_mem_norm'], 'm_xattn_w_q': out['m_xattn_w_q'], 'm_xattn_w_kv': out['m_xattn_w_kv'], 'm_xattn_w_o': out['m_xattn_w_o'], 'm_ffn_norm': out['m_ffn_norm'], 'm_ffn_w_up': out['m_ffn_w_up'], 'm_ffn_conv_w': out['m_ffn_conv_w'], 'm_ffn_conv_b': out['m_ffn_conv_b'], 'm_ffn_w_down': out['m_ffn_w_down'], 'm_final_norm': out['m_final_norm'], 'v_attn_norm': out['v_attn_norm'], 'v_attn_w_qkv': out['v_attn_w_qkv'], 'v_attn_q_gain': out['v_attn_q_gain'], 'v_attn_k_gain': out['v_attn_k_gain'], 'v_attn_w_o': out['v_attn_w_o'], 'v_pool_norm': out['v_pool_norm'], 'v_pool_w': out['v_pool_w'], 'v_pool_scale': out['v_pool_scale'], 'v_xattn_norm': out['v_xattn_norm'], 'v_mem_norm': out['v_mem_norm'], 'v_xattn_w_q': out['v_xattn_w_q'], 'v_xattn_w_kv': out['v_xattn_w_kv'], 'v_xattn_w_o': out['v_xattn_w_o'], 'v_ffn_norm': out['v_ffn_norm'], 'v_ffn_w_up': out['v_ffn_w_up'], 'v_ffn_conv_w': out['v_ffn_conv_w'], 'v_ffn_conv_b': out['v_ffn_conv_b'], 'v_ffn_w_down': out['v_ffn_w_down'], 'v_final_norm': out['v_final_norm']}


def _loss(weights, diff, rest, loss_target):
    with _jax.named_scope("forward"):
        args = {**rest, TWIN_DIFF_INPUT: diff, **{k: w.astype(_WEIGHT_DTYPES[k]) for k, w in weights.items()}}
        y = _forward(args)
    with _jax.named_scope("loss_head"):
        err = _jnp.square(y.astype(_jnp.float32) - loss_target)
        return 0.5 * _jnp.sum(_jnp.mean(err, axis=-1)) if err.ndim else 0.5 * err


def _adamw(w, g, m, v):
    m = ADAM_B1 * m + (1.0 - ADAM_B1) * g
    v = ADAM_B2 * v + (1.0 - ADAM_B2) * _jnp.square(g)
    m_hat = m / (1.0 - ADAM_B1 ** ADAM_STEP)
    v_hat = v / (1.0 - ADAM_B2 ** ADAM_STEP)
    delta = -ADAM_LR * (m_hat / (_jnp.sqrt(v_hat) + ADAM_EPS) + ADAM_WD * w)
    return delta, m, v


def reference(x, mem, attn_norm, attn_w_qkv, attn_q_gain, attn_k_gain, attn_w_o, pool_norm, pool_w, pool_scale, xattn_norm, mem_norm, xattn_w_q, xattn_w_kv, xattn_w_o, ffn_norm, ffn_w_up, ffn_conv_w, ffn_conv_b, ffn_w_down, final_norm, loss_target, m_attn_norm, m_attn_w_qkv, m_attn_q_gain, m_attn_k_gain, m_attn_w_o, m_pool_norm, m_pool_w, m_pool_scale, m_xattn_norm, m_mem_norm, m_xattn_w_q, m_xattn_w_kv, m_xattn_w_o, m_ffn_norm, m_ffn_w_up, m_ffn_conv_w, m_ffn_conv_b, m_ffn_w_down, m_final_norm, v_attn_norm, v_attn_w_qkv, v_attn_q_gain, v_attn_k_gain, v_attn_w_o, v_pool_norm, v_pool_w, v_pool_scale, v_xattn_norm, v_mem_norm, v_xattn_w_q, v_xattn_w_kv, v_xattn_w_o, v_ffn_norm, v_ffn_w_up, v_ffn_conv_w, v_ffn_conv_b, v_ffn_w_down, v_final_norm):
    given = dict(x=x, mem=mem, attn_norm=attn_norm, attn_w_qkv=attn_w_qkv, attn_q_gain=attn_q_gain, attn_k_gain=attn_k_gain, attn_w_o=attn_w_o, pool_norm=pool_norm, pool_w=pool_w, pool_scale=pool_scale, xattn_norm=xattn_norm, mem_norm=mem_norm, xattn_w_q=xattn_w_q, xattn_w_kv=xattn_w_kv, xattn_w_o=xattn_w_o, ffn_norm=ffn_norm, ffn_w_up=ffn_w_up, ffn_conv_w=ffn_conv_w, ffn_conv_b=ffn_conv_b, ffn_w_down=ffn_w_down, final_norm=final_norm, loss_target=loss_target, m_attn_norm=m_attn_norm, m_attn_w_qkv=m_attn_w_qkv, m_attn_q_gain=m_attn_q_gain, m_attn_k_gain=m_attn_k_gain, m_attn_w_o=m_attn_w_o, m_pool_norm=m_pool_norm, m_pool_w=m_pool_w, m_pool_scale=m_pool_scale, m_xattn_norm=m_xattn_norm, m_mem_norm=m_mem_norm, m_xattn_w_q=m_xattn_w_q, m_xattn_w_kv=m_xattn_w_kv, m_xattn_w_o=m_xattn_w_o, m_ffn_norm=m_ffn_norm, m_ffn_w_up=m_ffn_w_up, m_ffn_conv_w=m_ffn_conv_w, m_ffn_conv_b=m_ffn_conv_b, m_ffn_w_down=m_ffn_w_down, m_final_norm=m_final_norm, v_attn_norm=v_attn_norm, v_attn_w_qkv=v_attn_w_qkv, v_attn_q_gain=v_attn_q_gain, v_attn_k_gain=v_attn_k_gain, v_attn_w_o=v_attn_w_o, v_pool_norm=v_pool_norm, v_pool_w=v_pool_w, v_pool_scale=v_pool_scale, v_xattn_norm=v_xattn_norm, v_mem_norm=v_mem_norm, v_xattn_w_q=v_xattn_w_q, v_xattn_w_kv=v_xattn_w_kv, v_xattn_w_o=v_xattn_w_o, v_ffn_norm=v_ffn_norm, v_ffn_w_up=v_ffn_w_up, v_ffn_conv_w=v_ffn_conv_w, v_ffn_conv_b=v_ffn_conv_b, v_ffn_w_down=v_ffn_w_down, v_final_norm=v_final_norm)
    weights = {n: given[n] for n in TWIN_WEIGHTS}
    shared = {n: given[n] for n in SHARED_INPUTS}
    per_example = {n: given[n] for n in ['x', 'mem']}
    grad_fn = _jax.value_and_grad(_loss, argnums=(0, 1))

    def one_microbatch(ex, loss_target):
        ex = dict(ex)
        diff = ex.pop(TWIN_DIFF_INPUT)
        return grad_fn(weights, diff, {**shared, **ex}, loss_target)

    if N_MICROBATCH == 1:
        loss, (grad_w, grad_x) = one_microbatch(per_example, given["loss_target"])
    else:
        def body(carry, xs):
            loss_sum, grad_sum = carry
            l_k, (gw_k, gx_k) = one_microbatch(xs[0], xs[1])
            with _jax.named_scope("update"):
                return (loss_sum + l_k, _jax.tree.map(_jnp.add, grad_sum, gw_k)), gx_k

        init = (_jnp.zeros((), _jnp.float32), _jax.tree.map(_jnp.zeros_like, weights))
        (loss, grad_w), grad_x = _jax.lax.scan(body, init, (per_example, given["loss_target"]))
    with _jax.named_scope("update"):
        delta_w, new_m, new_v = {}, {}, {}
        for n in TWIN_WEIGHTS:
            delta_w[n], new_m[n], new_v[n] = _adamw(weights[n], grad_w[n], given["m_" + n], given["v_" + n])
    return (loss, grad_x, *[grad_w[n] for n in TWIN_WEIGHTS], *[delta_w[n] for n in TWIN_WEIGHTS],
            *[new_m[n] for n in TWIN_WEIGHTS], *[new_v[n] for n in TWIN_WEIGHTS])
```

```python
import functools
import itertools

import jax
import jax.numpy as jnp
from jax import lax
from jax.experimental import pallas as pl
from jax.experimental.pallas import tpu as pltpu

F32, BF16 = jnp.float32, jnp.bfloat16
EPS = 1e-6
GRID_W = 64
ROPE_THETA = 10000.0
HEAD_DIM = 64
N_HEADS = 16
N_KV = 4
X_HEADS = 4
X_HEAD_DIM = 256
POOL_GROUPS = 4
POOL_GROUP_W = 256
HALO = 8
LANES = 128
ADAM_LR, ADAM_B1, ADAM_B2, ADAM_EPS, ADAM_WD, ADAM_STEP = 0.001, 0.9, 0.999, 1e-08, 0.01, 10
VMEM_LIMIT = 48 * 1024 * 1024
MESH = pl.DeviceIdType.MESH
NEG = -1e30
ANY = pl.BlockSpec(memory_space=pl.ANY)


def _cp(sem=None):
    return pltpu.CompilerParams(dimension_semantics=sem, vmem_limit_bytes=VMEM_LIMIT)


def _pick(n, cands):
    for c in cands:
        if c <= n and n % c == 0:
            return c
    return n


def _mm(a, b, mode, *, name, out_dtype, tm=512, tn=None, tk=None, n=None, k=None, b_l=None, b_off=(0, 0),
        res=None, scale=None, out_full=None, out_l=None, out_off=(0, 0), alias=None):
    if mode == "tn":
        K, M = a.shape
    else:
        M, K = a.shape
    bs = b.shape[-2:]
    if mode == "nn":
        K = k or K
        N = n or bs[1]
    elif mode == "nt":
        N = n or bs[0]
    else:
        N = n or bs[1]
    tm = _pick(M, (tm, 256, 128))
    tn = tn or _pick(N, (512, 1408, 384, 256, 128))
    tk = tk or _pick(K, (1024, 1408, 512, 256, 128))
    assert M % tm == 0 and N % tn == 0 and K % tk == 0, (name, M, N, K, tm, tn, tk)
    nk = K // tk
    dims = {"nn": ((1,), (0,)), "nt": ((1,), (1,)), "tn": ((0,), (0,))}[mode]

    if mode == "tn":
        a_spec = pl.BlockSpec((tk, tm), lambda i, j, kk: (kk, i))
    else:
        a_spec = pl.BlockSpec((tm, tk), lambda i, j, kk: (i, kk))
    if mode == "nt":
        bb, (d0, d1) = (tn, tk), (b_off[0] // tn, b_off[1] // tk)
        assert b_off[0] % tn == 0 and b_off[1] % tk == 0
        bidx = lambda i, j, kk: (j + d0, kk + d1)
    else:
        bb, (d0, d1) = (tk, tn), (b_off[0] // tk, b_off[1] // tn)
        assert b_off[0] % tk == 0 and b_off[1] % tn == 0
        bidx = lambda i, j, kk: (kk + d0, j + d1)
    if b.ndim == 3:
        b_spec = pl.BlockSpec((None,) + bb, lambda i, j, kk: (b_l,) + bidx(i, j, kk))
    else:
        b_spec = pl.BlockSpec(bb, bidx)
    in_specs, operands = [a_spec, b_spec], [a, b]
    if res is not None:
        in_specs.append(pl.BlockSpec((tm, tn), lambda i, j, kk: (i, j)))
        operands.append(res)
    aliases = {}
    if alias is not None:
        aliases = {len(operands): 0}
        in_specs.append(ANY)
        operands.append(alias)
    if out_full is None:
        out_shape = jax.ShapeDtypeStruct((M, N), out_dtype)
        out_spec = pl.BlockSpec((tm, tn), lambda i, j, kk: (i, j))
    else:
        assert out_off[0] % tm == 0 and out_off[1] % tn == 0
        o0, o1 = out_off[0] // tm, out_off[1] // tn
        out_shape = jax.ShapeDtypeStruct(out_full, out_dtype)
        out_spec = pl.BlockSpec((None, tm, tn), lambda i, j, kk: (out_l, i + o0, j + o1))
    has_res, has_alias = res is not None, alias is not None

    def body(*refs):
        a_ref, b_ref = refs[0], refs[1]
        pos = 2
        res_ref = None
        if has_res:
            res_ref = refs[pos]
            pos += 1
        if has_alias:
            pos += 1
        o_ref, acc_ref = refs[pos], refs[pos + 1]
        kk = pl.program_id(2)
        part = lax.dot_general(a_ref[...].astype(BF16), b_ref[...].astype(BF16), (dims, ((), ())),
                               preferred_element_type=F32)

        def finish(acc):
            if scale is not None:
                acc = acc * scale
            if res_ref is not None:
                acc = acc + res_ref[...]
            o_ref[...] = acc.astype(o_ref.dtype)

        if nk == 1:
            finish(part)
        else:
            @pl.when(kk == 0)
            def _():
                acc_ref[...] = part

            @pl.when(jnp.logical_and(kk > 0, kk < nk - 1))
            def _():
                acc_ref[...] += part

            @pl.when(kk == nk - 1)
            def _():
                finish(acc_ref[...] + part)

    return pl.pallas_call(
        body, out_shape=out_shape, grid=(M // tm, N // tn, nk), in_specs=in_specs, out_specs=out_spec,
        scratch_shapes=[pltpu.VMEM((tm, tn) if nk > 1 else (8, 128), F32)], input_output_aliases=aliases,
        compiler_params=_cp(("parallel", "parallel", "arbitrary")), name=name)(*operands)


def _rms_fwd(x, gain, out_dtype, name):
    rows, d = x.shape
    tr = _pick(rows, (512, 256))

    def body(x_ref, g_ref, o_ref):
        xv = x_ref[...]
        r = lax.rsqrt(jnp.mean(xv * xv, axis=-1, keepdims=True) + EPS)
        o_ref[...] = (xv * r * g_ref[...]).astype(o_ref.dtype)

    return pl.pallas_call(
        body, out_shape=jax.ShapeDtypeStruct((rows, d), out_dtype), grid=(rows // tr,),
        in_specs=[pl.BlockSpec((tr, d), lambda i: (i, 0)), pl.BlockSpec((1, d), lambda i: (0, 0))],
        out_specs=pl.BlockSpec((tr, d), lambda i: (i, 0)), compiler_params=_cp(("parallel",)), name=name)(x, gain)


def _rms_bwd(x, gain, dh, dres, name):
    rows, d = x.shape
    tr = _pick(rows, (512, 256))
    need_dx = dres is not None

    def body(*refs):
        if need_dx:
            x_ref, g_ref, dh_ref, dres_ref, o_ref, dg_ref = refs
        else:
            x_ref, g_ref, dh_ref, dg_ref = refs
        i = pl.program_id(0)
        xv = x_ref[...]
        dhv = dh_ref[...].astype(F32)
        r = lax.rsqrt(jnp.mean(xv * xv, axis=-1, keepdims=True) + EPS)
        nv = xv * r
        part = jnp.sum(dhv * nv, axis=0, keepdims=True)

        @pl.when(i == 0)
        def _():
            dg_ref[...] = part

        @pl.when(i > 0)
        def _():
            dg_ref[...] += part

        if need_dx:
            dn = dhv * g_ref[...]
            dx = r * (dn - nv * jnp.mean(dn * nv, axis=-1, keepdims=True))
            o_ref[...] = dres_ref[...] + dx

    row_spec = pl.BlockSpec((tr, d), lambda i: (i, 0))
    vec_spec = pl.BlockSpec((1, d), lambda i: (0, 0))
    if need_dx:
        return pl.pallas_call(
            body, out_shape=(jax.ShapeDtypeStruct((rows, d), F32), jax.ShapeDtypeStruct((1, d), F32)),
            grid=(rows // tr,), in_specs=[row_spec, vec_spec, row_spec, row_spec], out_specs=(row_spec, vec_spec),
            compiler_params=_cp(("arbitrary",)), name=name)(x, gain, dh, dres)
    return None, pl.pallas_call(
        body, out_shape=jax.ShapeDtypeStruct((1, d), F32), grid=(rows // tr,),
        in_specs=[row_spec, vec_spec, row_spec], out_specs=vec_spec,
        compiler_params=_cp(("arbitrary",)), name=name)(x, gain, dh)


def _final_loss(x, gain, target, name):
    rows, d = x.shape
    tr = _pick(rows, (512, 256))
    nsteps = rows // tr

    def body(x_ref, g_ref, t_ref, dx_ref, dg_ref, loss_ref, acc_ref):
        i = pl.program_id(0)
        xv = x_ref[...]
        g = g_ref[...]
        r = lax.rsqrt(jnp.mean(xv * xv, axis=-1, keepdims=True) + EPS)
        nv = xv * r
        err = nv * g - t_ref[...]
        dy = err * (1.0 / d)
        dn = dy * g
        dx_ref[...] = r * (dn - nv * jnp.mean(dn * nv, axis=-1, keepdims=True))
        dgp = jnp.sum(dy * nv, axis=0, keepdims=True)
        lp = jnp.sum(err * err, axis=0, keepdims=True)

        @pl.when(i == 0)
        def _():
            dg_ref[...] = dgp
            acc_ref[...] = lp

        @pl.when(i > 0)
        def _():
            dg_ref[...] += dgp
            acc_ref[...] += lp

        @pl.when(i == nsteps - 1)
        def _():
            tot = jnp.sum(acc_ref[...], axis=1, keepdims=True) * (0.5 / d)
            loss_ref[...] = jnp.broadcast_to(tot, loss_ref.shape)

    row_spec = pl.BlockSpec((tr, d), lambda i: (i, 0))
    vec_spec = pl.BlockSpec((1, d), lambda i: (0, 0))
    return pl.pallas_call(
        body, out_shape=(jax.ShapeDtypeStruct((rows, d), F32), jax.ShapeDtypeStruct((1, d), F32),
                         jax.ShapeDtypeStruct((1, LANES), F32)),
        grid=(nsteps,), in_specs=[row_spec, vec_spec, row_spec],
        out_specs=(row_spec, vec_spec, pl.BlockSpec((1, LANES), lambda i: (0, 0))),
        scratch_shapes=[pltpu.VMEM((1, d), F32)], compiler_params=_cp(("arbitrary",)), name=name)(x, gain, target)


def _rope_tables(seq):
    n_rows = seq // GRID_W
    row = jnp.repeat(jnp.arange(n_rows, dtype=F32), GRID_W)
    col = jnp.tile(jnp.arange(GRID_W, dtype=F32), n_rows)
    pairs = HEAD_DIM // 4
    inv_freq = ROPE_THETA ** (-jnp.arange(pairs, dtype=F32) / pairs)
    ang = jnp.stack([row[:, None] * inv_freq, col[:, None] * inv_freq], axis=1)
    cos, sin = jnp.cos(ang), jnp.sin(ang)
    zero = jnp.zeros_like(sin[:, 0])
    c64 = jnp.concatenate([cos[:, 0], cos[:, 0], cos[:, 1], cos[:, 1]], axis=1)
    sp64 = jnp.concatenate([zero, sin[:, 0], zero, sin[:, 1]], axis=1)
    sm64 = jnp.concatenate([-sin[:, 0], zero, -sin[:, 1], zero], axis=1)
    return tuple(jnp.concatenate([t, t], axis=1) for t in (c64, sp64, sm64))


def _pair_norm(xv, lo):
    sq = xv * xv
    s_lo = jnp.sum(jnp.where(lo, sq, 0.0), axis=1, keepdims=True)
    s_hi = jnp.sum(jnp.where(lo, 0.0, sq), axis=1, keepdims=True)
    return lax.rsqrt(jnp.where(lo, s_lo, s_hi) * (1.0 / HEAD_DIM) + EPS)


def _rope(y, c, sp, sm):
    return y * c + pltpu.roll(y, 16, axis=1) * sp + pltpu.roll(y, LANES - 16, axis=1) * sm


def _rope_t(dz, c, sp, sm):
    return dz * c + pltpu.roll(dz * sp, LANES - 16, axis=1) + pltpu.roll(dz * sm, 16, axis=1)


def _qk_prep(qkv, qg2, kg2, tabs, name):
    seq = qkv.shape[0]
    ts = _pick(seq, (256, 128))
    nq, nkp = N_HEADS // 2, N_KV // 2
    qw, kw = N_HEADS * HEAD_DIM, N_KV * HEAD_DIM

    def body(x_ref, qg_ref, kg_ref, c_ref, sp_ref, sm_ref, q_ref, k_ref, kt_ref, v_ref, vt_ref):
        lo = lax.broadcasted_iota(jnp.int32, (ts, LANES), 1) < HEAD_DIM
        c, sp, sm = c_ref[...], sp_ref[...], sm_ref[...]
        for i in range(nq):
            xv = x_ref[:, i * LANES:(i + 1) * LANES]
            y = xv * _pair_norm(xv, lo) * qg_ref[...]
            q_ref[:, i * LANES:(i + 1) * LANES] = (_rope(y, c, sp, sm) * (HEAD_DIM ** -0.5)).astype(BF16)
        for i in range(nkp):
            xv = x_ref[:, qw + i * LANES:qw + (i + 1) * LANES]
            z = _rope(xv * _pair_norm(xv, lo) * kg_ref[...], c, sp, sm)
            k_ref[:, i * LANES:(i + 1) * LANES] = z.astype(BF16)
            kt_ref[i * LANES:(i + 1) * LANES, :] = z.T.astype(BF16)
            vv = x_ref[:, qw + kw + i * LANES:qw + kw + (i + 1) * LANES]
            v_ref[:, i * LANES:(i + 1) * LANES] = vv.astype(BF16)
            vt_ref[i * LANES:(i + 1) * LANES, :] = vv.T.astype(BF16)

    tab = pl.BlockSpec((ts, LANES), lambda i: (i, 0))
    vec = pl.BlockSpec((1, LANES), lambda i: (0, 0))
    return pl.pallas_call(
        body,
        out_shape=(jax.ShapeDtypeStruct((seq, qw), BF16), jax.ShapeDtypeStruct((seq, kw), BF16),
                   jax.ShapeDtypeStruct((kw, seq), BF16), jax.ShapeDtypeStruct((seq, kw), BF16),
                   jax.ShapeDtypeStruct((kw, seq), BF16)),
        grid=(seq // ts,),
        in_specs=[pl.BlockSpec((ts, qw + 2 * kw), lambda i: (i, 0)), vec, vec, tab, tab, tab],
        out_specs=(pl.BlockSpec((ts, qw), lambda i: (i, 0)), pl.BlockSpec((ts, kw), lambda i: (i, 0)),
                   pl.BlockSpec((kw, ts), lambda i: (0, i)), pl.BlockSpec((ts, kw), lambda i: (i, 0)),
                   pl.BlockSpec((kw, ts), lambda i: (0, i))),
        compiler_params=_cp(("parallel",)), name=name)(qkv, qg2, kg2, *tabs)


def _qk_prep_bwd(qkv, dq, dk, dv, qg2, kg2, tabs, name):
    seq = qkv.shape[0]
    ts = _pick(seq, (256, 128))
    nq, nkp = N_HEADS // 2, N_KV // 2
    qw, kw = N_HEADS * HEAD_DIM, N_KV * HEAD_DIM

    def body(x_ref, dq_ref, dk_ref, dv_ref, qg_ref, kg_ref, c_ref, sp_ref, sm_ref, o_ref, dqg_ref, dkg_ref):
        step = pl.program_id(0)
        lo = lax.broadcasted_iota(jnp.int32, (ts, LANES), 1) < HEAD_DIM
        c, sp, sm = c_ref[...], sp_ref[...], sm_ref[...]

        def one(xv, dz, gain):
            r = _pair_norm(xv, lo)
            nv = xv * r
            dy = _rope_t(dz, c, sp, sm)
            dgp = jnp.sum(dy * nv, axis=0, keepdims=True)
            dn = dy * gain
            t = dn * nv
            m_lo = jnp.sum(jnp.where(lo, t, 0.0), axis=1, keepdims=True)
            m_hi = jnp.sum(jnp.where(lo, 0.0, t), axis=1, keepdims=True)
            m = jnp.where(lo, m_lo, m_hi) * (1.0 / HEAD_DIM)
            return r * (dn - nv * m), dgp

        dqg = jnp.zeros((1, LANES), F32)
        for i in range(nq):
            sl = slice(i * LANES, (i + 1) * LANES)
            dx, dgp = one(x_ref[:, sl], dq_ref[:, sl] * (HEAD_DIM ** -0.5), qg_ref[...])
            o_ref[:, sl] = dx.astype(BF16)
            dqg = dqg + dgp
        dkg = jnp.zeros((1, LANES), F32)
        for i in range(nkp):
            sl = slice(i * LANES, (i + 1) * LANES)
            dx, dgp = one(x_ref[:, qw + i * LANES:qw + (i + 1) * LANES], dk_ref[:, sl], kg_ref[...])
            o_ref[:, qw + i * LANES:qw + (i + 1) * LANES] = dx.astype(BF16)
            dkg = dkg + dgp
            o_ref[:, qw + kw + i * LANES:qw + kw + (i + 1) * LANES] = dv_ref[:, sl].astype(BF16)

        @pl.when(step == 0)
        def _():
            dqg_ref[...] = dqg
            dkg_ref[...] = dkg

        @pl.when(step > 0)
        def _():
            dqg_ref[...] += dqg
            dkg_ref[...] += dkg

    tab = pl.BlockSpec((ts, LANES), lambda i: (i, 0))
    vec = pl.BlockSpec((1, LANES), lambda i: (0, 0))
    return pl.pallas_call(
        body,
        out_shape=(jax.ShapeDtypeStruct((seq, qw + 2 * kw), BF16), jax.ShapeDtypeStruct((1, LANES), F32),
                   jax.ShapeDtypeStruct((1, LANES), F32)),
        grid=(seq // ts,),
        in_specs=[pl.BlockSpec((ts, qw + 2 * kw), lambda i: (i, 0)), pl.BlockSpec((ts, qw), lambda i: (i, 0)),
                  pl.BlockSpec((ts, kw), lambda i: (i, 0)), pl.BlockSpec((ts, kw), lambda i: (i, 0)),
                  vec, vec, tab, tab, tab],
        out_specs=(pl.BlockSpec((ts, qw + 2 * kw), lambda i: (i, 0)), vec, vec),
        compiler_params=_cp(("arbitrary",)), name=name)(qkv, dq, dk, dv, qg2, kg2, *tabs)


def _slot(blk, off0, tq):
    half = lax.broadcasted_iota(jnp.int32, (tq, LANES), 1) // HEAD_DIM
    keep = half == jnp.where(off0, 0, 1)
    parts = []
    for p in range(2):
        pair = blk[:, p * LANES:(p + 1) * LANES].astype(F32)
        rolled = pltpu.roll(pair, HEAD_DIM, axis=1)
        parts.append(jnp.where(keep, jnp.where(off0, pair, rolled), 0.0))
        parts.append(jnp.where(keep, jnp.where(off0, rolled, pair), 0.0))
    return jnp.concatenate(parts, axis=0)


def _unslot(x4, off0, tq):
    lo = lax.broadcasted_iota(jnp.int32, (tq, LANES), 1) < HEAD_DIM
    pairs = []
    for p in range(2):
        h0 = x4[(2 * p) * tq:(2 * p + 1) * tq]
        h1 = x4[(2 * p + 1) * tq:(2 * p + 2) * tq]
        a = jnp.where(off0, h0, pltpu.roll(h0, HEAD_DIM, axis=1))
        b = jnp.where(off0, pltpu.roll(h1, HEAD_DIM, axis=1), h1)
        pairs.append(jnp.where(lo, a, b))
    return jnp.concatenate(pairs, axis=1)


def _flash_fwd(q, k, vt, name):
    seq = q.shape[0]
    tq = _pick(seq, (256, 128))
    tk = _pick(seq, (512, 256, 128))
    nq, nkv = seq // tq, seq // tk
    gw = 4 * HEAD_DIM

    def body(q_ref, k_ref, vt_ref, o_ref, lse_ref, q4_ref, m_ref, l_ref, acc_ref):
        g, ki = pl.program_id(0), pl.program_id(2)
        off0 = (g % 2) == 0

        @pl.when(ki == 0)
        def _():
            q4_ref[...] = _slot(q_ref[...], off0, tq).astype(BF16)
            m_ref[...] = jnp.full(m_ref.shape, NEG, F32)
            l_ref[...] = jnp.zeros(l_ref.shape, F32)
            acc_ref[...] = jnp.zeros(acc_ref.shape, F32)

        st = lax.dot_general(k_ref[...], q4_ref[...], (((1,), (1,)), ((), ())), preferred_element_type=F32)
        m_old = m_ref[...]
        m_new = jnp.maximum(m_old, jnp.max(st, axis=0, keepdims=True))
        alpha = jnp.exp(m_old - m_new)
        pt = jnp.exp(st - m_new)
        l_ref[...] = alpha * l_ref[...] + jnp.sum(pt, axis=0, keepdims=True)
        acc_ref[...] = alpha * acc_ref[...] + jnp.dot(vt_ref[...], pt.astype(BF16), preferred_element_type=F32)
        m_ref[...] = m_new

        @pl.when(ki == nkv - 1)
        def _():
            l = l_ref[...]
            o4 = (acc_ref[...] / l).T
            o_ref[...] = _unslot(o4, off0, tq).astype(o_ref.dtype)
            lse_ref[...] = jnp.broadcast_to(m_ref[...] + jnp.log(l), lse_ref.shape)

    return pl.pallas_call(
        body,
        out_shape=(jax.ShapeDtypeStruct((seq, N_HEADS * HEAD_DIM), BF16),
                   jax.ShapeDtypeStruct((N_KV * nq * 8, 4 * tq), F32)),
        grid=(N_KV, nq, nkv),
        in_specs=[pl.BlockSpec((tq, gw), lambda g, qi, ki: (qi, g)),
                  pl.BlockSpec((tk, LANES), lambda g, qi, ki: (ki, g // 2)),
                  pl.BlockSpec((LANES, tk), lambda g, qi, ki: (g // 2, ki))],
        out_specs=(pl.BlockSpec((tq, gw), lambda g, qi, ki: (qi, g)),
                   pl.BlockSpec((8, 4 * tq), lambda g, qi, ki: (g * nq + qi, 0))),
        scratch_shapes=[pltpu.VMEM((4 * tq, LANES), BF16), pltpu.VMEM((1, 4 * tq), F32),
                        pltpu.VMEM((1, 4 * tq), F32), pltpu.VMEM((LANES, 4 * tq), F32)],
        compiler_params=_cp(("parallel", "parallel", "arbitrary")), name=name)(q, k, vt)


def _flash_bwd(q, k, kt, v, do, o, lse, name):
    seq = q.shape[0]
    tq = _pick(seq, (256, 128))
    tk = _pick(seq, (512, 256, 128))
    nq, nkv = seq // tq, seq // tk
    gw = 4 * HEAD_DIM

    def body(q_ref, k_ref, kt_ref, v_ref, do_ref, o_ref, lse_ref, dq_ref, dk_ref, dv_ref,
             q4_ref, do4_ref, delta_ref, dqt_ref):
        g, qi, ki = pl.program_id(0), pl.program_id(1), pl.program_id(2)
        off0 = (g % 2) == 0

        @pl.when(jnp.logical_and(g % 2 == 0, jnp.logical_and(qi == 0, ki == 0)))
        def _():
            dk_ref[...] = jnp.zeros(dk_ref.shape, F32)
            dv_ref[...] = jnp.zeros(dv_ref.shape, F32)

        @pl.when(ki == 0)
        def _():
            q4_ref[...] = _slot(q_ref[...], off0, tq).astype(BF16)
            do4 = _slot(do_ref[...], off0, tq)
            do4_ref[...] = do4.astype(BF16)
            o4 = _slot(o_ref[...], off0, tq)
            delta_ref[...] = jnp.sum((do4 * o4).T, axis=0, keepdims=True)
            dqt_ref[...] = jnp.zeros(dqt_ref.shape, F32)

        q4, do4 = q4_ref[...], do4_ref[...]
        st = lax.dot_general(k_ref[...], q4, (((1,), (1,)), ((), ())), preferred_element_type=F32)
        pt = jnp.exp(st - lse_ref[0:1, :])
        dpt = lax.dot_general(v_ref[...], do4, (((1,), (1,)), ((), ())), preferred_element_type=F32)
        dst = (pt * (dpt - delta_ref[...])).astype(BF16)
        rows = pl.ds(pl.multiple_of(ki * tk, tk), tk)
        dv_ref[rows, :] += jnp.dot(pt.astype(BF16), do4, preferred_element_type=F32)
        dk_ref[rows, :] += jnp.dot(dst, q4, preferred_element_type=F32)
        dqt_ref[...] += jnp.dot(kt_ref[...], dst, preferred_element_type=F32)

        @pl.when(ki == nkv - 1)
        def _():
            dq_ref[...] = _unslot(dqt_ref[...].T, off0, tq)

    return pl.pallas_call(
        body,
        out_shape=(jax.ShapeDtypeStruct((seq, N_HEADS * HEAD_DIM), F32),
                   jax.ShapeDtypeStruct((seq, N_KV * HEAD_DIM), F32), jax.ShapeDtypeStruct((seq, N_KV * HEAD_DIM), F32)),
        grid=(N_KV, nq, nkv),
        in_specs=[pl.BlockSpec((tq, gw), lambda g, qi, ki: (qi, g)),
                  pl.BlockSpec((tk, LANES), lambda g, qi, ki: (ki, g // 2)),
                  pl.BlockSpec((LANES, tk), lambda g, qi, ki: (g // 2, ki)),
                  pl.BlockSpec((tk, LANES), lambda g, qi, ki: (ki, g // 2)),
                  pl.BlockSpec((tq, gw), lambda g, qi, ki: (qi, g)),
                  pl.BlockSpec((tq, gw), lambda g, qi, ki: (qi, g)),
                  pl.BlockSpec((8, 4 * tq), lambda g, qi, ki: (g * nq + qi, 0))],
        out_specs=(pl.BlockSpec((tq, gw), lambda g, qi, ki: (qi, g)),
                   pl.BlockSpec((seq, LANES), lambda g, qi, ki: (0, g // 2)),
                   pl.BlockSpec((seq, LANES), lambda g, qi, ki: (0, g // 2))),
        scratch_shapes=[pltpu.VMEM((4 * tq, LANES), BF16), pltpu.VMEM((4 * tq, LANES), BF16),
                        pltpu.VMEM((1, 4 * tq), F32), pltpu.VMEM((LANES, 4 * tq), F32)],
        compiler_params=_cp(("arbitrary", "arbitrary", "arbitrary")), name=name)(q, k, kt, v, do, o, lse)


def _xattn_fwd(q, kv, name):
    seq, d = q.shape
    mlen = kv.shape[0]
    tq = _pick(seq, (512, 256))

    def body(q_ref, k_ref, v_ref, o_ref):
        for h in range(X_HEADS):
            sl = slice(h * X_HEAD_DIM, (h + 1) * X_HEAD_DIM)
            s = lax.dot_general(q_ref[:, sl], k_ref[:, sl], (((1,), (1,)), ((), ())), preferred_element_type=F32)
            e = jnp.exp(s - jnp.max(s, axis=-1, keepdims=True))
            p = e / jnp.sum(e, axis=-1, keepdims=True)
            o_ref[:, sl] = jnp.dot(p.astype(BF16), v_ref[:, sl], preferred_element_type=F32).astype(o_ref.dtype)

    return pl.pallas_call(
        body, out_shape=jax.ShapeDtypeStruct((seq, d), BF16), grid=(seq // tq,),
        in_specs=[pl.BlockSpec((tq, d), lambda i: (i, 0)), pl.BlockSpec((mlen, d), lambda i: (0, 0)),
                  pl.BlockSpec((mlen, d), lambda i: (0, 1))],
        out_specs=pl.BlockSpec((tq, d), lambda i: (i, 0)), compiler_params=_cp(("parallel",)), name=name)(q, kv, kv)


def _xattn_bwd(q, kv, do, name):
    seq, d = q.shape
    mlen = kv.shape[0]
    tq = _pick(seq, (512, 256))
    scale = X_HEAD_DIM ** -0.5

    def body(q_ref, k_ref, v_ref, do_ref, dq_ref, dkv_ref):
        i = pl.program_id(0)

        @pl.when(i == 0)
        def _():
            dkv_ref[...] = jnp.zeros(dkv_ref.shape, F32)

        for h in range(X_HEADS):
            sl = slice(h * X_HEAD_DIM, (h + 1) * X_HEAD_DIM)
            qh, kh, vh = q_ref[:, sl], k_ref[:, sl], v_ref[:, sl]
            doh = do_ref[:, sl].astype(BF16)
            st = lax.dot_general(kh, qh, (((1,), (1,)), ((), ())), preferred_element_type=F32)
            e = jnp.exp(st - jnp.max(st, axis=0, keepdims=True))
            pt = e / jnp.sum(e, axis=0, keepdims=True)
            dpt = lax.dot_general(vh, doh, (((1,), (1,)), ((), ())), preferred_element_type=F32)
            dst = (pt * (dpt - jnp.sum(pt * dpt, axis=0, keepdims=True))).astype(BF16)
            dkv_ref[:, sl] += jnp.dot(dst, qh, preferred_element_type=F32)
            dkv_ref[:, d + h * X_HEAD_DIM:d + (h + 1) * X_HEAD_DIM] += jnp.dot(pt.astype(BF16), doh,
                                                                                 preferred_element_type=F32)
            dqh = lax.dot_general(dst, kh, (((0,), (0,)), ((), ())), preferred_element_type=F32)
            dq_ref[:, sl] = (dqh * scale).astype(dq_ref.dtype)

    return pl.pallas_call(
        body, out_shape=(jax.ShapeDtypeStruct((seq, d), BF16), jax.ShapeDtypeStruct((mlen, 2 * d), F32)),
        grid=(seq // tq,),
        in_specs=[pl.BlockSpec((tq, d), lambda i: (i, 0)), pl.BlockSpec((mlen, d), lambda i: (0, 0)),
                  pl.BlockSpec((mlen, d), lambda i: (0, 1)), pl.BlockSpec((tq, d), lambda i: (i, 0))],
        out_specs=(pl.BlockSpec((tq, d), lambda i: (i, 0)), pl.BlockSpec((mlen, 2 * d), lambda i: (0, 0))),
        compiler_params=_cp(("arbitrary",)), name=name)(q, kv, kv, do)


def _halo_specs(tr, tc, seq, col):
    per, last = tr // HALO, seq // HALO - 1
    return [pl.BlockSpec((tr, tc), lambda j, r: (r, col(j))),
            pl.BlockSpec((HALO, tc), lambda j, r: (jnp.maximum(r * per - 1, 0), col(j))),
            pl.BlockSpec((HALO, tc), lambda j, r: (jnp.minimum((r + 1) * per, last), col(j)))]


def _extend(main_ref, prev_ref, next_ref, r, nr):
    pv = (r > 0).astype(F32)
    nv = (r < nr - 1).astype(F32)
    return jnp.concatenate([prev_ref[...].astype(F32) * pv, main_ref[...].astype(F32),
                            next_ref[...].astype(F32) * nv], axis=0)


def _conv3(e, w_ref, n):
    return pltpu.roll(e, 1, axis=0) * w_ref[0:1, :] + e * w_ref[1:2, :] + pltpu.roll(e, n - 1, axis=0) * w_ref[2:3, :]


def _conv_gate_fwd(ug, uv, cw, cb, layer, name):
    seq, f = ug.shape
    tc = 256
    tr = _pick(seq, (512, 256))
    nc, nr = f // tc, seq // tr
    n = tr + 2 * HALO

    def body(g_ref, gp_ref, gn_ref, v_ref, vp_ref, vn_ref, wg_ref, wv_ref, bg_ref, bv_ref, o_ref):
        r = pl.program_id(1)
        cg = _conv3(_extend(g_ref, gp_ref, gn_ref, r, nr), wg_ref, n)[HALO:HALO + tr] + bg_ref[...]
        cv = _conv3(_extend(v_ref, vp_ref, vn_ref, r, nr), wv_ref, n)[HALO:HALO + tr] + bv_ref[...]
        o_ref[...] = (cg * jax.nn.sigmoid(cg) * cv).astype(o_ref.dtype)

    w_spec = lambda shift: pl.BlockSpec((None, 3, tc), lambda j, r: (layer, 0, j + shift))
    b_spec = lambda shift: pl.BlockSpec((None, 1, tc), lambda j, r: (layer, 0, j + shift))
    return pl.pallas_call(
        body, out_shape=jax.ShapeDtypeStruct((seq, f), BF16), grid=(nc, nr),
        in_specs=_halo_specs(tr, tc, seq, lambda j: j) * 2 + [w_spec(0), w_spec(nc), b_spec(0), b_spec(nc)],
        out_specs=pl.BlockSpec((tr, tc), lambda j, r: (r, j)),
        compiler_params=_cp(("parallel", "parallel")), name=name)(ug, ug, ug, uv, uv, uv, cw, cw, cb, cb)


def _conv_gate_bwd(ug, uv, dact, cw, cb, layer, name):
    seq, f = ug.shape
    tc = 256
    tr = _pick(seq, (512, 256))
    nc, nr = f // tc, seq // tr
    n = tr + 2 * HALO

    def body(g_ref, gp_ref, gn_ref, v_ref, vp_ref, vn_ref, d_ref, dp_ref, dn_ref, wg_ref, wv_ref, bg_ref, bv_ref,
             dug_ref, duv_ref, dwg_ref, dwv_ref):
        r = pl.program_id(1)
        eg = _extend(g_ref, gp_ref, gn_ref, r, nr)
        ev = _extend(v_ref, vp_ref, vn_ref, r, nr)
        da = _extend(d_ref, dp_ref, dn_ref, r, nr)
        cg = _conv3(eg, wg_ref, n) + bg_ref[...]
        cv = _conv3(ev, wv_ref, n) + bv_ref[...]
        sg = jax.nn.sigmoid(cg)
        dcv = da * (cg * sg)
        dcg = da * cv * (sg * (1.0 + cg * (1.0 - sg)))

        def back(dc, e, w_ref, du_ref, dw_ref):
            du = (pltpu.roll(dc, n - 1, axis=0) * w_ref[0:1, :] + dc * w_ref[1:2, :]
                  + pltpu.roll(dc, 1, axis=0) * w_ref[2:3, :])
            du_ref[...] = du[HALO:HALO + tr].astype(du_ref.dtype)
            dcm = dc[HALO:HALO + tr]
            taps = [jnp.sum(dcm * pltpu.roll(e, 1, axis=0)[HALO:HALO + tr], axis=0, keepdims=True),
                    jnp.sum(dcm * e[HALO:HALO + tr], axis=0, keepdims=True),
                    jnp.sum(dcm * pltpu.roll(e, n - 1, axis=0)[HALO:HALO + tr], axis=0, keepdims=True),
                    jnp.sum(dcm, axis=0, keepdims=True)]
            part = jnp.concatenate(taps + [jnp.zeros((4, tc), F32)], axis=0)

            @pl.when(r == 0)
            def _():
                dw_ref[...] = part

            @pl.when(r > 0)
            def _():
                dw_ref[...] += part

        back(dcg, eg, wg_ref, dug_ref, dwg_ref)
        back(dcv, ev, wv_ref, duv_ref, dwv_ref)

    w_spec = lambda shift: pl.BlockSpec((None, 3, tc), lambda j, r: (layer, 0, j + shift))
    b_spec = lambda shift: pl.BlockSpec((None, 1, tc), lambda j, r: (layer, 0, j + shift))
    out_rows = pl.BlockSpec((tr, tc), lambda j, r: (r, j))
    out_acc = pl.BlockSpec((8, tc), lambda j, r: (0, j))
    return pl.pallas_call(
        body,
        out_shape=(jax.ShapeDtypeStruct((seq, f), BF16), jax.ShapeDtypeStruct((seq, f), BF16),
                   jax.ShapeDtypeStruct((8, f), F32), jax.ShapeDtypeStruct((8, f), F32)),
        grid=(nc, nr),
        in_specs=_halo_specs(tr, tc, seq, lambda j: j) * 3 + [w_spec(0), w_spec(nc), b_spec(0), b_spec(nc)],
        out_specs=(out_rows, out_rows, out_acc, out_acc),
        compiler_params=_cp(("parallel", "arbitrary")), name=name)(ug, ug, ug, uv, uv, uv, dact, dact, dact, cw, cw, cb, cb)


def _pool_count(g, r, tr, n, seq):
    half = jnp.left_shift(1, g)
    t = r * tr - HALO + lax.broadcasted_iota(jnp.int32, (n, 1), 0)
    cnt = jnp.minimum(t + half, seq) - jnp.maximum(t - half, 0)
    return jnp.maximum(cnt, 1).astype(F32)


def _by_group(g, levels):
    out = levels[3]
    for i in (2, 1, 0):
        out = jnp.where(g == i, levels[i], out)
    return out


def _pool_mixed(e, g, cnt, n):
    w2 = e + pltpu.roll(e, 1, axis=0)
    w4 = pltpu.roll(w2, 1, axis=0) + pltpu.roll(w2, n - 1, axis=0)
    w8 = pltpu.roll(w4, 2, axis=0) + pltpu.roll(w4, n - 2, axis=0)
    w16 = pltpu.roll(w8, 4, axis=0) + pltpu.roll(w8, n - 4, axis=0)
    return _by_group(g, (w2, w4, w8, w16)) / cnt - e


def _pool_fwd(hp, xres, pw, scale, name):
    seq, d = hp.shape
    tc = POOL_GROUP_W
    tr = _pick(seq, (512, 256))
    nr = seq // tr
    n = tr + 2 * HALO

    def body(h_ref, hp_ref, hn_ref, x_ref, w_ref, s_ref, o_ref):
        g, r = pl.program_id(0), pl.program_id(1)
        e = _extend(h_ref, hp_ref, hn_ref, r, nr)
        mixed = _pool_mixed(e, g, _pool_count(g, r, tr, n, seq), n)[HALO:HALO + tr]
        y = jnp.dot(mixed.astype(BF16), w_ref[...], preferred_element_type=F32)
        o_ref[...] = x_ref[...] + y * s_ref[...]

    return pl.pallas_call(
        body, out_shape=jax.ShapeDtypeStruct((seq, d), F32), grid=(POOL_GROUPS, nr),
        in_specs=_halo_specs(tr, tc, seq, lambda j: j) + [
            pl.BlockSpec((tr, tc), lambda j, r: (r, j)), pl.BlockSpec((None, tc, tc), lambda j, r: (j, 0, 0)),
            pl.BlockSpec((1, tc), lambda j, r: (0, j))],
        out_specs=pl.BlockSpec((tr, tc), lambda j, r: (r, j)),
        compiler_params=_cp(("parallel", "parallel")), name=name)(hp, hp, hp, xres, pw, scale)


def _pool_bwd(hp, dy, pw, scale, name):
    seq, d = hp.shape
    tc = POOL_GROUP_W
    tr = _pick(seq, (512, 256))
    nr = seq // tr
    n = tr + 2 * HALO

    def body(h_ref, hp_ref, hn_ref, d_ref, dp_ref, dn_ref, w_ref, s_ref, dh_ref, dw_ref, ds_ref):
        g, r = pl.program_id(0), pl.program_id(1)
        cnt = _pool_count(g, r, tr, n, seq)
        e = _extend(h_ref, hp_ref, hn_ref, r, nr)
        mixed = _pool_mixed(e, g, cnt, n)[HALO:HALO + tr].astype(BF16)
        dye = _extend(d_ref, dp_ref, dn_ref, r, nr)
        dyp = (dye * s_ref[...]).astype(BF16)
        dmixed = lax.dot_general(dyp, w_ref[...], (((1,), (1,)), ((), ())), preferred_element_type=F32)
        dwin = dmixed / cnt
        m2 = dwin + pltpu.roll(dwin, n - 1, axis=0)
        m4 = pltpu.roll(m2, 1, axis=0) + pltpu.roll(m2, n - 1, axis=0)
        m8 = pltpu.roll(m4, 2, axis=0) + pltpu.roll(m4, n - 2, axis=0)
        m16 = pltpu.roll(m8, 4, axis=0) + pltpu.roll(m8, n - 4, axis=0)
        dh_ref[...] = (_by_group(g, (m2, m4, m8, m16)) - dmixed)[HALO:HALO + tr]
        ypre = jnp.dot(mixed, w_ref[...], preferred_element_type=F32)
        dsp = jnp.sum(d_ref[...] * ypre, axis=0, keepdims=True)
        dwp = lax.dot_general(mixed, dyp[HALO:HALO + tr], (((0,), (0,)), ((), ())), preferred_element_type=F32)

        @pl.when(r == 0)
        def _():
            dw_ref[...] = dwp
            ds_ref[...] = dsp

        @pl.when(r > 0)
        def _():
            dw_ref[...] += dwp
            ds_ref[...] += dsp

    return pl.pallas_call(
        body,
        out_shape=(jax.ShapeDtypeStruct((seq, d), F32), jax.ShapeDtypeStruct((POOL_GROUPS, tc, tc), F32),
                   jax.ShapeDtypeStruct((1, d), F32)),
        grid=(POOL_GROUPS, nr),
        in_specs=_halo_specs(tr, tc, seq, lambda j: j) * 2 + [
            pl.BlockSpec((None, tc, tc), lambda j, r: (j, 0, 0)), pl.BlockSpec((1, tc), lambda j, r: (0, j))],
        out_specs=(pl.BlockSpec((tr, tc), lambda j, r: (r, j)), pl.BlockSpec((None, tc, tc), lambda j, r: (j, 0, 0)),
                   pl.BlockSpec((1, tc), lambda j, r: (0, j))),
        compiler_params=_cp(("parallel", "arbitrary")), name=name)(hp, hp, hp, dy, dy, dy, pw, scale)


def _adamw_math(w, g, m, v):
    m = ADAM_B1 * m + (1.0 - ADAM_B1) * g
    v = ADAM_B2 * v + (1.0 - ADAM_B2) * (g * g)
    m_hat = m / (1.0 - ADAM_B1 ** ADAM_STEP)
    v_hat = v / (1.0 - ADAM_B2 ** ADAM_STEP)
    delta = -ADAM_LR * (m_hat / (jnp.sqrt(v_hat) + ADAM_EPS) + ADAM_WD * w)
    return delta, m, v


def _adamw(w, ga, gb, m, v, name):
    rows, cols = w.shape
    tr = _pick(rows, (256, 128, 64, 32, 16, 8))
    two = gb is not None

    def body(*refs):
        if two:
            w_ref, ga_ref, gb_ref, m_ref, v_ref, g_out, d_out, m_out, v_out = refs
            g = ga_ref[...] + gb_ref[...]
        else:
            w_ref, ga_ref, m_ref, v_ref, g_out, d_out, m_out, v_out = refs
            g = ga_ref[...]
        delta, m, v = _adamw_math(w_ref[...], g, m_ref[...], v_ref[...])
        g_out[...] = g
        d_out[...] = delta
        m_out[...] = m
        v_out[...] = v

    spec = pl.BlockSpec((tr, cols), lambda i: (i, 0))
    ops = [w, ga] + ([gb] if two else []) + [m, v]
    return pl.pallas_call(
        body, out_shape=tuple(jax.ShapeDtypeStruct((rows, cols), F32) for _ in range(4)), grid=(rows // tr,),
        in_specs=[spec] * len(ops), out_specs=(spec,) * 4, compiler_params=_cp(("parallel",)), name=name)(*ops)


def _sum4(parts, name):
    _, rows, cols = parts.shape
    tr = _pick(rows, (256, 128, 64, 32, 16))

    def body(p_ref, o_ref):
        acc = p_ref[0].astype(F32)
        for kk in range(1, 4):
            acc = acc + p_ref[kk].astype(F32)
        o_ref[...] = acc

    return pl.pallas_call(
        body, out_shape=jax.ShapeDtypeStruct((rows, cols), F32), grid=(rows // tr,),
        in_specs=[pl.BlockSpec((4, tr, cols), lambda i: (0, i, 0))], out_specs=pl.BlockSpec((tr, cols), lambda i: (i, 0)),
        compiler_params=_cp(("parallel",)), name=name)(parts)


def _place():
    x, y, c = lax.axis_index("x"), lax.axis_index("y"), lax.axis_index("c")
    chips = [(1 - x, y), (x, 1 - y), (1 - x, 1 - y)]
    return x, y, c, chips


def _window(ref, axis, j, size, c=None, half=None, lead=()):
    if axis == "r":
        if c is None:
            return ref.at[lead + (slice(None), pl.ds(pl.multiple_of(j * size, 32), size), slice(None))]
        return ref.at[lead + (slice(None), pl.ds(pl.multiple_of(j * size + c * half, 32), half), slice(None))]
    cols = pl.ds(pl.multiple_of(j * size, LANES), size)
    if c is None:
        return ref.at[lead + (slice(None), slice(None), cols)]
    return ref.at[lead + (slice(None), pl.ds(pl.multiple_of(c * half, 32), half), cols)]


def _gather_weights(shards, axes, name):
    nt = len(shards)
    fulls, sizes, halves = [], [], []
    for s, ax in zip(shards, axes):
        l, rs, cs = s.shape
        fulls.append(jax.ShapeDtypeStruct((l, 4 * rs, cs) if ax == "r" else (l, rs, 4 * cs), s.dtype))
        sizes.append(rs if ax == "r" else cs)
        halves.append(rs // 2)

    def body(*refs):
        src, dst = refs[:nt], refs[nt:2 * nt]
        send_sems, recv_sems, local_sems = refs[2 * nt:]
        x, y, c, chips = _place()
        me = 2 * x + y
        sib = (x, y, 1 - c)

        def win(t, j, core=None):
            return _window(dst[t], axes[t], j, sizes[t], core, halves[t])

        def my_half(t):
            return src[t].at[:, pl.ds(pl.multiple_of(c * halves[t], 16), halves[t]), :]

        def ici(t, kk, origin):
            px, py = chips[kk]
            return pltpu.make_async_remote_copy(
                src_ref=my_half(t), dst_ref=win(t, origin, c), send_sem=send_sems.at[t * 3 + kk],
                recv_sem=recv_sems.at[t * 3 + kk], device_id=(px, py, c), device_id_type=MESH)

        def d2d(t, kk, origin, core):
            return pltpu.make_async_remote_copy(
                src_ref=win(t, origin, core), dst_ref=win(t, origin, core), send_sem=send_sems.at[3 * nt + t * 3 + kk],
                recv_sem=recv_sems.at[3 * nt + t * 3 + kk], device_id=sib, device_id_type=MESH)

        local = [pltpu.make_async_copy(src[t], win(t, me), local_sems.at[t]) for t in range(nt)]
        for cp in local:
            cp.start()
        sends = [ici(t, kk, me) for t in range(nt) for kk in range(3)]
        for cp in sends:
            cp.start()
        passed = []
        for t in range(nt):
            for kk in range(3):
                px, py = chips[kk]
                origin = 2 * px + py
                ici(t, kk, origin).wait_recv()
                cp = d2d(t, kk, origin, c)
                cp.start()
                passed.append(cp)
        for t in range(nt):
            for kk in range(3):
                px, py = chips[kk]
                d2d(t, kk, 2 * px + py, 1 - c).wait_recv()
        for cp in sends + passed:
            cp.wait_send()
        for cp in local:
            cp.wait()

    return pl.pallas_call(
        body, out_shape=tuple(fulls), in_specs=[ANY] * nt, out_specs=tuple([ANY] * nt),
        scratch_shapes=[pltpu.SemaphoreType.DMA((6 * nt,)), pltpu.SemaphoreType.DMA((6 * nt,)),
                        pltpu.SemaphoreType.DMA((nt,))],
        name=name)(*shards)


def _scatter_grads(grads, axes, name):
    nt = len(grads)
    outs, sizes = [], []
    for gr, ax in zip(grads, axes):
        l, r, cc = gr.shape
        outs.append(jax.ShapeDtypeStruct((4, l, r // 4, cc) if ax == "r" else (4, l, r, cc // 4), gr.dtype))
        sizes.append(r // 4 if ax == "r" else cc // 4)

    def body(*refs):
        src, dst = refs[:nt], refs[nt:2 * nt]
        send_sems, recv_sems, local_sems = refs[2 * nt:]
        x, y, c, chips = _place()
        me = 2 * x + y

        def copy(t, kk):
            px, py = chips[kk]
            return pltpu.make_async_remote_copy(
                src_ref=_window(src[t], axes[t], 2 * px + py, sizes[t]), dst_ref=dst[t].at[me],
                send_sem=send_sems.at[t * 3 + kk], recv_sem=recv_sems.at[t * 3 + kk],
                device_id=(px, py, c), device_id_type=MESH)

        local = [pltpu.make_async_copy(_window(src[t], axes[t], me, sizes[t]), dst[t].at[me], local_sems.at[t])
                 for t in range(nt)]
        for cp in local:
            cp.start()
        sends = [copy(t, kk) for t in range(nt) for kk in range(3)]
        for cp in sends:
            cp.start()
        for t in range(nt):
            for kk in range(3):
                px, py = chips[kk]
                pltpu.make_async_remote_copy(
                    src_ref=dst[t].at[me], dst_ref=dst[t].at[2 * px + py], send_sem=send_sems.at[t * 3 + kk],
                    recv_sem=recv_sems.at[t * 3 + kk], device_id=(px, py, c), device_id_type=MESH).wait_recv()
        for cp in sends:
            cp.wait_send()
        for cp in local:
            cp.wait()

    return pl.pallas_call(
        body, out_shape=tuple(outs), in_specs=[ANY] * nt, out_specs=tuple([ANY] * nt),
        scratch_shapes=[pltpu.SemaphoreType.DMA((3 * nt,)), pltpu.SemaphoreType.DMA((3 * nt,)),
                        pltpu.SemaphoreType.DMA((nt,))],
        name=name)(*grads)


def _swap_sibling(arrs, name):
    nt = len(arrs)

    def body(*refs):
        src, dst = refs[:nt], refs[nt:2 * nt]
        send_sems, recv_sems = refs[2 * nt:]
        x, y, c, _ = _place()
        cps = [pltpu.make_async_remote_copy(src_ref=src[t], dst_ref=dst[t], send_sem=send_sems.at[t],
                                            recv_sem=recv_sems.at[t], device_id=(x, y, 1 - c), device_id_type=MESH)
               for t in range(nt)]
        for cp in cps:
            cp.start()
        for cp in cps:
            cp.wait()

    return pl.pallas_call(
        body, out_shape=tuple(jax.ShapeDtypeStruct(a.shape, a.dtype) for a in arrs), in_specs=[ANY] * nt,
        out_specs=tuple([ANY] * nt),
        scratch_shapes=[pltpu.SemaphoreType.DMA((nt,)), pltpu.SemaphoreType.DMA((nt,))], name=name)(*arrs)


def _gather8(pack, with_sum, name):
    rows = pack.shape[0]
    flips = [f for f in itertools.product((0, 1), repeat=3) if any(f)]

    def body(p_ref, all_ref, *rest):
        if with_sum:
            sum_ref, send_sems, recv_sems = rest
        else:
            send_sems, recv_sems = rest
        x, y, c, _ = _place()
        me = 4 * x + 2 * y + c

        def peer(f):
            return tuple(1 - v if fl else v for v, fl in zip((x, y, c), f))

        all_ref[me] = p_ref[...]
        cps = []
        for kk, f in enumerate(flips):
            cp = pltpu.make_async_remote_copy(src_ref=p_ref, dst_ref=all_ref.at[me], send_sem=send_sems.at[kk],
                                              recv_sem=recv_sems.at[kk], device_id=peer(f), device_id_type=MESH)
            cp.start()
            cps.append(cp)
        for kk, f in enumerate(flips):
            px, py, pc = peer(f)
            pltpu.make_async_remote_copy(src_ref=p_ref, dst_ref=all_ref.at[4 * px + 2 * py + pc],
                                         send_sem=send_sems.at[kk], recv_sem=recv_sems.at[kk], device_id=peer(f),
                                         device_id_type=MESH).wait_recv()
        for cp in cps:
            cp.wait_send()
        if with_sum:
            acc = all_ref[0]
            for d in range(1, 8):
                acc = acc + all_ref[d]
            sum_ref[...] = acc

    vm = pl.BlockSpec(memory_space=pltpu.VMEM)
    out_shape = [jax.ShapeDtypeStruct((8, rows, LANES), F32)] + ([jax.ShapeDtypeStruct((rows, LANES), F32)] if with_sum else [])
    return pl.pallas_call(
        body, out_shape=tuple(out_shape), in_specs=[vm], out_specs=tuple([vm] * len(out_shape)),
        scratch_shapes=[pltpu.SemaphoreType.DMA((7,)), pltpu.SemaphoreType.DMA((7,))], name=name)(pack)


def _pack(arrs):
    flat = jnp.concatenate([a.reshape(-1).astype(F32) for a in arrs])
    rows = -(-flat.shape[0] // (8 * LANES)) * 8
    return jnp.pad(flat, (0, rows * LANES - flat.shape[0])).reshape(rows, LANES)


def _unpack(flat, shapes):
    out, pos = [], 0
    for shp in shapes:
        size = 1
        for s in shp:
            size *= s
        out.append(flat[pos:pos + size].reshape(shp))
        pos += size
    return out


BIG = ("attn_w_qkv", "attn_w_o", "pool_w", "xattn_w_q", "xattn_w_kv", "xattn_w_o", "ffn_w_up", "ffn_w_down")
BIG_AXIS = ("c", "r", "r", "r", "c", "r", "c", "r")
SMALL_REPL = ("attn_norm", "attn_q_gain", "attn_k_gain", "xattn_norm", "mem_norm", "ffn_norm", "ffn_conv_b", "final_norm")
SMALL_SHARD = ("pool_norm", "pool_scale", "ffn_conv_w")
ORDER = ("attn_norm", "attn_w_qkv", "attn_q_gain", "attn_k_gain", "attn_w_o", "pool_norm", "pool_w", "pool_scale",
         "xattn_norm", "mem_norm", "xattn_w_q", "xattn_w_kv", "xattn_w_o", "ffn_norm", "ffn_w_up", "ffn_conv_w",
         "ffn_conv_b", "ffn_w_down", "final_norm")


def _step(x, mem, tgt, w, m, v):
    seq, d = x.shape
    xi, yi, ci = lax.axis_index("x"), lax.axis_index("y"), lax.axis_index("c")
    chip = 2 * xi + yi
    dff = w["ffn_w_down"].shape[1] * 4
    n_layers = w["ffn_norm"].shape[0]

    def as3d(a):
        return a.reshape(a.shape[-3:])
    shards = [as3d(w[nm]).astype(BF16) for nm in BIG]
    wq, wo, wp, wxq, wxkv, wxo, wup, wdn = _gather_weights(shards, BIG_AXIS, "gather_weights")
    small_in = [w[nm] for nm in SMALL_SHARD]
    (small_all,) = _gather8(_pack(small_in), False, "gather_small")
    per_chip = [_unpack(small_all[2 * j].reshape(-1), [a.shape for a in small_in]) for j in range(4)]
    pool_norm, pool_scale, conv_w = (jnp.concatenate([per_chip[j][i] for j in range(4)], axis=-1) for i in range(3))

    conv_b = w["ffn_conv_b"].reshape(n_layers, 1, -1)
    tabs = _rope_tables(seq)
    qg2 = jnp.tile(w["attn_q_gain"], (1, 2))
    kg2 = jnp.tile(w["attn_k_gain"], (1, 2))
    mm = functools.partial(_mm)

    saved = {}
    x0 = x
    h0 = _rms_fwd(x0, w["attn_norm"], BF16, "rms_attn")
    qkv = mm(h0, wq, "nn", b_l=0, out_dtype=F32, name="mm_qkv")
    q_r, k_r, k_t, v_b, v_t = _qk_prep(qkv, qg2, kg2, tabs, "qk_prep")
    o_at, lse = _flash_fwd(q_r, k_r, v_t, "flash_fwd")
    xs = [x0, mm(o_at, wo, "nn", b_l=0, res=x0, out_dtype=F32, name="mm_attn_o")]

    def xattn_fwd(l, xin):
        hq = _rms_fwd(xin, w["xattn_norm"][l:l + 1], BF16, f"rms_xq{l}")
        mn = _rms_fwd(mem, w["mem_norm"][l:l + 1], BF16, f"rms_mem{l}")
        xq = mm(hq, wxq, "nn", b_l=l, scale=X_HEAD_DIM ** -0.5, out_dtype=BF16, name=f"mm_xq{l}")
        kv = mm(mn, wxkv, "nn", b_l=l, out_dtype=BF16, name=f"mm_xkv{l}")
        xo = _xattn_fwd(xq, kv, f"xattn_fwd{l}")
        saved[f"x{l}"] = (hq, mn, xq, kv, xo)
        return mm(xo, wxo, "nn", b_l=l, res=xin, out_dtype=F32, name=f"mm_xo{l}")

    def ffn_fwd(l, xin):
        hf = _rms_fwd(xin, w["ffn_norm"][l:l + 1], BF16, f"rms_ffn{l}")
        ug = mm(hf, wup, "nn", b_l=l, n=dff, out_dtype=F32, name=f"mm_up_g{l}")
        uv = mm(hf, wup, "nn", b_l=l, n=dff, b_off=(0, dff), out_dtype=F32, name=f"mm_up_v{l}")
        act = _conv_gate_fwd(ug, uv, conv_w, conv_b, l, f"conv_gate{l}")
        saved[f"f{l}"] = (hf, ug, uv, act)
        return mm(act, wdn, "nn", b_l=l, res=xin, out_dtype=F32, name=f"mm_down{l}")

    xs.append(xattn_fwd(0, xs[-1]))
    xs.append(ffn_fwd(0, xs[-1]))
    hp = _rms_fwd(xs[-1], pool_norm, F32, "rms_pool")
    xs.append(_pool_fwd(hp, xs[-1], wp, pool_scale, "pool_fwd"))
    xs.append(xattn_fwd(1, xs[-1]))
    xs.append(ffn_fwd(1, xs[-1]))
    dres, g_final, loss = _final_loss(xs[6], w["final_norm"].reshape(1, d), tgt, "final_loss")

    grads = {}
    gbuf = {}

    def dw(nm, a, b, layer, full, off=(0, 0), n=None, tn=None):
        gbuf[nm] = _mm(a, b, "tn", out_dtype=BF16, out_full=full, out_l=layer, out_off=off, n=n, tn=tn,
                       alias=gbuf.get(nm), name=f"dw_{nm}{layer}_{off[1]}")

    def ffn_bwd(l, xin, dres):
        hf, ug, uv, act = saved[f"f{l}"]
        dw("ffn_w_down", act, dres, l, wdn.shape)
        dact = _mm(dres, wdn, "nt", b_l=l, out_dtype=F32, name=f"mm_dact{l}")
        dug, duv, dwg, dwv = _conv_gate_bwd(ug, uv, dact, conv_w, conv_b, l, f"conv_gate_bwd{l}")
        dw("ffn_w_up", hf, dug, l, wup.shape, tn=1408)
        dw("ffn_w_up", hf, duv, l, wup.shape, off=(0, dff), tn=1408)
        dhf = _mm(dug, wup, "nt", b_l=l, n=d, out_dtype=F32, name=f"mm_dhf_g{l}")
        dhf = _mm(duv, wup, "nt", b_l=l, n=d, b_off=(0, dff), res=dhf, out_dtype=F32, name=f"mm_dhf_v{l}")
        dres, dg = _rms_bwd(xin, w["ffn_norm"][l:l + 1], dhf, dres, f"rms_ffn_bwd{l}")
        return dres, dg, jnp.concatenate([dwg[:3], dwv[:3]], axis=1), jnp.concatenate([dwg[3], dwv[3]], axis=0)

    def xattn_bwd(l, xin, dres):
        hq, mn, xq, kv, xo = saved[f"x{l}"]
        dw("xattn_w_o", xo, dres, l, wxo.shape)
        dxo = _mm(dres, wxo, "nt", b_l=l, out_dtype=F32, name=f"mm_dxo{l}")
        dq, dkv = _xattn_bwd(xq, kv, dxo, f"xattn_bwd{l}")
        dw("xattn_w_q", hq, dq, l, wxq.shape)
        dhq = _mm(dq, wxq, "nt", b_l=l, out_dtype=F32, name=f"mm_dhq{l}")
        dw("xattn_w_kv", mn, dkv, l, wxkv.shape)
        dmn = _mm(dkv, wxkv, "nt", b_l=l, out_dtype=F32, name=f"mm_dmn{l}")
        _, dg_mem = _rms_bwd(mem, w["mem_norm"][l:l + 1], dmn, None, f"rms_mem_bwd{l}")
        dres, dg = _rms_bwd(xin, w["xattn_norm"][l:l + 1], dhq, dres, f"rms_xq_bwd{l}")
        return dres, dg, dg_mem

    g_ffn, g_xn, g_mn, g_cw, g_cb = [None] * n_layers, [None] * n_layers, [None] * n_layers, [None] * n_layers, [None] * n_layers
    dres, g_ffn[1], g_cw[1], g_cb[1] = ffn_bwd(1, xs[5], dres)
    dres, g_xn[1], g_mn[1] = xattn_bwd(1, xs[4], dres)
    dhp, g_pw, g_pscale = _pool_bwd(hp, dres, wp, pool_scale, "pool_bwd")
    dres, g_pnorm = _rms_bwd(xs[3], pool_norm, dhp, dres, "rms_pool_bwd")
    dres, g_ffn[0], g_cw[0], g_cb[0] = ffn_bwd(0, xs[2], dres)
    dres, g_xn[0], g_mn[0] = xattn_bwd(0, xs[1], dres)
    dw("attn_w_o", o_at, dres, 0, wo.shape)
    do = _mm(dres, wo, "nt", b_l=0, out_dtype=F32, name="mm_do")
    dq_r, dk_r, dv = _flash_bwd(q_r, k_r, k_t, v_b, do, o_at, lse, "flash_bwd")
    dqkv, dqg, dkg = _qk_prep_bwd(qkv, dq_r, dk_r, dv, qg2, kg2, tabs, "qk_prep_bwd")
    dw("attn_w_qkv", h0, dqkv, 0, wq.shape)
    dh0 = _mm(dqkv, wq, "nt", b_l=0, out_dtype=F32, name="mm_dh0")
    grad_x, g_an = _rms_bwd(x0, w["attn_norm"], dh0, dres, "rms_attn_bwd")
    gbuf["pool_w"] = g_pw.astype(BF16)

    small_g = {
        "attn_norm": g_an, "attn_q_gain": dqg[:, :HEAD_DIM] + dqg[:, HEAD_DIM:], "attn_k_gain": dkg[:, :HEAD_DIM] + dkg[:, HEAD_DIM:],
        "xattn_norm": jnp.concatenate(g_xn, axis=0), "mem_norm": jnp.concatenate(g_mn, axis=0),
        "ffn_norm": jnp.concatenate(g_ffn, axis=0), "ffn_conv_b": jnp.stack(g_cb, axis=0), "final_norm": g_final.reshape(d),
        "pool_norm": g_pnorm, "pool_scale": g_pscale, "ffn_conv_w": jnp.stack(g_cw, axis=0)}
    names = SMALL_REPL + SMALL_SHARD
    _, total = _gather8(_pack([loss[0, :1]] + [small_g[nm] for nm in names]), True, "reduce_small")
    parts = _unpack(total.reshape(-1), [(1,)] + [small_g[nm].shape for nm in names])
    loss_out = parts[0][0]
    for nm, g in zip(names, parts[1:]):
        if nm in SMALL_SHARD:
            size = w[nm].shape[-1]
            g = lax.dynamic_slice_in_dim(g, chip * size, size, axis=g.ndim - 1)
        grads[nm] = g.reshape(w[nm].shape)

    packed = [_pack([src[nm] for nm in names]) for src in (w, grads, m, v)]
    _, sd, sm, sv = _adamw(packed[0], packed[1], None, packed[2], packed[3], "adamw_small")
    shapes = [w[nm].shape for nm in names]
    delta = dict(zip(names, _unpack(sd.reshape(-1), shapes)))
    new_m = dict(zip(names, _unpack(sm.reshape(-1), shapes)))
    new_v = dict(zip(names, _unpack(sv.reshape(-1), shapes)))

    recv = _scatter_grads([gbuf[nm] for nm in BIG], BIG_AXIS, "scatter_grads")
    sums = []
    for nm, rc in zip(BIG, recv):
        sums.append(_sum4(rc.reshape(4, -1, rc.shape[-1]), f"sum4_{nm}"))
    others = _swap_sibling(sums, "swap_sums")
    for nm, mine, other in zip(BIG, sums, others):
        cols = mine.shape[-1]
        outs = _adamw(w[nm].reshape(-1, cols), mine, other, m[nm].reshape(-1, cols), v[nm].reshape(-1, cols), f"adamw_{nm}")
        grads[nm], delta[nm], new_m[nm], new_v[nm] = (o.reshape(w[nm].shape) for o in outs)

    return loss_out, grad_x, grads, delta, new_m, new_v


def kernel(x, mem, attn_norm, attn_w_qkv, attn_q_gain, attn_k_gain, attn_w_o, pool_norm, pool_w, pool_scale, xattn_norm, mem_norm, xattn_w_q, xattn_w_kv, xattn_w_o, ffn_norm, ffn_w_up, ffn_conv_w, ffn_conv_b, ffn_w_down, final_norm, loss_target, m_attn_norm, m_attn_w_qkv, m_attn_q_gain, m_attn_k_gain, m_attn_w_o, m_pool_norm, m_pool_w, m_pool_scale, m_xattn_norm, m_mem_norm, m_xattn_w_q, m_xattn_w_kv, m_xattn_w_o, m_ffn_norm, m_ffn_w_up, m_ffn_conv_w, m_ffn_conv_b, m_ffn_w_down, m_final_norm, v_attn_norm, v_attn_w_qkv, v_attn_q_gain, v_attn_k_gain, v_attn_w_o, v_pool_norm, v_pool_w, v_pool_scale, v_xattn_norm, v_mem_norm, v_xattn_w_q, v_xattn_w_kv, v_xattn_w_o, v_ffn_norm, v_ffn_w_up, v_ffn_conv_w, v_ffn_conv_b, v_ffn_w_down, v_final_norm):
    given = dict(locals())
    w = {nm: given[nm] for nm in ORDER}
    m = {nm: given["m_" + nm] for nm in ORDER}
    v = {nm: given["v_" + nm] for nm in ORDER}
    seq, d = x.shape[1], x.shape[2]
    loss, grad_x, grads, delta, new_m, new_v = _step(
        x.reshape(seq, d), mem.reshape(mem.shape[1], d), loss_target.reshape(seq, d), w, m, v)
    return (loss, grad_x.reshape(x.shape), *[grads[nm] for nm in ORDER], *[delta[nm] for nm in ORDER],
            *[new_m[nm] for nm in ORDER], *[new_v[nm] for nm in ORDER])
```

```python
import functools
import itertools

import jax
import jax.numpy as jnp
from jax import lax
from jax.experimental import pallas as pl
from jax.experimental.pallas import tpu as pltpu

F32, BF16 = jnp.float32, jnp.bfloat16
EPS = 1e-6
GRID_W = 64
ROPE_THETA = 10000.0
HEAD_DIM = 64
N_HEADS = 16
N_KV = 4
X_HEADS = 4
X_HEAD_DIM = 256
POOL_GROUPS = 4
POOL_GROUP_W = 256
HALO = 8
LANES = 128
ADAM_LR, ADAM_B1, ADAM_B2, ADAM_EPS, ADAM_WD, ADAM_STEP = 0.001, 0.9, 0.999, 1e-08, 0.01, 10
VMEM_LIMIT = 48 * 1024 * 1024
MESH = pl.DeviceIdType.MESH
NEG = -1e30
ANY = pl.BlockSpec(memory_space=pl.ANY)


def _cp(sem=None):
    return pltpu.CompilerParams(dimension_semantics=sem, vmem_limit_bytes=VMEM_LIMIT)


def _pick(n, cands):
    for c in cands:
        if c <= n and n % c == 0:
            return c
    return n


def _mm(a, b, mode, *, name, out_dtype, tm=None, tn=None, tk=None, n=None, k=None, b_l=None, b_off=(0, 0),
        res=None, scale=None, out_full=None, out_l=None, out_off=(0, 0), alias=None):
    if mode == "tn":
        K, M = a.shape
    else:
        M, K = a.shape
    bs = b.shape[-2:]
    if mode == "nn":
        K = k or K
        N = n or bs[1]
    elif mode == "nt":
        N = n or bs[0]
    else:
        N = n or bs[1]
    wide = (1408, 1024, 512, 256, 128)
    if mode == "tn":
        tm = tm or (M if M <= 1024 else _pick(M, wide))
        tk = tk or _pick(K, (2048, 1024, 512, 256, 128))
    else:
        tm = _pick(M, (tm or 512, 256, 128))
        tk = tk or (K if K <= 2816 else _pick(K, wide))
    tn = tn or (N if N <= 1536 else _pick(N, wide))
    assert M % tm == 0 and N % tn == 0 and K % tk == 0, (name, M, N, K, tm, tn, tk)
    nk = K // tk
    dims = {"nn": ((1,), (0,)), "nt": ((1,), (1,)), "tn": ((0,), (0,))}[mode]

    if mode == "tn":
        a_spec = pl.BlockSpec((tk, tm), lambda i, j, kk: (kk, i))
    else:
        a_spec = pl.BlockSpec((tm, tk), lambda i, j, kk: (i, kk))
    if mode == "nt":
        bb, (d0, d1) = (tn, tk), (b_off[0] // tn, b_off[1] // tk)
        assert b_off[0] % tn == 0 and b_off[1] % tk == 0
        bidx = lambda i, j, kk: (j + d0, kk + d1)
    else:
        bb, (d0, d1) = (tk, tn), (b_off[0] // tk, b_off[1] // tn)
        assert b_off[0] % tk == 0 and b_off[1] % tn == 0
        bidx = lambda i, j, kk: (kk + d0, j + d1)
    if b.ndim == 3:
        b_spec = pl.BlockSpec((None,) + bb, lambda i, j, kk: (b_l,) + bidx(i, j, kk))
    else:
        b_spec = pl.BlockSpec(bb, bidx)
    in_specs, operands = [a_spec, b_spec], [a, b]
    if res is not None:
        in_specs.append(pl.BlockSpec((tm, tn), lambda i, j, kk: (i, j)))
        operands.append(res)
    aliases = {}
    if alias is not None:
        aliases = {len(operands): 0}
        in_specs.append(ANY)
        operands.append(alias)
    if out_full is None:
        out_shape = jax.ShapeDtypeStruct((M, N), out_dtype)
        out_spec = pl.BlockSpec((tm, tn), lambda i, j, kk: (i, j))
    else:
        assert out_off[0] % tm == 0 and out_off[1] % tn == 0
        o0, o1 = out_off[0] // tm, out_off[1] // tn
        out_shape = jax.ShapeDtypeStruct(out_full, out_dtype)
        out_spec = pl.BlockSpec((None, tm, tn), lambda i, j, kk: (out_l, i + o0, j + o1))
    has_res, has_alias = res is not None, alias is not None

    def body(*refs):
        a_ref, b_ref = refs[0], refs[1]
        pos = 2
        res_ref = None
        if has_res:
            res_ref = refs[pos]
            pos += 1
        if has_alias:
            pos += 1
        o_ref, acc_ref = refs[pos], refs[pos + 1]
        kk = pl.program_id(2)
        part = lax.dot_general(a_ref[...].astype(BF16), b_ref[...].astype(BF16), (dims, ((), ())),
                               preferred_element_type=F32)

        def finish(acc):
            if scale is not None:
                acc = acc * scale
            if res_ref is not None:
                acc = acc + res_ref[...]
            o_ref[...] = acc.astype(o_ref.dtype)

        if nk == 1:
            finish(part)
        else:
            @pl.when(kk == 0)
            def _():
                acc_ref[...] = part

            @pl.when(jnp.logical_and(kk > 0, kk < nk - 1))
            def _():
                acc_ref[...] += part

            @pl.when(kk == nk - 1)
            def _():
                finish(acc_ref[...] + part)

    return pl.pallas_call(
        body, out_shape=out_shape, grid=(M // tm, N // tn, nk), in_specs=in_specs, out_specs=out_spec,
        scratch_shapes=[pltpu.VMEM((tm, tn) if nk > 1 else (8, 128), F32)], input_output_aliases=aliases,
        compiler_params=_cp(("parallel", "parallel", "arbitrary")), name=name)(*operands)


def _rms_fwd(x, gain, out_dtype, name):
    rows, d = x.shape
    tr = _pick(rows, (512, 256))

    def body(x_ref, g_ref, o_ref):
        xv = x_ref[...]
        r = lax.rsqrt(jnp.mean(xv * xv, axis=-1, keepdims=True) + EPS)
        o_ref[...] = (xv * r * g_ref[...]).astype(o_ref.dtype)

    return pl.pallas_call(
        body, out_shape=jax.ShapeDtypeStruct((rows, d), out_dtype), grid=(rows // tr,),
        in_specs=[pl.BlockSpec((tr, d), lambda i: (i, 0)), pl.BlockSpec((1, d), lambda i: (0, 0))],
        out_specs=pl.BlockSpec((tr, d), lambda i: (i, 0)), compiler_params=_cp(("parallel",)), name=name)(x, gain)


def _rms_bwd(x, gain, dh, dres, name):
    rows, d = x.shape
    tr = _pick(rows, (512, 256))
    need_dx = dres is not None

    def body(*refs):
        if need_dx:
            x_ref, g_ref, dh_ref, dres_ref, o_ref, dg_ref = refs
        else:
            x_ref, g_ref, dh_ref, dg_ref = refs
        i = pl.program_id(0)
        xv = x_ref[...]
        dhv = dh_ref[...].astype(F32)
        r = lax.rsqrt(jnp.mean(xv * xv, axis=-1, keepdims=True) + EPS)
        nv = xv * r
        part = jnp.sum(dhv * nv, axis=0, keepdims=True)

        @pl.when(i == 0)
        def _():
            dg_ref[...] = part

        @pl.when(i > 0)
        def _():
            dg_ref[...] += part

        if need_dx:
            dn = dhv * g_ref[...]
            dx = r * (dn - nv * jnp.mean(dn * nv, axis=-1, keepdims=True))
            o_ref[...] = dres_ref[...] + dx

    row_spec = pl.BlockSpec((tr, d), lambda i: (i, 0))
    vec_spec = pl.BlockSpec((1, d), lambda i: (0, 0))
    if need_dx:
        return pl.pallas_call(
            body, out_shape=(jax.ShapeDtypeStruct((rows, d), F32), jax.ShapeDtypeStruct((1, d), F32)),
            grid=(rows // tr,), in_specs=[row_spec, vec_spec, row_spec, row_spec], out_specs=(row_spec, vec_spec),
            compiler_params=_cp(("arbitrary",)), name=name)(x, gain, dh, dres)
    return None, pl.pallas_call(
        body, out_shape=jax.ShapeDtypeStruct((1, d), F32), grid=(rows // tr,),
        in_specs=[row_spec, vec_spec, row_spec], out_specs=vec_spec,
        compiler_params=_cp(("arbitrary",)), name=name)(x, gain, dh)


def _final_loss(x, gain, target, name):
    rows, d = x.shape
    tr = _pick(rows, (512, 256))
    nsteps = rows // tr

    def body(x_ref, g_ref, t_ref, dx_ref, dg_ref, loss_ref, acc_ref):
        i = pl.program_id(0)
        xv = x_ref[...]
        g = g_ref[...]
        r = lax.rsqrt(jnp.mean(xv * xv, axis=-1, keepdims=True) + EPS)
        nv = xv * r
        err = nv * g - t_ref[...]
        dy = err * (1.0 / d)
        dn = dy * g
        dx_ref[...] = r * (dn - nv * jnp.mean(dn * nv, axis=-1, keepdims=True))
        dgp = jnp.sum(dy * nv, axis=0, keepdims=True)
        lp = jnp.sum(err * err, axis=0, keepdims=True)

        @pl.when(i == 0)
        def _():
            dg_ref[...] = dgp
            acc_ref[...] = lp

        @pl.when(i > 0)
        def _():
            dg_ref[...] += dgp
            acc_ref[...] += lp

        @pl.when(i == nsteps - 1)
        def _():
            tot = jnp.sum(acc_ref[...], axis=1, keepdims=True) * (0.5 / d)
            loss_ref[...] = jnp.broadcast_to(tot, loss_ref.shape)

    row_spec = pl.BlockSpec((tr, d), lambda i: (i, 0))
    vec_spec = pl.BlockSpec((1, d), lambda i: (0, 0))
    return pl.pallas_call(
        body, out_shape=(jax.ShapeDtypeStruct((rows, d), F32), jax.ShapeDtypeStruct((1, d), F32),
                         jax.ShapeDtypeStruct((1, LANES), F32)),
        grid=(nsteps,), in_specs=[row_spec, vec_spec, row_spec],
        out_specs=(row_spec, vec_spec, pl.BlockSpec((1, LANES), lambda i: (0, 0))),
        scratch_shapes=[pltpu.VMEM((1, d), F32)], compiler_params=_cp(("arbitrary",)), name=name)(x, gain, target)


def _rope_tables(seq):
    n_rows = seq // GRID_W
    row = jnp.repeat(jnp.arange(n_rows, dtype=F32), GRID_W)
    col = jnp.tile(jnp.arange(GRID_W, dtype=F32), n_rows)
    pairs = HEAD_DIM // 4
    inv_freq = ROPE_THETA ** (-jnp.arange(pairs, dtype=F32) / pairs)
    ang = jnp.stack([row[:, None] * inv_freq, col[:, None] * inv_freq], axis=1)
    cos, sin = jnp.cos(ang), jnp.sin(ang)
    zero = jnp.zeros_like(sin[:, 0])
    c64 = jnp.concatenate([cos[:, 0], cos[:, 0], cos[:, 1], cos[:, 1]], axis=1)
    sp64 = jnp.concatenate([zero, sin[:, 0], zero, sin[:, 1]], axis=1)
    sm64 = jnp.concatenate([-sin[:, 0], zero, -sin[:, 1], zero], axis=1)
    return tuple(jnp.concatenate([t, t], axis=1) for t in (c64, sp64, sm64))


def _pair_norm(xv, lo):
    sq = xv * xv
    s_lo = jnp.sum(jnp.where(lo, sq, 0.0), axis=1, keepdims=True)
    s_hi = jnp.sum(jnp.where(lo, 0.0, sq), axis=1, keepdims=True)
    return lax.rsqrt(jnp.where(lo, s_lo, s_hi) * (1.0 / HEAD_DIM) + EPS)


def _rope(y, c, sp, sm):
    return y * c + pltpu.roll(y, 16, axis=1) * sp + pltpu.roll(y, LANES - 16, axis=1) * sm


def _rope_t(dz, c, sp, sm):
    return dz * c + pltpu.roll(dz * sp, LANES - 16, axis=1) + pltpu.roll(dz * sm, 16, axis=1)


def _qk_prep(qkv, qg2, kg2, tabs, name):
    seq = qkv.shape[0]
    ts = _pick(seq, (256, 128))
    nq, nkp = N_HEADS // 2, N_KV // 2
    qw, kw = N_HEADS * HEAD_DIM, N_KV * HEAD_DIM

    def body(x_ref, qg_ref, kg_ref, c_ref, sp_ref, sm_ref, q_ref, k_ref, kt_ref, v_ref, vt_ref):
        lo = lax.broadcasted_iota(jnp.int32, (ts, LANES), 1) < HEAD_DIM
        c, sp, sm = c_ref[...], sp_ref[...], sm_ref[...]
        for i in range(nq):
            xv = x_ref[:, i * LANES:(i + 1) * LANES]
            y = xv * _pair_norm(xv, lo) * qg_ref[...]
            q_ref[:, i * LANES:(i + 1) * LANES] = (_rope(y, c, sp, sm) * (HEAD_DIM ** -0.5)).astype(BF16)
        for i in range(nkp):
            xv = x_ref[:, qw + i * LANES:qw + (i + 1) * LANES]
            z = _rope(xv * _pair_norm(xv, lo) * kg_ref[...], c, sp, sm)
            k_ref[:, i * LANES:(i + 1) * LANES] = z.astype(BF16)
            kt_ref[i * LANES:(i + 1) * LANES, :] = z.T.astype(BF16)
            vv = x_ref[:, qw + kw + i * LANES:qw + kw + (i + 1) * LANES]
            v_ref[:, i * LANES:(i + 1) * LANES] = vv.astype(BF16)
            vt_ref[i * LANES:(i + 1) * LANES, :] = vv.T.astype(BF16)

    tab = pl.BlockSpec((ts, LANES), lambda i: (i, 0))
    vec = pl.BlockSpec((1, LANES), lambda i: (0, 0))
    return pl.pallas_call(
        body,
        out_shape=(jax.ShapeDtypeStruct((seq, qw), BF16), jax.ShapeDtypeStruct((seq, kw), BF16),
                   jax.ShapeDtypeStruct((kw, seq), BF16), jax.ShapeDtypeStruct((seq, kw), BF16),
                   jax.ShapeDtypeStruct((kw, seq), BF16)),
        grid=(seq // ts,),
        in_specs=[pl.BlockSpec((ts, qw + 2 * kw), lambda i: (i, 0)), vec, vec, tab, tab, tab],
        out_specs=(pl.BlockSpec((ts, qw), lambda i: (i, 0)), pl.BlockSpec((ts, kw), lambda i: (i, 0)),
                   pl.BlockSpec((kw, ts), lambda i: (0, i)), pl.BlockSpec((ts, kw), lambda i: (i, 0)),
                   pl.BlockSpec((kw, ts), lambda i: (0, i))),
        compiler_params=_cp(("parallel",)), name=name)(qkv, qg2, kg2, *tabs)


def _qk_prep_bwd(qkv, dq, dk, dv, qg2, kg2, tabs, name):
    seq = qkv.shape[0]
    ts = _pick(seq, (256, 128))
    nq, nkp = N_HEADS // 2, N_KV // 2
    qw, kw = N_HEADS * HEAD_DIM, N_KV * HEAD_DIM

    def body(x_ref, dq_ref, dk_ref, dv_ref, qg_ref, kg_ref, c_ref, sp_ref, sm_ref, o_ref, dqg_ref, dkg_ref):
        step = pl.program_id(0)
        lo = lax.broadcasted_iota(jnp.int32, (ts, LANES), 1) < HEAD_DIM
        c, sp, sm = c_ref[...], sp_ref[...], sm_ref[...]

        def one(xv, dz, gain):
            r = _pair_norm(xv, lo)
            nv = xv * r
            dy = _rope_t(dz, c, sp, sm)
            dgp = jnp.sum(dy * nv, axis=0, keepdims=True)
            dn = dy * gain
            t = dn * nv
            m_lo = jnp.sum(jnp.where(lo, t, 0.0), axis=1, keepdims=True)
            m_hi = jnp.sum(jnp.where(lo, 0.0, t), axis=1, keepdims=True)
            m = jnp.where(lo, m_lo, m_hi) * (1.0 / HEAD_DIM)
            return r * (dn - nv * m), dgp

        dqg = jnp.zeros((1, LANES), F32)
        for i in range(nq):
            sl = slice(i * LANES, (i + 1) * LANES)
            dx, dgp = one(x_ref[:, sl], dq_ref[:, sl] * (HEAD_DIM ** -0.5), qg_ref[...])
            o_ref[:, sl] = dx.astype(BF16)
            dqg = dqg + dgp
        dkg = jnp.zeros((1, LANES), F32)
        for i in range(nkp):
            sl = slice(i * LANES, (i + 1) * LANES)
            dx, dgp = one(x_ref[:, qw + i * LANES:qw + (i + 1) * LANES], dk_ref[:, sl], kg_ref[...])
            o_ref[:, qw + i * LANES:qw + (i + 1) * LANES] = dx.astype(BF16)
            dkg = dkg + dgp
            o_ref[:, qw + kw + i * LANES:qw + kw + (i + 1) * LANES] = dv_ref[:, sl].astype(BF16)

        @pl.when(step == 0)
        def _():
            dqg_ref[...] = dqg
            dkg_ref[...] = dkg

        @pl.when(step > 0)
        def _():
            dqg_ref[...] += dqg
            dkg_ref[...] += dkg

    tab = pl.BlockSpec((ts, LANES), lambda i: (i, 0))
    vec = pl.BlockSpec((1, LANES), lambda i: (0, 0))
    return pl.pallas_call(
        body,
        out_shape=(jax.ShapeDtypeStruct((seq, qw + 2 * kw), BF16), jax.ShapeDtypeStruct((1, LANES), F32),
                   jax.ShapeDtypeStruct((1, LANES), F32)),
        grid=(seq // ts,),
        in_specs=[pl.BlockSpec((ts, qw + 2 * kw), lambda i: (i, 0)), pl.BlockSpec((ts, qw), lambda i: (i, 0)),
                  pl.BlockSpec((ts, kw), lambda i: (i, 0)), pl.BlockSpec((ts, kw), lambda i: (i, 0)),
                  vec, vec, tab, tab, tab],
        out_specs=(pl.BlockSpec((ts, qw + 2 * kw), lambda i: (i, 0)), vec, vec),
        compiler_params=_cp(("arbitrary",)), name=name)(qkv, dq, dk, dv, qg2, kg2, *tabs)


def _slot(blk, off0, tq):
    half = lax.broadcasted_iota(jnp.int32, (tq, LANES), 1) // HEAD_DIM
    keep = half == jnp.where(off0, 0, 1)
    parts = []
    for p in range(2):
        pair = blk[:, p * LANES:(p + 1) * LANES].astype(F32)
        rolled = pltpu.roll(pair, HEAD_DIM, axis=1)
        parts.append(jnp.where(keep, jnp.where(off0, pair, rolled), 0.0))
        parts.append(jnp.where(keep, jnp.where(off0, rolled, pair), 0.0))
    return jnp.concatenate(parts, axis=0)


def _unslot(x4, off0, tq):
    lo = lax.broadcasted_iota(jnp.int32, (tq, LANES), 1) < HEAD_DIM
    pairs = []
    for p in range(2):
        h0 = x4[(2 * p) * tq:(2 * p + 1) * tq]
        h1 = x4[(2 * p + 1) * tq:(2 * p + 2) * tq]
        a = jnp.where(off0, h0, pltpu.roll(h0, HEAD_DIM, axis=1))
        b = jnp.where(off0, pltpu.roll(h1, HEAD_DIM, axis=1), h1)
        pairs.append(jnp.where(lo, a, b))
    return jnp.concatenate(pairs, axis=1)


def _flash_fwd(q, k, vt, plan, shards, name):
    seq = q.shape[0]
    tq = _pick(seq, (256, 128))
    tk = _pick(seq, (512, 256, 128))
    nq, nkv = seq // tq, seq // tk
    gw = 4 * HEAD_DIM
    nt = plan.nt

    def body(q_ref, k_ref, vt_ref, *rest):
        o_ref, lse_ref = rest[nt:nt + 2]
        q4_ref, m_ref, l_ref, acc_ref = rest[2 * nt + 2:2 * nt + 6]
        plan.bind(rest[:nt], rest[nt + 2:2 * nt + 2], *rest[2 * nt + 6:])
        g, qi, ki = pl.program_id(0), pl.program_id(1), pl.program_id(2)
        off0 = (g % 2) == 0
        inner0 = jnp.logical_and(qi == 0, ki == 0)

        @pl.when(jnp.logical_and(g == 0, inner0))
        def _():
            plan.start()

        @pl.when(jnp.logical_and(g == N_KV - 1, inner0))
        def _():
            plan.forward()

        @pl.when(ki == 0)
        def _():
            q4_ref[...] = _slot(q_ref[...], off0, tq).astype(BF16)
            m_ref[...] = jnp.full(m_ref.shape, NEG, F32)
            l_ref[...] = jnp.zeros(l_ref.shape, F32)
            acc_ref[...] = jnp.zeros(acc_ref.shape, F32)

        st = lax.dot_general(k_ref[...], q4_ref[...], (((1,), (1,)), ((), ())), preferred_element_type=F32)
        m_old = m_ref[...]
        m_new = jnp.maximum(m_old, jnp.max(st, axis=0, keepdims=True))
        alpha = jnp.exp(m_old - m_new)
        pt = jnp.exp(st - m_new)
        l_ref[...] = alpha * l_ref[...] + jnp.sum(pt, axis=0, keepdims=True)
        acc_ref[...] = alpha * acc_ref[...] + jnp.dot(vt_ref[...], pt.astype(BF16), preferred_element_type=F32)
        m_ref[...] = m_new

        @pl.when(ki == nkv - 1)
        def _():
            l = l_ref[...]
            o4 = (acc_ref[...] / l).T
            o_ref[...] = _unslot(o4, off0, tq).astype(o_ref.dtype)
            lse_ref[...] = jnp.broadcast_to(m_ref[...] + jnp.log(l), lse_ref.shape)

        @pl.when(jnp.logical_and(g == N_KV - 1, jnp.logical_and(qi == nq - 1, ki == nkv - 1)))
        def _():
            plan.finish()

    return pl.pallas_call(
        body,
        out_shape=(jax.ShapeDtypeStruct((seq, N_HEADS * HEAD_DIM), BF16),
                   jax.ShapeDtypeStruct((N_KV * nq * 8, 4 * tq), F32), *plan.out_shape),
        grid=(N_KV, nq, nkv),
        in_specs=[pl.BlockSpec((tq, gw), lambda g, qi, ki: (qi, g)),
                  pl.BlockSpec((tk, LANES), lambda g, qi, ki: (ki, g // 2)),
                  pl.BlockSpec((LANES, tk), lambda g, qi, ki: (g // 2, ki))] + [ANY] * nt,
        out_specs=(pl.BlockSpec((tq, gw), lambda g, qi, ki: (qi, g)),
                   pl.BlockSpec((8, 4 * tq), lambda g, qi, ki: (g * nq + qi, 0)), *([ANY] * nt)),
        scratch_shapes=[pltpu.VMEM((4 * tq, LANES), BF16), pltpu.VMEM((1, 4 * tq), F32),
                        pltpu.VMEM((1, 4 * tq), F32), pltpu.VMEM((LANES, 4 * tq), F32)] + plan.scratch,
        compiler_params=_cp(("arbitrary", "arbitrary", "arbitrary")), name=name)(q, k, vt, *shards)


def _flash_bwd(q, k, kt, v, do, o, lse, plan, grads, name):
    seq = q.shape[0]
    tq = _pick(seq, (256, 128))
    tk = _pick(seq, (512, 256, 128))
    nq, nkv = seq // tq, seq // tk
    gw = 4 * HEAD_DIM
    nt = plan.nt

    def body(q_ref, k_ref, kt_ref, v_ref, do_ref, o_ref, lse_ref, *rest):
        dq_ref, dk_ref, dv_ref = rest[nt:nt + 3]
        q4_ref, do4_ref, delta_ref, dqt_ref = rest[2 * nt + 3:2 * nt + 7]
        plan.bind(rest[:nt], rest[nt + 3:2 * nt + 3], *rest[2 * nt + 7:])
        g, qi, ki = pl.program_id(0), pl.program_id(1), pl.program_id(2)
        off0 = (g % 2) == 0

        @pl.when(jnp.logical_and(g == 0, jnp.logical_and(qi == 0, ki == 0)))
        def _():
            plan.start()

        @pl.when(jnp.logical_and(g % 2 == 0, jnp.logical_and(qi == 0, ki == 0)))
        def _():
            dk_ref[...] = jnp.zeros(dk_ref.shape, F32)
            dv_ref[...] = jnp.zeros(dv_ref.shape, F32)

        @pl.when(ki == 0)
        def _():
            q4_ref[...] = _slot(q_ref[...], off0, tq).astype(BF16)
            do4 = _slot(do_ref[...], off0, tq)
            do4_ref[...] = do4.astype(BF16)
            o4 = _slot(o_ref[...], off0, tq)
            delta_ref[...] = jnp.sum((do4 * o4).T, axis=0, keepdims=True)
            dqt_ref[...] = jnp.zeros(dqt_ref.shape, F32)

        q4, do4 = q4_ref[...], do4_ref[...]
        st = lax.dot_general(k_ref[...], q4, (((1,), (1,)), ((), ())), preferred_element_type=F32)
        pt = jnp.exp(st - lse_ref[0:1, :])
        dpt = lax.dot_general(v_ref[...], do4, (((1,), (1,)), ((), ())), preferred_element_type=F32)
        dst = (pt * (dpt - delta_ref[...])).astype(BF16)
        rows = pl.ds(pl.multiple_of(ki * tk, tk), tk)
        dv_ref[rows, :] += jnp.dot(pt.astype(BF16), do4, preferred_element_type=F32)
        dk_ref[rows, :] += jnp.dot(dst, q4, preferred_element_type=F32)
        dqt_ref[...] += jnp.dot(kt_ref[...], dst, preferred_element_type=F32)

        @pl.when(ki == nkv - 1)
        def _():
            dq_ref[...] = _unslot(dqt_ref[...].T, off0, tq)

        @pl.when(jnp.logical_and(g == N_KV - 1, jnp.logical_and(qi == nq - 1, ki == nkv - 1)))
        def _():
            plan.finish()

    return pl.pallas_call(
        body,
        out_shape=(jax.ShapeDtypeStruct((seq, N_HEADS * HEAD_DIM), F32),
                   jax.ShapeDtypeStruct((seq, N_KV * HEAD_DIM), F32), jax.ShapeDtypeStruct((seq, N_KV * HEAD_DIM), F32),
                   *plan.out_shape),
        grid=(N_KV, nq, nkv),
        in_specs=[pl.BlockSpec((tq, gw), lambda g, qi, ki: (qi, g)),
                  pl.BlockSpec((tk, LANES), lambda g, qi, ki: (ki, g // 2)),
                  pl.BlockSpec((LANES, tk), lambda g, qi, ki: (g // 2, ki)),
                  pl.BlockSpec((tk, LANES), lambda g, qi, ki: (ki, g // 2)),
                  pl.BlockSpec((tq, gw), lambda g, qi, ki: (qi, g)),
                  pl.BlockSpec((tq, gw), lambda g, qi, ki: (qi, g)),
                  pl.BlockSpec((8, 4 * tq), lambda g, qi, ki: (g * nq + qi, 0))] + [ANY] * nt,
        out_specs=(pl.BlockSpec((tq, gw), lambda g, qi, ki: (qi, g)),
                   pl.BlockSpec((seq, LANES), lambda g, qi, ki: (0, g // 2)),
                   pl.BlockSpec((seq, LANES), lambda g, qi, ki: (0, g // 2)), *([ANY] * nt)),
        scratch_shapes=[pltpu.VMEM((4 * tq, LANES), BF16), pltpu.VMEM((4 * tq, LANES), BF16),
                        pltpu.VMEM((1, 4 * tq), F32), pltpu.VMEM((LANES, 4 * tq), F32)] + plan.scratch,
        compiler_params=_cp(("arbitrary", "arbitrary", "arbitrary")), name=name)(q, k, kt, v, do, o, lse, *grads)


def _xattn_fwd(q, kv, name):
    seq, d = q.shape
    mlen = kv.shape[0]
    tq = _pick(seq, (512, 256))

    def body(q_ref, k_ref, v_ref, o_ref):
        for h in range(X_HEADS):
            sl = slice(h * X_HEAD_DIM, (h + 1) * X_HEAD_DIM)
            s = lax.dot_general(q_ref[:, sl], k_ref[:, sl], (((1,), (1,)), ((), ())), preferred_element_type=F32)
            e = jnp.exp(s - jnp.max(s, axis=-1, keepdims=True))
            p = e / jnp.sum(e, axis=-1, keepdims=True)
            o_ref[:, sl] = jnp.dot(p.astype(BF16), v_ref[:, sl], preferred_element_type=F32).astype(o_ref.dtype)

    return pl.pallas_call(
        body, out_shape=jax.ShapeDtypeStruct((seq, d), BF16), grid=(seq // tq,),
        in_specs=[pl.BlockSpec((tq, d), lambda i: (i, 0)), pl.BlockSpec((mlen, d), lambda i: (0, 0)),
                  pl.BlockSpec((mlen, d), lambda i: (0, 1))],
        out_specs=pl.BlockSpec((tq, d), lambda i: (i, 0)), compiler_params=_cp(("parallel",)), name=name)(q, kv, kv)


def _xattn_bwd(q, kv, do, name):
    seq, d = q.shape
    mlen = kv.shape[0]
    tq = _pick(seq, (512, 256))
    scale = X_HEAD_DIM ** -0.5

    def body(q_ref, k_ref, v_ref, do_ref, dq_ref, dkv_ref):
        i = pl.program_id(0)

        @pl.when(i == 0)
        def _():
            dkv_ref[...] = jnp.zeros(dkv_ref.shape, F32)

        for h in range(X_HEADS):
            sl = slice(h * X_HEAD_DIM, (h + 1) * X_HEAD_DIM)
            qh, kh, vh = q_ref[:, sl], k_ref[:, sl], v_ref[:, sl]
            doh = do_ref[:, sl].astype(BF16)
            st = lax.dot_general(kh, qh, (((1,), (1,)), ((), ())), preferred_element_type=F32)
            e = jnp.exp(st - jnp.max(st, axis=0, keepdims=True))
            pt = e / jnp.sum(e, axis=0, keepdims=True)
            dpt = lax.dot_general(vh, doh, (((1,), (1,)), ((), ())), preferred_element_type=F32)
            dst = (pt * (dpt - jnp.sum(pt * dpt, axis=0, keepdims=True))).astype(BF16)
            dkv_ref[:, sl] += jnp.dot(dst, qh, preferred_element_type=F32)
            dkv_ref[:, d + h * X_HEAD_DIM:d + (h + 1) * X_HEAD_DIM] += jnp.dot(pt.astype(BF16), doh,
                                                                                 preferred_element_type=F32)
            dqh = lax.dot_general(dst, kh, (((0,), (0,)), ((), ())), preferred_element_type=F32)
            dq_ref[:, sl] = (dqh * scale).astype(dq_ref.dtype)

    return pl.pallas_call(
        body, out_shape=(jax.ShapeDtypeStruct((seq, d), BF16), jax.ShapeDtypeStruct((mlen, 2 * d), F32)),
        grid=(seq // tq,),
        in_specs=[pl.BlockSpec((tq, d), lambda i: (i, 0)), pl.BlockSpec((mlen, d), lambda i: (0, 0)),
                  pl.BlockSpec((mlen, d), lambda i: (0, 1)), pl.BlockSpec((tq, d), lambda i: (i, 0))],
        out_specs=(pl.BlockSpec((tq, d), lambda i: (i, 0)), pl.BlockSpec((mlen, 2 * d), lambda i: (0, 0))),
        compiler_params=_cp(("arbitrary",)), name=name)(q, kv, kv, do)


def _halo_specs(tr, tc, seq, col):
    per, last = tr // HALO, seq // HALO - 1
    return [pl.BlockSpec((tr, tc), lambda j, r: (r, col(j))),
            pl.BlockSpec((HALO, tc), lambda j, r: (jnp.maximum(r * per - 1, 0), col(j))),
            pl.BlockSpec((HALO, tc), lambda j, r: (jnp.minimum((r + 1) * per, last), col(j)))]


def _extend(main_ref, prev_ref, next_ref, r, nr):
    pv = (r > 0).astype(F32)
    nv = (r < nr - 1).astype(F32)
    return jnp.concatenate([prev_ref[...].astype(F32) * pv, main_ref[...].astype(F32),
                            next_ref[...].astype(F32) * nv], axis=0)


def _conv3(e, w_ref, n):
    return pltpu.roll(e, 1, axis=0) * w_ref[0:1, :] + e * w_ref[1:2, :] + pltpu.roll(e, n - 1, axis=0) * w_ref[2:3, :]


def _conv_gate_fwd(ug, uv, cw, cb, layer, name):
    seq, f = ug.shape
    tc = 256
    tr = _pick(seq, (512, 256))
    nc, nr = f // tc, seq // tr
    n = tr + 2 * HALO

    def body(g_ref, gp_ref, gn_ref, v_ref, vp_ref, vn_ref, wg_ref, wv_ref, bg_ref, bv_ref, o_ref):
        r = pl.program_id(1)
        cg = _conv3(_extend(g_ref, gp_ref, gn_ref, r, nr), wg_ref, n)[HALO:HALO + tr] + bg_ref[...]
        cv = _conv3(_extend(v_ref, vp_ref, vn_ref, r, nr), wv_ref, n)[HALO:HALO + tr] + bv_ref[...]
        o_ref[...] = (cg * jax.nn.sigmoid(cg) * cv).astype(o_ref.dtype)

    w_spec = lambda shift: pl.BlockSpec((None, 3, tc), lambda j, r: (layer, 0, j + shift))
    b_spec = lambda shift: pl.BlockSpec((None, 1, tc), lambda j, r: (layer, 0, j + shift))
    return pl.pallas_call(
        body, out_shape=jax.ShapeDtypeStruct((seq, f), BF16), grid=(nc, nr),
        in_specs=_halo_specs(tr, tc, seq, lambda j: j) * 2 + [w_spec(0), w_spec(nc), b_spec(0), b_spec(nc)],
        out_specs=pl.BlockSpec((tr, tc), lambda j, r: (r, j)),
        compiler_params=_cp(("parallel", "parallel")), name=name)(ug, ug, ug, uv, uv, uv, cw, cw, cb, cb)


def _conv_gate_bwd(ug, uv, dact, cw, cb, layer, name):
    seq, f = ug.shape
    tc = 256
    tr = _pick(seq, (512, 256))
    nc, nr = f // tc, seq // tr
    n = tr + 2 * HALO

    def body(g_ref, gp_ref, gn_ref, v_ref, vp_ref, vn_ref, d_ref, dp_ref, dn_ref, wg_ref, wv_ref, bg_ref, bv_ref,
             dug_ref, duv_ref, dwg_ref, dwv_ref):
        r = pl.program_id(1)
        eg = _extend(g_ref, gp_ref, gn_ref, r, nr)
        ev = _extend(v_ref, vp_ref, vn_ref, r, nr)
        da = _extend(d_ref, dp_ref, dn_ref, r, nr)
        cg = _conv3(eg, wg_ref, n) + bg_ref[...]
        cv = _conv3(ev, wv_ref, n) + bv_ref[...]
        sg = jax.nn.sigmoid(cg)
        dcv = da * (cg * sg)
        dcg = da * cv * (sg * (1.0 + cg * (1.0 - sg)))

        def back(dc, e, w_ref, du_ref, dw_ref):
            du = (pltpu.roll(dc, n - 1, axis=0) * w_ref[0:1, :] + dc * w_ref[1:2, :]
                  + pltpu.roll(dc, 1, axis=0) * w_ref[2:3, :])
            du_ref[...] = du[HALO:HALO + tr].astype(du_ref.dtype)
            dcm = dc[HALO:HALO + tr]
            taps = [jnp.sum(dcm * pltpu.roll(e, 1, axis=0)[HALO:HALO + tr], axis=0, keepdims=True),
                    jnp.sum(dcm * e[HALO:HALO + tr], axis=0, keepdims=True),
                    jnp.sum(dcm * pltpu.roll(e, n - 1, axis=0)[HALO:HALO + tr], axis=0, keepdims=True),
                    jnp.sum(dcm, axis=0, keepdims=True)]
            part = jnp.concatenate(taps + [jnp.zeros((4, tc), F32)], axis=0)

            @pl.when(r == 0)
            def _():
                dw_ref[...] = part

            @pl.when(r > 0)
            def _():
                dw_ref[...] += part

        back(dcg, eg, wg_ref, dug_ref, dwg_ref)
        back(dcv, ev, wv_ref, duv_ref, dwv_ref)

    w_spec = lambda shift: pl.BlockSpec((None, 3, tc), lambda j, r: (layer, 0, j + shift))
    b_spec = lambda shift: pl.BlockSpec((None, 1, tc), lambda j, r: (layer, 0, j + shift))
    out_rows = pl.BlockSpec((tr, tc), lambda j, r: (r, j))
    out_acc = pl.BlockSpec((8, tc), lambda j, r: (0, j))
    return pl.pallas_call(
        body,
        out_shape=(jax.ShapeDtypeStruct((seq, f), BF16), jax.ShapeDtypeStruct((seq, f), BF16),
                   jax.ShapeDtypeStruct((8, f), F32), jax.ShapeDtypeStruct((8, f), F32)),
        grid=(nc, nr),
        in_specs=_halo_specs(tr, tc, seq, lambda j: j) * 3 + [w_spec(0), w_spec(nc), b_spec(0), b_spec(nc)],
        out_specs=(out_rows, out_rows, out_acc, out_acc),
        compiler_params=_cp(("parallel", "arbitrary")), name=name)(ug, ug, ug, uv, uv, uv, dact, dact, dact, cw, cw, cb, cb)


def _pool_count(g, r, tr, n, seq):
    half = jnp.left_shift(1, g)
    t = r * tr - HALO + lax.broadcasted_iota(jnp.int32, (n, 1), 0)
    cnt = jnp.minimum(t + half, seq) - jnp.maximum(t - half, 0)
    return jnp.maximum(cnt, 1).astype(F32)


def _by_group(g, levels):
    out = levels[3]
    for i in (2, 1, 0):
        out = jnp.where(g == i, levels[i], out)
    return out


def _pool_mixed(e, g, cnt, n):
    w2 = e + pltpu.roll(e, 1, axis=0)
    w4 = pltpu.roll(w2, 1, axis=0) + pltpu.roll(w2, n - 1, axis=0)
    w8 = pltpu.roll(w4, 2, axis=0) + pltpu.roll(w4, n - 2, axis=0)
    w16 = pltpu.roll(w8, 4, axis=0) + pltpu.roll(w8, n - 4, axis=0)
    return _by_group(g, (w2, w4, w8, w16)) / cnt - e


def _pool_fwd(hp, xres, pw, scale, name):
    seq, d = hp.shape
    tc = POOL_GROUP_W
    tr = _pick(seq, (512, 256))
    nr = seq // tr
    n = tr + 2 * HALO

    def body(h_ref, hp_ref, hn_ref, x_ref, w_ref, s_ref, o_ref):
        g, r = pl.program_id(0), pl.program_id(1)
        e = _extend(h_ref, hp_ref, hn_ref, r, nr)
        mixed = _pool_mixed(e, g, _pool_count(g, r, tr, n, seq), n)[HALO:HALO + tr]
        y = jnp.dot(mixed.astype(BF16), w_ref[...], preferred_element_type=F32)
        o_ref[...] = x_ref[...] + y * s_ref[...]

    return pl.pallas_call(
        body, out_shape=jax.ShapeDtypeStruct((seq, d), F32), grid=(POOL_GROUPS, nr),
        in_specs=_halo_specs(tr, tc, seq, lambda j: j) + [
            pl.BlockSpec((tr, tc), lambda j, r: (r, j)), pl.BlockSpec((None, tc, tc), lambda j, r: (j, 0, 0)),
            pl.BlockSpec((1, tc), lambda j, r: (0, j))],
        out_specs=pl.BlockSpec((tr, tc), lambda j, r: (r, j)),
        compiler_params=_cp(("parallel", "parallel")), name=name)(hp, hp, hp, xres, pw, scale)


def _pool_bwd(hp, dy, pw, scale, name):
    seq, d = hp.shape
    tc = POOL_GROUP_W
    tr = _pick(seq, (512, 256))
    nr = seq // tr
    n = tr + 2 * HALO

    def body(h_ref, hp_ref, hn_ref, d_ref, dp_ref, dn_ref, w_ref, s_ref, dh_ref, dw_ref, ds_ref):
        g, r = pl.program_id(0), pl.program_id(1)
        cnt = _pool_count(g, r, tr, n, seq)
        e = _extend(h_ref, hp_ref, hn_ref, r, nr)
        mixed = _pool_mixed(e, g, cnt, n)[HALO:HALO + tr].astype(BF16)
        dye = _extend(d_ref, dp_ref, dn_ref, r, nr)
        dyp = (dye * s_ref[...]).astype(BF16)
        dmixed = lax.dot_general(dyp, w_ref[...], (((1,), (1,)), ((), ())), preferred_element_type=F32)
        dwin = dmixed / cnt
        m2 = dwin + pltpu.roll(dwin, n - 1, axis=0)
        m4 = pltpu.roll(m2, 1, axis=0) + pltpu.roll(m2, n - 1, axis=0)
        m8 = pltpu.roll(m4, 2, axis=0) + pltpu.roll(m4, n - 2, axis=0)
        m16 = pltpu.roll(m8, 4, axis=0) + pltpu.roll(m8, n - 4, axis=0)
        dh_ref[...] = (_by_group(g, (m2, m4, m8, m16)) - dmixed)[HALO:HALO + tr]
        ypre = jnp.dot(mixed, w_ref[...], preferred_element_type=F32)
        dsp = jnp.sum(d_ref[...] * ypre, axis=0, keepdims=True)
        dwp = lax.dot_general(mixed, dyp[HALO:HALO + tr], (((0,), (0,)), ((), ())), preferred_element_type=F32)

        @pl.when(r == 0)
        def _():
            dw_ref[...] = dwp
            ds_ref[...] = dsp

        @pl.when(r > 0)
        def _():
            dw_ref[...] += dwp
            ds_ref[...] += dsp

    return pl.pallas_call(
        body,
        out_shape=(jax.ShapeDtypeStruct((seq, d), F32), jax.ShapeDtypeStruct((POOL_GROUPS, tc, tc), F32),
                   jax.ShapeDtypeStruct((1, d), F32)),
        grid=(POOL_GROUPS, nr),
        in_specs=_halo_specs(tr, tc, seq, lambda j: j) * 2 + [
            pl.BlockSpec((None, tc, tc), lambda j, r: (j, 0, 0)), pl.BlockSpec((1, tc), lambda j, r: (0, j))],
        out_specs=(pl.BlockSpec((tr, tc), lambda j, r: (r, j)), pl.BlockSpec((None, tc, tc), lambda j, r: (j, 0, 0)),
                   pl.BlockSpec((1, tc), lambda j, r: (0, j))),
        compiler_params=_cp(("parallel", "arbitrary")), name=name)(hp, hp, hp, dy, dy, dy, pw, scale)


def _adamw_math(w, g, m, v):
    m = ADAM_B1 * m + (1.0 - ADAM_B1) * g
    v = ADAM_B2 * v + (1.0 - ADAM_B2) * (g * g)
    m_hat = m / (1.0 - ADAM_B1 ** ADAM_STEP)
    v_hat = v / (1.0 - ADAM_B2 ** ADAM_STEP)
    delta = -ADAM_LR * (m_hat / (jnp.sqrt(v_hat) + ADAM_EPS) + ADAM_WD * w)
    return delta, m, v


def _adamw(w, ga, gb, m, v, name):
    rows, cols = w.shape
    tr = _pick(rows, (256, 128, 64, 32, 16, 8))
    two = gb is not None

    def body(*refs):
        if two:
            w_ref, ga_ref, gb_ref, m_ref, v_ref, g_out, d_out, m_out, v_out = refs
            g = ga_ref[...] + gb_ref[...]
        else:
            w_ref, ga_ref, m_ref, v_ref, g_out, d_out, m_out, v_out = refs
            g = ga_ref[...]
        delta, m, v = _adamw_math(w_ref[...], g, m_ref[...], v_ref[...])
        g_out[...] = g
        d_out[...] = delta
        m_out[...] = m
        v_out[...] = v

    spec = pl.BlockSpec((tr, cols), lambda i: (i, 0))
    ops = [w, ga] + ([gb] if two else []) + [m, v]
    return pl.pallas_call(
        body, out_shape=tuple(jax.ShapeDtypeStruct((rows, cols), F32) for _ in range(4)), grid=(rows // tr,),
        in_specs=[spec] * len(ops), out_specs=(spec,) * 4, compiler_params=_cp(("parallel",)), name=name)(*ops)


def _sum4(parts, name):
    _, rows, cols = parts.shape
    tr = _pick(rows, (256, 128, 64, 32, 16))

    def body(p_ref, o_ref):
        acc = p_ref[0].astype(F32)
        for kk in range(1, 4):
            acc = acc + p_ref[kk].astype(F32)
        o_ref[...] = acc

    return pl.pallas_call(
        body, out_shape=jax.ShapeDtypeStruct((rows, cols), F32), grid=(rows // tr,),
        in_specs=[pl.BlockSpec((4, tr, cols), lambda i: (0, i, 0))], out_specs=pl.BlockSpec((tr, cols), lambda i: (i, 0)),
        compiler_params=_cp(("parallel",)), name=name)(parts)


def _place():
    x, y, c = lax.axis_index("x"), lax.axis_index("y"), lax.axis_index("c")
    chips = [(1 - x, y), (x, 1 - y), (1 - x, 1 - y)]
    return x, y, c, chips


def _window(ref, axis, j, size, c=None, half=None, lead=()):
    if axis == "r":
        if c is None:
            return ref.at[lead + (slice(None), pl.ds(pl.multiple_of(j * size, 32), size), slice(None))]
        return ref.at[lead + (slice(None), pl.ds(pl.multiple_of(j * size + c * half, 32), half), slice(None))]
    cols = pl.ds(pl.multiple_of(j * size, LANES), size)
    if c is None:
        return ref.at[lead + (slice(None), slice(None), cols)]
    return ref.at[lead + (slice(None), pl.ds(pl.multiple_of(c * half, 32), half), cols)]


class _Gather:
    def __init__(self, shards, axes):
        self.nt, self.axes = len(shards), axes
        self.out_shape, self.sizes, self.halves = [], [], []
        for s, ax in zip(shards, axes):
            l, rs, cs = s.shape
            self.out_shape.append(jax.ShapeDtypeStruct((l, 4 * rs, cs) if ax == "r" else (l, rs, 4 * cs), s.dtype))
            self.sizes.append(rs if ax == "r" else cs)
            self.halves.append(rs // 2)
        self.scratch = [pltpu.SemaphoreType.DMA((6 * self.nt,)), pltpu.SemaphoreType.DMA((6 * self.nt,)),
                        pltpu.SemaphoreType.DMA((self.nt,))]

    def bind(self, src, dst, send_sems, recv_sems, local_sems):
        self.src, self.dst, self.send_sems, self.recv_sems, self.local_sems = src, dst, send_sems, recv_sems, local_sems

    def _win(self, t, j, core=None):
        return _window(self.dst[t], self.axes[t], j, self.sizes[t], core, self.halves[t])

    def _ici(self, t, kk, origin):
        _, _, c, chips = _place()
        px, py = chips[kk]
        half = self.src[t].at[:, pl.ds(pl.multiple_of(c * self.halves[t], 16), self.halves[t]), :]
        return pltpu.make_async_remote_copy(
            src_ref=half, dst_ref=self._win(t, origin, c), send_sem=self.send_sems.at[t * 3 + kk],
            recv_sem=self.recv_sems.at[t * 3 + kk], device_id=(px, py, c), device_id_type=MESH)

    def _d2d(self, t, kk, origin, core):
        x, y, c, _ = _place()
        k2 = 3 * self.nt + t * 3 + kk
        return pltpu.make_async_remote_copy(
            src_ref=self._win(t, origin, core), dst_ref=self._win(t, origin, core), send_sem=self.send_sems.at[k2],
            recv_sem=self.recv_sems.at[k2], device_id=(x, y, 1 - c), device_id_type=MESH)

    def _local(self, t):
        x, y, _, _ = _place()
        return pltpu.make_async_copy(self.src[t], self._win(t, 2 * x + y), self.local_sems.at[t])

    def _each(self):
        _, _, _, chips = _place()
        for t in range(self.nt):
            for kk in range(3):
                px, py = chips[kk]
                yield t, kk, 2 * px + py

    def start(self):
        x, y, _, _ = _place()
        for t in range(self.nt):
            self._local(t).start()
        for t, kk, _ in self._each():
            self._ici(t, kk, 2 * x + y).start()

    def forward(self):
        _, _, c, _ = _place()
        for t, kk, origin in self._each():
            self._ici(t, kk, origin).wait_recv()
            self._d2d(t, kk, origin, c).start()

    def finish(self):
        x, y, c, _ = _place()
        for t, kk, origin in self._each():
            self._d2d(t, kk, origin, 1 - c).wait_recv()
        for t, kk, origin in self._each():
            self._ici(t, kk, 2 * x + y).wait_send()
            self._d2d(t, kk, origin, c).wait_send()
        for t in range(self.nt):
            self._local(t).wait()


class _Scatter:
    def __init__(self, grads, axes):
        self.nt, self.axes = len(grads), axes
        self.out_shape, self.sizes = [], []
        for gr, ax in zip(grads, axes):
            l, r, cc = gr.shape
            self.out_shape.append(jax.ShapeDtypeStruct((4, l, r // 4, cc) if ax == "r" else (4, l, r, cc // 4), gr.dtype))
            self.sizes.append(r // 4 if ax == "r" else cc // 4)
        self.scratch = [pltpu.SemaphoreType.DMA((3 * self.nt,)), pltpu.SemaphoreType.DMA((3 * self.nt,)),
                        pltpu.SemaphoreType.DMA((self.nt,))]

    def bind(self, src, dst, send_sems, recv_sems, local_sems):
        self.src, self.dst, self.send_sems, self.recv_sems, self.local_sems = src, dst, send_sems, recv_sems, local_sems

    def _copy(self, t, kk, slot):
        x, y, c, chips = _place()
        px, py = chips[kk]
        return pltpu.make_async_remote_copy(
            src_ref=_window(self.src[t], self.axes[t], 2 * px + py, self.sizes[t]), dst_ref=self.dst[t].at[slot],
            send_sem=self.send_sems.at[t * 3 + kk], recv_sem=self.recv_sems.at[t * 3 + kk],
            device_id=(px, py, c), device_id_type=MESH)

    def _local(self, t):
        x, y, _, _ = _place()
        me = 2 * x + y
        return pltpu.make_async_copy(_window(self.src[t], self.axes[t], me, self.sizes[t]), self.dst[t].at[me],
                                     self.local_sems.at[t])

    def start(self):
        x, y, _, _ = _place()
        for t in range(self.nt):
            self._local(t).start()
            for kk in range(3):
                self._copy(t, kk, 2 * x + y).start()

    def finish(self):
        _, _, _, chips = _place()
        for t in range(self.nt):
            for kk in range(3):
                px, py = chips[kk]
                self._copy(t, kk, 2 * px + py).wait_recv()
        for t in range(self.nt):
            for kk in range(3):
                px, py = chips[kk]
                self._copy(t, kk, 2 * px + py).wait_send()
            self._local(t).wait()


def _comm_call(plan, operands, name):
    nt = plan.nt

    def body(*refs):
        plan.bind(refs[:nt], refs[nt:2 * nt], *refs[2 * nt:])
        plan.start()
        if hasattr(plan, "forward"):
            plan.forward()
        plan.finish()

    return pl.pallas_call(body, out_shape=tuple(plan.out_shape), in_specs=[ANY] * nt, out_specs=tuple([ANY] * nt),
                          scratch_shapes=plan.scratch, name=name)(*operands)


def _swap_sibling(arrs, name):
    nt = len(arrs)

    def body(*refs):
        src, dst = refs[:nt], refs[nt:2 * nt]
        send_sems, recv_sems = refs[2 * nt:]
        x, y, c, _ = _place()
        cps = [pltpu.make_async_remote_copy(src_ref=src[t], dst_ref=dst[t], send_sem=send_sems.at[t],
                                            recv_sem=recv_sems.at[t], device_id=(x, y, 1 - c), device_id_type=MESH)
               for t in range(nt)]
        for cp in cps:
            cp.start()
        for cp in cps:
            cp.wait()

    return pl.pallas_call(
        body, out_shape=tuple(jax.ShapeDtypeStruct(a.shape, a.dtype) for a in arrs), in_specs=[ANY] * nt,
        out_specs=tuple([ANY] * nt),
        scratch_shapes=[pltpu.SemaphoreType.DMA((nt,)), pltpu.SemaphoreType.DMA((nt,))], name=name)(*arrs)


def _gather8(pack, with_sum, name):
    rows = pack.shape[0]
    flips = [f for f in itertools.product((0, 1), repeat=3) if any(f)]

    def body(p_ref, all_ref, *rest):
        if with_sum:
            sum_ref, send_sems, recv_sems = rest
        else:
            send_sems, recv_sems = rest
        x, y, c, _ = _place()
        me = 4 * x + 2 * y + c

        def peer(f):
            return tuple(1 - v if fl else v for v, fl in zip((x, y, c), f))

        all_ref[me] = p_ref[...]
        cps = []
        for kk, f in enumerate(flips):
            cp = pltpu.make_async_remote_copy(src_ref=p_ref, dst_ref=all_ref.at[me], send_sem=send_sems.at[kk],
                                              recv_sem=recv_sems.at[kk], device_id=peer(f), device_id_type=MESH)
            cp.start()
            cps.append(cp)
        for kk, f in enumerate(flips):
            px, py, pc = peer(f)
            pltpu.make_async_remote_copy(src_ref=p_ref, dst_ref=all_ref.at[4 * px + 2 * py + pc],
                                         send_sem=send_sems.at[kk], recv_sem=recv_sems.at[kk], device_id=peer(f),
                                         device_id_type=MESH).wait_recv()
        for cp in cps:
            cp.wait_send()
        if with_sum:
            acc = all_ref[0]
            for d in range(1, 8):
                acc = acc + all_ref[d]
            sum_ref[...] = acc

    vm = pl.BlockSpec(memory_space=pltpu.VMEM)
    out_shape = [jax.ShapeDtypeStruct((8, rows, LANES), F32)] + ([jax.ShapeDtypeStruct((rows, LANES), F32)] if with_sum else [])
    return pl.pallas_call(
        body, out_shape=tuple(out_shape), in_specs=[vm], out_specs=tuple([vm] * len(out_shape)),
        scratch_shapes=[pltpu.SemaphoreType.DMA((7,)), pltpu.SemaphoreType.DMA((7,))], name=name)(pack)


def _pack(arrs):
    flat = jnp.concatenate([a.reshape(-1).astype(F32) for a in arrs])
    rows = -(-flat.shape[0] // (8 * LANES)) * 8
    return jnp.pad(flat, (0, rows * LANES - flat.shape[0])).reshape(rows, LANES)


def _unpack(flat, shapes):
    out, pos = [], 0
    for shp in shapes:
        size = 1
        for s in shp:
            size *= s
        out.append(flat[pos:pos + size].reshape(shp))
        pos += size
    return out


BIG = ("attn_w_qkv", "attn_w_o", "pool_w", "xattn_w_q", "xattn_w_kv", "xattn_w_o", "ffn_w_up", "ffn_w_down")
BIG_AXIS = ("c", "r", "r", "r", "c", "r", "c", "r")
SMALL_REPL = ("attn_norm", "attn_q_gain", "attn_k_gain", "xattn_norm", "mem_norm", "ffn_norm", "ffn_conv_b", "final_norm")
SMALL_SHARD = ("pool_norm", "pool_scale", "ffn_conv_w")
ORDER = ("attn_norm", "attn_w_qkv", "attn_q_gain", "attn_k_gain", "attn_w_o", "pool_norm", "pool_w", "pool_scale",
         "xattn_norm", "mem_norm", "xattn_w_q", "xattn_w_kv", "xattn_w_o", "ffn_norm", "ffn_w_up", "ffn_conv_w",
         "ffn_conv_b", "ffn_w_down", "final_norm")


def _step(x, mem, tgt, w, m, v):
    seq, d = x.shape
    xi, yi, ci = lax.axis_index("x"), lax.axis_index("y"), lax.axis_index("c")
    chip = 2 * xi + yi
    dff = w["ffn_w_down"].shape[1] * 4
    n_layers = w["ffn_norm"].shape[0]

    def as3d(a):
        return a.reshape(a.shape[-3:])
    shards = [as3d(w[nm]).astype(BF16) for nm in BIG]
    (wq,) = _comm_call(_Gather(shards[:1], BIG_AXIS[:1]), shards[:1], "gather_qkv")
    small_in = [w[nm] for nm in SMALL_SHARD]
    (small_all,) = _gather8(_pack(small_in), False, "gather_small")
    per_chip = [_unpack(small_all[2 * j].reshape(-1), [a.shape for a in small_in]) for j in range(4)]
    pool_norm, pool_scale, conv_w = (jnp.concatenate([per_chip[j][i] for j in range(4)], axis=-1) for i in range(3))

    conv_b = w["ffn_conv_b"].reshape(n_layers, 1, -1)
    tabs = _rope_tables(seq)
    qg2 = jnp.tile(w["attn_q_gain"], (1, 2))
    kg2 = jnp.tile(w["attn_k_gain"], (1, 2))
    mm = functools.partial(_mm)

    saved = {}
    x0 = x
    h0 = _rms_fwd(x0, w["attn_norm"], BF16, "rms_attn")
    qkv = mm(h0, wq, "nn", b_l=0, out_dtype=F32, name="mm_qkv")
    q_r, k_r, k_t, v_b, v_t = _qk_prep(qkv, qg2, kg2, tabs, "qk_prep")
    o_at, lse, wo, wp, wxq, wxkv, wxo, wup, wdn = _flash_fwd(
        q_r, k_r, v_t, _Gather(shards[1:], BIG_AXIS[1:]), shards[1:], "flash_fwd")
    xs = [x0, mm(o_at, wo, "nn", b_l=0, res=x0, out_dtype=F32, name="mm_attn_o")]

    def xattn_fwd(l, xin):
        hq = _rms_fwd(xin, w["xattn_norm"][l:l + 1], BF16, f"rms_xq{l}")
        mn = _rms_fwd(mem, w["mem_norm"][l:l + 1], BF16, f"rms_mem{l}")
        xq = mm(hq, wxq, "nn", b_l=l, scale=X_HEAD_DIM ** -0.5, out_dtype=BF16, name=f"mm_xq{l}")
        kv = mm(mn, wxkv, "nn", b_l=l, out_dtype=BF16, name=f"mm_xkv{l}")
        xo = _xattn_fwd(xq, kv, f"xattn_fwd{l}")
        saved[f"x{l}"] = (hq, mn, xq, kv, xo)
        return mm(xo, wxo, "nn", b_l=l, res=xin, out_dtype=F32, name=f"mm_xo{l}")

    def ffn_fwd(l, xin):
        hf = _rms_fwd(xin, w["ffn_norm"][l:l + 1], BF16, f"rms_ffn{l}")
        ug = mm(hf, wup, "nn", b_l=l, n=dff, out_dtype=F32, name=f"mm_up_g{l}")
        uv = mm(hf, wup, "nn", b_l=l, n=dff, b_off=(0, dff), out_dtype=F32, name=f"mm_up_v{l}")
        act = _conv_gate_fwd(ug, uv, conv_w, conv_b, l, f"conv_gate{l}")
        saved[f"f{l}"] = (hf, ug, uv, act)
        return mm(act, wdn, "nn", b_l=l, res=xin, out_dtype=F32, name=f"mm_down{l}")

    xs.append(xattn_fwd(0, xs[-1]))
    xs.append(ffn_fwd(0, xs[-1]))
    hp = _rms_fwd(xs[-1], pool_norm, F32, "rms_pool")
    xs.append(_pool_fwd(hp, xs[-1], wp, pool_scale, "pool_fwd"))
    xs.append(xattn_fwd(1, xs[-1]))
    xs.append(ffn_fwd(1, xs[-1]))
    dres, g_final, loss = _final_loss(xs[6], w["final_norm"].reshape(1, d), tgt, "final_loss")

    grads = {}
    gbuf = {}

    def dw(nm, a, b, layer, full, off=(0, 0), n=None, tn=None):
        gbuf[nm] = _mm(a, b, "tn", out_dtype=BF16, out_full=full, out_l=layer, out_off=off, n=n, tn=tn,
                       alias=gbuf.get(nm), name=f"dw_{nm}{layer}_{off[1]}")

    def ffn_bwd(l, xin, dres):
        hf, ug, uv, act = saved[f"f{l}"]
        dw("ffn_w_down", act, dres, l, wdn.shape)
        dact = _mm(dres, wdn, "nt", b_l=l, out_dtype=F32, name=f"mm_dact{l}")
        dug, duv, dwg, dwv = _conv_gate_bwd(ug, uv, dact, conv_w, conv_b, l, f"conv_gate_bwd{l}")
        dw("ffn_w_up", hf, dug, l, wup.shape, tn=1408)
        dw("ffn_w_up", hf, duv, l, wup.shape, off=(0, dff), tn=1408)
        dhf = _mm(dug, wup, "nt", b_l=l, n=d, out_dtype=F32, name=f"mm_dhf_g{l}")
        dhf = _mm(duv, wup, "nt", b_l=l, n=d, b_off=(0, dff), res=dhf, out_dtype=F32, name=f"mm_dhf_v{l}")
        dres, dg = _rms_bwd(xin, w["ffn_norm"][l:l + 1], dhf, dres, f"rms_ffn_bwd{l}")
        return dres, dg, jnp.concatenate([dwg[:3], dwv[:3]], axis=1), jnp.concatenate([dwg[3], dwv[3]], axis=0)

    def xattn_bwd(l, xin, dres):
        hq, mn, xq, kv, xo = saved[f"x{l}"]
        dw("xattn_w_o", xo, dres, l, wxo.shape)
        dxo = _mm(dres, wxo, "nt", b_l=l, out_dtype=F32, name=f"mm_dxo{l}")
        dq, dkv = _xattn_bwd(xq, kv, dxo, f"xattn_bwd{l}")
        dw("xattn_w_q", hq, dq, l, wxq.shape)
        dhq = _mm(dq, wxq, "nt", b_l=l, out_dtype=F32, name=f"mm_dhq{l}")
        dw("xattn_w_kv", mn, dkv, l, wxkv.shape)
        dmn = _mm(dkv, wxkv, "nt", b_l=l, out_dtype=F32, name=f"mm_dmn{l}")
        _, dg_mem = _rms_bwd(mem, w["mem_norm"][l:l + 1], dmn, None, f"rms_mem_bwd{l}")
        dres, dg = _rms_bwd(xin, w["xattn_norm"][l:l + 1], dhq, dres, f"rms_xq_bwd{l}")
        return dres, dg, dg_mem

    g_ffn, g_xn, g_mn, g_cw, g_cb = [None] * n_layers, [None] * n_layers, [None] * n_layers, [None] * n_layers, [None] * n_layers
    dres, g_ffn[1], g_cw[1], g_cb[1] = ffn_bwd(1, xs[5], dres)
    dres, g_xn[1], g_mn[1] = xattn_bwd(1, xs[4], dres)
    dhp, g_pw, g_pscale = _pool_bwd(hp, dres, wp, pool_scale, "pool_bwd")
    dres, g_pnorm = _rms_bwd(xs[3], pool_norm, dhp, dres, "rms_pool_bwd")
    dres, g_ffn[0], g_cw[0], g_cb[0] = ffn_bwd(0, xs[2], dres)
    dres, g_xn[0], g_mn[0] = xattn_bwd(0, xs[1], dres)
    dw("attn_w_o", o_at, dres, 0, wo.shape)
    do = _mm(dres, wo, "nt", b_l=0, out_dtype=F32, name="mm_do")
    gbuf["pool_w"] = g_pw.astype(BF16)
    early = [gbuf[nm] for nm in BIG[2:]]
    dq_r, dk_r, dv, *recv_early = _flash_bwd(q_r, k_r, k_t, v_b, do, o_at, lse, _Scatter(early, BIG_AXIS[2:]), early,
                                             "flash_bwd")
    dqkv, dqg, dkg = _qk_prep_bwd(qkv, dq_r, dk_r, dv, qg2, kg2, tabs, "qk_prep_bwd")
    dw("attn_w_qkv", h0, dqkv, 0, wq.shape)
    dh0 = _mm(dqkv, wq, "nt", b_l=0, out_dtype=F32, name="mm_dh0")
    grad_x, g_an = _rms_bwd(x0, w["attn_norm"], dh0, dres, "rms_attn_bwd")

    small_g = {
        "attn_norm": g_an, "attn_q_gain": dqg[:, :HEAD_DIM] + dqg[:, HEAD_DIM:], "attn_k_gain": dkg[:, :HEAD_DIM] + dkg[:, HEAD_DIM:],
        "xattn_norm": jnp.concatenate(g_xn, axis=0), "mem_norm": jnp.concatenate(g_mn, axis=0),
        "ffn_norm": jnp.concatenate(g_ffn, axis=0), "ffn_conv_b": jnp.stack(g_cb, axis=0), "final_norm": g_final.reshape(d),
        "pool_norm": g_pnorm, "pool_scale": g_pscale, "ffn_conv_w": jnp.stack(g_cw, axis=0)}
    names = SMALL_REPL + SMALL_SHARD
    _, total = _gather8(_pack([loss[0, :1]] + [small_g[nm] for nm in names]), True, "reduce_small")
    parts = _unpack(total.reshape(-1), [(1,)] + [small_g[nm].shape for nm in names])
    loss_out = parts[0][0]
    for nm, g in zip(names, parts[1:]):
        if nm in SMALL_SHARD:
            size = w[nm].shape[-1]
            g = lax.dynamic_slice_in_dim(g, chip * size, size, axis=g.ndim - 1)
        grads[nm] = g.reshape(w[nm].shape)

    packed = [_pack([src[nm] for nm in names]) for src in (w, grads, m, v)]
    _, sd, sm, sv = _adamw(packed[0], packed[1], None, packed[2], packed[3], "adamw_small")
    shapes = [w[nm].shape for nm in names]
    delta = dict(zip(names, _unpack(sd.reshape(-1), shapes)))
    new_m = dict(zip(names, _unpack(sm.reshape(-1), shapes)))
    new_v = dict(zip(names, _unpack(sv.reshape(-1), shapes)))

    late = [gbuf[nm] for nm in BIG[:2]]
    recv = list(_comm_call(_Scatter(late, BIG_AXIS[:2]), late, "scatter_attn")) + recv_early
    sums = []
    for nm, rc in zip(BIG, recv):
        sums.append(_sum4(rc.reshape(4, -1, rc.shape[-1]), f"sum4_{nm}"))
    others = _swap_sibling(sums, "swap_sums")
    for nm, mine, other in zip(BIG, sums, others):
        cols = mine.shape[-1]
        outs = _adamw(w[nm].reshape(-1, cols), mine, other, m[nm].reshape(-1, cols), v[nm].reshape(-1, cols), f"adamw_{nm}")
        grads[nm], delta[nm], new_m[nm], new_v[nm] = (o.reshape(w[nm].shape) for o in outs)

    return loss_out, grad_x, grads, delta, new_m, new_v


def kernel(x, mem, attn_norm, attn_w_qkv, attn_q_gain, attn_k_gain, attn_w_o, pool_norm, pool_w, pool_scale, xattn_norm, mem_norm, xattn_w_q, xattn_w_kv, xattn_w_o, ffn_norm, ffn_w_up, ffn_conv_w, ffn_conv_b, ffn_w_down, final_norm, loss_target, m_attn_norm, m_attn_w_qkv, m_attn_q_gain, m_attn_k_gain, m_attn_w_o, m_pool_norm, m_pool_w, m_pool_scale, m_xattn_norm, m_mem_norm, m_xattn_w_q, m_xattn_w_kv, m_xattn_w_o, m_ffn_norm, m_ffn_w_up, m_ffn_conv_w, m_ffn_conv_b, m_ffn_w_down, m_final_norm, v_attn_norm, v_attn_w_qkv, v_attn_q_gain, v_attn_k_gain, v_attn_w_o, v_pool_norm, v_pool_w, v_pool_scale, v_xattn_norm, v_mem_norm, v_xattn_w_q, v_xattn_w_kv, v_xattn_w_o, v_ffn_norm, v_ffn_w_up, v_ffn_conv_w, v_ffn_conv_b, v_ffn_w_down, v_final_norm):
    given = dict(locals())
    w = {nm: given[nm] for nm in ORDER}
    m = {nm: given["m_" + nm] for nm in ORDER}
    v = {nm: given["v_" + nm] for nm in ORDER}
    seq, d = x.shape[1], x.shape[2]
    loss, grad_x, grads, delta, new_m, new_v = _step(
        x.reshape(seq, d), mem.reshape(mem.shape[1], d), loss_target.reshape(seq, d), w, m, v)
    return (loss, grad_x.reshape(x.shape), *[grads[nm] for nm in ORDER], *[delta[nm] for nm in ORDER],
            *[new_m[nm] for nm in ORDER], *[new_v[nm] for nm in ORDER])
```

```python
import functools
import itertools

import jax
import jax.numpy as jnp
from jax import lax
from jax.experimental import pallas as pl
from jax.experimental.pallas import tpu as pltpu

F32, BF16 = jnp.float32, jnp.bfloat16
EPS = 1e-6
GRID_W = 64
ROPE_THETA = 10000.0
HEAD_DIM = 64
N_HEADS = 16
N_KV = 4
X_HEADS = 4
X_HEAD_DIM = 256
POOL_GROUPS = 4
POOL_GROUP_W = 256
HALO = 8
LANES = 128
ADAM_LR, ADAM_B1, ADAM_B2, ADAM_EPS, ADAM_WD, ADAM_STEP = 0.001, 0.9, 0.999, 1e-08, 0.01, 10
VMEM_LIMIT = 48 * 1024 * 1024
MESH = pl.DeviceIdType.MESH
NEG = -1e30
LOG2E = 1.4426950408889634
FLASH_TQ, FLASH_TK = 256, 2048
ANY = pl.BlockSpec(memory_space=pl.ANY)


def _cp(sem=None):
    return pltpu.CompilerParams(dimension_semantics=sem, vmem_limit_bytes=VMEM_LIMIT)


def _pick(n, cands):
    for c in cands:
        if c <= n and n % c == 0:
            return c
    return n


def _mm(a, b, mode, *, name, out_dtype, tm=None, tn=None, tk=None, n=None, k=None, b_l=None, b_off=(0, 0),
        res=None, scale=None, out_full=None, out_l=None, out_off=(0, 0), alias=None):
    if mode == "tn":
        K, M = a.shape
    else:
        M, K = a.shape
    bs = b.shape[-2:]
    if mode == "nn":
        K = k or K
        N = n or bs[1]
    elif mode == "nt":
        N = n or bs[0]
    else:
        N = n or bs[1]
    wide = (1408, 1024, 512, 256, 128)
    if mode == "tn":
        tm = tm or (M if M <= 1024 else _pick(M, wide))
        tk = tk or _pick(K, (2048, 1024, 512, 256, 128))
    else:
        tm = _pick(M, (tm or 512, 256, 128))
        tk = tk or (K if K <= 2816 else _pick(K, wide))
    tn = tn or (N if N <= 1536 else _pick(N, wide))
    assert M % tm == 0 and N % tn == 0 and K % tk == 0, (name, M, N, K, tm, tn, tk)
    nk = K // tk
    dims = {"nn": ((1,), (0,)), "nt": ((1,), (1,)), "tn": ((0,), (0,))}[mode]

    if mode == "tn":
        a_spec = pl.BlockSpec((tk, tm), lambda i, j, kk: (kk, i))
    else:
        a_spec = pl.BlockSpec((tm, tk), lambda i, j, kk: (i, kk))
    if mode == "nt":
        bb, (d0, d1) = (tn, tk), (b_off[0] // tn, b_off[1] // tk)
        assert b_off[0] % tn == 0 and b_off[1] % tk == 0
        bidx = lambda i, j, kk: (j + d0, kk + d1)
    else:
        bb, (d0, d1) = (tk, tn), (b_off[0] // tk, b_off[1] // tn)
        assert b_off[0] % tk == 0 and b_off[1] % tn == 0
        bidx = lambda i, j, kk: (kk + d0, j + d1)
    if b.ndim == 3:
        b_spec = pl.BlockSpec((None,) + bb, lambda i, j, kk: (b_l,) + bidx(i, j, kk))
    else:
        b_spec = pl.BlockSpec(bb, bidx)
    in_specs, operands = [a_spec, b_spec], [a, b]
    if res is not None:
        in_specs.append(pl.BlockSpec((tm, tn), lambda i, j, kk: (i, j)))
        operands.append(res)
    aliases = {}
    if alias is not None:
        aliases = {len(operands): 0}
        in_specs.append(ANY)
        operands.append(alias)
    if out_full is None:
        out_shape = jax.ShapeDtypeStruct((M, N), out_dtype)
        out_spec = pl.BlockSpec((tm, tn), lambda i, j, kk: (i, j))
    else:
        assert out_off[0] % tm == 0 and out_off[1] % tn == 0
        o0, o1 = out_off[0] // tm, out_off[1] // tn
        out_shape = jax.ShapeDtypeStruct(out_full, out_dtype)
        out_spec = pl.BlockSpec((None, tm, tn), lambda i, j, kk: (out_l, i + o0, j + o1))
    has_res, has_alias = res is not None, alias is not None

    def body(*refs):
        a_ref, b_ref = refs[0], refs[1]
        pos = 2
        res_ref = None
        if has_res:
            res_ref = refs[pos]
            pos += 1
        if has_alias:
            pos += 1
        o_ref, acc_ref = refs[pos], refs[pos + 1]
        kk = pl.program_id(2)
        part = lax.dot_general(a_ref[...].astype(BF16), b_ref[...].astype(BF16), (dims, ((), ())),
                               preferred_element_type=F32)

        def finish(acc):
            if scale is not None:
                acc = acc * scale
            if res_ref is not None:
                acc = acc + res_ref[...]
            o_ref[...] = acc.astype(o_ref.dtype)

        if nk == 1:
            finish(part)
        else:
            @pl.when(kk == 0)
            def _():
                acc_ref[...] = part

            @pl.when(jnp.logical_and(kk > 0, kk < nk - 1))
            def _():
                acc_ref[...] += part

            @pl.when(kk == nk - 1)
            def _():
                finish(acc_ref[...] + part)

    return pl.pallas_call(
        body, out_shape=out_shape, grid=(M // tm, N // tn, nk), in_specs=in_specs, out_specs=out_spec,
        scratch_shapes=[pltpu.VMEM((tm, tn) if nk > 1 else (8, 128), F32)], input_output_aliases=aliases,
        compiler_params=_cp(("parallel", "parallel", "arbitrary")), name=name)(*operands)


def _rms_fwd(x, gain, out_dtype, name):
    rows, d = x.shape
    tr = _pick(rows, (512, 256))

    def body(x_ref, g_ref, o_ref):
        xv = x_ref[...]
        r = lax.rsqrt(jnp.mean(xv * xv, axis=-1, keepdims=True) + EPS)
        o_ref[...] = (xv * r * g_ref[...]).astype(o_ref.dtype)

    return pl.pallas_call(
        body, out_shape=jax.ShapeDtypeStruct((rows, d), out_dtype), grid=(rows // tr,),
        in_specs=[pl.BlockSpec((tr, d), lambda i: (i, 0)), pl.BlockSpec((1, d), lambda i: (0, 0))],
        out_specs=pl.BlockSpec((tr, d), lambda i: (i, 0)), compiler_params=_cp(("parallel",)), name=name)(x, gain)


def _rms_bwd(x, gain, dh, dres, name):
    rows, d = x.shape
    tr = _pick(rows, (512, 256))
    need_dx = dres is not None

    def body(*refs):
        if need_dx:
            x_ref, g_ref, dh_ref, dres_ref, o_ref, dg_ref = refs
        else:
            x_ref, g_ref, dh_ref, dg_ref = refs
        i = pl.program_id(0)
        xv = x_ref[...]
        dhv = dh_ref[...].astype(F32)
        r = lax.rsqrt(jnp.mean(xv * xv, axis=-1, keepdims=True) + EPS)
        nv = xv * r
        part = jnp.sum(dhv * nv, axis=0, keepdims=True)

        @pl.when(i == 0)
        def _():
            dg_ref[...] = part

        @pl.when(i > 0)
        def _():
            dg_ref[...] += part

        if need_dx:
            dn = dhv * g_ref[...]
            dx = r * (dn - nv * jnp.mean(dn * nv, axis=-1, keepdims=True))
            o_ref[...] = dres_ref[...] + dx

    row_spec = pl.BlockSpec((tr, d), lambda i: (i, 0))
    vec_spec = pl.BlockSpec((1, d), lambda i: (0, 0))
    if need_dx:
        return pl.pallas_call(
            body, out_shape=(jax.ShapeDtypeStruct((rows, d), F32), jax.ShapeDtypeStruct((1, d), F32)),
            grid=(rows // tr,), in_specs=[row_spec, vec_spec, row_spec, row_spec], out_specs=(row_spec, vec_spec),
            compiler_params=_cp(("arbitrary",)), name=name)(x, gain, dh, dres)
    return None, pl.pallas_call(
        body, out_shape=jax.ShapeDtypeStruct((1, d), F32), grid=(rows // tr,),
        in_specs=[row_spec, vec_spec, row_spec], out_specs=vec_spec,
        compiler_params=_cp(("arbitrary",)), name=name)(x, gain, dh)


def _final_loss(x, gain, target, name):
    rows, d = x.shape
    tr = _pick(rows, (512, 256))
    nsteps = rows // tr

    def body(x_ref, g_ref, t_ref, dx_ref, dg_ref, loss_ref, acc_ref):
        i = pl.program_id(0)
        xv = x_ref[...]
        g = g_ref[...]
        r = lax.rsqrt(jnp.mean(xv * xv, axis=-1, keepdims=True) + EPS)
        nv = xv * r
        err = nv * g - t_ref[...]
        dy = err * (1.0 / d)
        dn = dy * g
        dx_ref[...] = r * (dn - nv * jnp.mean(dn * nv, axis=-1, keepdims=True))
        dgp = jnp.sum(dy * nv, axis=0, keepdims=True)
        lp = jnp.sum(err * err, axis=0, keepdims=True)

        @pl.when(i == 0)
        def _():
            dg_ref[...] = dgp
            acc_ref[...] = lp

        @pl.when(i > 0)
        def _():
            dg_ref[...] += dgp
            acc_ref[...] += lp

        @pl.when(i == nsteps - 1)
        def _():
            tot = jnp.sum(acc_ref[...], axis=1, keepdims=True) * (0.5 / d)
            loss_ref[...] = jnp.broadcast_to(tot, loss_ref.shape)

    row_spec = pl.BlockSpec((tr, d), lambda i: (i, 0))
    vec_spec = pl.BlockSpec((1, d), lambda i: (0, 0))
    return pl.pallas_call(
        body, out_shape=(jax.ShapeDtypeStruct((rows, d), F32), jax.ShapeDtypeStruct((1, d), F32),
                         jax.ShapeDtypeStruct((1, LANES), F32)),
        grid=(nsteps,), in_specs=[row_spec, vec_spec, row_spec],
        out_specs=(row_spec, vec_spec, pl.BlockSpec((1, LANES), lambda i: (0, 0))),
        scratch_shapes=[pltpu.VMEM((1, d), F32)], compiler_params=_cp(("arbitrary",)), name=name)(x, gain, target)


def _rope_tables(seq):
    n_rows = seq // GRID_W
    row = jnp.repeat(jnp.arange(n_rows, dtype=F32), GRID_W)
    col = jnp.tile(jnp.arange(GRID_W, dtype=F32), n_rows)
    pairs = HEAD_DIM // 4
    inv_freq = ROPE_THETA ** (-jnp.arange(pairs, dtype=F32) / pairs)
    ang = jnp.stack([row[:, None] * inv_freq, col[:, None] * inv_freq], axis=1)
    cos, sin = jnp.cos(ang), jnp.sin(ang)
    zero = jnp.zeros_like(sin[:, 0])
    c64 = jnp.concatenate([cos[:, 0], cos[:, 0], cos[:, 1], cos[:, 1]], axis=1)
    sp64 = jnp.concatenate([zero, sin[:, 0], zero, sin[:, 1]], axis=1)
    sm64 = jnp.concatenate([-sin[:, 0], zero, -sin[:, 1], zero], axis=1)
    return tuple(jnp.concatenate([t, t], axis=1) for t in (c64, sp64, sm64))


def _pair_norm(xv, lo):
    sq = xv * xv
    s_lo = jnp.sum(jnp.where(lo, sq, 0.0), axis=1, keepdims=True)
    s_hi = jnp.sum(jnp.where(lo, 0.0, sq), axis=1, keepdims=True)
    return lax.rsqrt(jnp.where(lo, s_lo, s_hi) * (1.0 / HEAD_DIM) + EPS)


def _rope(y, c, sp, sm):
    return y * c + pltpu.roll(y, 16, axis=1) * sp + pltpu.roll(y, LANES - 16, axis=1) * sm


def _rope_t(dz, c, sp, sm):
    return dz * c + pltpu.roll(dz * sp, LANES - 16, axis=1) + pltpu.roll(dz * sm, 16, axis=1)


def _qk_prep(qkv, qg2, kg2, tabs, name):
    seq = qkv.shape[0]
    ts = _pick(seq, (256, 128))
    nq, nkp = N_HEADS // 2, N_KV // 2
    qw, kw = N_HEADS * HEAD_DIM, N_KV * HEAD_DIM

    def body(x_ref, qg_ref, kg_ref, c_ref, sp_ref, sm_ref, q_ref, k_ref, kt_ref, v_ref, vt_ref):
        lo = lax.broadcasted_iota(jnp.int32, (ts, LANES), 1) < HEAD_DIM
        top = lax.broadcasted_iota(jnp.int32, (LANES, ts), 0) < HEAD_DIM
        c, sp, sm = c_ref[...], sp_ref[...], sm_ref[...]
        for i in range(nq):
            xv = x_ref[:, i * LANES:(i + 1) * LANES]
            y = xv * _pair_norm(xv, lo) * qg_ref[...]
            q_ref[:, i * LANES:(i + 1) * LANES] = (_rope(y, c, sp, sm) * (LOG2E * HEAD_DIM ** -0.5)).astype(BF16)
        for i in range(nkp):
            xv = x_ref[:, qw + i * LANES:qw + (i + 1) * LANES]
            z = _rope(xv * _pair_norm(xv, lo) * kg_ref[...], c, sp, sm)
            k_ref[:, i * LANES:(i + 1) * LANES] = z.astype(BF16)
            kt_ref[i * LANES:(i + 1) * LANES, :] = z.T.astype(BF16)
            vv = x_ref[:, qw + kw + i * LANES:qw + kw + (i + 1) * LANES]
            v_ref[:, i * LANES:(i + 1) * LANES] = vv.astype(BF16)
            vvt = vv.T
            vt_ref[(2 * i) * LANES:(2 * i + 1) * LANES, :] = jnp.where(top, vvt, 1.0).astype(BF16)
            vt_ref[(2 * i + 1) * LANES:(2 * i + 2) * LANES, :] = jnp.where(top, 1.0, vvt).astype(BF16)

    tab = pl.BlockSpec((ts, LANES), lambda i: (i, 0))
    vec = pl.BlockSpec((1, LANES), lambda i: (0, 0))
    return pl.pallas_call(
        body,
        out_shape=(jax.ShapeDtypeStruct((seq, qw), BF16), jax.ShapeDtypeStruct((seq, kw), BF16),
                   jax.ShapeDtypeStruct((kw, seq), BF16), jax.ShapeDtypeStruct((seq, kw), BF16),
                   jax.ShapeDtypeStruct((N_KV * LANES, seq), BF16)),
        grid=(seq // ts,),
        in_specs=[pl.BlockSpec((ts, qw + 2 * kw), lambda i: (i, 0)), vec, vec, tab, tab, tab],
        out_specs=(pl.BlockSpec((ts, qw), lambda i: (i, 0)), pl.BlockSpec((ts, kw), lambda i: (i, 0)),
                   pl.BlockSpec((kw, ts), lambda i: (0, i)), pl.BlockSpec((ts, kw), lambda i: (i, 0)),
                   pl.BlockSpec((N_KV * LANES, ts), lambda i: (0, i))),
        compiler_params=_cp(("parallel",)), name=name)(qkv, qg2, kg2, *tabs)


def _qk_prep_bwd(qkv, dq, dk, dv, qg2, kg2, tabs, name):
    seq = qkv.shape[0]
    ts = _pick(seq, (256, 128))
    nq, nkp = N_HEADS // 2, N_KV // 2
    qw, kw = N_HEADS * HEAD_DIM, N_KV * HEAD_DIM

    def body(x_ref, dq_ref, dk_ref, dv_ref, qg_ref, kg_ref, c_ref, sp_ref, sm_ref, o_ref, dqg_ref, dkg_ref):
        step = pl.program_id(0)
        lo = lax.broadcasted_iota(jnp.int32, (ts, LANES), 1) < HEAD_DIM
        c, sp, sm = c_ref[...], sp_ref[...], sm_ref[...]

        def one(xv, dz, gain):
            r = _pair_norm(xv, lo)
            nv = xv * r
            dy = _rope_t(dz, c, sp, sm)
            dgp = jnp.sum(dy * nv, axis=0, keepdims=True)
            dn = dy * gain
            t = dn * nv
            m_lo = jnp.sum(jnp.where(lo, t, 0.0), axis=1, keepdims=True)
            m_hi = jnp.sum(jnp.where(lo, 0.0, t), axis=1, keepdims=True)
            m = jnp.where(lo, m_lo, m_hi) * (1.0 / HEAD_DIM)
            return r * (dn - nv * m), dgp

        dqg = jnp.zeros((1, LANES), F32)
        for i in range(nq):
            sl = slice(i * LANES, (i + 1) * LANES)
            dx, dgp = one(x_ref[:, sl], dq_ref[:, sl] * (HEAD_DIM ** -0.5), qg_ref[...])
            o_ref[:, sl] = dx.astype(BF16)
            dqg = dqg + dgp
        dkg = jnp.zeros((1, LANES), F32)
        for i in range(nkp):
            sl = slice(i * LANES, (i + 1) * LANES)
            dx, dgp = one(x_ref[:, qw + i * LANES:qw + (i + 1) * LANES], dk_ref[:, sl], kg_ref[...])
            o_ref[:, qw + i * LANES:qw + (i + 1) * LANES] = dx.astype(BF16)
            dkg = dkg + dgp
            o_ref[:, qw + kw + i * LANES:qw + kw + (i + 1) * LANES] = dv_ref[:, sl].astype(BF16)

        @pl.when(step == 0)
        def _():
            dqg_ref[...] = dqg
            dkg_ref[...] = dkg

        @pl.when(step > 0)
        def _():
            dqg_ref[...] += dqg
            dkg_ref[...] += dkg

    tab = pl.BlockSpec((ts, LANES), lambda i: (i, 0))
    vec = pl.BlockSpec((1, LANES), lambda i: (0, 0))
    return pl.pallas_call(
        body,
        out_shape=(jax.ShapeDtypeStruct((seq, qw + 2 * kw), BF16), jax.ShapeDtypeStruct((1, LANES), F32),
                   jax.ShapeDtypeStruct((1, LANES), F32)),
        grid=(seq // ts,),
        in_specs=[pl.BlockSpec((ts, qw + 2 * kw), lambda i: (i, 0)), pl.BlockSpec((ts, qw), lambda i: (i, 0)),
                  pl.BlockSpec((ts, kw), lambda i: (i, 0)), pl.BlockSpec((ts, kw), lambda i: (i, 0)),
                  vec, vec, tab, tab, tab],
        out_specs=(pl.BlockSpec((ts, qw + 2 * kw), lambda i: (i, 0)), vec, vec),
        compiler_params=_cp(("arbitrary",)), name=name)(qkv, dq, dk, dv, qg2, kg2, *tabs)


def _slot(blk, off0, tq):
    half = lax.broadcasted_iota(jnp.int32, (tq, LANES), 1) // HEAD_DIM
    keep = half == jnp.where(off0, 0, 1)
    parts = []
    for p in range(2):
        pair = blk[:, p * LANES:(p + 1) * LANES].astype(F32)
        rolled = pltpu.roll(pair, HEAD_DIM, axis=1)
        parts.append(jnp.where(keep, jnp.where(off0, pair, rolled), 0.0))
        parts.append(jnp.where(keep, jnp.where(off0, rolled, pair), 0.0))
    return jnp.concatenate(parts, axis=0)


def _unslot(x4, off0, tq):
    lo = lax.broadcasted_iota(jnp.int32, (tq, LANES), 1) < HEAD_DIM
    pairs = []
    for p in range(2):
        h0 = x4[(2 * p) * tq:(2 * p + 1) * tq]
        h1 = x4[(2 * p + 1) * tq:(2 * p + 2) * tq]
        a = jnp.where(off0, h0, pltpu.roll(h0, HEAD_DIM, axis=1))
        b = jnp.where(off0, pltpu.roll(h1, HEAD_DIM, axis=1), h1)
        pairs.append(jnp.where(lo, a, b))
    return jnp.concatenate(pairs, axis=1)


def _flash_fwd(q, k, vt, plan, shards, name):
    seq = q.shape[0]
    tq = _pick(seq, (FLASH_TQ, 128))
    tk = _pick(seq, (FLASH_TK, 512, 256, 128))
    nq, nkv = seq // tq, seq // tk
    gw = 4 * HEAD_DIM
    nt = plan.nt

    def body(q_ref, k_ref, vt_ref, *rest):
        o_ref, lse_ref = rest[nt:nt + 2]
        q4_ref, m_ref, acc_ref = rest[2 * nt + 2:2 * nt + 5]
        plan.bind(rest[:nt], rest[nt + 2:2 * nt + 2], *rest[2 * nt + 5:])
        g, qi, ki = pl.program_id(0), pl.program_id(1), pl.program_id(2)
        off0 = (g % 2) == 0
        inner0 = jnp.logical_and(qi == 0, ki == 0)

        @pl.when(jnp.logical_and(g == 0, inner0))
        def _():
            plan.start()

        @pl.when(jnp.logical_and(g == N_KV - 1, inner0))
        def _():
            plan.forward()

        @pl.when(ki == 0)
        def _():
            q4_ref[...] = _slot(q_ref[...], off0, tq).astype(BF16)
            m_ref[...] = jnp.full(m_ref.shape, NEG, F32)
            acc_ref[...] = jnp.zeros(acc_ref.shape, F32)

        st = lax.dot_general(k_ref[...], q4_ref[...], (((1,), (1,)), ((), ())), preferred_element_type=F32)
        m_old = m_ref[...]
        m_new = jnp.maximum(m_old, jnp.max(st, axis=0, keepdims=True))
        pt = jnp.exp2(st - m_new).astype(BF16)
        acc_ref[...] = jnp.exp2(m_old - m_new) * acc_ref[...] + jnp.dot(vt_ref[...], pt, preferred_element_type=F32)
        m_ref[...] = m_new

        @pl.when(ki == nkv - 1)
        def _():
            acc = acc_ref[...]
            l = jnp.where(off0, acc[HEAD_DIM:HEAD_DIM + 1], acc[0:1])
            o4 = acc.T
            o4 = o4 / pltpu.roll(o4, HEAD_DIM, axis=1)
            o_ref[...] = _unslot(o4, off0, tq).astype(o_ref.dtype)
            lse_ref[...] = jnp.broadcast_to(m_ref[...] + jnp.log2(l), lse_ref.shape)

        @pl.when(jnp.logical_and(g == N_KV - 1, jnp.logical_and(qi == nq - 1, ki == nkv - 1)))
        def _():
            plan.finish()

    return pl.pallas_call(
        body,
        out_shape=(jax.ShapeDtypeStruct((seq, N_HEADS * HEAD_DIM), BF16),
                   jax.ShapeDtypeStruct((N_KV * nq * 8, 4 * tq), F32), *plan.out_shape),
        grid=(N_KV, nq, nkv),
        in_specs=[pl.BlockSpec((tq, gw), lambda g, qi, ki: (qi, g)),
                  pl.BlockSpec((tk, LANES), lambda g, qi, ki: (ki, g // 2)),
                  pl.BlockSpec((LANES, tk), lambda g, qi, ki: (g, ki))] + [ANY] * nt,
        out_specs=(pl.BlockSpec((tq, gw), lambda g, qi, ki: (qi, g)),
                   pl.BlockSpec((8, 4 * tq), lambda g, qi, ki: (g * nq + qi, 0)), *([ANY] * nt)),
        scratch_shapes=[pltpu.VMEM((4 * tq, LANES), BF16), pltpu.VMEM((1, 4 * tq), F32),
                        pltpu.VMEM((LANES, 4 * tq), F32)] + plan.scratch,
        compiler_params=_cp(("arbitrary", "arbitrary", "arbitrary")), name=name)(q, k, vt, *shards)


def _flash_bwd(q, k, kt, v, do, o, lse, plan, grads, name):
    seq = q.shape[0]
    tq = _pick(seq, (FLASH_TQ, 128))
    tk = _pick(seq, (FLASH_TK, 512, 256, 128))
    nq, nkv = seq // tq, seq // tk
    gw = 4 * HEAD_DIM
    nt = plan.nt

    def body(q_ref, k_ref, kt_ref, v_ref, do_ref, o_ref, lse_ref, *rest):
        dq_ref, dk_ref, dv_ref = rest[nt:nt + 3]
        q4_ref, do4_ref, delta_ref, dqt_ref = rest[2 * nt + 3:2 * nt + 7]
        plan.bind(rest[:nt], rest[nt + 3:2 * nt + 3], *rest[2 * nt + 7:])
        g, qi, ki = pl.program_id(0), pl.program_id(1), pl.program_id(2)
        off0 = (g % 2) == 0

        @pl.when(jnp.logical_and(g == 0, jnp.logical_and(qi == 0, ki == 0)))
        def _():
            plan.start()

        @pl.when(jnp.logical_and(g % 2 == 0, jnp.logical_and(qi == 0, ki == 0)))
        def _():
            dk_ref[...] = jnp.zeros(dk_ref.shape, F32)
            dv_ref[...] = jnp.zeros(dv_ref.shape, F32)

        @pl.when(ki == 0)
        def _():
            q4_ref[...] = _slot(q_ref[...], off0, tq).astype(BF16)
            do4 = _slot(do_ref[...], off0, tq)
            do4_ref[...] = do4.astype(BF16)
            o4 = _slot(o_ref[...], off0, tq)
            delta_ref[...] = jnp.sum((do4 * o4).T, axis=0, keepdims=True)
            dqt_ref[...] = jnp.zeros(dqt_ref.shape, F32)

        q4, do4 = q4_ref[...], do4_ref[...]
        st = lax.dot_general(k_ref[...], q4, (((1,), (1,)), ((), ())), preferred_element_type=F32)
        pt = jnp.exp2(st - lse_ref[0:1, :])
        dpt = lax.dot_general(v_ref[...], do4, (((1,), (1,)), ((), ())), preferred_element_type=F32)
        dst = (pt * (dpt - delta_ref[...])).astype(BF16)
        rows = pl.ds(pl.multiple_of(ki * tk, tk), tk)
        dv_ref[rows, :] += jnp.dot(pt.astype(BF16), do4, preferred_element_type=F32)
        dk_ref[rows, :] += jnp.dot(dst, q4, preferred_element_type=F32) * (1.0 / LOG2E)
        dqt_ref[...] += jnp.dot(kt_ref[...], dst, preferred_element_type=F32)

        @pl.when(ki == nkv - 1)
        def _():
            dq_ref[...] = _unslot(dqt_ref[...].T, off0, tq)

        @pl.when(jnp.logical_and(g == N_KV - 1, jnp.logical_and(qi == nq - 1, ki == nkv - 1)))
        def _():
            plan.finish()

    return pl.pallas_call(
        body,
        out_shape=(jax.ShapeDtypeStruct((seq, N_HEADS * HEAD_DIM), F32),
                   jax.ShapeDtypeStruct((seq, N_KV * HEAD_DIM), F32), jax.ShapeDtypeStruct((seq, N_KV * HEAD_DIM), F32),
                   *plan.out_shape),
        grid=(N_KV, nq, nkv),
        in_specs=[pl.BlockSpec((tq, gw), lambda g, qi, ki: (qi, g)),
                  pl.BlockSpec((tk, LANES), lambda g, qi, ki: (ki, g // 2)),
                  pl.BlockSpec((LANES, tk), lambda g, qi, ki: (g // 2, ki)),
                  pl.BlockSpec((tk, LANES), lambda g, qi, ki: (ki, g // 2)),
                  pl.BlockSpec((tq, gw), lambda g, qi, ki: (qi, g)),
                  pl.BlockSpec((tq, gw), lambda g, qi, ki: (qi, g)),
                  pl.BlockSpec((8, 4 * tq), lambda g, qi, ki: (g * nq + qi, 0))] + [ANY] * nt,
        out_specs=(pl.BlockSpec((tq, gw), lambda g, qi, ki: (qi, g)),
                   pl.BlockSpec((seq, LANES), lambda g, qi, ki: (0, g // 2)),
                   pl.BlockSpec((seq, LANES), lambda g, qi, ki: (0, g // 2)), *([ANY] * nt)),
        scratch_shapes=[pltpu.VMEM((4 * tq, LANES), BF16), pltpu.VMEM((4 * tq, LANES), BF16),
                        pltpu.VMEM((1, 4 * tq), F32), pltpu.VMEM((LANES, 4 * tq), F32)] + plan.scratch,
        compiler_params=_cp(("arbitrary", "arbitrary", "arbitrary")), name=name)(q, k, kt, v, do, o, lse, *grads)


def _xattn_fwd(q, kv, name):
    seq, d = q.shape
    mlen = kv.shape[0]
    tq = _pick(seq, (512, 256))

    def body(q_ref, k_ref, v_ref, o_ref):
        for h in range(X_HEADS):
            sl = slice(h * X_HEAD_DIM, (h + 1) * X_HEAD_DIM)
            s = lax.dot_general(q_ref[:, sl], k_ref[:, sl], (((1,), (1,)), ((), ())), preferred_element_type=F32)
            e = jnp.exp(s - jnp.max(s, axis=-1, keepdims=True))
            p = e / jnp.sum(e, axis=-1, keepdims=True)
            o_ref[:, sl] = jnp.dot(p.astype(BF16), v_ref[:, sl], preferred_element_type=F32).astype(o_ref.dtype)

    return pl.pallas_call(
        body, out_shape=jax.ShapeDtypeStruct((seq, d), BF16), grid=(seq // tq,),
        in_specs=[pl.BlockSpec((tq, d), lambda i: (i, 0)), pl.BlockSpec((mlen, d), lambda i: (0, 0)),
                  pl.BlockSpec((mlen, d), lambda i: (0, 1))],
        out_specs=pl.BlockSpec((tq, d), lambda i: (i, 0)), compiler_params=_cp(("parallel",)), name=name)(q, kv, kv)


def _xattn_bwd(q, kv, do, name):
    seq, d = q.shape
    mlen = kv.shape[0]
    tq = _pick(seq, (512, 256))
    scale = X_HEAD_DIM ** -0.5

    def body(q_ref, k_ref, v_ref, do_ref, dq_ref, dkv_ref):
        i = pl.program_id(0)

        @pl.when(i == 0)
        def _():
            dkv_ref[...] = jnp.zeros(dkv_ref.shape, F32)

        for h in range(X_HEADS):
            sl = slice(h * X_HEAD_DIM, (h + 1) * X_HEAD_DIM)
            qh, kh, vh = q_ref[:, sl], k_ref[:, sl], v_ref[:, sl]
            doh = do_ref[:, sl].astype(BF16)
            st = lax.dot_general(kh, qh, (((1,), (1,)), ((), ())), preferred_element_type=F32)
            e = jnp.exp(st - jnp.max(st, axis=0, keepdims=True))
            pt = e / jnp.sum(e, axis=0, keepdims=True)
            dpt = lax.dot_general(vh, doh, (((1,), (1,)), ((), ())), preferred_element_type=F32)
            dst = (pt * (dpt - jnp.sum(pt * dpt, axis=0, keepdims=True))).astype(BF16)
            dkv_ref[:, sl] += jnp.dot(dst, qh, preferred_element_type=F32)
            dkv_ref[:, d + h * X_HEAD_DIM:d + (h + 1) * X_HEAD_DIM] += jnp.dot(pt.astype(BF16), doh,
                                                                                 preferred_element_type=F32)
            dqh = lax.dot_general(dst, kh, (((0,), (0,)), ((), ())), preferred_element_type=F32)
            dq_ref[:, sl] = (dqh * scale).astype(dq_ref.dtype)

    return pl.pallas_call(
        body, out_shape=(jax.ShapeDtypeStruct((seq, d), BF16), jax.ShapeDtypeStruct((mlen, 2 * d), F32)),
        grid=(seq // tq,),
        in_specs=[pl.BlockSpec((tq, d), lambda i: (i, 0)), pl.BlockSpec((mlen, d), lambda i: (0, 0)),
                  pl.BlockSpec((mlen, d), lambda i: (0, 1)), pl.BlockSpec((tq, d), lambda i: (i, 0))],
        out_specs=(pl.BlockSpec((tq, d), lambda i: (i, 0)), pl.BlockSpec((mlen, 2 * d), lambda i: (0, 0))),
        compiler_params=_cp(("arbitrary",)), name=name)(q, kv, kv, do)


def _halo_specs(tr, tc, seq, col):
    per, last = tr // HALO, seq // HALO - 1
    return [pl.BlockSpec((tr, tc), lambda j, r: (r, col(j))),
            pl.BlockSpec((HALO, tc), lambda j, r: (jnp.maximum(r * per - 1, 0), col(j))),
            pl.BlockSpec((HALO, tc), lambda j, r: (jnp.minimum((r + 1) * per, last), col(j)))]


def _extend(main_ref, prev_ref, next_ref, r, nr):
    pv = (r > 0).astype(F32)
    nv = (r < nr - 1).astype(F32)
    return jnp.concatenate([prev_ref[...].astype(F32) * pv, main_ref[...].astype(F32),
                            next_ref[...].astype(F32) * nv], axis=0)


def _conv3(e, w_ref, n):
    return pltpu.roll(e, 1, axis=0) * w_ref[0:1, :] + e * w_ref[1:2, :] + pltpu.roll(e, n - 1, axis=0) * w_ref[2:3, :]


def _conv_gate_fwd(ug, uv, cw, cb, layer, name):
    seq, f = ug.shape
    tc = 256
    tr = _pick(seq, (512, 256))
    nc, nr = f // tc, seq // tr
    n = tr + 2 * HALO

    def body(g_ref, gp_ref, gn_ref, v_ref, vp_ref, vn_ref, wg_ref, wv_ref, bg_ref, bv_ref, o_ref):
        r = pl.program_id(1)
        cg = _conv3(_extend(g_ref, gp_ref, gn_ref, r, nr), wg_ref, n)[HALO:HALO + tr] + bg_ref[...]
        cv = _conv3(_extend(v_ref, vp_ref, vn_ref, r, nr), wv_ref, n)[HALO:HALO + tr] + bv_ref[...]
        o_ref[...] = (cg * jax.nn.sigmoid(cg) * cv).astype(o_ref.dtype)

    w_spec = lambda shift: pl.BlockSpec((None, 3, tc), lambda j, r: (layer, 0, j + shift))
    b_spec = lambda shift: pl.BlockSpec((None, 1, tc), lambda j, r: (layer, 0, j + shift))
    return pl.pallas_call(
        body, out_shape=jax.ShapeDtypeStruct((seq, f), BF16), grid=(nc, nr),
        in_specs=_halo_specs(tr, tc, seq, lambda j: j) * 2 + [w_spec(0), w_spec(nc), b_spec(0), b_spec(nc)],
        out_specs=pl.BlockSpec((tr, tc), lambda j, r: (r, j)),
        compiler_params=_cp(("parallel", "parallel")), name=name)(ug, ug, ug, uv, uv, uv, cw, cw, cb, cb)


def _conv_gate_bwd(ug, uv, dact, cw, cb, layer, name):
    seq, f = ug.shape
    tc = 256
    tr = _pick(seq, (512, 256))
    nc, nr = f // tc, seq // tr
    n = tr + 2 * HALO

    def body(g_ref, gp_ref, gn_ref, v_ref, vp_ref, vn_ref, d_ref, dp_ref, dn_ref, wg_ref, wv_ref, bg_ref, bv_ref,
             dug_ref, duv_ref, dwg_ref, dwv_ref):
        r = pl.program_id(1)
        eg = _extend(g_ref, gp_ref, gn_ref, r, nr)
        ev = _extend(v_ref, vp_ref, vn_ref, r, nr)
        da = _extend(d_ref, dp_ref, dn_ref, r, nr)
        cg = _conv3(eg, wg_ref, n) + bg_ref[...]
        cv = _conv3(ev, wv_ref, n) + bv_ref[...]
        sg = jax.nn.sigmoid(cg)
        dcv = da * (cg * sg)
        dcg = da * cv * (sg * (1.0 + cg * (1.0 - sg)))

        def back(dc, e, w_ref, du_ref, dw_ref):
            du = (pltpu.roll(dc, n - 1, axis=0) * w_ref[0:1, :] + dc * w_ref[1:2, :]
                  + pltpu.roll(dc, 1, axis=0) * w_ref[2:3, :])
            du_ref[...] = du[HALO:HALO + tr].astype(du_ref.dtype)
            dcm = dc[HALO:HALO + tr]
            taps = [jnp.sum(dcm * pltpu.roll(e, 1, axis=0)[HALO:HALO + tr], axis=0, keepdims=True),
                    jnp.sum(dcm * e[HALO:HALO + tr], axis=0, keepdims=True),
                    jnp.sum(dcm * pltpu.roll(e, n - 1, axis=0)[HALO:HALO + tr], axis=0, keepdims=True),
                    jnp.sum(dcm, axis=0, keepdims=True)]
            part = jnp.concatenate(taps + [jnp.zeros((4, tc), F32)], axis=0)

            @pl.when(r == 0)
            def _():
                dw_ref[...] = part

            @pl.when(r > 0)
            def _():
                dw_ref[...] += part

        back(dcg, eg, wg_ref, dug_ref, dwg_ref)
        back(dcv, ev, wv_ref, duv_ref, dwv_ref)

    w_spec = lambda shift: pl.BlockSpec((None, 3, tc), lambda j, r: (layer, 0, j + shift))
    b_spec = lambda shift: pl.BlockSpec((None, 1, tc), lambda j, r: (layer, 0, j + shift))
    out_rows = pl.BlockSpec((tr, tc), lambda j, r: (r, j))
    out_acc = pl.BlockSpec((8, tc), lambda j, r: (0, j))
    return pl.pallas_call(
        body,
        out_shape=(jax.ShapeDtypeStruct((seq, f), BF16), jax.ShapeDtypeStruct((seq, f), BF16),
                   jax.ShapeDtypeStruct((8, f), F32), jax.ShapeDtypeStruct((8, f), F32)),
        grid=(nc, nr),
        in_specs=_halo_specs(tr, tc, seq, lambda j: j) * 3 + [w_spec(0), w_spec(nc), b_spec(0), b_spec(nc)],
        out_specs=(out_rows, out_rows, out_acc, out_acc),
        compiler_params=_cp(("parallel", "arbitrary")), name=name)(ug, ug, ug, uv, uv, uv, dact, dact, dact, cw, cw, cb, cb)


def _pool_count(g, r, tr, n, seq):
    half = jnp.left_shift(1, g)
    t = r * tr - HALO + lax.broadcasted_iota(jnp.int32, (n, 1), 0)
    cnt = jnp.minimum(t + half, seq) - jnp.maximum(t - half, 0)
    return jnp.maximum(cnt, 1).astype(F32)


def _by_group(g, levels):
    out = levels[3]
    for i in (2, 1, 0):
        out = jnp.where(g == i, levels[i], out)
    return out


def _pool_mixed(e, g, cnt, n):
    w2 = e + pltpu.roll(e, 1, axis=0)
    w4 = pltpu.roll(w2, 1, axis=0) + pltpu.roll(w2, n - 1, axis=0)
    w8 = pltpu.roll(w4, 2, axis=0) + pltpu.roll(w4, n - 2, axis=0)
    w16 = pltpu.roll(w8, 4, axis=0) + pltpu.roll(w8, n - 4, axis=0)
    return _by_group(g, (w2, w4, w8, w16)) / cnt - e


def _pool_fwd(hp, xres, pw, scale, name):
    seq, d = hp.shape
    tc = POOL_GROUP_W
    tr = _pick(seq, (512, 256))
    nr = seq // tr
    n = tr + 2 * HALO

    def body(h_ref, hp_ref, hn_ref, x_ref, w_ref, s_ref, o_ref):
        g, r = pl.program_id(0), pl.program_id(1)
        e = _extend(h_ref, hp_ref, hn_ref, r, nr)
        mixed = _pool_mixed(e, g, _pool_count(g, r, tr, n, seq), n)[HALO:HALO + tr]
        y = jnp.dot(mixed.astype(BF16), w_ref[...], preferred_element_type=F32)
        o_ref[...] = x_ref[...] + y * s_ref[...]

    return pl.pallas_call(
        body, out_shape=jax.ShapeDtypeStruct((seq, d), F32), grid=(POOL_GROUPS, nr),
        in_specs=_halo_specs(tr, tc, seq, lambda j: j) + [
            pl.BlockSpec((tr, tc), lambda j, r: (r, j)), pl.BlockSpec((None, tc, tc), lambda j, r: (j, 0, 0)),
            pl.BlockSpec((1, tc), lambda j, r: (0, j))],
        out_specs=pl.BlockSpec((tr, tc), lambda j, r: (r, j)),
        compiler_params=_cp(("parallel", "parallel")), name=name)(hp, hp, hp, xres, pw, scale)


def _pool_bwd(hp, dy, pw, scale, name):
    seq, d = hp.shape
    tc = POOL_GROUP_W
    tr = _pick(seq, (512, 256))
    nr = seq // tr
    n = tr + 2 * HALO

    def body(h_ref, hp_ref, hn_ref, d_ref, dp_ref, dn_ref, w_ref, s_ref, dh_ref, dw_ref, ds_ref):
        g, r = pl.program_id(0), pl.program_id(1)
        cnt = _pool_count(g, r, tr, n, seq)
        e = _extend(h_ref, hp_ref, hn_ref, r, nr)
        mixed = _pool_mixed(e, g, cnt, n)[HALO:HALO + tr].astype(BF16)
        dye = _extend(d_ref, dp_ref, dn_ref, r, nr)
        dyp = (dye * s_ref[...]).astype(BF16)
        dmixed = lax.dot_general(dyp, w_ref[...], (((1,), (1,)), ((), ())), preferred_element_type=F32)
        dwin = dmixed / cnt
        m2 = dwin + pltpu.roll(dwin, n - 1, axis=0)
        m4 = pltpu.roll(m2, 1, axis=0) + pltpu.roll(m2, n - 1, axis=0)
        m8 = pltpu.roll(m4, 2, axis=0) + pltpu.roll(m4, n - 2, axis=0)
        m16 = pltpu.roll(m8, 4, axis=0) + pltpu.roll(m8, n - 4, axis=0)
        dh_ref[...] = (_by_group(g, (m2, m4, m8, m16)) - dmixed)[HALO:HALO + tr]
        ypre = jnp.dot(mixed, w_ref[...], preferred_element_type=F32)
        dsp = jnp.sum(d_ref[...] * ypre, axis=0, keepdims=True)
        dwp = lax.dot_general(mixed, dyp[HALO:HALO + tr], (((0,), (0,)), ((), ())), preferred_element_type=F32)

        @pl.when(r == 0)
        def _():
            dw_ref[...] = dwp
            ds_ref[...] = dsp

        @pl.when(r > 0)
        def _():
            dw_ref[...] += dwp
            ds_ref[...] += dsp

    return pl.pallas_call(
        body,
        out_shape=(jax.ShapeDtypeStruct((seq, d), F32), jax.ShapeDtypeStruct((POOL_GROUPS, tc, tc), F32),
                   jax.ShapeDtypeStruct((1, d), F32)),
        grid=(POOL_GROUPS, nr),
        in_specs=_halo_specs(tr, tc, seq, lambda j: j) * 2 + [
            pl.BlockSpec((None, tc, tc), lambda j, r: (j, 0, 0)), pl.BlockSpec((1, tc), lambda j, r: (0, j))],
        out_specs=(pl.BlockSpec((tr, tc), lambda j, r: (r, j)), pl.BlockSpec((None, tc, tc), lambda j, r: (j, 0, 0)),
                   pl.BlockSpec((1, tc), lambda j, r: (0, j))),
        compiler_params=_cp(("parallel", "arbitrary")), name=name)(hp, hp, hp, dy, dy, dy, pw, scale)


def _adamw_math(w, g, m, v):
    m = ADAM_B1 * m + (1.0 - ADAM_B1) * g
    v = ADAM_B2 * v + (1.0 - ADAM_B2) * (g * g)
    m_hat = m / (1.0 - ADAM_B1 ** ADAM_STEP)
    v_hat = v / (1.0 - ADAM_B2 ** ADAM_STEP)
    delta = -ADAM_LR * (m_hat / (jnp.sqrt(v_hat) + ADAM_EPS) + ADAM_WD * w)
    return delta, m, v


def _adamw(w, ga, gb, m, v, name):
    rows, cols = w.shape
    tr = _pick(rows, (256, 128, 64, 32, 16, 8))
    two = gb is not None

    def body(*refs):
        if two:
            w_ref, ga_ref, gb_ref, m_ref, v_ref, g_out, d_out, m_out, v_out = refs
            g = ga_ref[...] + gb_ref[...]
        else:
            w_ref, ga_ref, m_ref, v_ref, g_out, d_out, m_out, v_out = refs
            g = ga_ref[...]
        delta, m, v = _adamw_math(w_ref[...], g, m_ref[...], v_ref[...])
        g_out[...] = g
        d_out[...] = delta
        m_out[...] = m
        v_out[...] = v

    spec = pl.BlockSpec((tr, cols), lambda i: (i, 0))
    ops = [w, ga] + ([gb] if two else []) + [m, v]
    return pl.pallas_call(
        body, out_shape=tuple(jax.ShapeDtypeStruct((rows, cols), F32) for _ in range(4)), grid=(rows // tr,),
        in_specs=[spec] * len(ops), out_specs=(spec,) * 4, compiler_params=_cp(("parallel",)), name=name)(*ops)


def _sum4(parts, name):
    _, rows, cols = parts.shape
    tr = _pick(rows, (256, 128, 64, 32, 16))

    def body(p_ref, o_ref):
        acc = p_ref[0].astype(F32)
        for kk in range(1, 4):
            acc = acc + p_ref[kk].astype(F32)
        o_ref[...] = acc

    return pl.pallas_call(
        body, out_shape=jax.ShapeDtypeStruct((rows, cols), F32), grid=(rows // tr,),
        in_specs=[pl.BlockSpec((4, tr, cols), lambda i: (0, i, 0))], out_specs=pl.BlockSpec((tr, cols), lambda i: (i, 0)),
        compiler_params=_cp(("parallel",)), name=name)(parts)


def _place():
    x, y, c = lax.axis_index("x"), lax.axis_index("y"), lax.axis_index("c")
    chips = [(1 - x, y), (x, 1 - y), (1 - x, 1 - y)]
    return x, y, c, chips


def _window(ref, axis, j, size, c=None, half=None, lead=()):
    if axis == "r":
        if c is None:
            return ref.at[lead + (slice(None), pl.ds(pl.multiple_of(j * size, 32), size), slice(None))]
        return ref.at[lead + (slice(None), pl.ds(pl.multiple_of(j * size + c * half, 32), half), slice(None))]
    cols = pl.ds(pl.multiple_of(j * size, LANES), size)
    if c is None:
        return ref.at[lead + (slice(None), slice(None), cols)]
    return ref.at[lead + (slice(None), pl.ds(pl.multiple_of(c * half, 32), half), cols)]


class _Gather:
    def __init__(self, shards, axes):
        self.nt, self.axes = len(shards), axes
        self.out_shape, self.sizes, self.halves = [], [], []
        for s, ax in zip(shards, axes):
            l, rs, cs = s.shape
            self.out_shape.append(jax.ShapeDtypeStruct((l, 4 * rs, cs) if ax == "r" else (l, rs, 4 * cs), s.dtype))
            self.sizes.append(rs if ax == "r" else cs)
            self.halves.append(rs // 2)
        self.scratch = [pltpu.SemaphoreType.DMA((6 * self.nt,)), pltpu.SemaphoreType.DMA((6 * self.nt,)),
                        pltpu.SemaphoreType.DMA((self.nt,))]

    def bind(self, src, dst, send_sems, recv_sems, local_sems):
        self.src, self.dst, self.send_sems, self.recv_sems, self.local_sems = src, dst, send_sems, recv_sems, local_sems

    def _win(self, t, j, core=None):
        return _window(self.dst[t], self.axes[t], j, self.sizes[t], core, self.halves[t])

    def _ici(self, t, kk, origin):
        _, _, c, chips = _place()
        px, py = chips[kk]
        half = self.src[t].at[:, pl.ds(pl.multiple_of(c * self.halves[t], 16), self.halves[t]), :]
        return pltpu.make_async_remote_copy(
            src_ref=half, dst_ref=self._win(t, origin, c), send_sem=self.send_sems.at[t * 3 + kk],
            recv_sem=self.recv_sems.at[t * 3 + kk], device_id=(px, py, c), device_id_type=MESH)

    def _d2d(self, t, kk, origin, core):
        x, y, c, _ = _place()
        k2 = 3 * self.nt + t * 3 + kk
        return pltpu.make_async_remote_copy(
            src_ref=self._win(t, origin, core), dst_ref=self._win(t, origin, core), send_sem=self.send_sems.at[k2],
            recv_sem=self.recv_sems.at[k2], device_id=(x, y, 1 - c), device_id_type=MESH)

    def _local(self, t):
        x, y, _, _ = _place()
        return pltpu.make_async_copy(self.src[t], self._win(t, 2 * x + y), self.local_sems.at[t])

    def _each(self):
        _, _, _, chips = _place()
        for t in range(self.nt):
            for kk in range(3):
                px, py = chips[kk]
                yield t, kk, 2 * px + py

    def start(self):
        x, y, _, _ = _place()
        for t in range(self.nt):
            self._local(t).start()
        for t, kk, _ in self._each():
            self._ici(t, kk, 2 * x + y).start()

    def forward(self):
        _, _, c, _ = _place()
        for t, kk, origin in self._each():
            self._ici(t, kk, origin).wait_recv()
            self._d2d(t, kk, origin, c).start()

    def finish(self):
        x, y, c, _ = _place()
        for t, kk, origin in self._each():
            self._d2d(t, kk, origin, 1 - c).wait_recv()
        for t, kk, origin in self._each():
            self._ici(t, kk, 2 * x + y).wait_send()
            self._d2d(t, kk, origin, c).wait_send()
        for t in range(self.nt):
            self._local(t).wait()


class _Scatter:
    def __init__(self, grads, axes):
        self.nt, self.axes = len(grads), axes
        self.out_shape, self.sizes = [], []
        for gr, ax in zip(grads, axes):
            l, r, cc = gr.shape
            self.out_shape.append(jax.ShapeDtypeStruct((4, l, r // 4, cc) if ax == "r" else (4, l, r, cc // 4), gr.dtype))
            self.sizes.append(r // 4 if ax == "r" else cc // 4)
        self.scratch = [pltpu.SemaphoreType.DMA((3 * self.nt,)), pltpu.SemaphoreType.DMA((3 * self.nt,)),
                        pltpu.SemaphoreType.DMA((self.nt,))]

    def bind(self, src, dst, send_sems, recv_sems, local_sems):
        self.src, self.dst, self.send_sems, self.recv_sems, self.local_sems = src, dst, send_sems, recv_sems, local_sems

    def _copy(self, t, kk, slot):
        x, y, c, chips = _place()
        px, py = chips[kk]
        return pltpu.make_async_remote_copy(
            src_ref=_window(self.src[t], self.axes[t], 2 * px + py, self.sizes[t]), dst_ref=self.dst[t].at[slot],
            send_sem=self.send_sems.at[t * 3 + kk], recv_sem=self.recv_sems.at[t * 3 + kk],
            device_id=(px, py, c), device_id_type=MESH)

    def _local(self, t):
        x, y, _, _ = _place()
        me = 2 * x + y
        return pltpu.make_async_copy(_window(self.src[t], self.axes[t], me, self.sizes[t]), self.dst[t].at[me],
                                     self.local_sems.at[t])

    def start(self):
        x, y, _, _ = _place()
        for t in range(self.nt):
            self._local(t).start()
            for kk in range(3):
                self._copy(t, kk, 2 * x + y).start()

    def finish(self):
        _, _, _, chips = _place()
        for t in range(self.nt):
            for kk in range(3):
                px, py = chips[kk]
                self._copy(t, kk, 2 * px + py).wait_recv()
        for t in range(self.nt):
            for kk in range(3):
                px, py = chips[kk]
                self._copy(t, kk, 2 * px + py).wait_send()
            self._local(t).wait()


def _comm_call(plan, operands, name):
    nt = plan.nt

    def body(*refs):
        plan.bind(refs[:nt], refs[nt:2 * nt], *refs[2 * nt:])
        plan.start()
        if hasattr(plan, "forward"):
            plan.forward()
        plan.finish()

    return pl.pallas_call(body, out_shape=tuple(plan.out_shape), in_specs=[ANY] * nt, out_specs=tuple([ANY] * nt),
                          scratch_shapes=plan.scratch, name=name)(*operands)


def _swap_sibling(arrs, name):
    nt = len(arrs)

    def body(*refs):
        src, dst = refs[:nt], refs[nt:2 * nt]
        send_sems, recv_sems = refs[2 * nt:]
        x, y, c, _ = _place()
        cps = [pltpu.make_async_remote_copy(src_ref=src[t], dst_ref=dst[t], send_sem=send_sems.at[t],
                                            recv_sem=recv_sems.at[t], device_id=(x, y, 1 - c), device_id_type=MESH)
               for t in range(nt)]
        for cp in cps:
            cp.start()
        for cp in cps:
            cp.wait()

    return pl.pallas_call(
        body, out_shape=tuple(jax.ShapeDtypeStruct(a.shape, a.dtype) for a in arrs), in_specs=[ANY] * nt,
        out_specs=tuple([ANY] * nt),
        scratch_shapes=[pltpu.SemaphoreType.DMA((nt,)), pltpu.SemaphoreType.DMA((nt,))], name=name)(*arrs)


def _gather8(pack, with_sum, name):
    rows = pack.shape[0]
    flips = [f for f in itertools.product((0, 1), repeat=3) if any(f)]

    def body(p_ref, all_ref, *rest):
        if with_sum:
            sum_ref, send_sems, recv_sems = rest
        else:
            send_sems, recv_sems = rest
        x, y, c, _ = _place()
        me = 4 * x + 2 * y + c

        def peer(f):
            return tuple(1 - v if fl else v for v, fl in zip((x, y, c), f))

        all_ref[me] = p_ref[...]
        cps = []
        for kk, f in enumerate(flips):
            cp = pltpu.make_async_remote_copy(src_ref=p_ref, dst_ref=all_ref.at[me], send_sem=send_sems.at[kk],
                                              recv_sem=recv_sems.at[kk], device_id=peer(f), device_id_type=MESH)
            cp.start()
            cps.append(cp)
        for kk, f in enumerate(flips):
            px, py, pc = peer(f)
            pltpu.make_async_remote_copy(src_ref=p_ref, dst_ref=all_ref.at[4 * px + 2 * py + pc],
                                         send_sem=send_sems.at[kk], recv_sem=recv_sems.at[kk], device_id=peer(f),
                                         device_id_type=MESH).wait_recv()
        for cp in cps:
            cp.wait_send()
        if with_sum:
            acc = all_ref[0]
            for d in range(1, 8):
                acc = acc + all_ref[d]
            sum_ref[...] = acc

    vm = pl.BlockSpec(memory_space=pltpu.VMEM)
    out_shape = [jax.ShapeDtypeStruct((8, rows, LANES), F32)] + ([jax.ShapeDtypeStruct((rows, LANES), F32)] if with_sum else [])
    return pl.pallas_call(
        body, out_shape=tuple(out_shape), in_specs=[vm], out_specs=tuple([vm] * len(out_shape)),
        scratch_shapes=[pltpu.SemaphoreType.DMA((7,)), pltpu.SemaphoreType.DMA((7,))], name=name)(pack)


def _pack(arrs):
    flat = jnp.concatenate([a.reshape(-1).astype(F32) for a in arrs])
    rows = -(-flat.shape[0] // (8 * LANES)) * 8
    return jnp.pad(flat, (0, rows * LANES - flat.shape[0])).reshape(rows, LANES)


def _unpack(flat, shapes):
    out, pos = [], 0
    for shp in shapes:
        size = 1
        for s in shp:
            size *= s
        out.append(flat[pos:pos + size].reshape(shp))
        pos += size
    return out


BIG = ("attn_w_qkv", "attn_w_o", "pool_w", "xattn_w_q", "xattn_w_kv", "xattn_w_o", "ffn_w_up", "ffn_w_down")
BIG_AXIS = ("c", "r", "r", "r", "c", "r", "c", "r")
SMALL_REPL = ("attn_norm", "attn_q_gain", "attn_k_gain", "xattn_norm", "mem_norm", "ffn_norm", "ffn_conv_b", "final_norm")
SMALL_SHARD = ("pool_norm", "pool_scale", "ffn_conv_w")
ORDER = ("attn_norm", "attn_w_qkv", "attn_q_gain", "attn_k_gain", "attn_w_o", "pool_norm", "pool_w", "pool_scale",
         "xattn_norm", "mem_norm", "xattn_w_q", "xattn_w_kv", "xattn_w_o", "ffn_norm", "ffn_w_up", "ffn_conv_w",
         "ffn_conv_b", "ffn_w_down", "final_norm")


def _step(x, mem, tgt, w, m, v):
    seq, d = x.shape
    xi, yi, ci = lax.axis_index("x"), lax.axis_index("y"), lax.axis_index("c")
    chip = 2 * xi + yi
    dff = w["ffn_w_down"].shape[1] * 4
    n_layers = w["ffn_norm"].shape[0]

    def as3d(a):
        return a.reshape(a.shape[-3:])
    shards = [as3d(w[nm]).astype(BF16) for nm in BIG]
    (wq,) = _comm_call(_Gather(shards[:1], BIG_AXIS[:1]), shards[:1], "gather_qkv")
    small_in = [w[nm] for nm in SMALL_SHARD]
    (small_all,) = _gather8(_pack(small_in), False, "gather_small")
    per_chip = [_unpack(small_all[2 * j].reshape(-1), [a.shape for a in small_in]) for j in range(4)]
    pool_norm, pool_scale, conv_w = (jnp.concatenate([per_chip[j][i] for j in range(4)], axis=-1) for i in range(3))

    conv_b = w["ffn_conv_b"].reshape(n_layers, 1, -1)
    tabs = _rope_tables(seq)
    qg2 = jnp.tile(w["attn_q_gain"], (1, 2))
    kg2 = jnp.tile(w["attn_k_gain"], (1, 2))
    mm = functools.partial(_mm)

    saved = {}
    x0 = x
    h0 = _rms_fwd(x0, w["attn_norm"], BF16, "rms_attn")
    qkv = mm(h0, wq, "nn", b_l=0, out_dtype=F32, name="mm_qkv")
    q_r, k_r, k_t, v_b, v_t = _qk_prep(qkv, qg2, kg2, tabs, "qk_prep")
    o_at, lse, wo, wp, wxq, wxkv, wxo, wup, wdn = _flash_fwd(
        q_r, k_r, v_t, _Gather(shards[1:], BIG_AXIS[1:]), shards[1:], "flash_fwd")
    xs = [x0, mm(o_at, wo, "nn", b_l=0, res=x0, out_dtype=F32, name="mm_attn_o")]

    def xattn_fwd(l, xin):
        hq = _rms_fwd(xin, w["xattn_norm"][l:l + 1], BF16, f"rms_xq{l}")
        mn = _rms_fwd(mem, w["mem_norm"][l:l + 1], BF16, f"rms_mem{l}")
        xq = mm(hq, wxq, "nn", b_l=l, scale=X_HEAD_DIM ** -0.5, out_dtype=BF16, name=f"mm_xq{l}")
        kv = mm(mn, wxkv, "nn", b_l=l, out_dtype=BF16, name=f"mm_xkv{l}")
        xo = _xattn_fwd(xq, kv, f"xattn_fwd{l}")
        saved[f"x{l}"] = (hq, mn, xq, kv, xo)
        return mm(xo, wxo, "nn", b_l=l, res=xin, out_dtype=F32, name=f"mm_xo{l}")

    def ffn_fwd(l, xin):
        hf = _rms_fwd(xin, w["ffn_norm"][l:l + 1], BF16, f"rms_ffn{l}")
        ug = mm(hf, wup, "nn", b_l=l, n=dff, out_dtype=F32, name=f"mm_up_g{l}")
        uv = mm(hf, wup, "nn", b_l=l, n=dff, b_off=(0, dff), out_dtype=F32, name=f"mm_up_v{l}")
        act = _conv_gate_fwd(ug, uv, conv_w, conv_b, l, f"conv_gate{l}")
        saved[f"f{l}"] = (hf, ug, uv, act)
        return mm(act, wdn, "nn", b_l=l, res=xin, out_dtype=F32, name=f"mm_down{l}")

    xs.append(xattn_fwd(0, xs[-1]))
    xs.append(ffn_fwd(0, xs[-1]))
    hp = _rms_fwd(xs[-1], pool_norm, F32, "rms_pool")
    xs.append(_pool_fwd(hp, xs[-1], wp, pool_scale, "pool_fwd"))
    xs.append(xattn_fwd(1, xs[-1]))
    xs.append(ffn_fwd(1, xs[-1]))
    dres, g_final, loss = _final_loss(xs[6], w["final_norm"].reshape(1, d), tgt, "final_loss")

    grads = {}
    gbuf = {}

    def dw(nm, a, b, layer, full, off=(0, 0), n=None, tn=None):
        gbuf[nm] = _mm(a, b, "tn", out_dtype=BF16, out_full=full, out_l=layer, out_off=off, n=n, tn=tn,
                       alias=gbuf.get(nm), name=f"dw_{nm}{layer}_{off[1]}")

    def ffn_bwd(l, xin, dres):
        hf, ug, uv, act = saved[f"f{l}"]
        dw("ffn_w_down", act, dres, l, wdn.shape)
        dact = _mm(dres, wdn, "nt", b_l=l, out_dtype=F32, name=f"mm_dact{l}")
        dug, duv, dwg, dwv = _conv_gate_bwd(ug, uv, dact, conv_w, conv_b, l, f"conv_gate_bwd{l}")
        dw("ffn_w_up", hf, dug, l, wup.shape, tn=1408)
        dw("ffn_w_up", hf, duv, l, wup.shape, off=(0, dff), tn=1408)
        dhf = _mm(dug, wup, "nt", b_l=l, n=d, out_dtype=F32, name=f"mm_dhf_g{l}")
        dhf = _mm(duv, wup, "nt", b_l=l, n=d, b_off=(0, dff), res=dhf, out_dtype=F32, name=f"mm_dhf_v{l}")
        dres, dg = _rms_bwd(xin, w["ffn_norm"][l:l + 1], dhf, dres, f"rms_ffn_bwd{l}")
        return dres, dg, jnp.concatenate([dwg[:3], dwv[:3]], axis=1), jnp.concatenate([dwg[3], dwv[3]], axis=0)

    def xattn_bwd(l, xin, dres):
        hq, mn, xq, kv, xo = saved[f"x{l}"]
        dw("xattn_w_o", xo, dres, l, wxo.shape)
        dxo = _mm(dres, wxo, "nt", b_l=l, out_dtype=F32, name=f"mm_dxo{l}")
        dq, dkv = _xattn_bwd(xq, kv, dxo, f"xattn_bwd{l}")
        dw("xattn_w_q", hq, dq, l, wxq.shape)
        dhq = _mm(dq, wxq, "nt", b_l=l, out_dtype=F32, name=f"mm_dhq{l}")
        dw("xattn_w_kv", mn, dkv, l, wxkv.shape)
        dmn = _mm(dkv, wxkv, "nt", b_l=l, out_dtype=F32, name=f"mm_dmn{l}")
        _, dg_mem = _rms_bwd(mem, w["mem_norm"][l:l + 1], dmn, None, f"rms_mem_bwd{l}")
        dres, dg = _rms_bwd(xin, w["xattn_norm"][l:l + 1], dhq, dres, f"rms_xq_bwd{l}")
        return dres, dg, dg_mem

    g_ffn, g_xn, g_mn, g_cw, g_cb = [None] * n_layers, [None] * n_layers, [None] * n_layers, [None] * n_layers, [None] * n_layers
    dres, g_ffn[1], g_cw[1], g_cb[1] = ffn_bwd(1, xs[5], dres)
    dres, g_xn[1], g_mn[1] = xattn_bwd(1, xs[4], dres)
    dhp, g_pw, g_pscale = _pool_bwd(hp, dres, wp, pool_scale, "pool_bwd")
    dres, g_pnorm = _rms_bwd(xs[3], pool_norm, dhp, dres, "rms_pool_bwd")
    dres, g_ffn[0], g_cw[0], g_cb[0] = ffn_bwd(0, xs[2], dres)
    dres, g_xn[0], g_mn[0] = xattn_bwd(0, xs[1], dres)
    dw("attn_w_o", o_at, dres, 0, wo.shape)
    do = _mm(dres, wo, "nt", b_l=0, out_dtype=F32, name="mm_do")
    gbuf["pool_w"] = g_pw.astype(BF16)
    early = [gbuf[nm] for nm in BIG[2:]]
    dq_r, dk_r, dv, *recv_early = _flash_bwd(q_r, k_r, k_t, v_b, do, o_at, lse, _Scatter(early, BIG_AXIS[2:]), early,
                                             "flash_bwd")
    dqkv, dqg, dkg = _qk_prep_bwd(qkv, dq_r, dk_r, dv, qg2, kg2, tabs, "qk_prep_bwd")
    dw("attn_w_qkv", h0, dqkv, 0, wq.shape)
    dh0 = _mm(dqkv, wq, "nt", b_l=0, out_dtype=F32, name="mm_dh0")
    grad_x, g_an = _rms_bwd(x0, w["attn_norm"], dh0, dres, "rms_attn_bwd")

    small_g = {
        "attn_norm": g_an, "attn_q_gain": dqg[:, :HEAD_DIM] + dqg[:, HEAD_DIM:], "attn_k_gain": dkg[:, :HEAD_DIM] + dkg[:, HEAD_DIM:],
        "xattn_norm": jnp.concatenate(g_xn, axis=0), "mem_norm": jnp.concatenate(g_mn, axis=0),
        "ffn_norm": jnp.concatenate(g_ffn, axis=0), "ffn_conv_b": jnp.stack(g_cb, axis=0), "final_norm": g_final.reshape(d),
        "pool_norm": g_pnorm, "pool_scale": g_pscale, "ffn_conv_w": jnp.stack(g_cw, axis=0)}
    names = SMALL_REPL + SMALL_SHARD
    _, total = _gather8(_pack([loss[0, :1]] + [small_g[nm] for nm in names]), True, "reduce_small")
    parts = _unpack(total.reshape(-1), [(1,)] + [small_g[nm].shape for nm in names])
    loss_out = parts[0][0]
    for nm, g in zip(names, parts[1:]):
        if nm in SMALL_SHARD:
            size = w[nm].shape[-1]
            g = lax.dynamic_slice_in_dim(g, chip * size, size, axis=g.ndim - 1)
        grads[nm] = g.reshape(w[nm].shape)

    packed = [_pack([src[nm] for nm in names]) for src in (w, grads, m, v)]
    _, sd, sm, sv = _adamw(packed[0], packed[1], None, packed[2], packed[3], "adamw_small")
    shapes = [w[nm].shape for nm in names]
    delta = dict(zip(names, _unpack(sd.reshape(-1), shapes)))
    new_m = dict(zip(names, _unpack(sm.reshape(-1), shapes)))
    new_v = dict(zip(names, _unpack(sv.reshape(-1), shapes)))

    late = [gbuf[nm] for nm in BIG[:2]]
    recv = list(_comm_call(_Scatter(late, BIG_AXIS[:2]), late, "scatter_attn")) + recv_early
    sums = []
    for nm, rc in zip(BIG, recv):
        sums.append(_sum4(rc.reshape(4, -1, rc.shape[-1]), f"sum4_{nm}"))
    others = _swap_sibling(sums, "swap_sums")
    for nm, mine, other in zip(BIG, sums, others):
        cols = mine.shape[-1]
        outs = _adamw(w[nm].reshape(-1, cols), mine, other, m[nm].reshape(-1, cols), v[nm].reshape(-1, cols), f"adamw_{nm}")
        grads[nm], delta[nm], new_m[nm], new_v[nm] = (o.reshape(w[nm].shape) for o in outs)

    return loss_out, grad_x, grads, delta, new_m, new_v


def kernel(x, mem, attn_norm, attn_w_qkv, attn_q_gain, attn_k_gain, attn_w_o, pool_norm, pool_w, pool_scale, xattn_norm, mem_norm, xattn_w_q, xattn_w_kv, xattn_w_o, ffn_norm, ffn_w_up, ffn_conv_w, ffn_conv_b, ffn_w_down, final_norm, loss_target, m_attn_norm, m_attn_w_qkv, m_attn_q_gain, m_attn_k_gain, m_attn_w_o, m_pool_norm, m_pool_w, m_pool_scale, m_xattn_norm, m_mem_norm, m_xattn_w_q, m_xattn_w_kv, m_xattn_w_o, m_ffn_norm, m_ffn_w_up, m_ffn_conv_w, m_ffn_conv_b, m_ffn_w_down, m_final_norm, v_attn_norm, v_attn_w_qkv, v_attn_q_gain, v_attn_k_gain, v_attn_w_o, v_pool_norm, v_pool_w, v_pool_scale, v_xattn_norm, v_mem_norm, v_xattn_w_q, v_xattn_w_kv, v_xattn_w_o, v_ffn_norm, v_ffn_w_up, v_ffn_conv_w, v_ffn_conv_b, v_ffn_w_down, v_final_norm):
    given = dict(locals())
    w = {nm: given[nm] for nm in ORDER}
    m = {nm: given["m_" + nm] for nm in ORDER}
    v = {nm: given["v_" + nm] for nm in ORDER}
    seq, d = x.shape[1], x.shape[2]
    loss, grad_x, grads, delta, new_m, new_v = _step(
        x.reshape(seq, d), mem.reshape(mem.shape[1], d), loss_target.reshape(seq, d), w, m, v)
    return (loss, grad_x.reshape(x.shape), *[grads[nm] for nm in ORDER], *[delta[nm] for nm in ORDER],
            *[new_m[nm] for nm in ORDER], *[new_v[nm] for nm in ORDER])
```

```python
import functools
import itertools

import jax
import jax.numpy as jnp
from jax import lax
from jax.experimental import pallas as pl
from jax.experimental.pallas import tpu as pltpu

F32, BF16 = jnp.float32, jnp.bfloat16
EPS = 1e-6
GRID_W = 64
ROPE_THETA = 10000.0
HEAD_DIM = 64
N_HEADS = 16
N_KV = 4
X_HEADS = 4
X_HEAD_DIM = 256
POOL_GROUPS = 4
POOL_GROUP_W = 256
HALO = 16
LANES = 128
ADAM_LR, ADAM_B1, ADAM_B2, ADAM_EPS, ADAM_WD, ADAM_STEP = 0.001, 0.9, 0.999, 1e-08, 0.01, 10
VMEM_LIMIT = 48 * 1024 * 1024
MESH = pl.DeviceIdType.MESH
NEG = -1e30
LOG2E = 1.4426950408889634
FLASH_TQ, FLASH_TK = 256, 2048
ANY = pl.BlockSpec(memory_space=pl.ANY)


def _cp(sem=None):
    return pltpu.CompilerParams(dimension_semantics=sem, vmem_limit_bytes=VMEM_LIMIT)


def _pick(n, cands):
    for c in cands:
        if c <= n and n % c == 0:
            return c
    return n


def _mm(a, b, mode, *, name, out_dtype, tm=None, tn=None, tk=None, n=None, k=None, b_l=None, b_off=(0, 0),
        res=None, scale=None, out_full=None, out_l=None, out_off=(0, 0), alias=None):
    if mode == "tn":
        K, M = a.shape
    else:
        M, K = a.shape
    bs = b.shape[-2:]
    if mode == "nn":
        K = k or K
        N = n or bs[1]
    elif mode == "nt":
        N = n or bs[0]
    else:
        N = n or bs[1]
    wide = (1408, 1024, 512, 256, 128)
    if mode == "tn":
        tm = tm or (M if M <= 1024 else _pick(M, wide))
        tk = tk or _pick(K, (2048, 1024, 512, 256, 128))
    else:
        tm = _pick(M, (tm or 512, 256, 128))
        tk = tk or (K if K <= 2816 else _pick(K, wide))
    tn = tn or (N if N <= 1536 else _pick(N, wide))
    assert M % tm == 0 and N % tn == 0 and K % tk == 0, (name, M, N, K, tm, tn, tk)
    nk = K // tk
    dims = {"nn": ((1,), (0,)), "nt": ((1,), (1,)), "tn": ((0,), (0,))}[mode]

    j_outer = nk == 1 and mode != "tn"

    def at(f):
        return (lambda j, i, kk: f(i, j, kk)) if j_outer else f

    if mode == "tn":
        a_spec = pl.BlockSpec((tk, tm), at(lambda i, j, kk: (kk, i)))
    else:
        a_spec = pl.BlockSpec((tm, tk), at(lambda i, j, kk: (i, kk)))
    if mode == "nt":
        bb, (d0, d1) = (tn, tk), (b_off[0] // tn, b_off[1] // tk)
        assert b_off[0] % tn == 0 and b_off[1] % tk == 0
        bidx = lambda i, j, kk: (j + d0, kk + d1)
    else:
        bb, (d0, d1) = (tk, tn), (b_off[0] // tk, b_off[1] // tn)
        assert b_off[0] % tk == 0 and b_off[1] % tn == 0
        bidx = lambda i, j, kk: (kk + d0, j + d1)
    if b.ndim == 3:
        b_spec = pl.BlockSpec((None,) + bb, at(lambda i, j, kk: (b_l,) + bidx(i, j, kk)))
    else:
        b_spec = pl.BlockSpec(bb, at(bidx))
    in_specs, operands = [a_spec, b_spec], [a, b]
    if res is not None:
        in_specs.append(pl.BlockSpec((tm, tn), at(lambda i, j, kk: (i, j))))
        operands.append(res)
    aliases = {}
    if alias is not None:
        aliases = {len(operands): 0}
        in_specs.append(ANY)
        operands.append(alias)
    if out_full is None:
        out_shape = jax.ShapeDtypeStruct((M, N), out_dtype)
        out_spec = pl.BlockSpec((tm, tn), at(lambda i, j, kk: (i, j)))
    else:
        assert out_off[0] % tm == 0 and out_off[1] % tn == 0
        o0, o1 = out_off[0] // tm, out_off[1] // tn
        out_shape = jax.ShapeDtypeStruct(out_full, out_dtype)
        out_spec = pl.BlockSpec((None, tm, tn), at(lambda i, j, kk: (out_l, i + o0, j + o1)))
    has_res, has_alias = res is not None, alias is not None
    grid = (N // tn, M // tm, nk) if j_outer else (M // tm, N // tn, nk)

    def body(*refs):
        a_ref, b_ref = refs[0], refs[1]
        pos = 2
        res_ref = None
        if has_res:
            res_ref = refs[pos]
            pos += 1
        if has_alias:
            pos += 1
        o_ref, acc_ref = refs[pos], refs[pos + 1]
        kk = pl.program_id(2)
        part = lax.dot_general(a_ref[...].astype(BF16), b_ref[...].astype(BF16), (dims, ((), ())),
                               preferred_element_type=F32)

        def finish(acc):
            if scale is not None:
                acc = acc * scale
            if res_ref is not None:
                acc = acc + res_ref[...]
            o_ref[...] = acc.astype(o_ref.dtype)

        if nk == 1:
            finish(part)
        else:
            @pl.when(kk == 0)
            def _():
                acc_ref[...] = part

            @pl.when(jnp.logical_and(kk > 0, kk < nk - 1))
            def _():
                acc_ref[...] += part

            @pl.when(kk == nk - 1)
            def _():
                finish(acc_ref[...] + part)

    return pl.pallas_call(
        body, out_shape=out_shape, grid=grid, in_specs=in_specs, out_specs=out_spec,
        scratch_shapes=[pltpu.VMEM((tm, tn) if nk > 1 else (8, 128), F32)], input_output_aliases=aliases,
        compiler_params=_cp(("parallel", "parallel", "arbitrary")), name=name)(*operands)


def _rms_fwd(x, gain, out_dtype, name):
    rows, d = x.shape
    tr = _pick(rows, (512, 256))

    def body(x_ref, g_ref, o_ref):
        xv = x_ref[...]
        r = lax.rsqrt(jnp.mean(xv * xv, axis=-1, keepdims=True) + EPS)
        o_ref[...] = (xv * r * g_ref[...]).astype(o_ref.dtype)

    return pl.pallas_call(
        body, out_shape=jax.ShapeDtypeStruct((rows, d), out_dtype), grid=(rows // tr,),
        in_specs=[pl.BlockSpec((tr, d), lambda i: (i, 0)), pl.BlockSpec((1, d), lambda i: (0, 0))],
        out_specs=pl.BlockSpec((tr, d), lambda i: (i, 0)), compiler_params=_cp(("parallel",)), name=name)(x, gain)


def _rms_bwd(x, gain, dh, dres, name):
    rows, d = x.shape
    tr = _pick(rows, (512, 256))
    need_dx = dres is not None

    def body(*refs):
        if need_dx:
            x_ref, g_ref, dh_ref, dres_ref, o_ref, dg_ref = refs
        else:
            x_ref, g_ref, dh_ref, dg_ref = refs
        i = pl.program_id(0)
        xv = x_ref[...]
        dhv = dh_ref[...].astype(F32)
        r = lax.rsqrt(jnp.mean(xv * xv, axis=-1, keepdims=True) + EPS)
        nv = xv * r
        part = jnp.sum(dhv * nv, axis=0, keepdims=True)

        @pl.when(i == 0)
        def _():
            dg_ref[...] = part

        @pl.when(i > 0)
        def _():
            dg_ref[...] += part

        if need_dx:
            dn = dhv * g_ref[...]
            dx = r * (dn - nv * jnp.mean(dn * nv, axis=-1, keepdims=True))
            o_ref[...] = dres_ref[...] + dx

    row_spec = pl.BlockSpec((tr, d), lambda i: (i, 0))
    vec_spec = pl.BlockSpec((1, d), lambda i: (0, 0))
    if need_dx:
        return pl.pallas_call(
            body, out_shape=(jax.ShapeDtypeStruct((rows, d), F32), jax.ShapeDtypeStruct((1, d), F32)),
            grid=(rows // tr,), in_specs=[row_spec, vec_spec, row_spec, row_spec], out_specs=(row_spec, vec_spec),
            compiler_params=_cp(("arbitrary",)), name=name)(x, gain, dh, dres)
    return None, pl.pallas_call(
        body, out_shape=jax.ShapeDtypeStruct((1, d), F32), grid=(rows // tr,),
        in_specs=[row_spec, vec_spec, row_spec], out_specs=vec_spec,
        compiler_params=_cp(("arbitrary",)), name=name)(x, gain, dh)


def _final_loss(x, gain, target, name):
    rows, d = x.shape
    tr = _pick(rows, (512, 256))
    nsteps = rows // tr

    def body(x_ref, g_ref, t_ref, dx_ref, dg_ref, loss_ref, acc_ref):
        i = pl.program_id(0)
        xv = x_ref[...]
        g = g_ref[...]
        r = lax.rsqrt(jnp.mean(xv * xv, axis=-1, keepdims=True) + EPS)
        nv = xv * r
        err = nv * g - t_ref[...]
        dy = err * (1.0 / d)
        dn = dy * g
        dx_ref[...] = r * (dn - nv * jnp.mean(dn * nv, axis=-1, keepdims=True))
        dgp = jnp.sum(dy * nv, axis=0, keepdims=True)
        lp = jnp.sum(err * err, axis=0, keepdims=True)

        @pl.when(i == 0)
        def _():
            dg_ref[...] = dgp
            acc_ref[...] = lp

        @pl.when(i > 0)
        def _():
            dg_ref[...] += dgp
            acc_ref[...] += lp

        @pl.when(i == nsteps - 1)
        def _():
            tot = jnp.sum(acc_ref[...], axis=1, keepdims=True) * (0.5 / d)
            loss_ref[...] = jnp.broadcast_to(tot, loss_ref.shape)

    row_spec = pl.BlockSpec((tr, d), lambda i: (i, 0))
    vec_spec = pl.BlockSpec((1, d), lambda i: (0, 0))
    return pl.pallas_call(
        body, out_shape=(jax.ShapeDtypeStruct((rows, d), F32), jax.ShapeDtypeStruct((1, d), F32),
                         jax.ShapeDtypeStruct((1, LANES), F32)),
        grid=(nsteps,), in_specs=[row_spec, vec_spec, row_spec],
        out_specs=(row_spec, vec_spec, pl.BlockSpec((1, LANES), lambda i: (0, 0))),
        scratch_shapes=[pltpu.VMEM((1, d), F32)], compiler_params=_cp(("arbitrary",)), name=name)(x, gain, target)


def _rope_tables(seq):
    pairs = HEAD_DIM // 4
    lane = jnp.arange(LANES, dtype=jnp.int32) % HEAD_DIM
    by_col, second, pair = lane // (2 * pairs) == 1, (lane % (2 * pairs)) // pairs == 1, lane % pairs
    inv_freq = ROPE_THETA ** (-pair.astype(F32) / pairs)
    t = jnp.arange(seq, dtype=jnp.int32)[:, None]
    pos = jnp.where(by_col[None, :], t % GRID_W, t // GRID_W).astype(F32)
    ang = pos * inv_freq[None, :]
    cos, sin = jnp.cos(ang), jnp.sin(ang)
    return cos, jnp.where(second[None, :], sin, 0.0), jnp.where(second[None, :], 0.0, -sin)


def _pair_norm(xv, lo):
    sq = xv * xv
    s_lo = jnp.sum(jnp.where(lo, sq, 0.0), axis=1, keepdims=True)
    s_hi = jnp.sum(jnp.where(lo, 0.0, sq), axis=1, keepdims=True)
    return lax.rsqrt(jnp.where(lo, s_lo, s_hi) * (1.0 / HEAD_DIM) + EPS)


def _rope(y, c, sp, sm):
    return y * c + pltpu.roll(y, 16, axis=1) * sp + pltpu.roll(y, LANES - 16, axis=1) * sm


def _rope_t(dz, c, sp, sm):
    return dz * c + pltpu.roll(dz * sp, LANES - 16, axis=1) + pltpu.roll(dz * sm, 16, axis=1)


def _qk_prep(qkv, qg2, kg2, tabs, name):
    seq = qkv.shape[0]
    ts = _pick(seq, (256, 128))
    nq, nkp = N_HEADS // 2, N_KV // 2
    qw, kw = N_HEADS * HEAD_DIM, N_KV * HEAD_DIM

    def body(x_ref, qg_ref, kg_ref, c_ref, sp_ref, sm_ref, q_ref, k_ref, kt_ref, v_ref, vt_ref):
        lo = lax.broadcasted_iota(jnp.int32, (ts, LANES), 1) < HEAD_DIM
        top = lax.broadcasted_iota(jnp.int32, (LANES, ts), 0) < HEAD_DIM
        c, sp, sm = c_ref[...], sp_ref[...], sm_ref[...]
        for i in range(nq):
            xv = x_ref[:, i * LANES:(i + 1) * LANES]
            y = xv * _pair_norm(xv, lo) * qg_ref[...]
            q_ref[:, i * LANES:(i + 1) * LANES] = (_rope(y, c, sp, sm) * (LOG2E * HEAD_DIM ** -0.5)).astype(BF16)
        for i in range(nkp):
            xv = x_ref[:, qw + i * LANES:qw + (i + 1) * LANES]
            z = _rope(xv * _pair_norm(xv, lo) * kg_ref[...], c, sp, sm)
            k_ref[:, i * LANES:(i + 1) * LANES] = z.astype(BF16)
            kt_ref[i * LANES:(i + 1) * LANES, :] = z.T.astype(BF16)
            vv = x_ref[:, qw + kw + i * LANES:qw + kw + (i + 1) * LANES]
            v_ref[:, i * LANES:(i + 1) * LANES] = vv.astype(BF16)
            vvt = vv.T
            vt_ref[(2 * i) * LANES:(2 * i + 1) * LANES, :] = jnp.where(top, vvt, 1.0).astype(BF16)
            vt_ref[(2 * i + 1) * LANES:(2 * i + 2) * LANES, :] = jnp.where(top, 1.0, vvt).astype(BF16)

    tab = pl.BlockSpec((ts, LANES), lambda i: (i, 0))
    vec = pl.BlockSpec((1, LANES), lambda i: (0, 0))
    return pl.pallas_call(
        body,
        out_shape=(jax.ShapeDtypeStruct((seq, qw), BF16), jax.ShapeDtypeStruct((seq, kw), BF16),
                   jax.ShapeDtypeStruct((kw, seq), BF16), jax.ShapeDtypeStruct((seq, kw), BF16),
                   jax.ShapeDtypeStruct((N_KV * LANES, seq), BF16)),
        grid=(seq // ts,),
        in_specs=[pl.BlockSpec((ts, qw + 2 * kw), lambda i: (i, 0)), vec, vec, tab, tab, tab],
        out_specs=(pl.BlockSpec((ts, qw), lambda i: (i, 0)), pl.BlockSpec((ts, kw), lambda i: (i, 0)),
                   pl.BlockSpec((kw, ts), lambda i: (0, i)), pl.BlockSpec((ts, kw), lambda i: (i, 0)),
                   pl.BlockSpec((N_KV * LANES, ts), lambda i: (0, i))),
        compiler_params=_cp(("parallel",)), name=name)(qkv, qg2, kg2, *tabs)


def _qk_prep_bwd(qkv, dq, dk, dv, qg2, kg2, tabs, name):
    seq = qkv.shape[0]
    ts = _pick(seq, (256, 128))
    nq, nkp = N_HEADS // 2, N_KV // 2
    qw, kw = N_HEADS * HEAD_DIM, N_KV * HEAD_DIM

    def body(x_ref, dq_ref, dk_ref, dv_ref, qg_ref, kg_ref, c_ref, sp_ref, sm_ref, o_ref, dqg_ref, dkg_ref):
        step = pl.program_id(0)
        lo = lax.broadcasted_iota(jnp.int32, (ts, LANES), 1) < HEAD_DIM
        c, sp, sm = c_ref[...], sp_ref[...], sm_ref[...]

        def one(xv, dz, gain):
            r = _pair_norm(xv, lo)
            nv = xv * r
            dy = _rope_t(dz, c, sp, sm)
            dgp = jnp.sum(dy * nv, axis=0, keepdims=True)
            dn = dy * gain
            t = dn * nv
            m_lo = jnp.sum(jnp.where(lo, t, 0.0), axis=1, keepdims=True)
            m_hi = jnp.sum(jnp.where(lo, 0.0, t), axis=1, keepdims=True)
            m = jnp.where(lo, m_lo, m_hi) * (1.0 / HEAD_DIM)
            return r * (dn - nv * m), dgp

        dqg = jnp.zeros((1, LANES), F32)
        for i in range(nq):
            sl = slice(i * LANES, (i + 1) * LANES)
            dx, dgp = one(x_ref[:, sl], dq_ref[:, sl] * (HEAD_DIM ** -0.5), qg_ref[...])
            o_ref[:, sl] = dx.astype(BF16)
            dqg = dqg + dgp
        dkg = jnp.zeros((1, LANES), F32)
        for i in range(nkp):
            sl = slice(i * LANES, (i + 1) * LANES)
            dx, dgp = one(x_ref[:, qw + i * LANES:qw + (i + 1) * LANES], dk_ref[:, sl], kg_ref[...])
            o_ref[:, qw + i * LANES:qw + (i + 1) * LANES] = dx.astype(BF16)
            dkg = dkg + dgp
            o_ref[:, qw + kw + i * LANES:qw + kw + (i + 1) * LANES] = dv_ref[:, sl].astype(BF16)

        @pl.when(step == 0)
        def _():
            dqg_ref[...] = dqg
            dkg_ref[...] = dkg

        @pl.when(step > 0)
        def _():
            dqg_ref[...] += dqg
            dkg_ref[...] += dkg

    tab = pl.BlockSpec((ts, LANES), lambda i: (i, 0))
    vec = pl.BlockSpec((1, LANES), lambda i: (0, 0))
    return pl.pallas_call(
        body,
        out_shape=(jax.ShapeDtypeStruct((seq, qw + 2 * kw), BF16), jax.ShapeDtypeStruct((1, LANES), F32),
                   jax.ShapeDtypeStruct((1, LANES), F32)),
        grid=(seq // ts,),
        in_specs=[pl.BlockSpec((ts, qw + 2 * kw), lambda i: (i, 0)), pl.BlockSpec((ts, qw), lambda i: (i, 0)),
                  pl.BlockSpec((ts, kw), lambda i: (i, 0)), pl.BlockSpec((ts, kw), lambda i: (i, 0)),
                  vec, vec, tab, tab, tab],
        out_specs=(pl.BlockSpec((ts, qw + 2 * kw), lambda i: (i, 0)), vec, vec),
        compiler_params=_cp(("arbitrary",)), name=name)(qkv, dq, dk, dv, qg2, kg2, *tabs)


def _slot(blk, off0, tq):
    half = lax.broadcasted_iota(jnp.int32, (tq, LANES), 1) // HEAD_DIM
    keep = half == jnp.where(off0, 0, 1)
    parts = []
    for p in range(2):
        pair = blk[:, p * LANES:(p + 1) * LANES].astype(F32)
        rolled = pltpu.roll(pair, HEAD_DIM, axis=1)
        parts.append(jnp.where(keep, jnp.where(off0, pair, rolled), 0.0))
        parts.append(jnp.where(keep, jnp.where(off0, rolled, pair), 0.0))
    return jnp.concatenate(parts, axis=0)


def _unslot(x4, off0, tq):
    lo = lax.broadcasted_iota(jnp.int32, (tq, LANES), 1) < HEAD_DIM
    pairs = []
    for p in range(2):
        h0 = x4[(2 * p) * tq:(2 * p + 1) * tq]
        h1 = x4[(2 * p + 1) * tq:(2 * p + 2) * tq]
        a = jnp.where(off0, h0, pltpu.roll(h0, HEAD_DIM, axis=1))
        b = jnp.where(off0, pltpu.roll(h1, HEAD_DIM, axis=1), h1)
        pairs.append(jnp.where(lo, a, b))
    return jnp.concatenate(pairs, axis=1)


def _flash_fwd(q, k, vt, plan, shards, name):
    seq = q.shape[0]
    tq = _pick(seq, (FLASH_TQ, 128))
    tk = _pick(seq, (FLASH_TK, 512, 256, 128))
    nq, nkv = seq // tq, seq // tk
    gw = 4 * HEAD_DIM
    nt = plan.nt

    def body(q_ref, k_ref, vt_ref, *rest):
        o_ref, lse_ref = rest[nt:nt + 2]
        q4_ref, m_ref, acc_ref = rest[2 * nt + 2:2 * nt + 5]
        plan.bind(rest[:nt], rest[nt + 2:2 * nt + 2], *rest[2 * nt + 5:])
        g, qi, ki = pl.program_id(0), pl.program_id(1), pl.program_id(2)
        off0 = (g % 2) == 0
        inner0 = jnp.logical_and(qi == 0, ki == 0)

        @pl.when(jnp.logical_and(g == 0, inner0))
        def _():
            plan.start()

        @pl.when(jnp.logical_and(g == N_KV - 1, inner0))
        def _():
            plan.forward()

        @pl.when(ki == 0)
        def _():
            q4_ref[...] = _slot(q_ref[...], off0, tq).astype(BF16)
            m_ref[...] = jnp.full(m_ref.shape, NEG, F32)
            acc_ref[...] = jnp.zeros(acc_ref.shape, F32)

        st = lax.dot_general(k_ref[...], q4_ref[...], (((1,), (1,)), ((), ())), preferred_element_type=F32)
        m_old = m_ref[...]
        m_new = jnp.maximum(m_old, jnp.max(st, axis=0, keepdims=True))
        pt = jnp.exp2(st - m_new).astype(BF16)
        acc_ref[...] = jnp.exp2(m_old - m_new) * acc_ref[...] + jnp.dot(vt_ref[...], pt, preferred_element_type=F32)
        m_ref[...] = m_new

        @pl.when(ki == nkv - 1)
        def _():
            acc = acc_ref[...]
            l = jnp.where(off0, acc[HEAD_DIM:HEAD_DIM + 1], acc[0:1])
            o4 = acc.T
            o4 = o4 / pltpu.roll(o4, HEAD_DIM, axis=1)
            o_ref[...] = _unslot(o4, off0, tq).astype(o_ref.dtype)
            lse_ref[...] = jnp.broadcast_to(m_ref[...] + jnp.log2(l), lse_ref.shape)

        @pl.when(jnp.logical_and(g == N_KV - 1, jnp.logical_and(qi == nq - 1, ki == nkv - 1)))
        def _():
            plan.finish()

    return pl.pallas_call(
        body,
        out_shape=(jax.ShapeDtypeStruct((seq, N_HEADS * HEAD_DIM), BF16),
                   jax.ShapeDtypeStruct((N_KV * nq * 8, 4 * tq), F32), *plan.out_shape),
        grid=(N_KV, nq, nkv),
        in_specs=[pl.BlockSpec((tq, gw), lambda g, qi, ki: (qi, g)),
                  pl.BlockSpec((tk, LANES), lambda g, qi, ki: (ki, g // 2)),
                  pl.BlockSpec((LANES, tk), lambda g, qi, ki: (g, ki))] + [ANY] * nt,
        out_specs=(pl.BlockSpec((tq, gw), lambda g, qi, ki: (qi, g)),
                   pl.BlockSpec((8, 4 * tq), lambda g, qi, ki: (g * nq + qi, 0)), *([ANY] * nt)),
        scratch_shapes=[pltpu.VMEM((4 * tq, LANES), BF16), pltpu.VMEM((1, 4 * tq), F32),
                        pltpu.VMEM((LANES, 4 * tq), F32)] + plan.scratch,
        compiler_params=_cp(("arbitrary", "arbitrary", "arbitrary")), name=name)(q, k, vt, *shards)


def _flash_bwd(q, k, kt, v, do, o, lse, plan, grads, name):
    seq = q.shape[0]
    tq = _pick(seq, (FLASH_TQ, 128))
    tk = _pick(seq, (FLASH_TK, 512, 256, 128))
    nq, nkv = seq // tq, seq // tk
    gw = 4 * HEAD_DIM
    nt = plan.nt

    def body(q_ref, k_ref, kt_ref, v_ref, do_ref, o_ref, lse_ref, *rest):
        dq_ref, dk_ref, dv_ref = rest[nt:nt + 3]
        q4_ref, do4_ref, delta_ref, dqt_ref = rest[2 * nt + 3:2 * nt + 7]
        plan.bind(rest[:nt], rest[nt + 3:2 * nt + 3], *rest[2 * nt + 7:])
        g, qi, ki = pl.program_id(0), pl.program_id(1), pl.program_id(2)
        off0 = (g % 2) == 0

        @pl.when(jnp.logical_and(g == 0, jnp.logical_and(qi == 0, ki == 0)))
        def _():
            plan.start()

        @pl.when(jnp.logical_and(g % 2 == 0, jnp.logical_and(qi == 0, ki == 0)))
        def _():
            dk_ref[...] = jnp.zeros(dk_ref.shape, F32)
            dv_ref[...] = jnp.zeros(dv_ref.shape, F32)

        @pl.when(ki == 0)
        def _():
            q4_ref[...] = _slot(q_ref[...], off0, tq).astype(BF16)
            do4 = _slot(do_ref[...], off0, tq)
            do4_ref[...] = do4.astype(BF16)
            o4 = _slot(o_ref[...], off0, tq)
            delta_ref[...] = jnp.sum((do4 * o4).T, axis=0, keepdims=True)
            dqt_ref[...] = jnp.zeros(dqt_ref.shape, F32)

        q4, do4 = q4_ref[...], do4_ref[...]
        st = lax.dot_general(k_ref[...], q4, (((1,), (1,)), ((), ())), preferred_element_type=F32)
        pt = jnp.exp2(st - lse_ref[0:1, :])
        dpt = lax.dot_general(v_ref[...], do4, (((1,), (1,)), ((), ())), preferred_element_type=F32)
        dst = (pt * (dpt - delta_ref[...])).astype(BF16)
        rows = pl.ds(pl.multiple_of(ki * tk, tk), tk)
        dv_ref[rows, :] += jnp.dot(pt.astype(BF16), do4, preferred_element_type=F32)
        dk_ref[rows, :] += jnp.dot(dst, q4, preferred_element_type=F32) * (1.0 / LOG2E)
        dqt_ref[...] += jnp.dot(kt_ref[...], dst, preferred_element_type=F32)

        @pl.when(ki == nkv - 1)
        def _():
            dq_ref[...] = _unslot(dqt_ref[...].T, off0, tq)

        @pl.when(jnp.logical_and(g == N_KV - 1, jnp.logical_and(qi == nq - 1, ki == nkv - 1)))
        def _():
            plan.finish()

    return pl.pallas_call(
        body,
        out_shape=(jax.ShapeDtypeStruct((seq, N_HEADS * HEAD_DIM), F32),
                   jax.ShapeDtypeStruct((seq, N_KV * HEAD_DIM), F32), jax.ShapeDtypeStruct((seq, N_KV * HEAD_DIM), F32),
                   *plan.out_shape),
        grid=(N_KV, nq, nkv),
        in_specs=[pl.BlockSpec((tq, gw), lambda g, qi, ki: (qi, g)),
                  pl.BlockSpec((tk, LANES), lambda g, qi, ki: (ki, g // 2)),
                  pl.BlockSpec((LANES, tk), lambda g, qi, ki: (g // 2, ki)),
                  pl.BlockSpec((tk, LANES), lambda g, qi, ki: (ki, g // 2)),
                  pl.BlockSpec((tq, gw), lambda g, qi, ki: (qi, g)),
                  pl.BlockSpec((tq, gw), lambda g, qi, ki: (qi, g)),
                  pl.BlockSpec((8, 4 * tq), lambda g, qi, ki: (g * nq + qi, 0))] + [ANY] * nt,
        out_specs=(pl.BlockSpec((tq, gw), lambda g, qi, ki: (qi, g)),
                   pl.BlockSpec((seq, LANES), lambda g, qi, ki: (0, g // 2)),
                   pl.BlockSpec((seq, LANES), lambda g, qi, ki: (0, g // 2)), *([ANY] * nt)),
        scratch_shapes=[pltpu.VMEM((4 * tq, LANES), BF16), pltpu.VMEM((4 * tq, LANES), BF16),
                        pltpu.VMEM((1, 4 * tq), F32), pltpu.VMEM((LANES, 4 * tq), F32)] + plan.scratch,
        compiler_params=_cp(("arbitrary", "arbitrary", "arbitrary")), name=name)(q, k, kt, v, do, o, lse, *grads)


def _xattn_fwd(q, kv, name):
    seq, d = q.shape
    mlen = kv.shape[0]
    tq = _pick(seq, (512, 256))

    def body(q_ref, k_ref, v_ref, o_ref):
        for h in range(X_HEADS):
            sl = slice(h * X_HEAD_DIM, (h + 1) * X_HEAD_DIM)
            s = lax.dot_general(q_ref[:, sl], k_ref[:, sl], (((1,), (1,)), ((), ())), preferred_element_type=F32)
            e = jnp.exp(s - jnp.max(s, axis=-1, keepdims=True))
            p = e / jnp.sum(e, axis=-1, keepdims=True)
            o_ref[:, sl] = jnp.dot(p.astype(BF16), v_ref[:, sl], preferred_element_type=F32).astype(o_ref.dtype)

    return pl.pallas_call(
        body, out_shape=jax.ShapeDtypeStruct((seq, d), BF16), grid=(seq // tq,),
        in_specs=[pl.BlockSpec((tq, d), lambda i: (i, 0)), pl.BlockSpec((mlen, d), lambda i: (0, 0)),
                  pl.BlockSpec((mlen, d), lambda i: (0, 1))],
        out_specs=pl.BlockSpec((tq, d), lambda i: (i, 0)), compiler_params=_cp(("parallel",)), name=name)(q, kv, kv)


def _xattn_bwd(q, kv, do, name):
    seq, d = q.shape
    mlen = kv.shape[0]
    tq = _pick(seq, (512, 256))
    scale = X_HEAD_DIM ** -0.5

    def body(q_ref, k_ref, v_ref, do_ref, dq_ref, dkv_ref):
        i = pl.program_id(0)

        @pl.when(i == 0)
        def _():
            dkv_ref[...] = jnp.zeros(dkv_ref.shape, F32)

        for h in range(X_HEADS):
            sl = slice(h * X_HEAD_DIM, (h + 1) * X_HEAD_DIM)
            qh, kh, vh = q_ref[:, sl], k_ref[:, sl], v_ref[:, sl]
            doh = do_ref[:, sl].astype(BF16)
            st = lax.dot_general(kh, qh, (((1,), (1,)), ((), ())), preferred_element_type=F32)
            e = jnp.exp(st - jnp.max(st, axis=0, keepdims=True))
            pt = e / jnp.sum(e, axis=0, keepdims=True)
            dpt = lax.dot_general(vh, doh, (((1,), (1,)), ((), ())), preferred_element_type=F32)
            dst = (pt * (dpt - jnp.sum(pt * dpt, axis=0, keepdims=True))).astype(BF16)
            dkv_ref[:, sl] += jnp.dot(dst, qh, preferred_element_type=F32)
            dkv_ref[:, d + h * X_HEAD_DIM:d + (h + 1) * X_HEAD_DIM] += jnp.dot(pt.astype(BF16), doh,
                                                                                 preferred_element_type=F32)
            dqh = lax.dot_general(dst, kh, (((0,), (0,)), ((), ())), preferred_element_type=F32)
            dq_ref[:, sl] = (dqh * scale).astype(dq_ref.dtype)

    return pl.pallas_call(
        body, out_shape=(jax.ShapeDtypeStruct((seq, d), BF16), jax.ShapeDtypeStruct((mlen, 2 * d), F32)),
        grid=(seq // tq,),
        in_specs=[pl.BlockSpec((tq, d), lambda i: (i, 0)), pl.BlockSpec((mlen, d), lambda i: (0, 0)),
                  pl.BlockSpec((mlen, d), lambda i: (0, 1)), pl.BlockSpec((tq, d), lambda i: (i, 0))],
        out_specs=(pl.BlockSpec((tq, d), lambda i: (i, 0)), pl.BlockSpec((mlen, 2 * d), lambda i: (0, 0))),
        compiler_params=_cp(("arbitrary",)), name=name)(q, kv, kv, do)


def _halo_specs(tr, tc, seq, col):
    per, last = tr // HALO, seq // HALO - 1
    return [pl.BlockSpec((tr, tc), lambda j, r: (r, col(j))),
            pl.BlockSpec((HALO, tc), lambda j, r: (jnp.maximum(r * per - 1, 0), col(j))),
            pl.BlockSpec((HALO, tc), lambda j, r: (jnp.minimum((r + 1) * per, last), col(j)))]


def _extend(main_ref, prev_ref, next_ref, r, nr):
    pv = (r > 0).astype(F32)
    nv = (r < nr - 1).astype(F32)
    return jnp.concatenate([prev_ref[...].astype(F32) * pv, main_ref[...].astype(F32),
                            next_ref[...].astype(F32) * nv], axis=0)


def _conv3(e, w_ref, n):
    return pltpu.roll(e, 1, axis=0) * w_ref[0:1, :] + e * w_ref[1:2, :] + pltpu.roll(e, n - 1, axis=0) * w_ref[2:3, :]


def _conv_gate_fwd(ug, uv, cw, cb, layer, name):
    seq, f = ug.shape
    tc = 256
    tr = _pick(seq, (512, 256))
    nc, nr = f // tc, seq // tr
    n = tr + 2 * HALO

    def body(g_ref, gp_ref, gn_ref, v_ref, vp_ref, vn_ref, wg_ref, wv_ref, bg_ref, bv_ref, o_ref):
        r = pl.program_id(1)
        cg = _conv3(_extend(g_ref, gp_ref, gn_ref, r, nr), wg_ref, n)[HALO:HALO + tr] + bg_ref[...]
        cv = _conv3(_extend(v_ref, vp_ref, vn_ref, r, nr), wv_ref, n)[HALO:HALO + tr] + bv_ref[...]
        o_ref[...] = (cg * jax.nn.sigmoid(cg) * cv).astype(o_ref.dtype)

    w_spec = lambda shift: pl.BlockSpec((None, 3, tc), lambda j, r: (layer, 0, j + shift))
    b_spec = lambda shift: pl.BlockSpec((None, 1, tc), lambda j, r: (layer, 0, j + shift))
    return pl.pallas_call(
        body, out_shape=jax.ShapeDtypeStruct((seq, f), BF16), grid=(nc, nr),
        in_specs=_halo_specs(tr, tc, seq, lambda j: j) * 2 + [w_spec(0), w_spec(nc), b_spec(0), b_spec(nc)],
        out_specs=pl.BlockSpec((tr, tc), lambda j, r: (r, j)),
        compiler_params=_cp(("parallel", "parallel")), name=name)(ug, ug, ug, uv, uv, uv, cw, cw, cb, cb)


def _conv_gate_bwd(ug, uv, dact, cw, cb, layer, name):
    seq, f = ug.shape
    tc = 256
    tr = _pick(seq, (512, 256))
    nc, nr = f // tc, seq // tr
    n = tr + 2 * HALO

    def body(g_ref, gp_ref, gn_ref, v_ref, vp_ref, vn_ref, d_ref, dp_ref, dn_ref, wg_ref, wv_ref, bg_ref, bv_ref,
             dug_ref, duv_ref, dwg_ref, dwv_ref):
        r = pl.program_id(1)
        eg = _extend(g_ref, gp_ref, gn_ref, r, nr)
        ev = _extend(v_ref, vp_ref, vn_ref, r, nr)
        da = _extend(d_ref, dp_ref, dn_ref, r, nr)
        eg3 = (pltpu.roll(eg, 1, axis=0), eg, pltpu.roll(eg, n - 1, axis=0))
        ev3 = (pltpu.roll(ev, 1, axis=0), ev, pltpu.roll(ev, n - 1, axis=0))
        cg = eg3[0] * wg_ref[0:1, :] + eg3[1] * wg_ref[1:2, :] + eg3[2] * wg_ref[2:3, :] + bg_ref[...]
        cv = ev3[0] * wv_ref[0:1, :] + ev3[1] * wv_ref[1:2, :] + ev3[2] * wv_ref[2:3, :] + bv_ref[...]
        sg = jax.nn.sigmoid(cg)
        dcv = da * (cg * sg)
        dcg = da * cv * (sg * (1.0 + cg * (1.0 - sg)))

        def back(dc, e3, w_ref, du_ref, dw_ref):
            du = (pltpu.roll(dc, n - 1, axis=0) * w_ref[0:1, :] + dc * w_ref[1:2, :]
                  + pltpu.roll(dc, 1, axis=0) * w_ref[2:3, :])
            du_ref[...] = du[HALO:HALO + tr].astype(du_ref.dtype)
            dcm = dc[HALO:HALO + tr]
            taps = [jnp.sum(dcm * e[HALO:HALO + tr], axis=0, keepdims=True) for e in e3] + [
                    jnp.sum(dcm, axis=0, keepdims=True)]
            part = jnp.concatenate(taps + [jnp.zeros((4, tc), F32)], axis=0)

            @pl.when(r == 0)
            def _():
                dw_ref[...] = part

            @pl.when(r > 0)
            def _():
                dw_ref[...] += part

        back(dcg, eg3, wg_ref, dug_ref, dwg_ref)
        back(dcv, ev3, wv_ref, duv_ref, dwv_ref)

    w_spec = lambda shift: pl.BlockSpec((None, 3, tc), lambda j, r: (layer, 0, j + shift))
    b_spec = lambda shift: pl.BlockSpec((None, 1, tc), lambda j, r: (layer, 0, j + shift))
    out_rows = pl.BlockSpec((tr, tc), lambda j, r: (r, j))
    out_acc = pl.BlockSpec((8, tc), lambda j, r: (0, j))
    return pl.pallas_call(
        body,
        out_shape=(jax.ShapeDtypeStruct((seq, f), BF16), jax.ShapeDtypeStruct((seq, f), BF16),
                   jax.ShapeDtypeStruct((8, f), F32), jax.ShapeDtypeStruct((8, f), F32)),
        grid=(nc, nr),
        in_specs=_halo_specs(tr, tc, seq, lambda j: j) * 3 + [w_spec(0), w_spec(nc), b_spec(0), b_spec(nc)],
        out_specs=(out_rows, out_rows, out_acc, out_acc),
        compiler_params=_cp(("parallel", "arbitrary")), name=name)(ug, ug, ug, uv, uv, uv, dact, dact, dact, cw, cw, cb, cb)


def _pool_count(g, r, tr, n, seq):
    half = jnp.left_shift(1, g)
    t = r * tr - HALO + lax.broadcasted_iota(jnp.int32, (n, 1), 0)
    cnt = jnp.minimum(t + half, seq) - jnp.maximum(t - half, 0)
    return jnp.maximum(cnt, 1).astype(F32)


def _by_group(g, levels):
    out = levels[3]
    for i in (2, 1, 0):
        out = jnp.where(g == i, levels[i], out)
    return out


def _pool_mixed(e, g, cnt, n):
    w2 = e + pltpu.roll(e, 1, axis=0)
    w4 = pltpu.roll(w2, 1, axis=0) + pltpu.roll(w2, n - 1, axis=0)
    w8 = pltpu.roll(w4, 2, axis=0) + pltpu.roll(w4, n - 2, axis=0)
    w16 = pltpu.roll(w8, 4, axis=0) + pltpu.roll(w8, n - 4, axis=0)
    return _by_group(g, (w2, w4, w8, w16)) / cnt - e


def _pool_fwd(hp, xres, pw, scale, name):
    seq, d = hp.shape
    tc = POOL_GROUP_W
    tr = _pick(seq, (512, 256))
    nr = seq // tr
    n = tr + 2 * HALO

    def body(h_ref, hp_ref, hn_ref, x_ref, w_ref, s_ref, o_ref):
        g, r = pl.program_id(0), pl.program_id(1)
        e = _extend(h_ref, hp_ref, hn_ref, r, nr)
        mixed = _pool_mixed(e, g, _pool_count(g, r, tr, n, seq), n)[HALO:HALO + tr]
        y = jnp.dot(mixed.astype(BF16), w_ref[...], preferred_element_type=F32)
        o_ref[...] = x_ref[...] + y * s_ref[...]

    return pl.pallas_call(
        body, out_shape=jax.ShapeDtypeStruct((seq, d), F32), grid=(POOL_GROUPS, nr),
        in_specs=_halo_specs(tr, tc, seq, lambda j: j) + [
            pl.BlockSpec((tr, tc), lambda j, r: (r, j)), pl.BlockSpec((None, tc, tc), lambda j, r: (j, 0, 0)),
            pl.BlockSpec((1, tc), lambda j, r: (0, j))],
        out_specs=pl.BlockSpec((tr, tc), lambda j, r: (r, j)),
        compiler_params=_cp(("parallel", "parallel")), name=name)(hp, hp, hp, xres, pw, scale)


def _pool_bwd(hp, dy, pw, scale, name):
    seq, d = hp.shape
    tc = POOL_GROUP_W
    tr = _pick(seq, (512, 256))
    nr = seq // tr
    n = tr + 2 * HALO

    def body(h_ref, hp_ref, hn_ref, d_ref, dp_ref, dn_ref, w_ref, s_ref, dh_ref, dw_ref, ds_ref):
        g, r = pl.program_id(0), pl.program_id(1)
        cnt = _pool_count(g, r, tr, n, seq)
        e = _extend(h_ref, hp_ref, hn_ref, r, nr)
        mixed = _pool_mixed(e, g, cnt, n)[HALO:HALO + tr].astype(BF16)
        dye = _extend(d_ref, dp_ref, dn_ref, r, nr)
        dyp = (dye * s_ref[...]).astype(BF16)
        dmixed = lax.dot_general(dyp, w_ref[...], (((1,), (1,)), ((), ())), preferred_element_type=F32)
        dwin = dmixed / cnt
        m2 = dwin + pltpu.roll(dwin, n - 1, axis=0)
        m4 = pltpu.roll(m2, 1, axis=0) + pltpu.roll(m2, n - 1, axis=0)
        m8 = pltpu.roll(m4, 2, axis=0) + pltpu.roll(m4, n - 2, axis=0)
        m16 = pltpu.roll(m8, 4, axis=0) + pltpu.roll(m8, n - 4, axis=0)
        dh_ref[...] = (_by_group(g, (m2, m4, m8, m16)) - dmixed)[HALO:HALO + tr]
        ypre = jnp.dot(mixed, w_ref[...], preferred_element_type=F32)
        dsp = jnp.sum(d_ref[...] * ypre, axis=0, keepdims=True)
        dwp = lax.dot_general(mixed, dyp[HALO:HALO + tr], (((0,), (0,)), ((), ())), preferred_element_type=F32)

        @pl.when(r == 0)
        def _():
            dw_ref[...] = dwp
            ds_ref[...] = dsp

        @pl.when(r > 0)
        def _():
            dw_ref[...] += dwp
            ds_ref[...] += dsp

    return pl.pallas_call(
        body,
        out_shape=(jax.ShapeDtypeStruct((seq, d), F32), jax.ShapeDtypeStruct((POOL_GROUPS, tc, tc), F32),
                   jax.ShapeDtypeStruct((1, d), F32)),
        grid=(POOL_GROUPS, nr),
        in_specs=_halo_specs(tr, tc, seq, lambda j: j) * 2 + [
            pl.BlockSpec((None, tc, tc), lambda j, r: (j, 0, 0)), pl.BlockSpec((1, tc), lambda j, r: (0, j))],
        out_specs=(pl.BlockSpec((tr, tc), lambda j, r: (r, j)), pl.BlockSpec((None, tc, tc), lambda j, r: (j, 0, 0)),
                   pl.BlockSpec((1, tc), lambda j, r: (0, j))),
        compiler_params=_cp(("parallel", "arbitrary")), name=name)(hp, hp, hp, dy, dy, dy, pw, scale)


def _adamw_math(w, g, m, v):
    m = ADAM_B1 * m + (1.0 - ADAM_B1) * g
    v = ADAM_B2 * v + (1.0 - ADAM_B2) * (g * g)
    m_hat = m / (1.0 - ADAM_B1 ** ADAM_STEP)
    v_hat = v / (1.0 - ADAM_B2 ** ADAM_STEP)
    delta = -ADAM_LR * (m_hat / (jnp.sqrt(v_hat) + ADAM_EPS) + ADAM_WD * w)
    return delta, m, v


def _adamw(w, ga, gb, m, v, name):
    rows, cols = w.shape
    tr = _pick(rows, (256, 128, 64, 32, 16, 8))
    two = gb is not None

    def body(*refs):
        if two:
            w_ref, ga_ref, gb_ref, m_ref, v_ref, g_out, d_out, m_out, v_out = refs
            g = ga_ref[...] + gb_ref[...]
        else:
            w_ref, ga_ref, m_ref, v_ref, g_out, d_out, m_out, v_out = refs
            g = ga_ref[...]
        delta, m, v = _adamw_math(w_ref[...], g, m_ref[...], v_ref[...])
        g_out[...] = g
        d_out[...] = delta
        m_out[...] = m
        v_out[...] = v

    spec = pl.BlockSpec((tr, cols), lambda i: (i, 0))
    ops = [w, ga] + ([gb] if two else []) + [m, v]
    return pl.pallas_call(
        body, out_shape=tuple(jax.ShapeDtypeStruct((rows, cols), F32) for _ in range(4)), grid=(rows // tr,),
        in_specs=[spec] * len(ops), out_specs=(spec,) * 4, compiler_params=_cp(("parallel",)), name=name)(*ops)


def _sum4(parts, name):
    _, rows, cols = parts.shape
    tr = _pick(rows, (256, 128, 64, 32, 16))

    def body(p_ref, o_ref):
        acc = p_ref[0].astype(F32)
        for kk in range(1, 4):
            acc = acc + p_ref[kk].astype(F32)
        o_ref[...] = acc

    return pl.pallas_call(
        body, out_shape=jax.ShapeDtypeStruct((rows, cols), F32), grid=(rows // tr,),
        in_specs=[pl.BlockSpec((4, tr, cols), lambda i: (0, i, 0))], out_specs=pl.BlockSpec((tr, cols), lambda i: (i, 0)),
        compiler_params=_cp(("parallel",)), name=name)(parts)


def _place():
    x, y, c = lax.axis_index("x"), lax.axis_index("y"), lax.axis_index("c")
    chips = [(1 - x, y), (x, 1 - y), (1 - x, 1 - y)]
    return x, y, c, chips


def _window(ref, axis, j, size, c=None, half=None, lead=()):
    if axis == "r":
        if c is None:
            return ref.at[lead + (slice(None), pl.ds(pl.multiple_of(j * size, 32), size), slice(None))]
        return ref.at[lead + (slice(None), pl.ds(pl.multiple_of(j * size + c * half, 32), half), slice(None))]
    cols = pl.ds(pl.multiple_of(j * size, LANES), size)
    if c is None:
        return ref.at[lead + (slice(None), slice(None), cols)]
    return ref.at[lead + (slice(None), pl.ds(pl.multiple_of(c * half, 32), half), cols)]


class _Gather:
    def __init__(self, shards, axes):
        self.nt, self.axes = len(shards), axes
        self.out_shape, self.sizes, self.halves = [], [], []
        for s, ax in zip(shards, axes):
            l, rs, cs = s.shape
            self.out_shape.append(jax.ShapeDtypeStruct((l, 4 * rs, cs) if ax == "r" else (l, rs, 4 * cs), s.dtype))
            self.sizes.append(rs if ax == "r" else cs)
            self.halves.append(rs // 2)
        self.scratch = [pltpu.SemaphoreType.DMA((6 * self.nt,)), pltpu.SemaphoreType.DMA((6 * self.nt,)),
                        pltpu.SemaphoreType.DMA((self.nt,))]

    def bind(self, src, dst, send_sems, recv_sems, local_sems):
        self.src, self.dst, self.send_sems, self.recv_sems, self.local_sems = src, dst, send_sems, recv_sems, local_sems

    def _win(self, t, j, core=None):
        return _window(self.dst[t], self.axes[t], j, self.sizes[t], core, self.halves[t])

    def _ici(self, t, kk, origin):
        _, _, c, chips = _place()
        px, py = chips[kk]
        half = self.src[t].at[:, pl.ds(pl.multiple_of(c * self.halves[t], 16), self.halves[t]), :]
        return pltpu.make_async_remote_copy(
            src_ref=half, dst_ref=self._win(t, origin, c), send_sem=self.send_sems.at[t * 3 + kk],
            recv_sem=self.recv_sems.at[t * 3 + kk], device_id=(px, py, c), device_id_type=MESH)

    def _d2d(self, t, kk, origin, core):
        x, y, c, _ = _place()
        k2 = 3 * self.nt + t * 3 + kk
        return pltpu.make_async_remote_copy(
            src_ref=self._win(t, origin, core), dst_ref=self._win(t, origin, core), send_sem=self.send_sems.at[k2],
            recv_sem=self.recv_sems.at[k2], device_id=(x, y, 1 - c), device_id_type=MESH)

    def _local(self, t):
        x, y, _, _ = _place()
        return pltpu.make_async_copy(self.src[t], self._win(t, 2 * x + y), self.local_sems.at[t])

    def _each(self):
        _, _, _, chips = _place()
        for t in range(self.nt):
            for kk in range(3):
                px, py = chips[kk]
                yield t, kk, 2 * px + py

    def start(self):
        x, y, _, _ = _place()
        for t in range(self.nt):
            self._local(t).start()
        for t, kk, _ in self._each():
            self._ici(t, kk, 2 * x + y).start()

    def forward(self):
        _, _, c, _ = _place()
        for t, kk, origin in self._each():
            self._ici(t, kk, origin).wait_recv()
            self._d2d(t, kk, origin, c).start()

    def finish(self):
        x, y, c, _ = _place()
        for t, kk, origin in self._each():
            self._d2d(t, kk, origin, 1 - c).wait_recv()
        for t, kk, origin in self._each():
            self._ici(t, kk, 2 * x + y).wait_send()
            self._d2d(t, kk, origin, c).wait_send()
        for t in range(self.nt):
            self._local(t).wait()


class _Scatter:
    def __init__(self, grads, axes):
        self.nt, self.axes = len(grads), axes
        self.out_shape, self.sizes = [], []
        for gr, ax in zip(grads, axes):
            l, r, cc = gr.shape
            self.out_shape.append(jax.ShapeDtypeStruct((4, l, r // 4, cc) if ax == "r" else (4, l, r, cc // 4), gr.dtype))
            self.sizes.append(r // 4 if ax == "r" else cc // 4)
        self.scratch = [pltpu.SemaphoreType.DMA((3 * self.nt,)), pltpu.SemaphoreType.DMA((3 * self.nt,)),
                        pltpu.SemaphoreType.DMA((self.nt,))]

    def bind(self, src, dst, send_sems, recv_sems, local_sems):
        self.src, self.dst, self.send_sems, self.recv_sems, self.local_sems = src, dst, send_sems, recv_sems, local_sems

    def _copy(self, t, kk, slot):
        x, y, c, chips = _place()
        px, py = chips[kk]
        return pltpu.make_async_remote_copy(
            src_ref=_window(self.src[t], self.axes[t], 2 * px + py, self.sizes[t]), dst_ref=self.dst[t].at[slot],
            send_sem=self.send_sems.at[t * 3 + kk], recv_sem=self.recv_sems.at[t * 3 + kk],
            device_id=(px, py, c), device_id_type=MESH)

    def _local(self, t):
        x, y, _, _ = _place()
        me = 2 * x + y
        return pltpu.make_async_copy(_window(self.src[t], self.axes[t], me, self.sizes[t]), self.dst[t].at[me],
                                     self.local_sems.at[t])

    def start(self):
        x, y, _, _ = _place()
        for t in range(self.nt):
            self._local(t).start()
            for kk in range(3):
                self._copy(t, kk, 2 * x + y).start()

    def finish(self):
        _, _, _, chips = _place()
        for t in range(self.nt):
            for kk in range(3):
                px, py = chips[kk]
                self._copy(t, kk, 2 * px + py).wait_recv()
        for t in range(self.nt):
            for kk in range(3):
                px, py = chips[kk]
                self._copy(t, kk, 2 * px + py).wait_send()
            self._local(t).wait()


def _comm_call(plan, operands, name):
    nt = plan.nt

    def body(*refs):
        plan.bind(refs[:nt], refs[nt:2 * nt], *refs[2 * nt:])
        plan.start()
        if hasattr(plan, "forward"):
            plan.forward()
        plan.finish()

    return pl.pallas_call(body, out_shape=tuple(plan.out_shape), in_specs=[ANY] * nt, out_specs=tuple([ANY] * nt),
                          scratch_shapes=plan.scratch, name=name)(*operands)


def _swap_sibling(arrs, name):
    nt = len(arrs)

    def body(*refs):
        src, dst = refs[:nt], refs[nt:2 * nt]
        send_sems, recv_sems = refs[2 * nt:]
        x, y, c, _ = _place()
        cps = [pltpu.make_async_remote_copy(src_ref=src[t], dst_ref=dst[t], send_sem=send_sems.at[t],
                                            recv_sem=recv_sems.at[t], device_id=(x, y, 1 - c), device_id_type=MESH)
               for t in range(nt)]
        for cp in cps:
            cp.start()
        for cp in cps:
            cp.wait()

    return pl.pallas_call(
        body, out_shape=tuple(jax.ShapeDtypeStruct(a.shape, a.dtype) for a in arrs), in_specs=[ANY] * nt,
        out_specs=tuple([ANY] * nt),
        scratch_shapes=[pltpu.SemaphoreType.DMA((nt,)), pltpu.SemaphoreType.DMA((nt,))], name=name)(*arrs)


def _gather8(pack, with_sum, name):
    rows = pack.shape[0]
    flips = [f for f in itertools.product((0, 1), repeat=3) if any(f)]

    def body(p_ref, all_ref, *rest):
        if with_sum:
            sum_ref, send_sems, recv_sems = rest
        else:
            send_sems, recv_sems = rest
        x, y, c, _ = _place()
        me = 4 * x + 2 * y + c

        def peer(f):
            return tuple(1 - v if fl else v for v, fl in zip((x, y, c), f))

        all_ref[me] = p_ref[...]
        cps = []
        for kk, f in enumerate(flips):
            cp = pltpu.make_async_remote_copy(src_ref=p_ref, dst_ref=all_ref.at[me], send_sem=send_sems.at[kk],
                                              recv_sem=recv_sems.at[kk], device_id=peer(f), device_id_type=MESH)
            cp.start()
            cps.append(cp)
        for kk, f in enumerate(flips):
            px, py, pc = peer(f)
            pltpu.make_async_remote_copy(src_ref=p_ref, dst_ref=all_ref.at[4 * px + 2 * py + pc],
                                         send_sem=send_sems.at[kk], recv_sem=recv_sems.at[kk], device_id=peer(f),
                                         device_id_type=MESH).wait_recv()
        for cp in cps:
            cp.wait_send()
        if with_sum:
            acc = all_ref[0]
            for d in range(1, 8):
                acc = acc + all_ref[d]
            sum_ref[...] = acc

    vm = pl.BlockSpec(memory_space=pltpu.VMEM)
    out_shape = [jax.ShapeDtypeStruct((8, rows, LANES), F32)] + ([jax.ShapeDtypeStruct((rows, LANES), F32)] if with_sum else [])
    return pl.pallas_call(
        body, out_shape=tuple(out_shape), in_specs=[vm], out_specs=tuple([vm] * len(out_shape)),
        scratch_shapes=[pltpu.SemaphoreType.DMA((7,)), pltpu.SemaphoreType.DMA((7,))], name=name)(pack)


def _pack(arrs):
    flat = jnp.concatenate([a.reshape(-1).astype(F32) for a in arrs])
    rows = -(-flat.shape[0] // (8 * LANES)) * 8
    return jnp.pad(flat, (0, rows * LANES - flat.shape[0])).reshape(rows, LANES)


def _unpack(flat, shapes):
    out, pos = [], 0
    for shp in shapes:
        size = 1
        for s in shp:
            size *= s
        out.append(flat[pos:pos + size].reshape(shp))
        pos += size
    return out


BIG = ("attn_w_qkv", "attn_w_o", "pool_w", "xattn_w_q", "xattn_w_kv", "xattn_w_o", "ffn_w_up", "ffn_w_down")
BIG_AXIS = ("c", "r", "r", "r", "c", "r", "c", "r")
SMALL_REPL = ("attn_norm", "attn_q_gain", "attn_k_gain", "xattn_norm", "mem_norm", "ffn_norm", "ffn_conv_b", "final_norm")
SMALL_SHARD = ("pool_norm", "pool_scale", "ffn_conv_w")
ORDER = ("attn_norm", "attn_w_qkv", "attn_q_gain", "attn_k_gain", "attn_w_o", "pool_norm", "pool_w", "pool_scale",
         "xattn_norm", "mem_norm", "xattn_w_q", "xattn_w_kv", "xattn_w_o", "ffn_norm", "ffn_w_up", "ffn_conv_w",
         "ffn_conv_b", "ffn_w_down", "final_norm")


def _step(x, mem, tgt, w, m, v):
    seq, d = x.shape
    xi, yi, ci = lax.axis_index("x"), lax.axis_index("y"), lax.axis_index("c")
    chip = 2 * xi + yi
    dff = w["ffn_w_down"].shape[1] * 4
    n_layers = w["ffn_norm"].shape[0]

    def as3d(a):
        return a.reshape(a.shape[-3:])
    shards = [as3d(w[nm]).astype(BF16) for nm in BIG]
    (wq,) = _comm_call(_Gather(shards[:1], BIG_AXIS[:1]), shards[:1], "gather_qkv")
    small_in = [w[nm] for nm in SMALL_SHARD]
    (small_all,) = _gather8(_pack(small_in), False, "gather_small")
    per_chip = [_unpack(small_all[2 * j].reshape(-1), [a.shape for a in small_in]) for j in range(4)]
    pool_norm, pool_scale, conv_w = (jnp.concatenate([per_chip[j][i] for j in range(4)], axis=-1) for i in range(3))

    conv_b = w["ffn_conv_b"].reshape(n_layers, 1, -1)
    tabs = _rope_tables(seq)
    qg2 = jnp.tile(w["attn_q_gain"], (1, 2))
    kg2 = jnp.tile(w["attn_k_gain"], (1, 2))
    mm = functools.partial(_mm)

    saved = {}
    x0 = x
    h0 = _rms_fwd(x0, w["attn_norm"], BF16, "rms_attn")
    qkv = mm(h0, wq, "nn", b_l=0, out_dtype=F32, name="mm_qkv")
    q_r, k_r, k_t, v_b, v_t = _qk_prep(qkv, qg2, kg2, tabs, "qk_prep")
    o_at, lse, wo, wp, wxq, wxkv, wxo, wup, wdn = _flash_fwd(
        q_r, k_r, v_t, _Gather(shards[1:], BIG_AXIS[1:]), shards[1:], "flash_fwd")
    xs = [x0, mm(o_at, wo, "nn", b_l=0, res=x0, out_dtype=F32, name="mm_attn_o")]

    def xattn_fwd(l, xin):
        hq = _rms_fwd(xin, w["xattn_norm"][l:l + 1], BF16, f"rms_xq{l}")
        mn = _rms_fwd(mem, w["mem_norm"][l:l + 1], BF16, f"rms_mem{l}")
        xq = mm(hq, wxq, "nn", b_l=l, scale=X_HEAD_DIM ** -0.5, out_dtype=BF16, name=f"mm_xq{l}")
        kv = mm(mn, wxkv, "nn", b_l=l, out_dtype=BF16, name=f"mm_xkv{l}")
        xo = _xattn_fwd(xq, kv, f"xattn_fwd{l}")
        saved[f"x{l}"] = (hq, mn, xq, kv, xo)
        return mm(xo, wxo, "nn", b_l=l, res=xin, out_dtype=F32, name=f"mm_xo{l}")

    def ffn_fwd(l, xin):
        hf = _rms_fwd(xin, w["ffn_norm"][l:l + 1], BF16, f"rms_ffn{l}")
        ug = mm(hf, wup, "nn", b_l=l, n=dff, out_dtype=BF16, name=f"mm_up_g{l}")
        uv = mm(hf, wup, "nn", b_l=l, n=dff, b_off=(0, dff), out_dtype=BF16, name=f"mm_up_v{l}")
        act = _conv_gate_fwd(ug, uv, conv_w, conv_b, l, f"conv_gate{l}")
        saved[f"f{l}"] = (hf, ug, uv, act)
        return mm(act, wdn, "nn", b_l=l, res=xin, out_dtype=F32, name=f"mm_down{l}")

    xs.append(xattn_fwd(0, xs[-1]))
    xs.append(ffn_fwd(0, xs[-1]))
    hp = _rms_fwd(xs[-1], pool_norm, F32, "rms_pool")
    xs.append(_pool_fwd(hp, xs[-1], wp, pool_scale, "pool_fwd"))
    xs.append(xattn_fwd(1, xs[-1]))
    xs.append(ffn_fwd(1, xs[-1]))
    dres, g_final, loss = _final_loss(xs[6], w["final_norm"].reshape(1, d), tgt, "final_loss")

    grads = {}
    gbuf = {}

    def dw(nm, a, b, layer, full, off=(0, 0), n=None, tn=None):
        gbuf[nm] = _mm(a, b, "tn", out_dtype=BF16, out_full=full, out_l=layer, out_off=off, n=n, tn=tn,
                       alias=gbuf.get(nm), name=f"dw_{nm}{layer}_{off[1]}")

    def ffn_bwd(l, xin, dres):
        hf, ug, uv, act = saved[f"f{l}"]
        dw("ffn_w_down", act, dres, l, wdn.shape)
        dact = _mm(dres, wdn, "nt", b_l=l, out_dtype=BF16, name=f"mm_dact{l}")
        dug, duv, dwg, dwv = _conv_gate_bwd(ug, uv, dact, conv_w, conv_b, l, f"conv_gate_bwd{l}")
        dw("ffn_w_up", hf, dug, l, wup.shape, tn=1408)
        dw("ffn_w_up", hf, duv, l, wup.shape, off=(0, dff), tn=1408)
        dhf = _mm(dug, wup, "nt", b_l=l, n=d, out_dtype=F32, name=f"mm_dhf_g{l}")
        dhf = _mm(duv, wup, "nt", b_l=l, n=d, b_off=(0, dff), res=dhf, out_dtype=F32, name=f"mm_dhf_v{l}")
        dres, dg = _rms_bwd(xin, w["ffn_norm"][l:l + 1], dhf, dres, f"rms_ffn_bwd{l}")
        return dres, dg, jnp.concatenate([dwg[:3], dwv[:3]], axis=1), jnp.concatenate([dwg[3], dwv[3]], axis=0)

    def xattn_bwd(l, xin, dres):
        hq, mn, xq, kv, xo = saved[f"x{l}"]
        dw("xattn_w_o", xo, dres, l, wxo.shape)
        dxo = _mm(dres, wxo, "nt", b_l=l, out_dtype=BF16, name=f"mm_dxo{l}")
        dq, dkv = _xattn_bwd(xq, kv, dxo, f"xattn_bwd{l}")
        dw("xattn_w_q", hq, dq, l, wxq.shape)
        dhq = _mm(dq, wxq, "nt", b_l=l, out_dtype=F32, name=f"mm_dhq{l}")
        dw("xattn_w_kv", mn, dkv, l, wxkv.shape)
        dmn = _mm(dkv, wxkv, "nt", b_l=l, out_dtype=F32, name=f"mm_dmn{l}")
        _, dg_mem = _rms_bwd(mem, w["mem_norm"][l:l + 1], dmn, None, f"rms_mem_bwd{l}")
        dres, dg = _rms_bwd(xin, w["xattn_norm"][l:l + 1], dhq, dres, f"rms_xq_bwd{l}")
        return dres, dg, dg_mem

    g_ffn, g_xn, g_mn, g_cw, g_cb = [None] * n_layers, [None] * n_layers, [None] * n_layers, [None] * n_layers, [None] * n_layers
    dres, g_ffn[1], g_cw[1], g_cb[1] = ffn_bwd(1, xs[5], dres)
    dres, g_xn[1], g_mn[1] = xattn_bwd(1, xs[4], dres)
    dhp, g_pw, g_pscale = _pool_bwd(hp, dres, wp, pool_scale, "pool_bwd")
    dres, g_pnorm = _rms_bwd(xs[3], pool_norm, dhp, dres, "rms_pool_bwd")
    dres, g_ffn[0], g_cw[0], g_cb[0] = ffn_bwd(0, xs[2], dres)
    dres, g_xn[0], g_mn[0] = xattn_bwd(0, xs[1], dres)
    dw("attn_w_o", o_at, dres, 0, wo.shape)
    do = _mm(dres, wo, "nt", b_l=0, out_dtype=BF16, name="mm_do")
    gbuf["pool_w"] = g_pw.astype(BF16)
    early = [gbuf[nm] for nm in BIG[1:]]
    dq_r, dk_r, dv, *recv_early = _flash_bwd(q_r, k_r, k_t, v_b, do, o_at, lse, _Scatter(early, BIG_AXIS[1:]), early,
                                             "flash_bwd")
    dqkv, dqg, dkg = _qk_prep_bwd(qkv, dq_r, dk_r, dv, qg2, kg2, tabs, "qk_prep_bwd")
    dw("attn_w_qkv", h0, dqkv, 0, wq.shape)
    dh0 = _mm(dqkv, wq, "nt", b_l=0, out_dtype=F32, name="mm_dh0")
    grad_x, g_an = _rms_bwd(x0, w["attn_norm"], dh0, dres, "rms_attn_bwd")

    small_g = {
        "attn_norm": g_an, "attn_q_gain": dqg[:, :HEAD_DIM] + dqg[:, HEAD_DIM:], "attn_k_gain": dkg[:, :HEAD_DIM] + dkg[:, HEAD_DIM:],
        "xattn_norm": jnp.concatenate(g_xn, axis=0), "mem_norm": jnp.concatenate(g_mn, axis=0),
        "ffn_norm": jnp.concatenate(g_ffn, axis=0), "ffn_conv_b": jnp.stack(g_cb, axis=0), "final_norm": g_final.reshape(d),
        "pool_norm": g_pnorm, "pool_scale": g_pscale, "ffn_conv_w": jnp.stack(g_cw, axis=0)}
    names = SMALL_REPL + SMALL_SHARD
    _, total = _gather8(_pack([loss[0, :1]] + [small_g[nm] for nm in names]), True, "reduce_small")
    parts = _unpack(total.reshape(-1), [(1,)] + [small_g[nm].shape for nm in names])
    loss_out = parts[0][0]
    for nm, g in zip(names, parts[1:]):
        if nm in SMALL_SHARD:
            size = w[nm].shape[-1]
            g = lax.dynamic_slice_in_dim(g, chip * size, size, axis=g.ndim - 1)
        grads[nm] = g.reshape(w[nm].shape)

    packed = [_pack([src[nm] for nm in names]) for src in (w, grads, m, v)]
    _, sd, sm, sv = _adamw(packed[0], packed[1], None, packed[2], packed[3], "adamw_small")
    shapes = [w[nm].shape for nm in names]
    delta = dict(zip(names, _unpack(sd.reshape(-1), shapes)))
    new_m = dict(zip(names, _unpack(sm.reshape(-1), shapes)))
    new_v = dict(zip(names, _unpack(sv.reshape(-1), shapes)))

    late = [gbuf[nm] for nm in BIG[:1]]
    recv = list(_comm_call(_Scatter(late, BIG_AXIS[:1]), late, "scatter_qkv")) + recv_early
    sums = []
    for nm, rc in zip(BIG, recv):
        sums.append(_sum4(rc.reshape(4, -1, rc.shape[-1]), f"sum4_{nm}"))
    others = _swap_sibling(sums, "swap_sums")
    for nm, mine, other in zip(BIG, sums, others):
        cols = mine.shape[-1]
        outs = _adamw(w[nm].reshape(-1, cols), mine, other, m[nm].reshape(-1, cols), v[nm].reshape(-1, cols), f"adamw_{nm}")
        grads[nm], delta[nm], new_m[nm], new_v[nm] = (o.reshape(w[nm].shape) for o in outs)

    return loss_out, grad_x, grads, delta, new_m, new_v


def kernel(x, mem, attn_norm, attn_w_qkv, attn_q_gain, attn_k_gain, attn_w_o, pool_norm, pool_w, pool_scale, xattn_norm, mem_norm, xattn_w_q, xattn_w_kv, xattn_w_o, ffn_norm, ffn_w_up, ffn_conv_w, ffn_conv_b, ffn_w_down, final_norm, loss_target, m_attn_norm, m_attn_w_qkv, m_attn_q_gain, m_attn_k_gain, m_attn_w_o, m_pool_norm, m_pool_w, m_pool_scale, m_xattn_norm, m_mem_norm, m_xattn_w_q, m_xattn_w_kv, m_xattn_w_o, m_ffn_norm, m_ffn_w_up, m_ffn_conv_w, m_ffn_conv_b, m_ffn_w_down, m_final_norm, v_attn_norm, v_attn_w_qkv, v_attn_q_gain, v_attn_k_gain, v_attn_w_o, v_pool_norm, v_pool_w, v_pool_scale, v_xattn_norm, v_mem_norm, v_xattn_w_q, v_xattn_w_kv, v_xattn_w_o, v_ffn_norm, v_ffn_w_up, v_ffn_conv_w, v_ffn_conv_b, v_ffn_w_down, v_final_norm):
    given = dict(locals())
    w = {nm: given[nm] for nm in ORDER}
    m = {nm: given["m_" + nm] for nm in ORDER}
    v = {nm: given["v_" + nm] for nm in ORDER}
    seq, d = x.shape[1], x.shape[2]
    loss, grad_x, grads, delta, new_m, new_v = _step(
        x.reshape(seq, d), mem.reshape(mem.shape[1], d), loss_target.reshape(seq, d), w, m, v)
    return (loss, grad_x.reshape(x.shape), *[grads[nm] for nm in ORDER], *[delta[nm] for nm in ORDER],
            *[new_m[nm] for nm in ORDER], *[new_v[nm] for nm in ORDER])
```

```python
import functools
import itertools

import jax
import jax.numpy as jnp
from jax import lax
from jax.experimental import pallas as pl
from jax.experimental.pallas import tpu as pltpu

F32, BF16 = jnp.float32, jnp.bfloat16
EPS = 1e-6
GRID_W = 64
ROPE_THETA = 10000.0
HEAD_DIM = 64
N_HEADS = 16
N_KV = 4
X_HEADS = 4
X_HEAD_DIM = 256
POOL_GROUPS = 4
POOL_GROUP_W = 256
HALO = 16
LANES = 128
ADAM_LR, ADAM_B1, ADAM_B2, ADAM_EPS, ADAM_WD, ADAM_STEP = 0.001, 0.9, 0.999, 1e-08, 0.01, 10
VMEM_LIMIT = 48 * 1024 * 1024
MESH = pl.DeviceIdType.MESH
NEG = -1e30
LOG2E = 1.4426950408889634
FLASH_TQ, FLASH_TK = 256, 2048
ANY = pl.BlockSpec(memory_space=pl.ANY)


def _cp(sem=None):
    return pltpu.CompilerParams(dimension_semantics=sem, vmem_limit_bytes=VMEM_LIMIT)


def _pick(n, cands):
    for c in cands:
        if c <= n and n % c == 0:
            return c
    return n


def _mm(a, b, mode, *, name, out_dtype, tm=None, tn=None, tk=None, n=None, k=None, b_l=None, b_off=(0, 0),
        res=None, scale=None, out_full=None, out_l=None, out_off=(0, 0), alias=None, norm_out=None, norm_bwd=None):
    if mode == "tn":
        K, M = a.shape
    else:
        M, K = a.shape
    bs = b.shape[-2:]
    if mode == "nn":
        K = k or K
        N = n or bs[1]
    elif mode == "nt":
        N = n or bs[0]
    else:
        N = n or bs[1]
    wide = (1408, 1024, 512, 256, 128)
    if mode == "tn":
        tm = tm or (M if M <= 1024 else _pick(M, wide))
        tk = tk or _pick(K, (2048, 1024, 512, 256, 128))
    else:
        tm = _pick(M, (tm or 512, 256, 128))
        tk = tk or (K if K <= 2816 else _pick(K, wide))
    tn = tn or (N if N <= 1536 else _pick(N, wide))
    assert M % tm == 0 and N % tn == 0 and K % tk == 0, (name, M, N, K, tm, tn, tk)
    nk = K // tk
    dims = {"nn": ((1,), (0,)), "nt": ((1,), (1,)), "tn": ((0,), (0,))}[mode]

    j_outer = nk == 1 and mode != "tn"

    def at(f):
        return (lambda j, i, kk: f(i, j, kk)) if j_outer else f

    if mode == "tn":
        a_spec = pl.BlockSpec((tk, tm), at(lambda i, j, kk: (kk, i)))
    else:
        a_spec = pl.BlockSpec((tm, tk), at(lambda i, j, kk: (i, kk)))
    if mode == "nt":
        bb, (d0, d1) = (tn, tk), (b_off[0] // tn, b_off[1] // tk)
        assert b_off[0] % tn == 0 and b_off[1] % tk == 0
        bidx = lambda i, j, kk: (j + d0, kk + d1)
    else:
        bb, (d0, d1) = (tk, tn), (b_off[0] // tk, b_off[1] // tn)
        assert b_off[0] % tk == 0 and b_off[1] % tn == 0
        bidx = lambda i, j, kk: (kk + d0, j + d1)
    if b.ndim == 3:
        b_spec = pl.BlockSpec((None,) + bb, at(lambda i, j, kk: (b_l,) + bidx(i, j, kk)))
    else:
        b_spec = pl.BlockSpec(bb, at(bidx))
    in_specs, operands = [a_spec, b_spec], [a, b]
    if res is not None:
        in_specs.append(pl.BlockSpec((tm, tn), at(lambda i, j, kk: (i, j))))
        operands.append(res)
    aliases = {}
    if alias is not None:
        aliases = {len(operands): 0}
        in_specs.append(ANY)
        operands.append(alias)
    if out_full is None:
        out_shape = jax.ShapeDtypeStruct((M, N), out_dtype)
        out_spec = pl.BlockSpec((tm, tn), at(lambda i, j, kk: (i, j)))
    else:
        assert out_off[0] % tm == 0 and out_off[1] % tn == 0
        o0, o1 = out_off[0] // tm, out_off[1] // tn
        out_shape = jax.ShapeDtypeStruct(out_full, out_dtype)
        out_spec = pl.BlockSpec((None, tm, tn), at(lambda i, j, kk: (out_l, i + o0, j + o1)))
    has_res, has_alias = res is not None, alias is not None
    grid = (N // tn, M // tm, nk) if j_outer else (M // tm, N // tn, nk)
    n_extra = 0
    if norm_out is not None or norm_bwd is not None:
        assert j_outer and tn == N and out_full is None, name
        row = pl.BlockSpec((tm, tn), at(lambda i, j, kk: (i, 0)))
        vec = pl.BlockSpec((1, tn), at(lambda i, j, kk: (0, 0)))
        if norm_out is not None:
            in_specs.append(vec)
            operands.append(norm_out[0])
            n_extra = 1
            out_shape = (out_shape, jax.ShapeDtypeStruct((M, N), norm_out[1]))
            out_spec = (out_spec, row)
        else:
            in_specs += [row, vec, row]
            operands += list(norm_bwd)
            n_extra = 3
            out_shape = (out_shape, jax.ShapeDtypeStruct((1, N), F32))
            out_spec = (out_spec, vec)
    n_out = 1 if n_extra == 0 else 2

    def body(*refs):
        a_ref, b_ref = refs[0], refs[1]
        pos = 2
        res_ref = None
        if has_res:
            res_ref = refs[pos]
            pos += 1
        if has_alias:
            pos += 1
        extra = refs[pos:pos + n_extra]
        pos += n_extra
        o_ref, acc_ref = refs[pos], refs[pos + n_out]
        kk = pl.program_id(2)
        part = lax.dot_general(a_ref[...].astype(BF16), b_ref[...].astype(BF16), (dims, ((), ())),
                               preferred_element_type=F32)

        def finish(acc):
            if scale is not None:
                acc = acc * scale
            if res_ref is not None:
                acc = acc + res_ref[...]
            if norm_out is not None:
                r = lax.rsqrt(jnp.mean(acc * acc, axis=-1, keepdims=True) + EPS)
                refs[pos + 1][...] = (acc * r * extra[0][...]).astype(refs[pos + 1].dtype)
            if norm_bwd is not None:
                x_ref, g_ref, dres_ref = extra
                dg_ref, step = refs[pos + 1], pl.program_id(1)
                xv = x_ref[...]
                r = lax.rsqrt(jnp.mean(xv * xv, axis=-1, keepdims=True) + EPS)
                nv = xv * r
                dgp = jnp.sum(acc * nv, axis=0, keepdims=True)

                @pl.when(step == 0)
                def _():
                    dg_ref[...] = dgp

                @pl.when(step > 0)
                def _():
                    dg_ref[...] += dgp

                dn = acc * g_ref[...]
                acc = dres_ref[...] + r * (dn - nv * jnp.mean(dn * nv, axis=-1, keepdims=True))
            o_ref[...] = acc.astype(o_ref.dtype)

        if nk == 1:
            finish(part)
        else:
            @pl.when(kk == 0)
            def _():
                acc_ref[...] = part

            @pl.when(jnp.logical_and(kk > 0, kk < nk - 1))
            def _():
                acc_ref[...] += part

            @pl.when(kk == nk - 1)
            def _():
                finish(acc_ref[...] + part)

    return pl.pallas_call(
        body, out_shape=out_shape, grid=grid, in_specs=in_specs, out_specs=out_spec,
        scratch_shapes=[pltpu.VMEM((tm, tn) if nk > 1 else (8, 128), F32)], input_output_aliases=aliases,
        compiler_params=_cp(("arbitrary",) * 3 if norm_bwd is not None else ("parallel", "parallel", "arbitrary")),
        name=name)(*operands)


def _rms_fwd(x, gain, out_dtype, name):
    rows, d = x.shape
    tr = _pick(rows, (512, 256))

    def body(x_ref, g_ref, o_ref):
        xv = x_ref[...]
        r = lax.rsqrt(jnp.mean(xv * xv, axis=-1, keepdims=True) + EPS)
        o_ref[...] = (xv * r * g_ref[...]).astype(o_ref.dtype)

    return pl.pallas_call(
        body, out_shape=jax.ShapeDtypeStruct((rows, d), out_dtype), grid=(rows // tr,),
        in_specs=[pl.BlockSpec((tr, d), lambda i: (i, 0)), pl.BlockSpec((1, d), lambda i: (0, 0))],
        out_specs=pl.BlockSpec((tr, d), lambda i: (i, 0)), compiler_params=_cp(("parallel",)), name=name)(x, gain)


def _rms_bwd(x, gain, dh, dres, name):
    rows, d = x.shape
    tr = _pick(rows, (512, 256))
    need_dx = dres is not None

    def body(*refs):
        if need_dx:
            x_ref, g_ref, dh_ref, dres_ref, o_ref, dg_ref = refs
        else:
            x_ref, g_ref, dh_ref, dg_ref = refs
        i = pl.program_id(0)
        xv = x_ref[...]
        dhv = dh_ref[...].astype(F32)
        r = lax.rsqrt(jnp.mean(xv * xv, axis=-1, keepdims=True) + EPS)
        nv = xv * r
        part = jnp.sum(dhv * nv, axis=0, keepdims=True)

        @pl.when(i == 0)
        def _():
            dg_ref[...] = part

        @pl.when(i > 0)
        def _():
            dg_ref[...] += part

        if need_dx:
            dn = dhv * g_ref[...]
            dx = r * (dn - nv * jnp.mean(dn * nv, axis=-1, keepdims=True))
            o_ref[...] = dres_ref[...] + dx

    row_spec = pl.BlockSpec((tr, d), lambda i: (i, 0))
    vec_spec = pl.BlockSpec((1, d), lambda i: (0, 0))
    if need_dx:
        return pl.pallas_call(
            body, out_shape=(jax.ShapeDtypeStruct((rows, d), F32), jax.ShapeDtypeStruct((1, d), F32)),
            grid=(rows // tr,), in_specs=[row_spec, vec_spec, row_spec, row_spec], out_specs=(row_spec, vec_spec),
            compiler_params=_cp(("arbitrary",)), name=name)(x, gain, dh, dres)
    return None, pl.pallas_call(
        body, out_shape=jax.ShapeDtypeStruct((1, d), F32), grid=(rows // tr,),
        in_specs=[row_spec, vec_spec, row_spec], out_specs=vec_spec,
        compiler_params=_cp(("arbitrary",)), name=name)(x, gain, dh)


def _final_loss(x, gain, target, name):
    rows, d = x.shape
    tr = _pick(rows, (512, 256))
    nsteps = rows // tr

    def body(x_ref, g_ref, t_ref, dx_ref, dg_ref, loss_ref, acc_ref):
        i = pl.program_id(0)
        xv = x_ref[...]
        g = g_ref[...]
        r = lax.rsqrt(jnp.mean(xv * xv, axis=-1, keepdims=True) + EPS)
        nv = xv * r
        err = nv * g - t_ref[...]
        dy = err * (1.0 / d)
        dn = dy * g
        dx_ref[...] = r * (dn - nv * jnp.mean(dn * nv, axis=-1, keepdims=True))
        dgp = jnp.sum(dy * nv, axis=0, keepdims=True)
        lp = jnp.sum(err * err, axis=0, keepdims=True)

        @pl.when(i == 0)
        def _():
            dg_ref[...] = dgp
            acc_ref[...] = lp

        @pl.when(i > 0)
        def _():
            dg_ref[...] += dgp
            acc_ref[...] += lp

        @pl.when(i == nsteps - 1)
        def _():
            tot = jnp.sum(acc_ref[...], axis=1, keepdims=True) * (0.5 / d)
            loss_ref[...] = jnp.broadcast_to(tot, loss_ref.shape)

    row_spec = pl.BlockSpec((tr, d), lambda i: (i, 0))
    vec_spec = pl.BlockSpec((1, d), lambda i: (0, 0))
    return pl.pallas_call(
        body, out_shape=(jax.ShapeDtypeStruct((rows, d), F32), jax.ShapeDtypeStruct((1, d), F32),
                         jax.ShapeDtypeStruct((1, LANES), F32)),
        grid=(nsteps,), in_specs=[row_spec, vec_spec, row_spec],
        out_specs=(row_spec, vec_spec, pl.BlockSpec((1, LANES), lambda i: (0, 0))),
        scratch_shapes=[pltpu.VMEM((1, d), F32)], compiler_params=_cp(("arbitrary",)), name=name)(x, gain, target)


def _rope_tables(seq):
    pairs = HEAD_DIM // 4
    lane = jnp.arange(LANES, dtype=jnp.int32) % HEAD_DIM
    by_col, second, pair = lane // (2 * pairs) == 1, (lane % (2 * pairs)) // pairs == 1, lane % pairs
    inv_freq = ROPE_THETA ** (-pair.astype(F32) / pairs)
    t = jnp.arange(seq, dtype=jnp.int32)[:, None]
    pos = jnp.where(by_col[None, :], t % GRID_W, t // GRID_W).astype(F32)
    ang = pos * inv_freq[None, :]
    cos, sin = jnp.cos(ang), jnp.sin(ang)
    return cos, jnp.where(second[None, :], sin, 0.0), jnp.where(second[None, :], 0.0, -sin)


def _pair_norm(xv, lo):
    sq = xv * xv
    s_lo = jnp.sum(jnp.where(lo, sq, 0.0), axis=1, keepdims=True)
    s_hi = jnp.sum(jnp.where(lo, 0.0, sq), axis=1, keepdims=True)
    return lax.rsqrt(jnp.where(lo, s_lo, s_hi) * (1.0 / HEAD_DIM) + EPS)


def _rope(y, c, sp, sm):
    return y * c + pltpu.roll(y, 16, axis=1) * sp + pltpu.roll(y, LANES - 16, axis=1) * sm


def _rope_t(dz, c, sp, sm):
    return dz * c + pltpu.roll(dz * sp, LANES - 16, axis=1) + pltpu.roll(dz * sm, 16, axis=1)


def _qk_prep(qkv, qg2, kg2, tabs, name):
    seq = qkv.shape[0]
    ts = _pick(seq, (256, 128))
    nq, nkp = N_HEADS // 2, N_KV // 2
    qw, kw = N_HEADS * HEAD_DIM, N_KV * HEAD_DIM

    def body(x_ref, qg_ref, kg_ref, c_ref, sp_ref, sm_ref, q_ref, k_ref, kt_ref, v_ref, vt_ref):
        lo = lax.broadcasted_iota(jnp.int32, (ts, LANES), 1) < HEAD_DIM
        top = lax.broadcasted_iota(jnp.int32, (LANES, ts), 0) < HEAD_DIM
        c, sp, sm = c_ref[...], sp_ref[...], sm_ref[...]
        for i in range(nq):
            xv = x_ref[:, i * LANES:(i + 1) * LANES]
            y = xv * _pair_norm(xv, lo) * qg_ref[...]
            q_ref[:, i * LANES:(i + 1) * LANES] = (_rope(y, c, sp, sm) * (LOG2E * HEAD_DIM ** -0.5)).astype(BF16)
        for i in range(nkp):
            xv = x_ref[:, qw + i * LANES:qw + (i + 1) * LANES]
            z = _rope(xv * _pair_norm(xv, lo) * kg_ref[...], c, sp, sm)
            k_ref[:, i * LANES:(i + 1) * LANES] = z.astype(BF16)
            kt_ref[i * LANES:(i + 1) * LANES, :] = z.T.astype(BF16)
            vv = x_ref[:, qw + kw + i * LANES:qw + kw + (i + 1) * LANES]
            v_ref[:, i * LANES:(i + 1) * LANES] = vv.astype(BF16)
            vvt = vv.T
            vt_ref[(2 * i) * LANES:(2 * i + 1) * LANES, :] = jnp.where(top, vvt, 1.0).astype(BF16)
            vt_ref[(2 * i + 1) * LANES:(2 * i + 2) * LANES, :] = jnp.where(top, 1.0, vvt).astype(BF16)

    tab = pl.BlockSpec((ts, LANES), lambda i: (i, 0))
    vec = pl.BlockSpec((1, LANES), lambda i: (0, 0))
    return pl.pallas_call(
        body,
        out_shape=(jax.ShapeDtypeStruct((seq, qw), BF16), jax.ShapeDtypeStruct((seq, kw), BF16),
                   jax.ShapeDtypeStruct((kw, seq), BF16), jax.ShapeDtypeStruct((seq, kw), BF16),
                   jax.ShapeDtypeStruct((N_KV * LANES, seq), BF16)),
        grid=(seq // ts,),
        in_specs=[pl.BlockSpec((ts, qw + 2 * kw), lambda i: (i, 0)), vec, vec, tab, tab, tab],
        out_specs=(pl.BlockSpec((ts, qw), lambda i: (i, 0)), pl.BlockSpec((ts, kw), lambda i: (i, 0)),
                   pl.BlockSpec((kw, ts), lambda i: (0, i)), pl.BlockSpec((ts, kw), lambda i: (i, 0)),
                   pl.BlockSpec((N_KV * LANES, ts), lambda i: (0, i))),
        compiler_params=_cp(("parallel",)), name=name)(qkv, qg2, kg2, *tabs)


def _qk_prep_bwd(qkv, dq, dk, dv, qg2, kg2, tabs, name):
    seq = qkv.shape[0]
    ts = _pick(seq, (256, 128))
    nq, nkp = N_HEADS // 2, N_KV // 2
    qw, kw = N_HEADS * HEAD_DIM, N_KV * HEAD_DIM

    def body(x_ref, dq_ref, dk_ref, dv_ref, qg_ref, kg_ref, c_ref, sp_ref, sm_ref, o_ref, dqg_ref, dkg_ref):
        step = pl.program_id(0)
        lo = lax.broadcasted_iota(jnp.int32, (ts, LANES), 1) < HEAD_DIM
        c, sp, sm = c_ref[...], sp_ref[...], sm_ref[...]

        def one(xv, dz, gain):
            r = _pair_norm(xv, lo)
            nv = xv * r
            dy = _rope_t(dz, c, sp, sm)
            dgp = jnp.sum(dy * nv, axis=0, keepdims=True)
            dn = dy * gain
            t = dn * nv
            m_lo = jnp.sum(jnp.where(lo, t, 0.0), axis=1, keepdims=True)
            m_hi = jnp.sum(jnp.where(lo, 0.0, t), axis=1, keepdims=True)
            m = jnp.where(lo, m_lo, m_hi) * (1.0 / HEAD_DIM)
            return r * (dn - nv * m), dgp

        dqg = jnp.zeros((1, LANES), F32)
        for i in range(nq):
            sl = slice(i * LANES, (i + 1) * LANES)
            dx, dgp = one(x_ref[:, sl], dq_ref[:, sl] * (HEAD_DIM ** -0.5), qg_ref[...])
            o_ref[:, sl] = dx.astype(BF16)
            dqg = dqg + dgp
        dkg = jnp.zeros((1, LANES), F32)
        for i in range(nkp):
            sl = slice(i * LANES, (i + 1) * LANES)
            dx, dgp = one(x_ref[:, qw + i * LANES:qw + (i + 1) * LANES], dk_ref[:, sl], kg_ref[...])
            o_ref[:, qw + i * LANES:qw + (i + 1) * LANES] = dx.astype(BF16)
            dkg = dkg + dgp
            o_ref[:, qw + kw + i * LANES:qw + kw + (i + 1) * LANES] = dv_ref[:, sl].astype(BF16)

        @pl.when(step == 0)
        def _():
            dqg_ref[...] = dqg
            dkg_ref[...] = dkg

        @pl.when(step > 0)
        def _():
            dqg_ref[...] += dqg
            dkg_ref[...] += dkg

    tab = pl.BlockSpec((ts, LANES), lambda i: (i, 0))
    vec = pl.BlockSpec((1, LANES), lambda i: (0, 0))
    return pl.pallas_call(
        body,
        out_shape=(jax.ShapeDtypeStruct((seq, qw + 2 * kw), BF16), jax.ShapeDtypeStruct((1, LANES), F32),
                   jax.ShapeDtypeStruct((1, LANES), F32)),
        grid=(seq // ts,),
        in_specs=[pl.BlockSpec((ts, qw + 2 * kw), lambda i: (i, 0)), pl.BlockSpec((ts, qw), lambda i: (i, 0)),
                  pl.BlockSpec((ts, kw), lambda i: (i, 0)), pl.BlockSpec((ts, kw), lambda i: (i, 0)),
                  vec, vec, tab, tab, tab],
        out_specs=(pl.BlockSpec((ts, qw + 2 * kw), lambda i: (i, 0)), vec, vec),
        compiler_params=_cp(("arbitrary",)), name=name)(qkv, dq, dk, dv, qg2, kg2, *tabs)


def _slot(blk, off0, tq):
    half = lax.broadcasted_iota(jnp.int32, (tq, LANES), 1) // HEAD_DIM
    keep = half == jnp.where(off0, 0, 1)
    parts = []
    for p in range(2):
        pair = blk[:, p * LANES:(p + 1) * LANES].astype(F32)
        rolled = pltpu.roll(pair, HEAD_DIM, axis=1)
        parts.append(jnp.where(keep, jnp.where(off0, pair, rolled), 0.0))
        parts.append(jnp.where(keep, jnp.where(off0, rolled, pair), 0.0))
    return jnp.concatenate(parts, axis=0)


def _unslot(x4, off0, tq):
    lo = lax.broadcasted_iota(jnp.int32, (tq, LANES), 1) < HEAD_DIM
    pairs = []
    for p in range(2):
        h0 = x4[(2 * p) * tq:(2 * p + 1) * tq]
        h1 = x4[(2 * p + 1) * tq:(2 * p + 2) * tq]
        a = jnp.where(off0, h0, pltpu.roll(h0, HEAD_DIM, axis=1))
        b = jnp.where(off0, pltpu.roll(h1, HEAD_DIM, axis=1), h1)
        pairs.append(jnp.where(lo, a, b))
    return jnp.concatenate(pairs, axis=1)


def _flash_fwd(q, k, vt, plan, shards, name):
    seq = q.shape[0]
    tq = _pick(seq, (FLASH_TQ, 128))
    tk = _pick(seq, (FLASH_TK, 512, 256, 128))
    nq, nkv = seq // tq, seq // tk
    gw = 4 * HEAD_DIM
    nt = plan.nt

    def body(q_ref, k_ref, vt_ref, *rest):
        o_ref, lse_ref = rest[nt:nt + 2]
        q4_ref, m_ref, acc_ref = rest[2 * nt + 2:2 * nt + 5]
        plan.bind(rest[:nt], rest[nt + 2:2 * nt + 2], *rest[2 * nt + 5:])
        g, qi, ki = pl.program_id(0), pl.program_id(1), pl.program_id(2)
        off0 = (g % 2) == 0
        inner0 = jnp.logical_and(qi == 0, ki == 0)

        @pl.when(jnp.logical_and(g == 0, inner0))
        def _():
            plan.start()

        @pl.when(jnp.logical_and(g == N_KV - 1, inner0))
        def _():
            plan.forward()

        @pl.when(ki == 0)
        def _():
            q4_ref[...] = _slot(q_ref[...], off0, tq).astype(BF16)
            m_ref[...] = jnp.full(m_ref.shape, NEG, F32)
            acc_ref[...] = jnp.zeros(acc_ref.shape, F32)

        st = lax.dot_general(k_ref[...], q4_ref[...], (((1,), (1,)), ((), ())), preferred_element_type=F32)
        m_old = m_ref[...]
        m_new = jnp.maximum(m_old, jnp.max(st, axis=0, keepdims=True))
        pt = jnp.exp2(st - m_new).astype(BF16)
        acc_ref[...] = jnp.exp2(m_old - m_new) * acc_ref[...] + jnp.dot(vt_ref[...], pt, preferred_element_type=F32)
        m_ref[...] = m_new

        @pl.when(ki == nkv - 1)
        def _():
            acc = acc_ref[...]
            l = jnp.where(off0, acc[HEAD_DIM:HEAD_DIM + 1], acc[0:1])
            o4 = acc.T
            o4 = o4 / pltpu.roll(o4, HEAD_DIM, axis=1)
            o_ref[...] = _unslot(o4, off0, tq).astype(o_ref.dtype)
            lse_ref[...] = jnp.broadcast_to(m_ref[...] + jnp.log2(l), lse_ref.shape)

        @pl.when(jnp.logical_and(g == N_KV - 1, jnp.logical_and(qi == nq - 1, ki == nkv - 1)))
        def _():
            plan.finish()

    return pl.pallas_call(
        body,
        out_shape=(jax.ShapeDtypeStruct((seq, N_HEADS * HEAD_DIM), BF16),
                   jax.ShapeDtypeStruct((N_KV * nq * 8, 4 * tq), F32), *plan.out_shape),
        grid=(N_KV, nq, nkv),
        in_specs=[pl.BlockSpec((tq, gw), lambda g, qi, ki: (qi, g)),
                  pl.BlockSpec((tk, LANES), lambda g, qi, ki: (ki, g // 2)),
                  pl.BlockSpec((LANES, tk), lambda g, qi, ki: (g, ki))] + [ANY] * nt,
        out_specs=(pl.BlockSpec((tq, gw), lambda g, qi, ki: (qi, g)),
                   pl.BlockSpec((8, 4 * tq), lambda g, qi, ki: (g * nq + qi, 0)), *([ANY] * nt)),
        scratch_shapes=[pltpu.VMEM((4 * tq, LANES), BF16), pltpu.VMEM((1, 4 * tq), F32),
                        pltpu.VMEM((LANES, 4 * tq), F32)] + plan.scratch,
        compiler_params=_cp(("arbitrary", "arbitrary", "arbitrary")), name=name)(q, k, vt, *shards)


def _flash_bwd(q, k, kt, v, do, o, lse, plan, grads, name):
    seq = q.shape[0]
    tq = _pick(seq, (FLASH_TQ, 128))
    tk = _pick(seq, (FLASH_TK, 512, 256, 128))
    nq, nkv = seq // tq, seq // tk
    gw = 4 * HEAD_DIM
    nt = plan.nt

    def body(q_ref, k_ref, kt_ref, v_ref, do_ref, o_ref, lse_ref, *rest):
        dq_ref, dk_ref, dv_ref = rest[nt:nt + 3]
        q4_ref, do4_ref, delta_ref, dqt_ref = rest[2 * nt + 3:2 * nt + 7]
        plan.bind(rest[:nt], rest[nt + 3:2 * nt + 3], *rest[2 * nt + 7:])
        g, qi, ki = pl.program_id(0), pl.program_id(1), pl.program_id(2)
        off0 = (g % 2) == 0

        @pl.when(jnp.logical_and(g == 0, jnp.logical_and(qi == 0, ki == 0)))
        def _():
            plan.start()

        @pl.when(jnp.logical_and(g % 2 == 0, jnp.logical_and(qi == 0, ki == 0)))
        def _():
            dk_ref[...] = jnp.zeros(dk_ref.shape, F32)
            dv_ref[...] = jnp.zeros(dv_ref.shape, F32)

        @pl.when(ki == 0)
        def _():
            q4_ref[...] = _slot(q_ref[...], off0, tq).astype(BF16)
            do4 = _slot(do_ref[...], off0, tq)
            do4_ref[...] = do4.astype(BF16)
            o4 = _slot(o_ref[...], off0, tq)
            delta_ref[...] = jnp.sum((do4 * o4).T, axis=0, keepdims=True)
            dqt_ref[...] = jnp.zeros(dqt_ref.shape, F32)

        q4, do4 = q4_ref[...], do4_ref[...]
        st = lax.dot_general(k_ref[...], q4, (((1,), (1,)), ((), ())), preferred_element_type=F32)
        pt = jnp.exp2(st - lse_ref[0:1, :])
        dpt = lax.dot_general(v_ref[...], do4, (((1,), (1,)), ((), ())), preferred_element_type=F32)
        dst = (pt * (dpt - delta_ref[...])).astype(BF16)
        rows = pl.ds(pl.multiple_of(ki * tk, tk), tk)
        dv_ref[rows, :] += jnp.dot(pt.astype(BF16), do4, preferred_element_type=F32)
        dk_ref[rows, :] += jnp.dot(dst, q4, preferred_element_type=F32) * (1.0 / LOG2E)
        dqt_ref[...] += jnp.dot(kt_ref[...], dst, preferred_element_type=F32)

        @pl.when(ki == nkv - 1)
        def _():
            dq_ref[...] = _unslot(dqt_ref[...].T, off0, tq)

        @pl.when(jnp.logical_and(g == N_KV - 1, jnp.logical_and(qi == nq - 1, ki == nkv - 1)))
        def _():
            plan.finish()

    return pl.pallas_call(
        body,
        out_shape=(jax.ShapeDtypeStruct((seq, N_HEADS * HEAD_DIM), F32),
                   jax.ShapeDtypeStruct((seq, N_KV * HEAD_DIM), F32), jax.ShapeDtypeStruct((seq, N_KV * HEAD_DIM), F32),
                   *plan.out_shape),
        grid=(N_KV, nq, nkv),
        in_specs=[pl.BlockSpec((tq, gw), lambda g, qi, ki: (qi, g)),
                  pl.BlockSpec((tk, LANES), lambda g, qi, ki: (ki, g // 2)),
                  pl.BlockSpec((LANES, tk), lambda g, qi, ki: (g // 2, ki)),
                  pl.BlockSpec((tk, LANES), lambda g, qi, ki: (ki, g // 2)),
                  pl.BlockSpec((tq, gw), lambda g, qi, ki: (qi, g)),
                  pl.BlockSpec((tq, gw), lambda g, qi, ki: (qi, g)),
                  pl.BlockSpec((8, 4 * tq), lambda g, qi, ki: (g * nq + qi, 0))] + [ANY] * nt,
        out_specs=(pl.BlockSpec((tq, gw), lambda g, qi, ki: (qi, g)),
                   pl.BlockSpec((seq, LANES), lambda g, qi, ki: (0, g // 2)),
                   pl.BlockSpec((seq, LANES), lambda g, qi, ki: (0, g // 2)), *([ANY] * nt)),
        scratch_shapes=[pltpu.VMEM((4 * tq, LANES), BF16), pltpu.VMEM((4 * tq, LANES), BF16),
                        pltpu.VMEM((1, 4 * tq), F32), pltpu.VMEM((LANES, 4 * tq), F32)] + plan.scratch,
        compiler_params=_cp(("arbitrary", "arbitrary", "arbitrary")), name=name)(q, k, kt, v, do, o, lse, *grads)


def _xattn_fwd(q, kv, name):
    seq, d = q.shape
    mlen = kv.shape[0]
    tq = _pick(seq, (512, 256))

    def body(q_ref, k_ref, v_ref, o_ref):
        for h in range(X_HEADS):
            sl = slice(h * X_HEAD_DIM, (h + 1) * X_HEAD_DIM)
            s = lax.dot_general(q_ref[:, sl], k_ref[:, sl], (((1,), (1,)), ((), ())), preferred_element_type=F32)
            e = jnp.exp(s - jnp.max(s, axis=-1, keepdims=True))
            p = e / jnp.sum(e, axis=-1, keepdims=True)
            o_ref[:, sl] = jnp.dot(p.astype(BF16), v_ref[:, sl], preferred_element_type=F32).astype(o_ref.dtype)

    return pl.pallas_call(
        body, out_shape=jax.ShapeDtypeStruct((seq, d), BF16), grid=(seq // tq,),
        in_specs=[pl.BlockSpec((tq, d), lambda i: (i, 0)), pl.BlockSpec((mlen, d), lambda i: (0, 0)),
                  pl.BlockSpec((mlen, d), lambda i: (0, 1))],
        out_specs=pl.BlockSpec((tq, d), lambda i: (i, 0)), compiler_params=_cp(("parallel",)), name=name)(q, kv, kv)


def _xattn_bwd(q, kv, do, name):
    seq, d = q.shape
    mlen = kv.shape[0]
    tq = _pick(seq, (512, 256))
    scale = X_HEAD_DIM ** -0.5

    def body(q_ref, k_ref, v_ref, do_ref, dq_ref, dkv_ref):
        i = pl.program_id(0)

        @pl.when(i == 0)
        def _():
            dkv_ref[...] = jnp.zeros(dkv_ref.shape, F32)

        for h in range(X_HEADS):
            sl = slice(h * X_HEAD_DIM, (h + 1) * X_HEAD_DIM)
            qh, kh, vh = q_ref[:, sl], k_ref[:, sl], v_ref[:, sl]
            doh = do_ref[:, sl].astype(BF16)
            st = lax.dot_general(kh, qh, (((1,), (1,)), ((), ())), preferred_element_type=F32)
            e = jnp.exp(st - jnp.max(st, axis=0, keepdims=True))
            pt = e / jnp.sum(e, axis=0, keepdims=True)
            dpt = lax.dot_general(vh, doh, (((1,), (1,)), ((), ())), preferred_element_type=F32)
            dst = (pt * (dpt - jnp.sum(pt * dpt, axis=0, keepdims=True))).astype(BF16)
            dkv_ref[:, sl] += jnp.dot(dst, qh, preferred_element_type=F32)
            dkv_ref[:, d + h * X_HEAD_DIM:d + (h + 1) * X_HEAD_DIM] += jnp.dot(pt.astype(BF16), doh,
                                                                                 preferred_element_type=F32)
            dqh = lax.dot_general(dst, kh, (((0,), (0,)), ((), ())), preferred_element_type=F32)
            dq_ref[:, sl] = (dqh * scale).astype(dq_ref.dtype)

    return pl.pallas_call(
        body, out_shape=(jax.ShapeDtypeStruct((seq, d), BF16), jax.ShapeDtypeStruct((mlen, 2 * d), F32)),
        grid=(seq // tq,),
        in_specs=[pl.BlockSpec((tq, d), lambda i: (i, 0)), pl.BlockSpec((mlen, d), lambda i: (0, 0)),
                  pl.BlockSpec((mlen, d), lambda i: (0, 1)), pl.BlockSpec((tq, d), lambda i: (i, 0))],
        out_specs=(pl.BlockSpec((tq, d), lambda i: (i, 0)), pl.BlockSpec((mlen, 2 * d), lambda i: (0, 0))),
        compiler_params=_cp(("arbitrary",)), name=name)(q, kv, kv, do)


def _halo_specs(tr, tc, seq, col):
    per, last = tr // HALO, seq // HALO - 1
    return [pl.BlockSpec((tr, tc), lambda j, r: (r, col(j))),
            pl.BlockSpec((HALO, tc), lambda j, r: (jnp.maximum(r * per - 1, 0), col(j))),
            pl.BlockSpec((HALO, tc), lambda j, r: (jnp.minimum((r + 1) * per, last), col(j)))]


def _extend(main_ref, prev_ref, next_ref, r, nr):
    pv = (r > 0).astype(F32)
    nv = (r < nr - 1).astype(F32)
    return jnp.concatenate([prev_ref[...].astype(F32) * pv, main_ref[...].astype(F32),
                            next_ref[...].astype(F32) * nv], axis=0)


def _conv3(e, w_ref, n):
    return pltpu.roll(e, 1, axis=0) * w_ref[0:1, :] + e * w_ref[1:2, :] + pltpu.roll(e, n - 1, axis=0) * w_ref[2:3, :]


def _conv_gate_fwd(ug, uv, cw, cb, layer, name):
    seq, f = ug.shape
    tc = 256
    tr = _pick(seq, (512, 256))
    nc, nr = f // tc, seq // tr
    n = tr + 2 * HALO

    def body(g_ref, gp_ref, gn_ref, v_ref, vp_ref, vn_ref, wg_ref, wv_ref, bg_ref, bv_ref, o_ref):
        r = pl.program_id(1)
        cg = _conv3(_extend(g_ref, gp_ref, gn_ref, r, nr), wg_ref, n)[HALO:HALO + tr] + bg_ref[...]
        cv = _conv3(_extend(v_ref, vp_ref, vn_ref, r, nr), wv_ref, n)[HALO:HALO + tr] + bv_ref[...]
        o_ref[...] = (cg * jax.nn.sigmoid(cg) * cv).astype(o_ref.dtype)

    w_spec = lambda shift: pl.BlockSpec((None, 3, tc), lambda j, r: (layer, 0, j + shift))
    b_spec = lambda shift: pl.BlockSpec((None, 1, tc), lambda j, r: (layer, 0, j + shift))
    return pl.pallas_call(
        body, out_shape=jax.ShapeDtypeStruct((seq, f), BF16), grid=(nc, nr),
        in_specs=_halo_specs(tr, tc, seq, lambda j: j) * 2 + [w_spec(0), w_spec(nc), b_spec(0), b_spec(nc)],
        out_specs=pl.BlockSpec((tr, tc), lambda j, r: (r, j)),
        compiler_params=_cp(("parallel", "parallel")), name=name)(ug, ug, ug, uv, uv, uv, cw, cw, cb, cb)


def _conv_gate_bwd(ug, uv, dact, cw, cb, layer, name):
    seq, f = ug.shape
    tc = 256
    tr = _pick(seq, (512, 256))
    nc, nr = f // tc, seq // tr
    n = tr + 2 * HALO

    def body(g_ref, gp_ref, gn_ref, v_ref, vp_ref, vn_ref, d_ref, dp_ref, dn_ref, wg_ref, wv_ref, bg_ref, bv_ref,
             dug_ref, duv_ref, dwg_ref, dwv_ref):
        r = pl.program_id(1)
        eg = _extend(g_ref, gp_ref, gn_ref, r, nr)
        ev = _extend(v_ref, vp_ref, vn_ref, r, nr)
        da = _extend(d_ref, dp_ref, dn_ref, r, nr)
        eg3 = (pltpu.roll(eg, 1, axis=0), eg, pltpu.roll(eg, n - 1, axis=0))
        ev3 = (pltpu.roll(ev, 1, axis=0), ev, pltpu.roll(ev, n - 1, axis=0))
        cg = eg3[0] * wg_ref[0:1, :] + eg3[1] * wg_ref[1:2, :] + eg3[2] * wg_ref[2:3, :] + bg_ref[...]
        cv = ev3[0] * wv_ref[0:1, :] + ev3[1] * wv_ref[1:2, :] + ev3[2] * wv_ref[2:3, :] + bv_ref[...]
        sg = jax.nn.sigmoid(cg)
        dcv = da * (cg * sg)
        dcg = da * cv * (sg * (1.0 + cg * (1.0 - sg)))

        def back(dc, e3, w_ref, du_ref, dw_ref):
            du = (pltpu.roll(dc, n - 1, axis=0) * w_ref[0:1, :] + dc * w_ref[1:2, :]
                  + pltpu.roll(dc, 1, axis=0) * w_ref[2:3, :])
            du_ref[...] = du[HALO:HALO + tr].astype(du_ref.dtype)
            dcm = dc[HALO:HALO + tr]
            taps = [jnp.sum(dcm * e[HALO:HALO + tr], axis=0, keepdims=True) for e in e3] + [
                    jnp.sum(dcm, axis=0, keepdims=True)]
            part = jnp.concatenate(taps + [jnp.zeros((4, tc), F32)], axis=0)

            @pl.when(r == 0)
            def _():
                dw_ref[...] = part

            @pl.when(r > 0)
            def _():
                dw_ref[...] += part

        back(dcg, eg3, wg_ref, dug_ref, dwg_ref)
        back(dcv, ev3, wv_ref, duv_ref, dwv_ref)

    w_spec = lambda shift: pl.BlockSpec((None, 3, tc), lambda j, r: (layer, 0, j + shift))
    b_spec = lambda shift: pl.BlockSpec((None, 1, tc), lambda j, r: (layer, 0, j + shift))
    out_rows = pl.BlockSpec((tr, tc), lambda j, r: (r, j))
    out_acc = pl.BlockSpec((8, tc), lambda j, r: (0, j))
    return pl.pallas_call(
        body,
        out_shape=(jax.ShapeDtypeStruct((seq, f), BF16), jax.ShapeDtypeStruct((seq, f), BF16),
                   jax.ShapeDtypeStruct((8, f), F32), jax.ShapeDtypeStruct((8, f), F32)),
        grid=(nc, nr),
        in_specs=_halo_specs(tr, tc, seq, lambda j: j) * 3 + [w_spec(0), w_spec(nc), b_spec(0), b_spec(nc)],
        out_specs=(out_rows, out_rows, out_acc, out_acc),
        compiler_params=_cp(("parallel", "arbitrary")), name=name)(ug, ug, ug, uv, uv, uv, dact, dact, dact, cw, cw, cb, cb)


def _pool_count(g, r, tr, n, seq):
    half = jnp.left_shift(1, g)
    t = r * tr - HALO + lax.broadcasted_iota(jnp.int32, (n, 1), 0)
    cnt = jnp.minimum(t + half, seq) - jnp.maximum(t - half, 0)
    return jnp.maximum(cnt, 1).astype(F32)


def _by_group(g, levels):
    out = levels[3]
    for i in (2, 1, 0):
        out = jnp.where(g == i, levels[i], out)
    return out


def _pool_mixed(e, g, cnt, n):
    w2 = e + pltpu.roll(e, 1, axis=0)
    w4 = pltpu.roll(w2, 1, axis=0) + pltpu.roll(w2, n - 1, axis=0)
    w8 = pltpu.roll(w4, 2, axis=0) + pltpu.roll(w4, n - 2, axis=0)
    w16 = pltpu.roll(w8, 4, axis=0) + pltpu.roll(w8, n - 4, axis=0)
    return _by_group(g, (w2, w4, w8, w16)) / cnt - e


def _pool_fwd(hp, xres, pw, scale, name):
    seq, d = hp.shape
    tc = POOL_GROUP_W
    tr = _pick(seq, (512, 256))
    nr = seq // tr
    n = tr + 2 * HALO

    def body(h_ref, hp_ref, hn_ref, x_ref, w_ref, s_ref, o_ref):
        g, r = pl.program_id(0), pl.program_id(1)
        e = _extend(h_ref, hp_ref, hn_ref, r, nr)
        mixed = _pool_mixed(e, g, _pool_count(g, r, tr, n, seq), n)[HALO:HALO + tr]
        y = jnp.dot(mixed.astype(BF16), w_ref[...], preferred_element_type=F32)
        o_ref[...] = x_ref[...] + y * s_ref[...]

    return pl.pallas_call(
        body, out_shape=jax.ShapeDtypeStruct((seq, d), F32), grid=(POOL_GROUPS, nr),
        in_specs=_halo_specs(tr, tc, seq, lambda j: j) + [
            pl.BlockSpec((tr, tc), lambda j, r: (r, j)), pl.BlockSpec((None, tc, tc), lambda j, r: (j, 0, 0)),
            pl.BlockSpec((1, tc), lambda j, r: (0, j))],
        out_specs=pl.BlockSpec((tr, tc), lambda j, r: (r, j)),
        compiler_params=_cp(("parallel", "parallel")), name=name)(hp, hp, hp, xres, pw, scale)


def _pool_bwd(hp, dy, pw, scale, name):
    seq, d = hp.shape
    tc = POOL_GROUP_W
    tr = _pick(seq, (512, 256))
    nr = seq // tr
    n = tr + 2 * HALO

    def body(h_ref, hp_ref, hn_ref, d_ref, dp_ref, dn_ref, w_ref, s_ref, dh_ref, dw_ref, ds_ref):
        g, r = pl.program_id(0), pl.program_id(1)
        cnt = _pool_count(g, r, tr, n, seq)
        e = _extend(h_ref, hp_ref, hn_ref, r, nr)
        mixed = _pool_mixed(e, g, cnt, n)[HALO:HALO + tr].astype(BF16)
        dye = _extend(d_ref, dp_ref, dn_ref, r, nr)
        dyp = (dye * s_ref[...]).astype(BF16)
        dmixed = lax.dot_general(dyp, w_ref[...], (((1,), (1,)), ((), ())), preferred_element_type=F32)
        dwin = dmixed / cnt
        m2 = dwin + pltpu.roll(dwin, n - 1, axis=0)
        m4 = pltpu.roll(m2, 1, axis=0) + pltpu.roll(m2, n - 1, axis=0)
        m8 = pltpu.roll(m4, 2, axis=0) + pltpu.roll(m4, n - 2, axis=0)
        m16 = pltpu.roll(m8, 4, axis=0) + pltpu.roll(m8, n - 4, axis=0)
        dh_ref[...] = (_by_group(g, (m2, m4, m8, m16)) - dmixed)[HALO:HALO + tr]
        ypre = jnp.dot(mixed, w_ref[...], preferred_element_type=F32)
        dsp = jnp.sum(d_ref[...] * ypre, axis=0, keepdims=True)
        dwp = lax.dot_general(mixed, dyp[HALO:HALO + tr], (((0,), (0,)), ((), ())), preferred_element_type=F32)

        @pl.when(r == 0)
        def _():
            dw_ref[...] = dwp
            ds_ref[...] = dsp

        @pl.when(r > 0)
        def _():
            dw_ref[...] += dwp
            ds_ref[...] += dsp

    return pl.pallas_call(
        body,
        out_shape=(jax.ShapeDtypeStruct((seq, d), F32), jax.ShapeDtypeStruct((POOL_GROUPS, tc, tc), F32),
                   jax.ShapeDtypeStruct((1, d), F32)),
        grid=(POOL_GROUPS, nr),
        in_specs=_halo_specs(tr, tc, seq, lambda j: j) * 2 + [
            pl.BlockSpec((None, tc, tc), lambda j, r: (j, 0, 0)), pl.BlockSpec((1, tc), lambda j, r: (0, j))],
        out_specs=(pl.BlockSpec((tr, tc), lambda j, r: (r, j)), pl.BlockSpec((None, tc, tc), lambda j, r: (j, 0, 0)),
                   pl.BlockSpec((1, tc), lambda j, r: (0, j))),
        compiler_params=_cp(("parallel", "arbitrary")), name=name)(hp, hp, hp, dy, dy, dy, pw, scale)


def _adamw_math(w, g, m, v):
    m = ADAM_B1 * m + (1.0 - ADAM_B1) * g
    v = ADAM_B2 * v + (1.0 - ADAM_B2) * (g * g)
    m_hat = m / (1.0 - ADAM_B1 ** ADAM_STEP)
    v_hat = v / (1.0 - ADAM_B2 ** ADAM_STEP)
    delta = -ADAM_LR * (m_hat / (jnp.sqrt(v_hat) + ADAM_EPS) + ADAM_WD * w)
    return delta, m, v


def _adamw(w, ga, gb, m, v, name):
    rows, cols = w.shape
    tr = _pick(rows, (256, 128, 64, 32, 16, 8))
    two = gb is not None

    def body(*refs):
        if two:
            w_ref, ga_ref, gb_ref, m_ref, v_ref, g_out, d_out, m_out, v_out = refs
            g = ga_ref[...] + gb_ref[...]
        else:
            w_ref, ga_ref, m_ref, v_ref, g_out, d_out, m_out, v_out = refs
            g = ga_ref[...]
        delta, m, v = _adamw_math(w_ref[...], g, m_ref[...], v_ref[...])
        g_out[...] = g
        d_out[...] = delta
        m_out[...] = m
        v_out[...] = v

    spec = pl.BlockSpec((tr, cols), lambda i: (i, 0))
    ops = [w, ga] + ([gb] if two else []) + [m, v]
    return pl.pallas_call(
        body, out_shape=tuple(jax.ShapeDtypeStruct((rows, cols), F32) for _ in range(4)), grid=(rows // tr,),
        in_specs=[spec] * len(ops), out_specs=(spec,) * 4, compiler_params=_cp(("parallel",)), name=name)(*ops)


def _sum4(parts, name):
    _, rows, cols = parts.shape
    tr = _pick(rows, (256, 128, 64, 32, 16))

    def body(p_ref, o_ref):
        acc = p_ref[0].astype(F32)
        for kk in range(1, 4):
            acc = acc + p_ref[kk].astype(F32)
        o_ref[...] = acc

    return pl.pallas_call(
        body, out_shape=jax.ShapeDtypeStruct((rows, cols), F32), grid=(rows // tr,),
        in_specs=[pl.BlockSpec((4, tr, cols), lambda i: (0, i, 0))], out_specs=pl.BlockSpec((tr, cols), lambda i: (i, 0)),
        compiler_params=_cp(("parallel",)), name=name)(parts)


def _place():
    x, y, c = lax.axis_index("x"), lax.axis_index("y"), lax.axis_index("c")
    chips = [(1 - x, y), (x, 1 - y), (1 - x, 1 - y)]
    return x, y, c, chips


def _window(ref, axis, j, size, c=None, half=None, lead=()):
    if axis == "r":
        if c is None:
            return ref.at[lead + (slice(None), pl.ds(pl.multiple_of(j * size, 32), size), slice(None))]
        return ref.at[lead + (slice(None), pl.ds(pl.multiple_of(j * size + c * half, 32), half), slice(None))]
    cols = pl.ds(pl.multiple_of(j * size, LANES), size)
    if c is None:
        return ref.at[lead + (slice(None), slice(None), cols)]
    return ref.at[lead + (slice(None), pl.ds(pl.multiple_of(c * half, 32), half), cols)]


class _Gather:
    def __init__(self, shards, axes):
        self.nt, self.axes = len(shards), axes
        self.out_shape, self.sizes, self.halves = [], [], []
        for s, ax in zip(shards, axes):
            l, rs, cs = s.shape
            self.out_shape.append(jax.ShapeDtypeStruct((l, 4 * rs, cs) if ax == "r" else (l, rs, 4 * cs), s.dtype))
            self.sizes.append(rs if ax == "r" else cs)
            self.halves.append(rs // 2)
        self.scratch = [pltpu.SemaphoreType.DMA((6 * self.nt,)), pltpu.SemaphoreType.DMA((6 * self.nt,)),
                        pltpu.SemaphoreType.DMA((self.nt,))]

    def bind(self, src, dst, send_sems, recv_sems, local_sems):
        self.src, self.dst, self.send_sems, self.recv_sems, self.local_sems = src, dst, send_sems, recv_sems, local_sems

    def _win(self, t, j, core=None):
        return _window(self.dst[t], self.axes[t], j, self.sizes[t], core, self.halves[t])

    def _ici(self, t, kk, origin):
        _, _, c, chips = _place()
        px, py = chips[kk]
        half = self.src[t].at[:, pl.ds(pl.multiple_of(c * self.halves[t], 16), self.halves[t]), :]
        return pltpu.make_async_remote_copy(
            src_ref=half, dst_ref=self._win(t, origin, c), send_sem=self.send_sems.at[t * 3 + kk],
            recv_sem=self.recv_sems.at[t * 3 + kk], device_id=(px, py, c), device_id_type=MESH)

    def _d2d(self, t, kk, origin, core):
        x, y, c, _ = _place()
        k2 = 3 * self.nt + t * 3 + kk
        return pltpu.make_async_remote_copy(
            src_ref=self._win(t, origin, core), dst_ref=self._win(t, origin, core), send_sem=self.send_sems.at[k2],
            recv_sem=self.recv_sems.at[k2], device_id=(x, y, 1 - c), device_id_type=MESH)

    def _local(self, t):
        x, y, _, _ = _place()
        return pltpu.make_async_copy(self.src[t], self._win(t, 2 * x + y), self.local_sems.at[t])

    def _each(self):
        _, _, _, chips = _place()
        for t in range(self.nt):
            for kk in range(3):
                px, py = chips[kk]
                yield t, kk, 2 * px + py

    def start(self):
        x, y, _, _ = _place()
        for t in range(self.nt):
            self._local(t).start()
        for t, kk, _ in self._each():
            self._ici(t, kk, 2 * x + y).start()

    def forward(self):
        _, _, c, _ = _place()
        for t, kk, origin in self._each():
            self._ici(t, kk, origin).wait_recv()
            self._d2d(t, kk, origin, c).start()

    def finish(self):
        x, y, c, _ = _place()
        for t, kk, origin in self._each():
            self._d2d(t, kk, origin, 1 - c).wait_recv()
        for t, kk, origin in self._each():
            self._ici(t, kk, 2 * x + y).wait_send()
            self._d2d(t, kk, origin, c).wait_send()
        for t in range(self.nt):
            self._local(t).wait()


class _Scatter:
    def __init__(self, grads, axes):
        self.nt, self.axes = len(grads), axes
        self.out_shape, self.sizes = [], []
        for gr, ax in zip(grads, axes):
            l, r, cc = gr.shape
            self.out_shape.append(jax.ShapeDtypeStruct((4, l, r // 4, cc) if ax == "r" else (4, l, r, cc // 4), gr.dtype))
            self.sizes.append(r // 4 if ax == "r" else cc // 4)
        self.scratch = [pltpu.SemaphoreType.DMA((3 * self.nt,)), pltpu.SemaphoreType.DMA((3 * self.nt,)),
                        pltpu.SemaphoreType.DMA((self.nt,))]

    def bind(self, src, dst, send_sems, recv_sems, local_sems):
        self.src, self.dst, self.send_sems, self.recv_sems, self.local_sems = src, dst, send_sems, recv_sems, local_sems

    def _copy(self, t, kk, slot):
        x, y, c, chips = _place()
        px, py = chips[kk]
        return pltpu.make_async_remote_copy(
            src_ref=_window(self.src[t], self.axes[t], 2 * px + py, self.sizes[t]), dst_ref=self.dst[t].at[slot],
            send_sem=self.send_sems.at[t * 3 + kk], recv_sem=self.recv_sems.at[t * 3 + kk],
            device_id=(px, py, c), device_id_type=MESH)

    def _local(self, t):
        x, y, _, _ = _place()
        me = 2 * x + y
        return pltpu.make_async_copy(_window(self.src[t], self.axes[t], me, self.sizes[t]), self.dst[t].at[me],
                                     self.local_sems.at[t])

    def start(self):
        x, y, _, _ = _place()
        for t in range(self.nt):
            self._local(t).start()
            for kk in range(3):
                self._copy(t, kk, 2 * x + y).start()

    def finish(self):
        _, _, _, chips = _place()
        for t in range(self.nt):
            for kk in range(3):
                px, py = chips[kk]
                self._copy(t, kk, 2 * px + py).wait_recv()
        for t in range(self.nt):
            for kk in range(3):
                px, py = chips[kk]
                self._copy(t, kk, 2 * px + py).wait_send()
            self._local(t).wait()


def _comm_call(plan, operands, name):
    nt = plan.nt

    def body(*refs):
        plan.bind(refs[:nt], refs[nt:2 * nt], *refs[2 * nt:])
        plan.start()
        if hasattr(plan, "forward"):
            plan.forward()
        plan.finish()

    return pl.pallas_call(body, out_shape=tuple(plan.out_shape), in_specs=[ANY] * nt, out_specs=tuple([ANY] * nt),
                          scratch_shapes=plan.scratch, name=name)(*operands)


def _swap_sibling(arrs, name):
    nt = len(arrs)

    def body(*refs):
        src, dst = refs[:nt], refs[nt:2 * nt]
        send_sems, recv_sems = refs[2 * nt:]
        x, y, c, _ = _place()
        cps = [pltpu.make_async_remote_copy(src_ref=src[t], dst_ref=dst[t], send_sem=send_sems.at[t],
                                            recv_sem=recv_sems.at[t], device_id=(x, y, 1 - c), device_id_type=MESH)
               for t in range(nt)]
        for cp in cps:
            cp.start()
        for cp in cps:
            cp.wait()

    return pl.pallas_call(
        body, out_shape=tuple(jax.ShapeDtypeStruct(a.shape, a.dtype) for a in arrs), in_specs=[ANY] * nt,
        out_specs=tuple([ANY] * nt),
        scratch_shapes=[pltpu.SemaphoreType.DMA((nt,)), pltpu.SemaphoreType.DMA((nt,))], name=name)(*arrs)


def _gather8(pack, with_sum, name):
    rows = pack.shape[0]
    flips = [f for f in itertools.product((0, 1), repeat=3) if any(f)]

    def body(p_ref, all_ref, *rest):
        if with_sum:
            sum_ref, send_sems, recv_sems = rest
        else:
            send_sems, recv_sems = rest
        x, y, c, _ = _place()
        me = 4 * x + 2 * y + c

        def peer(f):
            return tuple(1 - v if fl else v for v, fl in zip((x, y, c), f))

        all_ref[me] = p_ref[...]
        cps = []
        for kk, f in enumerate(flips):
            cp = pltpu.make_async_remote_copy(src_ref=p_ref, dst_ref=all_ref.at[me], send_sem=send_sems.at[kk],
                                              recv_sem=recv_sems.at[kk], device_id=peer(f), device_id_type=MESH)
            cp.start()
            cps.append(cp)
        for kk, f in enumerate(flips):
            px, py, pc = peer(f)
            pltpu.make_async_remote_copy(src_ref=p_ref, dst_ref=all_ref.at[4 * px + 2 * py + pc],
                                         send_sem=send_sems.at[kk], recv_sem=recv_sems.at[kk], device_id=peer(f),
                                         device_id_type=MESH).wait_recv()
        for cp in cps:
            cp.wait_send()
        if with_sum:
            acc = all_ref[0]
            for d in range(1, 8):
                acc = acc + all_ref[d]
            sum_ref[...] = acc

    vm = pl.BlockSpec(memory_space=pltpu.VMEM)
    out_shape = [jax.ShapeDtypeStruct((8, rows, LANES), F32)] + ([jax.ShapeDtypeStruct((rows, LANES), F32)] if with_sum else [])
    return pl.pallas_call(
        body, out_shape=tuple(out_shape), in_specs=[vm], out_specs=tuple([vm] * len(out_shape)),
        scratch_shapes=[pltpu.SemaphoreType.DMA((7,)), pltpu.SemaphoreType.DMA((7,))], name=name)(pack)


def _pack(arrs):
    flat = jnp.concatenate([a.reshape(-1).astype(F32) for a in arrs])
    rows = -(-flat.shape[0] // (8 * LANES)) * 8
    return jnp.pad(flat, (0, rows * LANES - flat.shape[0])).reshape(rows, LANES)


def _unpack(flat, shapes):
    out, pos = [], 0
    for shp in shapes:
        size = 1
        for s in shp:
            size *= s
        out.append(flat[pos:pos + size].reshape(shp))
        pos += size
    return out


BIG = ("attn_w_qkv", "attn_w_o", "pool_w", "xattn_w_q", "xattn_w_kv", "xattn_w_o", "ffn_w_up", "ffn_w_down")
BIG_AXIS = ("c", "r", "r", "r", "c", "r", "c", "r")
SMALL_REPL = ("attn_norm", "attn_q_gain", "attn_k_gain", "xattn_norm", "mem_norm", "ffn_norm", "ffn_conv_b", "final_norm")
SMALL_SHARD = ("pool_norm", "pool_scale", "ffn_conv_w")
ORDER = ("attn_norm", "attn_w_qkv", "attn_q_gain", "attn_k_gain", "attn_w_o", "pool_norm", "pool_w", "pool_scale",
         "xattn_norm", "mem_norm", "xattn_w_q", "xattn_w_kv", "xattn_w_o", "ffn_norm", "ffn_w_up", "ffn_conv_w",
         "ffn_conv_b", "ffn_w_down", "final_norm")


def _step(x, mem, tgt, w, m, v):
    seq, d = x.shape
    xi, yi, ci = lax.axis_index("x"), lax.axis_index("y"), lax.axis_index("c")
    chip = 2 * xi + yi
    dff = w["ffn_w_down"].shape[1] * 4
    n_layers = w["ffn_norm"].shape[0]

    def as3d(a):
        return a.reshape(a.shape[-3:])
    shards = [as3d(w[nm]).astype(BF16) for nm in BIG]
    (wq,) = _comm_call(_Gather(shards[:1], BIG_AXIS[:1]), shards[:1], "gather_qkv")
    small_in = [w[nm] for nm in SMALL_SHARD]
    (small_all,) = _gather8(_pack(small_in), False, "gather_small")
    per_chip = [_unpack(small_all[2 * j].reshape(-1), [a.shape for a in small_in]) for j in range(4)]
    pool_norm, pool_scale, conv_w = (jnp.concatenate([per_chip[j][i] for j in range(4)], axis=-1) for i in range(3))

    conv_b = w["ffn_conv_b"].reshape(n_layers, 1, -1)
    tabs = _rope_tables(seq)
    qg2 = jnp.tile(w["attn_q_gain"], (1, 2))
    kg2 = jnp.tile(w["attn_k_gain"], (1, 2))
    mm = functools.partial(_mm)

    saved = {}
    x0 = x
    h0 = _rms_fwd(x0, w["attn_norm"], BF16, "rms_attn")
    qkv = mm(h0, wq, "nn", b_l=0, out_dtype=F32, name="mm_qkv")
    q_r, k_r, k_t, v_b, v_t = _qk_prep(qkv, qg2, kg2, tabs, "qk_prep")
    o_at, lse, wo, wp, wxq, wxkv, wxo, wup, wdn = _flash_fwd(
        q_r, k_r, v_t, _Gather(shards[1:], BIG_AXIS[1:]), shards[1:], "flash_fwd")
    x1, hq0 = mm(o_at, wo, "nn", b_l=0, res=x0, out_dtype=F32, norm_out=(w["xattn_norm"][0:1], BF16), name="mm_attn_o")

    def xattn_fwd(l, xin, hq):
        mn = _rms_fwd(mem, w["mem_norm"][l:l + 1], BF16, f"rms_mem{l}")
        xq = mm(hq, wxq, "nn", b_l=l, scale=X_HEAD_DIM ** -0.5, out_dtype=BF16, name=f"mm_xq{l}")
        kv = mm(mn, wxkv, "nn", b_l=l, out_dtype=BF16, name=f"mm_xkv{l}")
        xo = _xattn_fwd(xq, kv, f"xattn_fwd{l}")
        saved[f"x{l}"] = (hq, mn, xq, kv, xo)
        return mm(xo, wxo, "nn", b_l=l, res=xin, out_dtype=F32, norm_out=(w["ffn_norm"][l:l + 1], BF16), name=f"mm_xo{l}")

    def ffn_fwd(l, xin, hf, norm_out):
        ug = mm(hf, wup, "nn", b_l=l, n=dff, out_dtype=BF16, name=f"mm_up_g{l}")
        uv = mm(hf, wup, "nn", b_l=l, n=dff, b_off=(0, dff), out_dtype=BF16, name=f"mm_up_v{l}")
        act = _conv_gate_fwd(ug, uv, conv_w, conv_b, l, f"conv_gate{l}")
        saved[f"f{l}"] = (hf, ug, uv, act)
        return mm(act, wdn, "nn", b_l=l, res=xin, out_dtype=F32, norm_out=norm_out, name=f"mm_down{l}")

    x2, hf0 = xattn_fwd(0, x1, hq0)
    x3, hp = ffn_fwd(0, x2, hf0, (pool_norm, F32))
    x4 = _pool_fwd(hp, x3, wp, pool_scale, "pool_fwd")
    x5, hf1 = xattn_fwd(1, x4, _rms_fwd(x4, w["xattn_norm"][1:2], BF16, "rms_xq1"))
    xs = [x0, x1, x2, x3, x4, x5, ffn_fwd(1, x5, hf1, None)]
    dres, g_final, loss = _final_loss(xs[6], w["final_norm"].reshape(1, d), tgt, "final_loss")

    grads = {}
    gbuf = {}

    def dw(nm, a, b, layer, full, off=(0, 0), n=None, tn=None):
        gbuf[nm] = _mm(a, b, "tn", out_dtype=BF16, out_full=full, out_l=layer, out_off=off, n=n, tn=tn,
                       alias=gbuf.get(nm), name=f"dw_{nm}{layer}_{off[1]}")

    def ffn_bwd(l, xin, dres):
        hf, ug, uv, act = saved[f"f{l}"]
        dw("ffn_w_down", act, dres, l, wdn.shape)
        dact = _mm(dres, wdn, "nt", b_l=l, out_dtype=BF16, name=f"mm_dact{l}")
        dug, duv, dwg, dwv = _conv_gate_bwd(ug, uv, dact, conv_w, conv_b, l, f"conv_gate_bwd{l}")
        dw("ffn_w_up", hf, dug, l, wup.shape, tn=1408)
        dw("ffn_w_up", hf, duv, l, wup.shape, off=(0, dff), tn=1408)
        dhf = _mm(dug, wup, "nt", b_l=l, n=d, out_dtype=F32, name=f"mm_dhf_g{l}")
        dres, dg = _mm(duv, wup, "nt", b_l=l, n=d, b_off=(0, dff), res=dhf, out_dtype=F32, tm=256,
                       norm_bwd=(xin, w["ffn_norm"][l:l + 1], dres), name=f"mm_dhf_v{l}")
        return dres, dg, jnp.concatenate([dwg[:3], dwv[:3]], axis=1), jnp.concatenate([dwg[3], dwv[3]], axis=0)

    def xattn_bwd(l, xin, dres):
        hq, mn, xq, kv, xo = saved[f"x{l}"]
        dw("xattn_w_o", xo, dres, l, wxo.shape)
        dxo = _mm(dres, wxo, "nt", b_l=l, out_dtype=BF16, name=f"mm_dxo{l}")
        dq, dkv = _xattn_bwd(xq, kv, dxo, f"xattn_bwd{l}")
        dw("xattn_w_q", hq, dq, l, wxq.shape)
        dw("xattn_w_kv", mn, dkv, l, wxkv.shape)
        dmn = _mm(dkv, wxkv, "nt", b_l=l, out_dtype=F32, name=f"mm_dmn{l}")
        _, dg_mem = _rms_bwd(mem, w["mem_norm"][l:l + 1], dmn, None, f"rms_mem_bwd{l}")
        dres, dg = _mm(dq, wxq, "nt", b_l=l, out_dtype=F32, norm_bwd=(xin, w["xattn_norm"][l:l + 1], dres),
                       name=f"mm_dhq{l}")
        return dres, dg, dg_mem

    g_ffn, g_xn, g_mn, g_cw, g_cb = [None] * n_layers, [None] * n_layers, [None] * n_layers, [None] * n_layers, [None] * n_layers
    dres, g_ffn[1], g_cw[1], g_cb[1] = ffn_bwd(1, xs[5], dres)
    dres, g_xn[1], g_mn[1] = xattn_bwd(1, xs[4], dres)
    dhp, g_pw, g_pscale = _pool_bwd(hp, dres, wp, pool_scale, "pool_bwd")
    dres, g_pnorm = _rms_bwd(xs[3], pool_norm, dhp, dres, "rms_pool_bwd")
    dres, g_ffn[0], g_cw[0], g_cb[0] = ffn_bwd(0, xs[2], dres)
    dres, g_xn[0], g_mn[0] = xattn_bwd(0, xs[1], dres)
    dw("attn_w_o", o_at, dres, 0, wo.shape)
    do = _mm(dres, wo, "nt", b_l=0, out_dtype=BF16, name="mm_do")
    gbuf["pool_w"] = g_pw.astype(BF16)
    early = [gbuf[nm] for nm in BIG[1:]]
    dq_r, dk_r, dv, *recv_early = _flash_bwd(q_r, k_r, k_t, v_b, do, o_at, lse, _Scatter(early, BIG_AXIS[1:]), early,
                                             "flash_bwd")
    dqkv, dqg, dkg = _qk_prep_bwd(qkv, dq_r, dk_r, dv, qg2, kg2, tabs, "qk_prep_bwd")
    dw("attn_w_qkv", h0, dqkv, 0, wq.shape)
    grad_x, g_an = _mm(dqkv, wq, "nt", b_l=0, out_dtype=F32, norm_bwd=(x0, w["attn_norm"], dres), name="mm_dh0")

    small_g = {
        "attn_norm": g_an, "attn_q_gain": dqg[:, :HEAD_DIM] + dqg[:, HEAD_DIM:], "attn_k_gain": dkg[:, :HEAD_DIM] + dkg[:, HEAD_DIM:],
        "xattn_norm": jnp.concatenate(g_xn, axis=0), "mem_norm": jnp.concatenate(g_mn, axis=0),
        "ffn_norm": jnp.concatenate(g_ffn, axis=0), "ffn_conv_b": jnp.stack(g_cb, axis=0), "final_norm": g_final.reshape(d),
        "pool_norm": g_pnorm, "pool_scale": g_pscale, "ffn_conv_w": jnp.stack(g_cw, axis=0)}
    names = SMALL_REPL + SMALL_SHARD
    _, total = _gather8(_pack([loss[0, :1]] + [small_g[nm] for nm in names]), True, "reduce_small")
    parts = _unpack(total.reshape(-1), [(1,)] + [small_g[nm].shape for nm in names])
    loss_out = parts[0][0]
    for nm, g in zip(names, parts[1:]):
        if nm in SMALL_SHARD:
            size = w[nm].shape[-1]
            g = lax.dynamic_slice_in_dim(g, chip * size, size, axis=g.ndim - 1)
        grads[nm] = g.reshape(w[nm].shape)

    packed = [_pack([src[nm] for nm in names]) for src in (w, grads, m, v)]
    _, sd, sm, sv = _adamw(packed[0], packed[1], None, packed[2], packed[3], "adamw_small")
    shapes = [w[nm].shape for nm in names]
    delta = dict(zip(names, _unpack(sd.reshape(-1), shapes)))
    new_m = dict(zip(names, _unpack(sm.reshape(-1), shapes)))
    new_v = dict(zip(names, _unpack(sv.reshape(-1), shapes)))

    late = [gbuf[nm] for nm in BIG[:1]]
    recv = list(_comm_call(_Scatter(late, BIG_AXIS[:1]), late, "scatter_qkv")) + recv_early
    sums = []
    for nm, rc in zip(BIG, recv):
        sums.append(_sum4(rc.reshape(4, -1, rc.shape[-1]), f"sum4_{nm}"))
    others = _swap_sibling(sums, "swap_sums")
    for nm, mine, other in zip(BIG, sums, others):
        cols = mine.shape[-1]
        outs = _adamw(w[nm].reshape(-1, cols), mine, other, m[nm].reshape(-1, cols), v[nm].reshape(-1, cols), f"adamw_{nm}")
        grads[nm], delta[nm], new_m[nm], new_v[nm] = (o.reshape(w[nm].shape) for o in outs)

    return loss_out, grad_x, grads, delta, new_m, new_v


def kernel(x, mem, attn_norm, attn_w_qkv, attn_q_gain, attn_k_gain, attn_w_o, pool_norm, pool_w, pool_scale, xattn_norm, mem_norm, xattn_w_q, xattn_w_kv, xattn_w_o, ffn_norm, ffn_w_up, ffn_conv_w, ffn_conv_b, ffn_w_down, final_norm, loss_target, m_attn_norm, m_attn_w_qkv, m_attn_q_gain, m_attn_k_gain, m_attn_w_o, m_pool_norm, m_pool_w, m_pool_scale, m_xattn_norm, m_mem_norm, m_xattn_w_q, m_xattn_w_kv, m_xattn_w_o, m_ffn_norm, m_ffn_w_up, m_ffn_conv_w, m_ffn_conv_b, m_ffn_w_down, m_final_norm, v_attn_norm, v_attn_w_qkv, v_attn_q_gain, v_attn_k_gain, v_attn_w_o, v_pool_norm, v_pool_w, v_pool_scale, v_xattn_norm, v_mem_norm, v_xattn_w_q, v_xattn_w_kv, v_xattn_w_o, v_ffn_norm, v_ffn_w_up, v_ffn_conv_w, v_ffn_conv_b, v_ffn_w_down, v_final_norm):
    given = dict(locals())
    w = {nm: given[nm] for nm in ORDER}
    m = {nm: given["m_" + nm] for nm in ORDER}
    v = {nm: given["v_" + nm] for nm in ORDER}
    seq, d = x.shape[1], x.shape[2]
    loss, grad_x, grads, delta, new_m, new_v = _step(
        x.reshape(seq, d), mem.reshape(mem.shape[1], d), loss_target.reshape(seq, d), w, m, v)
    return (loss, grad_x.reshape(x.shape), *[grads[nm] for nm in ORDER], *[delta[nm] for nm in ORDER],
            *[new_m[nm] for nm in ORDER], *[new_v[nm] for nm in ORDER])
```

```python
import functools
import itertools

import jax
import jax.numpy as jnp
from jax import lax
from jax.experimental import pallas as pl
from jax.experimental.pallas import tpu as pltpu

F32, BF16 = jnp.float32, jnp.bfloat16
EPS = 1e-6
GRID_W = 64
ROPE_THETA = 10000.0
HEAD_DIM = 64
N_HEADS = 16
N_KV = 4
X_HEADS = 4
X_HEAD_DIM = 256
POOL_GROUPS = 4
POOL_GROUP_W = 256
HALO = 16
LANES = 128
ADAM_LR, ADAM_B1, ADAM_B2, ADAM_EPS, ADAM_WD, ADAM_STEP = 0.001, 0.9, 0.999, 1e-08, 0.01, 10
VMEM_LIMIT = 48 * 1024 * 1024
MESH = pl.DeviceIdType.MESH
NEG = -1e30
LOG2E = 1.4426950408889634
FLASH_TQ, FLASH_TK = 256, 2048
ANY = pl.BlockSpec(memory_space=pl.ANY)


def _cp(sem=None):
    return pltpu.CompilerParams(dimension_semantics=sem, vmem_limit_bytes=VMEM_LIMIT)


def _pick(n, cands):
    for c in cands:
        if c <= n and n % c == 0:
            return c
    return n


def _mm(a, b, mode, *, name, out_dtype, tm=None, tn=None, tk=None, n=None, k=None, b_l=None, b_off=(0, 0),
        res=None, scale=None, out_full=None, out_l=None, out_off=(0, 0), alias=None, norm_out=None, norm_bwd=None):
    if mode == "tn":
        K, M = a.shape
    else:
        M, K = a.shape
    bs = b.shape[-2:]
    if mode == "nn":
        K = k or K
        N = n or bs[1]
    elif mode == "nt":
        N = n or bs[0]
    else:
        N = n or bs[1]
    wide = (1408, 1024, 512, 256, 128)
    if mode == "tn":
        tm = tm or (M if M <= 1024 else _pick(M, wide))
        tk = tk or _pick(K, (2048, 1024, 512, 256, 128))
    else:
        tm = _pick(M, (tm or 512, 256, 128))
        tk = tk or (K if K <= 2816 else _pick(K, wide))
    tn = tn or (N if N <= 1536 else _pick(N, wide))
    assert M % tm == 0 and N % tn == 0 and K % tk == 0, (name, M, N, K, tm, tn, tk)
    nk = K // tk
    dims = {"nn": ((1,), (0,)), "nt": ((1,), (1,)), "tn": ((0,), (0,))}[mode]

    j_outer = nk == 1 and mode != "tn"

    def at(f):
        return (lambda j, i, kk: f(i, j, kk)) if j_outer else f

    if mode == "tn":
        a_spec = pl.BlockSpec((tk, tm), at(lambda i, j, kk: (kk, i)))
    else:
        a_spec = pl.BlockSpec((tm, tk), at(lambda i, j, kk: (i, kk)))
    if mode == "nt":
        bb, (d0, d1) = (tn, tk), (b_off[0] // tn, b_off[1] // tk)
        assert b_off[0] % tn == 0 and b_off[1] % tk == 0
        bidx = lambda i, j, kk: (j + d0, kk + d1)
    else:
        bb, (d0, d1) = (tk, tn), (b_off[0] // tk, b_off[1] // tn)
        assert b_off[0] % tk == 0 and b_off[1] % tn == 0
        bidx = lambda i, j, kk: (kk + d0, j + d1)
    if b.ndim == 3:
        b_spec = pl.BlockSpec((None,) + bb, at(lambda i, j, kk: (b_l,) + bidx(i, j, kk)))
    else:
        b_spec = pl.BlockSpec(bb, at(bidx))
    in_specs, operands = [a_spec, b_spec], [a, b]
    if res is not None:
        in_specs.append(pl.BlockSpec((tm, tn), at(lambda i, j, kk: (i, j))))
        operands.append(res)
    aliases = {}
    if alias is not None:
        aliases = {len(operands): 0}
        in_specs.append(ANY)
        operands.append(alias)
    if out_full is None:
        out_shape = jax.ShapeDtypeStruct((M, N), out_dtype)
        out_spec = pl.BlockSpec((tm, tn), at(lambda i, j, kk: (i, j)))
    else:
        assert out_off[0] % tm == 0 and out_off[1] % tn == 0
        o0, o1 = out_off[0] // tm, out_off[1] // tn
        out_shape = jax.ShapeDtypeStruct(out_full, out_dtype)
        out_spec = pl.BlockSpec((None, tm, tn), at(lambda i, j, kk: (out_l, i + o0, j + o1)))
    has_res, has_alias = res is not None, alias is not None
    grid = (N // tn, M // tm, nk) if j_outer else (M // tm, N // tn, nk)
    n_extra = 0
    if norm_out is not None or norm_bwd is not None:
        assert j_outer and tn == N and out_full is None, name
        row = pl.BlockSpec((tm, tn), at(lambda i, j, kk: (i, 0)))
        vec = pl.BlockSpec((1, tn), at(lambda i, j, kk: (0, 0)))
        if norm_out is not None:
            in_specs.append(vec)
            operands.append(norm_out[0])
            n_extra = 1
            out_shape = (out_shape, jax.ShapeDtypeStruct((M, N), norm_out[1]))
            out_spec = (out_spec, row)
        else:
            in_specs += [row, vec, row]
            operands += list(norm_bwd)
            n_extra = 3
            out_shape = (out_shape, jax.ShapeDtypeStruct((1, N), F32))
            out_spec = (out_spec, vec)
    n_out = 1 if n_extra == 0 else 2

    def body(*refs):
        a_ref, b_ref = refs[0], refs[1]
        pos = 2
        res_ref = None
        if has_res:
            res_ref = refs[pos]
            pos += 1
        if has_alias:
            pos += 1
        extra = refs[pos:pos + n_extra]
        pos += n_extra
        o_ref, acc_ref = refs[pos], refs[pos + n_out]
        kk = pl.program_id(2)
        part = lax.dot_general(a_ref[...].astype(BF16), b_ref[...].astype(BF16), (dims, ((), ())),
                               preferred_element_type=F32)

        def finish(acc):
            if scale is not None:
                acc = acc * scale
            if res_ref is not None:
                acc = acc + res_ref[...]
            if norm_out is not None:
                r = lax.rsqrt(jnp.mean(acc * acc, axis=-1, keepdims=True) + EPS)
                refs[pos + 1][...] = (acc * r * extra[0][...]).astype(refs[pos + 1].dtype)
            if norm_bwd is not None:
                x_ref, g_ref, dres_ref = extra
                dg_ref, step = refs[pos + 1], pl.program_id(1)
                xv = x_ref[...]
                r = lax.rsqrt(jnp.mean(xv * xv, axis=-1, keepdims=True) + EPS)
                nv = xv * r
                dgp = jnp.sum(acc * nv, axis=0, keepdims=True)

                @pl.when(step == 0)
                def _():
                    dg_ref[...] = dgp

                @pl.when(step > 0)
                def _():
                    dg_ref[...] += dgp

                dn = acc * g_ref[...]
                acc = dres_ref[...] + r * (dn - nv * jnp.mean(dn * nv, axis=-1, keepdims=True))
            o_ref[...] = acc.astype(o_ref.dtype)

        if nk == 1:
            finish(part)
        else:
            @pl.when(kk == 0)
            def _():
                acc_ref[...] = part

            @pl.when(jnp.logical_and(kk > 0, kk < nk - 1))
            def _():
                acc_ref[...] += part

            @pl.when(kk == nk - 1)
            def _():
                finish(acc_ref[...] + part)

    return pl.pallas_call(
        body, out_shape=out_shape, grid=grid, in_specs=in_specs, out_specs=out_spec,
        scratch_shapes=[pltpu.VMEM((tm, tn) if nk > 1 else (8, 128), F32)], input_output_aliases=aliases,
        compiler_params=_cp(("arbitrary",) * 3 if norm_bwd is not None else ("parallel", "parallel", "arbitrary")),
        name=name)(*operands)


def _rms_fwd(x, gain, out_dtype, name):
    rows, d = x.shape
    tr = _pick(rows, (512, 256))

    def body(x_ref, g_ref, o_ref):
        xv = x_ref[...]
        r = lax.rsqrt(jnp.mean(xv * xv, axis=-1, keepdims=True) + EPS)
        o_ref[...] = (xv * r * g_ref[...]).astype(o_ref.dtype)

    return pl.pallas_call(
        body, out_shape=jax.ShapeDtypeStruct((rows, d), out_dtype), grid=(rows // tr,),
        in_specs=[pl.BlockSpec((tr, d), lambda i: (i, 0)), pl.BlockSpec((1, d), lambda i: (0, 0))],
        out_specs=pl.BlockSpec((tr, d), lambda i: (i, 0)), compiler_params=_cp(("parallel",)), name=name)(x, gain)


def _rms_bwd(x, gain, dh, dres, name):
    rows, d = x.shape
    tr = _pick(rows, (512, 256))
    need_dx = dres is not None

    def body(*refs):
        if need_dx:
            x_ref, g_ref, dh_ref, dres_ref, o_ref, dg_ref = refs
        else:
            x_ref, g_ref, dh_ref, dg_ref = refs
        i = pl.program_id(0)
        xv = x_ref[...]
        dhv = dh_ref[...].astype(F32)
        r = lax.rsqrt(jnp.mean(xv * xv, axis=-1, keepdims=True) + EPS)
        nv = xv * r
        part = jnp.sum(dhv * nv, axis=0, keepdims=True)

        @pl.when(i == 0)
        def _():
            dg_ref[...] = part

        @pl.when(i > 0)
        def _():
            dg_ref[...] += part

        if need_dx:
            dn = dhv * g_ref[...]
            dx = r * (dn - nv * jnp.mean(dn * nv, axis=-1, keepdims=True))
            o_ref[...] = dres_ref[...] + dx

    row_spec = pl.BlockSpec((tr, d), lambda i: (i, 0))
    vec_spec = pl.BlockSpec((1, d), lambda i: (0, 0))
    if need_dx:
        return pl.pallas_call(
            body, out_shape=(jax.ShapeDtypeStruct((rows, d), F32), jax.ShapeDtypeStruct((1, d), F32)),
            grid=(rows // tr,), in_specs=[row_spec, vec_spec, row_spec, row_spec], out_specs=(row_spec, vec_spec),
            compiler_params=_cp(("arbitrary",)), name=name)(x, gain, dh, dres)
    return None, pl.pallas_call(
        body, out_shape=jax.ShapeDtypeStruct((1, d), F32), grid=(rows // tr,),
        in_specs=[row_spec, vec_spec, row_spec], out_specs=vec_spec,
        compiler_params=_cp(("arbitrary",)), name=name)(x, gain, dh)


def _final_loss(x, gain, target, name):
    rows, d = x.shape
    tr = _pick(rows, (512, 256))
    nsteps = rows // tr

    def body(x_ref, g_ref, t_ref, dx_ref, dg_ref, loss_ref, acc_ref):
        i = pl.program_id(0)
        xv = x_ref[...]
        g = g_ref[...]
        r = lax.rsqrt(jnp.mean(xv * xv, axis=-1, keepdims=True) + EPS)
        nv = xv * r
        err = nv * g - t_ref[...]
        dy = err * (1.0 / d)
        dn = dy * g
        dx_ref[...] = r * (dn - nv * jnp.mean(dn * nv, axis=-1, keepdims=True))
        dgp = jnp.sum(dy * nv, axis=0, keepdims=True)
        lp = jnp.sum(err * err, axis=0, keepdims=True)

        @pl.when(i == 0)
        def _():
            dg_ref[...] = dgp
            acc_ref[...] = lp

        @pl.when(i > 0)
        def _():
            dg_ref[...] += dgp
            acc_ref[...] += lp

        @pl.when(i == nsteps - 1)
        def _():
            tot = jnp.sum(acc_ref[...], axis=1, keepdims=True) * (0.5 / d)
            loss_ref[...] = jnp.broadcast_to(tot, loss_ref.shape)

    row_spec = pl.BlockSpec((tr, d), lambda i: (i, 0))
    vec_spec = pl.BlockSpec((1, d), lambda i: (0, 0))
    return pl.pallas_call(
        body, out_shape=(jax.ShapeDtypeStruct((rows, d), F32), jax.ShapeDtypeStruct((1, d), F32),
                         jax.ShapeDtypeStruct((1, LANES), F32)),
        grid=(nsteps,), in_specs=[row_spec, vec_spec, row_spec],
        out_specs=(row_spec, vec_spec, pl.BlockSpec((1, LANES), lambda i: (0, 0))),
        scratch_shapes=[pltpu.VMEM((1, d), F32)], compiler_params=_cp(("arbitrary",)), name=name)(x, gain, target)


def _rope_tables(seq):
    pairs = HEAD_DIM // 4
    lane = jnp.arange(LANES, dtype=jnp.int32) % HEAD_DIM
    by_col, second, pair = lane // (2 * pairs) == 1, (lane % (2 * pairs)) // pairs == 1, lane % pairs
    inv_freq = ROPE_THETA ** (-pair.astype(F32) / pairs)
    t = jnp.arange(seq, dtype=jnp.int32)[:, None]
    pos = jnp.where(by_col[None, :], t % GRID_W, t // GRID_W).astype(F32)
    ang = pos * inv_freq[None, :]
    cos, sin = jnp.cos(ang), jnp.sin(ang)
    return cos, jnp.where(second[None, :], sin, 0.0), jnp.where(second[None, :], 0.0, -sin)


def _pair_norm(xv, lo):
    sq = xv * xv
    s_lo = jnp.sum(jnp.where(lo, sq, 0.0), axis=1, keepdims=True)
    s_hi = jnp.sum(jnp.where(lo, 0.0, sq), axis=1, keepdims=True)
    return lax.rsqrt(jnp.where(lo, s_lo, s_hi) * (1.0 / HEAD_DIM) + EPS)


def _rope(y, c, sp, sm):
    return y * c + pltpu.roll(y, 16, axis=1) * sp + pltpu.roll(y, LANES - 16, axis=1) * sm


def _rope_t(dz, c, sp, sm):
    return dz * c + pltpu.roll(dz * sp, LANES - 16, axis=1) + pltpu.roll(dz * sm, 16, axis=1)


def _qk_prep(qkv, qg2, kg2, tabs, name):
    seq = qkv.shape[0]
    ts = _pick(seq, (256, 128))
    nq, nkp = N_HEADS // 2, N_KV // 2
    qw, kw = N_HEADS * HEAD_DIM, N_KV * HEAD_DIM

    def body(x_ref, qg_ref, kg_ref, c_ref, sp_ref, sm_ref, q_ref, k_ref, kt_ref, v_ref, vt_ref):
        lo = lax.broadcasted_iota(jnp.int32, (ts, LANES), 1) < HEAD_DIM
        top = lax.broadcasted_iota(jnp.int32, (LANES, ts), 0) < HEAD_DIM
        c, sp, sm = c_ref[...], sp_ref[...], sm_ref[...]
        for i in range(nq):
            xv = x_ref[:, i * LANES:(i + 1) * LANES]
            y = xv * _pair_norm(xv, lo) * qg_ref[...]
            q_ref[:, i * LANES:(i + 1) * LANES] = (_rope(y, c, sp, sm) * (LOG2E * HEAD_DIM ** -0.5)).astype(BF16)
        for i in range(nkp):
            xv = x_ref[:, qw + i * LANES:qw + (i + 1) * LANES]
            z = _rope(xv * _pair_norm(xv, lo) * kg_ref[...], c, sp, sm)
            k_ref[:, i * LANES:(i + 1) * LANES] = z.astype(BF16)
            kt_ref[i * LANES:(i + 1) * LANES, :] = z.T.astype(BF16)
            vv = x_ref[:, qw + kw + i * LANES:qw + kw + (i + 1) * LANES]
            v_ref[:, i * LANES:(i + 1) * LANES] = vv.astype(BF16)
            vvt = vv.T
            vt_ref[(2 * i) * LANES:(2 * i + 1) * LANES, :] = jnp.where(top, vvt, 1.0).astype(BF16)
            vt_ref[(2 * i + 1) * LANES:(2 * i + 2) * LANES, :] = jnp.where(top, 1.0, vvt).astype(BF16)

    tab = pl.BlockSpec((ts, LANES), lambda i: (i, 0))
    vec = pl.BlockSpec((1, LANES), lambda i: (0, 0))
    return pl.pallas_call(
        body,
        out_shape=(jax.ShapeDtypeStruct((seq, qw), BF16), jax.ShapeDtypeStruct((seq, kw), BF16),
                   jax.ShapeDtypeStruct((kw, seq), BF16), jax.ShapeDtypeStruct((seq, kw), BF16),
                   jax.ShapeDtypeStruct((N_KV * LANES, seq), BF16)),
        grid=(seq // ts,),
        in_specs=[pl.BlockSpec((ts, qw + 2 * kw), lambda i: (i, 0)), vec, vec, tab, tab, tab],
        out_specs=(pl.BlockSpec((ts, qw), lambda i: (i, 0)), pl.BlockSpec((ts, kw), lambda i: (i, 0)),
                   pl.BlockSpec((kw, ts), lambda i: (0, i)), pl.BlockSpec((ts, kw), lambda i: (i, 0)),
                   pl.BlockSpec((N_KV * LANES, ts), lambda i: (0, i))),
        compiler_params=_cp(("parallel",)), name=name)(qkv, qg2, kg2, *tabs)


def _qk_prep_bwd(qkv, dq, dk, dv, qg2, kg2, tabs, name):
    seq = qkv.shape[0]
    ts = _pick(seq, (256, 128))
    nq, nkp = N_HEADS // 2, N_KV // 2
    qw, kw = N_HEADS * HEAD_DIM, N_KV * HEAD_DIM

    def body(x_ref, dq_ref, dk_ref, dv_ref, qg_ref, kg_ref, c_ref, sp_ref, sm_ref, o_ref, dqg_ref, dkg_ref):
        step = pl.program_id(0)
        lo = lax.broadcasted_iota(jnp.int32, (ts, LANES), 1) < HEAD_DIM
        c, sp, sm = c_ref[...], sp_ref[...], sm_ref[...]

        def one(xv, dz, gain):
            r = _pair_norm(xv, lo)
            nv = xv * r
            dy = _rope_t(dz, c, sp, sm)
            dgp = jnp.sum(dy * nv, axis=0, keepdims=True)
            dn = dy * gain
            t = dn * nv
            m_lo = jnp.sum(jnp.where(lo, t, 0.0), axis=1, keepdims=True)
            m_hi = jnp.sum(jnp.where(lo, 0.0, t), axis=1, keepdims=True)
            m = jnp.where(lo, m_lo, m_hi) * (1.0 / HEAD_DIM)
            return r * (dn - nv * m), dgp

        dqg = jnp.zeros((1, LANES), F32)
        for i in range(nq):
            sl = slice(i * LANES, (i + 1) * LANES)
            dx, dgp = one(x_ref[:, sl], dq_ref[:, sl] * (HEAD_DIM ** -0.5), qg_ref[...])
            o_ref[:, sl] = dx.astype(BF16)
            dqg = dqg + dgp
        dkg = jnp.zeros((1, LANES), F32)
        for i in range(nkp):
            sl = slice(i * LANES, (i + 1) * LANES)
            dx, dgp = one(x_ref[:, qw + i * LANES:qw + (i + 1) * LANES], dk_ref[:, sl], kg_ref[...])
            o_ref[:, qw + i * LANES:qw + (i + 1) * LANES] = dx.astype(BF16)
            dkg = dkg + dgp
            o_ref[:, qw + kw + i * LANES:qw + kw + (i + 1) * LANES] = dv_ref[:, sl].astype(BF16)

        @pl.when(step == 0)
        def _():
            dqg_ref[...] = dqg
            dkg_ref[...] = dkg

        @pl.when(step > 0)
        def _():
            dqg_ref[...] += dqg
            dkg_ref[...] += dkg

    tab = pl.BlockSpec((ts, LANES), lambda i: (i, 0))
    vec = pl.BlockSpec((1, LANES), lambda i: (0, 0))
    return pl.pallas_call(
        body,
        out_shape=(jax.ShapeDtypeStruct((seq, qw + 2 * kw), BF16), jax.ShapeDtypeStruct((1, LANES), F32),
                   jax.ShapeDtypeStruct((1, LANES), F32)),
        grid=(seq // ts,),
        in_specs=[pl.BlockSpec((ts, qw + 2 * kw), lambda i: (i, 0)), pl.BlockSpec((ts, qw), lambda i: (i, 0)),
                  pl.BlockSpec((ts, kw), lambda i: (i, 0)), pl.BlockSpec((ts, kw), lambda i: (i, 0)),
                  vec, vec, tab, tab, tab],
        out_specs=(pl.BlockSpec((ts, qw + 2 * kw), lambda i: (i, 0)), vec, vec),
        compiler_params=_cp(("arbitrary",)), name=name)(qkv, dq, dk, dv, qg2, kg2, *tabs)


def _slot(blk, off0, tq):
    half = lax.broadcasted_iota(jnp.int32, (tq, LANES), 1) // HEAD_DIM
    keep = half == jnp.where(off0, 0, 1)
    parts = []
    for p in range(2):
        pair = blk[:, p * LANES:(p + 1) * LANES].astype(F32)
        rolled = pltpu.roll(pair, HEAD_DIM, axis=1)
        parts.append(jnp.where(keep, jnp.where(off0, pair, rolled), 0.0))
        parts.append(jnp.where(keep, jnp.where(off0, rolled, pair), 0.0))
    return jnp.concatenate(parts, axis=0)


def _unslot(x4, off0, tq):
    lo = lax.broadcasted_iota(jnp.int32, (tq, LANES), 1) < HEAD_DIM
    pairs = []
    for p in range(2):
        h0 = x4[(2 * p) * tq:(2 * p + 1) * tq]
        h1 = x4[(2 * p + 1) * tq:(2 * p + 2) * tq]
        a = jnp.where(off0, h0, pltpu.roll(h0, HEAD_DIM, axis=1))
        b = jnp.where(off0, pltpu.roll(h1, HEAD_DIM, axis=1), h1)
        pairs.append(jnp.where(lo, a, b))
    return jnp.concatenate(pairs, axis=1)


def _flash_fwd(q, k, vt, plan, shards, name):
    seq = q.shape[0]
    tq = _pick(seq, (FLASH_TQ, 128))
    tk = _pick(seq, (FLASH_TK, 512, 256, 128))
    nq, nkv = seq // tq, seq // tk
    gw = 4 * HEAD_DIM
    nt = plan.nt

    def body(q_ref, k_ref, vt_ref, *rest):
        o_ref, lse_ref = rest[nt:nt + 2]
        q4_ref, m_ref, acc_ref = rest[2 * nt + 2:2 * nt + 5]
        plan.bind(rest[:nt], rest[nt + 2:2 * nt + 2], *rest[2 * nt + 5:])
        g, qi, ki = pl.program_id(0), pl.program_id(1), pl.program_id(2)
        off0 = (g % 2) == 0
        inner0 = jnp.logical_and(qi == 0, ki == 0)

        @pl.when(jnp.logical_and(g == 0, inner0))
        def _():
            plan.start()

        @pl.when(jnp.logical_and(g == N_KV - 1, inner0))
        def _():
            plan.forward()

        @pl.when(ki == 0)
        def _():
            q4_ref[...] = _slot(q_ref[...], off0, tq).astype(BF16)
            m_ref[...] = jnp.full(m_ref.shape, NEG, F32)
            acc_ref[...] = jnp.zeros(acc_ref.shape, F32)

        st = lax.dot_general(k_ref[...], q4_ref[...], (((1,), (1,)), ((), ())), preferred_element_type=F32)
        m_old = m_ref[...]
        m_new = jnp.maximum(m_old, jnp.max(st, axis=0, keepdims=True))
        pt = jnp.exp2(st - m_new).astype(BF16)
        acc_ref[...] = jnp.exp2(m_old - m_new) * acc_ref[...] + jnp.dot(vt_ref[...], pt, preferred_element_type=F32)
        m_ref[...] = m_new

        @pl.when(ki == nkv - 1)
        def _():
            acc = acc_ref[...]
            l = jnp.where(off0, acc[HEAD_DIM:HEAD_DIM + 1], acc[0:1])
            o4 = acc.T
            o4 = o4 / pltpu.roll(o4, HEAD_DIM, axis=1)
            o_ref[...] = _unslot(o4, off0, tq).astype(o_ref.dtype)
            lse_ref[...] = jnp.broadcast_to(m_ref[...] + jnp.log2(l), lse_ref.shape)

        @pl.when(jnp.logical_and(g == N_KV - 1, jnp.logical_and(qi == nq - 1, ki == nkv - 1)))
        def _():
            plan.finish()

    return pl.pallas_call(
        body,
        out_shape=(jax.ShapeDtypeStruct((seq, N_HEADS * HEAD_DIM), BF16),
                   jax.ShapeDtypeStruct((N_KV * nq * 8, 4 * tq), F32), *plan.out_shape),
        grid=(N_KV, nq, nkv),
        in_specs=[pl.BlockSpec((tq, gw), lambda g, qi, ki: (qi, g)),
                  pl.BlockSpec((tk, LANES), lambda g, qi, ki: (ki, g // 2)),
                  pl.BlockSpec((LANES, tk), lambda g, qi, ki: (g, ki))] + [ANY] * nt,
        out_specs=(pl.BlockSpec((tq, gw), lambda g, qi, ki: (qi, g)),
                   pl.BlockSpec((8, 4 * tq), lambda g, qi, ki: (g * nq + qi, 0)), *([ANY] * nt)),
        scratch_shapes=[pltpu.VMEM((4 * tq, LANES), BF16), pltpu.VMEM((1, 4 * tq), F32),
                        pltpu.VMEM((LANES, 4 * tq), F32)] + plan.scratch,
        compiler_params=_cp(("arbitrary", "arbitrary", "arbitrary")), name=name)(q, k, vt, *shards)


def _flash_bwd(q, k, kt, v, do, o, lse, plan, grads, name):
    seq = q.shape[0]
    tq = _pick(seq, (FLASH_TQ, 128))
    tk = _pick(seq, (FLASH_TK, 512, 256, 128))
    nq, nkv = seq // tq, seq // tk
    gw = 4 * HEAD_DIM
    nt = plan.nt

    def body(q_ref, k_ref, kt_ref, v_ref, do_ref, o_ref, lse_ref, *rest):
        dq_ref, dk_ref, dv_ref = rest[nt:nt + 3]
        q4_ref, do4_ref, delta_ref, dqt_ref = rest[2 * nt + 3:2 * nt + 7]
        plan.bind(rest[:nt], rest[nt + 3:2 * nt + 3], *rest[2 * nt + 7:])
        g, qi, ki = pl.program_id(0), pl.program_id(1), pl.program_id(2)
        off0 = (g % 2) == 0

        @pl.when(jnp.logical_and(g == 0, jnp.logical_and(qi == 0, ki == 0)))
        def _():
            plan.start()

        @pl.when(jnp.logical_and(g % 2 == 0, jnp.logical_and(qi == 0, ki == 0)))
        def _():
            dk_ref[...] = jnp.zeros(dk_ref.shape, F32)
            dv_ref[...] = jnp.zeros(dv_ref.shape, F32)

        @pl.when(ki == 0)
        def _():
            q4_ref[...] = _slot(q_ref[...], off0, tq).astype(BF16)
            do4 = _slot(do_ref[...], off0, tq)
            do4_ref[...] = do4.astype(BF16)
            o4 = _slot(o_ref[...], off0, tq)
            delta_ref[...] = jnp.sum((do4 * o4).T, axis=0, keepdims=True)
            dqt_ref[...] = jnp.zeros(dqt_ref.shape, F32)

        q4, do4 = q4_ref[...], do4_ref[...]
        st = lax.dot_general(k_ref[...], q4, (((1,), (1,)), ((), ())), preferred_element_type=F32)
        pt = jnp.exp2(st - lse_ref[0:1, :])
        dpt = lax.dot_general(v_ref[...], do4, (((1,), (1,)), ((), ())), preferred_element_type=F32)
        dst = (pt * (dpt - delta_ref[...])).astype(BF16)
        rows = pl.ds(pl.multiple_of(ki * tk, tk), tk)
        dv_ref[rows, :] += jnp.dot(pt.astype(BF16), do4, preferred_element_type=F32)
        dk_ref[rows, :] += jnp.dot(dst, q4, preferred_element_type=F32) * (1.0 / LOG2E)
        dqt_ref[...] += jnp.dot(kt_ref[...], dst, preferred_element_type=F32)

        @pl.when(ki == nkv - 1)
        def _():
            dq_ref[...] = _unslot(dqt_ref[...].T, off0, tq)

        @pl.when(jnp.logical_and(g == N_KV - 1, jnp.logical_and(qi == nq - 1, ki == nkv - 1)))
        def _():
            plan.finish()

    return pl.pallas_call(
        body,
        out_shape=(jax.ShapeDtypeStruct((seq, N_HEADS * HEAD_DIM), F32),
                   jax.ShapeDtypeStruct((seq, N_KV * HEAD_DIM), F32), jax.ShapeDtypeStruct((seq, N_KV * HEAD_DIM), F32),
                   *plan.out_shape),
        grid=(N_KV, nq, nkv),
        in_specs=[pl.BlockSpec((tq, gw), lambda g, qi, ki: (qi, g)),
                  pl.BlockSpec((tk, LANES), lambda g, qi, ki: (ki, g // 2)),
                  pl.BlockSpec((LANES, tk), lambda g, qi, ki: (g // 2, ki)),
                  pl.BlockSpec((tk, LANES), lambda g, qi, ki: (ki, g // 2)),
                  pl.BlockSpec((tq, gw), lambda g, qi, ki: (qi, g)),
                  pl.BlockSpec((tq, gw), lambda g, qi, ki: (qi, g)),
                  pl.BlockSpec((8, 4 * tq), lambda g, qi, ki: (g * nq + qi, 0))] + [ANY] * nt,
        out_specs=(pl.BlockSpec((tq, gw), lambda g, qi, ki: (qi, g)),
                   pl.BlockSpec((seq, LANES), lambda g, qi, ki: (0, g // 2)),
                   pl.BlockSpec((seq, LANES), lambda g, qi, ki: (0, g // 2)), *([ANY] * nt)),
        scratch_shapes=[pltpu.VMEM((4 * tq, LANES), BF16), pltpu.VMEM((4 * tq, LANES), BF16),
                        pltpu.VMEM((1, 4 * tq), F32), pltpu.VMEM((LANES, 4 * tq), F32)] + plan.scratch,
        compiler_params=_cp(("arbitrary", "arbitrary", "arbitrary")), name=name)(q, k, kt, v, do, o, lse, *grads)


def _xattn_fwd(q, kv, name):
    seq, d = q.shape
    mlen = kv.shape[0]
    tq = _pick(seq, (512, 256))

    def body(q_ref, k_ref, v_ref, o_ref):
        for h in range(X_HEADS):
            sl = slice(h * X_HEAD_DIM, (h + 1) * X_HEAD_DIM)
            s = lax.dot_general(q_ref[:, sl], k_ref[:, sl], (((1,), (1,)), ((), ())), preferred_element_type=F32)
            e = jnp.exp(s - jnp.max(s, axis=-1, keepdims=True))
            p = e / jnp.sum(e, axis=-1, keepdims=True)
            o_ref[:, sl] = jnp.dot(p.astype(BF16), v_ref[:, sl], preferred_element_type=F32).astype(o_ref.dtype)

    return pl.pallas_call(
        body, out_shape=jax.ShapeDtypeStruct((seq, d), BF16), grid=(seq // tq,),
        in_specs=[pl.BlockSpec((tq, d), lambda i: (i, 0)), pl.BlockSpec((mlen, d), lambda i: (0, 0)),
                  pl.BlockSpec((mlen, d), lambda i: (0, 1))],
        out_specs=pl.BlockSpec((tq, d), lambda i: (i, 0)), compiler_params=_cp(("parallel",)), name=name)(q, kv, kv)


def _xattn_bwd(q, kv, do, name):
    seq, d = q.shape
    mlen = kv.shape[0]
    tq = _pick(seq, (512, 256))
    scale = X_HEAD_DIM ** -0.5

    def body(q_ref, k_ref, v_ref, do_ref, dq_ref, dkv_ref):
        i = pl.program_id(0)

        @pl.when(i == 0)
        def _():
            dkv_ref[...] = jnp.zeros(dkv_ref.shape, F32)

        for h in range(X_HEADS):
            sl = slice(h * X_HEAD_DIM, (h + 1) * X_HEAD_DIM)
            qh, kh, vh = q_ref[:, sl], k_ref[:, sl], v_ref[:, sl]
            doh = do_ref[:, sl].astype(BF16)
            st = lax.dot_general(kh, qh, (((1,), (1,)), ((), ())), preferred_element_type=F32)
            e = jnp.exp(st - jnp.max(st, axis=0, keepdims=True))
            pt = e / jnp.sum(e, axis=0, keepdims=True)
            dpt = lax.dot_general(vh, doh, (((1,), (1,)), ((), ())), preferred_element_type=F32)
            dst = (pt * (dpt - jnp.sum(pt * dpt, axis=0, keepdims=True))).astype(BF16)
            dkv_ref[:, sl] += jnp.dot(dst, qh, preferred_element_type=F32)
            dkv_ref[:, d + h * X_HEAD_DIM:d + (h + 1) * X_HEAD_DIM] += jnp.dot(pt.astype(BF16), doh,
                                                                                 preferred_element_type=F32)
            dqh = lax.dot_general(dst, kh, (((0,), (0,)), ((), ())), preferred_element_type=F32)
            dq_ref[:, sl] = (dqh * scale).astype(dq_ref.dtype)

    return pl.pallas_call(
        body, out_shape=(jax.ShapeDtypeStruct((seq, d), BF16), jax.ShapeDtypeStruct((mlen, 2 * d), F32)),
        grid=(seq // tq,),
        in_specs=[pl.BlockSpec((tq, d), lambda i: (i, 0)), pl.BlockSpec((mlen, d), lambda i: (0, 0)),
                  pl.BlockSpec((mlen, d), lambda i: (0, 1)), pl.BlockSpec((tq, d), lambda i: (i, 0))],
        out_specs=(pl.BlockSpec((tq, d), lambda i: (i, 0)), pl.BlockSpec((mlen, 2 * d), lambda i: (0, 0))),
        compiler_params=_cp(("arbitrary",)), name=name)(q, kv, kv, do)


def _halo_specs(tr, tc, seq, col):
    per, last = tr // HALO, seq // HALO - 1
    return [pl.BlockSpec((tr, tc), lambda j, r: (r, col(j))),
            pl.BlockSpec((HALO, tc), lambda j, r: (jnp.maximum(r * per - 1, 0), col(j))),
            pl.BlockSpec((HALO, tc), lambda j, r: (jnp.minimum((r + 1) * per, last), col(j)))]


def _extend(main_ref, prev_ref, next_ref, r, nr):
    pv = (r > 0).astype(F32)
    nv = (r < nr - 1).astype(F32)
    return jnp.concatenate([prev_ref[...].astype(F32) * pv, main_ref[...].astype(F32),
                            next_ref[...].astype(F32) * nv], axis=0)


def _conv3(e, w_ref, n):
    return pltpu.roll(e, 1, axis=0) * w_ref[0:1, :] + e * w_ref[1:2, :] + pltpu.roll(e, n - 1, axis=0) * w_ref[2:3, :]


def _halo_specs_rows(tr, tc, seq):
    per, last = tr // HALO, seq // HALO - 1
    return [pl.BlockSpec((tr, tc), lambda r, j: (r, j)),
            pl.BlockSpec((HALO, tc), lambda r, j: (jnp.maximum(r * per - 1, 0), j)),
            pl.BlockSpec((HALO, tc), lambda r, j: (jnp.minimum((r + 1) * per, last), j))]


def _gate_down(ug, uv, cw, cb, wdn, layer, xres, norm_out, name):
    seq, f = ug.shape
    d = xres.shape[1]
    tc = 256
    tr = _pick(seq, (512, 256))
    nc, nr = f // tc, seq // tr
    n = tr + 2 * HALO
    has_norm = norm_out is not None

    def body(g_ref, gp_ref, gn_ref, v_ref, vp_ref, vn_ref, wg_ref, wv_ref, bg_ref, bv_ref, wd_ref, x_ref, *rest):
        gain_ref = rest[0] if has_norm else None
        act_ref, o_ref = rest[has_norm:has_norm + 2]
        acc_ref = rest[-1]
        r, j = pl.program_id(0), pl.program_id(1)
        cg = _conv3(_extend(g_ref, gp_ref, gn_ref, r, nr), wg_ref, n)[HALO:HALO + tr] + bg_ref[...]
        cv = _conv3(_extend(v_ref, vp_ref, vn_ref, r, nr), wv_ref, n)[HALO:HALO + tr] + bv_ref[...]
        act = (cg * jax.nn.sigmoid(cg) * cv).astype(BF16)
        act_ref[...] = act
        part = jnp.dot(act, wd_ref[...], preferred_element_type=F32)

        @pl.when(j == 0)
        def _():
            acc_ref[...] = part

        @pl.when(j > 0)
        def _():
            acc_ref[...] += part

        @pl.when(j == nc - 1)
        def _():
            out = x_ref[...] + acc_ref[...]
            o_ref[...] = out
            if has_norm:
                rr = lax.rsqrt(jnp.mean(out * out, axis=-1, keepdims=True) + EPS)
                rest[has_norm + 2][...] = (out * rr * gain_ref[...]).astype(rest[has_norm + 2].dtype)

    w_spec = lambda shift: pl.BlockSpec((None, 3, tc), lambda r, j: (layer, 0, j + shift))
    b_spec = lambda shift: pl.BlockSpec((None, 1, tc), lambda r, j: (layer, 0, j + shift))
    row = pl.BlockSpec((tr, d), lambda r, j: (r, 0))
    vec = pl.BlockSpec((1, d), lambda r, j: (0, 0))
    in_specs = _halo_specs_rows(tr, tc, seq) * 2 + [w_spec(0), w_spec(nc), b_spec(0), b_spec(nc),
                                                    pl.BlockSpec((None, tc, d), lambda r, j: (layer, j, 0)), row]
    operands = [ug, ug, ug, uv, uv, uv, cw, cw, cb, cb, wdn, xres]
    out_shape = [jax.ShapeDtypeStruct((seq, f), BF16), jax.ShapeDtypeStruct((seq, d), F32)]
    out_specs = [pl.BlockSpec((tr, tc), lambda r, j: (r, j)), row]
    if has_norm:
        in_specs.append(vec)
        operands.append(norm_out[0])
        out_shape.append(jax.ShapeDtypeStruct((seq, d), norm_out[1]))
        out_specs.append(row)
    return pl.pallas_call(
        body, out_shape=tuple(out_shape), grid=(nr, nc), in_specs=in_specs, out_specs=tuple(out_specs),
        scratch_shapes=[pltpu.VMEM((tr, d), F32)], compiler_params=_cp(("parallel", "arbitrary")), name=name)(*operands)


def _gate_up_bwd(ug, uv, dact, cw, cb, wup, layer, xin, gain, dres, name):
    seq, f = ug.shape
    d = xin.shape[1]
    tc = 256
    tr = _pick(seq, (512, 256))
    nc, nr = f // tc, seq // tr
    n = tr + 2 * HALO

    def body(g_ref, gp_ref, gn_ref, v_ref, vp_ref, vn_ref, d_ref, dp_ref, dn_ref, wg_ref, wv_ref, bg_ref, bv_ref,
             ug_w_ref, uv_w_ref, x_ref, gain_ref, dres_ref, dug_ref, duv_ref, dwg_ref, dwv_ref, o_ref, dg_ref, acc_ref):
        r, j = pl.program_id(0), pl.program_id(1)
        eg = _extend(g_ref, gp_ref, gn_ref, r, nr)
        ev = _extend(v_ref, vp_ref, vn_ref, r, nr)
        da = _extend(d_ref, dp_ref, dn_ref, r, nr)
        eg3 = (pltpu.roll(eg, 1, axis=0), eg, pltpu.roll(eg, n - 1, axis=0))
        ev3 = (pltpu.roll(ev, 1, axis=0), ev, pltpu.roll(ev, n - 1, axis=0))
        cg = eg3[0] * wg_ref[0:1, :] + eg3[1] * wg_ref[1:2, :] + eg3[2] * wg_ref[2:3, :] + bg_ref[...]
        cv = ev3[0] * wv_ref[0:1, :] + ev3[1] * wv_ref[1:2, :] + ev3[2] * wv_ref[2:3, :] + bv_ref[...]
        sg = jax.nn.sigmoid(cg)
        dcv = da * (cg * sg)
        dcg = da * cv * (sg * (1.0 + cg * (1.0 - sg)))

        def back(dc, e3, w_ref, du_ref, dw_ref, up_ref):
            du = (pltpu.roll(dc, n - 1, axis=0) * w_ref[0:1, :] + dc * w_ref[1:2, :]
                  + pltpu.roll(dc, 1, axis=0) * w_ref[2:3, :])[HALO:HALO + tr].astype(BF16)
            du_ref[...] = du
            dcm = dc[HALO:HALO + tr]
            taps = [jnp.sum(dcm * e[HALO:HALO + tr], axis=0, keepdims=True) for e in e3] + [
                    jnp.sum(dcm, axis=0, keepdims=True)]
            part = jnp.concatenate(taps + [jnp.zeros((4, tc), F32)], axis=0)

            @pl.when(r == 0)
            def _():
                dw_ref[j] = part

            @pl.when(r > 0)
            def _():
                dw_ref[j] += part

            return lax.dot_general(du, up_ref[...], (((1,), (1,)), ((), ())), preferred_element_type=F32)

        dh = back(dcg, eg3, wg_ref, dug_ref, dwg_ref, ug_w_ref) + back(dcv, ev3, wv_ref, duv_ref, dwv_ref, uv_w_ref)

        @pl.when(j == 0)
        def _():
            acc_ref[...] = dh

        @pl.when(j > 0)
        def _():
            acc_ref[...] += dh

        @pl.when(j == nc - 1)
        def _():
            dhv, xv = acc_ref[...], x_ref[...]
            rr = lax.rsqrt(jnp.mean(xv * xv, axis=-1, keepdims=True) + EPS)
            nv = xv * rr
            dn = dhv * gain_ref[...]
            o_ref[...] = dres_ref[...] + rr * (dn - nv * jnp.mean(dn * nv, axis=-1, keepdims=True))
            dgp = jnp.sum(dhv * nv, axis=0, keepdims=True)

            @pl.when(r == 0)
            def _():
                dg_ref[...] = dgp

            @pl.when(r > 0)
            def _():
                dg_ref[...] += dgp

    w_spec = lambda shift: pl.BlockSpec((None, 3, tc), lambda r, j: (layer, 0, j + shift))
    b_spec = lambda shift: pl.BlockSpec((None, 1, tc), lambda r, j: (layer, 0, j + shift))
    up_spec = lambda shift: pl.BlockSpec((None, d, tc), lambda r, j: (layer, 0, j + shift))
    row = pl.BlockSpec((tr, d), lambda r, j: (r, 0))
    vec = pl.BlockSpec((1, d), lambda r, j: (0, 0))
    out_rows = pl.BlockSpec((tr, tc), lambda r, j: (r, j))
    out_acc = pl.BlockSpec((nc, 8, tc), lambda r, j: (0, 0, 0))
    return pl.pallas_call(
        body,
        out_shape=(jax.ShapeDtypeStruct((seq, f), BF16), jax.ShapeDtypeStruct((seq, f), BF16),
                   jax.ShapeDtypeStruct((nc, 8, tc), F32), jax.ShapeDtypeStruct((nc, 8, tc), F32),
                   jax.ShapeDtypeStruct((seq, d), F32), jax.ShapeDtypeStruct((1, d), F32)),
        grid=(nr, nc),
        in_specs=_halo_specs_rows(tr, tc, seq) * 3 + [w_spec(0), w_spec(nc), b_spec(0), b_spec(nc), up_spec(0), up_spec(nc),
                                                      row, vec, row],
        out_specs=(out_rows, out_rows, out_acc, out_acc, row, vec),
        scratch_shapes=[pltpu.VMEM((tr, d), F32)], compiler_params=_cp(("arbitrary", "arbitrary")),
        name=name)(ug, ug, ug, uv, uv, uv, dact, dact, dact, cw, cw, cb, cb, wup, wup, xin, gain, dres)


def _pool_count(g, r, tr, n, seq):
    half = jnp.left_shift(1, g)
    t = r * tr - HALO + lax.broadcasted_iota(jnp.int32, (n, 1), 0)
    cnt = jnp.minimum(t + half, seq) - jnp.maximum(t - half, 0)
    return jnp.maximum(cnt, 1).astype(F32)


def _by_group(g, levels):
    out = levels[3]
    for i in (2, 1, 0):
        out = jnp.where(g == i, levels[i], out)
    return out


def _pool_mixed(e, g, cnt, n):
    w2 = e + pltpu.roll(e, 1, axis=0)
    w4 = pltpu.roll(w2, 1, axis=0) + pltpu.roll(w2, n - 1, axis=0)
    w8 = pltpu.roll(w4, 2, axis=0) + pltpu.roll(w4, n - 2, axis=0)
    w16 = pltpu.roll(w8, 4, axis=0) + pltpu.roll(w8, n - 4, axis=0)
    return _by_group(g, (w2, w4, w8, w16)) / cnt - e


def _pool_fwd(hp, xres, pw, scale, name):
    seq, d = hp.shape
    tc = POOL_GROUP_W
    tr = _pick(seq, (512, 256))
    nr = seq // tr
    n = tr + 2 * HALO

    def body(h_ref, hp_ref, hn_ref, x_ref, w_ref, s_ref, o_ref):
        g, r = pl.program_id(0), pl.program_id(1)
        e = _extend(h_ref, hp_ref, hn_ref, r, nr)
        mixed = _pool_mixed(e, g, _pool_count(g, r, tr, n, seq), n)[HALO:HALO + tr]
        y = jnp.dot(mixed.astype(BF16), w_ref[...], preferred_element_type=F32)
        o_ref[...] = x_ref[...] + y * s_ref[...]

    return pl.pallas_call(
        body, out_shape=jax.ShapeDtypeStruct((seq, d), F32), grid=(POOL_GROUPS, nr),
        in_specs=_halo_specs(tr, tc, seq, lambda j: j) + [
            pl.BlockSpec((tr, tc), lambda j, r: (r, j)), pl.BlockSpec((None, tc, tc), lambda j, r: (j, 0, 0)),
            pl.BlockSpec((1, tc), lambda j, r: (0, j))],
        out_specs=pl.BlockSpec((tr, tc), lambda j, r: (r, j)),
        compiler_params=_cp(("parallel", "parallel")), name=name)(hp, hp, hp, xres, pw, scale)


def _pool_bwd(hp, dy, pw, scale, name):
    seq, d = hp.shape
    tc = POOL_GROUP_W
    tr = _pick(seq, (512, 256))
    nr = seq // tr
    n = tr + 2 * HALO

    def body(h_ref, hp_ref, hn_ref, d_ref, dp_ref, dn_ref, w_ref, s_ref, dh_ref, dw_ref, ds_ref):
        g, r = pl.program_id(0), pl.program_id(1)
        cnt = _pool_count(g, r, tr, n, seq)
        e = _extend(h_ref, hp_ref, hn_ref, r, nr)
        mixed = _pool_mixed(e, g, cnt, n)[HALO:HALO + tr].astype(BF16)
        dye = _extend(d_ref, dp_ref, dn_ref, r, nr)
        dyp = (dye * s_ref[...]).astype(BF16)
        dmixed = lax.dot_general(dyp, w_ref[...], (((1,), (1,)), ((), ())), preferred_element_type=F32)
        dwin = dmixed / cnt
        m2 = dwin + pltpu.roll(dwin, n - 1, axis=0)
        m4 = pltpu.roll(m2, 1, axis=0) + pltpu.roll(m2, n - 1, axis=0)
        m8 = pltpu.roll(m4, 2, axis=0) + pltpu.roll(m4, n - 2, axis=0)
        m16 = pltpu.roll(m8, 4, axis=0) + pltpu.roll(m8, n - 4, axis=0)
        dh_ref[...] = (_by_group(g, (m2, m4, m8, m16)) - dmixed)[HALO:HALO + tr]
        ypre = jnp.dot(mixed, w_ref[...], preferred_element_type=F32)
        dsp = jnp.sum(d_ref[...] * ypre, axis=0, keepdims=True)
        dwp = lax.dot_general(mixed, dyp[HALO:HALO + tr], (((0,), (0,)), ((), ())), preferred_element_type=F32)

        @pl.when(r == 0)
        def _():
            dw_ref[...] = dwp
            ds_ref[...] = dsp

        @pl.when(r > 0)
        def _():
            dw_ref[...] += dwp
            ds_ref[...] += dsp

    return pl.pallas_call(
        body,
        out_shape=(jax.ShapeDtypeStruct((seq, d), F32), jax.ShapeDtypeStruct((POOL_GROUPS, tc, tc), F32),
                   jax.ShapeDtypeStruct((1, d), F32)),
        grid=(POOL_GROUPS, nr),
        in_specs=_halo_specs(tr, tc, seq, lambda j: j) * 2 + [
            pl.BlockSpec((None, tc, tc), lambda j, r: (j, 0, 0)), pl.BlockSpec((1, tc), lambda j, r: (0, j))],
        out_specs=(pl.BlockSpec((tr, tc), lambda j, r: (r, j)), pl.BlockSpec((None, tc, tc), lambda j, r: (j, 0, 0)),
                   pl.BlockSpec((1, tc), lambda j, r: (0, j))),
        compiler_params=_cp(("parallel", "arbitrary")), name=name)(hp, hp, hp, dy, dy, dy, pw, scale)


def _adamw_math(w, g, m, v):
    m = ADAM_B1 * m + (1.0 - ADAM_B1) * g
    v = ADAM_B2 * v + (1.0 - ADAM_B2) * (g * g)
    m_hat = m / (1.0 - ADAM_B1 ** ADAM_STEP)
    v_hat = v / (1.0 - ADAM_B2 ** ADAM_STEP)
    delta = -ADAM_LR * (m_hat / (jnp.sqrt(v_hat) + ADAM_EPS) + ADAM_WD * w)
    return delta, m, v


def _adamw(w, ga, gb, m, v, name):
    rows, cols = w.shape
    tr = _pick(rows, (256, 128, 64, 32, 16, 8))
    two = gb is not None

    def body(*refs):
        if two:
            w_ref, ga_ref, gb_ref, m_ref, v_ref, g_out, d_out, m_out, v_out = refs
            g = ga_ref[...] + gb_ref[...]
        else:
            w_ref, ga_ref, m_ref, v_ref, g_out, d_out, m_out, v_out = refs
            g = ga_ref[...]
        delta, m, v = _adamw_math(w_ref[...], g, m_ref[...], v_ref[...])
        g_out[...] = g
        d_out[...] = delta
        m_out[...] = m
        v_out[...] = v

    spec = pl.BlockSpec((tr, cols), lambda i: (i, 0))
    ops = [w, ga] + ([gb] if two else []) + [m, v]
    return pl.pallas_call(
        body, out_shape=tuple(jax.ShapeDtypeStruct((rows, cols), F32) for _ in range(4)), grid=(rows // tr,),
        in_specs=[spec] * len(ops), out_specs=(spec,) * 4, compiler_params=_cp(("parallel",)), name=name)(*ops)


def _sum4(parts, name):
    _, rows, cols = parts.shape
    tr = _pick(rows, (256, 128, 64, 32, 16))

    def body(p_ref, o_ref):
        acc = p_ref[0].astype(F32)
        for kk in range(1, 4):
            acc = acc + p_ref[kk].astype(F32)
        o_ref[...] = acc

    return pl.pallas_call(
        body, out_shape=jax.ShapeDtypeStruct((rows, cols), F32), grid=(rows // tr,),
        in_specs=[pl.BlockSpec((4, tr, cols), lambda i: (0, i, 0))], out_specs=pl.BlockSpec((tr, cols), lambda i: (i, 0)),
        compiler_params=_cp(("parallel",)), name=name)(parts)


def _place():
    x, y, c = lax.axis_index("x"), lax.axis_index("y"), lax.axis_index("c")
    chips = [(1 - x, y), (x, 1 - y), (1 - x, 1 - y)]
    return x, y, c, chips


def _window(ref, axis, j, size, c=None, half=None, lead=()):
    if axis == "r":
        if c is None:
            return ref.at[lead + (slice(None), pl.ds(pl.multiple_of(j * size, 32), size), slice(None))]
        return ref.at[lead + (slice(None), pl.ds(pl.multiple_of(j * size + c * half, 32), half), slice(None))]
    cols = pl.ds(pl.multiple_of(j * size, LANES), size)
    if c is None:
        return ref.at[lead + (slice(None), slice(None), cols)]
    return ref.at[lead + (slice(None), pl.ds(pl.multiple_of(c * half, 32), half), cols)]


class _Gather:
    def __init__(self, shards, axes):
        self.nt, self.axes = len(shards), axes
        self.out_shape, self.sizes, self.halves = [], [], []
        for s, ax in zip(shards, axes):
            l, rs, cs = s.shape
            self.out_shape.append(jax.ShapeDtypeStruct((l, 4 * rs, cs) if ax == "r" else (l, rs, 4 * cs), s.dtype))
            self.sizes.append(rs if ax == "r" else cs)
            self.halves.append(rs // 2)
        self.scratch = [pltpu.SemaphoreType.DMA((6 * self.nt,)), pltpu.SemaphoreType.DMA((6 * self.nt,)),
                        pltpu.SemaphoreType.DMA((self.nt,))]

    def bind(self, src, dst, send_sems, recv_sems, local_sems):
        self.src, self.dst, self.send_sems, self.recv_sems, self.local_sems = src, dst, send_sems, recv_sems, local_sems

    def _win(self, t, j, core=None):
        return _window(self.dst[t], self.axes[t], j, self.sizes[t], core, self.halves[t])

    def _ici(self, t, kk, origin):
        _, _, c, chips = _place()
        px, py = chips[kk]
        half = self.src[t].at[:, pl.ds(pl.multiple_of(c * self.halves[t], 16), self.halves[t]), :]
        return pltpu.make_async_remote_copy(
            src_ref=half, dst_ref=self._win(t, origin, c), send_sem=self.send_sems.at[t * 3 + kk],
            recv_sem=self.recv_sems.at[t * 3 + kk], device_id=(px, py, c), device_id_type=MESH)

    def _d2d(self, t, kk, origin, core):
        x, y, c, _ = _place()
        k2 = 3 * self.nt + t * 3 + kk
        return pltpu.make_async_remote_copy(
            src_ref=self._win(t, origin, core), dst_ref=self._win(t, origin, core), send_sem=self.send_sems.at[k2],
            recv_sem=self.recv_sems.at[k2], device_id=(x, y, 1 - c), device_id_type=MESH)

    def _local(self, t):
        x, y, _, _ = _place()
        return pltpu.make_async_copy(self.src[t], self._win(t, 2 * x + y), self.local_sems.at[t])

    def _each(self):
        _, _, _, chips = _place()
        for t in range(self.nt):
            for kk in range(3):
                px, py = chips[kk]
                yield t, kk, 2 * px + py

    def start(self):
        x, y, _, _ = _place()
        for t in range(self.nt):
            self._local(t).start()
        for t, kk, _ in self._each():
            self._ici(t, kk, 2 * x + y).start()

    def forward(self):
        _, _, c, _ = _place()
        for t, kk, origin in self._each():
            self._ici(t, kk, origin).wait_recv()
            self._d2d(t, kk, origin, c).start()

    def finish(self):
        x, y, c, _ = _place()
        for t, kk, origin in self._each():
            self._d2d(t, kk, origin, 1 - c).wait_recv()
        for t, kk, origin in self._each():
            self._ici(t, kk, 2 * x + y).wait_send()
            self._d2d(t, kk, origin, c).wait_send()
        for t in range(self.nt):
            self._local(t).wait()


class _Scatter:
    def __init__(self, grads, axes):
        self.nt, self.axes = len(grads), axes
        self.out_shape, self.sizes = [], []
        for gr, ax in zip(grads, axes):
            l, r, cc = gr.shape
            self.out_shape.append(jax.ShapeDtypeStruct((4, l, r // 4, cc) if ax == "r" else (4, l, r, cc // 4), gr.dtype))
            self.sizes.append(r // 4 if ax == "r" else cc // 4)
        self.scratch = [pltpu.SemaphoreType.DMA((3 * self.nt,)), pltpu.SemaphoreType.DMA((3 * self.nt,)),
                        pltpu.SemaphoreType.DMA((self.nt,))]

    def bind(self, src, dst, send_sems, recv_sems, local_sems):
        self.src, self.dst, self.send_sems, self.recv_sems, self.local_sems = src, dst, send_sems, recv_sems, local_sems

    def _copy(self, t, kk, slot):
        x, y, c, chips = _place()
        px, py = chips[kk]
        return pltpu.make_async_remote_copy(
            src_ref=_window(self.src[t], self.axes[t], 2 * px + py, self.sizes[t]), dst_ref=self.dst[t].at[slot],
            send_sem=self.send_sems.at[t * 3 + kk], recv_sem=self.recv_sems.at[t * 3 + kk],
            device_id=(px, py, c), device_id_type=MESH)

    def _local(self, t):
        x, y, _, _ = _place()
        me = 2 * x + y
        return pltpu.make_async_copy(_window(self.src[t], self.axes[t], me, self.sizes[t]), self.dst[t].at[me],
                                     self.local_sems.at[t])

    def start(self):
        x, y, _, _ = _place()
        for t in range(self.nt):
            self._local(t).start()
            for kk in range(3):
                self._copy(t, kk, 2 * x + y).start()

    def finish(self):
        _, _, _, chips = _place()
        for t in range(self.nt):
            for kk in range(3):
                px, py = chips[kk]
                self._copy(t, kk, 2 * px + py).wait_recv()
        for t in range(self.nt):
            for kk in range(3):
                px, py = chips[kk]
                self._copy(t, kk, 2 * px + py).wait_send()
            self._local(t).wait()


def _comm_call(plan, operands, name):
    nt = plan.nt

    def body(*refs):
        plan.bind(refs[:nt], refs[nt:2 * nt], *refs[2 * nt:])
        plan.start()
        if hasattr(plan, "forward"):
            plan.forward()
        plan.finish()

    return pl.pallas_call(body, out_shape=tuple(plan.out_shape), in_specs=[ANY] * nt, out_specs=tuple([ANY] * nt),
                          scratch_shapes=plan.scratch, name=name)(*operands)


def _swap_sibling(arrs, name):
    nt = len(arrs)

    def body(*refs):
        src, dst = refs[:nt], refs[nt:2 * nt]
        send_sems, recv_sems = refs[2 * nt:]
        x, y, c, _ = _place()
        cps = [pltpu.make_async_remote_copy(src_ref=src[t], dst_ref=dst[t], send_sem=send_sems.at[t],
                                            recv_sem=recv_sems.at[t], device_id=(x, y, 1 - c), device_id_type=MESH)
               for t in range(nt)]
        for cp in cps:
            cp.start()
        for cp in cps:
            cp.wait()

    return pl.pallas_call(
        body, out_shape=tuple(jax.ShapeDtypeStruct(a.shape, a.dtype) for a in arrs), in_specs=[ANY] * nt,
        out_specs=tuple([ANY] * nt),
        scratch_shapes=[pltpu.SemaphoreType.DMA((nt,)), pltpu.SemaphoreType.DMA((nt,))], name=name)(*arrs)


def _gather8(pack, with_sum, name):
    rows = pack.shape[0]
    flips = [f for f in itertools.product((0, 1), repeat=3) if any(f)]

    def body(p_ref, all_ref, *rest):
        if with_sum:
            sum_ref, send_sems, recv_sems = rest
        else:
            send_sems, recv_sems = rest
        x, y, c, _ = _place()
        me = 4 * x + 2 * y + c

        def peer(f):
            return tuple(1 - v if fl else v for v, fl in zip((x, y, c), f))

        all_ref[me] = p_ref[...]
        cps = []
        for kk, f in enumerate(flips):
            cp = pltpu.make_async_remote_copy(src_ref=p_ref, dst_ref=all_ref.at[me], send_sem=send_sems.at[kk],
                                              recv_sem=recv_sems.at[kk], device_id=peer(f), device_id_type=MESH)
            cp.start()
            cps.append(cp)
        for kk, f in enumerate(flips):
            px, py, pc = peer(f)
            pltpu.make_async_remote_copy(src_ref=p_ref, dst_ref=all_ref.at[4 * px + 2 * py + pc],
                                         send_sem=send_sems.at[kk], recv_sem=recv_sems.at[kk], device_id=peer(f),
                                         device_id_type=MESH).wait_recv()
        for cp in cps:
            cp.wait_send()
        if with_sum:
            acc = all_ref[0]
            for d in range(1, 8):
                acc = acc + all_ref[d]
            sum_ref[...] = acc

    vm = pl.BlockSpec(memory_space=pltpu.VMEM)
    out_shape = [jax.ShapeDtypeStruct((8, rows, LANES), F32)] + ([jax.ShapeDtypeStruct((rows, LANES), F32)] if with_sum else [])
    return pl.pallas_call(
        body, out_shape=tuple(out_shape), in_specs=[vm], out_specs=tuple([vm] * len(out_shape)),
        scratch_shapes=[pltpu.SemaphoreType.DMA((7,)), pltpu.SemaphoreType.DMA((7,))], name=name)(pack)


def _pack(arrs):
    flat = jnp.concatenate([a.reshape(-1).astype(F32) for a in arrs])
    rows = -(-flat.shape[0] // (8 * LANES)) * 8
    return jnp.pad(flat, (0, rows * LANES - flat.shape[0])).reshape(rows, LANES)


def _unpack(flat, shapes):
    out, pos = [], 0
    for shp in shapes:
        size = 1
        for s in shp:
            size *= s
        out.append(flat[pos:pos + size].reshape(shp))
        pos += size
    return out


BIG = ("attn_w_qkv", "attn_w_o", "pool_w", "xattn_w_q", "xattn_w_kv", "xattn_w_o", "ffn_w_up", "ffn_w_down")
BIG_AXIS = ("c", "r", "r", "r", "c", "r", "c", "r")
SMALL_REPL = ("attn_norm", "attn_q_gain", "attn_k_gain", "xattn_norm", "mem_norm", "ffn_norm", "ffn_conv_b", "final_norm")
SMALL_SHARD = ("pool_norm", "pool_scale", "ffn_conv_w")
ORDER = ("attn_norm", "attn_w_qkv", "attn_q_gain", "attn_k_gain", "attn_w_o", "pool_norm", "pool_w", "pool_scale",
         "xattn_norm", "mem_norm", "xattn_w_q", "xattn_w_kv", "xattn_w_o", "ffn_norm", "ffn_w_up", "ffn_conv_w",
         "ffn_conv_b", "ffn_w_down", "final_norm")


def _step(x, mem, tgt, w, m, v):
    seq, d = x.shape
    xi, yi, ci = lax.axis_index("x"), lax.axis_index("y"), lax.axis_index("c")
    chip = 2 * xi + yi
    dff = w["ffn_w_down"].shape[1] * 4
    n_layers = w["ffn_norm"].shape[0]

    def as3d(a):
        return a.reshape(a.shape[-3:])
    shards = [as3d(w[nm]).astype(BF16) for nm in BIG]
    (wq,) = _comm_call(_Gather(shards[:1], BIG_AXIS[:1]), shards[:1], "gather_qkv")
    small_in = [w[nm] for nm in SMALL_SHARD]
    (small_all,) = _gather8(_pack(small_in), False, "gather_small")
    per_chip = [_unpack(small_all[2 * j].reshape(-1), [a.shape for a in small_in]) for j in range(4)]
    pool_norm, pool_scale, conv_w = (jnp.concatenate([per_chip[j][i] for j in range(4)], axis=-1) for i in range(3))

    conv_b = w["ffn_conv_b"].reshape(n_layers, 1, -1)
    tabs = _rope_tables(seq)
    qg2 = jnp.tile(w["attn_q_gain"], (1, 2))
    kg2 = jnp.tile(w["attn_k_gain"], (1, 2))
    mm = functools.partial(_mm)

    saved = {}
    x0 = x
    h0 = _rms_fwd(x0, w["attn_norm"], BF16, "rms_attn")
    qkv = mm(h0, wq, "nn", b_l=0, out_dtype=F32, name="mm_qkv")
    q_r, k_r, k_t, v_b, v_t = _qk_prep(qkv, qg2, kg2, tabs, "qk_prep")
    o_at, lse, wo, wp, wxq, wxkv, wxo, wup, wdn = _flash_fwd(
        q_r, k_r, v_t, _Gather(shards[1:], BIG_AXIS[1:]), shards[1:], "flash_fwd")
    x1, hq0 = mm(o_at, wo, "nn", b_l=0, res=x0, out_dtype=F32, norm_out=(w["xattn_norm"][0:1], BF16), name="mm_attn_o")

    def xattn_fwd(l, xin, hq):
        mn = _rms_fwd(mem, w["mem_norm"][l:l + 1], BF16, f"rms_mem{l}")
        xq = mm(hq, wxq, "nn", b_l=l, scale=X_HEAD_DIM ** -0.5, out_dtype=BF16, name=f"mm_xq{l}")
        kv = mm(mn, wxkv, "nn", b_l=l, out_dtype=BF16, name=f"mm_xkv{l}")
        xo = _xattn_fwd(xq, kv, f"xattn_fwd{l}")
        saved[f"x{l}"] = (hq, mn, xq, kv, xo)
        return mm(xo, wxo, "nn", b_l=l, res=xin, out_dtype=F32, norm_out=(w["ffn_norm"][l:l + 1], BF16), name=f"mm_xo{l}")

    def ffn_fwd(l, xin, hf, norm_out):
        ug = mm(hf, wup, "nn", b_l=l, n=dff, out_dtype=BF16, name=f"mm_up_g{l}")
        uv = mm(hf, wup, "nn", b_l=l, n=dff, b_off=(0, dff), out_dtype=BF16, name=f"mm_up_v{l}")
        act, *out = _gate_down(ug, uv, conv_w, conv_b, wdn, l, xin, norm_out, f"gate_down{l}")
        saved[f"f{l}"] = (hf, ug, uv, act)
        return out if norm_out is not None else out[0]

    x2, hf0 = xattn_fwd(0, x1, hq0)
    x3, hp = ffn_fwd(0, x2, hf0, (pool_norm, F32))
    x4 = _pool_fwd(hp, x3, wp, pool_scale, "pool_fwd")
    x5, hf1 = xattn_fwd(1, x4, _rms_fwd(x4, w["xattn_norm"][1:2], BF16, "rms_xq1"))
    xs = [x0, x1, x2, x3, x4, x5, ffn_fwd(1, x5, hf1, None)]
    dres, g_final, loss = _final_loss(xs[6], w["final_norm"].reshape(1, d), tgt, "final_loss")

    grads = {}
    gbuf = {}

    def dw(nm, a, b, layer, full, off=(0, 0), n=None, tn=None):
        gbuf[nm] = _mm(a, b, "tn", out_dtype=BF16, out_full=full, out_l=layer, out_off=off, n=n, tn=tn,
                       alias=gbuf.get(nm), name=f"dw_{nm}{layer}_{off[1]}")

    def ffn_bwd(l, xin, dres):
        hf, ug, uv, act = saved[f"f{l}"]
        dw("ffn_w_down", act, dres, l, wdn.shape)
        dact = _mm(dres, wdn, "nt", b_l=l, out_dtype=BF16, name=f"mm_dact{l}")
        dug, duv, dwg, dwv, dres, dg = _gate_up_bwd(ug, uv, dact, conv_w, conv_b, wup, l, xin, w["ffn_norm"][l:l + 1],
                                                    dres, f"gate_up_bwd{l}")
        dw("ffn_w_up", hf, dug, l, wup.shape, tn=1408)
        dw("ffn_w_up", hf, duv, l, wup.shape, off=(0, dff), tn=1408)
        taps = jnp.concatenate([dwg, dwv], axis=0).transpose(1, 0, 2).reshape(8, 2 * dff)
        return dres, dg, taps[:3], taps[3]

    def xattn_bwd(l, xin, dres):
        hq, mn, xq, kv, xo = saved[f"x{l}"]
        dw("xattn_w_o", xo, dres, l, wxo.shape)
        dxo = _mm(dres, wxo, "nt", b_l=l, out_dtype=BF16, name=f"mm_dxo{l}")
        dq, dkv = _xattn_bwd(xq, kv, dxo, f"xattn_bwd{l}")
        dw("xattn_w_q", hq, dq, l, wxq.shape)
        dw("xattn_w_kv", mn, dkv, l, wxkv.shape)
        dmn = _mm(dkv, wxkv, "nt", b_l=l, out_dtype=F32, name=f"mm_dmn{l}")
        _, dg_mem = _rms_bwd(mem, w["mem_norm"][l:l + 1], dmn, None, f"rms_mem_bwd{l}")
        dres, dg = _mm(dq, wxq, "nt", b_l=l, out_dtype=F32, norm_bwd=(xin, w["xattn_norm"][l:l + 1], dres),
                       name=f"mm_dhq{l}")
        return dres, dg, dg_mem

    g_ffn, g_xn, g_mn, g_cw, g_cb = [None] * n_layers, [None] * n_layers, [None] * n_layers, [None] * n_layers, [None] * n_layers
    dres, g_ffn[1], g_cw[1], g_cb[1] = ffn_bwd(1, xs[5], dres)
    dres, g_xn[1], g_mn[1] = xattn_bwd(1, xs[4], dres)
    dhp, g_pw, g_pscale = _pool_bwd(hp, dres, wp, pool_scale, "pool_bwd")
    dres, g_pnorm = _rms_bwd(xs[3], pool_norm, dhp, dres, "rms_pool_bwd")
    dres, g_ffn[0], g_cw[0], g_cb[0] = ffn_bwd(0, xs[2], dres)
    dres, g_xn[0], g_mn[0] = xattn_bwd(0, xs[1], dres)
    dw("attn_w_o", o_at, dres, 0, wo.shape)
    do = _mm(dres, wo, "nt", b_l=0, out_dtype=BF16, name="mm_do")
    gbuf["pool_w"] = g_pw.astype(BF16)
    early = [gbuf[nm] for nm in BIG[1:]]
    dq_r, dk_r, dv, *recv_early = _flash_bwd(q_r, k_r, k_t, v_b, do, o_at, lse, _Scatter(early, BIG_AXIS[1:]), early,
                                             "flash_bwd")
    dqkv, dqg, dkg = _qk_prep_bwd(qkv, dq_r, dk_r, dv, qg2, kg2, tabs, "qk_prep_bwd")
    dw("attn_w_qkv", h0, dqkv, 0, wq.shape)
    grad_x, g_an = _mm(dqkv, wq, "nt", b_l=0, out_dtype=F32, norm_bwd=(x0, w["attn_norm"], dres), name="mm_dh0")

    small_g = {
        "attn_norm": g_an, "attn_q_gain": dqg[:, :HEAD_DIM] + dqg[:, HEAD_DIM:], "attn_k_gain": dkg[:, :HEAD_DIM] + dkg[:, HEAD_DIM:],
        "xattn_norm": jnp.concatenate(g_xn, axis=0), "mem_norm": jnp.concatenate(g_mn, axis=0),
        "ffn_norm": jnp.concatenate(g_ffn, axis=0), "ffn_conv_b": jnp.stack(g_cb, axis=0), "final_norm": g_final.reshape(d),
        "pool_norm": g_pnorm, "pool_scale": g_pscale, "ffn_conv_w": jnp.stack(g_cw, axis=0)}
    names = SMALL_REPL + SMALL_SHARD
    _, total = _gather8(_pack([loss[0, :1]] + [small_g[nm] for nm in names]), True, "reduce_small")
    parts = _unpack(total.reshape(-1), [(1,)] + [small_g[nm].shape for nm in names])
    loss_out = parts[0][0]
    for nm, g in zip(names, parts[1:]):
        if nm in SMALL_SHARD:
            size = w[nm].shape[-1]
            g = lax.dynamic_slice_in_dim(g, chip * size, size, axis=g.ndim - 1)
        grads[nm] = g.reshape(w[nm].shape)

    packed = [_pack([src[nm] for nm in names]) for src in (w, grads, m, v)]
    _, sd, sm, sv = _adamw(packed[0], packed[1], None, packed[2], packed[3], "adamw_small")
    shapes = [w[nm].shape for nm in names]
    delta = dict(zip(names, _unpack(sd.reshape(-1), shapes)))
    new_m = dict(zip(names, _unpack(sm.reshape(-1), shapes)))
    new_v = dict(zip(names, _unpack(sv.reshape(-1), shapes)))

    late = [gbuf[nm] for nm in BIG[:1]]
    recv = list(_comm_call(_Scatter(late, BIG_AXIS[:1]), late, "scatter_qkv")) + recv_early
    sums = []
    for nm, rc in zip(BIG, recv):
        sums.append(_sum4(rc.reshape(4, -1, rc.shape[-1]), f"sum4_{nm}"))
    others = _swap_sibling(sums, "swap_sums")
    for nm, mine, other in zip(BIG, sums, others):
        cols = mine.shape[-1]
        outs = _adamw(w[nm].reshape(-1, cols), mine, other, m[nm].reshape(-1, cols), v[nm].reshape(-1, cols), f"adamw_{nm}")
        grads[nm], delta[nm], new_m[nm], new_v[nm] = (o.reshape(w[nm].shape) for o in outs)

    return loss_out, grad_x, grads, delta, new_m, new_v


def kernel(x, mem, attn_norm, attn_w_qkv, attn_q_gain, attn_k_gain, attn_w_o, pool_norm, pool_w, pool_scale, xattn_norm, mem_norm, xattn_w_q, xattn_w_kv, xattn_w_o, ffn_norm, ffn_w_up, ffn_conv_w, ffn_conv_b, ffn_w_down, final_norm, loss_target, m_attn_norm, m_attn_w_qkv, m_attn_q_gain, m_attn_k_gain, m_attn_w_o, m_pool_norm, m_pool_w, m_pool_scale, m_xattn_norm, m_mem_norm, m_xattn_w_q, m_xattn_w_kv, m_xattn_w_o, m_ffn_norm, m_ffn_w_up, m_ffn_conv_w, m_ffn_conv_b, m_ffn_w_down, m_final_norm, v_attn_norm, v_attn_w_qkv, v_attn_q_gain, v_attn_k_gain, v_attn_w_o, v_pool_norm, v_pool_w, v_pool_scale, v_xattn_norm, v_mem_norm, v_xattn_w_q, v_xattn_w_kv, v_xattn_w_o, v_ffn_norm, v_ffn_w_up, v_ffn_conv_w, v_ffn_conv_b, v_ffn_w_down, v_final_norm):
    given = dict(locals())
    w = {nm: given[nm] for nm in ORDER}
    m = {nm: given["m_" + nm] for nm in ORDER}
    v = {nm: given["v_" + nm] for nm in ORDER}
    seq, d = x.shape[1], x.shape[2]
    loss, grad_x, grads, delta, new_m, new_v = _step(
        x.reshape(seq, d), mem.reshape(mem.shape[1], d), loss_target.reshape(seq, d), w, m, v)
    return (loss, grad_x.reshape(x.shape), *[grads[nm] for nm in ORDER], *[delta[nm] for nm in ORDER],
            *[new_m[nm] for nm in ORDER], *[new_v[nm] for nm in ORDER])
```

```python
import functools
import itertools

import jax
import jax.numpy as jnp
from jax import lax
from jax.experimental import pallas as pl
from jax.experimental.pallas import tpu as pltpu

F32, BF16 = jnp.float32, jnp.bfloat16
EPS = 1e-6
GRID_W = 64
ROPE_THETA = 10000.0
HEAD_DIM = 64
N_HEADS = 16
N_KV = 4
X_HEADS = 4
X_HEAD_DIM = 256
POOL_GROUPS = 4
POOL_GROUP_W = 256
HALO = 16
LANES = 128
ADAM_LR, ADAM_B1, ADAM_B2, ADAM_EPS, ADAM_WD, ADAM_STEP = 0.001, 0.9, 0.999, 1e-08, 0.01, 10
VMEM_LIMIT = 48 * 1024 * 1024
MESH = pl.DeviceIdType.MESH
NEG = -1e30
LOG2E = 1.4426950408889634
FLASH_TQ, FLASH_TK = 512, 4096
FLASH_SUB = 512
ANY = pl.BlockSpec(memory_space=pl.ANY)


def _cp(sem=None):
    return pltpu.CompilerParams(dimension_semantics=sem, vmem_limit_bytes=VMEM_LIMIT)


def _pick(n, cands):
    for c in cands:
        if c <= n and n % c == 0:
            return c
    return n


def _mm(a, b, mode, *, name, out_dtype, tm=None, tn=None, tk=None, n=None, k=None, b_l=None, b_off=(0, 0),
        res=None, scale=None, out_full=None, out_l=None, out_off=(0, 0), alias=None, norm_out=None, norm_bwd=None):
    if mode == "tn":
        K, M = a.shape
    else:
        M, K = a.shape
    bs = b.shape[-2:]
    if mode == "nn":
        K = k or K
        N = n or bs[1]
    elif mode == "nt":
        N = n or bs[0]
    else:
        N = n or bs[1]
    wide = (1408, 1024, 512, 256, 128)
    if mode == "tn":
        tm = tm or (M if M <= 1024 else _pick(M, wide))
        tk = tk or _pick(K, (2048, 1024, 512, 256, 128))
    else:
        tm = _pick(M, (tm or 512, 256, 128))
        tk = tk or (K if K <= 2816 else _pick(K, wide))
    tn = tn or (N if N <= 1536 else _pick(N, wide))
    assert M % tm == 0 and N % tn == 0 and K % tk == 0, (name, M, N, K, tm, tn, tk)
    nk = K // tk
    dims = {"nn": ((1,), (0,)), "nt": ((1,), (1,)), "tn": ((0,), (0,))}[mode]

    j_outer = nk == 1 and mode != "tn"

    def at(f):
        return (lambda j, i, kk: f(i, j, kk)) if j_outer else f

    if mode == "tn":
        a_spec = pl.BlockSpec((tk, tm), at(lambda i, j, kk: (kk, i)))
    else:
        a_spec = pl.BlockSpec((tm, tk), at(lambda i, j, kk: (i, kk)))
    if mode == "nt":
        bb, (d0, d1) = (tn, tk), (b_off[0] // tn, b_off[1] // tk)
        assert b_off[0] % tn == 0 and b_off[1] % tk == 0
        bidx = lambda i, j, kk: (j + d0, kk + d1)
    else:
        bb, (d0, d1) = (tk, tn), (b_off[0] // tk, b_off[1] // tn)
        assert b_off[0] % tk == 0 and b_off[1] % tn == 0
        bidx = lambda i, j, kk: (kk + d0, j + d1)
    if b.ndim == 3:
        b_spec = pl.BlockSpec((None,) + bb, at(lambda i, j, kk: (b_l,) + bidx(i, j, kk)))
    else:
        b_spec = pl.BlockSpec(bb, at(bidx))
    in_specs, operands = [a_spec, b_spec], [a, b]
    if res is not None:
        in_specs.append(pl.BlockSpec((tm, tn), at(lambda i, j, kk: (i, j))))
        operands.append(res)
    aliases = {}
    if alias is not None:
        aliases = {len(operands): 0}
        in_specs.append(ANY)
        operands.append(alias)
    if out_full is None:
        out_shape = jax.ShapeDtypeStruct((M, N), out_dtype)
        out_spec = pl.BlockSpec((tm, tn), at(lambda i, j, kk: (i, j)))
    else:
        assert out_off[0] % tm == 0 and out_off[1] % tn == 0
        o0, o1 = out_off[0] // tm, out_off[1] // tn
        out_shape = jax.ShapeDtypeStruct(out_full, out_dtype)
        out_spec = pl.BlockSpec((None, tm, tn), at(lambda i, j, kk: (out_l, i + o0, j + o1)))
    has_res, has_alias = res is not None, alias is not None
    grid = (N // tn, M // tm, nk) if j_outer else (M // tm, N // tn, nk)
    n_extra = 0
    if norm_out is not None or norm_bwd is not None:
        assert j_outer and tn == N and out_full is None, name
        row = pl.BlockSpec((tm, tn), at(lambda i, j, kk: (i, 0)))
        vec = pl.BlockSpec((1, tn), at(lambda i, j, kk: (0, 0)))
        if norm_out is not None:
            in_specs.append(vec)
            operands.append(norm_out[0])
            n_extra = 1
            out_shape = (out_shape, jax.ShapeDtypeStruct((M, N), norm_out[1]))
            out_spec = (out_spec, row)
        else:
            in_specs += [row, vec, row]
            operands += list(norm_bwd)
            n_extra = 3
            out_shape = (out_shape, jax.ShapeDtypeStruct((1, N), F32))
            out_spec = (out_spec, vec)
    n_out = 1 if n_extra == 0 else 2

    def body(*refs):
        a_ref, b_ref = refs[0], refs[1]
        pos = 2
        res_ref = None
        if has_res:
            res_ref = refs[pos]
            pos += 1
        if has_alias:
            pos += 1
        extra = refs[pos:pos + n_extra]
        pos += n_extra
        o_ref, acc_ref = refs[pos], refs[pos + n_out]
        kk = pl.program_id(2)
        part = lax.dot_general(a_ref[...].astype(BF16), b_ref[...].astype(BF16), (dims, ((), ())),
                               preferred_element_type=F32)

        def finish(acc):
            if scale is not None:
                acc = acc * scale
            if res_ref is not None:
                acc = acc + res_ref[...]
            if norm_out is not None:
                r = lax.rsqrt(jnp.mean(acc * acc, axis=-1, keepdims=True) + EPS)
                refs[pos + 1][...] = (acc * r * extra[0][...]).astype(refs[pos + 1].dtype)
            if norm_bwd is not None:
                x_ref, g_ref, dres_ref = extra
                dg_ref, step = refs[pos + 1], pl.program_id(1)
                xv = x_ref[...]
                r = lax.rsqrt(jnp.mean(xv * xv, axis=-1, keepdims=True) + EPS)
                nv = xv * r
                dgp = jnp.sum(acc * nv, axis=0, keepdims=True)

                @pl.when(step == 0)
                def _():
                    dg_ref[...] = dgp

                @pl.when(step > 0)
                def _():
                    dg_ref[...] += dgp

                dn = acc * g_ref[...]
                acc = dres_ref[...] + r * (dn - nv * jnp.mean(dn * nv, axis=-1, keepdims=True))
            o_ref[...] = acc.astype(o_ref.dtype)

        if nk == 1:
            finish(part)
        else:
            @pl.when(kk == 0)
            def _():
                acc_ref[...] = part

            @pl.when(jnp.logical_and(kk > 0, kk < nk - 1))
            def _():
                acc_ref[...] += part

            @pl.when(kk == nk - 1)
            def _():
                finish(acc_ref[...] + part)

    return pl.pallas_call(
        body, out_shape=out_shape, grid=grid, in_specs=in_specs, out_specs=out_spec,
        scratch_shapes=[pltpu.VMEM((tm, tn) if nk > 1 else (8, 128), F32)], input_output_aliases=aliases,
        compiler_params=_cp(("arbitrary",) * 3 if norm_bwd is not None else ("parallel", "parallel", "arbitrary")),
        name=name)(*operands)


def _rms_fwd(x, gain, out_dtype, name):
    rows, d = x.shape
    tr = _pick(rows, (512, 256))

    def body(x_ref, g_ref, o_ref):
        xv = x_ref[...]
        r = lax.rsqrt(jnp.mean(xv * xv, axis=-1, keepdims=True) + EPS)
        o_ref[...] = (xv * r * g_ref[...]).astype(o_ref.dtype)

    return pl.pallas_call(
        body, out_shape=jax.ShapeDtypeStruct((rows, d), out_dtype), grid=(rows // tr,),
        in_specs=[pl.BlockSpec((tr, d), lambda i: (i, 0)), pl.BlockSpec((1, d), lambda i: (0, 0))],
        out_specs=pl.BlockSpec((tr, d), lambda i: (i, 0)), compiler_params=_cp(("parallel",)), name=name)(x, gain)


def _rms_bwd(x, gain, dh, dres, name):
    rows, d = x.shape
    tr = _pick(rows, (512, 256))
    need_dx = dres is not None

    def body(*refs):
        if need_dx:
            x_ref, g_ref, dh_ref, dres_ref, o_ref, dg_ref = refs
        else:
            x_ref, g_ref, dh_ref, dg_ref = refs
        i = pl.program_id(0)
        xv = x_ref[...]
        dhv = dh_ref[...].astype(F32)
        r = lax.rsqrt(jnp.mean(xv * xv, axis=-1, keepdims=True) + EPS)
        nv = xv * r
        part = jnp.sum(dhv * nv, axis=0, keepdims=True)

        @pl.when(i == 0)
        def _():
            dg_ref[...] = part

        @pl.when(i > 0)
        def _():
            dg_ref[...] += part

        if need_dx:
            dn = dhv * g_ref[...]
            dx = r * (dn - nv * jnp.mean(dn * nv, axis=-1, keepdims=True))
            o_ref[...] = dres_ref[...] + dx

    row_spec = pl.BlockSpec((tr, d), lambda i: (i, 0))
    vec_spec = pl.BlockSpec((1, d), lambda i: (0, 0))
    if need_dx:
        return pl.pallas_call(
            body, out_shape=(jax.ShapeDtypeStruct((rows, d), F32), jax.ShapeDtypeStruct((1, d), F32)),
            grid=(rows // tr,), in_specs=[row_spec, vec_spec, row_spec, row_spec], out_specs=(row_spec, vec_spec),
            compiler_params=_cp(("arbitrary",)), name=name)(x, gain, dh, dres)
    return None, pl.pallas_call(
        body, out_shape=jax.ShapeDtypeStruct((1, d), F32), grid=(rows // tr,),
        in_specs=[row_spec, vec_spec, row_spec], out_specs=vec_spec,
        compiler_params=_cp(("arbitrary",)), name=name)(x, gain, dh)


def _final_loss(x, gain, target, name):
    rows, d = x.shape
    tr = _pick(rows, (512, 256))
    nsteps = rows // tr

    def body(x_ref, g_ref, t_ref, dx_ref, dg_ref, loss_ref, acc_ref):
        i = pl.program_id(0)
        xv = x_ref[...]
        g = g_ref[...]
        r = lax.rsqrt(jnp.mean(xv * xv, axis=-1, keepdims=True) + EPS)
        nv = xv * r
        err = nv * g - t_ref[...]
        dy = err * (1.0 / d)
        dn = dy * g
        dx_ref[...] = r * (dn - nv * jnp.mean(dn * nv, axis=-1, keepdims=True))
        dgp = jnp.sum(dy * nv, axis=0, keepdims=True)
        lp = jnp.sum(err * err, axis=0, keepdims=True)

        @pl.when(i == 0)
        def _():
            dg_ref[...] = dgp
            acc_ref[...] = lp

        @pl.when(i > 0)
        def _():
            dg_ref[...] += dgp
            acc_ref[...] += lp

        @pl.when(i == nsteps - 1)
        def _():
            tot = jnp.sum(acc_ref[...], axis=1, keepdims=True) * (0.5 / d)
            loss_ref[...] = jnp.broadcast_to(tot, loss_ref.shape)

    row_spec = pl.BlockSpec((tr, d), lambda i: (i, 0))
    vec_spec = pl.BlockSpec((1, d), lambda i: (0, 0))
    return pl.pallas_call(
        body, out_shape=(jax.ShapeDtypeStruct((rows, d), F32), jax.ShapeDtypeStruct((1, d), F32),
                         jax.ShapeDtypeStruct((1, LANES), F32)),
        grid=(nsteps,), in_specs=[row_spec, vec_spec, row_spec],
        out_specs=(row_spec, vec_spec, pl.BlockSpec((1, LANES), lambda i: (0, 0))),
        scratch_shapes=[pltpu.VMEM((1, d), F32)], compiler_params=_cp(("arbitrary",)), name=name)(x, gain, target)


def _rope_tables(seq):
    pairs = HEAD_DIM // 4
    lane = jnp.arange(LANES, dtype=jnp.int32) % HEAD_DIM
    by_col, second, pair = lane // (2 * pairs) == 1, (lane % (2 * pairs)) // pairs == 1, lane % pairs
    inv_freq = ROPE_THETA ** (-pair.astype(F32) / pairs)
    t = jnp.arange(seq, dtype=jnp.int32)[:, None]
    pos = jnp.where(by_col[None, :], t % GRID_W, t // GRID_W).astype(F32)
    ang = pos * inv_freq[None, :]
    cos, sin = jnp.cos(ang), jnp.sin(ang)
    return cos, jnp.where(second[None, :], sin, 0.0), jnp.where(second[None, :], 0.0, -sin)


def _pair_norm(xv, lo):
    sq = xv * xv
    s_lo = jnp.sum(jnp.where(lo, sq, 0.0), axis=1, keepdims=True)
    s_hi = jnp.sum(jnp.where(lo, 0.0, sq), axis=1, keepdims=True)
    return lax.rsqrt(jnp.where(lo, s_lo, s_hi) * (1.0 / HEAD_DIM) + EPS)


def _rope(y, c, sp, sm):
    return y * c + pltpu.roll(y, 16, axis=1) * sp + pltpu.roll(y, LANES - 16, axis=1) * sm


def _rope_t(dz, c, sp, sm):
    return dz * c + pltpu.roll(dz * sp, LANES - 16, axis=1) + pltpu.roll(dz * sm, 16, axis=1)


def _qk_prep(qkv, qg2, kg2, tabs, name):
    seq = qkv.shape[0]
    ts = _pick(seq, (256, 128))
    nq, nkp = N_HEADS // 2, N_KV // 2
    qw, kw = N_HEADS * HEAD_DIM, N_KV * HEAD_DIM

    def body(x_ref, qg_ref, kg_ref, c_ref, sp_ref, sm_ref, q_ref, k_ref, kt_ref, v_ref, vt_ref):
        lo = lax.broadcasted_iota(jnp.int32, (ts, LANES), 1) < HEAD_DIM
        top = lax.broadcasted_iota(jnp.int32, (LANES, ts), 0) < HEAD_DIM
        c, sp, sm = c_ref[...], sp_ref[...], sm_ref[...]
        for i in range(nq):
            xv = x_ref[:, i * LANES:(i + 1) * LANES]
            y = xv * _pair_norm(xv, lo) * qg_ref[...]
            q_ref[:, i * LANES:(i + 1) * LANES] = (_rope(y, c, sp, sm) * (LOG2E * HEAD_DIM ** -0.5)).astype(BF16)
        for i in range(nkp):
            xv = x_ref[:, qw + i * LANES:qw + (i + 1) * LANES]
            z = _rope(xv * _pair_norm(xv, lo) * kg_ref[...], c, sp, sm)
            k_ref[:, i * LANES:(i + 1) * LANES] = z.astype(BF16)
            kt_ref[i * LANES:(i + 1) * LANES, :] = z.T.astype(BF16)
            vv = x_ref[:, qw + kw + i * LANES:qw + kw + (i + 1) * LANES]
            v_ref[:, i * LANES:(i + 1) * LANES] = vv.astype(BF16)
            vvt = vv.T
            vt_ref[(2 * i) * LANES:(2 * i + 1) * LANES, :] = jnp.where(top, vvt, 1.0).astype(BF16)
            vt_ref[(2 * i + 1) * LANES:(2 * i + 2) * LANES, :] = jnp.where(top, 1.0, vvt).astype(BF16)

    tab = pl.BlockSpec((ts, LANES), lambda i: (i, 0))
    vec = pl.BlockSpec((1, LANES), lambda i: (0, 0))
    return pl.pallas_call(
        body,
        out_shape=(jax.ShapeDtypeStruct((seq, qw), BF16), jax.ShapeDtypeStruct((seq, kw), BF16),
                   jax.ShapeDtypeStruct((kw, seq), BF16), jax.ShapeDtypeStruct((seq, kw), BF16),
                   jax.ShapeDtypeStruct((N_KV * LANES, seq), BF16)),
        grid=(seq // ts,),
        in_specs=[pl.BlockSpec((ts, qw + 2 * kw), lambda i: (i, 0)), vec, vec, tab, tab, tab],
        out_specs=(pl.BlockSpec((ts, qw), lambda i: (i, 0)), pl.BlockSpec((ts, kw), lambda i: (i, 0)),
                   pl.BlockSpec((kw, ts), lambda i: (0, i)), pl.BlockSpec((ts, kw), lambda i: (i, 0)),
                   pl.BlockSpec((N_KV * LANES, ts), lambda i: (0, i))),
        compiler_params=_cp(("parallel",)), name=name)(qkv, qg2, kg2, *tabs)


def _qk_prep_bwd(qkv, dq, dk, dv, qg2, kg2, tabs, name):
    seq = qkv.shape[0]
    ts = _pick(seq, (256, 128))
    nq, nkp = N_HEADS // 2, N_KV // 2
    qw, kw = N_HEADS * HEAD_DIM, N_KV * HEAD_DIM

    def body(x_ref, dq_ref, dk_ref, dv_ref, qg_ref, kg_ref, c_ref, sp_ref, sm_ref, o_ref, dqg_ref, dkg_ref):
        step = pl.program_id(0)
        lo = lax.broadcasted_iota(jnp.int32, (ts, LANES), 1) < HEAD_DIM
        c, sp, sm = c_ref[...], sp_ref[...], sm_ref[...]

        def one(xv, dz, gain):
            r = _pair_norm(xv, lo)
            nv = xv * r
            dy = _rope_t(dz, c, sp, sm)
            dgp = jnp.sum(dy * nv, axis=0, keepdims=True)
            dn = dy * gain
            t = dn * nv
            m_lo = jnp.sum(jnp.where(lo, t, 0.0), axis=1, keepdims=True)
            m_hi = jnp.sum(jnp.where(lo, 0.0, t), axis=1, keepdims=True)
            m = jnp.where(lo, m_lo, m_hi) * (1.0 / HEAD_DIM)
            return r * (dn - nv * m), dgp

        dqg = jnp.zeros((1, LANES), F32)
        for i in range(nq):
            sl = slice(i * LANES, (i + 1) * LANES)
            dx, dgp = one(x_ref[:, sl], dq_ref[:, sl] * (HEAD_DIM ** -0.5), qg_ref[...])
            o_ref[:, sl] = dx.astype(BF16)
            dqg = dqg + dgp
        dkg = jnp.zeros((1, LANES), F32)
        for i in range(nkp):
            sl = slice(i * LANES, (i + 1) * LANES)
            dx, dgp = one(x_ref[:, qw + i * LANES:qw + (i + 1) * LANES], dk_ref[:, sl], kg_ref[...])
            o_ref[:, qw + i * LANES:qw + (i + 1) * LANES] = dx.astype(BF16)
            dkg = dkg + dgp
            o_ref[:, qw + kw + i * LANES:qw + kw + (i + 1) * LANES] = dv_ref[:, sl].astype(BF16)

        @pl.when(step == 0)
        def _():
            dqg_ref[...] = dqg
            dkg_ref[...] = dkg

        @pl.when(step > 0)
        def _():
            dqg_ref[...] += dqg
            dkg_ref[...] += dkg

    tab = pl.BlockSpec((ts, LANES), lambda i: (i, 0))
    vec = pl.BlockSpec((1, LANES), lambda i: (0, 0))
    return pl.pallas_call(
        body,
        out_shape=(jax.ShapeDtypeStruct((seq, qw + 2 * kw), BF16), jax.ShapeDtypeStruct((1, LANES), F32),
                   jax.ShapeDtypeStruct((1, LANES), F32)),
        grid=(seq // ts,),
        in_specs=[pl.BlockSpec((ts, qw + 2 * kw), lambda i: (i, 0)), pl.BlockSpec((ts, qw), lambda i: (i, 0)),
                  pl.BlockSpec((ts, kw), lambda i: (i, 0)), pl.BlockSpec((ts, kw), lambda i: (i, 0)),
                  vec, vec, tab, tab, tab],
        out_specs=(pl.BlockSpec((ts, qw + 2 * kw), lambda i: (i, 0)), vec, vec),
        compiler_params=_cp(("arbitrary",)), name=name)(qkv, dq, dk, dv, qg2, kg2, *tabs)


def _slot(blk, off0, tq):
    half = lax.broadcasted_iota(jnp.int32, (tq, LANES), 1) // HEAD_DIM
    keep = half == jnp.where(off0, 0, 1)
    parts = []
    for p in range(2):
        pair = blk[:, p * LANES:(p + 1) * LANES].astype(F32)
        rolled = pltpu.roll(pair, HEAD_DIM, axis=1)
        parts.append(jnp.where(keep, jnp.where(off0, pair, rolled), 0.0))
        parts.append(jnp.where(keep, jnp.where(off0, rolled, pair), 0.0))
    return jnp.concatenate(parts, axis=0)


def _unslot(x4, off0, tq):
    lo = lax.broadcasted_iota(jnp.int32, (tq, LANES), 1) < HEAD_DIM
    pairs = []
    for p in range(2):
        h0 = x4[(2 * p) * tq:(2 * p + 1) * tq]
        h1 = x4[(2 * p + 1) * tq:(2 * p + 2) * tq]
        a = jnp.where(off0, h0, pltpu.roll(h0, HEAD_DIM, axis=1))
        b = jnp.where(off0, pltpu.roll(h1, HEAD_DIM, axis=1), h1)
        pairs.append(jnp.where(lo, a, b))
    return jnp.concatenate(pairs, axis=1)


def _flash_fwd(q, k, vt, plan, shards, name):
    seq = q.shape[0]
    tq = _pick(seq, (FLASH_TQ, 128))
    tk = _pick(seq, (FLASH_TK, 2048, 512, 256, 128))
    sub = _pick(tk, (FLASH_SUB, 256, 128))
    nq, nkv, nsub = seq // tq, seq // tk, tk // sub
    gw = 4 * HEAD_DIM
    nt = plan.nt

    def body(q_ref, k_ref, vt_ref, *rest):
        o_ref, lse_ref = rest[nt:nt + 2]
        q4_ref, m_ref, acc_ref, st_ref = rest[2 * nt + 2:2 * nt + 6]
        plan.bind(rest[:nt], rest[nt + 2:2 * nt + 2], *rest[2 * nt + 6:])
        g, qi, ki = pl.program_id(0), pl.program_id(1), pl.program_id(2)
        off0 = (g % 2) == 0
        @pl.when(jnp.logical_and(g == 0, jnp.logical_and(qi == 0, ki == 0)))
        def _():
            plan.start()

        @pl.when(jnp.logical_and(g == N_KV - 1, jnp.logical_and(qi == nq - 1, ki == 0)))
        def _():
            plan.forward()

        @pl.when(ki == 0)
        def _():
            q4_ref[...] = _slot(q_ref[...], off0, tq).astype(BF16)
            m_ref[...] = jnp.full(m_ref.shape, NEG, F32)
            acc_ref[...] = jnp.zeros(acc_ref.shape, F32)

        q4 = q4_ref[...]

        def scores(c):
            st_ref[c % 2] = lax.dot_general(k_ref[c * sub:(c + 1) * sub, :], q4, (((1,), (1,)), ((), ())),
                                            preferred_element_type=F32)

        m, acc = m_ref[...], acc_ref[...]
        scores(0)
        for c in range(nsub):
            if c + 1 < nsub:
                scores(c + 1)
            st = st_ref[c % 2]
            m_new = jnp.maximum(m, jnp.max(st, axis=0, keepdims=True))
            pt = jnp.exp2(st - m_new).astype(BF16)
            acc = jnp.exp2(m - m_new) * acc + jnp.dot(vt_ref[:, c * sub:(c + 1) * sub], pt, preferred_element_type=F32)
            m = m_new
        m_ref[...] = m
        acc_ref[...] = acc

        @pl.when(ki == nkv - 1)
        def _():
            acc = acc_ref[...]
            l = jnp.where(off0, acc[HEAD_DIM:HEAD_DIM + 1], acc[0:1])
            o4 = acc.T
            o4 = o4 / pltpu.roll(o4, HEAD_DIM, axis=1)
            o_ref[...] = _unslot(o4, off0, tq).astype(o_ref.dtype)
            lse_ref[...] = jnp.broadcast_to(m_ref[...] + jnp.log2(l), lse_ref.shape)

        @pl.when(jnp.logical_and(g == N_KV - 1, jnp.logical_and(qi == nq - 1, ki == nkv - 1)))
        def _():
            plan.finish()

    return pl.pallas_call(
        body,
        out_shape=(jax.ShapeDtypeStruct((seq, N_HEADS * HEAD_DIM), BF16),
                   jax.ShapeDtypeStruct((N_KV * nq * 8, 4 * tq), F32), *plan.out_shape),
        grid=(N_KV, nq, nkv),
        in_specs=[pl.BlockSpec((tq, gw), lambda g, qi, ki: (qi, g)),
                  pl.BlockSpec((tk, LANES), lambda g, qi, ki: (ki, g // 2)),
                  pl.BlockSpec((LANES, tk), lambda g, qi, ki: (g, ki))] + [ANY] * nt,
        out_specs=(pl.BlockSpec((tq, gw), lambda g, qi, ki: (qi, g)),
                   pl.BlockSpec((8, 4 * tq), lambda g, qi, ki: (g * nq + qi, 0)), *([ANY] * nt)),
        scratch_shapes=[pltpu.VMEM((4 * tq, LANES), BF16), pltpu.VMEM((1, 4 * tq), F32),
                        pltpu.VMEM((LANES, 4 * tq), F32), pltpu.VMEM((2, sub, 4 * tq), F32)] + plan.scratch,
        compiler_params=_cp(("arbitrary", "arbitrary", "arbitrary")), name=name)(q, k, vt, *shards)


def _flash_bwd(q, k, kt, v, do, o, lse, plan, grads, name):
    seq = q.shape[0]
    tq = _pick(seq, (FLASH_TQ, 128))
    tk = _pick(seq, (FLASH_TK, 2048, 512, 256, 128))
    sub = _pick(tk, (FLASH_SUB, 256, 128))
    nq, nkv, nsub = seq // tq, seq // tk, tk // sub
    gw = 4 * HEAD_DIM
    nt = plan.nt

    def body(q_ref, k_ref, kt_ref, v_ref, do_ref, o_ref, lse_ref, *rest):
        dq_ref, dk_ref, dv_ref = rest[nt:nt + 3]
        q4_ref, do4_ref, delta_ref, dqt_ref, st_ref, dpt_ref = rest[2 * nt + 3:2 * nt + 9]
        plan.bind(rest[:nt], rest[nt + 3:2 * nt + 3], *rest[2 * nt + 9:])
        g, qi, ki = pl.program_id(0), pl.program_id(1), pl.program_id(2)
        off0 = (g % 2) == 0

        @pl.when(jnp.logical_and(g == 0, jnp.logical_and(qi == 0, ki == 0)))
        def _():
            plan.start()

        @pl.when(jnp.logical_and(g % 2 == 0, jnp.logical_and(qi == 0, ki == 0)))
        def _():
            dk_ref[...] = jnp.zeros(dk_ref.shape, F32)
            dv_ref[...] = jnp.zeros(dv_ref.shape, F32)

        @pl.when(ki == 0)
        def _():
            q4_ref[...] = _slot(q_ref[...], off0, tq).astype(BF16)
            do4 = _slot(do_ref[...], off0, tq)
            do4_ref[...] = do4.astype(BF16)
            o4 = _slot(o_ref[...], off0, tq)
            delta_ref[...] = jnp.sum((do4 * o4).T, axis=0, keepdims=True)
            dqt_ref[...] = jnp.zeros(dqt_ref.shape, F32)

        q4, do4 = q4_ref[...], do4_ref[...]
        lse_row, delta = lse_ref[0:1, :], delta_ref[...]

        def products(c):
            rows = slice(c * sub, (c + 1) * sub)
            st_ref[c % 2] = lax.dot_general(k_ref[rows, :], q4, (((1,), (1,)), ((), ())), preferred_element_type=F32)
            dpt_ref[c % 2] = lax.dot_general(v_ref[rows, :], do4, (((1,), (1,)), ((), ())), preferred_element_type=F32)

        dqt = dqt_ref[...]
        products(0)
        for c in range(nsub):
            if c + 1 < nsub:
                products(c + 1)
            pt = jnp.exp2(st_ref[c % 2] - lse_row)
            dst = (pt * (dpt_ref[c % 2] - delta)).astype(BF16)
            rows = pl.ds(pl.multiple_of(ki * tk + c * sub, sub), sub)
            dv_ref[rows, :] += jnp.dot(pt.astype(BF16), do4, preferred_element_type=F32)
            dk_ref[rows, :] += jnp.dot(dst, q4, preferred_element_type=F32) * (1.0 / LOG2E)
            dqt = dqt + jnp.dot(kt_ref[:, c * sub:(c + 1) * sub], dst, preferred_element_type=F32)
        dqt_ref[...] = dqt

        @pl.when(ki == nkv - 1)
        def _():
            dq_ref[...] = _unslot(dqt_ref[...].T, off0, tq)

        @pl.when(jnp.logical_and(g == N_KV - 1, jnp.logical_and(qi == nq - 1, ki == nkv - 1)))
        def _():
            plan.finish()

    return pl.pallas_call(
        body,
        out_shape=(jax.ShapeDtypeStruct((seq, N_HEADS * HEAD_DIM), F32),
                   jax.ShapeDtypeStruct((seq, N_KV * HEAD_DIM), F32), jax.ShapeDtypeStruct((seq, N_KV * HEAD_DIM), F32),
                   *plan.out_shape),
        grid=(N_KV, nq, nkv),
        in_specs=[pl.BlockSpec((tq, gw), lambda g, qi, ki: (qi, g)),
                  pl.BlockSpec((tk, LANES), lambda g, qi, ki: (ki, g // 2)),
                  pl.BlockSpec((LANES, tk), lambda g, qi, ki: (g // 2, ki)),
                  pl.BlockSpec((tk, LANES), lambda g, qi, ki: (ki, g // 2)),
                  pl.BlockSpec((tq, gw), lambda g, qi, ki: (qi, g)),
                  pl.BlockSpec((tq, gw), lambda g, qi, ki: (qi, g)),
                  pl.BlockSpec((8, 4 * tq), lambda g, qi, ki: (g * nq + qi, 0))] + [ANY] * nt,
        out_specs=(pl.BlockSpec((tq, gw), lambda g, qi, ki: (qi, g)),
                   pl.BlockSpec((seq, LANES), lambda g, qi, ki: (0, g // 2)),
                   pl.BlockSpec((seq, LANES), lambda g, qi, ki: (0, g // 2)), *([ANY] * nt)),
        scratch_shapes=[pltpu.VMEM((4 * tq, LANES), BF16), pltpu.VMEM((4 * tq, LANES), BF16),
                        pltpu.VMEM((1, 4 * tq), F32), pltpu.VMEM((LANES, 4 * tq), F32),
                        pltpu.VMEM((2, sub, 4 * tq), F32), pltpu.VMEM((2, sub, 4 * tq), F32)] + plan.scratch,
        compiler_params=_cp(("arbitrary", "arbitrary", "arbitrary")), name=name)(q, k, kt, v, do, o, lse, *grads)


def _xattn_fwd(q, kv, name):
    seq, d = q.shape
    mlen = kv.shape[0]
    tq = _pick(seq, (512, 256))

    def body(q_ref, k_ref, v_ref, o_ref):
        for h in range(X_HEADS):
            sl = slice(h * X_HEAD_DIM, (h + 1) * X_HEAD_DIM)
            s = lax.dot_general(q_ref[:, sl], k_ref[:, sl], (((1,), (1,)), ((), ())), preferred_element_type=F32)
            e = jnp.exp(s - jnp.max(s, axis=-1, keepdims=True))
            p = e / jnp.sum(e, axis=-1, keepdims=True)
            o_ref[:, sl] = jnp.dot(p.astype(BF16), v_ref[:, sl], preferred_element_type=F32).astype(o_ref.dtype)

    return pl.pallas_call(
        body, out_shape=jax.ShapeDtypeStruct((seq, d), BF16), grid=(seq // tq,),
        in_specs=[pl.BlockSpec((tq, d), lambda i: (i, 0)), pl.BlockSpec((mlen, d), lambda i: (0, 0)),
                  pl.BlockSpec((mlen, d), lambda i: (0, 1))],
        out_specs=pl.BlockSpec((tq, d), lambda i: (i, 0)), compiler_params=_cp(("parallel",)), name=name)(q, kv, kv)


def _xattn_bwd(q, kv, do, name):
    seq, d = q.shape
    mlen = kv.shape[0]
    tq = _pick(seq, (512, 256))
    scale = X_HEAD_DIM ** -0.5

    def body(q_ref, k_ref, v_ref, do_ref, dq_ref, dkv_ref):
        i = pl.program_id(0)

        @pl.when(i == 0)
        def _():
            dkv_ref[...] = jnp.zeros(dkv_ref.shape, F32)

        for h in range(X_HEADS):
            sl = slice(h * X_HEAD_DIM, (h + 1) * X_HEAD_DIM)
            qh, kh, vh = q_ref[:, sl], k_ref[:, sl], v_ref[:, sl]
            doh = do_ref[:, sl].astype(BF16)
            st = lax.dot_general(kh, qh, (((1,), (1,)), ((), ())), preferred_element_type=F32)
            e = jnp.exp(st - jnp.max(st, axis=0, keepdims=True))
            pt = e / jnp.sum(e, axis=0, keepdims=True)
            dpt = lax.dot_general(vh, doh, (((1,), (1,)), ((), ())), preferred_element_type=F32)
            dst = (pt * (dpt - jnp.sum(pt * dpt, axis=0, keepdims=True))).astype(BF16)
            dkv_ref[:, sl] += jnp.dot(dst, qh, preferred_element_type=F32)
            dkv_ref[:, d + h * X_HEAD_DIM:d + (h + 1) * X_HEAD_DIM] += jnp.dot(pt.astype(BF16), doh,
                                                                                 preferred_element_type=F32)
            dqh = lax.dot_general(dst, kh, (((0,), (0,)), ((), ())), preferred_element_type=F32)
            dq_ref[:, sl] = (dqh * scale).astype(dq_ref.dtype)

    return pl.pallas_call(
        body, out_shape=(jax.ShapeDtypeStruct((seq, d), BF16), jax.ShapeDtypeStruct((mlen, 2 * d), F32)),
        grid=(seq // tq,),
        in_specs=[pl.BlockSpec((tq, d), lambda i: (i, 0)), pl.BlockSpec((mlen, d), lambda i: (0, 0)),
                  pl.BlockSpec((mlen, d), lambda i: (0, 1)), pl.BlockSpec((tq, d), lambda i: (i, 0))],
        out_specs=(pl.BlockSpec((tq, d), lambda i: (i, 0)), pl.BlockSpec((mlen, 2 * d), lambda i: (0, 0))),
        compiler_params=_cp(("arbitrary",)), name=name)(q, kv, kv, do)


def _halo_specs(tr, tc, seq, col):
    per, last = tr // HALO, seq // HALO - 1
    return [pl.BlockSpec((tr, tc), lambda j, r: (r, col(j))),
            pl.BlockSpec((HALO, tc), lambda j, r: (jnp.maximum(r * per - 1, 0), col(j))),
            pl.BlockSpec((HALO, tc), lambda j, r: (jnp.minimum((r + 1) * per, last), col(j)))]


def _extend(main_ref, prev_ref, next_ref, r, nr):
    pv = (r > 0).astype(F32)
    nv = (r < nr - 1).astype(F32)
    return jnp.concatenate([prev_ref[...].astype(F32) * pv, main_ref[...].astype(F32),
                            next_ref[...].astype(F32) * nv], axis=0)


def _conv3(e, w_ref, n):
    return pltpu.roll(e, 1, axis=0) * w_ref[0:1, :] + e * w_ref[1:2, :] + pltpu.roll(e, n - 1, axis=0) * w_ref[2:3, :]


def _conv_gate_fwd(ug, uv, cw, cb, layer, name, plan=None, shards=(), fulls=()):
    seq, f = ug.shape
    tc = 256
    tr = _pick(seq, (512, 256))
    nc, nr = f // tc, seq // tr
    n = tr + 2 * HALO
    nt = plan.nt if plan is not None else 0

    def body(g_ref, gp_ref, gn_ref, v_ref, vp_ref, vn_ref, wg_ref, wv_ref, bg_ref, bv_ref, *rest):
        o_ref = rest[2 * nt]
        j, r = pl.program_id(0), pl.program_id(1)
        if plan is not None:
            plan.bind(rest[:nt], rest[2 * nt + 1:3 * nt + 1], *rest[3 * nt + 1:])

            @pl.when(jnp.logical_and(j == 0, r == 0))
            def _():
                plan.start()

        cg = _conv3(_extend(g_ref, gp_ref, gn_ref, r, nr), wg_ref, n)[HALO:HALO + tr] + bg_ref[...]
        cv = _conv3(_extend(v_ref, vp_ref, vn_ref, r, nr), wv_ref, n)[HALO:HALO + tr] + bv_ref[...]
        o_ref[...] = (cg * jax.nn.sigmoid(cg) * cv).astype(o_ref.dtype)

        if plan is not None:
            @pl.when(jnp.logical_and(j == nc - 1, r == nr - 1))
            def _():
                plan.forward()
                plan.finish()

    w_spec = lambda shift: pl.BlockSpec((None, 3, tc), lambda j, r: (layer, 0, j + shift))
    b_spec = lambda shift: pl.BlockSpec((None, 1, tc), lambda j, r: (layer, 0, j + shift))
    act_shape = jax.ShapeDtypeStruct((seq, f), BF16)
    act_spec = pl.BlockSpec((tr, tc), lambda j, r: (r, j))
    in_specs = _halo_specs(tr, tc, seq, lambda j: j) * 2 + [w_spec(0), w_spec(nc), b_spec(0), b_spec(nc)]
    operands = (ug, ug, ug, uv, uv, uv, cw, cw, cb, cb)
    if plan is None:
        return pl.pallas_call(body, out_shape=act_shape, grid=(nc, nr), in_specs=in_specs, out_specs=act_spec,
                              compiler_params=_cp(("parallel", "parallel")), name=name)(*operands)
    return pl.pallas_call(
        body, out_shape=(act_shape, *plan.out_shape), grid=(nc, nr), in_specs=in_specs + [ANY] * (2 * nt),
        out_specs=(act_spec, *([ANY] * nt)), scratch_shapes=plan.scratch,
        input_output_aliases={len(operands) + nt + t: 1 + t for t in range(nt)},
        compiler_params=_cp(("arbitrary", "arbitrary")), name=name)(*operands, *shards, *fulls)


def _conv_gate_bwd(ug, uv, dact, cw, cb, layer, name):
    seq, f = ug.shape
    tc = 256
    tr = _pick(seq, (512, 256))
    nc, nr = f // tc, seq // tr
    n = tr + 2 * HALO

    def body(g_ref, gp_ref, gn_ref, v_ref, vp_ref, vn_ref, d_ref, dp_ref, dn_ref, wg_ref, wv_ref, bg_ref, bv_ref,
             dug_ref, duv_ref, dwg_ref, dwv_ref):
        r = pl.program_id(1)
        eg = _extend(g_ref, gp_ref, gn_ref, r, nr)
        ev = _extend(v_ref, vp_ref, vn_ref, r, nr)
        da = _extend(d_ref, dp_ref, dn_ref, r, nr)
        eg3 = (pltpu.roll(eg, 1, axis=0), eg, pltpu.roll(eg, n - 1, axis=0))
        ev3 = (pltpu.roll(ev, 1, axis=0), ev, pltpu.roll(ev, n - 1, axis=0))
        cg = eg3[0] * wg_ref[0:1, :] + eg3[1] * wg_ref[1:2, :] + eg3[2] * wg_ref[2:3, :] + bg_ref[...]
        cv = ev3[0] * wv_ref[0:1, :] + ev3[1] * wv_ref[1:2, :] + ev3[2] * wv_ref[2:3, :] + bv_ref[...]
        sg = jax.nn.sigmoid(cg)
        dcv = da * (cg * sg)
        dcg = da * cv * (sg * (1.0 + cg * (1.0 - sg)))

        def back(dc, e3, w_ref, du_ref, dw_ref):
            du = (pltpu.roll(dc, n - 1, axis=0) * w_ref[0:1, :] + dc * w_ref[1:2, :]
                  + pltpu.roll(dc, 1, axis=0) * w_ref[2:3, :])
            du_ref[...] = du[HALO:HALO + tr].astype(du_ref.dtype)
            dcm = dc[HALO:HALO + tr]
            taps = [jnp.sum(dcm * e[HALO:HALO + tr], axis=0, keepdims=True) for e in e3] + [
                    jnp.sum(dcm, axis=0, keepdims=True)]
            part = jnp.concatenate(taps + [jnp.zeros((4, tc), F32)], axis=0)

            @pl.when(r == 0)
            def _():
                dw_ref[...] = part

            @pl.when(r > 0)
            def _():
                dw_ref[...] += part

        back(dcg, eg3, wg_ref, dug_ref, dwg_ref)
        back(dcv, ev3, wv_ref, duv_ref, dwv_ref)

    w_spec = lambda shift: pl.BlockSpec((None, 3, tc), lambda j, r: (layer, 0, j + shift))
    b_spec = lambda shift: pl.BlockSpec((None, 1, tc), lambda j, r: (layer, 0, j + shift))
    out_rows = pl.BlockSpec((tr, tc), lambda j, r: (r, j))
    out_acc = pl.BlockSpec((8, tc), lambda j, r: (0, j))
    return pl.pallas_call(
        body,
        out_shape=(jax.ShapeDtypeStruct((seq, f), BF16), jax.ShapeDtypeStruct((seq, f), BF16),
                   jax.ShapeDtypeStruct((8, f), F32), jax.ShapeDtypeStruct((8, f), F32)),
        grid=(nc, nr),
        in_specs=_halo_specs(tr, tc, seq, lambda j: j) * 3 + [w_spec(0), w_spec(nc), b_spec(0), b_spec(nc)],
        out_specs=(out_rows, out_rows, out_acc, out_acc),
        compiler_params=_cp(("parallel", "arbitrary")), name=name)(ug, ug, ug, uv, uv, uv, dact, dact, dact, cw, cw, cb, cb)


def _pool_count(g, r, tr, n, seq):
    half = jnp.left_shift(1, g)
    t = r * tr - HALO + lax.broadcasted_iota(jnp.int32, (n, 1), 0)
    cnt = jnp.minimum(t + half, seq) - jnp.maximum(t - half, 0)
    return jnp.maximum(cnt, 1).astype(F32)


def _by_group(g, levels):
    out = levels[3]
    for i in (2, 1, 0):
        out = jnp.where(g == i, levels[i], out)
    return out


def _pool_mixed(e, g, cnt, n):
    w2 = e + pltpu.roll(e, 1, axis=0)
    w4 = pltpu.roll(w2, 1, axis=0) + pltpu.roll(w2, n - 1, axis=0)
    w8 = pltpu.roll(w4, 2, axis=0) + pltpu.roll(w4, n - 2, axis=0)
    w16 = pltpu.roll(w8, 4, axis=0) + pltpu.roll(w8, n - 4, axis=0)
    return _by_group(g, (w2, w4, w8, w16)) / cnt - e


def _pool_fwd(hp, xres, pw, scale, name):
    seq, d = hp.shape
    tc = POOL_GROUP_W
    tr = _pick(seq, (512, 256))
    nr = seq // tr
    n = tr + 2 * HALO

    def body(h_ref, hp_ref, hn_ref, x_ref, w_ref, s_ref, o_ref):
        g, r = pl.program_id(0), pl.program_id(1)
        e = _extend(h_ref, hp_ref, hn_ref, r, nr)
        mixed = _pool_mixed(e, g, _pool_count(g, r, tr, n, seq), n)[HALO:HALO + tr]
        y = jnp.dot(mixed.astype(BF16), w_ref[...], preferred_element_type=F32)
        o_ref[...] = x_ref[...] + y * s_ref[...]

    return pl.pallas_call(
        body, out_shape=jax.ShapeDtypeStruct((seq, d), F32), grid=(POOL_GROUPS, nr),
        in_specs=_halo_specs(tr, tc, seq, lambda j: j) + [
            pl.BlockSpec((tr, tc), lambda j, r: (r, j)), pl.BlockSpec((None, tc, tc), lambda j, r: (j, 0, 0)),
            pl.BlockSpec((1, tc), lambda j, r: (0, j))],
        out_specs=pl.BlockSpec((tr, tc), lambda j, r: (r, j)),
        compiler_params=_cp(("parallel", "parallel")), name=name)(hp, hp, hp, xres, pw, scale)


def _pool_bwd(hp, dy, pw, scale, name):
    seq, d = hp.shape
    tc = POOL_GROUP_W
    tr = _pick(seq, (512, 256))
    nr = seq // tr
    n = tr + 2 * HALO

    def body(h_ref, hp_ref, hn_ref, d_ref, dp_ref, dn_ref, w_ref, s_ref, dh_ref, dw_ref, ds_ref):
        g, r = pl.program_id(0), pl.program_id(1)
        cnt = _pool_count(g, r, tr, n, seq)
        e = _extend(h_ref, hp_ref, hn_ref, r, nr)
        mixed = _pool_mixed(e, g, cnt, n)[HALO:HALO + tr].astype(BF16)
        dye = _extend(d_ref, dp_ref, dn_ref, r, nr)
        dyp = (dye * s_ref[...]).astype(BF16)
        dmixed = lax.dot_general(dyp, w_ref[...], (((1,), (1,)), ((), ())), preferred_element_type=F32)
        dwin = dmixed / cnt
        m2 = dwin + pltpu.roll(dwin, n - 1, axis=0)
        m4 = pltpu.roll(m2, 1, axis=0) + pltpu.roll(m2, n - 1, axis=0)
        m8 = pltpu.roll(m4, 2, axis=0) + pltpu.roll(m4, n - 2, axis=0)
        m16 = pltpu.roll(m8, 4, axis=0) + pltpu.roll(m8, n - 4, axis=0)
        dh_ref[...] = (_by_group(g, (m2, m4, m8, m16)) - dmixed)[HALO:HALO + tr]
        ypre = jnp.dot(mixed, w_ref[...], preferred_element_type=F32)
        dsp = jnp.sum(d_ref[...] * ypre, axis=0, keepdims=True)
        dwp = lax.dot_general(mixed, dyp[HALO:HALO + tr], (((0,), (0,)), ((), ())), preferred_element_type=F32)

        @pl.when(r == 0)
        def _():
            dw_ref[...] = dwp
            ds_ref[...] = dsp

        @pl.when(r > 0)
        def _():
            dw_ref[...] += dwp
            ds_ref[...] += dsp

    return pl.pallas_call(
        body,
        out_shape=(jax.ShapeDtypeStruct((seq, d), F32), jax.ShapeDtypeStruct((POOL_GROUPS, tc, tc), F32),
                   jax.ShapeDtypeStruct((1, d), F32)),
        grid=(POOL_GROUPS, nr),
        in_specs=_halo_specs(tr, tc, seq, lambda j: j) * 2 + [
            pl.BlockSpec((None, tc, tc), lambda j, r: (j, 0, 0)), pl.BlockSpec((1, tc), lambda j, r: (0, j))],
        out_specs=(pl.BlockSpec((tr, tc), lambda j, r: (r, j)), pl.BlockSpec((None, tc, tc), lambda j, r: (j, 0, 0)),
                   pl.BlockSpec((1, tc), lambda j, r: (0, j))),
        compiler_params=_cp(("parallel", "arbitrary")), name=name)(hp, hp, hp, dy, dy, dy, pw, scale)


def _adamw_math(w, g, m, v):
    m = ADAM_B1 * m + (1.0 - ADAM_B1) * g
    v = ADAM_B2 * v + (1.0 - ADAM_B2) * (g * g)
    m_hat = m / (1.0 - ADAM_B1 ** ADAM_STEP)
    v_hat = v / (1.0 - ADAM_B2 ** ADAM_STEP)
    delta = -ADAM_LR * (m_hat / (jnp.sqrt(v_hat) + ADAM_EPS) + ADAM_WD * w)
    return delta, m, v


def _adamw(w, ga, gb, m, v, name):
    rows, cols = w.shape
    tr = _pick(rows, (256, 128, 64, 32, 16, 8))
    two = gb is not None

    def body(*refs):
        if two:
            w_ref, ga_ref, gb_ref, m_ref, v_ref, g_out, d_out, m_out, v_out = refs
            g = ga_ref[...] + gb_ref[...]
        else:
            w_ref, ga_ref, m_ref, v_ref, g_out, d_out, m_out, v_out = refs
            g = ga_ref[...]
        delta, m, v = _adamw_math(w_ref[...], g, m_ref[...], v_ref[...])
        g_out[...] = g
        d_out[...] = delta
        m_out[...] = m
        v_out[...] = v

    spec = pl.BlockSpec((tr, cols), lambda i: (i, 0))
    ops = [w, ga] + ([gb] if two else []) + [m, v]
    return pl.pallas_call(
        body, out_shape=tuple(jax.ShapeDtypeStruct((rows, cols), F32) for _ in range(4)), grid=(rows // tr,),
        in_specs=[spec] * len(ops), out_specs=(spec,) * 4, compiler_params=_cp(("parallel",)), name=name)(*ops)


def _sum4(parts, name):
    _, rows, cols = parts.shape
    tr = _pick(rows, (256, 128, 64, 32, 16))

    def body(p_ref, o_ref):
        acc = p_ref[0].astype(F32)
        for kk in range(1, 4):
            acc = acc + p_ref[kk].astype(F32)
        o_ref[...] = acc

    return pl.pallas_call(
        body, out_shape=jax.ShapeDtypeStruct((rows, cols), F32), grid=(rows // tr,),
        in_specs=[pl.BlockSpec((4, tr, cols), lambda i: (0, i, 0))], out_specs=pl.BlockSpec((tr, cols), lambda i: (i, 0)),
        compiler_params=_cp(("parallel",)), name=name)(parts)


def _place():
    x, y, c = lax.axis_index("x"), lax.axis_index("y"), lax.axis_index("c")
    chips = [(1 - x, y), (x, 1 - y), (1 - x, 1 - y)]
    return x, y, c, chips


def _window(ref, axis, j, size, c=None, half=None, lead=(), layers=slice(None)):
    if axis == "r":
        if c is None:
            return ref.at[lead + (layers, pl.ds(pl.multiple_of(j * size, 32), size), slice(None))]
        return ref.at[lead + (layers, pl.ds(pl.multiple_of(j * size + c * half, 32), half), slice(None))]
    cols = pl.ds(pl.multiple_of(j * size, LANES), size)
    if c is None:
        return ref.at[lead + (layers, slice(None), cols)]
    return ref.at[lead + (layers, pl.ds(pl.multiple_of(c * half, 32), half), cols)]


class _Gather:
    def __init__(self, shards, axes, layers=None):
        self.nt, self.axes = len(shards), axes
        self.layers = layers or [slice(None)] * self.nt
        self.out_shape, self.sizes, self.halves = [], [], []
        for s, ax in zip(shards, axes):
            l, rs, cs = s.shape
            self.out_shape.append(jax.ShapeDtypeStruct((l, 4 * rs, cs) if ax == "r" else (l, rs, 4 * cs), s.dtype))
            self.sizes.append(rs if ax == "r" else cs)
            self.halves.append(rs // 2)
        self.scratch = [pltpu.SemaphoreType.DMA((6 * self.nt,)), pltpu.SemaphoreType.DMA((6 * self.nt,)),
                        pltpu.SemaphoreType.DMA((self.nt,))]

    def bind(self, src, dst, send_sems, recv_sems, local_sems):
        self.src, self.dst, self.send_sems, self.recv_sems, self.local_sems = src, dst, send_sems, recv_sems, local_sems

    def _win(self, t, j, core=None):
        return _window(self.dst[t], self.axes[t], j, self.sizes[t], core, self.halves[t], layers=self.layers[t])

    def _ici(self, t, kk, origin):
        _, _, c, chips = _place()
        px, py = chips[kk]
        half = self.src[t].at[self.layers[t], pl.ds(pl.multiple_of(c * self.halves[t], 16), self.halves[t]), :]
        return pltpu.make_async_remote_copy(
            src_ref=half, dst_ref=self._win(t, origin, c), send_sem=self.send_sems.at[t * 3 + kk],
            recv_sem=self.recv_sems.at[t * 3 + kk], device_id=(px, py, c), device_id_type=MESH)

    def _d2d(self, t, kk, origin, core):
        x, y, c, _ = _place()
        k2 = 3 * self.nt + t * 3 + kk
        return pltpu.make_async_remote_copy(
            src_ref=self._win(t, origin, core), dst_ref=self._win(t, origin, core), send_sem=self.send_sems.at[k2],
            recv_sem=self.recv_sems.at[k2], device_id=(x, y, 1 - c), device_id_type=MESH)

    def _local(self, t):
        x, y, _, _ = _place()
        return pltpu.make_async_copy(self.src[t].at[self.layers[t]], self._win(t, 2 * x + y), self.local_sems.at[t])

    def _each(self):
        _, _, _, chips = _place()
        for t in range(self.nt):
            for kk in range(3):
                px, py = chips[kk]
                yield t, kk, 2 * px + py

    def start(self):
        x, y, _, _ = _place()
        for t in range(self.nt):
            self._local(t).start()
        for t, kk, _ in self._each():
            self._ici(t, kk, 2 * x + y).start()

    def forward(self):
        _, _, c, _ = _place()
        for t, kk, origin in self._each():
            self._ici(t, kk, origin).wait_recv()
            self._d2d(t, kk, origin, c).start()

    def finish(self):
        x, y, c, _ = _place()
        for t, kk, origin in self._each():
            self._d2d(t, kk, origin, 1 - c).wait_recv()
        for t, kk, origin in self._each():
            self._ici(t, kk, 2 * x + y).wait_send()
            self._d2d(t, kk, origin, c).wait_send()
        for t in range(self.nt):
            self._local(t).wait()


class _Scatter:
    def __init__(self, grads, axes):
        self.nt, self.axes = len(grads), axes
        self.out_shape, self.sizes = [], []
        for gr, ax in zip(grads, axes):
            l, r, cc = gr.shape
            self.out_shape.append(jax.ShapeDtypeStruct((4, l, r // 4, cc) if ax == "r" else (4, l, r, cc // 4), gr.dtype))
            self.sizes.append(r // 4 if ax == "r" else cc // 4)
        self.scratch = [pltpu.SemaphoreType.DMA((3 * self.nt,)), pltpu.SemaphoreType.DMA((3 * self.nt,)),
                        pltpu.SemaphoreType.DMA((self.nt,))]

    def bind(self, src, dst, send_sems, recv_sems, local_sems):
        self.src, self.dst, self.send_sems, self.recv_sems, self.local_sems = src, dst, send_sems, recv_sems, local_sems

    def _copy(self, t, kk, slot):
        x, y, c, chips = _place()
        px, py = chips[kk]
        return pltpu.make_async_remote_copy(
            src_ref=_window(self.src[t], self.axes[t], 2 * px + py, self.sizes[t]), dst_ref=self.dst[t].at[slot],
            send_sem=self.send_sems.at[t * 3 + kk], recv_sem=self.recv_sems.at[t * 3 + kk],
            device_id=(px, py, c), device_id_type=MESH)

    def _local(self, t):
        x, y, _, _ = _place()
        me = 2 * x + y
        return pltpu.make_async_copy(_window(self.src[t], self.axes[t], me, self.sizes[t]), self.dst[t].at[me],
                                     self.local_sems.at[t])

    def start(self):
        x, y, _, _ = _place()
        for t in range(self.nt):
            self._local(t).start()
            for kk in range(3):
                self._copy(t, kk, 2 * x + y).start()

    def finish(self):
        _, _, _, chips = _place()
        for t in range(self.nt):
            for kk in range(3):
                px, py = chips[kk]
                self._copy(t, kk, 2 * px + py).wait_recv()
        for t in range(self.nt):
            for kk in range(3):
                px, py = chips[kk]
                self._copy(t, kk, 2 * px + py).wait_send()
            self._local(t).wait()


def _comm_call(plan, operands, name):
    nt = plan.nt

    def body(*refs):
        plan.bind(refs[:nt], refs[nt:2 * nt], *refs[2 * nt:])
        plan.start()
        if hasattr(plan, "forward"):
            plan.forward()
        plan.finish()

    return pl.pallas_call(body, out_shape=tuple(plan.out_shape), in_specs=[ANY] * nt, out_specs=tuple([ANY] * nt),
                          scratch_shapes=plan.scratch, name=name)(*operands)


def _swap_sibling(arrs, name):
    nt = len(arrs)

    def body(*refs):
        src, dst = refs[:nt], refs[nt:2 * nt]
        send_sems, recv_sems = refs[2 * nt:]
        x, y, c, _ = _place()
        cps = [pltpu.make_async_remote_copy(src_ref=src[t], dst_ref=dst[t], send_sem=send_sems.at[t],
                                            recv_sem=recv_sems.at[t], device_id=(x, y, 1 - c), device_id_type=MESH)
               for t in range(nt)]
        for cp in cps:
            cp.start()
        for cp in cps:
            cp.wait()

    return pl.pallas_call(
        body, out_shape=tuple(jax.ShapeDtypeStruct(a.shape, a.dtype) for a in arrs), in_specs=[ANY] * nt,
        out_specs=tuple([ANY] * nt),
        scratch_shapes=[pltpu.SemaphoreType.DMA((nt,)), pltpu.SemaphoreType.DMA((nt,))], name=name)(*arrs)


def _gather8(pack, with_sum, name):
    rows = pack.shape[0]
    flips = [f for f in itertools.product((0, 1), repeat=3) if any(f)]

    def body(p_ref, all_ref, *rest):
        if with_sum:
            sum_ref, send_sems, recv_sems = rest
        else:
            send_sems, recv_sems = rest
        x, y, c, _ = _place()
        me = 4 * x + 2 * y + c

        def peer(f):
            return tuple(1 - v if fl else v for v, fl in zip((x, y, c), f))

        all_ref[me] = p_ref[...]
        cps = []
        for kk, f in enumerate(flips):
            cp = pltpu.make_async_remote_copy(src_ref=p_ref, dst_ref=all_ref.at[me], send_sem=send_sems.at[kk],
                                              recv_sem=recv_sems.at[kk], device_id=peer(f), device_id_type=MESH)
            cp.start()
            cps.append(cp)
        for kk, f in enumerate(flips):
            px, py, pc = peer(f)
            pltpu.make_async_remote_copy(src_ref=p_ref, dst_ref=all_ref.at[4 * px + 2 * py + pc],
                                         send_sem=send_sems.at[kk], recv_sem=recv_sems.at[kk], device_id=peer(f),
                                         device_id_type=MESH).wait_recv()
        for cp in cps:
            cp.wait_send()
        if with_sum:
            acc = all_ref[0]
            for d in range(1, 8):
                acc = acc + all_ref[d]
            sum_ref[...] = acc

    vm = pl.BlockSpec(memory_space=pltpu.VMEM)
    out_shape = [jax.ShapeDtypeStruct((8, rows, LANES), F32)] + ([jax.ShapeDtypeStruct((rows, LANES), F32)] if with_sum else [])
    return pl.pallas_call(
        body, out_shape=tuple(out_shape), in_specs=[vm], out_specs=tuple([vm] * len(out_shape)),
        scratch_shapes=[pltpu.SemaphoreType.DMA((7,)), pltpu.SemaphoreType.DMA((7,))], name=name)(pack)


def _pack(arrs):
    flat = jnp.concatenate([a.reshape(-1).astype(F32) for a in arrs])
    rows = -(-flat.shape[0] // (8 * LANES)) * 8
    return jnp.pad(flat, (0, rows * LANES - flat.shape[0])).reshape(rows, LANES)


def _unpack(flat, shapes):
    out, pos = [], 0
    for shp in shapes:
        size = 1
        for s in shp:
            size *= s
        out.append(flat[pos:pos + size].reshape(shp))
        pos += size
    return out


BIG = ("attn_w_qkv", "attn_w_o", "pool_w", "xattn_w_q", "xattn_w_kv", "xattn_w_o", "ffn_w_up", "ffn_w_down")
BIG_AXIS = ("c", "r", "r", "r", "c", "r", "c", "r")
SMALL_REPL = ("attn_norm", "attn_q_gain", "attn_k_gain", "xattn_norm", "mem_norm", "ffn_norm", "ffn_conv_b", "final_norm")
SMALL_SHARD = ("pool_norm", "pool_scale", "ffn_conv_w")
ORDER = ("attn_norm", "attn_w_qkv", "attn_q_gain", "attn_k_gain", "attn_w_o", "pool_norm", "pool_w", "pool_scale",
         "xattn_norm", "mem_norm", "xattn_w_q", "xattn_w_kv", "xattn_w_o", "ffn_norm", "ffn_w_up", "ffn_conv_w",
         "ffn_conv_b", "ffn_w_down", "final_norm")


def _step(x, mem, tgt, w, m, v):
    seq, d = x.shape
    xi, yi, ci = lax.axis_index("x"), lax.axis_index("y"), lax.axis_index("c")
    chip = 2 * xi + yi
    dff = w["ffn_w_down"].shape[1] * 4
    n_layers = w["ffn_norm"].shape[0]

    def as3d(a):
        return a.reshape(a.shape[-3:])
    shards = [as3d(w[nm]).astype(BF16) for nm in BIG]
    (wq,) = _comm_call(_Gather(shards[:1], BIG_AXIS[:1]), shards[:1], "gather_qkv")
    small_in = [w[nm] for nm in SMALL_SHARD]
    (small_all,) = _gather8(_pack(small_in), False, "gather_small")
    per_chip = [_unpack(small_all[2 * j].reshape(-1), [a.shape for a in small_in]) for j in range(4)]
    pool_norm, pool_scale, conv_w = (jnp.concatenate([per_chip[j][i] for j in range(4)], axis=-1) for i in range(3))

    conv_b = w["ffn_conv_b"].reshape(n_layers, 1, -1)
    tabs = _rope_tables(seq)
    qg2 = jnp.tile(w["attn_q_gain"], (1, 2))
    kg2 = jnp.tile(w["attn_k_gain"], (1, 2))
    mm = functools.partial(_mm)

    saved = {}
    x0 = x
    h0 = _rms_fwd(x0, w["attn_norm"], BF16, "rms_attn")
    qkv = mm(h0, wq, "nn", b_l=0, out_dtype=F32, name="mm_qkv")
    q_r, k_r, k_t, v_b, v_t = _qk_prep(qkv, qg2, kg2, tabs, "qk_prep")
    first = [slice(None)] * 5 + [slice(0, 1)] * 2
    o_at, lse, wo, wp, wxq, wxkv, wxo, wup, wdn = _flash_fwd(
        q_r, k_r, v_t, _Gather(shards[1:], BIG_AXIS[1:], first), shards[1:], "flash_fwd")
    ffn_w = {"up": wup, "down": wdn}
    x1, hq0 = mm(o_at, wo, "nn", b_l=0, res=x0, out_dtype=F32, norm_out=(w["xattn_norm"][0:1], BF16), name="mm_attn_o")

    def xattn_fwd(l, xin, hq):
        mn = _rms_fwd(mem, w["mem_norm"][l:l + 1], BF16, f"rms_mem{l}")
        xq = mm(hq, wxq, "nn", b_l=l, scale=X_HEAD_DIM ** -0.5, out_dtype=BF16, name=f"mm_xq{l}")
        kv = mm(mn, wxkv, "nn", b_l=l, out_dtype=BF16, name=f"mm_xkv{l}")
        xo = _xattn_fwd(xq, kv, f"xattn_fwd{l}")
        saved[f"x{l}"] = (hq, mn, xq, kv, xo)
        return mm(xo, wxo, "nn", b_l=l, res=xin, out_dtype=F32, norm_out=(w["ffn_norm"][l:l + 1], BF16), name=f"mm_xo{l}")

    def ffn_fwd(l, xin, hf, norm_out):
        ug = mm(hf, ffn_w["up"], "nn", b_l=l, n=dff, out_dtype=BF16, name=f"mm_up_g{l}")
        uv = mm(hf, ffn_w["up"], "nn", b_l=l, n=dff, b_off=(0, dff), out_dtype=BF16, name=f"mm_up_v{l}")
        if l == 0:
            rest = _Gather(shards[6:], BIG_AXIS[6:], [slice(1, 2)] * 2)
            act, ffn_w["up"], ffn_w["down"] = _conv_gate_fwd(ug, uv, conv_w, conv_b, l, f"conv_gate{l}", rest, shards[6:],
                                                             [ffn_w["up"], ffn_w["down"]])
        else:
            act = _conv_gate_fwd(ug, uv, conv_w, conv_b, l, f"conv_gate{l}")
        saved[f"f{l}"] = (hf, ug, uv, act)
        return mm(act, ffn_w["down"], "nn", b_l=l, res=xin, out_dtype=F32, norm_out=norm_out, name=f"mm_down{l}")

    x2, hf0 = xattn_fwd(0, x1, hq0)
    x3, hp = ffn_fwd(0, x2, hf0, (pool_norm, F32))
    x4 = _pool_fwd(hp, x3, wp, pool_scale, "pool_fwd")
    x5, hf1 = xattn_fwd(1, x4, _rms_fwd(x4, w["xattn_norm"][1:2], BF16, "rms_xq1"))
    xs = [x0, x1, x2, x3, x4, x5, ffn_fwd(1, x5, hf1, None)]
    dres, g_final, loss = _final_loss(xs[6], w["final_norm"].reshape(1, d), tgt, "final_loss")

    grads = {}
    gbuf = {}

    def dw(nm, a, b, layer, full, off=(0, 0), n=None, tn=None):
        gbuf[nm] = _mm(a, b, "tn", out_dtype=BF16, out_full=full, out_l=layer, out_off=off, n=n, tn=tn,
                       alias=gbuf.get(nm), name=f"dw_{nm}{layer}_{off[1]}")

    def ffn_bwd(l, xin, dres):
        hf, ug, uv, act = saved[f"f{l}"]
        wup, wdn = ffn_w["up"], ffn_w["down"]
        dw("ffn_w_down", act, dres, l, wdn.shape)
        dact = _mm(dres, wdn, "nt", b_l=l, out_dtype=BF16, name=f"mm_dact{l}")
        dug, duv, dwg, dwv = _conv_gate_bwd(ug, uv, dact, conv_w, conv_b, l, f"conv_gate_bwd{l}")
        dw("ffn_w_up", hf, dug, l, wup.shape, tn=1408)
        dw("ffn_w_up", hf, duv, l, wup.shape, off=(0, dff), tn=1408)
        dhf = _mm(dug, wup, "nt", b_l=l, n=d, out_dtype=F32, name=f"mm_dhf_g{l}")
        dres, dg = _mm(duv, wup, "nt", b_l=l, n=d, b_off=(0, dff), res=dhf, out_dtype=F32, tm=256,
                       norm_bwd=(xin, w["ffn_norm"][l:l + 1], dres), name=f"mm_dhf_v{l}")
        return dres, dg, jnp.concatenate([dwg[:3], dwv[:3]], axis=1), jnp.concatenate([dwg[3], dwv[3]], axis=0)

    def xattn_bwd(l, xin, dres):
        hq, mn, xq, kv, xo = saved[f"x{l}"]
        dw("xattn_w_o", xo, dres, l, wxo.shape)
        dxo = _mm(dres, wxo, "nt", b_l=l, out_dtype=BF16, name=f"mm_dxo{l}")
        dq, dkv = _xattn_bwd(xq, kv, dxo, f"xattn_bwd{l}")
        dw("xattn_w_q", hq, dq, l, wxq.shape)
        dw("xattn_w_kv", mn, dkv, l, wxkv.shape)
        dmn = _mm(dkv, wxkv, "nt", b_l=l, out_dtype=F32, name=f"mm_dmn{l}")
        _, dg_mem = _rms_bwd(mem, w["mem_norm"][l:l + 1], dmn, None, f"rms_mem_bwd{l}")
        dres, dg = _mm(dq, wxq, "nt", b_l=l, out_dtype=F32, norm_bwd=(xin, w["xattn_norm"][l:l + 1], dres),
                       name=f"mm_dhq{l}")
        return dres, dg, dg_mem

    g_ffn, g_xn, g_mn, g_cw, g_cb = [None] * n_layers, [None] * n_layers, [None] * n_layers, [None] * n_layers, [None] * n_layers
    dres, g_ffn[1], g_cw[1], g_cb[1] = ffn_bwd(1, xs[5], dres)
    dres, g_xn[1], g_mn[1] = xattn_bwd(1, xs[4], dres)
    dhp, g_pw, g_pscale = _pool_bwd(hp, dres, wp, pool_scale, "pool_bwd")
    dres, g_pnorm = _rms_bwd(xs[3], pool_norm, dhp, dres, "rms_pool_bwd")
    dres, g_ffn[0], g_cw[0], g_cb[0] = ffn_bwd(0, xs[2], dres)
    dres, g_xn[0], g_mn[0] = xattn_bwd(0, xs[1], dres)
    dw("attn_w_o", o_at, dres, 0, wo.shape)
    do = _mm(dres, wo, "nt", b_l=0, out_dtype=BF16, name="mm_do")
    gbuf["pool_w"] = g_pw.astype(BF16)
    early = [gbuf[nm] for nm in BIG[1:]]
    dq_r, dk_r, dv, *recv_early = _flash_bwd(q_r, k_r, k_t, v_b, do, o_at, lse, _Scatter(early, BIG_AXIS[1:]), early,
                                             "flash_bwd")
    dqkv, dqg, dkg = _qk_prep_bwd(qkv, dq_r, dk_r, dv, qg2, kg2, tabs, "qk_prep_bwd")
    dw("attn_w_qkv", h0, dqkv, 0, wq.shape)
    grad_x, g_an = _mm(dqkv, wq, "nt", b_l=0, out_dtype=F32, norm_bwd=(x0, w["attn_norm"], dres), name="mm_dh0")

    small_g = {
        "attn_norm": g_an, "attn_q_gain": dqg[:, :HEAD_DIM] + dqg[:, HEAD_DIM:], "attn_k_gain": dkg[:, :HEAD_DIM] + dkg[:, HEAD_DIM:],
        "xattn_norm": jnp.concatenate(g_xn, axis=0), "mem_norm": jnp.concatenate(g_mn, axis=0),
        "ffn_norm": jnp.concatenate(g_ffn, axis=0), "ffn_conv_b": jnp.stack(g_cb, axis=0), "final_norm": g_final.reshape(d),
        "pool_norm": g_pnorm, "pool_scale": g_pscale, "ffn_conv_w": jnp.stack(g_cw, axis=0)}
    names = SMALL_REPL + SMALL_SHARD
    _, total = _gather8(_pack([loss[0, :1]] + [small_g[nm] for nm in names]), True, "reduce_small")
    parts = _unpack(total.reshape(-1), [(1,)] + [small_g[nm].shape for nm in names])
    loss_out = parts[0][0]
    for nm, g in zip(names, parts[1:]):
        if nm in SMALL_SHARD:
            size = w[nm].shape[-1]
            g = lax.dynamic_slice_in_dim(g, chip * size, size, axis=g.ndim - 1)
        grads[nm] = g.reshape(w[nm].shape)

    packed = [_pack([src[nm] for nm in names]) for src in (w, grads, m, v)]
    _, sd, sm, sv = _adamw(packed[0], packed[1], None, packed[2], packed[3], "adamw_small")
    shapes = [w[nm].shape for nm in names]
    delta = dict(zip(names, _unpack(sd.reshape(-1), shapes)))
    new_m = dict(zip(names, _unpack(sm.reshape(-1), shapes)))
    new_v = dict(zip(names, _unpack(sv.reshape(-1), shapes)))

    late = [gbuf[nm] for nm in BIG[:1]]
    recv = list(_comm_call(_Scatter(late, BIG_AXIS[:1]), late, "scatter_qkv")) + recv_early
    sums = []
    for nm, rc in zip(BIG, recv):
        sums.append(_sum4(rc.reshape(4, -1, rc.shape[-1]), f"sum4_{nm}"))
    others = _swap_sibling(sums, "swap_sums")
    for nm, mine, other in zip(BIG, sums, others):
        cols = mine.shape[-1]
        outs = _adamw(w[nm].reshape(-1, cols), mine, other, m[nm].reshape(-1, cols), v[nm].reshape(-1, cols), f"adamw_{nm}")
        grads[nm], delta[nm], new_m[nm], new_v[nm] = (o.reshape(w[nm].shape) for o in outs)

    return loss_out, grad_x, grads, delta, new_m, new_v


def kernel(x, mem, attn_norm, attn_w_qkv, attn_q_gain, attn_k_gain, attn_w_o, pool_norm, pool_w, pool_scale, xattn_norm, mem_norm, xattn_w_q, xattn_w_kv, xattn_w_o, ffn_norm, ffn_w_up, ffn_conv_w, ffn_conv_b, ffn_w_down, final_norm, loss_target, m_attn_norm, m_attn_w_qkv, m_attn_q_gain, m_attn_k_gain, m_attn_w_o, m_pool_norm, m_pool_w, m_pool_scale, m_xattn_norm, m_mem_norm, m_xattn_w_q, m_xattn_w_kv, m_xattn_w_o, m_ffn_norm, m_ffn_w_up, m_ffn_conv_w, m_ffn_conv_b, m_ffn_w_down, m_final_norm, v_attn_norm, v_attn_w_qkv, v_attn_q_gain, v_attn_k_gain, v_attn_w_o, v_pool_norm, v_pool_w, v_pool_scale, v_xattn_norm, v_mem_norm, v_xattn_w_q, v_xattn_w_kv, v_xattn_w_o, v_ffn_norm, v_ffn_w_up, v_ffn_conv_w, v_ffn_conv_b, v_ffn_w_down, v_final_norm):
    given = dict(locals())
    w = {nm: given[nm] for nm in ORDER}
    m = {nm: given["m_" + nm] for nm in ORDER}
    v = {nm: given["v_" + nm] for nm in ORDER}
    seq, d = x.shape[1], x.shape[2]
    loss, grad_x, grads, delta, new_m, new_v = _step(
        x.reshape(seq, d), mem.reshape(mem.shape[1], d), loss_target.reshape(seq, d), w, m, v)
    return (loss, grad_x.reshape(x.shape), *[grads[nm] for nm in ORDER], *[delta[nm] for nm in ORDER],
            *[new_m[nm] for nm in ORDER], *[new_v[nm] for nm in ORDER])
```

```python
import functools
import itertools

import jax
import jax.numpy as jnp
from jax import lax
from jax.experimental import pallas as pl
from jax.experimental.pallas import tpu as pltpu

F32, BF16 = jnp.float32, jnp.bfloat16
EPS = 1e-6
GRID_W = 64
ROPE_THETA = 10000.0
HEAD_DIM = 64
N_HEADS = 16
N_KV = 4
X_HEADS = 4
X_HEAD_DIM = 256
POOL_GROUPS = 4
POOL_GROUP_W = 256
HALO = 16
LANES = 128
ADAM_LR, ADAM_B1, ADAM_B2, ADAM_EPS, ADAM_WD, ADAM_STEP = 0.001, 0.9, 0.999, 1e-08, 0.01, 10
VMEM_LIMIT = 48 * 1024 * 1024
MESH = pl.DeviceIdType.MESH
NEG = -1e30
LOG2E = 1.4426950408889634
FLASH_TQ, FLASH_TK = 512, 4096
FLASH_SUB = 512
ANY = pl.BlockSpec(memory_space=pl.ANY)


def _cp(sem=None):
    return pltpu.CompilerParams(dimension_semantics=sem, vmem_limit_bytes=VMEM_LIMIT)


def _pick(n, cands):
    for c in cands:
        if c <= n and n % c == 0:
            return c
    return n


def _mm(a, b, mode, *, name, out_dtype, tm=None, tn=None, tk=None, n=None, k=None, b_l=None, b_off=(0, 0),
        res=None, scale=None, out_full=None, out_l=None, out_off=(0, 0), alias=None, norm_out=None, norm_bwd=None):
    if mode == "tn":
        K, M = a.shape
    else:
        M, K = a.shape
    bs = b.shape[-2:]
    if mode == "nn":
        K = k or K
        N = n or bs[1]
    elif mode == "nt":
        N = n or bs[0]
    else:
        N = n or bs[1]
    wide = (1408, 1024, 512, 256, 128)
    if mode == "tn":
        tm = tm or (M if M <= 1024 else _pick(M, wide))
        tk = tk or _pick(K, (2048, 1024, 512, 256, 128))
    else:
        tm = _pick(M, (tm or 512, 256, 128))
        tk = tk or (K if K <= 2816 else _pick(K, wide))
    tn = tn or (N if N <= 1536 else _pick(N, wide))
    assert M % tm == 0 and N % tn == 0 and K % tk == 0, (name, M, N, K, tm, tn, tk)
    nk = K // tk
    dims = {"nn": ((1,), (0,)), "nt": ((1,), (1,)), "tn": ((0,), (0,))}[mode]

    j_outer = nk == 1 and mode != "tn"

    def at(f):
        return (lambda j, i, kk: f(i, j, kk)) if j_outer else f

    if mode == "tn":
        a_spec = pl.BlockSpec((tk, tm), at(lambda i, j, kk: (kk, i)))
    else:
        a_spec = pl.BlockSpec((tm, tk), at(lambda i, j, kk: (i, kk)))
    if mode == "nt":
        bb, (d0, d1) = (tn, tk), (b_off[0] // tn, b_off[1] // tk)
        assert b_off[0] % tn == 0 and b_off[1] % tk == 0
        bidx = lambda i, j, kk: (j + d0, kk + d1)
    else:
        bb, (d0, d1) = (tk, tn), (b_off[0] // tk, b_off[1] // tn)
        assert b_off[0] % tk == 0 and b_off[1] % tn == 0
        bidx = lambda i, j, kk: (kk + d0, j + d1)
    if b.ndim == 3:
        b_spec = pl.BlockSpec((None,) + bb, at(lambda i, j, kk: (b_l,) + bidx(i, j, kk)))
    else:
        b_spec = pl.BlockSpec(bb, at(bidx))
    in_specs, operands = [a_spec, b_spec], [a, b]
    if res is not None:
        in_specs.append(pl.BlockSpec((tm, tn), at(lambda i, j, kk: (i, j))))
        operands.append(res)
    aliases = {}
    if alias is not None:
        aliases = {len(operands): 0}
        in_specs.append(ANY)
        operands.append(alias)
    if out_full is None:
        out_shape = jax.ShapeDtypeStruct((M, N), out_dtype)
        out_spec = pl.BlockSpec((tm, tn), at(lambda i, j, kk: (i, j)))
    else:
        assert out_off[0] % tm == 0 and out_off[1] % tn == 0
        o0, o1 = out_off[0] // tm, out_off[1] // tn
        out_shape = jax.ShapeDtypeStruct(out_full, out_dtype)
        out_spec = pl.BlockSpec((None, tm, tn), at(lambda i, j, kk: (out_l, i + o0, j + o1)))
    has_res, has_alias = res is not None, alias is not None
    grid = (N // tn, M // tm, nk) if j_outer else (M // tm, N // tn, nk)
    n_extra = 0
    if norm_out is not None or norm_bwd is not None:
        assert j_outer and tn == N and out_full is None, name
        row = pl.BlockSpec((tm, tn), at(lambda i, j, kk: (i, 0)))
        vec = pl.BlockSpec((1, tn), at(lambda i, j, kk: (0, 0)))
        if norm_out is not None:
            in_specs.append(vec)
            operands.append(norm_out[0])
            n_extra = 1
            out_shape = (out_shape, jax.ShapeDtypeStruct((M, N), norm_out[1]))
            out_spec = (out_spec, row)
        else:
            in_specs += [row, vec, row]
            operands += list(norm_bwd)
            n_extra = 3
            out_shape = (out_shape, jax.ShapeDtypeStruct((1, N), F32))
            out_spec = (out_spec, vec)
    n_out = 1 if n_extra == 0 else 2

    def body(*refs):
        a_ref, b_ref = refs[0], refs[1]
        pos = 2
        res_ref = None
        if has_res:
            res_ref = refs[pos]
            pos += 1
        if has_alias:
            pos += 1
        extra = refs[pos:pos + n_extra]
        pos += n_extra
        o_ref, acc_ref = refs[pos], refs[pos + n_out]
        kk = pl.program_id(2)
        part = lax.dot_general(a_ref[...].astype(BF16), b_ref[...].astype(BF16), (dims, ((), ())),
                               preferred_element_type=F32)

        def finish(acc):
            if scale is not None:
                acc = acc * scale
            if res_ref is not None:
                acc = acc + res_ref[...]
            if norm_out is not None:
                r = lax.rsqrt(jnp.mean(acc * acc, axis=-1, keepdims=True) + EPS)
                refs[pos + 1][...] = (acc * r * extra[0][...]).astype(refs[pos + 1].dtype)
            if norm_bwd is not None:
                x_ref, g_ref, dres_ref = extra
                dg_ref, step = refs[pos + 1], pl.program_id(1)
                xv = x_ref[...]
                r = lax.rsqrt(jnp.mean(xv * xv, axis=-1, keepdims=True) + EPS)
                nv = xv * r
                dgp = jnp.sum(acc * nv, axis=0, keepdims=True)

                @pl.when(step == 0)
                def _():
                    dg_ref[...] = dgp

                @pl.when(step > 0)
                def _():
                    dg_ref[...] += dgp

                dn = acc * g_ref[...]
                acc = dres_ref[...] + r * (dn - nv * jnp.mean(dn * nv, axis=-1, keepdims=True))
            o_ref[...] = acc.astype(o_ref.dtype)

        if nk == 1:
            finish(part)
        else:
            @pl.when(kk == 0)
            def _():
                acc_ref[...] = part

            @pl.when(jnp.logical_and(kk > 0, kk < nk - 1))
            def _():
                acc_ref[...] += part

            @pl.when(kk == nk - 1)
            def _():
                finish(acc_ref[...] + part)

    return pl.pallas_call(
        body, out_shape=out_shape, grid=grid, in_specs=in_specs, out_specs=out_spec,
        scratch_shapes=[pltpu.VMEM((tm, tn) if nk > 1 else (8, 128), F32)], input_output_aliases=aliases,
        compiler_params=_cp(("arbitrary",) * 3 if norm_bwd is not None else ("parallel", "parallel", "arbitrary")),
        name=name)(*operands)


def _rms_fwd(x, gain, out_dtype, name):
    rows, d = x.shape
    tr = _pick(rows, (512, 256))

    def body(x_ref, g_ref, o_ref):
        xv = x_ref[...]
        r = lax.rsqrt(jnp.mean(xv * xv, axis=-1, keepdims=True) + EPS)
        o_ref[...] = (xv * r * g_ref[...]).astype(o_ref.dtype)

    return pl.pallas_call(
        body, out_shape=jax.ShapeDtypeStruct((rows, d), out_dtype), grid=(rows // tr,),
        in_specs=[pl.BlockSpec((tr, d), lambda i: (i, 0)), pl.BlockSpec((1, d), lambda i: (0, 0))],
        out_specs=pl.BlockSpec((tr, d), lambda i: (i, 0)), compiler_params=_cp(("parallel",)), name=name)(x, gain)


def _rms_bwd(x, gain, dh, dres, name):
    rows, d = x.shape
    tr = _pick(rows, (512, 256))
    need_dx = dres is not None

    def body(*refs):
        if need_dx:
            x_ref, g_ref, dh_ref, dres_ref, o_ref, dg_ref = refs
        else:
            x_ref, g_ref, dh_ref, dg_ref = refs
        i = pl.program_id(0)
        xv = x_ref[...]
        dhv = dh_ref[...].astype(F32)
        r = lax.rsqrt(jnp.mean(xv * xv, axis=-1, keepdims=True) + EPS)
        nv = xv * r
        part = jnp.sum(dhv * nv, axis=0, keepdims=True)

        @pl.when(i == 0)
        def _():
            dg_ref[...] = part

        @pl.when(i > 0)
        def _():
            dg_ref[...] += part

        if need_dx:
            dn = dhv * g_ref[...]
            dx = r * (dn - nv * jnp.mean(dn * nv, axis=-1, keepdims=True))
            o_ref[...] = dres_ref[...] + dx

    row_spec = pl.BlockSpec((tr, d), lambda i: (i, 0))
    vec_spec = pl.BlockSpec((1, d), lambda i: (0, 0))
    if need_dx:
        return pl.pallas_call(
            body, out_shape=(jax.ShapeDtypeStruct((rows, d), F32), jax.ShapeDtypeStruct((1, d), F32)),
            grid=(rows // tr,), in_specs=[row_spec, vec_spec, row_spec, row_spec], out_specs=(row_spec, vec_spec),
            compiler_params=_cp(("arbitrary",)), name=name)(x, gain, dh, dres)
    return None, pl.pallas_call(
        body, out_shape=jax.ShapeDtypeStruct((1, d), F32), grid=(rows // tr,),
        in_specs=[row_spec, vec_spec, row_spec], out_specs=vec_spec,
        compiler_params=_cp(("arbitrary",)), name=name)(x, gain, dh)


def _final_loss(x, gain, target, name):
    rows, d = x.shape
    tr = _pick(rows, (512, 256))
    nsteps = rows // tr

    def body(x_ref, g_ref, t_ref, dx_ref, dg_ref, loss_ref, acc_ref):
        i = pl.program_id(0)
        xv = x_ref[...]
        g = g_ref[...]
        r = lax.rsqrt(jnp.mean(xv * xv, axis=-1, keepdims=True) + EPS)
        nv = xv * r
        err = nv * g - t_ref[...]
        dy = err * (1.0 / d)
        dn = dy * g
        dx_ref[...] = r * (dn - nv * jnp.mean(dn * nv, axis=-1, keepdims=True))
        dgp = jnp.sum(dy * nv, axis=0, keepdims=True)
        lp = jnp.sum(err * err, axis=0, keepdims=True)

        @pl.when(i == 0)
        def _():
            dg_ref[...] = dgp
            acc_ref[...] = lp

        @pl.when(i > 0)
        def _():
            dg_ref[...] += dgp
            acc_ref[...] += lp

        @pl.when(i == nsteps - 1)
        def _():
            tot = jnp.sum(acc_ref[...], axis=1, keepdims=True) * (0.5 / d)
            loss_ref[...] = jnp.broadcast_to(tot, loss_ref.shape)

    row_spec = pl.BlockSpec((tr, d), lambda i: (i, 0))
    vec_spec = pl.BlockSpec((1, d), lambda i: (0, 0))
    return pl.pallas_call(
        body, out_shape=(jax.ShapeDtypeStruct((rows, d), F32), jax.ShapeDtypeStruct((1, d), F32),
                         jax.ShapeDtypeStruct((1, LANES), F32)),
        grid=(nsteps,), in_specs=[row_spec, vec_spec, row_spec],
        out_specs=(row_spec, vec_spec, pl.BlockSpec((1, LANES), lambda i: (0, 0))),
        scratch_shapes=[pltpu.VMEM((1, d), F32)], compiler_params=_cp(("arbitrary",)), name=name)(x, gain, target)


def _rope_tables(seq):
    pairs = HEAD_DIM // 4
    lane = jnp.arange(LANES, dtype=jnp.int32) % HEAD_DIM
    by_col, second, pair = lane // (2 * pairs) == 1, (lane % (2 * pairs)) // pairs == 1, lane % pairs
    inv_freq = ROPE_THETA ** (-pair.astype(F32) / pairs)
    t = jnp.arange(seq, dtype=jnp.int32)[:, None]
    pos = jnp.where(by_col[None, :], t % GRID_W, t // GRID_W).astype(F32)
    ang = pos * inv_freq[None, :]
    cos, sin = jnp.cos(ang), jnp.sin(ang)
    return cos, jnp.where(second[None, :], sin, 0.0), jnp.where(second[None, :], 0.0, -sin)


def _pair_norm(xv, lo):
    sq = xv * xv
    s_lo = jnp.sum(jnp.where(lo, sq, 0.0), axis=1, keepdims=True)
    s_hi = jnp.sum(jnp.where(lo, 0.0, sq), axis=1, keepdims=True)
    return lax.rsqrt(jnp.where(lo, s_lo, s_hi) * (1.0 / HEAD_DIM) + EPS)


def _rope(y, c, sp, sm):
    return y * c + pltpu.roll(y, 16, axis=1) * sp + pltpu.roll(y, LANES - 16, axis=1) * sm


def _rope_t(dz, c, sp, sm):
    return dz * c + pltpu.roll(dz * sp, LANES - 16, axis=1) + pltpu.roll(dz * sm, 16, axis=1)


def _qk_prep(qkv, qg2, kg2, tabs, name):
    seq = qkv.shape[0]
    ts = _pick(seq, (256, 128))
    nq, nkp = N_HEADS // 2, N_KV // 2
    qw, kw = N_HEADS * HEAD_DIM, N_KV * HEAD_DIM

    def body(x_ref, qg_ref, kg_ref, c_ref, sp_ref, sm_ref, q_ref, k_ref, kt_ref, v_ref, vt_ref):
        lo = lax.broadcasted_iota(jnp.int32, (ts, LANES), 1) < HEAD_DIM
        top = lax.broadcasted_iota(jnp.int32, (LANES, ts), 0) < HEAD_DIM
        c, sp, sm = c_ref[...], sp_ref[...], sm_ref[...]
        for i in range(nq):
            xv = x_ref[:, i * LANES:(i + 1) * LANES]
            y = xv * _pair_norm(xv, lo) * qg_ref[...]
            q_ref[:, i * LANES:(i + 1) * LANES] = (_rope(y, c, sp, sm) * (LOG2E * HEAD_DIM ** -0.5)).astype(BF16)
        for i in range(nkp):
            xv = x_ref[:, qw + i * LANES:qw + (i + 1) * LANES]
            z = _rope(xv * _pair_norm(xv, lo) * kg_ref[...], c, sp, sm)
            k_ref[:, i * LANES:(i + 1) * LANES] = z.astype(BF16)
            kt_ref[i * LANES:(i + 1) * LANES, :] = z.T.astype(BF16)
            vv = x_ref[:, qw + kw + i * LANES:qw + kw + (i + 1) * LANES]
            v_ref[:, i * LANES:(i + 1) * LANES] = vv.astype(BF16)
            vvt = vv.T
            vt_ref[(2 * i) * LANES:(2 * i + 1) * LANES, :] = jnp.where(top, vvt, 1.0).astype(BF16)
            vt_ref[(2 * i + 1) * LANES:(2 * i + 2) * LANES, :] = jnp.where(top, 1.0, vvt).astype(BF16)

    tab = pl.BlockSpec((ts, LANES), lambda i: (i, 0))
    vec = pl.BlockSpec((1, LANES), lambda i: (0, 0))
    return pl.pallas_call(
        body,
        out_shape=(jax.ShapeDtypeStruct((seq, qw), BF16), jax.ShapeDtypeStruct((seq, kw), BF16),
                   jax.ShapeDtypeStruct((kw, seq), BF16), jax.ShapeDtypeStruct((seq, kw), BF16),
                   jax.ShapeDtypeStruct((N_KV * LANES, seq), BF16)),
        grid=(seq // ts,),
        in_specs=[pl.BlockSpec((ts, qw + 2 * kw), lambda i: (i, 0)), vec, vec, tab, tab, tab],
        out_specs=(pl.BlockSpec((ts, qw), lambda i: (i, 0)), pl.BlockSpec((ts, kw), lambda i: (i, 0)),
                   pl.BlockSpec((kw, ts), lambda i: (0, i)), pl.BlockSpec((ts, kw), lambda i: (i, 0)),
                   pl.BlockSpec((N_KV * LANES, ts), lambda i: (0, i))),
        compiler_params=_cp(("parallel",)), name=name)(qkv, qg2, kg2, *tabs)


def _qk_prep_bwd(qkv, dq, dk, dv, qg2, kg2, tabs, plan, sends, name):
    seq = qkv.shape[0]
    ts = _pick(seq, (256, 128))
    nq, nkp = N_HEADS // 2, N_KV // 2
    qw, kw = N_HEADS * HEAD_DIM, N_KV * HEAD_DIM
    nt, nsteps = plan.nt, seq // ts

    def body(x_ref, dq_ref, dk_ref, dv_ref, qg_ref, kg_ref, c_ref, sp_ref, sm_ref, *rest):
        o_ref, dqg_ref, dkg_ref = rest[nt:nt + 3]
        plan.bind(rest[:nt], rest[nt + 3:2 * nt + 3], *rest[2 * nt + 3:])
        step = pl.program_id(0)

        @pl.when(step == 0)
        def _():
            plan.start()

        lo = lax.broadcasted_iota(jnp.int32, (ts, LANES), 1) < HEAD_DIM
        c, sp, sm = c_ref[...], sp_ref[...], sm_ref[...]

        def one(xv, dz, gain):
            r = _pair_norm(xv, lo)
            nv = xv * r
            dy = _rope_t(dz, c, sp, sm)
            dgp = jnp.sum(dy * nv, axis=0, keepdims=True)
            dn = dy * gain
            t = dn * nv
            m_lo = jnp.sum(jnp.where(lo, t, 0.0), axis=1, keepdims=True)
            m_hi = jnp.sum(jnp.where(lo, 0.0, t), axis=1, keepdims=True)
            m = jnp.where(lo, m_lo, m_hi) * (1.0 / HEAD_DIM)
            return r * (dn - nv * m), dgp

        dqg = jnp.zeros((1, LANES), F32)
        for i in range(nq):
            sl = slice(i * LANES, (i + 1) * LANES)
            dx, dgp = one(x_ref[:, sl], dq_ref[:, sl] * (HEAD_DIM ** -0.5), qg_ref[...])
            o_ref[:, sl] = dx.astype(BF16)
            dqg = dqg + dgp
        dkg = jnp.zeros((1, LANES), F32)
        for i in range(nkp):
            sl = slice(i * LANES, (i + 1) * LANES)
            dx, dgp = one(x_ref[:, qw + i * LANES:qw + (i + 1) * LANES], dk_ref[:, sl], kg_ref[...])
            o_ref[:, qw + i * LANES:qw + (i + 1) * LANES] = dx.astype(BF16)
            dkg = dkg + dgp
            o_ref[:, qw + kw + i * LANES:qw + kw + (i + 1) * LANES] = dv_ref[:, sl].astype(BF16)

        @pl.when(step == 0)
        def _():
            dqg_ref[...] = dqg
            dkg_ref[...] = dkg

        @pl.when(step > 0)
        def _():
            dqg_ref[...] += dqg
            dkg_ref[...] += dkg

        @pl.when(step == nsteps - 1)
        def _():
            plan.finish()

    tab = pl.BlockSpec((ts, LANES), lambda i: (i, 0))
    vec = pl.BlockSpec((1, LANES), lambda i: (0, 0))
    return pl.pallas_call(
        body,
        out_shape=(jax.ShapeDtypeStruct((seq, qw + 2 * kw), BF16), jax.ShapeDtypeStruct((1, LANES), F32),
                   jax.ShapeDtypeStruct((1, LANES), F32), *plan.out_shape),
        grid=(nsteps,),
        in_specs=[pl.BlockSpec((ts, qw + 2 * kw), lambda i: (i, 0)), pl.BlockSpec((ts, qw), lambda i: (i, 0)),
                  pl.BlockSpec((ts, kw), lambda i: (i, 0)), pl.BlockSpec((ts, kw), lambda i: (i, 0)),
                  vec, vec, tab, tab, tab] + [ANY] * nt,
        out_specs=(pl.BlockSpec((ts, qw + 2 * kw), lambda i: (i, 0)), vec, vec, *([ANY] * nt)),
        scratch_shapes=plan.scratch, compiler_params=_cp(("arbitrary",)), name=name)(qkv, dq, dk, dv, qg2, kg2, *tabs, *sends)


def _slot(blk, off0, tq):
    half = lax.broadcasted_iota(jnp.int32, (tq, LANES), 1) // HEAD_DIM
    keep = half == jnp.where(off0, 0, 1)
    parts = []
    for p in range(2):
        pair = blk[:, p * LANES:(p + 1) * LANES].astype(F32)
        rolled = pltpu.roll(pair, HEAD_DIM, axis=1)
        parts.append(jnp.where(keep, jnp.where(off0, pair, rolled), 0.0))
        parts.append(jnp.where(keep, jnp.where(off0, rolled, pair), 0.0))
    return jnp.concatenate(parts, axis=0)


def _unslot(x4, off0, tq):
    lo = lax.broadcasted_iota(jnp.int32, (tq, LANES), 1) < HEAD_DIM
    pairs = []
    for p in range(2):
        h0 = x4[(2 * p) * tq:(2 * p + 1) * tq]
        h1 = x4[(2 * p + 1) * tq:(2 * p + 2) * tq]
        a = jnp.where(off0, h0, pltpu.roll(h0, HEAD_DIM, axis=1))
        b = jnp.where(off0, pltpu.roll(h1, HEAD_DIM, axis=1), h1)
        pairs.append(jnp.where(lo, a, b))
    return jnp.concatenate(pairs, axis=1)


def _flash_fwd(q, k, vt, plan, shards, name):
    seq = q.shape[0]
    tq = _pick(seq, (FLASH_TQ, 128))
    tk = _pick(seq, (FLASH_TK, 2048, 512, 256, 128))
    sub = _pick(tk, (FLASH_SUB, 256, 128))
    nq, nkv, nsub = seq // tq, seq // tk, tk // sub
    gw = 4 * HEAD_DIM
    nt = plan.nt

    def body(q_ref, k_ref, vt_ref, *rest):
        o_ref, lse_ref = rest[nt:nt + 2]
        q4_ref, m_ref, acc_ref, st_ref = rest[2 * nt + 2:2 * nt + 6]
        plan.bind(rest[:nt], rest[nt + 2:2 * nt + 2], *rest[2 * nt + 6:])
        g, qi, ki = pl.program_id(0), pl.program_id(1), pl.program_id(2)
        off0 = (g % 2) == 0
        @pl.when(jnp.logical_and(g == 0, jnp.logical_and(qi == 0, ki == 0)))
        def _():
            plan.start()

        @pl.when(jnp.logical_and(g == N_KV - 1, jnp.logical_and(qi == nq - 1, ki == 0)))
        def _():
            plan.forward()

        @pl.when(ki == 0)
        def _():
            q4_ref[...] = _slot(q_ref[...], off0, tq).astype(BF16)
            m_ref[...] = jnp.full(m_ref.shape, NEG, F32)
            acc_ref[...] = jnp.zeros(acc_ref.shape, F32)

        q4 = q4_ref[...]

        def scores(c):
            st_ref[c % 2] = lax.dot_general(k_ref[c * sub:(c + 1) * sub, :], q4, (((1,), (1,)), ((), ())),
                                            preferred_element_type=F32)

        m, acc = m_ref[...], acc_ref[...]
        scores(0)
        for c in range(nsub):
            if c + 1 < nsub:
                scores(c + 1)
            st = st_ref[c % 2]
            m_new = jnp.maximum(m, jnp.max(st, axis=0, keepdims=True))
            pt = jnp.exp2(st - m_new).astype(BF16)
            acc = jnp.exp2(m - m_new) * acc + jnp.dot(vt_ref[:, c * sub:(c + 1) * sub], pt, preferred_element_type=F32)
            m = m_new
        m_ref[...] = m
        acc_ref[...] = acc

        @pl.when(ki == nkv - 1)
        def _():
            acc = acc_ref[...]
            l = jnp.where(off0, acc[HEAD_DIM:HEAD_DIM + 1], acc[0:1])
            o4 = acc.T
            o4 = o4 / pltpu.roll(o4, HEAD_DIM, axis=1)
            o_ref[...] = _unslot(o4, off0, tq).astype(o_ref.dtype)
            lse_ref[...] = jnp.broadcast_to(m_ref[...] + jnp.log2(l), lse_ref.shape)

        @pl.when(jnp.logical_and(g == N_KV - 1, jnp.logical_and(qi == nq - 1, ki == nkv - 1)))
        def _():
            plan.finish()

    return pl.pallas_call(
        body,
        out_shape=(jax.ShapeDtypeStruct((seq, N_HEADS * HEAD_DIM), BF16),
                   jax.ShapeDtypeStruct((N_KV * nq * 8, 4 * tq), F32), *plan.out_shape),
        grid=(N_KV, nq, nkv),
        in_specs=[pl.BlockSpec((tq, gw), lambda g, qi, ki: (qi, g)),
                  pl.BlockSpec((tk, LANES), lambda g, qi, ki: (ki, g // 2)),
                  pl.BlockSpec((LANES, tk), lambda g, qi, ki: (g, ki))] + [ANY] * nt,
        out_specs=(pl.BlockSpec((tq, gw), lambda g, qi, ki: (qi, g)),
                   pl.BlockSpec((8, 4 * tq), lambda g, qi, ki: (g * nq + qi, 0)), *([ANY] * nt)),
        scratch_shapes=[pltpu.VMEM((4 * tq, LANES), BF16), pltpu.VMEM((1, 4 * tq), F32),
                        pltpu.VMEM((LANES, 4 * tq), F32), pltpu.VMEM((2, sub, 4 * tq), F32)] + plan.scratch,
        compiler_params=_cp(("arbitrary", "arbitrary", "arbitrary")), name=name)(q, k, vt, *shards)


def _flash_bwd(q, k, kt, v, do, o, lse, plan, grads, name):
    seq = q.shape[0]
    tq = _pick(seq, (FLASH_TQ, 128))
    tk = _pick(seq, (FLASH_TK, 2048, 512, 256, 128))
    sub = _pick(tk, (FLASH_SUB, 256, 128))
    nq, nkv, nsub = seq // tq, seq // tk, tk // sub
    gw = 4 * HEAD_DIM
    nt = plan.nt

    def body(q_ref, k_ref, kt_ref, v_ref, do_ref, o_ref, lse_ref, *rest):
        dq_ref, dk_ref, dv_ref = rest[nt:nt + 3]
        q4_ref, do4_ref, delta_ref, dqt_ref, st_ref, dpt_ref = rest[2 * nt + 3:2 * nt + 9]
        plan.bind(rest[:nt], rest[nt + 3:2 * nt + 3], *rest[2 * nt + 9:])
        g, qi, ki = pl.program_id(0), pl.program_id(1), pl.program_id(2)
        off0 = (g % 2) == 0

        @pl.when(jnp.logical_and(g == 0, jnp.logical_and(qi == 0, ki == 0)))
        def _():
            plan.start()

        @pl.when(jnp.logical_and(g % 2 == 0, jnp.logical_and(qi == 0, ki == 0)))
        def _():
            dk_ref[...] = jnp.zeros(dk_ref.shape, F32)
            dv_ref[...] = jnp.zeros(dv_ref.shape, F32)

        @pl.when(ki == 0)
        def _():
            q4_ref[...] = _slot(q_ref[...], off0, tq).astype(BF16)
            do4 = _slot(do_ref[...], off0, tq)
            do4_ref[...] = do4.astype(BF16)
            o4 = _slot(o_ref[...], off0, tq)
            delta_ref[...] = jnp.sum((do4 * o4).T, axis=0, keepdims=True)
            dqt_ref[...] = jnp.zeros(dqt_ref.shape, F32)

        q4, do4 = q4_ref[...], do4_ref[...]
        lse_row, delta = lse_ref[0:1, :], delta_ref[...]

        def products(c):
            rows = slice(c * sub, (c + 1) * sub)
            st_ref[c % 2] = lax.dot_general(k_ref[rows, :], q4, (((1,), (1,)), ((), ())), preferred_element_type=F32)
            dpt_ref[c % 2] = lax.dot_general(v_ref[rows, :], do4, (((1,), (1,)), ((), ())), preferred_element_type=F32)

        dqt = dqt_ref[...]
        products(0)
        for c in range(nsub):
            if c + 1 < nsub:
                products(c + 1)
            pt = jnp.exp2(st_ref[c % 2] - lse_row)
            dst = (pt * (dpt_ref[c % 2] - delta)).astype(BF16)
            rows = pl.ds(pl.multiple_of(ki * tk + c * sub, sub), sub)
            dv_ref[rows, :] += jnp.dot(pt.astype(BF16), do4, preferred_element_type=F32)
            dk_ref[rows, :] += jnp.dot(dst, q4, preferred_element_type=F32) * (1.0 / LOG2E)
            dqt = dqt + jnp.dot(kt_ref[:, c * sub:(c + 1) * sub], dst, preferred_element_type=F32)
        dqt_ref[...] = dqt

        @pl.when(ki == nkv - 1)
        def _():
            dq_ref[...] = _unslot(dqt_ref[...].T, off0, tq)

        @pl.when(jnp.logical_and(g == N_KV - 1, jnp.logical_and(qi == nq - 1, ki == nkv - 1)))
        def _():
            plan.finish()

    return pl.pallas_call(
        body,
        out_shape=(jax.ShapeDtypeStruct((seq, N_HEADS * HEAD_DIM), F32),
                   jax.ShapeDtypeStruct((seq, N_KV * HEAD_DIM), F32), jax.ShapeDtypeStruct((seq, N_KV * HEAD_DIM), F32),
                   *plan.out_shape),
        grid=(N_KV, nq, nkv),
        in_specs=[pl.BlockSpec((tq, gw), lambda g, qi, ki: (qi, g)),
                  pl.BlockSpec((tk, LANES), lambda g, qi, ki: (ki, g // 2)),
                  pl.BlockSpec((LANES, tk), lambda g, qi, ki: (g // 2, ki)),
                  pl.BlockSpec((tk, LANES), lambda g, qi, ki: (ki, g // 2)),
                  pl.BlockSpec((tq, gw), lambda g, qi, ki: (qi, g)),
                  pl.BlockSpec((tq, gw), lambda g, qi, ki: (qi, g)),
                  pl.BlockSpec((8, 4 * tq), lambda g, qi, ki: (g * nq + qi, 0))] + [ANY] * nt,
        out_specs=(pl.BlockSpec((tq, gw), lambda g, qi, ki: (qi, g)),
                   pl.BlockSpec((seq, LANES), lambda g, qi, ki: (0, g // 2)),
                   pl.BlockSpec((seq, LANES), lambda g, qi, ki: (0, g // 2)), *([ANY] * nt)),
        scratch_shapes=[pltpu.VMEM((4 * tq, LANES), BF16), pltpu.VMEM((4 * tq, LANES), BF16),
                        pltpu.VMEM((1, 4 * tq), F32), pltpu.VMEM((LANES, 4 * tq), F32),
                        pltpu.VMEM((2, sub, 4 * tq), F32), pltpu.VMEM((2, sub, 4 * tq), F32)] + plan.scratch,
        compiler_params=_cp(("arbitrary", "arbitrary", "arbitrary")), name=name)(q, k, kt, v, do, o, lse, *grads)


def _xattn_fwd(q, kv, name):
    seq, d = q.shape
    mlen = kv.shape[0]
    tq = _pick(seq, (512, 256))

    def body(q_ref, k_ref, v_ref, o_ref):
        for h in range(X_HEADS):
            sl = slice(h * X_HEAD_DIM, (h + 1) * X_HEAD_DIM)
            s = lax.dot_general(q_ref[:, sl], k_ref[:, sl], (((1,), (1,)), ((), ())), preferred_element_type=F32)
            e = jnp.exp(s - jnp.max(s, axis=-1, keepdims=True))
            p = e / jnp.sum(e, axis=-1, keepdims=True)
            o_ref[:, sl] = jnp.dot(p.astype(BF16), v_ref[:, sl], preferred_element_type=F32).astype(o_ref.dtype)

    return pl.pallas_call(
        body, out_shape=jax.ShapeDtypeStruct((seq, d), BF16), grid=(seq // tq,),
        in_specs=[pl.BlockSpec((tq, d), lambda i: (i, 0)), pl.BlockSpec((mlen, d), lambda i: (0, 0)),
                  pl.BlockSpec((mlen, d), lambda i: (0, 1))],
        out_specs=pl.BlockSpec((tq, d), lambda i: (i, 0)), compiler_params=_cp(("parallel",)), name=name)(q, kv, kv)


def _xattn_bwd(q, kv, do, name):
    seq, d = q.shape
    mlen = kv.shape[0]
    tq = _pick(seq, (512, 256))
    scale = X_HEAD_DIM ** -0.5

    def body(q_ref, k_ref, v_ref, do_ref, dq_ref, dkv_ref):
        i = pl.program_id(0)

        @pl.when(i == 0)
        def _():
            dkv_ref[...] = jnp.zeros(dkv_ref.shape, F32)

        for h in range(X_HEADS):
            sl = slice(h * X_HEAD_DIM, (h + 1) * X_HEAD_DIM)
            qh, kh, vh = q_ref[:, sl], k_ref[:, sl], v_ref[:, sl]
            doh = do_ref[:, sl].astype(BF16)
            st = lax.dot_general(kh, qh, (((1,), (1,)), ((), ())), preferred_element_type=F32)
            e = jnp.exp(st - jnp.max(st, axis=0, keepdims=True))
            pt = e / jnp.sum(e, axis=0, keepdims=True)
            dpt = lax.dot_general(vh, doh, (((1,), (1,)), ((), ())), preferred_element_type=F32)
            dst = (pt * (dpt - jnp.sum(pt * dpt, axis=0, keepdims=True))).astype(BF16)
            dkv_ref[:, sl] += jnp.dot(dst, qh, preferred_element_type=F32)
            dkv_ref[:, d + h * X_HEAD_DIM:d + (h + 1) * X_HEAD_DIM] += jnp.dot(pt.astype(BF16), doh,
                                                                                 preferred_element_type=F32)
            dqh = lax.dot_general(dst, kh, (((0,), (0,)), ((), ())), preferred_element_type=F32)
            dq_ref[:, sl] = (dqh * scale).astype(dq_ref.dtype)

    return pl.pallas_call(
        body, out_shape=(jax.ShapeDtypeStruct((seq, d), BF16), jax.ShapeDtypeStruct((mlen, 2 * d), F32)),
        grid=(seq // tq,),
        in_specs=[pl.BlockSpec((tq, d), lambda i: (i, 0)), pl.BlockSpec((mlen, d), lambda i: (0, 0)),
                  pl.BlockSpec((mlen, d), lambda i: (0, 1)), pl.BlockSpec((tq, d), lambda i: (i, 0))],
        out_specs=(pl.BlockSpec((tq, d), lambda i: (i, 0)), pl.BlockSpec((mlen, 2 * d), lambda i: (0, 0))),
        compiler_params=_cp(("arbitrary",)), name=name)(q, kv, kv, do)


def _halo_specs(tr, tc, seq, col):
    per, last = tr // HALO, seq // HALO - 1
    return [pl.BlockSpec((tr, tc), lambda j, r: (r, col(j))),
            pl.BlockSpec((HALO, tc), lambda j, r: (jnp.maximum(r * per - 1, 0), col(j))),
            pl.BlockSpec((HALO, tc), lambda j, r: (jnp.minimum((r + 1) * per, last), col(j)))]


def _extend(main_ref, prev_ref, next_ref, r, nr):
    pv = (r > 0).astype(F32)
    nv = (r < nr - 1).astype(F32)
    return jnp.concatenate([prev_ref[...].astype(F32) * pv, main_ref[...].astype(F32),
                            next_ref[...].astype(F32) * nv], axis=0)


def _conv3(e, w_ref, n):
    return pltpu.roll(e, 1, axis=0) * w_ref[0:1, :] + e * w_ref[1:2, :] + pltpu.roll(e, n - 1, axis=0) * w_ref[2:3, :]


def _conv_gate_fwd(ug, uv, cw, cb, layer, name, plan=None, shards=(), fulls=()):
    seq, f = ug.shape
    tc = 256
    tr = _pick(seq, (512, 256))
    nc, nr = f // tc, seq // tr
    n = tr + 2 * HALO
    nt = plan.nt if plan is not None else 0

    def body(g_ref, gp_ref, gn_ref, v_ref, vp_ref, vn_ref, wg_ref, wv_ref, bg_ref, bv_ref, *rest):
        o_ref = rest[2 * nt]
        j, r = pl.program_id(0), pl.program_id(1)
        if plan is not None:
            plan.bind(rest[:nt], rest[2 * nt + 1:3 * nt + 1], *rest[3 * nt + 1:])

            @pl.when(jnp.logical_and(j == 0, r == 0))
            def _():
                plan.start()

        cg = _conv3(_extend(g_ref, gp_ref, gn_ref, r, nr), wg_ref, n)[HALO:HALO + tr] + bg_ref[...]
        cv = _conv3(_extend(v_ref, vp_ref, vn_ref, r, nr), wv_ref, n)[HALO:HALO + tr] + bv_ref[...]
        o_ref[...] = (cg * jax.nn.sigmoid(cg) * cv).astype(o_ref.dtype)

        if plan is not None:
            @pl.when(jnp.logical_and(j == nc - 1, r == nr - 1))
            def _():
                plan.forward()
                plan.finish()

    w_spec = lambda shift: pl.BlockSpec((None, 3, tc), lambda j, r: (layer, 0, j + shift))
    b_spec = lambda shift: pl.BlockSpec((None, 1, tc), lambda j, r: (layer, 0, j + shift))
    act_shape = jax.ShapeDtypeStruct((seq, f), BF16)
    act_spec = pl.BlockSpec((tr, tc), lambda j, r: (r, j))
    in_specs = _halo_specs(tr, tc, seq, lambda j: j) * 2 + [w_spec(0), w_spec(nc), b_spec(0), b_spec(nc)]
    operands = (ug, ug, ug, uv, uv, uv, cw, cw, cb, cb)
    if plan is None:
        return pl.pallas_call(body, out_shape=act_shape, grid=(nc, nr), in_specs=in_specs, out_specs=act_spec,
                              compiler_params=_cp(("parallel", "parallel")), name=name)(*operands)
    return pl.pallas_call(
        body, out_shape=(act_shape, *plan.out_shape), grid=(nc, nr), in_specs=in_specs + [ANY] * (2 * nt),
        out_specs=(act_spec, *([ANY] * nt)), scratch_shapes=plan.scratch,
        input_output_aliases={len(operands) + nt + t: 1 + t for t in range(nt)},
        compiler_params=_cp(("arbitrary", "arbitrary")), name=name)(*operands, *shards, *fulls)


def _conv_gate_bwd(ug, uv, dact, cw, cb, layer, name):
    seq, f = ug.shape
    tc = 256
    tr = _pick(seq, (512, 256))
    nc, nr = f // tc, seq // tr
    n = tr + 2 * HALO

    def body(g_ref, gp_ref, gn_ref, v_ref, vp_ref, vn_ref, d_ref, dp_ref, dn_ref, wg_ref, wv_ref, bg_ref, bv_ref,
             dug_ref, duv_ref, dwg_ref, dwv_ref):
        r = pl.program_id(1)
        eg = _extend(g_ref, gp_ref, gn_ref, r, nr)
        ev = _extend(v_ref, vp_ref, vn_ref, r, nr)
        da = _extend(d_ref, dp_ref, dn_ref, r, nr)
        eg3 = (pltpu.roll(eg, 1, axis=0), eg, pltpu.roll(eg, n - 1, axis=0))
        ev3 = (pltpu.roll(ev, 1, axis=0), ev, pltpu.roll(ev, n - 1, axis=0))
        cg = eg3[0] * wg_ref[0:1, :] + eg3[1] * wg_ref[1:2, :] + eg3[2] * wg_ref[2:3, :] + bg_ref[...]
        cv = ev3[0] * wv_ref[0:1, :] + ev3[1] * wv_ref[1:2, :] + ev3[2] * wv_ref[2:3, :] + bv_ref[...]
        sg = jax.nn.sigmoid(cg)
        dcv = da * (cg * sg)
        dcg = da * cv * (sg * (1.0 + cg * (1.0 - sg)))

        def back(dc, e3, w_ref, du_ref, dw_ref):
            du = (pltpu.roll(dc, n - 1, axis=0) * w_ref[0:1, :] + dc * w_ref[1:2, :]
                  + pltpu.roll(dc, 1, axis=0) * w_ref[2:3, :])
            du_ref[...] = du[HALO:HALO + tr].astype(du_ref.dtype)
            dcm = dc[HALO:HALO + tr]
            taps = [jnp.sum(dcm * e[HALO:HALO + tr], axis=0, keepdims=True) for e in e3] + [
                    jnp.sum(dcm, axis=0, keepdims=True)]
            part = jnp.concatenate(taps + [jnp.zeros((4, tc), F32)], axis=0)

            @pl.when(r == 0)
            def _():
                dw_ref[...] = part

            @pl.when(r > 0)
            def _():
                dw_ref[...] += part

        back(dcg, eg3, wg_ref, dug_ref, dwg_ref)
        back(dcv, ev3, wv_ref, duv_ref, dwv_ref)

    w_spec = lambda shift: pl.BlockSpec((None, 3, tc), lambda j, r: (layer, 0, j + shift))
    b_spec = lambda shift: pl.BlockSpec((None, 1, tc), lambda j, r: (layer, 0, j + shift))
    out_rows = pl.BlockSpec((tr, tc), lambda j, r: (r, j))
    out_acc = pl.BlockSpec((8, tc), lambda j, r: (0, j))
    return pl.pallas_call(
        body,
        out_shape=(jax.ShapeDtypeStruct((seq, f), BF16), jax.ShapeDtypeStruct((seq, f), BF16),
                   jax.ShapeDtypeStruct((8, f), F32), jax.ShapeDtypeStruct((8, f), F32)),
        grid=(nc, nr),
        in_specs=_halo_specs(tr, tc, seq, lambda j: j) * 3 + [w_spec(0), w_spec(nc), b_spec(0), b_spec(nc)],
        out_specs=(out_rows, out_rows, out_acc, out_acc),
        compiler_params=_cp(("parallel", "arbitrary")), name=name)(ug, ug, ug, uv, uv, uv, dact, dact, dact, cw, cw, cb, cb)


def _pool_count(g, r, tr, n, seq):
    half = jnp.left_shift(1, g)
    t = r * tr - HALO + lax.broadcasted_iota(jnp.int32, (n, 1), 0)
    cnt = jnp.minimum(t + half, seq) - jnp.maximum(t - half, 0)
    return jnp.maximum(cnt, 1).astype(F32)


def _by_group(g, levels):
    out = levels[3]
    for i in (2, 1, 0):
        out = jnp.where(g == i, levels[i], out)
    return out


def _pool_mixed(e, g, cnt, n):
    w2 = e + pltpu.roll(e, 1, axis=0)
    w4 = pltpu.roll(w2, 1, axis=0) + pltpu.roll(w2, n - 1, axis=0)
    w8 = pltpu.roll(w4, 2, axis=0) + pltpu.roll(w4, n - 2, axis=0)
    w16 = pltpu.roll(w8, 4, axis=0) + pltpu.roll(w8, n - 4, axis=0)
    return _by_group(g, (w2, w4, w8, w16)) / cnt - e


def _pool_fwd(hp, xres, pw, scale, name):
    seq, d = hp.shape
    tc = POOL_GROUP_W
    tr = _pick(seq, (512, 256))
    nr = seq // tr
    n = tr + 2 * HALO

    def body(h_ref, hp_ref, hn_ref, x_ref, w_ref, s_ref, o_ref):
        g, r = pl.program_id(0), pl.program_id(1)
        e = _extend(h_ref, hp_ref, hn_ref, r, nr)
        mixed = _pool_mixed(e, g, _pool_count(g, r, tr, n, seq), n)[HALO:HALO + tr]
        y = jnp.dot(mixed.astype(BF16), w_ref[...], preferred_element_type=F32)
        o_ref[...] = x_ref[...] + y * s_ref[...]

    return pl.pallas_call(
        body, out_shape=jax.ShapeDtypeStruct((seq, d), F32), grid=(POOL_GROUPS, nr),
        in_specs=_halo_specs(tr, tc, seq, lambda j: j) + [
            pl.BlockSpec((tr, tc), lambda j, r: (r, j)), pl.BlockSpec((None, tc, tc), lambda j, r: (j, 0, 0)),
            pl.BlockSpec((1, tc), lambda j, r: (0, j))],
        out_specs=pl.BlockSpec((tr, tc), lambda j, r: (r, j)),
        compiler_params=_cp(("parallel", "parallel")), name=name)(hp, hp, hp, xres, pw, scale)


def _pool_bwd(hp, dy, pw, scale, name):
    seq, d = hp.shape
    tc = POOL_GROUP_W
    tr = _pick(seq, (512, 256))
    nr = seq // tr
    n = tr + 2 * HALO

    def body(h_ref, hp_ref, hn_ref, d_ref, dp_ref, dn_ref, w_ref, s_ref, dh_ref, dw_ref, ds_ref):
        g, r = pl.program_id(0), pl.program_id(1)
        cnt = _pool_count(g, r, tr, n, seq)
        e = _extend(h_ref, hp_ref, hn_ref, r, nr)
        mixed = _pool_mixed(e, g, cnt, n)[HALO:HALO + tr].astype(BF16)
        dye = _extend(d_ref, dp_ref, dn_ref, r, nr)
        dyp = (dye * s_ref[...]).astype(BF16)
        dmixed = lax.dot_general(dyp, w_ref[...], (((1,), (1,)), ((), ())), preferred_element_type=F32)
        dwin = dmixed / cnt
        m2 = dwin + pltpu.roll(dwin, n - 1, axis=0)
        m4 = pltpu.roll(m2, 1, axis=0) + pltpu.roll(m2, n - 1, axis=0)
        m8 = pltpu.roll(m4, 2, axis=0) + pltpu.roll(m4, n - 2, axis=0)
        m16 = pltpu.roll(m8, 4, axis=0) + pltpu.roll(m8, n - 4, axis=0)
        dh_ref[...] = (_by_group(g, (m2, m4, m8, m16)) - dmixed)[HALO:HALO + tr]
        ypre = jnp.dot(mixed, w_ref[...], preferred_element_type=F32)
        dsp = jnp.sum(d_ref[...] * ypre, axis=0, keepdims=True)
        dwp = lax.dot_general(mixed, dyp[HALO:HALO + tr], (((0,), (0,)), ((), ())), preferred_element_type=F32)

        @pl.when(r == 0)
        def _():
            dw_ref[...] = dwp
            ds_ref[...] = dsp

        @pl.when(r > 0)
        def _():
            dw_ref[...] += dwp
            ds_ref[...] += dsp

    return pl.pallas_call(
        body,
        out_shape=(jax.ShapeDtypeStruct((seq, d), F32), jax.ShapeDtypeStruct((POOL_GROUPS, tc, tc), F32),
                   jax.ShapeDtypeStruct((1, d), F32)),
        grid=(POOL_GROUPS, nr),
        in_specs=_halo_specs(tr, tc, seq, lambda j: j) * 2 + [
            pl.BlockSpec((None, tc, tc), lambda j, r: (j, 0, 0)), pl.BlockSpec((1, tc), lambda j, r: (0, j))],
        out_specs=(pl.BlockSpec((tr, tc), lambda j, r: (r, j)), pl.BlockSpec((None, tc, tc), lambda j, r: (j, 0, 0)),
                   pl.BlockSpec((1, tc), lambda j, r: (0, j))),
        compiler_params=_cp(("parallel", "arbitrary")), name=name)(hp, hp, hp, dy, dy, dy, pw, scale)


def _adamw_math(w, g, m, v):
    m = ADAM_B1 * m + (1.0 - ADAM_B1) * g
    v = ADAM_B2 * v + (1.0 - ADAM_B2) * (g * g)
    m_hat = m / (1.0 - ADAM_B1 ** ADAM_STEP)
    v_hat = v / (1.0 - ADAM_B2 ** ADAM_STEP)
    delta = -ADAM_LR * (m_hat / (jnp.sqrt(v_hat) + ADAM_EPS) + ADAM_WD * w)
    return delta, m, v


def _adamw(w, ga, gb, m, v, name):
    rows, cols = w.shape
    tr = _pick(rows, (256, 128, 64, 32, 16, 8))
    two = gb is not None

    def body(*refs):
        if two:
            w_ref, ga_ref, gb_ref, m_ref, v_ref, g_out, d_out, m_out, v_out = refs
            g = ga_ref[...] + gb_ref[...]
        else:
            w_ref, ga_ref, m_ref, v_ref, g_out, d_out, m_out, v_out = refs
            g = ga_ref[...]
        delta, m, v = _adamw_math(w_ref[...], g, m_ref[...], v_ref[...])
        g_out[...] = g
        d_out[...] = delta
        m_out[...] = m
        v_out[...] = v

    spec = pl.BlockSpec((tr, cols), lambda i: (i, 0))
    ops = [w, ga] + ([gb] if two else []) + [m, v]
    return pl.pallas_call(
        body, out_shape=tuple(jax.ShapeDtypeStruct((rows, cols), F32) for _ in range(4)), grid=(rows // tr,),
        in_specs=[spec] * len(ops), out_specs=(spec,) * 4, compiler_params=_cp(("parallel",)), name=name)(*ops)


def _sum4(parts, name):
    _, rows, cols = parts.shape
    tr = _pick(rows, (256, 128, 64, 32, 16))

    def body(p_ref, o_ref):
        acc = p_ref[0].astype(F32)
        for kk in range(1, 4):
            acc = acc + p_ref[kk].astype(F32)
        o_ref[...] = acc

    return pl.pallas_call(
        body, out_shape=jax.ShapeDtypeStruct((rows, cols), F32), grid=(rows // tr,),
        in_specs=[pl.BlockSpec((4, tr, cols), lambda i: (0, i, 0))], out_specs=pl.BlockSpec((tr, cols), lambda i: (i, 0)),
        compiler_params=_cp(("parallel",)), name=name)(parts)


def _place():
    x, y, c = lax.axis_index("x"), lax.axis_index("y"), lax.axis_index("c")
    chips = [(1 - x, y), (x, 1 - y), (1 - x, 1 - y)]
    return x, y, c, chips


def _window(ref, axis, j, size, c=None, half=None, lead=(), layers=slice(None)):
    if axis == "r":
        if c is None:
            return ref.at[lead + (layers, pl.ds(pl.multiple_of(j * size, 32), size), slice(None))]
        return ref.at[lead + (layers, pl.ds(pl.multiple_of(j * size + c * half, 32), half), slice(None))]
    cols = pl.ds(pl.multiple_of(j * size, LANES), size)
    if c is None:
        return ref.at[lead + (layers, slice(None), cols)]
    return ref.at[lead + (layers, pl.ds(pl.multiple_of(c * half, 32), half), cols)]


class _Gather:
    def __init__(self, shards, axes, layers=None):
        self.nt, self.axes = len(shards), axes
        self.layers = layers or [slice(None)] * self.nt
        self.out_shape, self.sizes, self.halves = [], [], []
        for s, ax in zip(shards, axes):
            l, rs, cs = s.shape
            self.out_shape.append(jax.ShapeDtypeStruct((l, 4 * rs, cs) if ax == "r" else (l, rs, 4 * cs), s.dtype))
            self.sizes.append(rs if ax == "r" else cs)
            self.halves.append(rs // 2)
        self.scratch = [pltpu.SemaphoreType.DMA((6 * self.nt,)), pltpu.SemaphoreType.DMA((6 * self.nt,)),
                        pltpu.SemaphoreType.DMA((self.nt,))]

    def bind(self, src, dst, send_sems, recv_sems, local_sems):
        self.src, self.dst, self.send_sems, self.recv_sems, self.local_sems = src, dst, send_sems, recv_sems, local_sems

    def _win(self, t, j, core=None):
        return _window(self.dst[t], self.axes[t], j, self.sizes[t], core, self.halves[t], layers=self.layers[t])

    def _ici(self, t, kk, origin):
        _, _, c, chips = _place()
        px, py = chips[kk]
        half = self.src[t].at[self.layers[t], pl.ds(pl.multiple_of(c * self.halves[t], 16), self.halves[t]), :]
        return pltpu.make_async_remote_copy(
            src_ref=half, dst_ref=self._win(t, origin, c), send_sem=self.send_sems.at[t * 3 + kk],
            recv_sem=self.recv_sems.at[t * 3 + kk], device_id=(px, py, c), device_id_type=MESH)

    def _d2d(self, t, kk, origin, core):
        x, y, c, _ = _place()
        k2 = 3 * self.nt + t * 3 + kk
        return pltpu.make_async_remote_copy(
            src_ref=self._win(t, origin, core), dst_ref=self._win(t, origin, core), send_sem=self.send_sems.at[k2],
            recv_sem=self.recv_sems.at[k2], device_id=(x, y, 1 - c), device_id_type=MESH)

    def _local(self, t):
        x, y, _, _ = _place()
        return pltpu.make_async_copy(self.src[t].at[self.layers[t]], self._win(t, 2 * x + y), self.local_sems.at[t])

    def _each(self):
        _, _, _, chips = _place()
        for t in range(self.nt):
            for kk in range(3):
                px, py = chips[kk]
                yield t, kk, 2 * px + py

    def start(self):
        x, y, _, _ = _place()
        for t in range(self.nt):
            self._local(t).start()
        for t, kk, _ in self._each():
            self._ici(t, kk, 2 * x + y).start()

    def forward(self):
        _, _, c, _ = _place()
        for t, kk, origin in self._each():
            self._ici(t, kk, origin).wait_recv()
            self._d2d(t, kk, origin, c).start()

    def finish(self):
        x, y, c, _ = _place()
        for t, kk, origin in self._each():
            self._d2d(t, kk, origin, 1 - c).wait_recv()
        for t, kk, origin in self._each():
            self._ici(t, kk, 2 * x + y).wait_send()
            self._d2d(t, kk, origin, c).wait_send()
        for t in range(self.nt):
            self._local(t).wait()


class _Scatter:
    def __init__(self, grads, axes):
        self.nt, self.axes = len(grads), axes
        self.out_shape, self.sizes = [], []
        for gr, ax in zip(grads, axes):
            l, r, cc = gr.shape
            self.out_shape.append(jax.ShapeDtypeStruct((4, l, r // 4, cc) if ax == "r" else (4, l, r, cc // 4), gr.dtype))
            self.sizes.append(r // 4 if ax == "r" else cc // 4)
        self.scratch = [pltpu.SemaphoreType.DMA((3 * self.nt,)), pltpu.SemaphoreType.DMA((3 * self.nt,)),
                        pltpu.SemaphoreType.DMA((self.nt,))]

    def bind(self, src, dst, send_sems, recv_sems, local_sems):
        self.src, self.dst, self.send_sems, self.recv_sems, self.local_sems = src, dst, send_sems, recv_sems, local_sems

    def _copy(self, t, kk, slot):
        x, y, c, chips = _place()
        px, py = chips[kk]
        return pltpu.make_async_remote_copy(
            src_ref=_window(self.src[t], self.axes[t], 2 * px + py, self.sizes[t]), dst_ref=self.dst[t].at[slot],
            send_sem=self.send_sems.at[t * 3 + kk], recv_sem=self.recv_sems.at[t * 3 + kk],
            device_id=(px, py, c), device_id_type=MESH)

    def _local(self, t):
        x, y, _, _ = _place()
        me = 2 * x + y
        return pltpu.make_async_copy(_window(self.src[t], self.axes[t], me, self.sizes[t]), self.dst[t].at[me],
                                     self.local_sems.at[t])

    def start(self):
        x, y, _, _ = _place()
        for t in range(self.nt):
            self._local(t).start()
            for kk in range(3):
                self._copy(t, kk, 2 * x + y).start()

    def finish(self):
        _, _, _, chips = _place()
        for t in range(self.nt):
            for kk in range(3):
                px, py = chips[kk]
                self._copy(t, kk, 2 * px + py).wait_recv()
        for t in range(self.nt):
            for kk in range(3):
                px, py = chips[kk]
                self._copy(t, kk, 2 * px + py).wait_send()
            self._local(t).wait()


def _comm_call(plan, operands, name):
    nt = plan.nt

    def body(*refs):
        plan.bind(refs[:nt], refs[nt:2 * nt], *refs[2 * nt:])
        plan.start()
        if hasattr(plan, "forward"):
            plan.forward()
        plan.finish()

    return pl.pallas_call(body, out_shape=tuple(plan.out_shape), in_specs=[ANY] * nt, out_specs=tuple([ANY] * nt),
                          scratch_shapes=plan.scratch, name=name)(*operands)


class _Swap:
    def __init__(self, arrs):
        self.nt = len(arrs)
        self.out_shape = [jax.ShapeDtypeStruct(a.shape, a.dtype) for a in arrs]
        self.scratch = [pltpu.SemaphoreType.DMA((self.nt,)), pltpu.SemaphoreType.DMA((self.nt,))]

    def bind(self, src, dst, send_sems, recv_sems):
        self.src, self.dst, self.send_sems, self.recv_sems = src, dst, send_sems, recv_sems

    def _copy(self, t):
        x, y, c, _ = _place()
        return pltpu.make_async_remote_copy(src_ref=self.src[t], dst_ref=self.dst[t], send_sem=self.send_sems.at[t],
                                            recv_sem=self.recv_sems.at[t], device_id=(x, y, 1 - c), device_id_type=MESH)

    def start(self):
        for t in range(self.nt):
            self._copy(t).start()

    def finish(self):
        for t in range(self.nt):
            self._copy(t).wait()


def _gather8(pack, with_sum, name):
    rows = pack.shape[0]
    flips = [f for f in itertools.product((0, 1), repeat=3) if any(f)]

    def body(p_ref, all_ref, *rest):
        if with_sum:
            sum_ref, send_sems, recv_sems = rest
        else:
            send_sems, recv_sems = rest
        x, y, c, _ = _place()
        me = 4 * x + 2 * y + c

        def peer(f):
            return tuple(1 - v if fl else v for v, fl in zip((x, y, c), f))

        all_ref[me] = p_ref[...]
        cps = []
        for kk, f in enumerate(flips):
            cp = pltpu.make_async_remote_copy(src_ref=p_ref, dst_ref=all_ref.at[me], send_sem=send_sems.at[kk],
                                              recv_sem=recv_sems.at[kk], device_id=peer(f), device_id_type=MESH)
            cp.start()
            cps.append(cp)
        for kk, f in enumerate(flips):
            px, py, pc = peer(f)
            pltpu.make_async_remote_copy(src_ref=p_ref, dst_ref=all_ref.at[4 * px + 2 * py + pc],
                                         send_sem=send_sems.at[kk], recv_sem=recv_sems.at[kk], device_id=peer(f),
                                         device_id_type=MESH).wait_recv()
        for cp in cps:
            cp.wait_send()
        if with_sum:
            acc = all_ref[0]
            for d in range(1, 8):
                acc = acc + all_ref[d]
            sum_ref[...] = acc

    vm = pl.BlockSpec(memory_space=pltpu.VMEM)
    out_shape = [jax.ShapeDtypeStruct((8, rows, LANES), F32)] + ([jax.ShapeDtypeStruct((rows, LANES), F32)] if with_sum else [])
    return pl.pallas_call(
        body, out_shape=tuple(out_shape), in_specs=[vm], out_specs=tuple([vm] * len(out_shape)),
        scratch_shapes=[pltpu.SemaphoreType.DMA((7,)), pltpu.SemaphoreType.DMA((7,))], name=name)(pack)


def _pack(arrs):
    flat = jnp.concatenate([a.reshape(-1).astype(F32) for a in arrs])
    rows = -(-flat.shape[0] // (8 * LANES)) * 8
    return jnp.pad(flat, (0, rows * LANES - flat.shape[0])).reshape(rows, LANES)


def _unpack(flat, shapes):
    out, pos = [], 0
    for shp in shapes:
        size = 1
        for s in shp:
            size *= s
        out.append(flat[pos:pos + size].reshape(shp))
        pos += size
    return out


BIG = ("attn_w_qkv", "attn_w_o", "pool_w", "xattn_w_q", "xattn_w_kv", "xattn_w_o", "ffn_w_up", "ffn_w_down")
BIG_AXIS = ("c", "r", "r", "r", "c", "r", "c", "r")
SMALL_REPL = ("attn_norm", "attn_q_gain", "attn_k_gain", "xattn_norm", "mem_norm", "ffn_norm", "ffn_conv_b", "final_norm")
SMALL_SHARD = ("pool_norm", "pool_scale", "ffn_conv_w")
ORDER = ("attn_norm", "attn_w_qkv", "attn_q_gain", "attn_k_gain", "attn_w_o", "pool_norm", "pool_w", "pool_scale",
         "xattn_norm", "mem_norm", "xattn_w_q", "xattn_w_kv", "xattn_w_o", "ffn_norm", "ffn_w_up", "ffn_conv_w",
         "ffn_conv_b", "ffn_w_down", "final_norm")


def _step(x, mem, tgt, w, m, v):
    seq, d = x.shape
    xi, yi, ci = lax.axis_index("x"), lax.axis_index("y"), lax.axis_index("c")
    chip = 2 * xi + yi
    dff = w["ffn_w_down"].shape[1] * 4
    n_layers = w["ffn_norm"].shape[0]

    def as3d(a):
        return a.reshape(a.shape[-3:])
    shards = [as3d(w[nm]).astype(BF16) for nm in BIG]
    (wq,) = _comm_call(_Gather(shards[:1], BIG_AXIS[:1]), shards[:1], "gather_qkv")
    small_in = [w[nm] for nm in SMALL_SHARD]
    (small_all,) = _gather8(_pack(small_in), False, "gather_small")
    per_chip = [_unpack(small_all[2 * j].reshape(-1), [a.shape for a in small_in]) for j in range(4)]
    pool_norm, pool_scale, conv_w = (jnp.concatenate([per_chip[j][i] for j in range(4)], axis=-1) for i in range(3))

    conv_b = w["ffn_conv_b"].reshape(n_layers, 1, -1)
    tabs = _rope_tables(seq)
    qg2 = jnp.tile(w["attn_q_gain"], (1, 2))
    kg2 = jnp.tile(w["attn_k_gain"], (1, 2))
    mm = functools.partial(_mm)

    saved = {}
    x0 = x
    h0 = _rms_fwd(x0, w["attn_norm"], BF16, "rms_attn")
    qkv = mm(h0, wq, "nn", b_l=0, out_dtype=F32, name="mm_qkv")
    q_r, k_r, k_t, v_b, v_t = _qk_prep(qkv, qg2, kg2, tabs, "qk_prep")
    first = [slice(None)] * 5 + [slice(0, 1)] * 2
    o_at, lse, wo, wp, wxq, wxkv, wxo, wup, wdn = _flash_fwd(
        q_r, k_r, v_t, _Gather(shards[1:], BIG_AXIS[1:], first), shards[1:], "flash_fwd")
    ffn_w = {"up": wup, "down": wdn}
    x1, hq0 = mm(o_at, wo, "nn", b_l=0, res=x0, out_dtype=F32, norm_out=(w["xattn_norm"][0:1], BF16), name="mm_attn_o")

    def xattn_fwd(l, xin, hq):
        mn = _rms_fwd(mem, w["mem_norm"][l:l + 1], BF16, f"rms_mem{l}")
        xq = mm(hq, wxq, "nn", b_l=l, scale=X_HEAD_DIM ** -0.5, out_dtype=BF16, name=f"mm_xq{l}")
        kv = mm(mn, wxkv, "nn", b_l=l, out_dtype=BF16, name=f"mm_xkv{l}")
        xo = _xattn_fwd(xq, kv, f"xattn_fwd{l}")
        saved[f"x{l}"] = (hq, mn, xq, kv, xo)
        return mm(xo, wxo, "nn", b_l=l, res=xin, out_dtype=F32, norm_out=(w["ffn_norm"][l:l + 1], BF16), name=f"mm_xo{l}")

    def ffn_fwd(l, xin, hf, norm_out):
        ug = mm(hf, ffn_w["up"], "nn", b_l=l, n=dff, out_dtype=BF16, name=f"mm_up_g{l}")
        uv = mm(hf, ffn_w["up"], "nn", b_l=l, n=dff, b_off=(0, dff), out_dtype=BF16, name=f"mm_up_v{l}")
        if l == 0:
            rest = _Gather(shards[6:], BIG_AXIS[6:], [slice(1, 2)] * 2)
            act, ffn_w["up"], ffn_w["down"] = _conv_gate_fwd(ug, uv, conv_w, conv_b, l, f"conv_gate{l}", rest, shards[6:],
                                                             [ffn_w["up"], ffn_w["down"]])
        else:
            act = _conv_gate_fwd(ug, uv, conv_w, conv_b, l, f"conv_gate{l}")
        saved[f"f{l}"] = (hf, ug, uv, act)
        return mm(act, ffn_w["down"], "nn", b_l=l, res=xin, out_dtype=F32, norm_out=norm_out, name=f"mm_down{l}")

    x2, hf0 = xattn_fwd(0, x1, hq0)
    x3, hp = ffn_fwd(0, x2, hf0, (pool_norm, F32))
    x4 = _pool_fwd(hp, x3, wp, pool_scale, "pool_fwd")
    x5, hf1 = xattn_fwd(1, x4, _rms_fwd(x4, w["xattn_norm"][1:2], BF16, "rms_xq1"))
    xs = [x0, x1, x2, x3, x4, x5, ffn_fwd(1, x5, hf1, None)]
    dres, g_final, loss = _final_loss(xs[6], w["final_norm"].reshape(1, d), tgt, "final_loss")

    grads = {}
    gbuf = {}

    def dw(nm, a, b, layer, full, off=(0, 0), n=None, tn=None):
        gbuf[nm] = _mm(a, b, "tn", out_dtype=BF16, out_full=full, out_l=layer, out_off=off, n=n, tn=tn,
                       alias=gbuf.get(nm), name=f"dw_{nm}{layer}_{off[1]}")

    def ffn_bwd(l, xin, dres):
        hf, ug, uv, act = saved[f"f{l}"]
        wup, wdn = ffn_w["up"], ffn_w["down"]
        dw("ffn_w_down", act, dres, l, wdn.shape)
        dact = _mm(dres, wdn, "nt", b_l=l, out_dtype=BF16, name=f"mm_dact{l}")
        dug, duv, dwg, dwv = _conv_gate_bwd(ug, uv, dact, conv_w, conv_b, l, f"conv_gate_bwd{l}")
        dw("ffn_w_up", hf, dug, l, wup.shape, tn=1408)
        dw("ffn_w_up", hf, duv, l, wup.shape, off=(0, dff), tn=1408)
        dhf = _mm(dug, wup, "nt", b_l=l, n=d, out_dtype=F32, name=f"mm_dhf_g{l}")
        dres, dg = _mm(duv, wup, "nt", b_l=l, n=d, b_off=(0, dff), res=dhf, out_dtype=F32, tm=256,
                       norm_bwd=(xin, w["ffn_norm"][l:l + 1], dres), name=f"mm_dhf_v{l}")
        return dres, dg, jnp.concatenate([dwg[:3], dwv[:3]], axis=1), jnp.concatenate([dwg[3], dwv[3]], axis=0)

    def xattn_bwd(l, xin, dres):
        hq, mn, xq, kv, xo = saved[f"x{l}"]
        dw("xattn_w_o", xo, dres, l, wxo.shape)
        dxo = _mm(dres, wxo, "nt", b_l=l, out_dtype=BF16, name=f"mm_dxo{l}")
        dq, dkv = _xattn_bwd(xq, kv, dxo, f"xattn_bwd{l}")
        dw("xattn_w_q", hq, dq, l, wxq.shape)
        dw("xattn_w_kv", mn, dkv, l, wxkv.shape)
        dmn = _mm(dkv, wxkv, "nt", b_l=l, out_dtype=F32, name=f"mm_dmn{l}")
        _, dg_mem = _rms_bwd(mem, w["mem_norm"][l:l + 1], dmn, None, f"rms_mem_bwd{l}")
        dres, dg = _mm(dq, wxq, "nt", b_l=l, out_dtype=F32, norm_bwd=(xin, w["xattn_norm"][l:l + 1], dres),
                       name=f"mm_dhq{l}")
        return dres, dg, dg_mem

    g_ffn, g_xn, g_mn, g_cw, g_cb = [None] * n_layers, [None] * n_layers, [None] * n_layers, [None] * n_layers, [None] * n_layers
    dres, g_ffn[1], g_cw[1], g_cb[1] = ffn_bwd(1, xs[5], dres)
    dres, g_xn[1], g_mn[1] = xattn_bwd(1, xs[4], dres)
    dhp, g_pw, g_pscale = _pool_bwd(hp, dres, wp, pool_scale, "pool_bwd")
    dres, g_pnorm = _rms_bwd(xs[3], pool_norm, dhp, dres, "rms_pool_bwd")
    dres, g_ffn[0], g_cw[0], g_cb[0] = ffn_bwd(0, xs[2], dres)
    dres, g_xn[0], g_mn[0] = xattn_bwd(0, xs[1], dres)
    dw("attn_w_o", o_at, dres, 0, wo.shape)
    do = _mm(dres, wo, "nt", b_l=0, out_dtype=BF16, name="mm_do")
    gbuf["pool_w"] = g_pw.astype(BF16)
    early = [gbuf[nm] for nm in BIG[1:]]
    dq_r, dk_r, dv, *recv_early = _flash_bwd(q_r, k_r, k_t, v_b, do, o_at, lse, _Scatter(early, BIG_AXIS[1:]), early,
                                             "flash_bwd")
    def sum4(nm, rc):
        return _sum4(rc.reshape(4, -1, rc.shape[-1]), f"sum4_{nm}")
    sums_early = [sum4(nm, rc) for nm, rc in zip(BIG[1:], recv_early)]
    dqkv, dqg, dkg, *others_early = _qk_prep_bwd(qkv, dq_r, dk_r, dv, qg2, kg2, tabs, _Swap(sums_early), sums_early,
                                                 "qk_prep_bwd")
    dw("attn_w_qkv", h0, dqkv, 0, wq.shape)
    grad_x, g_an = _mm(dqkv, wq, "nt", b_l=0, out_dtype=F32, norm_bwd=(x0, w["attn_norm"], dres), name="mm_dh0")

    small_g = {
        "attn_norm": g_an, "attn_q_gain": dqg[:, :HEAD_DIM] + dqg[:, HEAD_DIM:], "attn_k_gain": dkg[:, :HEAD_DIM] + dkg[:, HEAD_DIM:],
        "xattn_norm": jnp.concatenate(g_xn, axis=0), "mem_norm": jnp.concatenate(g_mn, axis=0),
        "ffn_norm": jnp.concatenate(g_ffn, axis=0), "ffn_conv_b": jnp.stack(g_cb, axis=0), "final_norm": g_final.reshape(d),
        "pool_norm": g_pnorm, "pool_scale": g_pscale, "ffn_conv_w": jnp.stack(g_cw, axis=0)}
    names = SMALL_REPL + SMALL_SHARD
    _, total = _gather8(_pack([loss[0, :1]] + [small_g[nm] for nm in names]), True, "reduce_small")
    parts = _unpack(total.reshape(-1), [(1,)] + [small_g[nm].shape for nm in names])
    loss_out = parts[0][0]
    for nm, g in zip(names, parts[1:]):
        if nm in SMALL_SHARD:
            size = w[nm].shape[-1]
            g = lax.dynamic_slice_in_dim(g, chip * size, size, axis=g.ndim - 1)
        grads[nm] = g.reshape(w[nm].shape)

    packed = [_pack([src[nm] for nm in names]) for src in (w, grads, m, v)]
    _, sd, sm, sv = _adamw(packed[0], packed[1], None, packed[2], packed[3], "adamw_small")
    shapes = [w[nm].shape for nm in names]
    delta = dict(zip(names, _unpack(sd.reshape(-1), shapes)))
    new_m = dict(zip(names, _unpack(sm.reshape(-1), shapes)))
    new_v = dict(zip(names, _unpack(sv.reshape(-1), shapes)))

    late = [gbuf[nm] for nm in BIG[:1]]
    sums_late = [sum4(BIG[0], _comm_call(_Scatter(late, BIG_AXIS[:1]), late, "scatter_qkv")[0])]
    others_late = _comm_call(_Swap(sums_late), sums_late, "swap_qkv")
    for nm, mine, other in zip(BIG, sums_late + sums_early, list(others_late) + others_early):
        cols = mine.shape[-1]
        outs = _adamw(w[nm].reshape(-1, cols), mine, other, m[nm].reshape(-1, cols), v[nm].reshape(-1, cols), f"adamw_{nm}")
        grads[nm], delta[nm], new_m[nm], new_v[nm] = (o.reshape(w[nm].shape) for o in outs)

    return loss_out, grad_x, grads, delta, new_m, new_v


def kernel(x, mem, attn_norm, attn_w_qkv, attn_q_gain, attn_k_gain, attn_w_o, pool_norm, pool_w, pool_scale, xattn_norm, mem_norm, xattn_w_q, xattn_w_kv, xattn_w_o, ffn_norm, ffn_w_up, ffn_conv_w, ffn_conv_b, ffn_w_down, final_norm, loss_target, m_attn_norm, m_attn_w_qkv, m_attn_q_gain, m_attn_k_gain, m_attn_w_o, m_pool_norm, m_pool_w, m_pool_scale, m_xattn_norm, m_mem_norm, m_xattn_w_q, m_xattn_w_kv, m_xattn_w_o, m_ffn_norm, m_ffn_w_up, m_ffn_conv_w, m_ffn_conv_b, m_ffn_w_down, m_final_norm, v_attn_norm, v_attn_w_qkv, v_attn_q_gain, v_attn_k_gain, v_attn_w_o, v_pool_norm, v_pool_w, v_pool_scale, v_xattn_norm, v_mem_norm, v_xattn_w_q, v_xattn_w_kv, v_xattn_w_o, v_ffn_norm, v_ffn_w_up, v_ffn_conv_w, v_ffn_conv_b, v_ffn_w_down, v_final_norm):
    given = dict(locals())
    w = {nm: given[nm] for nm in ORDER}
    m = {nm: given["m_" + nm] for nm in ORDER}
    v = {nm: given["v_" + nm] for nm in ORDER}
    seq, d = x.shape[1], x.shape[2]
    loss, grad_x, grads, delta, new_m, new_v = _step(
        x.reshape(seq, d), mem.reshape(mem.shape[1], d), loss_target.reshape(seq, d), w, m, v)
    return (loss, grad_x.reshape(x.shape), *[grads[nm] for nm in ORDER], *[delta[nm] for nm in ORDER],
            *[new_m[nm] for nm in ORDER], *[new_v[nm] for nm in ORDER])
```

```python
import functools
import itertools

import jax
import jax.numpy as jnp
from jax import lax
from jax.experimental import pallas as pl
from jax.experimental.pallas import tpu as pltpu

F32, BF16 = jnp.float32, jnp.bfloat16
EPS = 1e-6
GRID_W = 64
ROPE_THETA = 10000.0
HEAD_DIM = 64
N_HEADS = 16
N_KV = 4
X_HEADS = 4
X_HEAD_DIM = 256
POOL_GROUPS = 4
POOL_GROUP_W = 256
HALO = 16
LANES = 128
ADAM_LR, ADAM_B1, ADAM_B2, ADAM_EPS, ADAM_WD, ADAM_STEP = 0.001, 0.9, 0.999, 1e-08, 0.01, 10
VMEM_LIMIT = 48 * 1024 * 1024
MESH = pl.DeviceIdType.MESH
NEG = -1e30
LOG2E = 1.4426950408889634
FLASH_TQ, FLASH_TK = 512, 4096
FLASH_SUB = 512
ANY = pl.BlockSpec(memory_space=pl.ANY)


def _cp(sem=None):
    return pltpu.CompilerParams(dimension_semantics=sem, vmem_limit_bytes=VMEM_LIMIT)


def _pick(n, cands):
    for c in cands:
        if c <= n and n % c == 0:
            return c
    return n


def _mm(a, b, mode, *, name, out_dtype, tm=None, tn=None, tk=None, n=None, k=None, b_l=None, b_off=(0, 0),
        res=None, scale=None, out_full=None, out_l=None, out_off=(0, 0), alias=None, norm_out=None, norm_bwd=None):
    if mode == "tn":
        K, M = a.shape
    else:
        M, K = a.shape
    bs = b.shape[-2:]
    if mode == "nn":
        K = k or K
        N = n or bs[1]
    elif mode == "nt":
        N = n or bs[0]
    else:
        N = n or bs[1]
    wide = (1408, 1024, 512, 256, 128)
    if mode == "tn":
        tm = tm or (M if M <= 1024 else _pick(M, wide))
        tk = tk or _pick(K, (2048, 1024, 512, 256, 128))
    else:
        tm = _pick(M, (tm or 512, 256, 128))
        tk = tk or (K if K <= 2816 else _pick(K, wide))
    tn = tn or (N if N <= 1536 else _pick(N, wide))
    assert M % tm == 0 and N % tn == 0 and K % tk == 0, (name, M, N, K, tm, tn, tk)
    nk = K // tk
    dims = {"nn": ((1,), (0,)), "nt": ((1,), (1,)), "tn": ((0,), (0,))}[mode]

    j_outer = nk == 1 and mode != "tn"

    def at(f):
        return (lambda j, i, kk: f(i, j, kk)) if j_outer else f

    if mode == "tn":
        a_spec = pl.BlockSpec((tk, tm), at(lambda i, j, kk: (kk, i)))
    else:
        a_spec = pl.BlockSpec((tm, tk), at(lambda i, j, kk: (i, kk)))
    if mode == "nt":
        bb, (d0, d1) = (tn, tk), (b_off[0] // tn, b_off[1] // tk)
        assert b_off[0] % tn == 0 and b_off[1] % tk == 0
        bidx = lambda i, j, kk: (j + d0, kk + d1)
    else:
        bb, (d0, d1) = (tk, tn), (b_off[0] // tk, b_off[1] // tn)
        assert b_off[0] % tk == 0 and b_off[1] % tn == 0
        bidx = lambda i, j, kk: (kk + d0, j + d1)
    if b.ndim == 3:
        b_spec = pl.BlockSpec((None,) + bb, at(lambda i, j, kk: (b_l,) + bidx(i, j, kk)))
    else:
        b_spec = pl.BlockSpec(bb, at(bidx))
    in_specs, operands = [a_spec, b_spec], [a, b]
    if res is not None:
        in_specs.append(pl.BlockSpec((tm, tn), at(lambda i, j, kk: (i, j))))
        operands.append(res)
    aliases = {}
    if alias is not None:
        aliases = {len(operands): 0}
        in_specs.append(ANY)
        operands.append(alias)
    if out_full is None:
        out_shape = jax.ShapeDtypeStruct((M, N), out_dtype)
        out_spec = pl.BlockSpec((tm, tn), at(lambda i, j, kk: (i, j)))
    else:
        assert out_off[0] % tm == 0 and out_off[1] % tn == 0
        o0, o1 = out_off[0] // tm, out_off[1] // tn
        out_shape = jax.ShapeDtypeStruct(out_full, out_dtype)
        out_spec = pl.BlockSpec((None, tm, tn), at(lambda i, j, kk: (out_l, i + o0, j + o1)))
    has_res, has_alias = res is not None, alias is not None
    grid = (N // tn, M // tm, nk) if j_outer else (M // tm, N // tn, nk)
    n_extra = 0
    if norm_out is not None or norm_bwd is not None:
        assert j_outer and tn == N and out_full is None, name
        row = pl.BlockSpec((tm, tn), at(lambda i, j, kk: (i, 0)))
        vec = pl.BlockSpec((1, tn), at(lambda i, j, kk: (0, 0)))
        if norm_out is not None:
            in_specs.append(vec)
            operands.append(norm_out[0])
            n_extra = 1
            out_shape = (out_shape, jax.ShapeDtypeStruct((M, N), norm_out[1]))
            out_spec = (out_spec, row)
        else:
            in_specs += [row, vec, row]
            operands += list(norm_bwd)
            n_extra = 3
            out_shape = (out_shape, jax.ShapeDtypeStruct((1, N), F32))
            out_spec = (out_spec, vec)
    n_out = 1 if n_extra == 0 else 2

    def body(*refs):
        a_ref, b_ref = refs[0], refs[1]
        pos = 2
        res_ref = None
        if has_res:
            res_ref = refs[pos]
            pos += 1
        if has_alias:
            pos += 1
        extra = refs[pos:pos + n_extra]
        pos += n_extra
        o_ref, acc_ref = refs[pos], refs[pos + n_out]
        kk = pl.program_id(2)
        part = lax.dot_general(a_ref[...].astype(BF16), b_ref[...].astype(BF16), (dims, ((), ())),
                               preferred_element_type=F32)

        def finish(acc):
            if scale is not None:
                acc = acc * scale
            if res_ref is not None:
                acc = acc + res_ref[...]
            if norm_out is not None:
                r = lax.rsqrt(jnp.mean(acc * acc, axis=-1, keepdims=True) + EPS)
                refs[pos + 1][...] = (acc * r * extra[0][...]).astype(refs[pos + 1].dtype)
            if norm_bwd is not None:
                x_ref, g_ref, dres_ref = extra
                dg_ref, step = refs[pos + 1], pl.program_id(1)
                xv = x_ref[...]
                r = lax.rsqrt(jnp.mean(xv * xv, axis=-1, keepdims=True) + EPS)
                nv = xv * r
                dgp = jnp.sum(acc * nv, axis=0, keepdims=True)

                @pl.when(step == 0)
                def _():
                    dg_ref[...] = dgp

                @pl.when(step > 0)
                def _():
                    dg_ref[...] += dgp

                dn = acc * g_ref[...]
                acc = dres_ref[...] + r * (dn - nv * jnp.mean(dn * nv, axis=-1, keepdims=True))
            o_ref[...] = acc.astype(o_ref.dtype)

        if nk == 1:
            finish(part)
        else:
            @pl.when(kk == 0)
            def _():
                acc_ref[...] = part

            @pl.when(jnp.logical_and(kk > 0, kk < nk - 1))
            def _():
                acc_ref[...] += part

            @pl.when(kk == nk - 1)
            def _():
                finish(acc_ref[...] + part)

    return pl.pallas_call(
        body, out_shape=out_shape, grid=grid, in_specs=in_specs, out_specs=out_spec,
        scratch_shapes=[pltpu.VMEM((tm, tn) if nk > 1 else (8, 128), F32)], input_output_aliases=aliases,
        compiler_params=_cp(("arbitrary",) * 3 if norm_bwd is not None else ("parallel", "parallel", "arbitrary")),
        name=name)(*operands)


def _rms_fwd(x, gain, out_dtype, name):
    rows, d = x.shape
    tr = _pick(rows, (512, 256))

    def body(x_ref, g_ref, o_ref):
        xv = x_ref[...]
        r = lax.rsqrt(jnp.mean(xv * xv, axis=-1, keepdims=True) + EPS)
        o_ref[...] = (xv * r * g_ref[...]).astype(o_ref.dtype)

    return pl.pallas_call(
        body, out_shape=jax.ShapeDtypeStruct((rows, d), out_dtype), grid=(rows // tr,),
        in_specs=[pl.BlockSpec((tr, d), lambda i: (i, 0)), pl.BlockSpec((1, d), lambda i: (0, 0))],
        out_specs=pl.BlockSpec((tr, d), lambda i: (i, 0)), compiler_params=_cp(("parallel",)), name=name)(x, gain)


def _rms_bwd(x, gain, dh, dres, name):
    rows, d = x.shape
    tr = _pick(rows, (512, 256))
    need_dx = dres is not None

    def body(*refs):
        if need_dx:
            x_ref, g_ref, dh_ref, dres_ref, o_ref, dg_ref = refs
        else:
            x_ref, g_ref, dh_ref, dg_ref = refs
        i = pl.program_id(0)
        xv = x_ref[...]
        dhv = dh_ref[...].astype(F32)
        r = lax.rsqrt(jnp.mean(xv * xv, axis=-1, keepdims=True) + EPS)
        nv = xv * r
        part = jnp.sum(dhv * nv, axis=0, keepdims=True)

        @pl.when(i == 0)
        def _():
            dg_ref[...] = part

        @pl.when(i > 0)
        def _():
            dg_ref[...] += part

        if need_dx:
            dn = dhv * g_ref[...]
            dx = r * (dn - nv * jnp.mean(dn * nv, axis=-1, keepdims=True))
            o_ref[...] = dres_ref[...] + dx

    row_spec = pl.BlockSpec((tr, d), lambda i: (i, 0))
    vec_spec = pl.BlockSpec((1, d), lambda i: (0, 0))
    if need_dx:
        return pl.pallas_call(
            body, out_shape=(jax.ShapeDtypeStruct((rows, d), F32), jax.ShapeDtypeStruct((1, d), F32)),
            grid=(rows // tr,), in_specs=[row_spec, vec_spec, row_spec, row_spec], out_specs=(row_spec, vec_spec),
            compiler_params=_cp(("arbitrary",)), name=name)(x, gain, dh, dres)
    return None, pl.pallas_call(
        body, out_shape=jax.ShapeDtypeStruct((1, d), F32), grid=(rows // tr,),
        in_specs=[row_spec, vec_spec, row_spec], out_specs=vec_spec,
        compiler_params=_cp(("arbitrary",)), name=name)(x, gain, dh)


def _final_loss(x, gain, target, name):
    rows, d = x.shape
    tr = _pick(rows, (512, 256))
    nsteps = rows // tr

    def body(x_ref, g_ref, t_ref, dx_ref, dg_ref, loss_ref, acc_ref):
        i = pl.program_id(0)
        xv = x_ref[...]
        g = g_ref[...]
        r = lax.rsqrt(jnp.mean(xv * xv, axis=-1, keepdims=True) + EPS)
        nv = xv * r
        err = nv * g - t_ref[...]
        dy = err * (1.0 / d)
        dn = dy * g
        dx_ref[...] = r * (dn - nv * jnp.mean(dn * nv, axis=-1, keepdims=True))
        dgp = jnp.sum(dy * nv, axis=0, keepdims=True)
        lp = jnp.sum(err * err, axis=0, keepdims=True)

        @pl.when(i == 0)
        def _():
            dg_ref[...] = dgp
            acc_ref[...] = lp

        @pl.when(i > 0)
        def _():
            dg_ref[...] += dgp
            acc_ref[...] += lp

        @pl.when(i == nsteps - 1)
        def _():
            tot = jnp.sum(acc_ref[...], axis=1, keepdims=True) * (0.5 / d)
            loss_ref[...] = jnp.broadcast_to(tot, loss_ref.shape)

    row_spec = pl.BlockSpec((tr, d), lambda i: (i, 0))
    vec_spec = pl.BlockSpec((1, d), lambda i: (0, 0))
    return pl.pallas_call(
        body, out_shape=(jax.ShapeDtypeStruct((rows, d), F32), jax.ShapeDtypeStruct((1, d), F32),
                         jax.ShapeDtypeStruct((1, LANES), F32)),
        grid=(nsteps,), in_specs=[row_spec, vec_spec, row_spec],
        out_specs=(row_spec, vec_spec, pl.BlockSpec((1, LANES), lambda i: (0, 0))),
        scratch_shapes=[pltpu.VMEM((1, d), F32)], compiler_params=_cp(("arbitrary",)), name=name)(x, gain, target)


def _rope_tables(seq):
    pairs = HEAD_DIM // 4
    lane = jnp.arange(LANES, dtype=jnp.int32) % HEAD_DIM
    by_col, second, pair = lane // (2 * pairs) == 1, (lane % (2 * pairs)) // pairs == 1, lane % pairs
    inv_freq = ROPE_THETA ** (-pair.astype(F32) / pairs)
    t = jnp.arange(seq, dtype=jnp.int32)[:, None]
    pos = jnp.where(by_col[None, :], t % GRID_W, t // GRID_W).astype(F32)
    ang = pos * inv_freq[None, :]
    cos, sin = jnp.cos(ang), jnp.sin(ang)
    return cos, jnp.where(second[None, :], sin, 0.0), jnp.where(second[None, :], 0.0, -sin)


def _pair_norm(xv, lo):
    sq = xv * xv
    s_lo = jnp.sum(jnp.where(lo, sq, 0.0), axis=1, keepdims=True)
    s_hi = jnp.sum(jnp.where(lo, 0.0, sq), axis=1, keepdims=True)
    return lax.rsqrt(jnp.where(lo, s_lo, s_hi) * (1.0 / HEAD_DIM) + EPS)


def _rope(y, c, sp, sm):
    return y * c + pltpu.roll(y, 16, axis=1) * sp + pltpu.roll(y, LANES - 16, axis=1) * sm


def _rope_t(dz, c, sp, sm):
    return dz * c + pltpu.roll(dz * sp, LANES - 16, axis=1) + pltpu.roll(dz * sm, 16, axis=1)


def _qk_prep(qkv, qg2, kg2, tabs, name):
    seq = qkv.shape[0]
    ts = _pick(seq, (256, 128))
    nq, nkp = N_HEADS // 2, N_KV // 2
    qw, kw = N_HEADS * HEAD_DIM, N_KV * HEAD_DIM

    def body(x_ref, qg_ref, kg_ref, c_ref, sp_ref, sm_ref, q_ref, k_ref, kt_ref, v_ref, vt_ref):
        lo = lax.broadcasted_iota(jnp.int32, (ts, LANES), 1) < HEAD_DIM
        top = lax.broadcasted_iota(jnp.int32, (LANES, ts), 0) < HEAD_DIM
        c, sp, sm = c_ref[...], sp_ref[...], sm_ref[...]
        for i in range(nq):
            xv = x_ref[:, i * LANES:(i + 1) * LANES]
            y = xv * _pair_norm(xv, lo) * qg_ref[...]
            q_ref[:, i * LANES:(i + 1) * LANES] = (_rope(y, c, sp, sm) * (LOG2E * HEAD_DIM ** -0.5)).astype(BF16)
        for i in range(nkp):
            xv = x_ref[:, qw + i * LANES:qw + (i + 1) * LANES]
            z = _rope(xv * _pair_norm(xv, lo) * kg_ref[...], c, sp, sm)
            k_ref[:, i * LANES:(i + 1) * LANES] = z.astype(BF16)
            kt_ref[i * LANES:(i + 1) * LANES, :] = z.T.astype(BF16)
            vv = x_ref[:, qw + kw + i * LANES:qw + kw + (i + 1) * LANES]
            v_ref[:, i * LANES:(i + 1) * LANES] = vv.astype(BF16)
            vvt = vv.T
            vt_ref[(2 * i) * LANES:(2 * i + 1) * LANES, :] = jnp.where(top, vvt, 1.0).astype(BF16)
            vt_ref[(2 * i + 1) * LANES:(2 * i + 2) * LANES, :] = jnp.where(top, 1.0, vvt).astype(BF16)

    tab = pl.BlockSpec((ts, LANES), lambda i: (i, 0))
    vec = pl.BlockSpec((1, LANES), lambda i: (0, 0))
    return pl.pallas_call(
        body,
        out_shape=(jax.ShapeDtypeStruct((seq, qw), BF16), jax.ShapeDtypeStruct((seq, kw), BF16),
                   jax.ShapeDtypeStruct((kw, seq), BF16), jax.ShapeDtypeStruct((seq, kw), BF16),
                   jax.ShapeDtypeStruct((N_KV * LANES, seq), BF16)),
        grid=(seq // ts,),
        in_specs=[pl.BlockSpec((ts, qw + 2 * kw), lambda i: (i, 0)), vec, vec, tab, tab, tab],
        out_specs=(pl.BlockSpec((ts, qw), lambda i: (i, 0)), pl.BlockSpec((ts, kw), lambda i: (i, 0)),
                   pl.BlockSpec((kw, ts), lambda i: (0, i)), pl.BlockSpec((ts, kw), lambda i: (i, 0)),
                   pl.BlockSpec((N_KV * LANES, ts), lambda i: (0, i))),
        compiler_params=_cp(("parallel",)), name=name)(qkv, qg2, kg2, *tabs)


def _qk_prep_bwd(qkv, dq, dk, dv, qg2, kg2, tabs, plan, sends, name):
    seq = qkv.shape[0]
    ts = _pick(seq, (256, 128))
    nq, nkp = N_HEADS // 2, N_KV // 2
    qw, kw = N_HEADS * HEAD_DIM, N_KV * HEAD_DIM
    nt, nsteps = plan.nt, seq // ts

    def body(x_ref, dq_ref, dk_ref, dv_ref, qg_ref, kg_ref, c_ref, sp_ref, sm_ref, *rest):
        o_ref, dqg_ref, dkg_ref = rest[nt:nt + 3]
        plan.bind(rest[:nt], rest[nt + 3:2 * nt + 3], *rest[2 * nt + 3:])
        step = pl.program_id(0)

        @pl.when(step == 0)
        def _():
            plan.start()

        lo = lax.broadcasted_iota(jnp.int32, (ts, LANES), 1) < HEAD_DIM
        c, sp, sm = c_ref[...], sp_ref[...], sm_ref[...]

        def one(xv, dz, gain):
            r = _pair_norm(xv, lo)
            nv = xv * r
            dy = _rope_t(dz, c, sp, sm)
            dgp = jnp.sum(dy * nv, axis=0, keepdims=True)
            dn = dy * gain
            t = dn * nv
            m_lo = jnp.sum(jnp.where(lo, t, 0.0), axis=1, keepdims=True)
            m_hi = jnp.sum(jnp.where(lo, 0.0, t), axis=1, keepdims=True)
            m = jnp.where(lo, m_lo, m_hi) * (1.0 / HEAD_DIM)
            return r * (dn - nv * m), dgp

        dqg = jnp.zeros((1, LANES), F32)
        for i in range(nq):
            sl = slice(i * LANES, (i + 1) * LANES)
            dx, dgp = one(x_ref[:, sl], dq_ref[:, sl] * (HEAD_DIM ** -0.5), qg_ref[...])
            o_ref[:, sl] = dx.astype(BF16)
            dqg = dqg + dgp
        dkg = jnp.zeros((1, LANES), F32)
        for i in range(nkp):
            sl = slice(i * LANES, (i + 1) * LANES)
            dx, dgp = one(x_ref[:, qw + i * LANES:qw + (i + 1) * LANES], dk_ref[:, sl], kg_ref[...])
            o_ref[:, qw + i * LANES:qw + (i + 1) * LANES] = dx.astype(BF16)
            dkg = dkg + dgp
            o_ref[:, qw + kw + i * LANES:qw + kw + (i + 1) * LANES] = dv_ref[:, sl].astype(BF16)

        @pl.when(step == 0)
        def _():
            dqg_ref[...] = dqg
            dkg_ref[...] = dkg

        @pl.when(step > 0)
        def _():
            dqg_ref[...] += dqg
            dkg_ref[...] += dkg

        @pl.when(step == nsteps - 1)
        def _():
            plan.finish()

    tab = pl.BlockSpec((ts, LANES), lambda i: (i, 0))
    vec = pl.BlockSpec((1, LANES), lambda i: (0, 0))
    return pl.pallas_call(
        body,
        out_shape=(jax.ShapeDtypeStruct((seq, qw + 2 * kw), BF16), jax.ShapeDtypeStruct((1, LANES), F32),
                   jax.ShapeDtypeStruct((1, LANES), F32), *plan.out_shape),
        grid=(nsteps,),
        in_specs=[pl.BlockSpec((ts, qw + 2 * kw), lambda i: (i, 0)), pl.BlockSpec((ts, qw), lambda i: (i, 0)),
                  pl.BlockSpec((ts, kw), lambda i: (i, 0)), pl.BlockSpec((ts, kw), lambda i: (i, 0)),
                  vec, vec, tab, tab, tab] + [ANY] * nt,
        out_specs=(pl.BlockSpec((ts, qw + 2 * kw), lambda i: (i, 0)), vec, vec, *([ANY] * nt)),
        scratch_shapes=plan.scratch, compiler_params=_cp(("arbitrary",)), name=name)(qkv, dq, dk, dv, qg2, kg2, *tabs, *sends)


def _slot(blk, off0, tq):
    half = lax.broadcasted_iota(jnp.int32, (tq, LANES), 1) // HEAD_DIM
    keep = half == jnp.where(off0, 0, 1)
    parts = []
    for p in range(2):
        pair = blk[:, p * LANES:(p + 1) * LANES].astype(F32)
        rolled = pltpu.roll(pair, HEAD_DIM, axis=1)
        parts.append(jnp.where(keep, jnp.where(off0, pair, rolled), 0.0))
        parts.append(jnp.where(keep, jnp.where(off0, rolled, pair), 0.0))
    return jnp.concatenate(parts, axis=0)


def _unslot(x4, off0, tq):
    lo = lax.broadcasted_iota(jnp.int32, (tq, LANES), 1) < HEAD_DIM
    pairs = []
    for p in range(2):
        h0 = x4[(2 * p) * tq:(2 * p + 1) * tq]
        h1 = x4[(2 * p + 1) * tq:(2 * p + 2) * tq]
        a = jnp.where(off0, h0, pltpu.roll(h0, HEAD_DIM, axis=1))
        b = jnp.where(off0, pltpu.roll(h1, HEAD_DIM, axis=1), h1)
        pairs.append(jnp.where(lo, a, b))
    return jnp.concatenate(pairs, axis=1)


def _flash_fwd(q, k, vt, plan, shards, name):
    seq = q.shape[0]
    tq = _pick(seq, (FLASH_TQ, 128))
    tk = _pick(seq, (FLASH_TK, 2048, 512, 256, 128))
    sub = _pick(tk, (FLASH_SUB, 256, 128))
    nq, nkv, nsub = seq // tq, seq // tk, tk // sub
    gw = 4 * HEAD_DIM
    nt = plan.nt

    def body(q_ref, k_ref, vt_ref, *rest):
        o_ref, lse_ref = rest[nt:nt + 2]
        q4_ref, m_ref, acc_ref, st_ref = rest[2 * nt + 2:2 * nt + 6]
        plan.bind(rest[:nt], rest[nt + 2:2 * nt + 2], *rest[2 * nt + 6:])
        g, qi, ki = pl.program_id(0), pl.program_id(1), pl.program_id(2)
        off0 = (g % 2) == 0
        @pl.when(jnp.logical_and(g == 0, jnp.logical_and(qi == 0, ki == 0)))
        def _():
            plan.start()

        @pl.when(jnp.logical_and(g == N_KV - 1, jnp.logical_and(qi == nq - 1, ki == 0)))
        def _():
            plan.forward()

        @pl.when(ki == 0)
        def _():
            q4_ref[...] = _slot(q_ref[...], off0, tq).astype(BF16)
            m_ref[...] = jnp.full(m_ref.shape, NEG, F32)
            acc_ref[...] = jnp.zeros(acc_ref.shape, F32)

        q4 = q4_ref[...]

        def scores(c):
            st_ref[c % 2] = lax.dot_general(k_ref[c * sub:(c + 1) * sub, :], q4, (((1,), (1,)), ((), ())),
                                            preferred_element_type=F32)

        m, acc = m_ref[...], acc_ref[...]
        scores(0)
        for c in range(nsub):
            if c + 1 < nsub:
                scores(c + 1)
            st = st_ref[c % 2]
            m_new = jnp.maximum(m, jnp.max(st, axis=0, keepdims=True))
            pt = jnp.exp2(st - m_new).astype(BF16)
            acc = jnp.exp2(m - m_new) * acc + jnp.dot(vt_ref[:, c * sub:(c + 1) * sub], pt, preferred_element_type=F32)
            m = m_new
        m_ref[...] = m
        acc_ref[...] = acc

        @pl.when(ki == nkv - 1)
        def _():
            acc = acc_ref[...]
            l = jnp.where(off0, acc[HEAD_DIM:HEAD_DIM + 1], acc[0:1])
            o4 = acc.T
            o4 = o4 / pltpu.roll(o4, HEAD_DIM, axis=1)
            o_ref[...] = _unslot(o4, off0, tq).astype(o_ref.dtype)
            lse_ref[...] = jnp.broadcast_to(m_ref[...] + jnp.log2(l), lse_ref.shape)

        @pl.when(jnp.logical_and(g == N_KV - 1, jnp.logical_and(qi == nq - 1, ki == nkv - 1)))
        def _():
            plan.finish()

    return pl.pallas_call(
        body,
        out_shape=(jax.ShapeDtypeStruct((seq, N_HEADS * HEAD_DIM), BF16),
                   jax.ShapeDtypeStruct((N_KV * nq * 8, 4 * tq), F32), *plan.out_shape),
        grid=(N_KV, nq, nkv),
        in_specs=[pl.BlockSpec((tq, gw), lambda g, qi, ki: (qi, g)),
                  pl.BlockSpec((tk, LANES), lambda g, qi, ki: (ki, g // 2)),
                  pl.BlockSpec((LANES, tk), lambda g, qi, ki: (g, ki))] + [ANY] * nt,
        out_specs=(pl.BlockSpec((tq, gw), lambda g, qi, ki: (qi, g)),
                   pl.BlockSpec((8, 4 * tq), lambda g, qi, ki: (g * nq + qi, 0)), *([ANY] * nt)),
        scratch_shapes=[pltpu.VMEM((4 * tq, LANES), BF16), pltpu.VMEM((1, 4 * tq), F32),
                        pltpu.VMEM((LANES, 4 * tq), F32), pltpu.VMEM((2, sub, 4 * tq), F32)] + plan.scratch,
        compiler_params=_cp(("arbitrary", "arbitrary", "arbitrary")), name=name)(q, k, vt, *shards)


def _flash_bwd(q, k, kt, v, do, o, lse, plan, grads, name):
    seq = q.shape[0]
    tq = _pick(seq, (FLASH_TQ, 128))
    tk = _pick(seq, (FLASH_TK, 2048, 512, 256, 128))
    sub = _pick(tk, (FLASH_SUB, 256, 128))
    nq, nkv, nsub = seq // tq, seq // tk, tk // sub
    gw = 4 * HEAD_DIM
    nt = plan.nt

    def body(q_ref, k_ref, kt_ref, v_ref, do_ref, o_ref, lse_ref, *rest):
        dq_ref, dk_ref, dv_ref = rest[nt:nt + 3]
        q4_ref, do4_ref, delta_ref, dqt_ref, st_ref, dpt_ref = rest[2 * nt + 3:2 * nt + 9]
        plan.bind(rest[:nt], rest[nt + 3:2 * nt + 3], *rest[2 * nt + 9:])
        g, qi, ki = pl.program_id(0), pl.program_id(1), pl.program_id(2)
        off0 = (g % 2) == 0

        @pl.when(jnp.logical_and(g == 0, jnp.logical_and(qi == 0, ki == 0)))
        def _():
            plan.start()

        @pl.when(jnp.logical_and(g % 2 == 0, jnp.logical_and(qi == 0, ki == 0)))
        def _():
            dk_ref[...] = jnp.zeros(dk_ref.shape, F32)
            dv_ref[...] = jnp.zeros(dv_ref.shape, F32)

        @pl.when(ki == 0)
        def _():
            q4_ref[...] = _slot(q_ref[...], off0, tq).astype(BF16)
            do4 = _slot(do_ref[...], off0, tq)
            do4_ref[...] = do4.astype(BF16)
            o4 = _slot(o_ref[...], off0, tq)
            delta_ref[...] = jnp.sum((do4 * o4).T, axis=0, keepdims=True)
            dqt_ref[...] = jnp.zeros(dqt_ref.shape, F32)

        q4, do4 = q4_ref[...], do4_ref[...]
        lse_row, delta = lse_ref[0:1, :], delta_ref[...]

        def products(c):
            rows = slice(c * sub, (c + 1) * sub)
            st_ref[c % 2] = lax.dot_general(k_ref[rows, :], q4, (((1,), (1,)), ((), ())), preferred_element_type=F32)
            dpt_ref[c % 2] = lax.dot_general(v_ref[rows, :], do4, (((1,), (1,)), ((), ())), preferred_element_type=F32)

        dqt = dqt_ref[...]
        products(0)
        for c in range(nsub):
            if c + 1 < nsub:
                products(c + 1)
            pt = jnp.exp2(st_ref[c % 2] - lse_row)
            dst = (pt * (dpt_ref[c % 2] - delta)).astype(BF16)
            rows = pl.ds(pl.multiple_of(ki * tk + c * sub, sub), sub)
            dv_ref[rows, :] += jnp.dot(pt.astype(BF16), do4, preferred_element_type=F32)
            dk_ref[rows, :] += jnp.dot(dst, q4, preferred_element_type=F32) * (1.0 / LOG2E)
            dqt = dqt + jnp.dot(kt_ref[:, c * sub:(c + 1) * sub], dst, preferred_element_type=F32)
        dqt_ref[...] = dqt

        @pl.when(ki == nkv - 1)
        def _():
            dq_ref[...] = _unslot(dqt_ref[...].T, off0, tq)

        @pl.when(jnp.logical_and(g == N_KV - 1, jnp.logical_and(qi == nq - 1, ki == nkv - 1)))
        def _():
            plan.finish()

    return pl.pallas_call(
        body,
        out_shape=(jax.ShapeDtypeStruct((seq, N_HEADS * HEAD_DIM), F32),
                   jax.ShapeDtypeStruct((seq, N_KV * HEAD_DIM), F32), jax.ShapeDtypeStruct((seq, N_KV * HEAD_DIM), F32),
                   *plan.out_shape),
        grid=(N_KV, nq, nkv),
        in_specs=[pl.BlockSpec((tq, gw), lambda g, qi, ki: (qi, g)),
                  pl.BlockSpec((tk, LANES), lambda g, qi, ki: (ki, g // 2)),
                  pl.BlockSpec((LANES, tk), lambda g, qi, ki: (g // 2, ki)),
                  pl.BlockSpec((tk, LANES), lambda g, qi, ki: (ki, g // 2)),
                  pl.BlockSpec((tq, gw), lambda g, qi, ki: (qi, g)),
                  pl.BlockSpec((tq, gw), lambda g, qi, ki: (qi, g)),
                  pl.BlockSpec((8, 4 * tq), lambda g, qi, ki: (g * nq + qi, 0))] + [ANY] * nt,
        out_specs=(pl.BlockSpec((tq, gw), lambda g, qi, ki: (qi, g)),
                   pl.BlockSpec((seq, LANES), lambda g, qi, ki: (0, g // 2)),
                   pl.BlockSpec((seq, LANES), lambda g, qi, ki: (0, g // 2)), *([ANY] * nt)),
        scratch_shapes=[pltpu.VMEM((4 * tq, LANES), BF16), pltpu.VMEM((4 * tq, LANES), BF16),
                        pltpu.VMEM((1, 4 * tq), F32), pltpu.VMEM((LANES, 4 * tq), F32),
                        pltpu.VMEM((2, sub, 4 * tq), F32), pltpu.VMEM((2, sub, 4 * tq), F32)] + plan.scratch,
        compiler_params=_cp(("arbitrary", "arbitrary", "arbitrary")), name=name)(q, k, kt, v, do, o, lse, *grads)


def _xattn_fwd(q, kv, name):
    seq, d = q.shape
    mlen = kv.shape[0]
    tq = _pick(seq, (512, 256))

    def body(q_ref, k_ref, v_ref, o_ref):
        for h in range(X_HEADS):
            sl = slice(h * X_HEAD_DIM, (h + 1) * X_HEAD_DIM)
            s = lax.dot_general(q_ref[:, sl], k_ref[:, sl], (((1,), (1,)), ((), ())), preferred_element_type=F32)
            e = jnp.exp(s - jnp.max(s, axis=-1, keepdims=True))
            p = e / jnp.sum(e, axis=-1, keepdims=True)
            o_ref[:, sl] = jnp.dot(p.astype(BF16), v_ref[:, sl], preferred_element_type=F32).astype(o_ref.dtype)

    return pl.pallas_call(
        body, out_shape=jax.ShapeDtypeStruct((seq, d), BF16), grid=(seq // tq,),
        in_specs=[pl.BlockSpec((tq, d), lambda i: (i, 0)), pl.BlockSpec((mlen, d), lambda i: (0, 0)),
                  pl.BlockSpec((mlen, d), lambda i: (0, 1))],
        out_specs=pl.BlockSpec((tq, d), lambda i: (i, 0)), compiler_params=_cp(("parallel",)), name=name)(q, kv, kv)


def _xattn_bwd(q, kv, do, name):
    seq, d = q.shape
    mlen = kv.shape[0]
    tq = _pick(seq, (512, 256))
    scale = X_HEAD_DIM ** -0.5

    def body(q_ref, k_ref, v_ref, do_ref, dq_ref, dkv_ref):
        i = pl.program_id(0)

        @pl.when(i == 0)
        def _():
            dkv_ref[...] = jnp.zeros(dkv_ref.shape, F32)

        for h in range(X_HEADS):
            sl = slice(h * X_HEAD_DIM, (h + 1) * X_HEAD_DIM)
            qh, kh, vh = q_ref[:, sl], k_ref[:, sl], v_ref[:, sl]
            doh = do_ref[:, sl].astype(BF16)
            st = lax.dot_general(kh, qh, (((1,), (1,)), ((), ())), preferred_element_type=F32)
            e = jnp.exp(st - jnp.max(st, axis=0, keepdims=True))
            pt = e / jnp.sum(e, axis=0, keepdims=True)
            dpt = lax.dot_general(vh, doh, (((1,), (1,)), ((), ())), preferred_element_type=F32)
            dst = (pt * (dpt - jnp.sum(pt * dpt, axis=0, keepdims=True))).astype(BF16)
            dkv_ref[:, sl] += jnp.dot(dst, qh, preferred_element_type=F32)
            dkv_ref[:, d + h * X_HEAD_DIM:d + (h + 1) * X_HEAD_DIM] += jnp.dot(pt.astype(BF16), doh,
                                                                                 preferred_element_type=F32)
            dqh = lax.dot_general(dst, kh, (((0,), (0,)), ((), ())), preferred_element_type=F32)
            dq_ref[:, sl] = (dqh * scale).astype(dq_ref.dtype)

    return pl.pallas_call(
        body, out_shape=(jax.ShapeDtypeStruct((seq, d), BF16), jax.ShapeDtypeStruct((mlen, 2 * d), F32)),
        grid=(seq // tq,),
        in_specs=[pl.BlockSpec((tq, d), lambda i: (i, 0)), pl.BlockSpec((mlen, d), lambda i: (0, 0)),
                  pl.BlockSpec((mlen, d), lambda i: (0, 1)), pl.BlockSpec((tq, d), lambda i: (i, 0))],
        out_specs=(pl.BlockSpec((tq, d), lambda i: (i, 0)), pl.BlockSpec((mlen, 2 * d), lambda i: (0, 0))),
        compiler_params=_cp(("arbitrary",)), name=name)(q, kv, kv, do)


def _halo_specs(tr, tc, seq, col):
    per, last = tr // HALO, seq // HALO - 1
    return [pl.BlockSpec((tr, tc), lambda j, r: (r, col(j))),
            pl.BlockSpec((HALO, tc), lambda j, r: (jnp.maximum(r * per - 1, 0), col(j))),
            pl.BlockSpec((HALO, tc), lambda j, r: (jnp.minimum((r + 1) * per, last), col(j)))]


def _extend(main_ref, prev_ref, next_ref, r, nr):
    pv = (r > 0).astype(F32)
    nv = (r < nr - 1).astype(F32)
    return jnp.concatenate([prev_ref[...].astype(F32) * pv, main_ref[...].astype(F32),
                            next_ref[...].astype(F32) * nv], axis=0)


def _conv3(e, w_ref, n):
    return pltpu.roll(e, 1, axis=0) * w_ref[0:1, :] + e * w_ref[1:2, :] + pltpu.roll(e, n - 1, axis=0) * w_ref[2:3, :]


def _conv_gate_fwd(ug, uv, cw, cb, layer, name, plan=None, shards=(), fulls=()):
    seq, f = ug.shape
    tc = 256
    tr = _pick(seq, (512, 256))
    nc, nr = f // tc, seq // tr
    n = tr + 2 * HALO
    nt = plan.nt if plan is not None else 0

    def body(g_ref, gp_ref, gn_ref, v_ref, vp_ref, vn_ref, wg_ref, wv_ref, bg_ref, bv_ref, *rest):
        o_ref = rest[2 * nt]
        j, r = pl.program_id(0), pl.program_id(1)
        if plan is not None:
            plan.bind(rest[:nt], rest[2 * nt + 1:3 * nt + 1], *rest[3 * nt + 1:])

            @pl.when(jnp.logical_and(j == 0, r == 0))
            def _():
                plan.start()

        cg = _conv3(_extend(g_ref, gp_ref, gn_ref, r, nr), wg_ref, n)[HALO:HALO + tr] + bg_ref[...]
        cv = _conv3(_extend(v_ref, vp_ref, vn_ref, r, nr), wv_ref, n)[HALO:HALO + tr] + bv_ref[...]
        o_ref[...] = (cg * jax.nn.sigmoid(cg) * cv).astype(o_ref.dtype)

        if plan is not None:
            @pl.when(jnp.logical_and(j == nc - 1, r == nr - 1))
            def _():
                plan.forward()
                plan.finish()

    w_spec = lambda shift: pl.BlockSpec((None, 3, tc), lambda j, r: (layer, 0, j + shift))
    b_spec = lambda shift: pl.BlockSpec((None, 1, tc), lambda j, r: (layer, 0, j + shift))
    act_shape = jax.ShapeDtypeStruct((seq, f), BF16)
    act_spec = pl.BlockSpec((tr, tc), lambda j, r: (r, j))
    in_specs = _halo_specs(tr, tc, seq, lambda j: j) * 2 + [w_spec(0), w_spec(nc), b_spec(0), b_spec(nc)]
    operands = (ug, ug, ug, uv, uv, uv, cw, cw, cb, cb)
    if plan is None:
        return pl.pallas_call(body, out_shape=act_shape, grid=(nc, nr), in_specs=in_specs, out_specs=act_spec,
                              compiler_params=_cp(("parallel", "parallel")), name=name)(*operands)
    return pl.pallas_call(
        body, out_shape=(act_shape, *plan.out_shape), grid=(nc, nr), in_specs=in_specs + [ANY] * (2 * nt),
        out_specs=(act_spec, *([ANY] * nt)), scratch_shapes=plan.scratch,
        input_output_aliases={len(operands) + nt + t: 1 + t for t in range(nt)},
        compiler_params=_cp(("arbitrary", "arbitrary")), name=name)(*operands, *shards, *fulls)


def _conv_gate_bwd(ug, uv, dact, cw, cb, layer, name):
    seq, f = ug.shape
    tc = 256
    tr = _pick(seq, (512, 256))
    nc, nr = f // tc, seq // tr
    n = tr + 2 * HALO

    def body(g_ref, gp_ref, gn_ref, v_ref, vp_ref, vn_ref, d_ref, dp_ref, dn_ref, wg_ref, wv_ref, bg_ref, bv_ref,
             dug_ref, duv_ref, dwg_ref, dwv_ref):
        r = pl.program_id(1)
        eg = _extend(g_ref, gp_ref, gn_ref, r, nr)
        ev = _extend(v_ref, vp_ref, vn_ref, r, nr)
        da = _extend(d_ref, dp_ref, dn_ref, r, nr)
        eg3 = (pltpu.roll(eg, 1, axis=0), eg, pltpu.roll(eg, n - 1, axis=0))
        ev3 = (pltpu.roll(ev, 1, axis=0), ev, pltpu.roll(ev, n - 1, axis=0))
        cg = eg3[0] * wg_ref[0:1, :] + eg3[1] * wg_ref[1:2, :] + eg3[2] * wg_ref[2:3, :] + bg_ref[...]
        cv = ev3[0] * wv_ref[0:1, :] + ev3[1] * wv_ref[1:2, :] + ev3[2] * wv_ref[2:3, :] + bv_ref[...]
        sg = jax.nn.sigmoid(cg)
        dcv = da * (cg * sg)
        dcg = da * cv * (sg * (1.0 + cg * (1.0 - sg)))

        def back(dc, e3, w_ref, du_ref, dw_ref):
            du = (pltpu.roll(dc, n - 1, axis=0) * w_ref[0:1, :] + dc * w_ref[1:2, :]
                  + pltpu.roll(dc, 1, axis=0) * w_ref[2:3, :])
            du_ref[...] = du[HALO:HALO + tr].astype(du_ref.dtype)
            dcm = dc[HALO:HALO + tr]
            taps = [jnp.sum(dcm * e[HALO:HALO + tr], axis=0, keepdims=True) for e in e3] + [
                    jnp.sum(dcm, axis=0, keepdims=True)]
            part = jnp.concatenate(taps + [jnp.zeros((4, tc), F32)], axis=0)

            @pl.when(r == 0)
            def _():
                dw_ref[...] = part

            @pl.when(r > 0)
            def _():
                dw_ref[...] += part

        back(dcg, eg3, wg_ref, dug_ref, dwg_ref)
        back(dcv, ev3, wv_ref, duv_ref, dwv_ref)

    w_spec = lambda shift: pl.BlockSpec((None, 3, tc), lambda j, r: (layer, 0, j + shift))
    b_spec = lambda shift: pl.BlockSpec((None, 1, tc), lambda j, r: (layer, 0, j + shift))
    out_rows = pl.BlockSpec((tr, tc), lambda j, r: (r, j))
    out_acc = pl.BlockSpec((8, tc), lambda j, r: (0, j))
    return pl.pallas_call(
        body,
        out_shape=(jax.ShapeDtypeStruct((seq, f), BF16), jax.ShapeDtypeStruct((seq, f), BF16),
                   jax.ShapeDtypeStruct((8, f), F32), jax.ShapeDtypeStruct((8, f), F32)),
        grid=(nc, nr),
        in_specs=_halo_specs(tr, tc, seq, lambda j: j) * 3 + [w_spec(0), w_spec(nc), b_spec(0), b_spec(nc)],
        out_specs=(out_rows, out_rows, out_acc, out_acc),
        compiler_params=_cp(("parallel", "arbitrary")), name=name)(ug, ug, ug, uv, uv, uv, dact, dact, dact, cw, cw, cb, cb)


def _pool_count(g, r, tr, n, seq):
    half = jnp.left_shift(1, g)
    t = r * tr - HALO + lax.broadcasted_iota(jnp.int32, (n, 1), 0)
    cnt = jnp.minimum(t + half, seq) - jnp.maximum(t - half, 0)
    return jnp.maximum(cnt, 1).astype(F32)


def _by_group(g, levels):
    out = levels[3]
    for i in (2, 1, 0):
        out = jnp.where(g == i, levels[i], out)
    return out


def _pool_mixed(e, g, cnt, n):
    w2 = e + pltpu.roll(e, 1, axis=0)
    w4 = pltpu.roll(w2, 1, axis=0) + pltpu.roll(w2, n - 1, axis=0)
    w8 = pltpu.roll(w4, 2, axis=0) + pltpu.roll(w4, n - 2, axis=0)
    w16 = pltpu.roll(w8, 4, axis=0) + pltpu.roll(w8, n - 4, axis=0)
    return _by_group(g, (w2, w4, w8, w16)) / cnt - e


def _pool_fwd(hp, xres, pw, scale, name):
    seq, d = hp.shape
    tc = POOL_GROUP_W
    tr = _pick(seq, (512, 256))
    nr = seq // tr
    n = tr + 2 * HALO

    def body(h_ref, hp_ref, hn_ref, x_ref, w_ref, s_ref, o_ref):
        g, r = pl.program_id(0), pl.program_id(1)
        e = _extend(h_ref, hp_ref, hn_ref, r, nr)
        mixed = _pool_mixed(e, g, _pool_count(g, r, tr, n, seq), n)[HALO:HALO + tr]
        y = jnp.dot(mixed.astype(BF16), w_ref[...], preferred_element_type=F32)
        o_ref[...] = x_ref[...] + y * s_ref[...]

    return pl.pallas_call(
        body, out_shape=jax.ShapeDtypeStruct((seq, d), F32), grid=(POOL_GROUPS, nr),
        in_specs=_halo_specs(tr, tc, seq, lambda j: j) + [
            pl.BlockSpec((tr, tc), lambda j, r: (r, j)), pl.BlockSpec((None, tc, tc), lambda j, r: (j, 0, 0)),
            pl.BlockSpec((1, tc), lambda j, r: (0, j))],
        out_specs=pl.BlockSpec((tr, tc), lambda j, r: (r, j)),
        compiler_params=_cp(("parallel", "parallel")), name=name)(hp, hp, hp, xres, pw, scale)


def _pool_bwd(hp, dy, pw, scale, name):
    seq, d = hp.shape
    tc = POOL_GROUP_W
    tr = _pick(seq, (512, 256))
    nr = seq // tr
    n = tr + 2 * HALO

    def body(h_ref, hp_ref, hn_ref, d_ref, dp_ref, dn_ref, w_ref, s_ref, dh_ref, dw_ref, ds_ref):
        g, r = pl.program_id(0), pl.program_id(1)
        cnt = _pool_count(g, r, tr, n, seq)
        e = _extend(h_ref, hp_ref, hn_ref, r, nr)
        mixed = _pool_mixed(e, g, cnt, n)[HALO:HALO + tr].astype(BF16)
        dye = _extend(d_ref, dp_ref, dn_ref, r, nr)
        dyp = (dye * s_ref[...]).astype(BF16)
        dmixed = lax.dot_general(dyp, w_ref[...], (((1,), (1,)), ((), ())), preferred_element_type=F32)
        dwin = dmixed / cnt
        m2 = dwin + pltpu.roll(dwin, n - 1, axis=0)
        m4 = pltpu.roll(m2, 1, axis=0) + pltpu.roll(m2, n - 1, axis=0)
        m8 = pltpu.roll(m4, 2, axis=0) + pltpu.roll(m4, n - 2, axis=0)
        m16 = pltpu.roll(m8, 4, axis=0) + pltpu.roll(m8, n - 4, axis=0)
        dh_ref[...] = (_by_group(g, (m2, m4, m8, m16)) - dmixed)[HALO:HALO + tr]
        ypre = jnp.dot(mixed, w_ref[...], preferred_element_type=F32)
        dsp = jnp.sum(d_ref[...] * ypre, axis=0, keepdims=True)
        dwp = lax.dot_general(mixed, dyp[HALO:HALO + tr], (((0,), (0,)), ((), ())), preferred_element_type=F32)

        @pl.when(r == 0)
        def _():
            dw_ref[...] = dwp
            ds_ref[...] = dsp

        @pl.when(r > 0)
        def _():
            dw_ref[...] += dwp
            ds_ref[...] += dsp

    return pl.pallas_call(
        body,
        out_shape=(jax.ShapeDtypeStruct((seq, d), F32), jax.ShapeDtypeStruct((POOL_GROUPS, tc, tc), F32),
                   jax.ShapeDtypeStruct((1, d), F32)),
        grid=(POOL_GROUPS, nr),
        in_specs=_halo_specs(tr, tc, seq, lambda j: j) * 2 + [
            pl.BlockSpec((None, tc, tc), lambda j, r: (j, 0, 0)), pl.BlockSpec((1, tc), lambda j, r: (0, j))],
        out_specs=(pl.BlockSpec((tr, tc), lambda j, r: (r, j)), pl.BlockSpec((None, tc, tc), lambda j, r: (j, 0, 0)),
                   pl.BlockSpec((1, tc), lambda j, r: (0, j))),
        compiler_params=_cp(("parallel", "arbitrary")), name=name)(hp, hp, hp, dy, dy, dy, pw, scale)


def _adamw_math(w, g, m, v):
    m = ADAM_B1 * m + (1.0 - ADAM_B1) * g
    v = ADAM_B2 * v + (1.0 - ADAM_B2) * (g * g)
    m_hat = m / (1.0 - ADAM_B1 ** ADAM_STEP)
    v_hat = v / (1.0 - ADAM_B2 ** ADAM_STEP)
    delta = -ADAM_LR * (m_hat / (jnp.sqrt(v_hat) + ADAM_EPS) + ADAM_WD * w)
    return delta, m, v


def _adamw(w, ga, gb, m, v, name, plan=None, sends=()):
    rows, cols = w.shape
    tr = _pick(rows, (256, 128, 64, 32, 16, 8))
    two = gb is not None
    nin = 5 if two else 4
    nt, nsteps = (plan.nt if plan is not None else 0), rows // tr

    def body(*refs):
        if two:
            w_ref, ga_ref, gb_ref, m_ref, v_ref = refs[:5]
            g = ga_ref[...] + gb_ref[...]
        else:
            w_ref, ga_ref, m_ref, v_ref = refs[:4]
            g = ga_ref[...]
        g_out, d_out, m_out, v_out = refs[nin + nt:nin + nt + 4]
        if plan is not None:
            plan.bind(refs[nin:nin + nt], refs[nin + nt + 4:nin + 2 * nt + 4], *refs[nin + 2 * nt + 4:])

            @pl.when(pl.program_id(0) == 0)
            def _():
                plan.start()

        delta, m, v = _adamw_math(w_ref[...], g, m_ref[...], v_ref[...])
        g_out[...] = g
        d_out[...] = delta
        m_out[...] = m
        v_out[...] = v

        if plan is not None:
            @pl.when(pl.program_id(0) == nsteps - 1)
            def _():
                plan.finish()

    spec = pl.BlockSpec((tr, cols), lambda i: (i, 0))
    ops = [w, ga] + ([gb] if two else []) + [m, v]
    outs = tuple(jax.ShapeDtypeStruct((rows, cols), F32) for _ in range(4))
    if plan is None:
        return pl.pallas_call(body, out_shape=outs, grid=(nsteps,), in_specs=[spec] * nin, out_specs=(spec,) * 4,
                              compiler_params=_cp(("parallel",)), name=name)(*ops)
    return pl.pallas_call(
        body, out_shape=outs + tuple(plan.out_shape), grid=(nsteps,), in_specs=[spec] * nin + [ANY] * nt,
        out_specs=(spec,) * 4 + tuple([ANY] * nt), scratch_shapes=plan.scratch, compiler_params=_cp(("arbitrary",)),
        name=name)(*ops, *sends)


def _sum4(parts, name):
    _, rows, cols = parts.shape
    tr = _pick(rows, (256, 128, 64, 32, 16))

    def body(p_ref, o_ref):
        acc = p_ref[0].astype(F32)
        for kk in range(1, 4):
            acc = acc + p_ref[kk].astype(F32)
        o_ref[...] = acc

    return pl.pallas_call(
        body, out_shape=jax.ShapeDtypeStruct((rows, cols), F32), grid=(rows // tr,),
        in_specs=[pl.BlockSpec((4, tr, cols), lambda i: (0, i, 0))], out_specs=pl.BlockSpec((tr, cols), lambda i: (i, 0)),
        compiler_params=_cp(("parallel",)), name=name)(parts)


def _place():
    x, y, c = lax.axis_index("x"), lax.axis_index("y"), lax.axis_index("c")
    chips = [(1 - x, y), (x, 1 - y), (1 - x, 1 - y)]
    return x, y, c, chips


def _window(ref, axis, j, size, c=None, half=None, lead=(), layers=slice(None)):
    if axis == "r":
        if c is None:
            return ref.at[lead + (layers, pl.ds(pl.multiple_of(j * size, 32), size), slice(None))]
        return ref.at[lead + (layers, pl.ds(pl.multiple_of(j * size + c * half, 32), half), slice(None))]
    cols = pl.ds(pl.multiple_of(j * size, LANES), size)
    if c is None:
        return ref.at[lead + (layers, slice(None), cols)]
    return ref.at[lead + (layers, pl.ds(pl.multiple_of(c * half, 32), half), cols)]


class _Gather:
    def __init__(self, shards, axes, layers=None):
        self.nt, self.axes = len(shards), axes
        self.layers = layers or [slice(None)] * self.nt
        self.out_shape, self.sizes, self.halves = [], [], []
        for s, ax in zip(shards, axes):
            l, rs, cs = s.shape
            self.out_shape.append(jax.ShapeDtypeStruct((l, 4 * rs, cs) if ax == "r" else (l, rs, 4 * cs), s.dtype))
            self.sizes.append(rs if ax == "r" else cs)
            self.halves.append(rs // 2)
        self.scratch = [pltpu.SemaphoreType.DMA((6 * self.nt,)), pltpu.SemaphoreType.DMA((6 * self.nt,)),
                        pltpu.SemaphoreType.DMA((self.nt,))]

    def bind(self, src, dst, send_sems, recv_sems, local_sems):
        self.src, self.dst, self.send_sems, self.recv_sems, self.local_sems = src, dst, send_sems, recv_sems, local_sems

    def _win(self, t, j, core=None):
        return _window(self.dst[t], self.axes[t], j, self.sizes[t], core, self.halves[t], layers=self.layers[t])

    def _ici(self, t, kk, origin):
        _, _, c, chips = _place()
        px, py = chips[kk]
        half = self.src[t].at[self.layers[t], pl.ds(pl.multiple_of(c * self.halves[t], 16), self.halves[t]), :]
        return pltpu.make_async_remote_copy(
            src_ref=half, dst_ref=self._win(t, origin, c), send_sem=self.send_sems.at[t * 3 + kk],
            recv_sem=self.recv_sems.at[t * 3 + kk], device_id=(px, py, c), device_id_type=MESH)

    def _d2d(self, t, kk, origin, core):
        x, y, c, _ = _place()
        k2 = 3 * self.nt + t * 3 + kk
        return pltpu.make_async_remote_copy(
            src_ref=self._win(t, origin, core), dst_ref=self._win(t, origin, core), send_sem=self.send_sems.at[k2],
            recv_sem=self.recv_sems.at[k2], device_id=(x, y, 1 - c), device_id_type=MESH)

    def _local(self, t):
        x, y, _, _ = _place()
        return pltpu.make_async_copy(self.src[t].at[self.layers[t]], self._win(t, 2 * x + y), self.local_sems.at[t])

    def _each(self):
        _, _, _, chips = _place()
        for t in range(self.nt):
            for kk in range(3):
                px, py = chips[kk]
                yield t, kk, 2 * px + py

    def start(self):
        x, y, _, _ = _place()
        for t in range(self.nt):
            self._local(t).start()
        for t, kk, _ in self._each():
            self._ici(t, kk, 2 * x + y).start()

    def forward(self):
        _, _, c, _ = _place()
        for t, kk, origin in self._each():
            self._ici(t, kk, origin).wait_recv()
            self._d2d(t, kk, origin, c).start()

    def finish(self):
        x, y, c, _ = _place()
        for t, kk, origin in self._each():
            self._d2d(t, kk, origin, 1 - c).wait_recv()
        for t, kk, origin in self._each():
            self._ici(t, kk, 2 * x + y).wait_send()
            self._d2d(t, kk, origin, c).wait_send()
        for t in range(self.nt):
            self._local(t).wait()


class _Scatter:
    def __init__(self, grads, axes):
        self.nt, self.axes = len(grads), axes
        self.out_shape, self.sizes = [], []
        for gr, ax in zip(grads, axes):
            l, r, cc = gr.shape
            self.out_shape.append(jax.ShapeDtypeStruct((4, l, r // 4, cc) if ax == "r" else (4, l, r, cc // 4), gr.dtype))
            self.sizes.append(r // 4 if ax == "r" else cc // 4)
        self.scratch = [pltpu.SemaphoreType.DMA((3 * self.nt,)), pltpu.SemaphoreType.DMA((3 * self.nt,)),
                        pltpu.SemaphoreType.DMA((self.nt,))]

    def bind(self, src, dst, send_sems, recv_sems, local_sems):
        self.src, self.dst, self.send_sems, self.recv_sems, self.local_sems = src, dst, send_sems, recv_sems, local_sems

    def _copy(self, t, kk, slot):
        x, y, c, chips = _place()
        px, py = chips[kk]
        return pltpu.make_async_remote_copy(
            src_ref=_window(self.src[t], self.axes[t], 2 * px + py, self.sizes[t]), dst_ref=self.dst[t].at[slot],
            send_sem=self.send_sems.at[t * 3 + kk], recv_sem=self.recv_sems.at[t * 3 + kk],
            device_id=(px, py, c), device_id_type=MESH)

    def _local(self, t):
        x, y, _, _ = _place()
        me = 2 * x + y
        return pltpu.make_async_copy(_window(self.src[t], self.axes[t], me, self.sizes[t]), self.dst[t].at[me],
                                     self.local_sems.at[t])

    def start(self):
        x, y, _, _ = _place()
        for t in range(self.nt):
            self._local(t).start()
            for kk in range(3):
                self._copy(t, kk, 2 * x + y).start()

    def finish(self):
        _, _, _, chips = _place()
        for t in range(self.nt):
            for kk in range(3):
                px, py = chips[kk]
                self._copy(t, kk, 2 * px + py).wait_recv()
        for t in range(self.nt):
            for kk in range(3):
                px, py = chips[kk]
                self._copy(t, kk, 2 * px + py).wait_send()
            self._local(t).wait()


def _comm_call(plan, operands, name):
    nt = plan.nt

    def body(*refs):
        plan.bind(refs[:nt], refs[nt:2 * nt], *refs[2 * nt:])
        plan.start()
        if hasattr(plan, "forward"):
            plan.forward()
        plan.finish()

    return pl.pallas_call(body, out_shape=tuple(plan.out_shape), in_specs=[ANY] * nt, out_specs=tuple([ANY] * nt),
                          scratch_shapes=plan.scratch, name=name)(*operands)


class _Swap:
    def __init__(self, arrs):
        self.nt = len(arrs)
        self.out_shape = [jax.ShapeDtypeStruct(a.shape, a.dtype) for a in arrs]
        self.scratch = [pltpu.SemaphoreType.DMA((self.nt,)), pltpu.SemaphoreType.DMA((self.nt,))]

    def bind(self, src, dst, send_sems, recv_sems):
        self.src, self.dst, self.send_sems, self.recv_sems = src, dst, send_sems, recv_sems

    def _copy(self, t):
        x, y, c, _ = _place()
        return pltpu.make_async_remote_copy(src_ref=self.src[t], dst_ref=self.dst[t], send_sem=self.send_sems.at[t],
                                            recv_sem=self.recv_sems.at[t], device_id=(x, y, 1 - c), device_id_type=MESH)

    def start(self):
        for t in range(self.nt):
            self._copy(t).start()

    def finish(self):
        for t in range(self.nt):
            self._copy(t).wait()


def _gather8(pack, with_sum, name):
    rows = pack.shape[0]
    flips = [f for f in itertools.product((0, 1), repeat=3) if any(f)]

    def body(p_ref, all_ref, *rest):
        if with_sum:
            sum_ref, send_sems, recv_sems = rest
        else:
            send_sems, recv_sems = rest
        x, y, c, _ = _place()
        me = 4 * x + 2 * y + c

        def peer(f):
            return tuple(1 - v if fl else v for v, fl in zip((x, y, c), f))

        all_ref[me] = p_ref[...]
        cps = []
        for kk, f in enumerate(flips):
            cp = pltpu.make_async_remote_copy(src_ref=p_ref, dst_ref=all_ref.at[me], send_sem=send_sems.at[kk],
                                              recv_sem=recv_sems.at[kk], device_id=peer(f), device_id_type=MESH)
            cp.start()
            cps.append(cp)
        for kk, f in enumerate(flips):
            px, py, pc = peer(f)
            pltpu.make_async_remote_copy(src_ref=p_ref, dst_ref=all_ref.at[4 * px + 2 * py + pc],
                                         send_sem=send_sems.at[kk], recv_sem=recv_sems.at[kk], device_id=peer(f),
                                         device_id_type=MESH).wait_recv()
        for cp in cps:
            cp.wait_send()
        if with_sum:
            acc = all_ref[0]
            for d in range(1, 8):
                acc = acc + all_ref[d]
            sum_ref[...] = acc

    vm = pl.BlockSpec(memory_space=pltpu.VMEM)
    out_shape = [jax.ShapeDtypeStruct((8, rows, LANES), F32)] + ([jax.ShapeDtypeStruct((rows, LANES), F32)] if with_sum else [])
    return pl.pallas_call(
        body, out_shape=tuple(out_shape), in_specs=[vm], out_specs=tuple([vm] * len(out_shape)),
        scratch_shapes=[pltpu.SemaphoreType.DMA((7,)), pltpu.SemaphoreType.DMA((7,))], name=name)(pack)


def _pack(arrs):
    flat = jnp.concatenate([a.reshape(-1).astype(F32) for a in arrs])
    rows = -(-flat.shape[0] // (8 * LANES)) * 8
    return jnp.pad(flat, (0, rows * LANES - flat.shape[0])).reshape(rows, LANES)


def _unpack(flat, shapes):
    out, pos = [], 0
    for shp in shapes:
        size = 1
        for s in shp:
            size *= s
        out.append(flat[pos:pos + size].reshape(shp))
        pos += size
    return out


BIG = ("attn_w_qkv", "attn_w_o", "pool_w", "xattn_w_q", "xattn_w_kv", "xattn_w_o", "ffn_w_up", "ffn_w_down")
BIG_AXIS = ("c", "r", "r", "r", "c", "r", "c", "r")
SMALL_REPL = ("attn_norm", "attn_q_gain", "attn_k_gain", "xattn_norm", "mem_norm", "ffn_norm", "ffn_conv_b", "final_norm")
SMALL_SHARD = ("pool_norm", "pool_scale", "ffn_conv_w")
ORDER = ("attn_norm", "attn_w_qkv", "attn_q_gain", "attn_k_gain", "attn_w_o", "pool_norm", "pool_w", "pool_scale",
         "xattn_norm", "mem_norm", "xattn_w_q", "xattn_w_kv", "xattn_w_o", "ffn_norm", "ffn_w_up", "ffn_conv_w",
         "ffn_conv_b", "ffn_w_down", "final_norm")


def _step(x, mem, tgt, w, m, v):
    seq, d = x.shape
    xi, yi, ci = lax.axis_index("x"), lax.axis_index("y"), lax.axis_index("c")
    chip = 2 * xi + yi
    dff = w["ffn_w_down"].shape[1] * 4
    n_layers = w["ffn_norm"].shape[0]

    def as3d(a):
        return a.reshape(a.shape[-3:])
    shards = [as3d(w[nm]).astype(BF16) for nm in BIG]
    (wq,) = _comm_call(_Gather(shards[:1], BIG_AXIS[:1]), shards[:1], "gather_qkv")
    small_in = [w[nm] for nm in SMALL_SHARD]
    (small_all,) = _gather8(_pack(small_in), False, "gather_small")
    per_chip = [_unpack(small_all[2 * j].reshape(-1), [a.shape for a in small_in]) for j in range(4)]
    pool_norm, pool_scale, conv_w = (jnp.concatenate([per_chip[j][i] for j in range(4)], axis=-1) for i in range(3))

    conv_b = w["ffn_conv_b"].reshape(n_layers, 1, -1)
    tabs = _rope_tables(seq)
    qg2 = jnp.tile(w["attn_q_gain"], (1, 2))
    kg2 = jnp.tile(w["attn_k_gain"], (1, 2))
    mm = functools.partial(_mm)

    saved = {}
    x0 = x
    h0 = _rms_fwd(x0, w["attn_norm"], BF16, "rms_attn")
    qkv = mm(h0, wq, "nn", b_l=0, out_dtype=F32, name="mm_qkv")
    q_r, k_r, k_t, v_b, v_t = _qk_prep(qkv, qg2, kg2, tabs, "qk_prep")
    first = [slice(None)] * 5 + [slice(0, 1)] * 2
    o_at, lse, wo, wp, wxq, wxkv, wxo, wup, wdn = _flash_fwd(
        q_r, k_r, v_t, _Gather(shards[1:], BIG_AXIS[1:], first), shards[1:], "flash_fwd")
    ffn_w = {"up": wup, "down": wdn}
    x1, hq0 = mm(o_at, wo, "nn", b_l=0, res=x0, out_dtype=F32, norm_out=(w["xattn_norm"][0:1], BF16), name="mm_attn_o")

    def xattn_fwd(l, xin, hq):
        mn = _rms_fwd(mem, w["mem_norm"][l:l + 1], BF16, f"rms_mem{l}")
        xq = mm(hq, wxq, "nn", b_l=l, scale=X_HEAD_DIM ** -0.5, out_dtype=BF16, name=f"mm_xq{l}")
        kv = mm(mn, wxkv, "nn", b_l=l, out_dtype=BF16, name=f"mm_xkv{l}")
        xo = _xattn_fwd(xq, kv, f"xattn_fwd{l}")
        saved[f"x{l}"] = (hq, mn, xq, kv, xo)
        return mm(xo, wxo, "nn", b_l=l, res=xin, out_dtype=F32, norm_out=(w["ffn_norm"][l:l + 1], BF16), name=f"mm_xo{l}")

    def ffn_fwd(l, xin, hf, norm_out):
        ug = mm(hf, ffn_w["up"], "nn", b_l=l, n=dff, out_dtype=BF16, name=f"mm_up_g{l}")
        uv = mm(hf, ffn_w["up"], "nn", b_l=l, n=dff, b_off=(0, dff), out_dtype=BF16, name=f"mm_up_v{l}")
        if l == 0:
            rest = _Gather(shards[6:], BIG_AXIS[6:], [slice(1, 2)] * 2)
            act, ffn_w["up"], ffn_w["down"] = _conv_gate_fwd(ug, uv, conv_w, conv_b, l, f"conv_gate{l}", rest, shards[6:],
                                                             [ffn_w["up"], ffn_w["down"]])
        else:
            act = _conv_gate_fwd(ug, uv, conv_w, conv_b, l, f"conv_gate{l}")
        saved[f"f{l}"] = (hf, ug, uv, act)
        return mm(act, ffn_w["down"], "nn", b_l=l, res=xin, out_dtype=F32, norm_out=norm_out, name=f"mm_down{l}")

    x2, hf0 = xattn_fwd(0, x1, hq0)
    x3, hp = ffn_fwd(0, x2, hf0, (pool_norm, F32))
    x4 = _pool_fwd(hp, x3, wp, pool_scale, "pool_fwd")
    x5, hf1 = xattn_fwd(1, x4, _rms_fwd(x4, w["xattn_norm"][1:2], BF16, "rms_xq1"))
    xs = [x0, x1, x2, x3, x4, x5, ffn_fwd(1, x5, hf1, None)]
    dres, g_final, loss = _final_loss(xs[6], w["final_norm"].reshape(1, d), tgt, "final_loss")

    grads = {}
    gbuf = {}

    def dw(nm, a, b, layer, full, off=(0, 0), n=None, tn=None):
        gbuf[nm] = _mm(a, b, "tn", out_dtype=BF16, out_full=full, out_l=layer, out_off=off, n=n, tn=tn,
                       alias=gbuf.get(nm), name=f"dw_{nm}{layer}_{off[1]}")

    def ffn_bwd(l, xin, dres):
        hf, ug, uv, act = saved[f"f{l}"]
        wup, wdn = ffn_w["up"], ffn_w["down"]
        dw("ffn_w_down", act, dres, l, wdn.shape)
        dact = _mm(dres, wdn, "nt", b_l=l, out_dtype=BF16, name=f"mm_dact{l}")
        dug, duv, dwg, dwv = _conv_gate_bwd(ug, uv, dact, conv_w, conv_b, l, f"conv_gate_bwd{l}")
        dw("ffn_w_up", hf, dug, l, wup.shape, tn=1408)
        dw("ffn_w_up", hf, duv, l, wup.shape, off=(0, dff), tn=1408)
        dhf = _mm(dug, wup, "nt", b_l=l, n=d, out_dtype=F32, name=f"mm_dhf_g{l}")
        dres, dg = _mm(duv, wup, "nt", b_l=l, n=d, b_off=(0, dff), res=dhf, out_dtype=F32, tm=256,
                       norm_bwd=(xin, w["ffn_norm"][l:l + 1], dres), name=f"mm_dhf_v{l}")
        return dres, dg, jnp.concatenate([dwg[:3], dwv[:3]], axis=1), jnp.concatenate([dwg[3], dwv[3]], axis=0)

    def xattn_bwd(l, xin, dres):
        hq, mn, xq, kv, xo = saved[f"x{l}"]
        dw("xattn_w_o", xo, dres, l, wxo.shape)
        dxo = _mm(dres, wxo, "nt", b_l=l, out_dtype=BF16, name=f"mm_dxo{l}")
        dq, dkv = _xattn_bwd(xq, kv, dxo, f"xattn_bwd{l}")
        dw("xattn_w_q", hq, dq, l, wxq.shape)
        dw("xattn_w_kv", mn, dkv, l, wxkv.shape)
        dmn = _mm(dkv, wxkv, "nt", b_l=l, out_dtype=F32, name=f"mm_dmn{l}")
        _, dg_mem = _rms_bwd(mem, w["mem_norm"][l:l + 1], dmn, None, f"rms_mem_bwd{l}")
        dres, dg = _mm(dq, wxq, "nt", b_l=l, out_dtype=F32, norm_bwd=(xin, w["xattn_norm"][l:l + 1], dres),
                       name=f"mm_dhq{l}")
        return dres, dg, dg_mem

    g_ffn, g_xn, g_mn, g_cw, g_cb = [None] * n_layers, [None] * n_layers, [None] * n_layers, [None] * n_layers, [None] * n_layers
    dres, g_ffn[1], g_cw[1], g_cb[1] = ffn_bwd(1, xs[5], dres)
    dres, g_xn[1], g_mn[1] = xattn_bwd(1, xs[4], dres)
    dhp, g_pw, g_pscale = _pool_bwd(hp, dres, wp, pool_scale, "pool_bwd")
    dres, g_pnorm = _rms_bwd(xs[3], pool_norm, dhp, dres, "rms_pool_bwd")
    dres, g_ffn[0], g_cw[0], g_cb[0] = ffn_bwd(0, xs[2], dres)
    dres, g_xn[0], g_mn[0] = xattn_bwd(0, xs[1], dres)
    dw("attn_w_o", o_at, dres, 0, wo.shape)
    do = _mm(dres, wo, "nt", b_l=0, out_dtype=BF16, name="mm_do")
    gbuf["pool_w"] = g_pw.astype(BF16)
    early = [gbuf[nm] for nm in BIG[1:]]
    dq_r, dk_r, dv, *recv_early = _flash_bwd(q_r, k_r, k_t, v_b, do, o_at, lse, _Scatter(early, BIG_AXIS[1:]), early,
                                             "flash_bwd")
    def sum4(nm, rc):
        return _sum4(rc.reshape(4, -1, rc.shape[-1]), f"sum4_{nm}")
    sums_early = [sum4(nm, rc) for nm, rc in zip(BIG[1:], recv_early)]
    dqkv, dqg, dkg, *others_early = _qk_prep_bwd(qkv, dq_r, dk_r, dv, qg2, kg2, tabs, _Swap(sums_early), sums_early,
                                                 "qk_prep_bwd")
    dw("attn_w_qkv", h0, dqkv, 0, wq.shape)
    grad_x, g_an = _mm(dqkv, wq, "nt", b_l=0, out_dtype=F32, norm_bwd=(x0, w["attn_norm"], dres), name="mm_dh0")

    small_g = {
        "attn_norm": g_an, "attn_q_gain": dqg[:, :HEAD_DIM] + dqg[:, HEAD_DIM:], "attn_k_gain": dkg[:, :HEAD_DIM] + dkg[:, HEAD_DIM:],
        "xattn_norm": jnp.concatenate(g_xn, axis=0), "mem_norm": jnp.concatenate(g_mn, axis=0),
        "ffn_norm": jnp.concatenate(g_ffn, axis=0), "ffn_conv_b": jnp.stack(g_cb, axis=0), "final_norm": g_final.reshape(d),
        "pool_norm": g_pnorm, "pool_scale": g_pscale, "ffn_conv_w": jnp.stack(g_cw, axis=0)}
    names = SMALL_REPL + SMALL_SHARD
    _, total = _gather8(_pack([loss[0, :1]] + [small_g[nm] for nm in names]), True, "reduce_small")
    parts = _unpack(total.reshape(-1), [(1,)] + [small_g[nm].shape for nm in names])
    loss_out = parts[0][0]
    for nm, g in zip(names, parts[1:]):
        if nm in SMALL_SHARD:
            size = w[nm].shape[-1]
            g = lax.dynamic_slice_in_dim(g, chip * size, size, axis=g.ndim - 1)
        grads[nm] = g.reshape(w[nm].shape)

    packed = [_pack([src[nm] for nm in names]) for src in (w, grads, m, v)]
    _, sd, sm, sv = _adamw(packed[0], packed[1], None, packed[2], packed[3], "adamw_small")
    shapes = [w[nm].shape for nm in names]
    delta = dict(zip(names, _unpack(sd.reshape(-1), shapes)))
    new_m = dict(zip(names, _unpack(sm.reshape(-1), shapes)))
    new_v = dict(zip(names, _unpack(sv.reshape(-1), shapes)))

    def big_adamw(nm, mine, other, plan=None, sends=()):
        cols = mine.shape[-1]
        outs = _adamw(w[nm].reshape(-1, cols), mine, other, m[nm].reshape(-1, cols), v[nm].reshape(-1, cols),
                      f"adamw_{nm}", plan, sends)
        grads[nm], delta[nm], new_m[nm], new_v[nm] = (o.reshape(w[nm].shape) for o in outs[:4])
        return outs[4:]

    late = [gbuf[BIG[0]]]
    pairs = dict(zip(BIG[1:], zip(sums_early, others_early)))
    (recv_late,) = big_adamw("ffn_w_up", *pairs["ffn_w_up"], _Scatter(late, BIG_AXIS[:1]), late)
    for nm in BIG[1:]:
        if nm != "ffn_w_up":
            big_adamw(nm, *pairs[nm])
    sums_late = [sum4(BIG[0], recv_late)]
    (other_late,) = _comm_call(_Swap(sums_late), sums_late, "swap_qkv")
    big_adamw(BIG[0], sums_late[0], other_late)

    return loss_out, grad_x, grads, delta, new_m, new_v


def kernel(x, mem, attn_norm, attn_w_qkv, attn_q_gain, attn_k_gain, attn_w_o, pool_norm, pool_w, pool_scale, xattn_norm, mem_norm, xattn_w_q, xattn_w_kv, xattn_w_o, ffn_norm, ffn_w_up, ffn_conv_w, ffn_conv_b, ffn_w_down, final_norm, loss_target, m_attn_norm, m_attn_w_qkv, m_attn_q_gain, m_attn_k_gain, m_attn_w_o, m_pool_norm, m_pool_w, m_pool_scale, m_xattn_norm, m_mem_norm, m_xattn_w_q, m_xattn_w_kv, m_xattn_w_o, m_ffn_norm, m_ffn_w_up, m_ffn_conv_w, m_ffn_conv_b, m_ffn_w_down, m_final_norm, v_attn_norm, v_attn_w_qkv, v_attn_q_gain, v_attn_k_gain, v_attn_w_o, v_pool_norm, v_pool_w, v_pool_scale, v_xattn_norm, v_mem_norm, v_xattn_w_q, v_xattn_w_kv, v_xattn_w_o, v_ffn_norm, v_ffn_w_up, v_ffn_conv_w, v_ffn_conv_b, v_ffn_w_down, v_final_norm):
    given = dict(locals())
    w = {nm: given[nm] for nm in ORDER}
    m = {nm: given["m_" + nm] for nm in ORDER}
    v = {nm: given["v_" + nm] for nm in ORDER}
    seq, d = x.shape[1], x.shape[2]
    loss, grad_x, grads, delta, new_m, new_v = _step(
        x.reshape(seq, d), mem.reshape(mem.shape[1], d), loss_target.reshape(seq, d), w, m, v)
    return (loss, grad_x.reshape(x.shape), *[grads[nm] for nm in ORDER], *[delta[nm] for nm in ORDER],
            *[new_m[nm] for nm in ORDER], *[new_v[nm] for nm in ORDER])
```

```python
import functools
import itertools

import jax
import jax.numpy as jnp
from jax import lax
from jax.experimental import pallas as pl
from jax.experimental.pallas import tpu as pltpu

F32, BF16 = jnp.float32, jnp.bfloat16
EPS = 1e-6
GRID_W = 64
ROPE_THETA = 10000.0
HEAD_DIM = 64
N_HEADS = 16
N_KV = 4
X_HEADS = 4
X_HEAD_DIM = 256
POOL_GROUPS = 4
POOL_GROUP_W = 256
HALO = 16
LANES = 128
ADAM_LR, ADAM_B1, ADAM_B2, ADAM_EPS, ADAM_WD, ADAM_STEP = 0.001, 0.9, 0.999, 1e-08, 0.01, 10
VMEM_LIMIT = 48 * 1024 * 1024
MESH = pl.DeviceIdType.MESH
NEG = -1e30
LOG2E = 1.4426950408889634
FLASH_TQ, FLASH_TK = 512, 4096
FLASH_SUB = 512
ANY = pl.BlockSpec(memory_space=pl.ANY)


def _cp(sem=None):
    return pltpu.CompilerParams(dimension_semantics=sem, vmem_limit_bytes=VMEM_LIMIT)


def _pick(n, cands):
    for c in cands:
        if c <= n and n % c == 0:
            return c
    return n


def _mm(a, b, mode, *, name, out_dtype, tm=None, tn=None, tk=None, n=None, k=None, b_l=None, b_off=(0, 0),
        res=None, scale=None, out_full=None, out_l=None, out_off=(0, 0), alias=None, norm_out=None, norm_bwd=None):
    if mode == "tn":
        K, M = a.shape
    else:
        M, K = a.shape
    bs = b.shape[-2:]
    if mode == "nn":
        K = k or K
        N = n or bs[1]
    elif mode == "nt":
        N = n or bs[0]
    else:
        N = n or bs[1]
    wide = (1408, 1024, 512, 256, 128)
    if mode == "tn":
        tm = tm or (M if M <= 1024 else _pick(M, wide))
        tk = tk or _pick(K, (2048, 1024, 512, 256, 128))
    else:
        tm = _pick(M, (tm or 512, 256, 128))
        tk = tk or (K if K <= 2816 else _pick(K, wide))
    tn = tn or (N if N <= 1536 else _pick(N, wide))
    assert M % tm == 0 and N % tn == 0 and K % tk == 0, (name, M, N, K, tm, tn, tk)
    nk = K // tk
    dims = {"nn": ((1,), (0,)), "nt": ((1,), (1,)), "tn": ((0,), (0,))}[mode]

    j_outer = nk == 1 and mode != "tn"

    def at(f):
        return (lambda j, i, kk: f(i, j, kk)) if j_outer else f

    if mode == "tn":
        a_spec = pl.BlockSpec((tk, tm), at(lambda i, j, kk: (kk, i)))
    else:
        a_spec = pl.BlockSpec((tm, tk), at(lambda i, j, kk: (i, kk)))
    if mode == "nt":
        bb, (d0, d1) = (tn, tk), (b_off[0] // tn, b_off[1] // tk)
        assert b_off[0] % tn == 0 and b_off[1] % tk == 0
        bidx = lambda i, j, kk: (j + d0, kk + d1)
    else:
        bb, (d0, d1) = (tk, tn), (b_off[0] // tk, b_off[1] // tn)
        assert b_off[0] % tk == 0 and b_off[1] % tn == 0
        bidx = lambda i, j, kk: (kk + d0, j + d1)
    if b.ndim == 3:
        b_spec = pl.BlockSpec((None,) + bb, at(lambda i, j, kk: (b_l,) + bidx(i, j, kk)))
    else:
        b_spec = pl.BlockSpec(bb, at(bidx))
    in_specs, operands = [a_spec, b_spec], [a, b]
    if res is not None:
        in_specs.append(pl.BlockSpec((tm, tn), at(lambda i, j, kk: (i, j))))
        operands.append(res)
    aliases = {}
    if alias is not None:
        aliases = {len(operands): 0}
        in_specs.append(ANY)
        operands.append(alias)
    if out_full is None:
        out_shape = jax.ShapeDtypeStruct((M, N), out_dtype)
        out_spec = pl.BlockSpec((tm, tn), at(lambda i, j, kk: (i, j)))
    else:
        assert out_off[0] % tm == 0 and out_off[1] % tn == 0
        o0, o1 = out_off[0] // tm, out_off[1] // tn
        out_shape = jax.ShapeDtypeStruct(out_full, out_dtype)
        out_spec = pl.BlockSpec((None, tm, tn), at(lambda i, j, kk: (out_l, i + o0, j + o1)))
    has_res, has_alias = res is not None, alias is not None
    grid = (N // tn, M // tm, nk) if j_outer else (M // tm, N // tn, nk)
    n_extra = 0
    if norm_out is not None or norm_bwd is not None:
        assert j_outer and tn == N and out_full is None, name
        row = pl.BlockSpec((tm, tn), at(lambda i, j, kk: (i, 0)))
        vec = pl.BlockSpec((1, tn), at(lambda i, j, kk: (0, 0)))
        if norm_out is not None:
            in_specs.append(vec)
            operands.append(norm_out[0])
            n_extra = 1
            out_shape = (out_shape, jax.ShapeDtypeStruct((M, N), norm_out[1]))
            out_spec = (out_spec, row)
        else:
            in_specs += [row, vec, row]
            operands += list(norm_bwd)
            n_extra = 3
            out_shape = (out_shape, jax.ShapeDtypeStruct((1, N), F32))
            out_spec = (out_spec, vec)
    n_out = 1 if n_extra == 0 else 2

    def body(*refs):
        a_ref, b_ref = refs[0], refs[1]
        pos = 2
        res_ref = None
        if has_res:
            res_ref = refs[pos]
            pos += 1
        if has_alias:
            pos += 1
        extra = refs[pos:pos + n_extra]
        pos += n_extra
        o_ref, acc_ref = refs[pos], refs[pos + n_out]
        kk = pl.program_id(2)
        part = lax.dot_general(a_ref[...].astype(BF16), b_ref[...].astype(BF16), (dims, ((), ())),
                               preferred_element_type=F32)

        def finish(acc):
            if scale is not None:
                acc = acc * scale
            if res_ref is not None:
                acc = acc + res_ref[...]
            if norm_out is not None:
                r = lax.rsqrt(jnp.mean(acc * acc, axis=-1, keepdims=True) + EPS)
                refs[pos + 1][...] = (acc * r * extra[0][...]).astype(refs[pos + 1].dtype)
            if norm_bwd is not None:
                x_ref, g_ref, dres_ref = extra
                dg_ref, step = refs[pos + 1], pl.program_id(1)
                xv = x_ref[...]
                r = lax.rsqrt(jnp.mean(xv * xv, axis=-1, keepdims=True) + EPS)
                nv = xv * r
                dgp = jnp.sum(acc * nv, axis=0, keepdims=True)

                @pl.when(step == 0)
                def _():
                    dg_ref[...] = dgp

                @pl.when(step > 0)
                def _():
                    dg_ref[...] += dgp

                dn = acc * g_ref[...]
                acc = dres_ref[...] + r * (dn - nv * jnp.mean(dn * nv, axis=-1, keepdims=True))
            o_ref[...] = acc.astype(o_ref.dtype)

        if nk == 1:
            finish(part)
        else:
            @pl.when(kk == 0)
            def _():
                acc_ref[...] = part

            @pl.when(jnp.logical_and(kk > 0, kk < nk - 1))
            def _():
                acc_ref[...] += part

            @pl.when(kk == nk - 1)
            def _():
                finish(acc_ref[...] + part)

    return pl.pallas_call(
        body, out_shape=out_shape, grid=grid, in_specs=in_specs, out_specs=out_spec,
        scratch_shapes=[pltpu.VMEM((tm, tn) if nk > 1 else (8, 128), F32)], input_output_aliases=aliases,
        compiler_params=_cp(("arbitrary",) * 3 if norm_bwd is not None else ("parallel", "parallel", "arbitrary")),
        name=name)(*operands)


def _hosted(plan, nsteps, step, compute):
    if plan is None:
        return compute()

    @pl.when(step == 0)
    def _():
        plan.start()

    compute()

    @pl.when(step == nsteps - 1)
    def _():
        if hasattr(plan, "forward"):
            plan.forward()
        plan.finish()


def _rms_fwd(x, gain, out_dtype, name, plan=None, sends=()):
    rows, d = x.shape
    tr = _pick(rows, (512, 256))
    nt, nsteps = (plan.nt if plan is not None else 0), rows // tr

    def body(x_ref, g_ref, *rest):
        o_ref = rest[nt]
        if plan is not None:
            plan.bind(rest[:nt], rest[nt + 1:2 * nt + 1], *rest[2 * nt + 1:])

        def compute():
            xv = x_ref[...]
            r = lax.rsqrt(jnp.mean(xv * xv, axis=-1, keepdims=True) + EPS)
            o_ref[...] = (xv * r * g_ref[...]).astype(o_ref.dtype)

        _hosted(plan, nsteps, pl.program_id(0), compute)

    out = jax.ShapeDtypeStruct((rows, d), out_dtype)
    row = pl.BlockSpec((tr, d), lambda i: (i, 0))
    in_specs = [row, pl.BlockSpec((1, d), lambda i: (0, 0))]
    if plan is None:
        return pl.pallas_call(body, out_shape=out, grid=(nsteps,), in_specs=in_specs, out_specs=row,
                              compiler_params=_cp(("parallel",)), name=name)(x, gain)
    return pl.pallas_call(
        body, out_shape=(out, *plan.out_shape), grid=(nsteps,), in_specs=in_specs + [ANY] * nt,
        out_specs=(row, *([ANY] * nt)), scratch_shapes=plan.scratch, compiler_params=_cp(("arbitrary",)),
        name=name)(x, gain, *sends)


def _rms_bwd(x, gain, dh, dres, name):
    rows, d = x.shape
    tr = _pick(rows, (512, 256))
    need_dx = dres is not None

    def body(*refs):
        if need_dx:
            x_ref, g_ref, dh_ref, dres_ref, o_ref, dg_ref = refs
        else:
            x_ref, g_ref, dh_ref, dg_ref = refs
        i = pl.program_id(0)
        xv = x_ref[...]
        dhv = dh_ref[...].astype(F32)
        r = lax.rsqrt(jnp.mean(xv * xv, axis=-1, keepdims=True) + EPS)
        nv = xv * r
        part = jnp.sum(dhv * nv, axis=0, keepdims=True)

        @pl.when(i == 0)
        def _():
            dg_ref[...] = part

        @pl.when(i > 0)
        def _():
            dg_ref[...] += part

        if need_dx:
            dn = dhv * g_ref[...]
            dx = r * (dn - nv * jnp.mean(dn * nv, axis=-1, keepdims=True))
            o_ref[...] = dres_ref[...] + dx

    row_spec = pl.BlockSpec((tr, d), lambda i: (i, 0))
    vec_spec = pl.BlockSpec((1, d), lambda i: (0, 0))
    if need_dx:
        return pl.pallas_call(
            body, out_shape=(jax.ShapeDtypeStruct((rows, d), F32), jax.ShapeDtypeStruct((1, d), F32)),
            grid=(rows // tr,), in_specs=[row_spec, vec_spec, row_spec, row_spec], out_specs=(row_spec, vec_spec),
            compiler_params=_cp(("arbitrary",)), name=name)(x, gain, dh, dres)
    return None, pl.pallas_call(
        body, out_shape=jax.ShapeDtypeStruct((1, d), F32), grid=(rows // tr,),
        in_specs=[row_spec, vec_spec, row_spec], out_specs=vec_spec,
        compiler_params=_cp(("arbitrary",)), name=name)(x, gain, dh)


def _final_loss(x, gain, target, name):
    rows, d = x.shape
    tr = _pick(rows, (512, 256))
    nsteps = rows // tr

    def body(x_ref, g_ref, t_ref, dx_ref, dg_ref, loss_ref, acc_ref):
        i = pl.program_id(0)
        xv = x_ref[...]
        g = g_ref[...]
        r = lax.rsqrt(jnp.mean(xv * xv, axis=-1, keepdims=True) + EPS)
        nv = xv * r
        err = nv * g - t_ref[...]
        dy = err * (1.0 / d)
        dn = dy * g
        dx_ref[...] = r * (dn - nv * jnp.mean(dn * nv, axis=-1, keepdims=True))
        dgp = jnp.sum(dy * nv, axis=0, keepdims=True)
        lp = jnp.sum(err * err, axis=0, keepdims=True)

        @pl.when(i == 0)
        def _():
            dg_ref[...] = dgp
            acc_ref[...] = lp

        @pl.when(i > 0)
        def _():
            dg_ref[...] += dgp
            acc_ref[...] += lp

        @pl.when(i == nsteps - 1)
        def _():
            tot = jnp.sum(acc_ref[...], axis=1, keepdims=True) * (0.5 / d)
            loss_ref[...] = jnp.broadcast_to(tot, loss_ref.shape)

    row_spec = pl.BlockSpec((tr, d), lambda i: (i, 0))
    vec_spec = pl.BlockSpec((1, d), lambda i: (0, 0))
    return pl.pallas_call(
        body, out_shape=(jax.ShapeDtypeStruct((rows, d), F32), jax.ShapeDtypeStruct((1, d), F32),
                         jax.ShapeDtypeStruct((1, LANES), F32)),
        grid=(nsteps,), in_specs=[row_spec, vec_spec, row_spec],
        out_specs=(row_spec, vec_spec, pl.BlockSpec((1, LANES), lambda i: (0, 0))),
        scratch_shapes=[pltpu.VMEM((1, d), F32)], compiler_params=_cp(("arbitrary",)), name=name)(x, gain, target)


def _rope_tables(seq):
    pairs = HEAD_DIM // 4
    lane = jnp.arange(LANES, dtype=jnp.int32) % HEAD_DIM
    by_col, second, pair = lane // (2 * pairs) == 1, (lane % (2 * pairs)) // pairs == 1, lane % pairs
    inv_freq = ROPE_THETA ** (-pair.astype(F32) / pairs)
    t = jnp.arange(seq, dtype=jnp.int32)[:, None]
    pos = jnp.where(by_col[None, :], t % GRID_W, t // GRID_W).astype(F32)
    ang = pos * inv_freq[None, :]
    cos, sin = jnp.cos(ang), jnp.sin(ang)
    return cos, jnp.where(second[None, :], sin, 0.0), jnp.where(second[None, :], 0.0, -sin)


def _pair_norm(xv, lo):
    sq = xv * xv
    s_lo = jnp.sum(jnp.where(lo, sq, 0.0), axis=1, keepdims=True)
    s_hi = jnp.sum(jnp.where(lo, 0.0, sq), axis=1, keepdims=True)
    return lax.rsqrt(jnp.where(lo, s_lo, s_hi) * (1.0 / HEAD_DIM) + EPS)


def _rope(y, c, sp, sm):
    return y * c + pltpu.roll(y, 16, axis=1) * sp + pltpu.roll(y, LANES - 16, axis=1) * sm


def _rope_t(dz, c, sp, sm):
    return dz * c + pltpu.roll(dz * sp, LANES - 16, axis=1) + pltpu.roll(dz * sm, 16, axis=1)


def _qk_prep(qkv, qg2, kg2, tabs, plan, sends, name):
    seq = qkv.shape[0]
    ts = _pick(seq, (256, 128))
    nq, nkp = N_HEADS // 2, N_KV // 2
    qw, kw = N_HEADS * HEAD_DIM, N_KV * HEAD_DIM
    nt, nsteps = plan.nt, seq // ts

    def body(x_ref, qg_ref, kg_ref, c_ref, sp_ref, sm_ref, *rest):
        q_ref, k_ref, kt_ref, v_ref, vt_ref = rest[nt:nt + 5]
        plan.bind(rest[:nt], rest[nt + 5:2 * nt + 5], *rest[2 * nt + 5:])
        _hosted(plan, nsteps, pl.program_id(0), lambda: compute(x_ref, qg_ref, kg_ref, c_ref, sp_ref, sm_ref,
                                                                q_ref, k_ref, kt_ref, v_ref, vt_ref))

    def compute(x_ref, qg_ref, kg_ref, c_ref, sp_ref, sm_ref, q_ref, k_ref, kt_ref, v_ref, vt_ref):
        lo = lax.broadcasted_iota(jnp.int32, (ts, LANES), 1) < HEAD_DIM
        top = lax.broadcasted_iota(jnp.int32, (LANES, ts), 0) < HEAD_DIM
        c, sp, sm = c_ref[...], sp_ref[...], sm_ref[...]
        for i in range(nq):
            xv = x_ref[:, i * LANES:(i + 1) * LANES]
            y = xv * _pair_norm(xv, lo) * qg_ref[...]
            q_ref[:, i * LANES:(i + 1) * LANES] = (_rope(y, c, sp, sm) * (LOG2E * HEAD_DIM ** -0.5)).astype(BF16)
        for i in range(nkp):
            xv = x_ref[:, qw + i * LANES:qw + (i + 1) * LANES]
            z = _rope(xv * _pair_norm(xv, lo) * kg_ref[...], c, sp, sm)
            k_ref[:, i * LANES:(i + 1) * LANES] = z.astype(BF16)
            kt_ref[i * LANES:(i + 1) * LANES, :] = z.T.astype(BF16)
            vv = x_ref[:, qw + kw + i * LANES:qw + kw + (i + 1) * LANES]
            v_ref[:, i * LANES:(i + 1) * LANES] = vv.astype(BF16)
            vvt = vv.T
            vt_ref[(2 * i) * LANES:(2 * i + 1) * LANES, :] = jnp.where(top, vvt, 1.0).astype(BF16)
            vt_ref[(2 * i + 1) * LANES:(2 * i + 2) * LANES, :] = jnp.where(top, 1.0, vvt).astype(BF16)

    tab = pl.BlockSpec((ts, LANES), lambda i: (i, 0))
    vec = pl.BlockSpec((1, LANES), lambda i: (0, 0))
    return pl.pallas_call(
        body,
        out_shape=(jax.ShapeDtypeStruct((seq, qw), BF16), jax.ShapeDtypeStruct((seq, kw), BF16),
                   jax.ShapeDtypeStruct((kw, seq), BF16), jax.ShapeDtypeStruct((seq, kw), BF16),
                   jax.ShapeDtypeStruct((N_KV * LANES, seq), BF16), *plan.out_shape),
        grid=(nsteps,),
        in_specs=[pl.BlockSpec((ts, qw + 2 * kw), lambda i: (i, 0)), vec, vec, tab, tab, tab] + [ANY] * nt,
        out_specs=(pl.BlockSpec((ts, qw), lambda i: (i, 0)), pl.BlockSpec((ts, kw), lambda i: (i, 0)),
                   pl.BlockSpec((kw, ts), lambda i: (0, i)), pl.BlockSpec((ts, kw), lambda i: (i, 0)),
                   pl.BlockSpec((N_KV * LANES, ts), lambda i: (0, i)), *([ANY] * nt)),
        scratch_shapes=plan.scratch, compiler_params=_cp(("arbitrary",)), name=name)(qkv, qg2, kg2, *tabs, *sends)


def _qk_prep_bwd(qkv, dq, dk, dv, qg2, kg2, tabs, plan, sends, name):
    seq = qkv.shape[0]
    ts = _pick(seq, (256, 128))
    nq, nkp = N_HEADS // 2, N_KV // 2
    qw, kw = N_HEADS * HEAD_DIM, N_KV * HEAD_DIM
    nt, nsteps = plan.nt, seq // ts

    def body(x_ref, dq_ref, dk_ref, dv_ref, qg_ref, kg_ref, c_ref, sp_ref, sm_ref, *rest):
        o_ref, dqg_ref, dkg_ref = rest[nt:nt + 3]
        plan.bind(rest[:nt], rest[nt + 3:2 * nt + 3], *rest[2 * nt + 3:])
        step = pl.program_id(0)

        @pl.when(step == 0)
        def _():
            plan.start()

        lo = lax.broadcasted_iota(jnp.int32, (ts, LANES), 1) < HEAD_DIM
        c, sp, sm = c_ref[...], sp_ref[...], sm_ref[...]

        def one(xv, dz, gain):
            r = _pair_norm(xv, lo)
            nv = xv * r
            dy = _rope_t(dz, c, sp, sm)
            dgp = jnp.sum(dy * nv, axis=0, keepdims=True)
            dn = dy * gain
            t = dn * nv
            m_lo = jnp.sum(jnp.where(lo, t, 0.0), axis=1, keepdims=True)
            m_hi = jnp.sum(jnp.where(lo, 0.0, t), axis=1, keepdims=True)
            m = jnp.where(lo, m_lo, m_hi) * (1.0 / HEAD_DIM)
            return r * (dn - nv * m), dgp

        dqg = jnp.zeros((1, LANES), F32)
        for i in range(nq):
            sl = slice(i * LANES, (i + 1) * LANES)
            dx, dgp = one(x_ref[:, sl], dq_ref[:, sl] * (HEAD_DIM ** -0.5), qg_ref[...])
            o_ref[:, sl] = dx.astype(BF16)
            dqg = dqg + dgp
        dkg = jnp.zeros((1, LANES), F32)
        for i in range(nkp):
            sl = slice(i * LANES, (i + 1) * LANES)
            dx, dgp = one(x_ref[:, qw + i * LANES:qw + (i + 1) * LANES], dk_ref[:, sl], kg_ref[...])
            o_ref[:, qw + i * LANES:qw + (i + 1) * LANES] = dx.astype(BF16)
            dkg = dkg + dgp
            o_ref[:, qw + kw + i * LANES:qw + kw + (i + 1) * LANES] = dv_ref[:, sl].astype(BF16)

        @pl.when(step == 0)
        def _():
            dqg_ref[...] = dqg
            dkg_ref[...] = dkg

        @pl.when(step > 0)
        def _():
            dqg_ref[...] += dqg
            dkg_ref[...] += dkg

        @pl.when(step == nsteps - 1)
        def _():
            plan.finish()

    tab = pl.BlockSpec((ts, LANES), lambda i: (i, 0))
    vec = pl.BlockSpec((1, LANES), lambda i: (0, 0))
    return pl.pallas_call(
        body,
        out_shape=(jax.ShapeDtypeStruct((seq, qw + 2 * kw), BF16), jax.ShapeDtypeStruct((1, LANES), F32),
                   jax.ShapeDtypeStruct((1, LANES), F32), *plan.out_shape),
        grid=(nsteps,),
        in_specs=[pl.BlockSpec((ts, qw + 2 * kw), lambda i: (i, 0)), pl.BlockSpec((ts, qw), lambda i: (i, 0)),
                  pl.BlockSpec((ts, kw), lambda i: (i, 0)), pl.BlockSpec((ts, kw), lambda i: (i, 0)),
                  vec, vec, tab, tab, tab] + [ANY] * nt,
        out_specs=(pl.BlockSpec((ts, qw + 2 * kw), lambda i: (i, 0)), vec, vec, *([ANY] * nt)),
        scratch_shapes=plan.scratch, compiler_params=_cp(("arbitrary",)), name=name)(qkv, dq, dk, dv, qg2, kg2, *tabs, *sends)


def _slot(blk, off0, tq):
    half = lax.broadcasted_iota(jnp.int32, (tq, LANES), 1) // HEAD_DIM
    keep = half == jnp.where(off0, 0, 1)
    parts = []
    for p in range(2):
        pair = blk[:, p * LANES:(p + 1) * LANES].astype(F32)
        rolled = pltpu.roll(pair, HEAD_DIM, axis=1)
        parts.append(jnp.where(keep, jnp.where(off0, pair, rolled), 0.0))
        parts.append(jnp.where(keep, jnp.where(off0, rolled, pair), 0.0))
    return jnp.concatenate(parts, axis=0)


def _unslot(x4, off0, tq):
    lo = lax.broadcasted_iota(jnp.int32, (tq, LANES), 1) < HEAD_DIM
    pairs = []
    for p in range(2):
        h0 = x4[(2 * p) * tq:(2 * p + 1) * tq]
        h1 = x4[(2 * p + 1) * tq:(2 * p + 2) * tq]
        a = jnp.where(off0, h0, pltpu.roll(h0, HEAD_DIM, axis=1))
        b = jnp.where(off0, pltpu.roll(h1, HEAD_DIM, axis=1), h1)
        pairs.append(jnp.where(lo, a, b))
    return jnp.concatenate(pairs, axis=1)


def _flash_fwd(q, k, vt, plan, shards, name):
    seq = q.shape[0]
    tq = _pick(seq, (FLASH_TQ, 128))
    tk = _pick(seq, (FLASH_TK, 2048, 512, 256, 128))
    sub = _pick(tk, (FLASH_SUB, 256, 128))
    nq, nkv, nsub = seq // tq, seq // tk, tk // sub
    gw = 4 * HEAD_DIM
    nt = plan.nt

    def body(q_ref, k_ref, vt_ref, *rest):
        o_ref, lse_ref = rest[nt:nt + 2]
        q4_ref, m_ref, acc_ref, st_ref = rest[2 * nt + 2:2 * nt + 6]
        plan.bind(rest[:nt], rest[nt + 2:2 * nt + 2], *rest[2 * nt + 6:])
        g, qi, ki = pl.program_id(0), pl.program_id(1), pl.program_id(2)
        off0 = (g % 2) == 0
        @pl.when(jnp.logical_and(g == 0, jnp.logical_and(qi == 0, ki == 0)))
        def _():
            plan.start()

        @pl.when(jnp.logical_and(g == N_KV - 1, jnp.logical_and(qi == nq - 1, ki == 0)))
        def _():
            plan.forward()

        @pl.when(ki == 0)
        def _():
            q4_ref[...] = _slot(q_ref[...], off0, tq).astype(BF16)
            m_ref[...] = jnp.full(m_ref.shape, NEG, F32)
            acc_ref[...] = jnp.zeros(acc_ref.shape, F32)

        q4 = q4_ref[...]

        def scores(c):
            st_ref[c % 2] = lax.dot_general(k_ref[c * sub:(c + 1) * sub, :], q4, (((1,), (1,)), ((), ())),
                                            preferred_element_type=F32)

        m, acc = m_ref[...], acc_ref[...]
        scores(0)
        for c in range(nsub):
            if c + 1 < nsub:
                scores(c + 1)
            st = st_ref[c % 2]
            m_new = jnp.maximum(m, jnp.max(st, axis=0, keepdims=True))
            pt = jnp.exp2(st - m_new).astype(BF16)
            acc = jnp.exp2(m - m_new) * acc + jnp.dot(vt_ref[:, c * sub:(c + 1) * sub], pt, preferred_element_type=F32)
            m = m_new
        m_ref[...] = m
        acc_ref[...] = acc

        @pl.when(ki == nkv - 1)
        def _():
            acc = acc_ref[...]
            l = jnp.where(off0, acc[HEAD_DIM:HEAD_DIM + 1], acc[0:1])
            o4 = acc.T
            o4 = o4 / pltpu.roll(o4, HEAD_DIM, axis=1)
            o_ref[...] = _unslot(o4, off0, tq).astype(o_ref.dtype)
            lse_ref[...] = jnp.broadcast_to(m_ref[...] + jnp.log2(l), lse_ref.shape)

        @pl.when(jnp.logical_and(g == N_KV - 1, jnp.logical_and(qi == nq - 1, ki == nkv - 1)))
        def _():
            plan.finish()

    return pl.pallas_call(
        body,
        out_shape=(jax.ShapeDtypeStruct((seq, N_HEADS * HEAD_DIM), BF16),
                   jax.ShapeDtypeStruct((N_KV * nq * 8, 4 * tq), F32), *plan.out_shape),
        grid=(N_KV, nq, nkv),
        in_specs=[pl.BlockSpec((tq, gw), lambda g, qi, ki: (qi, g)),
                  pl.BlockSpec((tk, LANES), lambda g, qi, ki: (ki, g // 2)),
                  pl.BlockSpec((LANES, tk), lambda g, qi, ki: (g, ki))] + [ANY] * nt,
        out_specs=(pl.BlockSpec((tq, gw), lambda g, qi, ki: (qi, g)),
                   pl.BlockSpec((8, 4 * tq), lambda g, qi, ki: (g * nq + qi, 0)), *([ANY] * nt)),
        scratch_shapes=[pltpu.VMEM((4 * tq, LANES), BF16), pltpu.VMEM((1, 4 * tq), F32),
                        pltpu.VMEM((LANES, 4 * tq), F32), pltpu.VMEM((2, sub, 4 * tq), F32)] + plan.scratch,
        compiler_params=_cp(("arbitrary", "arbitrary", "arbitrary")), name=name)(q, k, vt, *shards)


def _flash_bwd(q, k, kt, v, do, o, lse, plan, grads, name):
    seq = q.shape[0]
    tq = _pick(seq, (FLASH_TQ, 128))
    tk = _pick(seq, (FLASH_TK, 2048, 512, 256, 128))
    sub = _pick(tk, (FLASH_SUB, 256, 128))
    nq, nkv, nsub = seq // tq, seq // tk, tk // sub
    gw = 4 * HEAD_DIM
    nt = plan.nt

    def body(q_ref, k_ref, kt_ref, v_ref, do_ref, o_ref, lse_ref, *rest):
        dq_ref, dk_ref, dv_ref = rest[nt:nt + 3]
        q4_ref, do4_ref, delta_ref, dqt_ref, st_ref, dpt_ref = rest[2 * nt + 3:2 * nt + 9]
        plan.bind(rest[:nt], rest[nt + 3:2 * nt + 3], *rest[2 * nt + 9:])
        g, qi, ki = pl.program_id(0), pl.program_id(1), pl.program_id(2)
        off0 = (g % 2) == 0

        @pl.when(jnp.logical_and(g == 0, jnp.logical_and(qi == 0, ki == 0)))
        def _():
            plan.start()

        @pl.when(jnp.logical_and(g % 2 == 0, jnp.logical_and(qi == 0, ki == 0)))
        def _():
            dk_ref[...] = jnp.zeros(dk_ref.shape, F32)
            dv_ref[...] = jnp.zeros(dv_ref.shape, F32)

        @pl.when(ki == 0)
        def _():
            q4_ref[...] = _slot(q_ref[...], off0, tq).astype(BF16)
            do4 = _slot(do_ref[...], off0, tq)
            do4_ref[...] = do4.astype(BF16)
            o4 = _slot(o_ref[...], off0, tq)
            delta_ref[...] = jnp.sum((do4 * o4).T, axis=0, keepdims=True)
            dqt_ref[...] = jnp.zeros(dqt_ref.shape, F32)

        q4, do4 = q4_ref[...], do4_ref[...]
        lse_row, delta = lse_ref[0:1, :], delta_ref[...]

        def products(c):
            rows = slice(c * sub, (c + 1) * sub)
            st_ref[c % 2] = lax.dot_general(k_ref[rows, :], q4, (((1,), (1,)), ((), ())), preferred_element_type=F32)
            dpt_ref[c % 2] = lax.dot_general(v_ref[rows, :], do4, (((1,), (1,)), ((), ())), preferred_element_type=F32)

        dqt = dqt_ref[...]
        products(0)
        for c in range(nsub):
            if c + 1 < nsub:
                products(c + 1)
            pt = jnp.exp2(st_ref[c % 2] - lse_row)
            dst = (pt * (dpt_ref[c % 2] - delta)).astype(BF16)
            rows = pl.ds(pl.multiple_of(ki * tk + c * sub, sub), sub)
            dv_ref[rows, :] += jnp.dot(pt.astype(BF16), do4, preferred_element_type=F32)
            dk_ref[rows, :] += jnp.dot(dst, q4, preferred_element_type=F32) * (1.0 / LOG2E)
            dqt = dqt + jnp.dot(kt_ref[:, c * sub:(c + 1) * sub], dst, preferred_element_type=F32)
        dqt_ref[...] = dqt

        @pl.when(ki == nkv - 1)
        def _():
            dq_ref[...] = _unslot(dqt_ref[...].T, off0, tq)

        @pl.when(jnp.logical_and(g == N_KV - 1, jnp.logical_and(qi == nq - 1, ki == nkv - 1)))
        def _():
            plan.finish()

    return pl.pallas_call(
        body,
        out_shape=(jax.ShapeDtypeStruct((seq, N_HEADS * HEAD_DIM), F32),
                   jax.ShapeDtypeStruct((seq, N_KV * HEAD_DIM), F32), jax.ShapeDtypeStruct((seq, N_KV * HEAD_DIM), F32),
                   *plan.out_shape),
        grid=(N_KV, nq, nkv),
        in_specs=[pl.BlockSpec((tq, gw), lambda g, qi, ki: (qi, g)),
                  pl.BlockSpec((tk, LANES), lambda g, qi, ki: (ki, g // 2)),
                  pl.BlockSpec((LANES, tk), lambda g, qi, ki: (g // 2, ki)),
                  pl.BlockSpec((tk, LANES), lambda g, qi, ki: (ki, g // 2)),
                  pl.BlockSpec((tq, gw), lambda g, qi, ki: (qi, g)),
                  pl.BlockSpec((tq, gw), lambda g, qi, ki: (qi, g)),
                  pl.BlockSpec((8, 4 * tq), lambda g, qi, ki: (g * nq + qi, 0))] + [ANY] * nt,
        out_specs=(pl.BlockSpec((tq, gw), lambda g, qi, ki: (qi, g)),
                   pl.BlockSpec((seq, LANES), lambda g, qi, ki: (0, g // 2)),
                   pl.BlockSpec((seq, LANES), lambda g, qi, ki: (0, g // 2)), *([ANY] * nt)),
        scratch_shapes=[pltpu.VMEM((4 * tq, LANES), BF16), pltpu.VMEM((4 * tq, LANES), BF16),
                        pltpu.VMEM((1, 4 * tq), F32), pltpu.VMEM((LANES, 4 * tq), F32),
                        pltpu.VMEM((2, sub, 4 * tq), F32), pltpu.VMEM((2, sub, 4 * tq), F32)] + plan.scratch,
        compiler_params=_cp(("arbitrary", "arbitrary", "arbitrary")), name=name)(q, k, kt, v, do, o, lse, *grads)


def _xattn_fwd(q, kv, name):
    seq, d = q.shape
    mlen = kv.shape[0]
    tq = _pick(seq, (512, 256))

    def body(q_ref, k_ref, v_ref, o_ref):
        for h in range(X_HEADS):
            sl = slice(h * X_HEAD_DIM, (h + 1) * X_HEAD_DIM)
            s = lax.dot_general(q_ref[:, sl], k_ref[:, sl], (((1,), (1,)), ((), ())), preferred_element_type=F32)
            e = jnp.exp(s - jnp.max(s, axis=-1, keepdims=True))
            p = e / jnp.sum(e, axis=-1, keepdims=True)
            o_ref[:, sl] = jnp.dot(p.astype(BF16), v_ref[:, sl], preferred_element_type=F32).astype(o_ref.dtype)

    return pl.pallas_call(
        body, out_shape=jax.ShapeDtypeStruct((seq, d), BF16), grid=(seq // tq,),
        in_specs=[pl.BlockSpec((tq, d), lambda i: (i, 0)), pl.BlockSpec((mlen, d), lambda i: (0, 0)),
                  pl.BlockSpec((mlen, d), lambda i: (0, 1))],
        out_specs=pl.BlockSpec((tq, d), lambda i: (i, 0)), compiler_params=_cp(("parallel",)), name=name)(q, kv, kv)


def _xattn_bwd(q, kv, do, name):
    seq, d = q.shape
    mlen = kv.shape[0]
    tq = _pick(seq, (512, 256))
    scale = X_HEAD_DIM ** -0.5

    def body(q_ref, k_ref, v_ref, do_ref, dq_ref, dkv_ref):
        i = pl.program_id(0)

        @pl.when(i == 0)
        def _():
            dkv_ref[...] = jnp.zeros(dkv_ref.shape, F32)

        for h in range(X_HEADS):
            sl = slice(h * X_HEAD_DIM, (h + 1) * X_HEAD_DIM)
            qh, kh, vh = q_ref[:, sl], k_ref[:, sl], v_ref[:, sl]
            doh = do_ref[:, sl].astype(BF16)
            st = lax.dot_general(kh, qh, (((1,), (1,)), ((), ())), preferred_element_type=F32)
            e = jnp.exp(st - jnp.max(st, axis=0, keepdims=True))
            pt = e / jnp.sum(e, axis=0, keepdims=True)
            dpt = lax.dot_general(vh, doh, (((1,), (1,)), ((), ())), preferred_element_type=F32)
            dst = (pt * (dpt - jnp.sum(pt * dpt, axis=0, keepdims=True))).astype(BF16)
            dkv_ref[:, sl] += jnp.dot(dst, qh, preferred_element_type=F32)
            dkv_ref[:, d + h * X_HEAD_DIM:d + (h + 1) * X_HEAD_DIM] += jnp.dot(pt.astype(BF16), doh,
                                                                                 preferred_element_type=F32)
            dqh = lax.dot_general(dst, kh, (((0,), (0,)), ((), ())), preferred_element_type=F32)
            dq_ref[:, sl] = (dqh * scale).astype(dq_ref.dtype)

    return pl.pallas_call(
        body, out_shape=(jax.ShapeDtypeStruct((seq, d), BF16), jax.ShapeDtypeStruct((mlen, 2 * d), F32)),
        grid=(seq // tq,),
        in_specs=[pl.BlockSpec((tq, d), lambda i: (i, 0)), pl.BlockSpec((mlen, d), lambda i: (0, 0)),
                  pl.BlockSpec((mlen, d), lambda i: (0, 1)), pl.BlockSpec((tq, d), lambda i: (i, 0))],
        out_specs=(pl.BlockSpec((tq, d), lambda i: (i, 0)), pl.BlockSpec((mlen, 2 * d), lambda i: (0, 0))),
        compiler_params=_cp(("arbitrary",)), name=name)(q, kv, kv, do)


def _halo_specs(tr, tc, seq, col):
    per, last = tr // HALO, seq // HALO - 1
    return [pl.BlockSpec((tr, tc), lambda j, r: (r, col(j))),
            pl.BlockSpec((HALO, tc), lambda j, r: (jnp.maximum(r * per - 1, 0), col(j))),
            pl.BlockSpec((HALO, tc), lambda j, r: (jnp.minimum((r + 1) * per, last), col(j)))]


def _extend(main_ref, prev_ref, next_ref, r, nr):
    pv = (r > 0).astype(F32)
    nv = (r < nr - 1).astype(F32)
    return jnp.concatenate([prev_ref[...].astype(F32) * pv, main_ref[...].astype(F32),
                            next_ref[...].astype(F32) * nv], axis=0)


def _conv3(e, w_ref, n):
    return pltpu.roll(e, 1, axis=0) * w_ref[0:1, :] + e * w_ref[1:2, :] + pltpu.roll(e, n - 1, axis=0) * w_ref[2:3, :]


def _conv_gate_fwd(ug, uv, cw, cb, layer, name, plan=None, shards=(), fulls=()):
    seq, f = ug.shape
    tc = 256
    tr = _pick(seq, (512, 256))
    nc, nr = f // tc, seq // tr
    n = tr + 2 * HALO
    nt = plan.nt if plan is not None else 0

    def body(g_ref, gp_ref, gn_ref, v_ref, vp_ref, vn_ref, wg_ref, wv_ref, bg_ref, bv_ref, *rest):
        o_ref = rest[2 * nt]
        j, r = pl.program_id(0), pl.program_id(1)
        if plan is not None:
            plan.bind(rest[:nt], rest[2 * nt + 1:3 * nt + 1], *rest[3 * nt + 1:])

            @pl.when(jnp.logical_and(j == 0, r == 0))
            def _():
                plan.start()

        cg = _conv3(_extend(g_ref, gp_ref, gn_ref, r, nr), wg_ref, n)[HALO:HALO + tr] + bg_ref[...]
        cv = _conv3(_extend(v_ref, vp_ref, vn_ref, r, nr), wv_ref, n)[HALO:HALO + tr] + bv_ref[...]
        o_ref[...] = (cg * jax.nn.sigmoid(cg) * cv).astype(o_ref.dtype)

        if plan is not None:
            @pl.when(jnp.logical_and(j == nc - 1, r == nr - 1))
            def _():
                plan.forward()
                plan.finish()

    w_spec = lambda shift: pl.BlockSpec((None, 3, tc), lambda j, r: (layer, 0, j + shift))
    b_spec = lambda shift: pl.BlockSpec((None, 1, tc), lambda j, r: (layer, 0, j + shift))
    act_shape = jax.ShapeDtypeStruct((seq, f), BF16)
    act_spec = pl.BlockSpec((tr, tc), lambda j, r: (r, j))
    in_specs = _halo_specs(tr, tc, seq, lambda j: j) * 2 + [w_spec(0), w_spec(nc), b_spec(0), b_spec(nc)]
    operands = (ug, ug, ug, uv, uv, uv, cw, cw, cb, cb)
    if plan is None:
        return pl.pallas_call(body, out_shape=act_shape, grid=(nc, nr), in_specs=in_specs, out_specs=act_spec,
                              compiler_params=_cp(("parallel", "parallel")), name=name)(*operands)
    return pl.pallas_call(
        body, out_shape=(act_shape, *plan.out_shape), grid=(nc, nr), in_specs=in_specs + [ANY] * (2 * nt),
        out_specs=(act_spec, *([ANY] * nt)), scratch_shapes=plan.scratch,
        input_output_aliases={len(operands) + nt + t: 1 + t for t in range(nt)},
        compiler_params=_cp(("arbitrary", "arbitrary")), name=name)(*operands, *shards, *fulls)


def _conv_gate_bwd(ug, uv, dact, cw, cb, layer, name):
    seq, f = ug.shape
    tc = 256
    tr = _pick(seq, (512, 256))
    nc, nr = f // tc, seq // tr
    n = tr + 2 * HALO

    def body(g_ref, gp_ref, gn_ref, v_ref, vp_ref, vn_ref, d_ref, dp_ref, dn_ref, wg_ref, wv_ref, bg_ref, bv_ref,
             dug_ref, duv_ref, dwg_ref, dwv_ref):
        r = pl.program_id(1)
        eg = _extend(g_ref, gp_ref, gn_ref, r, nr)
        ev = _extend(v_ref, vp_ref, vn_ref, r, nr)
        da = _extend(d_ref, dp_ref, dn_ref, r, nr)
        eg3 = (pltpu.roll(eg, 1, axis=0), eg, pltpu.roll(eg, n - 1, axis=0))
        ev3 = (pltpu.roll(ev, 1, axis=0), ev, pltpu.roll(ev, n - 1, axis=0))
        cg = eg3[0] * wg_ref[0:1, :] + eg3[1] * wg_ref[1:2, :] + eg3[2] * wg_ref[2:3, :] + bg_ref[...]
        cv = ev3[0] * wv_ref[0:1, :] + ev3[1] * wv_ref[1:2, :] + ev3[2] * wv_ref[2:3, :] + bv_ref[...]
        sg = jax.nn.sigmoid(cg)
        dcv = da * (cg * sg)
        dcg = da * cv * (sg * (1.0 + cg * (1.0 - sg)))

        def back(dc, e3, w_ref, du_ref, dw_ref):
            du = (pltpu.roll(dc, n - 1, axis=0) * w_ref[0:1, :] + dc * w_ref[1:2, :]
                  + pltpu.roll(dc, 1, axis=0) * w_ref[2:3, :])
            du_ref[...] = du[HALO:HALO + tr].astype(du_ref.dtype)
            dcm = dc[HALO:HALO + tr]
            taps = [jnp.sum(dcm * e[HALO:HALO + tr], axis=0, keepdims=True) for e in e3] + [
                    jnp.sum(dcm, axis=0, keepdims=True)]
            part = jnp.concatenate(taps + [jnp.zeros((4, tc), F32)], axis=0)

            @pl.when(r == 0)
            def _():
                dw_ref[...] = part

            @pl.when(r > 0)
            def _():
                dw_ref[...] += part

        back(dcg, eg3, wg_ref, dug_ref, dwg_ref)
        back(dcv, ev3, wv_ref, duv_ref, dwv_ref)

    w_spec = lambda shift: pl.BlockSpec((None, 3, tc), lambda j, r: (layer, 0, j + shift))
    b_spec = lambda shift: pl.BlockSpec((None, 1, tc), lambda j, r: (layer, 0, j + shift))
    out_rows = pl.BlockSpec((tr, tc), lambda j, r: (r, j))
    out_acc = pl.BlockSpec((8, tc), lambda j, r: (0, j))
    return pl.pallas_call(
        body,
        out_shape=(jax.ShapeDtypeStruct((seq, f), BF16), jax.ShapeDtypeStruct((seq, f), BF16),
                   jax.ShapeDtypeStruct((8, f), F32), jax.ShapeDtypeStruct((8, f), F32)),
        grid=(nc, nr),
        in_specs=_halo_specs(tr, tc, seq, lambda j: j) * 3 + [w_spec(0), w_spec(nc), b_spec(0), b_spec(nc)],
        out_specs=(out_rows, out_rows, out_acc, out_acc),
        compiler_params=_cp(("parallel", "arbitrary")), name=name)(ug, ug, ug, uv, uv, uv, dact, dact, dact, cw, cw, cb, cb)


def _pool_count(g, r, tr, n, seq):
    half = jnp.left_shift(1, g)
    t = r * tr - HALO + lax.broadcasted_iota(jnp.int32, (n, 1), 0)
    cnt = jnp.minimum(t + half, seq) - jnp.maximum(t - half, 0)
    return jnp.maximum(cnt, 1).astype(F32)


def _by_group(g, levels):
    out = levels[3]
    for i in (2, 1, 0):
        out = jnp.where(g == i, levels[i], out)
    return out


def _pool_mixed(e, g, cnt, n):
    w2 = e + pltpu.roll(e, 1, axis=0)
    w4 = pltpu.roll(w2, 1, axis=0) + pltpu.roll(w2, n - 1, axis=0)
    w8 = pltpu.roll(w4, 2, axis=0) + pltpu.roll(w4, n - 2, axis=0)
    w16 = pltpu.roll(w8, 4, axis=0) + pltpu.roll(w8, n - 4, axis=0)
    return _by_group(g, (w2, w4, w8, w16)) / cnt - e


def _pool_fwd(hp, xres, pw, scale, name, plan, shards, fulls):
    seq, d = hp.shape
    tc = POOL_GROUP_W
    tr = _pick(seq, (512, 256))
    nr = seq // tr
    n = tr + 2 * HALO
    nt = plan.nt

    def body(h_ref, hp_ref, hn_ref, x_ref, w_ref, s_ref, *rest):
        o_ref = rest[2 * nt]
        plan.bind(rest[:nt], rest[2 * nt + 1:3 * nt + 1], *rest[3 * nt + 1:])
        g, r = pl.program_id(0), pl.program_id(1)

        def compute():
            e = _extend(h_ref, hp_ref, hn_ref, r, nr)
            mixed = _pool_mixed(e, g, _pool_count(g, r, tr, n, seq), n)[HALO:HALO + tr]
            y = jnp.dot(mixed.astype(BF16), w_ref[...], preferred_element_type=F32)
            o_ref[...] = x_ref[...] + y * s_ref[...]

        _hosted(plan, POOL_GROUPS * nr, g * nr + r, compute)

    operands = (hp, hp, hp, xres, pw, scale)
    return pl.pallas_call(
        body, out_shape=(jax.ShapeDtypeStruct((seq, d), F32), *plan.out_shape), grid=(POOL_GROUPS, nr),
        in_specs=_halo_specs(tr, tc, seq, lambda j: j) + [
            pl.BlockSpec((tr, tc), lambda j, r: (r, j)), pl.BlockSpec((None, tc, tc), lambda j, r: (j, 0, 0)),
            pl.BlockSpec((1, tc), lambda j, r: (0, j))] + [ANY] * (2 * nt),
        out_specs=(pl.BlockSpec((tr, tc), lambda j, r: (r, j)), *([ANY] * nt)), scratch_shapes=plan.scratch,
        input_output_aliases={len(operands) + nt + t: 1 + t for t in range(nt)},
        compiler_params=_cp(("arbitrary", "arbitrary")), name=name)(*operands, *shards, *fulls)


def _pool_bwd(hp, dy, pw, scale, name):
    seq, d = hp.shape
    tc = POOL_GROUP_W
    tr = _pick(seq, (512, 256))
    nr = seq // tr
    n = tr + 2 * HALO

    def body(h_ref, hp_ref, hn_ref, d_ref, dp_ref, dn_ref, w_ref, s_ref, dh_ref, dw_ref, ds_ref):
        g, r = pl.program_id(0), pl.program_id(1)
        cnt = _pool_count(g, r, tr, n, seq)
        e = _extend(h_ref, hp_ref, hn_ref, r, nr)
        mixed = _pool_mixed(e, g, cnt, n)[HALO:HALO + tr].astype(BF16)
        dye = _extend(d_ref, dp_ref, dn_ref, r, nr)
        dyp = (dye * s_ref[...]).astype(BF16)
        dmixed = lax.dot_general(dyp, w_ref[...], (((1,), (1,)), ((), ())), preferred_element_type=F32)
        dwin = dmixed / cnt
        m2 = dwin + pltpu.roll(dwin, n - 1, axis=0)
        m4 = pltpu.roll(m2, 1, axis=0) + pltpu.roll(m2, n - 1, axis=0)
        m8 = pltpu.roll(m4, 2, axis=0) + pltpu.roll(m4, n - 2, axis=0)
        m16 = pltpu.roll(m8, 4, axis=0) + pltpu.roll(m8, n - 4, axis=0)
        dh_ref[...] = (_by_group(g, (m2, m4, m8, m16)) - dmixed)[HALO:HALO + tr]
        ypre = jnp.dot(mixed, w_ref[...], preferred_element_type=F32)
        dsp = jnp.sum(d_ref[...] * ypre, axis=0, keepdims=True)
        dwp = lax.dot_general(mixed, dyp[HALO:HALO + tr], (((0,), (0,)), ((), ())), preferred_element_type=F32)

        @pl.when(r == 0)
        def _():
            dw_ref[...] = dwp
            ds_ref[...] = dsp

        @pl.when(r > 0)
        def _():
            dw_ref[...] += dwp
            ds_ref[...] += dsp

    return pl.pallas_call(
        body,
        out_shape=(jax.ShapeDtypeStruct((seq, d), F32), jax.ShapeDtypeStruct((POOL_GROUPS, tc, tc), F32),
                   jax.ShapeDtypeStruct((1, d), F32)),
        grid=(POOL_GROUPS, nr),
        in_specs=_halo_specs(tr, tc, seq, lambda j: j) * 2 + [
            pl.BlockSpec((None, tc, tc), lambda j, r: (j, 0, 0)), pl.BlockSpec((1, tc), lambda j, r: (0, j))],
        out_specs=(pl.BlockSpec((tr, tc), lambda j, r: (r, j)), pl.BlockSpec((None, tc, tc), lambda j, r: (j, 0, 0)),
                   pl.BlockSpec((1, tc), lambda j, r: (0, j))),
        compiler_params=_cp(("parallel", "arbitrary")), name=name)(hp, hp, hp, dy, dy, dy, pw, scale)


def _adamw_math(w, g, m, v):
    m = ADAM_B1 * m + (1.0 - ADAM_B1) * g
    v = ADAM_B2 * v + (1.0 - ADAM_B2) * (g * g)
    m_hat = m / (1.0 - ADAM_B1 ** ADAM_STEP)
    v_hat = v / (1.0 - ADAM_B2 ** ADAM_STEP)
    delta = -ADAM_LR * (m_hat / (jnp.sqrt(v_hat) + ADAM_EPS) + ADAM_WD * w)
    return delta, m, v


def _adamw(w, ga, gb, m, v, name):
    rows, cols = w.shape
    tr = _pick(rows, (256, 128, 64, 32, 16, 8))
    two = gb is not None

    def body(*refs):
        if two:
            w_ref, ga_ref, gb_ref, m_ref, v_ref, g_out, d_out, m_out, v_out = refs
            g = ga_ref[...] + gb_ref[...]
        else:
            w_ref, ga_ref, m_ref, v_ref, g_out, d_out, m_out, v_out = refs
            g = ga_ref[...]
        delta, m, v = _adamw_math(w_ref[...], g, m_ref[...], v_ref[...])
        g_out[...] = g
        d_out[...] = delta
        m_out[...] = m
        v_out[...] = v

    spec = pl.BlockSpec((tr, cols), lambda i: (i, 0))
    ops = [w, ga] + ([gb] if two else []) + [m, v]
    return pl.pallas_call(
        body, out_shape=tuple(jax.ShapeDtypeStruct((rows, cols), F32) for _ in range(4)), grid=(rows // tr,),
        in_specs=[spec] * len(ops), out_specs=(spec,) * 4, compiler_params=_cp(("parallel",)), name=name)(*ops)


def _sum4(parts, name):
    _, rows, cols = parts.shape
    tr = _pick(rows, (256, 128, 64, 32, 16))

    def body(p_ref, o_ref):
        acc = p_ref[0].astype(F32)
        for kk in range(1, 4):
            acc = acc + p_ref[kk].astype(F32)
        o_ref[...] = acc

    return pl.pallas_call(
        body, out_shape=jax.ShapeDtypeStruct((rows, cols), F32), grid=(rows // tr,),
        in_specs=[pl.BlockSpec((4, tr, cols), lambda i: (0, i, 0))], out_specs=pl.BlockSpec((tr, cols), lambda i: (i, 0)),
        compiler_params=_cp(("parallel",)), name=name)(parts)


def _place():
    x, y, c = lax.axis_index("x"), lax.axis_index("y"), lax.axis_index("c")
    chips = [(1 - x, y), (x, 1 - y), (1 - x, 1 - y)]
    return x, y, c, chips


def _window(ref, axis, j, size, c=None, half=None, lead=(), layers=slice(None)):
    if axis == "r":
        if c is None:
            return ref.at[lead + (layers, pl.ds(pl.multiple_of(j * size, 32), size), slice(None))]
        return ref.at[lead + (layers, pl.ds(pl.multiple_of(j * size + c * half, 32), half), slice(None))]
    cols = pl.ds(pl.multiple_of(j * size, LANES), size)
    if c is None:
        return ref.at[lead + (layers, slice(None), cols)]
    return ref.at[lead + (layers, pl.ds(pl.multiple_of(c * half, 32), half), cols)]


class _Gather:
    def __init__(self, shards, axes, layers=None):
        self.nt, self.axes = len(shards), axes
        self.layers = layers or [slice(None)] * self.nt
        self.out_shape, self.sizes, self.halves = [], [], []
        for s, ax in zip(shards, axes):
            l, rs, cs = s.shape
            self.out_shape.append(jax.ShapeDtypeStruct((l, 4 * rs, cs) if ax == "r" else (l, rs, 4 * cs), s.dtype))
            self.sizes.append(rs if ax == "r" else cs)
            self.halves.append(rs // 2)
        self.scratch = [pltpu.SemaphoreType.DMA((6 * self.nt,)), pltpu.SemaphoreType.DMA((6 * self.nt,)),
                        pltpu.SemaphoreType.DMA((self.nt,))]

    def bind(self, src, dst, send_sems, recv_sems, local_sems):
        self.src, self.dst, self.send_sems, self.recv_sems, self.local_sems = src, dst, send_sems, recv_sems, local_sems

    def _win(self, t, j, core=None):
        return _window(self.dst[t], self.axes[t], j, self.sizes[t], core, self.halves[t], layers=self.layers[t])

    def _ici(self, t, kk, origin):
        _, _, c, chips = _place()
        px, py = chips[kk]
        half = self.src[t].at[self.layers[t], pl.ds(pl.multiple_of(c * self.halves[t], 16), self.halves[t]), :]
        return pltpu.make_async_remote_copy(
            src_ref=half, dst_ref=self._win(t, origin, c), send_sem=self.send_sems.at[t * 3 + kk],
            recv_sem=self.recv_sems.at[t * 3 + kk], device_id=(px, py, c), device_id_type=MESH)

    def _d2d(self, t, kk, origin, core):
        x, y, c, _ = _place()
        k2 = 3 * self.nt + t * 3 + kk
        return pltpu.make_async_remote_copy(
            src_ref=self._win(t, origin, core), dst_ref=self._win(t, origin, core), send_sem=self.send_sems.at[k2],
            recv_sem=self.recv_sems.at[k2], device_id=(x, y, 1 - c), device_id_type=MESH)

    def _local(self, t):
        x, y, _, _ = _place()
        return pltpu.make_async_copy(self.src[t].at[self.layers[t]], self._win(t, 2 * x + y), self.local_sems.at[t])

    def _each(self):
        _, _, _, chips = _place()
        for t in range(self.nt):
            for kk in range(3):
                px, py = chips[kk]
                yield t, kk, 2 * px + py

    def start(self):
        x, y, _, _ = _place()
        for t in range(self.nt):
            self._local(t).start()
        for t, kk, _ in self._each():
            self._ici(t, kk, 2 * x + y).start()

    def forward(self):
        _, _, c, _ = _place()
        for t, kk, origin in self._each():
            self._ici(t, kk, origin).wait_recv()
            self._d2d(t, kk, origin, c).start()

    def finish(self):
        x, y, c, _ = _place()
        for t, kk, origin in self._each():
            self._d2d(t, kk, origin, 1 - c).wait_recv()
        for t, kk, origin in self._each():
            self._ici(t, kk, 2 * x + y).wait_send()
            self._d2d(t, kk, origin, c).wait_send()
        for t in range(self.nt):
            self._local(t).wait()


class _Scatter:
    def __init__(self, grads, axes):
        self.nt, self.axes = len(grads), axes
        self.out_shape, self.sizes = [], []
        for gr, ax in zip(grads, axes):
            l, r, cc = gr.shape
            self.out_shape.append(jax.ShapeDtypeStruct((4, l, r // 4, cc) if ax == "r" else (4, l, r, cc // 4), gr.dtype))
            self.sizes.append(r // 4 if ax == "r" else cc // 4)
        self.scratch = [pltpu.SemaphoreType.DMA((3 * self.nt,)), pltpu.SemaphoreType.DMA((3 * self.nt,)),
                        pltpu.SemaphoreType.DMA((self.nt,))]

    def bind(self, src, dst, send_sems, recv_sems, local_sems):
        self.src, self.dst, self.send_sems, self.recv_sems, self.local_sems = src, dst, send_sems, recv_sems, local_sems

    def _copy(self, t, kk, slot):
        x, y, c, chips = _place()
        px, py = chips[kk]
        return pltpu.make_async_remote_copy(
            src_ref=_window(self.src[t], self.axes[t], 2 * px + py, self.sizes[t]), dst_ref=self.dst[t].at[slot],
            send_sem=self.send_sems.at[t * 3 + kk], recv_sem=self.recv_sems.at[t * 3 + kk],
            device_id=(px, py, c), device_id_type=MESH)

    def _local(self, t):
        x, y, _, _ = _place()
        me = 2 * x + y
        return pltpu.make_async_copy(_window(self.src[t], self.axes[t], me, self.sizes[t]), self.dst[t].at[me],
                                     self.local_sems.at[t])

    def start(self):
        x, y, _, _ = _place()
        for t in range(self.nt):
            self._local(t).start()
            for kk in range(3):
                self._copy(t, kk, 2 * x + y).start()

    def finish(self):
        _, _, _, chips = _place()
        for t in range(self.nt):
            for kk in range(3):
                px, py = chips[kk]
                self._copy(t, kk, 2 * px + py).wait_recv()
        for t in range(self.nt):
            for kk in range(3):
                px, py = chips[kk]
                self._copy(t, kk, 2 * px + py).wait_send()
            self._local(t).wait()


def _comm_call(plan, operands, name):
    nt = plan.nt

    def body(*refs):
        plan.bind(refs[:nt], refs[nt:2 * nt], *refs[2 * nt:])
        plan.start()
        if hasattr(plan, "forward"):
            plan.forward()
        plan.finish()

    return pl.pallas_call(body, out_shape=tuple(plan.out_shape), in_specs=[ANY] * nt, out_specs=tuple([ANY] * nt),
                          scratch_shapes=plan.scratch, name=name)(*operands)


class _GatherAll:
    FLIPS = [f for f in itertools.product((0, 1), repeat=3) if any(f)]

    def __init__(self, pack):
        self.nt = 1
        self.out_shape = [jax.ShapeDtypeStruct((8,) + pack.shape, pack.dtype)]
        self.scratch = [pltpu.SemaphoreType.DMA((7,)), pltpu.SemaphoreType.DMA((7,)), pltpu.SemaphoreType.DMA((1,))]

    def bind(self, src, dst, send_sems, recv_sems, local_sems):
        self.src, self.dst, self.send_sems, self.recv_sems, self.local_sems = src[0], dst[0], send_sems, recv_sems, local_sems

    def _copy(self, kk, mine):
        x, y, c, _ = _place()
        px, py, pc = (1 - v if fl else v for v, fl in zip((x, y, c), self.FLIPS[kk]))
        slot = 4 * x + 2 * y + c if mine else 4 * px + 2 * py + pc
        return pltpu.make_async_remote_copy(src_ref=self.src, dst_ref=self.dst.at[slot], send_sem=self.send_sems.at[kk],
                                            recv_sem=self.recv_sems.at[kk], device_id=(px, py, pc), device_id_type=MESH)

    def _local(self):
        x, y, c, _ = _place()
        return pltpu.make_async_copy(self.src, self.dst.at[4 * x + 2 * y + c], self.local_sems.at[0])

    def start(self):
        self._local().start()
        for kk in range(7):
            self._copy(kk, True).start()

    def finish(self):
        for kk in range(7):
            self._copy(kk, False).wait_recv()
        for kk in range(7):
            self._copy(kk, True).wait_send()
        self._local().wait()


class _Swap:
    def __init__(self, arrs):
        self.nt = len(arrs)
        self.out_shape = [jax.ShapeDtypeStruct(a.shape, a.dtype) for a in arrs]
        self.scratch = [pltpu.SemaphoreType.DMA((self.nt,)), pltpu.SemaphoreType.DMA((self.nt,))]

    def bind(self, src, dst, send_sems, recv_sems):
        self.src, self.dst, self.send_sems, self.recv_sems = src, dst, send_sems, recv_sems

    def _copy(self, t):
        x, y, c, _ = _place()
        return pltpu.make_async_remote_copy(src_ref=self.src[t], dst_ref=self.dst[t], send_sem=self.send_sems.at[t],
                                            recv_sem=self.recv_sems.at[t], device_id=(x, y, 1 - c), device_id_type=MESH)

    def start(self):
        for t in range(self.nt):
            self._copy(t).start()

    def finish(self):
        for t in range(self.nt):
            self._copy(t).wait()


def _gather8(pack, with_sum, name):
    rows = pack.shape[0]
    flips = [f for f in itertools.product((0, 1), repeat=3) if any(f)]

    def body(p_ref, all_ref, *rest):
        if with_sum:
            sum_ref, send_sems, recv_sems = rest
        else:
            send_sems, recv_sems = rest
        x, y, c, _ = _place()
        me = 4 * x + 2 * y + c

        def peer(f):
            return tuple(1 - v if fl else v for v, fl in zip((x, y, c), f))

        all_ref[me] = p_ref[...]
        cps = []
        for kk, f in enumerate(flips):
            cp = pltpu.make_async_remote_copy(src_ref=p_ref, dst_ref=all_ref.at[me], send_sem=send_sems.at[kk],
                                              recv_sem=recv_sems.at[kk], device_id=peer(f), device_id_type=MESH)
            cp.start()
            cps.append(cp)
        for kk, f in enumerate(flips):
            px, py, pc = peer(f)
            pltpu.make_async_remote_copy(src_ref=p_ref, dst_ref=all_ref.at[4 * px + 2 * py + pc],
                                         send_sem=send_sems.at[kk], recv_sem=recv_sems.at[kk], device_id=peer(f),
                                         device_id_type=MESH).wait_recv()
        for cp in cps:
            cp.wait_send()
        if with_sum:
            acc = all_ref[0]
            for d in range(1, 8):
                acc = acc + all_ref[d]
            sum_ref[...] = acc

    vm = pl.BlockSpec(memory_space=pltpu.VMEM)
    out_shape = [jax.ShapeDtypeStruct((8, rows, LANES), F32)] + ([jax.ShapeDtypeStruct((rows, LANES), F32)] if with_sum else [])
    return pl.pallas_call(
        body, out_shape=tuple(out_shape), in_specs=[vm], out_specs=tuple([vm] * len(out_shape)),
        scratch_shapes=[pltpu.SemaphoreType.DMA((7,)), pltpu.SemaphoreType.DMA((7,))], name=name)(pack)


def _pack(arrs):
    flat = jnp.concatenate([a.reshape(-1).astype(F32) for a in arrs])
    rows = -(-flat.shape[0] // (8 * LANES)) * 8
    return jnp.pad(flat, (0, rows * LANES - flat.shape[0])).reshape(rows, LANES)


def _unpack(flat, shapes):
    out, pos = [], 0
    for shp in shapes:
        size = 1
        for s in shp:
            size *= s
        out.append(flat[pos:pos + size].reshape(shp))
        pos += size
    return out


BIG = ("attn_w_qkv", "attn_w_o", "pool_w", "xattn_w_q", "xattn_w_kv", "xattn_w_o", "ffn_w_up", "ffn_w_down")
BIG_AXIS = ("c", "r", "r", "r", "c", "r", "c", "r")
SMALL_REPL = ("attn_norm", "attn_q_gain", "attn_k_gain", "xattn_norm", "mem_norm", "ffn_norm", "ffn_conv_b", "final_norm")
SMALL_SHARD = ("pool_norm", "pool_scale", "ffn_conv_w")
ORDER = ("attn_norm", "attn_w_qkv", "attn_q_gain", "attn_k_gain", "attn_w_o", "pool_norm", "pool_w", "pool_scale",
         "xattn_norm", "mem_norm", "xattn_w_q", "xattn_w_kv", "xattn_w_o", "ffn_norm", "ffn_w_up", "ffn_conv_w",
         "ffn_conv_b", "ffn_w_down", "final_norm")


def _step(x, mem, tgt, w, m, v):
    seq, d = x.shape
    xi, yi, ci = lax.axis_index("x"), lax.axis_index("y"), lax.axis_index("c")
    chip = 2 * xi + yi
    dff = w["ffn_w_down"].shape[1] * 4
    n_layers = w["ffn_norm"].shape[0]

    def as3d(a):
        return a.reshape(a.shape[-3:])
    shards = [as3d(w[nm]).astype(BF16) for nm in BIG]
    small_in = [w[nm] for nm in SMALL_SHARD]
    small_pack = _pack(small_in)
    conv_b = w["ffn_conv_b"].reshape(n_layers, 1, -1)
    tabs = _rope_tables(seq)
    qg2 = jnp.tile(w["attn_q_gain"], (1, 2))
    kg2 = jnp.tile(w["attn_k_gain"], (1, 2))
    mm = functools.partial(_mm)

    saved = {}
    x0 = x
    h0, wq = _rms_fwd(x0, w["attn_norm"], BF16, "rms_attn", _Gather(shards[:1], BIG_AXIS[:1]), shards[:1])
    qkv = mm(h0, wq, "nn", b_l=0, out_dtype=F32, name="mm_qkv")
    q_r, k_r, k_t, v_b, v_t, small_all = _qk_prep(qkv, qg2, kg2, tabs, _GatherAll(small_pack), [small_pack], "qk_prep")
    per_chip = [_unpack(small_all[2 * j].reshape(-1), [a.shape for a in small_in]) for j in range(4)]
    pool_norm, pool_scale, conv_w = (jnp.concatenate([per_chip[j][i] for j in range(4)], axis=-1) for i in range(3))
    first = [slice(None)] * 5 + [slice(0, 1)] * 2
    o_at, lse, wo, wp, wxq, wxkv, wxo, wup, wdn = _flash_fwd(
        q_r, k_r, v_t, _Gather(shards[1:], BIG_AXIS[1:], first), shards[1:], "flash_fwd")
    ffn_w = {"up": wup, "down": wdn}
    x1, hq0 = mm(o_at, wo, "nn", b_l=0, res=x0, out_dtype=F32, norm_out=(w["xattn_norm"][0:1], BF16), name="mm_attn_o")

    def xattn_fwd(l, xin, hq):
        mn = _rms_fwd(mem, w["mem_norm"][l:l + 1], BF16, f"rms_mem{l}")
        xq = mm(hq, wxq, "nn", b_l=l, scale=X_HEAD_DIM ** -0.5, out_dtype=BF16, name=f"mm_xq{l}")
        kv = mm(mn, wxkv, "nn", b_l=l, out_dtype=BF16, name=f"mm_xkv{l}")
        xo = _xattn_fwd(xq, kv, f"xattn_fwd{l}")
        saved[f"x{l}"] = (hq, mn, xq, kv, xo)
        return mm(xo, wxo, "nn", b_l=l, res=xin, out_dtype=F32, norm_out=(w["ffn_norm"][l:l + 1], BF16), name=f"mm_xo{l}")

    def ffn_fwd(l, xin, hf, norm_out):
        ug = mm(hf, ffn_w["up"], "nn", b_l=l, n=dff, out_dtype=BF16, name=f"mm_up_g{l}")
        uv = mm(hf, ffn_w["up"], "nn", b_l=l, n=dff, b_off=(0, dff), out_dtype=BF16, name=f"mm_up_v{l}")
        if l == 0:
            rest = _Gather(shards[6:7], BIG_AXIS[6:7], [slice(1, 2)])
            act, ffn_w["up"] = _conv_gate_fwd(ug, uv, conv_w, conv_b, l, f"conv_gate{l}", rest, shards[6:7], [ffn_w["up"]])
        else:
            act = _conv_gate_fwd(ug, uv, conv_w, conv_b, l, f"conv_gate{l}")
        saved[f"f{l}"] = (hf, ug, uv, act)
        return mm(act, ffn_w["down"], "nn", b_l=l, res=xin, out_dtype=F32, norm_out=norm_out, name=f"mm_down{l}")

    x2, hf0 = xattn_fwd(0, x1, hq0)
    x3, hp = ffn_fwd(0, x2, hf0, (pool_norm, F32))
    x4, ffn_w["down"] = _pool_fwd(hp, x3, wp, pool_scale, "pool_fwd", _Gather(shards[7:], BIG_AXIS[7:], [slice(1, 2)]),
                                  shards[7:], [ffn_w["down"]])
    x5, hf1 = xattn_fwd(1, x4, _rms_fwd(x4, w["xattn_norm"][1:2], BF16, "rms_xq1"))
    xs = [x0, x1, x2, x3, x4, x5, ffn_fwd(1, x5, hf1, None)]
    dres, g_final, loss = _final_loss(xs[6], w["final_norm"].reshape(1, d), tgt, "final_loss")

    grads = {}
    gbuf = {}

    def dw(nm, a, b, layer, full, off=(0, 0), n=None, tn=None):
        gbuf[nm] = _mm(a, b, "tn", out_dtype=BF16, out_full=full, out_l=layer, out_off=off, n=n, tn=tn,
                       alias=gbuf.get(nm), name=f"dw_{nm}{layer}_{off[1]}")

    def ffn_bwd(l, xin, dres):
        hf, ug, uv, act = saved[f"f{l}"]
        wup, wdn = ffn_w["up"], ffn_w["down"]
        dw("ffn_w_down", act, dres, l, wdn.shape)
        dact = _mm(dres, wdn, "nt", b_l=l, out_dtype=BF16, name=f"mm_dact{l}")
        dug, duv, dwg, dwv = _conv_gate_bwd(ug, uv, dact, conv_w, conv_b, l, f"conv_gate_bwd{l}")
        dw("ffn_w_up", hf, dug, l, wup.shape, tn=1408)
        dw("ffn_w_up", hf, duv, l, wup.shape, off=(0, dff), tn=1408)
        dhf = _mm(dug, wup, "nt", b_l=l, n=d, out_dtype=F32, name=f"mm_dhf_g{l}")
        dres, dg = _mm(duv, wup, "nt", b_l=l, n=d, b_off=(0, dff), res=dhf, out_dtype=F32, tm=256,
                       norm_bwd=(xin, w["ffn_norm"][l:l + 1], dres), name=f"mm_dhf_v{l}")
        return dres, dg, jnp.concatenate([dwg[:3], dwv[:3]], axis=1), jnp.concatenate([dwg[3], dwv[3]], axis=0)

    def xattn_bwd(l, xin, dres):
        hq, mn, xq, kv, xo = saved[f"x{l}"]
        dw("xattn_w_o", xo, dres, l, wxo.shape)
        dxo = _mm(dres, wxo, "nt", b_l=l, out_dtype=BF16, name=f"mm_dxo{l}")
        dq, dkv = _xattn_bwd(xq, kv, dxo, f"xattn_bwd{l}")
        dw("xattn_w_q", hq, dq, l, wxq.shape)
        dw("xattn_w_kv", mn, dkv, l, wxkv.shape)
        dmn = _mm(dkv, wxkv, "nt", b_l=l, out_dtype=F32, name=f"mm_dmn{l}")
        _, dg_mem = _rms_bwd(mem, w["mem_norm"][l:l + 1], dmn, None, f"rms_mem_bwd{l}")
        dres, dg = _mm(dq, wxq, "nt", b_l=l, out_dtype=F32, norm_bwd=(xin, w["xattn_norm"][l:l + 1], dres),
                       name=f"mm_dhq{l}")
        return dres, dg, dg_mem

    g_ffn, g_xn, g_mn, g_cw, g_cb = [None] * n_layers, [None] * n_layers, [None] * n_layers, [None] * n_layers, [None] * n_layers
    dres, g_ffn[1], g_cw[1], g_cb[1] = ffn_bwd(1, xs[5], dres)
    dres, g_xn[1], g_mn[1] = xattn_bwd(1, xs[4], dres)
    dhp, g_pw, g_pscale = _pool_bwd(hp, dres, wp, pool_scale, "pool_bwd")
    dres, g_pnorm = _rms_bwd(xs[3], pool_norm, dhp, dres, "rms_pool_bwd")
    dres, g_ffn[0], g_cw[0], g_cb[0] = ffn_bwd(0, xs[2], dres)
    dres, g_xn[0], g_mn[0] = xattn_bwd(0, xs[1], dres)
    dw("attn_w_o", o_at, dres, 0, wo.shape)
    do = _mm(dres, wo, "nt", b_l=0, out_dtype=BF16, name="mm_do")
    gbuf["pool_w"] = g_pw.astype(BF16)
    early = [gbuf[nm] for nm in BIG[1:]]
    dq_r, dk_r, dv, *recv_early = _flash_bwd(q_r, k_r, k_t, v_b, do, o_at, lse, _Scatter(early, BIG_AXIS[1:]), early,
                                             "flash_bwd")
    def sum4(nm, rc):
        return _sum4(rc.reshape(4, -1, rc.shape[-1]), f"sum4_{nm}")
    sums_early = [sum4(nm, rc) for nm, rc in zip(BIG[1:], recv_early)]
    dqkv, dqg, dkg, *others_early = _qk_prep_bwd(qkv, dq_r, dk_r, dv, qg2, kg2, tabs, _Swap(sums_early), sums_early,
                                                 "qk_prep_bwd")
    dw("attn_w_qkv", h0, dqkv, 0, wq.shape)
    grad_x, g_an = _mm(dqkv, wq, "nt", b_l=0, out_dtype=F32, norm_bwd=(x0, w["attn_norm"], dres), name="mm_dh0")

    small_g = {
        "attn_norm": g_an, "attn_q_gain": dqg[:, :HEAD_DIM] + dqg[:, HEAD_DIM:], "attn_k_gain": dkg[:, :HEAD_DIM] + dkg[:, HEAD_DIM:],
        "xattn_norm": jnp.concatenate(g_xn, axis=0), "mem_norm": jnp.concatenate(g_mn, axis=0),
        "ffn_norm": jnp.concatenate(g_ffn, axis=0), "ffn_conv_b": jnp.stack(g_cb, axis=0), "final_norm": g_final.reshape(d),
        "pool_norm": g_pnorm, "pool_scale": g_pscale, "ffn_conv_w": jnp.stack(g_cw, axis=0)}
    names = SMALL_REPL + SMALL_SHARD
    _, total = _gather8(_pack([loss[0, :1]] + [small_g[nm] for nm in names]), True, "reduce_small")
    parts = _unpack(total.reshape(-1), [(1,)] + [small_g[nm].shape for nm in names])
    loss_out = parts[0][0]
    for nm, g in zip(names, parts[1:]):
        if nm in SMALL_SHARD:
            size = w[nm].shape[-1]
            g = lax.dynamic_slice_in_dim(g, chip * size, size, axis=g.ndim - 1)
        grads[nm] = g.reshape(w[nm].shape)

    packed = [_pack([src[nm] for nm in names]) for src in (w, grads, m, v)]
    _, sd, sm, sv = _adamw(packed[0], packed[1], None, packed[2], packed[3], "adamw_small")
    shapes = [w[nm].shape for nm in names]
    delta = dict(zip(names, _unpack(sd.reshape(-1), shapes)))
    new_m = dict(zip(names, _unpack(sm.reshape(-1), shapes)))
    new_v = dict(zip(names, _unpack(sv.reshape(-1), shapes)))

    late = [gbuf[nm] for nm in BIG[:1]]
    sums_late = [sum4(BIG[0], _comm_call(_Scatter(late, BIG_AXIS[:1]), late, "scatter_qkv")[0])]
    others_late = _comm_call(_Swap(sums_late), sums_late, "swap_qkv")
    for nm, mine, other in zip(BIG, sums_late + sums_early, list(others_late) + others_early):
        cols = mine.shape[-1]
        outs = _adamw(w[nm].reshape(-1, cols), mine, other, m[nm].reshape(-1, cols), v[nm].reshape(-1, cols), f"adamw_{nm}")
        grads[nm], delta[nm], new_m[nm], new_v[nm] = (o.reshape(w[nm].shape) for o in outs)

    return loss_out, grad_x, grads, delta, new_m, new_v


def kernel(x, mem, attn_norm, attn_w_qkv, attn_q_gain, attn_k_gain, attn_w_o, pool_norm, pool_w, pool_scale, xattn_norm, mem_norm, xattn_w_q, xattn_w_kv, xattn_w_o, ffn_norm, ffn_w_up, ffn_conv_w, ffn_conv_b, ffn_w_down, final_norm, loss_target, m_attn_norm, m_attn_w_qkv, m_attn_q_gain, m_attn_k_gain, m_attn_w_o, m_pool_norm, m_pool_w, m_pool_scale, m_xattn_norm, m_mem_norm, m_xattn_w_q, m_xattn_w_kv, m_xattn_w_o, m_ffn_norm, m_ffn_w_up, m_ffn_conv_w, m_ffn_conv_b, m_ffn_w_down, m_final_norm, v_attn_norm, v_attn_w_qkv, v_attn_q_gain, v_attn_k_gain, v_attn_w_o, v_pool_norm, v_pool_w, v_pool_scale, v_xattn_norm, v_mem_norm, v_xattn_w_q, v_xattn_w_kv, v_xattn_w_o, v_ffn_norm, v_ffn_w_up, v_ffn_conv_w, v_ffn_conv_b, v_ffn_w_down, v_final_norm):
    given = dict(locals())
    w = {nm: given[nm] for nm in ORDER}
    m = {nm: given["m_" + nm] for nm in ORDER}
    v = {nm: given["v_" + nm] for nm in ORDER}
    seq, d = x.shape[1], x.shape[2]
    loss, grad_x, grads, delta, new_m, new_v = _step(
        x.reshape(seq, d), mem.reshape(mem.shape[1], d), loss_target.reshape(seq, d), w, m, v)
    return (loss, grad_x.reshape(x.shape), *[grads[nm] for nm in ORDER], *[delta[nm] for nm in ORDER],
            *[new_m[nm] for nm in ORDER], *[new_v[nm] for nm in ORDER])
```

```python
import functools
import itertools

import jax
import jax.numpy as jnp
from jax import lax
from jax.experimental import pallas as pl
from jax.experimental.pallas import tpu as pltpu

F32, BF16 = jnp.float32, jnp.bfloat16
EPS = 1e-6
GRID_W = 64
ROPE_THETA = 10000.0
HEAD_DIM = 64
N_HEADS = 16
N_KV = 4
X_HEADS = 4
X_HEAD_DIM = 256
POOL_GROUPS = 4
POOL_GROUP_W = 256
HALO = 16
LANES = 128
ADAM_LR, ADAM_B1, ADAM_B2, ADAM_EPS, ADAM_WD, ADAM_STEP = 0.001, 0.9, 0.999, 1e-08, 0.01, 10
VMEM_LIMIT = 48 * 1024 * 1024
MESH = pl.DeviceIdType.MESH
NEG = -1e30
LOG2E = 1.4426950408889634
FLASH_TQ, FLASH_TK = 512, 4096
FLASH_SUB = 512
ANY = pl.BlockSpec(memory_space=pl.ANY)


def _cp(sem=None):
    return pltpu.CompilerParams(dimension_semantics=sem, vmem_limit_bytes=VMEM_LIMIT)


def _pick(n, cands):
    for c in cands:
        if c <= n and n % c == 0:
            return c
    return n


def _mm(a, b, mode, *, name, out_dtype, tm=None, tn=None, tk=None, n=None, k=None, b_l=None, b_off=(0, 0),
        res=None, scale=None, out_full=None, out_l=None, out_off=(0, 0), alias=None, norm_out=None, norm_bwd=None):
    if mode == "tn":
        K, M = a.shape
    else:
        M, K = a.shape
    bs = b.shape[-2:]
    if mode == "nn":
        K = k or K
        N = n or bs[1]
    elif mode == "nt":
        N = n or bs[0]
    else:
        N = n or bs[1]
    wide = (1408, 1024, 512, 256, 128)
    if mode == "tn":
        tm = tm or (M if M <= 1024 else _pick(M, wide))
        tk = tk or _pick(K, (2048, 1024, 512, 256, 128))
    else:
        tm = _pick(M, (tm or 512, 256, 128))
        tk = tk or (K if K <= 2816 else _pick(K, wide))
    tn = tn or (N if N <= 1536 else _pick(N, wide))
    assert M % tm == 0 and N % tn == 0 and K % tk == 0, (name, M, N, K, tm, tn, tk)
    nk = K // tk
    dims = {"nn": ((1,), (0,)), "nt": ((1,), (1,)), "tn": ((0,), (0,))}[mode]

    j_outer = nk == 1 and mode != "tn"

    def at(f):
        return (lambda j, i, kk: f(i, j, kk)) if j_outer else f

    if mode == "tn":
        a_spec = pl.BlockSpec((tk, tm), at(lambda i, j, kk: (kk, i)))
    else:
        a_spec = pl.BlockSpec((tm, tk), at(lambda i, j, kk: (i, kk)))
    if mode == "nt":
        bb, (d0, d1) = (tn, tk), (b_off[0] // tn, b_off[1] // tk)
        assert b_off[0] % tn == 0 and b_off[1] % tk == 0
        bidx = lambda i, j, kk: (j + d0, kk + d1)
    else:
        bb, (d0, d1) = (tk, tn), (b_off[0] // tk, b_off[1] // tn)
        assert b_off[0] % tk == 0 and b_off[1] % tn == 0
        bidx = lambda i, j, kk: (kk + d0, j + d1)
    if b.ndim == 3:
        b_spec = pl.BlockSpec((None,) + bb, at(lambda i, j, kk: (b_l,) + bidx(i, j, kk)))
    else:
        b_spec = pl.BlockSpec(bb, at(bidx))
    in_specs, operands = [a_spec, b_spec], [a, b]
    if res is not None:
        in_specs.append(pl.BlockSpec((tm, tn), at(lambda i, j, kk: (i, j))))
        operands.append(res)
    aliases = {}
    if alias is not None:
        aliases = {len(operands): 0}
        in_specs.append(ANY)
        operands.append(alias)
    if out_full is None:
        out_shape = jax.ShapeDtypeStruct((M, N), out_dtype)
        out_spec = pl.BlockSpec((tm, tn), at(lambda i, j, kk: (i, j)))
    else:
        assert out_off[0] % tm == 0 and out_off[1] % tn == 0
        o0, o1 = out_off[0] // tm, out_off[1] // tn
        out_shape = jax.ShapeDtypeStruct(out_full, out_dtype)
        out_spec = pl.BlockSpec((None, tm, tn), at(lambda i, j, kk: (out_l, i + o0, j + o1)))
    has_res, has_alias = res is not None, alias is not None
    grid = (N // tn, M // tm, nk) if j_outer else (M // tm, N // tn, nk)
    n_extra = 0
    if norm_out is not None or norm_bwd is not None:
        assert j_outer and tn == N and out_full is None, name
        row = pl.BlockSpec((tm, tn), at(lambda i, j, kk: (i, 0)))
        vec = pl.BlockSpec((1, tn), at(lambda i, j, kk: (0, 0)))
        if norm_out is not None:
            in_specs.append(vec)
            operands.append(norm_out[0])
            n_extra = 1
            out_shape = (out_shape, jax.ShapeDtypeStruct((M, N), norm_out[1]))
            out_spec = (out_spec, row)
        else:
            in_specs += [row, vec, row]
            operands += list(norm_bwd)
            n_extra = 3
            out_shape = (out_shape, jax.ShapeDtypeStruct((1, N), F32))
            out_spec = (out_spec, vec)
    n_out = 1 if n_extra == 0 else 2

    def body(*refs):
        a_ref, b_ref = refs[0], refs[1]
        pos = 2
        res_ref = None
        if has_res:
            res_ref = refs[pos]
            pos += 1
        if has_alias:
            pos += 1
        extra = refs[pos:pos + n_extra]
        pos += n_extra
        o_ref, acc_ref = refs[pos], refs[pos + n_out]
        kk = pl.program_id(2)
        part = lax.dot_general(a_ref[...].astype(BF16), b_ref[...].astype(BF16), (dims, ((), ())),
                               preferred_element_type=F32)

        def finish(acc):
            if scale is not None:
                acc = acc * scale
            if res_ref is not None:
                acc = acc + res_ref[...]
            if norm_out is not None:
                r = lax.rsqrt(jnp.mean(acc * acc, axis=-1, keepdims=True) + EPS)
                refs[pos + 1][...] = (acc * r * extra[0][...]).astype(refs[pos + 1].dtype)
            if norm_bwd is not None:
                x_ref, g_ref, dres_ref = extra
                dg_ref, step = refs[pos + 1], pl.program_id(1)
                xv = x_ref[...]
                r = lax.rsqrt(jnp.mean(xv * xv, axis=-1, keepdims=True) + EPS)
                nv = xv * r
                dgp = jnp.sum(acc * nv, axis=0, keepdims=True)

                @pl.when(step == 0)
                def _():
                    dg_ref[...] = dgp

                @pl.when(step > 0)
                def _():
                    dg_ref[...] += dgp

                dn = acc * g_ref[...]
                acc = dres_ref[...] + r * (dn - nv * jnp.mean(dn * nv, axis=-1, keepdims=True))
            o_ref[...] = acc.astype(o_ref.dtype)

        if nk == 1:
            finish(part)
        else:
            @pl.when(kk == 0)
            def _():
                acc_ref[...] = part

            @pl.when(jnp.logical_and(kk > 0, kk < nk - 1))
            def _():
                acc_ref[...] += part

            @pl.when(kk == nk - 1)
            def _():
                finish(acc_ref[...] + part)

    return pl.pallas_call(
        body, out_shape=out_shape, grid=grid, in_specs=in_specs, out_specs=out_spec,
        scratch_shapes=[pltpu.VMEM((tm, tn) if nk > 1 else (8, 128), F32)], input_output_aliases=aliases,
        compiler_params=_cp(("arbitrary",) * 3 if norm_bwd is not None else ("parallel", "parallel", "arbitrary")),
        name=name)(*operands)


def _hosted(plan, nsteps, step, compute):
    if plan is None:
        return compute()

    @pl.when(step == 0)
    def _():
        plan.start()

    compute()

    @pl.when(step == nsteps - 1)
    def _():
        if hasattr(plan, "forward"):
            plan.forward()
        plan.finish()


def _rms_fwd(x, gain, out_dtype, name, plan=None, sends=()):
    rows, d = x.shape
    tr = _pick(rows, (512, 256))
    nt, nsteps = (plan.nt if plan is not None else 0), rows // tr

    def body(x_ref, g_ref, *rest):
        o_ref = rest[nt]
        if plan is not None:
            plan.bind(rest[:nt], rest[nt + 1:2 * nt + 1], *rest[2 * nt + 1:])

        def compute():
            xv = x_ref[...]
            r = lax.rsqrt(jnp.mean(xv * xv, axis=-1, keepdims=True) + EPS)
            o_ref[...] = (xv * r * g_ref[...]).astype(o_ref.dtype)

        _hosted(plan, nsteps, pl.program_id(0), compute)

    out = jax.ShapeDtypeStruct((rows, d), out_dtype)
    row = pl.BlockSpec((tr, d), lambda i: (i, 0))
    in_specs = [row, pl.BlockSpec((1, d), lambda i: (0, 0))]
    if plan is None:
        return pl.pallas_call(body, out_shape=out, grid=(nsteps,), in_specs=in_specs, out_specs=row,
                              compiler_params=_cp(("parallel",)), name=name)(x, gain)
    return pl.pallas_call(
        body, out_shape=(out, *plan.out_shape), grid=(nsteps,), in_specs=in_specs + [ANY] * nt,
        out_specs=(row, *([ANY] * nt)), scratch_shapes=plan.scratch, compiler_params=_cp(("arbitrary",)),
        name=name)(x, gain, *sends)


def _rms_bwd(x, gain, dh, dres, name):
    rows, d = x.shape
    tr = _pick(rows, (512, 256))
    need_dx = dres is not None

    def body(*refs):
        if need_dx:
            x_ref, g_ref, dh_ref, dres_ref, o_ref, dg_ref = refs
        else:
            x_ref, g_ref, dh_ref, dg_ref = refs
        i = pl.program_id(0)
        xv = x_ref[...]
        dhv = dh_ref[...].astype(F32)
        r = lax.rsqrt(jnp.mean(xv * xv, axis=-1, keepdims=True) + EPS)
        nv = xv * r
        part = jnp.sum(dhv * nv, axis=0, keepdims=True)

        @pl.when(i == 0)
        def _():
            dg_ref[...] = part

        @pl.when(i > 0)
        def _():
            dg_ref[...] += part

        if need_dx:
            dn = dhv * g_ref[...]
            dx = r * (dn - nv * jnp.mean(dn * nv, axis=-1, keepdims=True))
            o_ref[...] = dres_ref[...] + dx

    row_spec = pl.BlockSpec((tr, d), lambda i: (i, 0))
    vec_spec = pl.BlockSpec((1, d), lambda i: (0, 0))
    if need_dx:
        return pl.pallas_call(
            body, out_shape=(jax.ShapeDtypeStruct((rows, d), F32), jax.ShapeDtypeStruct((1, d), F32)),
            grid=(rows // tr,), in_specs=[row_spec, vec_spec, row_spec, row_spec], out_specs=(row_spec, vec_spec),
            compiler_params=_cp(("arbitrary",)), name=name)(x, gain, dh, dres)
    return None, pl.pallas_call(
        body, out_shape=jax.ShapeDtypeStruct((1, d), F32), grid=(rows // tr,),
        in_specs=[row_spec, vec_spec, row_spec], out_specs=vec_spec,
        compiler_params=_cp(("arbitrary",)), name=name)(x, gain, dh)


def _final_loss(x, gain, target, name):
    rows, d = x.shape
    tr = _pick(rows, (512, 256))
    nsteps = rows // tr

    def body(x_ref, g_ref, t_ref, dx_ref, dg_ref, loss_ref, acc_ref):
        i = pl.program_id(0)
        xv = x_ref[...]
        g = g_ref[...]
        r = lax.rsqrt(jnp.mean(xv * xv, axis=-1, keepdims=True) + EPS)
        nv = xv * r
        err = nv * g - t_ref[...]
        dy = err * (1.0 / d)
        dn = dy * g
        dx_ref[...] = r * (dn - nv * jnp.mean(dn * nv, axis=-1, keepdims=True))
        dgp = jnp.sum(dy * nv, axis=0, keepdims=True)
        lp = jnp.sum(err * err, axis=0, keepdims=True)

        @pl.when(i == 0)
        def _():
            dg_ref[...] = dgp
            acc_ref[...] = lp

        @pl.when(i > 0)
        def _():
            dg_ref[...] += dgp
            acc_ref[...] += lp

        @pl.when(i == nsteps - 1)
        def _():
            tot = jnp.sum(acc_ref[...], axis=1, keepdims=True) * (0.5 / d)
            loss_ref[...] = jnp.broadcast_to(tot, loss_ref.shape)

    row_spec = pl.BlockSpec((tr, d), lambda i: (i, 0))
    vec_spec = pl.BlockSpec((1, d), lambda i: (0, 0))
    return pl.pallas_call(
        body, out_shape=(jax.ShapeDtypeStruct((rows, d), F32), jax.ShapeDtypeStruct((1, d), F32),
                         jax.ShapeDtypeStruct((1, LANES), F32)),
        grid=(nsteps,), in_specs=[row_spec, vec_spec, row_spec],
        out_specs=(row_spec, vec_spec, pl.BlockSpec((1, LANES), lambda i: (0, 0))),
        scratch_shapes=[pltpu.VMEM((1, d), F32)], compiler_params=_cp(("arbitrary",)), name=name)(x, gain, target)


def _rope_tables(seq):
    pairs = HEAD_DIM // 4
    lane = jnp.arange(LANES, dtype=jnp.int32) % HEAD_DIM
    by_col, second, pair = lane // (2 * pairs) == 1, (lane % (2 * pairs)) // pairs == 1, lane % pairs
    inv_freq = ROPE_THETA ** (-pair.astype(F32) / pairs)
    t = jnp.arange(seq, dtype=jnp.int32)[:, None]
    pos = jnp.where(by_col[None, :], t % GRID_W, t // GRID_W).astype(F32)
    ang = pos * inv_freq[None, :]
    cos, sin = jnp.cos(ang), jnp.sin(ang)
    return cos, jnp.where(second[None, :], sin, 0.0), jnp.where(second[None, :], 0.0, -sin)


def _pair_norm(xv, lo):
    sq = xv * xv
    s_lo = jnp.sum(jnp.where(lo, sq, 0.0), axis=1, keepdims=True)
    s_hi = jnp.sum(jnp.where(lo, 0.0, sq), axis=1, keepdims=True)
    return lax.rsqrt(jnp.where(lo, s_lo, s_hi) * (1.0 / HEAD_DIM) + EPS)


def _rope(y, c, sp, sm):
    return y * c + pltpu.roll(y, 16, axis=1) * sp + pltpu.roll(y, LANES - 16, axis=1) * sm


def _rope_t(dz, c, sp, sm):
    return dz * c + pltpu.roll(dz * sp, LANES - 16, axis=1) + pltpu.roll(dz * sm, 16, axis=1)


def _qk_prep(qkv, qg2, kg2, tabs, plan, sends, name):
    seq = qkv.shape[0]
    ts = _pick(seq, (256, 128))
    nq, nkp = N_HEADS // 2, N_KV // 2
    qw, kw = N_HEADS * HEAD_DIM, N_KV * HEAD_DIM
    nt, nsteps = plan.nt, seq // ts

    def body(x_ref, qg_ref, kg_ref, c_ref, sp_ref, sm_ref, *rest):
        q_ref, k_ref, kt_ref, v_ref, vt_ref = rest[nt:nt + 5]
        plan.bind(rest[:nt], rest[nt + 5:2 * nt + 5], *rest[2 * nt + 5:])
        _hosted(plan, nsteps, pl.program_id(0), lambda: compute(x_ref, qg_ref, kg_ref, c_ref, sp_ref, sm_ref,
                                                                q_ref, k_ref, kt_ref, v_ref, vt_ref))

    def compute(x_ref, qg_ref, kg_ref, c_ref, sp_ref, sm_ref, q_ref, k_ref, kt_ref, v_ref, vt_ref):
        lo = lax.broadcasted_iota(jnp.int32, (ts, LANES), 1) < HEAD_DIM
        top = lax.broadcasted_iota(jnp.int32, (LANES, ts), 0) < HEAD_DIM
        c, sp, sm = c_ref[...], sp_ref[...], sm_ref[...]
        for i in range(nq):
            xv = x_ref[:, i * LANES:(i + 1) * LANES]
            y = xv * _pair_norm(xv, lo) * qg_ref[...]
            q_ref[:, i * LANES:(i + 1) * LANES] = (_rope(y, c, sp, sm) * (LOG2E * HEAD_DIM ** -0.5)).astype(BF16)
        for i in range(nkp):
            xv = x_ref[:, qw + i * LANES:qw + (i + 1) * LANES]
            z = _rope(xv * _pair_norm(xv, lo) * kg_ref[...], c, sp, sm)
            k_ref[:, i * LANES:(i + 1) * LANES] = z.astype(BF16)
            kt_ref[i * LANES:(i + 1) * LANES, :] = z.T.astype(BF16)
            vv = x_ref[:, qw + kw + i * LANES:qw + kw + (i + 1) * LANES]
            v_ref[:, i * LANES:(i + 1) * LANES] = vv.astype(BF16)
            vvt = vv.T
            vt_ref[(2 * i) * LANES:(2 * i + 1) * LANES, :] = jnp.where(top, vvt, 1.0).astype(BF16)
            vt_ref[(2 * i + 1) * LANES:(2 * i + 2) * LANES, :] = jnp.where(top, 1.0, vvt).astype(BF16)

    tab = pl.BlockSpec((ts, LANES), lambda i: (i, 0))
    vec = pl.BlockSpec((1, LANES), lambda i: (0, 0))
    return pl.pallas_call(
        body,
        out_shape=(jax.ShapeDtypeStruct((seq, qw), BF16), jax.ShapeDtypeStruct((seq, kw), BF16),
                   jax.ShapeDtypeStruct((kw, seq), BF16), jax.ShapeDtypeStruct((seq, kw), BF16),
                   jax.ShapeDtypeStruct((N_KV * LANES, seq), BF16), *plan.out_shape),
        grid=(nsteps,),
        in_specs=[pl.BlockSpec((ts, qw + 2 * kw), lambda i: (i, 0)), vec, vec, tab, tab, tab] + [ANY] * nt,
        out_specs=(pl.BlockSpec((ts, qw), lambda i: (i, 0)), pl.BlockSpec((ts, kw), lambda i: (i, 0)),
                   pl.BlockSpec((kw, ts), lambda i: (0, i)), pl.BlockSpec((ts, kw), lambda i: (i, 0)),
                   pl.BlockSpec((N_KV * LANES, ts), lambda i: (0, i)), *([ANY] * nt)),
        scratch_shapes=plan.scratch, compiler_params=_cp(("arbitrary",)), name=name)(qkv, qg2, kg2, *tabs, *sends)


def _qk_prep_bwd(qkv, dq, dk, dv, qg2, kg2, tabs, plan, sends, name):
    seq = qkv.shape[0]
    ts = _pick(seq, (256, 128))
    nq, nkp = N_HEADS // 2, N_KV // 2
    qw, kw = N_HEADS * HEAD_DIM, N_KV * HEAD_DIM
    nt, nsteps = plan.nt, seq // ts

    def body(x_ref, dq_ref, dk_ref, dv_ref, qg_ref, kg_ref, c_ref, sp_ref, sm_ref, *rest):
        o_ref, dqg_ref, dkg_ref = rest[nt:nt + 3]
        plan.bind(rest[:nt], rest[nt + 3:2 * nt + 3], *rest[2 * nt + 3:])
        step = pl.program_id(0)

        @pl.when(step == 0)
        def _():
            plan.start()

        lo = lax.broadcasted_iota(jnp.int32, (ts, LANES), 1) < HEAD_DIM
        c, sp, sm = c_ref[...], sp_ref[...], sm_ref[...]

        def one(xv, dz, gain):
            r = _pair_norm(xv, lo)
            nv = xv * r
            dy = _rope_t(dz, c, sp, sm)
            dgp = jnp.sum(dy * nv, axis=0, keepdims=True)
            dn = dy * gain
            t = dn * nv
            m_lo = jnp.sum(jnp.where(lo, t, 0.0), axis=1, keepdims=True)
            m_hi = jnp.sum(jnp.where(lo, 0.0, t), axis=1, keepdims=True)
            m = jnp.where(lo, m_lo, m_hi) * (1.0 / HEAD_DIM)
            return r * (dn - nv * m), dgp

        dqg = jnp.zeros((1, LANES), F32)
        for i in range(nq):
            sl = slice(i * LANES, (i + 1) * LANES)
            dx, dgp = one(x_ref[:, sl], dq_ref[:, sl] * (HEAD_DIM ** -0.5), qg_ref[...])
            o_ref[:, sl] = dx.astype(BF16)
            dqg = dqg + dgp
        dkg = jnp.zeros((1, LANES), F32)
        for i in range(nkp):
            sl = slice(i * LANES, (i + 1) * LANES)
            dx, dgp = one(x_ref[:, qw + i * LANES:qw + (i + 1) * LANES], dk_ref[:, sl], kg_ref[...])
            o_ref[:, qw + i * LANES:qw + (i + 1) * LANES] = dx.astype(BF16)
            dkg = dkg + dgp
            o_ref[:, qw + kw + i * LANES:qw + kw + (i + 1) * LANES] = dv_ref[:, sl].astype(BF16)

        @pl.when(step == 0)
        def _():
            dqg_ref[...] = dqg
            dkg_ref[...] = dkg

        @pl.when(step > 0)
        def _():
            dqg_ref[...] += dqg
            dkg_ref[...] += dkg

        @pl.when(step == nsteps - 1)
        def _():
            plan.finish()

    tab = pl.BlockSpec((ts, LANES), lambda i: (i, 0))
    vec = pl.BlockSpec((1, LANES), lambda i: (0, 0))
    return pl.pallas_call(
        body,
        out_shape=(jax.ShapeDtypeStruct((seq, qw + 2 * kw), BF16), jax.ShapeDtypeStruct((1, LANES), F32),
                   jax.ShapeDtypeStruct((1, LANES), F32), *plan.out_shape),
        grid=(nsteps,),
        in_specs=[pl.BlockSpec((ts, qw + 2 * kw), lambda i: (i, 0)), pl.BlockSpec((ts, qw), lambda i: (i, 0)),
                  pl.BlockSpec((ts, kw), lambda i: (i, 0)), pl.BlockSpec((ts, kw), lambda i: (i, 0)),
                  vec, vec, tab, tab, tab] + [ANY] * nt,
        out_specs=(pl.BlockSpec((ts, qw + 2 * kw), lambda i: (i, 0)), vec, vec, *([ANY] * nt)),
        scratch_shapes=plan.scratch, compiler_params=_cp(("arbitrary",)), name=name)(qkv, dq, dk, dv, qg2, kg2, *tabs, *sends)


def _slot(blk, off0, tq):
    half = lax.broadcasted_iota(jnp.int32, (tq, LANES), 1) // HEAD_DIM
    keep = half == jnp.where(off0, 0, 1)
    parts = []
    for p in range(2):
        pair = blk[:, p * LANES:(p + 1) * LANES].astype(F32)
        rolled = pltpu.roll(pair, HEAD_DIM, axis=1)
        parts.append(jnp.where(keep, jnp.where(off0, pair, rolled), 0.0))
        parts.append(jnp.where(keep, jnp.where(off0, rolled, pair), 0.0))
    return jnp.concatenate(parts, axis=0)


def _unslot(x4, off0, tq):
    lo = lax.broadcasted_iota(jnp.int32, (tq, LANES), 1) < HEAD_DIM
    pairs = []
    for p in range(2):
        h0 = x4[(2 * p) * tq:(2 * p + 1) * tq]
        h1 = x4[(2 * p + 1) * tq:(2 * p + 2) * tq]
        a = jnp.where(off0, h0, pltpu.roll(h0, HEAD_DIM, axis=1))
        b = jnp.where(off0, pltpu.roll(h1, HEAD_DIM, axis=1), h1)
        pairs.append(jnp.where(lo, a, b))
    return jnp.concatenate(pairs, axis=1)


def _flash_fwd(q, k, vt, plan, shards, name):
    seq = q.shape[0]
    tq = _pick(seq, (FLASH_TQ, 128))
    tk = _pick(seq, (FLASH_TK, 2048, 512, 256, 128))
    sub = _pick(tk, (FLASH_SUB, 256, 128))
    nq, nkv, nsub = seq // tq, seq // tk, tk // sub
    gw = 4 * HEAD_DIM
    nt = plan.nt

    def body(q_ref, k_ref, vt_ref, *rest):
        o_ref, lse_ref = rest[nt:nt + 2]
        q4_ref, m_ref, acc_ref, st_ref = rest[2 * nt + 2:2 * nt + 6]
        plan.bind(rest[:nt], rest[nt + 2:2 * nt + 2], *rest[2 * nt + 6:])
        g, qi, ki = pl.program_id(0), pl.program_id(1), pl.program_id(2)
        off0 = (g % 2) == 0
        @pl.when(jnp.logical_and(g == 0, jnp.logical_and(qi == 0, ki == 0)))
        def _():
            plan.start()

        @pl.when(jnp.logical_and(g == N_KV - 1, jnp.logical_and(qi == nq - 1, ki == 0)))
        def _():
            plan.forward()

        @pl.when(ki == 0)
        def _():
            q4_ref[...] = _slot(q_ref[...], off0, tq).astype(BF16)
            m_ref[...] = jnp.full(m_ref.shape, NEG, F32)
            acc_ref[...] = jnp.zeros(acc_ref.shape, F32)

        q4 = q4_ref[...]

        def scores(c):
            st_ref[c % 2] = lax.dot_general(k_ref[c * sub:(c + 1) * sub, :], q4, (((1,), (1,)), ((), ())),
                                            preferred_element_type=F32)

        m, acc = m_ref[...], acc_ref[...]
        scores(0)
        for c in range(nsub):
            if c + 1 < nsub:
                scores(c + 1)
            st = st_ref[c % 2]
            m_new = jnp.maximum(m, jnp.max(st, axis=0, keepdims=True))
            pt = jnp.exp2(st - m_new).astype(BF16)
            acc = jnp.exp2(m - m_new) * acc + jnp.dot(vt_ref[:, c * sub:(c + 1) * sub], pt, preferred_element_type=F32)
            m = m_new
        m_ref[...] = m
        acc_ref[...] = acc

        @pl.when(ki == nkv - 1)
        def _():
            acc = acc_ref[...]
            l = jnp.where(off0, acc[HEAD_DIM:HEAD_DIM + 1], acc[0:1])
            o4 = acc.T
            o4 = o4 / pltpu.roll(o4, HEAD_DIM, axis=1)
            o_ref[...] = _unslot(o4, off0, tq).astype(o_ref.dtype)
            lse_ref[...] = jnp.broadcast_to(m_ref[...] + jnp.log2(l), lse_ref.shape)

        @pl.when(jnp.logical_and(g == N_KV - 1, jnp.logical_and(qi == nq - 1, ki == nkv - 1)))
        def _():
            plan.finish()

    return pl.pallas_call(
        body,
        out_shape=(jax.ShapeDtypeStruct((seq, N_HEADS * HEAD_DIM), BF16),
                   jax.ShapeDtypeStruct((N_KV * nq * 8, 4 * tq), F32), *plan.out_shape),
        grid=(N_KV, nq, nkv),
        in_specs=[pl.BlockSpec((tq, gw), lambda g, qi, ki: (qi, g)),
                  pl.BlockSpec((tk, LANES), lambda g, qi, ki: (ki, g // 2)),
                  pl.BlockSpec((LANES, tk), lambda g, qi, ki: (g, ki))] + [ANY] * nt,
        out_specs=(pl.BlockSpec((tq, gw), lambda g, qi, ki: (qi, g)),
                   pl.BlockSpec((8, 4 * tq), lambda g, qi, ki: (g * nq + qi, 0)), *([ANY] * nt)),
        scratch_shapes=[pltpu.VMEM((4 * tq, LANES), BF16), pltpu.VMEM((1, 4 * tq), F32),
                        pltpu.VMEM((LANES, 4 * tq), F32), pltpu.VMEM((2, sub, 4 * tq), F32)] + plan.scratch,
        compiler_params=_cp(("arbitrary", "arbitrary", "arbitrary")), name=name)(q, k, vt, *shards)


def _flash_bwd(q, k, kt, v, do, o, lse, plan, grads, name):
    seq = q.shape[0]
    tq = _pick(seq, (FLASH_TQ, 128))
    tk = _pick(seq, (FLASH_TK, 2048, 512, 256, 128))
    sub = _pick(tk, (FLASH_SUB, 256, 128))
    nq, nkv, nsub = seq // tq, seq // tk, tk // sub
    gw = 4 * HEAD_DIM
    nt = plan.nt

    def body(q_ref, k_ref, kt_ref, v_ref, do_ref, o_ref, lse_ref, *rest):
        dq_ref, dk_ref, dv_ref = rest[nt:nt + 3]
        q4_ref, do4_ref, delta_ref, dqt_ref, st_ref, dpt_ref = rest[2 * nt + 3:2 * nt + 9]
        plan.bind(rest[:nt], rest[nt + 3:2 * nt + 3], *rest[2 * nt + 9:])
        g, qi, ki = pl.program_id(0), pl.program_id(1), pl.program_id(2)
        off0 = (g % 2) == 0

        @pl.when(jnp.logical_and(g == 0, jnp.logical_and(qi == 0, ki == 0)))
        def _():
            plan.start()

        @pl.when(jnp.logical_and(g % 2 == 0, jnp.logical_and(qi == 0, ki == 0)))
        def _():
            dk_ref[...] = jnp.zeros(dk_ref.shape, F32)
            dv_ref[...] = jnp.zeros(dv_ref.shape, F32)

        @pl.when(ki == 0)
        def _():
            q4_ref[...] = _slot(q_ref[...], off0, tq).astype(BF16)
            do4 = _slot(do_ref[...], off0, tq)
            do4_ref[...] = do4.astype(BF16)
            o4 = _slot(o_ref[...], off0, tq)
            delta_ref[...] = jnp.sum((do4 * o4).T, axis=0, keepdims=True)
            dqt_ref[...] = jnp.zeros(dqt_ref.shape, F32)

        q4, do4 = q4_ref[...], do4_ref[...]
        lse_row, delta = lse_ref[0:1, :], delta_ref[...]

        def products(c):
            rows = slice(c * sub, (c + 1) * sub)
            st_ref[c % 2] = lax.dot_general(k_ref[rows, :], q4, (((1,), (1,)), ((), ())), preferred_element_type=F32)
            dpt_ref[c % 2] = lax.dot_general(v_ref[rows, :], do4, (((1,), (1,)), ((), ())), preferred_element_type=F32)

        dqt = dqt_ref[...]
        products(0)
        for c in range(nsub):
            if c + 1 < nsub:
                products(c + 1)
            pt = jnp.exp2(st_ref[c % 2] - lse_row)
            dst = (pt * (dpt_ref[c % 2] - delta)).astype(BF16)
            rows = pl.ds(pl.multiple_of(ki * tk + c * sub, sub), sub)
            dv_ref[rows, :] += jnp.dot(pt.astype(BF16), do4, preferred_element_type=F32)
            dk_ref[rows, :] += jnp.dot(dst, q4, preferred_element_type=F32) * (1.0 / LOG2E)
            dqt = dqt + jnp.dot(kt_ref[:, c * sub:(c + 1) * sub], dst, preferred_element_type=F32)
        dqt_ref[...] = dqt

        @pl.when(ki == nkv - 1)
        def _():
            dq_ref[...] = _unslot(dqt_ref[...].T, off0, tq)

        @pl.when(jnp.logical_and(g == N_KV - 1, jnp.logical_and(qi == nq - 1, ki == nkv - 1)))
        def _():
            plan.finish()

    return pl.pallas_call(
        body,
        out_shape=(jax.ShapeDtypeStruct((seq, N_HEADS * HEAD_DIM), F32),
                   jax.ShapeDtypeStruct((seq, N_KV * HEAD_DIM), F32), jax.ShapeDtypeStruct((seq, N_KV * HEAD_DIM), F32),
                   *plan.out_shape),
        grid=(N_KV, nq, nkv),
        in_specs=[pl.BlockSpec((tq, gw), lambda g, qi, ki: (qi, g)),
                  pl.BlockSpec((tk, LANES), lambda g, qi, ki: (ki, g // 2)),
                  pl.BlockSpec((LANES, tk), lambda g, qi, ki: (g // 2, ki)),
                  pl.BlockSpec((tk, LANES), lambda g, qi, ki: (ki, g // 2)),
                  pl.BlockSpec((tq, gw), lambda g, qi, ki: (qi, g)),
                  pl.BlockSpec((tq, gw), lambda g, qi, ki: (qi, g)),
                  pl.BlockSpec((8, 4 * tq), lambda g, qi, ki: (g * nq + qi, 0))] + [ANY] * nt,
        out_specs=(pl.BlockSpec((tq, gw), lambda g, qi, ki: (qi, g)),
                   pl.BlockSpec((seq, LANES), lambda g, qi, ki: (0, g // 2)),
                   pl.BlockSpec((seq, LANES), lambda g, qi, ki: (0, g // 2)), *([ANY] * nt)),
        scratch_shapes=[pltpu.VMEM((4 * tq, LANES), BF16), pltpu.VMEM((4 * tq, LANES), BF16),
                        pltpu.VMEM((1, 4 * tq), F32), pltpu.VMEM((LANES, 4 * tq), F32),
                        pltpu.VMEM((2, sub, 4 * tq), F32), pltpu.VMEM((2, sub, 4 * tq), F32)] + plan.scratch,
        compiler_params=_cp(("arbitrary", "arbitrary", "arbitrary")), name=name)(q, k, kt, v, do, o, lse, *grads)


def _xattn_fwd(q, kv, name):
    seq, d = q.shape
    mlen = kv.shape[0]
    tq = _pick(seq, (512, 256))

    def body(q_ref, k_ref, v_ref, o_ref):
        for h in range(X_HEADS):
            sl = slice(h * X_HEAD_DIM, (h + 1) * X_HEAD_DIM)
            s = lax.dot_general(q_ref[:, sl], k_ref[:, sl], (((1,), (1,)), ((), ())), preferred_element_type=F32)
            e = jnp.exp(s - jnp.max(s, axis=-1, keepdims=True))
            p = e / jnp.sum(e, axis=-1, keepdims=True)
            o_ref[:, sl] = jnp.dot(p.astype(BF16), v_ref[:, sl], preferred_element_type=F32).astype(o_ref.dtype)

    return pl.pallas_call(
        body, out_shape=jax.ShapeDtypeStruct((seq, d), BF16), grid=(seq // tq,),
        in_specs=[pl.BlockSpec((tq, d), lambda i: (i, 0)), pl.BlockSpec((mlen, d), lambda i: (0, 0)),
                  pl.BlockSpec((mlen, d), lambda i: (0, 1))],
        out_specs=pl.BlockSpec((tq, d), lambda i: (i, 0)), compiler_params=_cp(("parallel",)), name=name)(q, kv, kv)


def _xattn_bwd(q, kv, do, name):
    seq, d = q.shape
    mlen = kv.shape[0]
    tq = _pick(seq, (512, 256))
    scale = X_HEAD_DIM ** -0.5

    def body(q_ref, k_ref, v_ref, do_ref, dq_ref, dkv_ref):
        i = pl.program_id(0)

        @pl.when(i == 0)
        def _():
            dkv_ref[...] = jnp.zeros(dkv_ref.shape, F32)

        for h in range(X_HEADS):
            sl = slice(h * X_HEAD_DIM, (h + 1) * X_HEAD_DIM)
            qh, kh, vh = q_ref[:, sl], k_ref[:, sl], v_ref[:, sl]
            doh = do_ref[:, sl].astype(BF16)
            st = lax.dot_general(kh, qh, (((1,), (1,)), ((), ())), preferred_element_type=F32)
            e = jnp.exp(st - jnp.max(st, axis=0, keepdims=True))
            pt = e / jnp.sum(e, axis=0, keepdims=True)
            dpt = lax.dot_general(vh, doh, (((1,), (1,)), ((), ())), preferred_element_type=F32)
            dst = (pt * (dpt - jnp.sum(pt * dpt, axis=0, keepdims=True))).astype(BF16)
            dkv_ref[:, sl] += jnp.dot(dst, qh, preferred_element_type=F32)
            dkv_ref[:, d + h * X_HEAD_DIM:d + (h + 1) * X_HEAD_DIM] += jnp.dot(pt.astype(BF16), doh,
                                                                                 preferred_element_type=F32)
            dqh = lax.dot_general(dst, kh, (((0,), (0,)), ((), ())), preferred_element_type=F32)
            dq_ref[:, sl] = (dqh * scale).astype(dq_ref.dtype)

    return pl.pallas_call(
        body, out_shape=(jax.ShapeDtypeStruct((seq, d), BF16), jax.ShapeDtypeStruct((mlen, 2 * d), F32)),
        grid=(seq // tq,),
        in_specs=[pl.BlockSpec((tq, d), lambda i: (i, 0)), pl.BlockSpec((mlen, d), lambda i: (0, 0)),
                  pl.BlockSpec((mlen, d), lambda i: (0, 1)), pl.BlockSpec((tq, d), lambda i: (i, 0))],
        out_specs=(pl.BlockSpec((tq, d), lambda i: (i, 0)), pl.BlockSpec((mlen, 2 * d), lambda i: (0, 0))),
        compiler_params=_cp(("arbitrary",)), name=name)(q, kv, kv, do)


def _halo_specs(tr, tc, seq, col):
    per, last = tr // HALO, seq // HALO - 1
    return [pl.BlockSpec((tr, tc), lambda j, r: (r, col(j))),
            pl.BlockSpec((HALO, tc), lambda j, r: (jnp.maximum(r * per - 1, 0), col(j))),
            pl.BlockSpec((HALO, tc), lambda j, r: (jnp.minimum((r + 1) * per, last), col(j)))]


def _extend(main_ref, prev_ref, next_ref, r, nr):
    pv = (r > 0).astype(F32)
    nv = (r < nr - 1).astype(F32)
    return jnp.concatenate([prev_ref[...].astype(F32) * pv, main_ref[...].astype(F32),
                            next_ref[...].astype(F32) * nv], axis=0)


def _conv3(e, w_ref, n):
    return pltpu.roll(e, 1, axis=0) * w_ref[0:1, :] + e * w_ref[1:2, :] + pltpu.roll(e, n - 1, axis=0) * w_ref[2:3, :]


def _conv_gate_fwd(ug, uv, cw, cb, layer, name, plan=None, shards=(), fulls=()):
    seq, f = ug.shape
    tc = 256
    tr = _pick(seq, (512, 256))
    nc, nr = f // tc, seq // tr
    n = tr + 2 * HALO
    nt = plan.nt if plan is not None else 0

    def body(g_ref, gp_ref, gn_ref, v_ref, vp_ref, vn_ref, wg_ref, wv_ref, bg_ref, bv_ref, *rest):
        o_ref = rest[2 * nt]
        j, r = pl.program_id(0), pl.program_id(1)
        if plan is not None:
            plan.bind(rest[:nt], rest[2 * nt + 1:3 * nt + 1], *rest[3 * nt + 1:])

            @pl.when(jnp.logical_and(j == 0, r == 0))
            def _():
                plan.start()

        cg = _conv3(_extend(g_ref, gp_ref, gn_ref, r, nr), wg_ref, n)[HALO:HALO + tr] + bg_ref[...]
        cv = _conv3(_extend(v_ref, vp_ref, vn_ref, r, nr), wv_ref, n)[HALO:HALO + tr] + bv_ref[...]
        o_ref[...] = (cg * jax.nn.sigmoid(cg) * cv).astype(o_ref.dtype)

        if plan is not None:
            @pl.when(jnp.logical_and(j == nc - 1, r == nr - 1))
            def _():
                plan.forward()
                plan.finish()

    w_spec = lambda shift: pl.BlockSpec((None, 3, tc), lambda j, r: (layer, 0, j + shift))
    b_spec = lambda shift: pl.BlockSpec((None, 1, tc), lambda j, r: (layer, 0, j + shift))
    act_shape = jax.ShapeDtypeStruct((seq, f), BF16)
    act_spec = pl.BlockSpec((tr, tc), lambda j, r: (r, j))
    in_specs = _halo_specs(tr, tc, seq, lambda j: j) * 2 + [w_spec(0), w_spec(nc), b_spec(0), b_spec(nc)]
    operands = (ug, ug, ug, uv, uv, uv, cw, cw, cb, cb)
    if plan is None:
        return pl.pallas_call(body, out_shape=act_shape, grid=(nc, nr), in_specs=in_specs, out_specs=act_spec,
                              compiler_params=_cp(("parallel", "parallel")), name=name)(*operands)
    return pl.pallas_call(
        body, out_shape=(act_shape, *plan.out_shape), grid=(nc, nr), in_specs=in_specs + [ANY] * (2 * nt),
        out_specs=(act_spec, *([ANY] * nt)), scratch_shapes=plan.scratch,
        input_output_aliases={len(operands) + nt + t: 1 + t for t in range(nt)},
        compiler_params=_cp(("arbitrary", "arbitrary")), name=name)(*operands, *shards, *fulls)


def _conv_gate_bwd(ug, uv, dact, cw, cb, layer, name):
    seq, f = ug.shape
    tc = 256
    tr = _pick(seq, (512, 256))
    nc, nr = f // tc, seq // tr
    n = tr + 2 * HALO

    def body(g_ref, gp_ref, gn_ref, v_ref, vp_ref, vn_ref, d_ref, dp_ref, dn_ref, wg_ref, wv_ref, bg_ref, bv_ref,
             dug_ref, duv_ref, dwg_ref, dwv_ref):
        r = pl.program_id(1)
        eg = _extend(g_ref, gp_ref, gn_ref, r, nr)
        ev = _extend(v_ref, vp_ref, vn_ref, r, nr)
        da = _extend(d_ref, dp_ref, dn_ref, r, nr)
        eg3 = (pltpu.roll(eg, 1, axis=0), eg, pltpu.roll(eg, n - 1, axis=0))
        ev3 = (pltpu.roll(ev, 1, axis=0), ev, pltpu.roll(ev, n - 1, axis=0))
        cg = eg3[0] * wg_ref[0:1, :] + eg3[1] * wg_ref[1:2, :] + eg3[2] * wg_ref[2:3, :] + bg_ref[...]
        cv = ev3[0] * wv_ref[0:1, :] + ev3[1] * wv_ref[1:2, :] + ev3[2] * wv_ref[2:3, :] + bv_ref[...]
        sg = jax.nn.sigmoid(cg)
        dcv = da * (cg * sg)
        dcg = da * cv * (sg * (1.0 + cg * (1.0 - sg)))

        def back(dc, e3, w_ref, du_ref, dw_ref):
            du = (pltpu.roll(dc, n - 1, axis=0) * w_ref[0:1, :] + dc * w_ref[1:2, :]
                  + pltpu.roll(dc, 1, axis=0) * w_ref[2:3, :])
            du_ref[...] = du[HALO:HALO + tr].astype(du_ref.dtype)
            dcm = dc[HALO:HALO + tr]
            taps = [jnp.sum(dcm * e[HALO:HALO + tr], axis=0, keepdims=True) for e in e3] + [
                    jnp.sum(dcm, axis=0, keepdims=True)]
            part = jnp.concatenate(taps + [jnp.zeros((4, tc), F32)], axis=0)

            @pl.when(r == 0)
            def _():
                dw_ref[...] = part

            @pl.when(r > 0)
            def _():
                dw_ref[...] += part

        back(dcg, eg3, wg_ref, dug_ref, dwg_ref)
        back(dcv, ev3, wv_ref, duv_ref, dwv_ref)

    w_spec = lambda shift: pl.BlockSpec((None, 3, tc), lambda j, r: (layer, 0, j + shift))
    b_spec = lambda shift: pl.BlockSpec((None, 1, tc), lambda j, r: (layer, 0, j + shift))
    out_rows = pl.BlockSpec((tr, tc), lambda j, r: (r, j))
    out_acc = pl.BlockSpec((8, tc), lambda j, r: (0, j))
    return pl.pallas_call(
        body,
        out_shape=(jax.ShapeDtypeStruct((seq, f), BF16), jax.ShapeDtypeStruct((seq, f), BF16),
                   jax.ShapeDtypeStruct((8, f), F32), jax.ShapeDtypeStruct((8, f), F32)),
        grid=(nc, nr),
        in_specs=_halo_specs(tr, tc, seq, lambda j: j) * 3 + [w_spec(0), w_spec(nc), b_spec(0), b_spec(nc)],
        out_specs=(out_rows, out_rows, out_acc, out_acc),
        compiler_params=_cp(("parallel", "arbitrary")), name=name)(ug, ug, ug, uv, uv, uv, dact, dact, dact, cw, cw, cb, cb)


def _pool_count(g, r, tr, n, seq):
    half = jnp.left_shift(1, g)
    t = r * tr - HALO + lax.broadcasted_iota(jnp.int32, (n, 1), 0)
    cnt = jnp.minimum(t + half, seq) - jnp.maximum(t - half, 0)
    return jnp.maximum(cnt, 1).astype(F32)


def _by_group(g, levels):
    out = levels[3]
    for i in (2, 1, 0):
        out = jnp.where(g == i, levels[i], out)
    return out


def _pool_mixed(e, g, cnt, n):
    w2 = e + pltpu.roll(e, 1, axis=0)
    w4 = pltpu.roll(w2, 1, axis=0) + pltpu.roll(w2, n - 1, axis=0)
    w8 = pltpu.roll(w4, 2, axis=0) + pltpu.roll(w4, n - 2, axis=0)
    w16 = pltpu.roll(w8, 4, axis=0) + pltpu.roll(w8, n - 4, axis=0)
    return _by_group(g, (w2, w4, w8, w16)) / cnt - e


def _pool_fwd(hp, xres, pw, scale, name, plan, shards, fulls):
    seq, d = hp.shape
    tc = POOL_GROUP_W
    tr = _pick(seq, (512, 256))
    nr = seq // tr
    n = tr + 2 * HALO
    nt = plan.nt

    def body(h_ref, hp_ref, hn_ref, x_ref, w_ref, s_ref, *rest):
        o_ref = rest[2 * nt]
        plan.bind(rest[:nt], rest[2 * nt + 1:3 * nt + 1], *rest[3 * nt + 1:])
        g, r = pl.program_id(0), pl.program_id(1)

        def compute():
            e = _extend(h_ref, hp_ref, hn_ref, r, nr)
            mixed = _pool_mixed(e, g, _pool_count(g, r, tr, n, seq), n)[HALO:HALO + tr]
            y = jnp.dot(mixed.astype(BF16), w_ref[...], preferred_element_type=F32)
            o_ref[...] = x_ref[...] + y * s_ref[...]

        _hosted(plan, POOL_GROUPS * nr, g * nr + r, compute)

    operands = (hp, hp, hp, xres, pw, scale)
    return pl.pallas_call(
        body, out_shape=(jax.ShapeDtypeStruct((seq, d), F32), *plan.out_shape), grid=(POOL_GROUPS, nr),
        in_specs=_halo_specs(tr, tc, seq, lambda j: j) + [
            pl.BlockSpec((tr, tc), lambda j, r: (r, j)), pl.BlockSpec((None, tc, tc), lambda j, r: (j, 0, 0)),
            pl.BlockSpec((1, tc), lambda j, r: (0, j))] + [ANY] * (2 * nt),
        out_specs=(pl.BlockSpec((tr, tc), lambda j, r: (r, j)), *([ANY] * nt)), scratch_shapes=plan.scratch,
        input_output_aliases={len(operands) + nt + t: 1 + t for t in range(nt)},
        compiler_params=_cp(("arbitrary", "arbitrary")), name=name)(*operands, *shards, *fulls)


def _pool_bwd(hp, dy, pw, scale, name):
    seq, d = hp.shape
    tc = POOL_GROUP_W
    tr = _pick(seq, (512, 256))
    nr = seq // tr
    n = tr + 2 * HALO

    def body(h_ref, hp_ref, hn_ref, d_ref, dp_ref, dn_ref, w_ref, s_ref, dh_ref, dw_ref, ds_ref):
        g, r = pl.program_id(0), pl.program_id(1)
        cnt = _pool_count(g, r, tr, n, seq)
        e = _extend(h_ref, hp_ref, hn_ref, r, nr)
        mixed = _pool_mixed(e, g, cnt, n)[HALO:HALO + tr].astype(BF16)
        dye = _extend(d_ref, dp_ref, dn_ref, r, nr)
        dyp = (dye * s_ref[...]).astype(BF16)
        dmixed = lax.dot_general(dyp, w_ref[...], (((1,), (1,)), ((), ())), preferred_element_type=F32)
        dwin = dmixed / cnt
        m2 = dwin + pltpu.roll(dwin, n - 1, axis=0)
        m4 = pltpu.roll(m2, 1, axis=0) + pltpu.roll(m2, n - 1, axis=0)
        m8 = pltpu.roll(m4, 2, axis=0) + pltpu.roll(m4, n - 2, axis=0)
        m16 = pltpu.roll(m8, 4, axis=0) + pltpu.roll(m8, n - 4, axis=0)
        dh_ref[...] = (_by_group(g, (m2, m4, m8, m16)) - dmixed)[HALO:HALO + tr]
        ypre = jnp.dot(mixed, w_ref[...], preferred_element_type=F32)
        dsp = jnp.sum(d_ref[...] * ypre, axis=0, keepdims=True)
        dwp = lax.dot_general(mixed, dyp[HALO:HALO + tr], (((0,), (0,)), ((), ())), preferred_element_type=F32)

        @pl.when(r == 0)
        def _():
            dw_ref[...] = dwp
            ds_ref[...] = dsp

        @pl.when(r > 0)
        def _():
            dw_ref[...] += dwp
            ds_ref[...] += dsp

    return pl.pallas_call(
        body,
        out_shape=(jax.ShapeDtypeStruct((seq, d), F32), jax.ShapeDtypeStruct((POOL_GROUPS, tc, tc), F32),
                   jax.ShapeDtypeStruct((1, d), F32)),
        grid=(POOL_GROUPS, nr),
        in_specs=_halo_specs(tr, tc, seq, lambda j: j) * 2 + [
            pl.BlockSpec((None, tc, tc), lambda j, r: (j, 0, 0)), pl.BlockSpec((1, tc), lambda j, r: (0, j))],
        out_specs=(pl.BlockSpec((tr, tc), lambda j, r: (r, j)), pl.BlockSpec((None, tc, tc), lambda j, r: (j, 0, 0)),
                   pl.BlockSpec((1, tc), lambda j, r: (0, j))),
        compiler_params=_cp(("parallel", "arbitrary")), name=name)(hp, hp, hp, dy, dy, dy, pw, scale)


def _adamw_math(w, g, m, v):
    m = ADAM_B1 * m + (1.0 - ADAM_B1) * g
    v = ADAM_B2 * v + (1.0 - ADAM_B2) * (g * g)
    m_hat = m / (1.0 - ADAM_B1 ** ADAM_STEP)
    v_hat = v / (1.0 - ADAM_B2 ** ADAM_STEP)
    delta = -ADAM_LR * (m_hat / (jnp.sqrt(v_hat) + ADAM_EPS) + ADAM_WD * w)
    return delta, m, v


def _adamw(w, ga, gb, m, v, name):
    rows, cols = w.shape
    tr = _pick(rows, (256, 128, 64, 32, 16, 8))
    two = gb is not None

    def body(*refs):
        if two:
            w_ref, ga_ref, gb_ref, m_ref, v_ref, g_out, d_out, m_out, v_out = refs
            g = ga_ref[...] + gb_ref[...]
        else:
            w_ref, ga_ref, m_ref, v_ref, g_out, d_out, m_out, v_out = refs
            g = ga_ref[...]
        delta, m, v = _adamw_math(w_ref[...], g, m_ref[...], v_ref[...])
        g_out[...] = g
        d_out[...] = delta
        m_out[...] = m
        v_out[...] = v

    spec = pl.BlockSpec((tr, cols), lambda i: (i, 0))
    ops = [w, ga] + ([gb] if two else []) + [m, v]
    return pl.pallas_call(
        body, out_shape=tuple(jax.ShapeDtypeStruct((rows, cols), F32) for _ in range(4)), grid=(rows // tr,),
        in_specs=[spec] * len(ops), out_specs=(spec,) * 4, compiler_params=_cp(("parallel",)), name=name)(*ops)


def _sum4(parts, name):
    _, rows, cols = parts.shape
    tr = _pick(rows, (256, 128, 64, 32, 16))

    def body(p_ref, o_ref):
        acc = p_ref[0].astype(F32)
        for kk in range(1, 4):
            acc = acc + p_ref[kk].astype(F32)
        o_ref[...] = acc

    return pl.pallas_call(
        body, out_shape=jax.ShapeDtypeStruct((rows, cols), F32), grid=(rows // tr,),
        in_specs=[pl.BlockSpec((4, tr, cols), lambda i: (0, i, 0))], out_specs=pl.BlockSpec((tr, cols), lambda i: (i, 0)),
        compiler_params=_cp(("parallel",)), name=name)(parts)


def _place():
    x, y, c = lax.axis_index("x"), lax.axis_index("y"), lax.axis_index("c")
    chips = [(1 - x, y), (x, 1 - y), (1 - x, 1 - y)]
    return x, y, c, chips


def _window(ref, axis, j, size, c=None, half=None, lead=(), layers=slice(None)):
    if axis == "r":
        if c is None:
            return ref.at[lead + (layers, pl.ds(pl.multiple_of(j * size, 32), size), slice(None))]
        return ref.at[lead + (layers, pl.ds(pl.multiple_of(j * size + c * half, 32), half), slice(None))]
    cols = pl.ds(pl.multiple_of(j * size, LANES), size)
    if c is None:
        return ref.at[lead + (layers, slice(None), cols)]
    return ref.at[lead + (layers, pl.ds(pl.multiple_of(c * half, 32), half), cols)]


class _Gather:
    def __init__(self, shards, axes, layers=None):
        self.nt, self.axes = len(shards), axes
        self.layers = layers or [slice(None)] * self.nt
        self.out_shape, self.sizes, self.halves = [], [], []
        for s, ax in zip(shards, axes):
            l, rs, cs = s.shape
            self.out_shape.append(jax.ShapeDtypeStruct((l, 4 * rs, cs) if ax == "r" else (l, rs, 4 * cs), s.dtype))
            self.sizes.append(rs if ax == "r" else cs)
            self.halves.append(rs // 2)
        self.scratch = [pltpu.SemaphoreType.DMA((6 * self.nt,)), pltpu.SemaphoreType.DMA((6 * self.nt,)),
                        pltpu.SemaphoreType.DMA((self.nt,))]

    def bind(self, src, dst, send_sems, recv_sems, local_sems):
        self.src, self.dst, self.send_sems, self.recv_sems, self.local_sems = src, dst, send_sems, recv_sems, local_sems

    def _win(self, t, j, core=None):
        return _window(self.dst[t], self.axes[t], j, self.sizes[t], core, self.halves[t], layers=self.layers[t])

    def _ici(self, t, kk, origin):
        _, _, c, chips = _place()
        px, py = chips[kk]
        half = self.src[t].at[self.layers[t], pl.ds(pl.multiple_of(c * self.halves[t], 16), self.halves[t]), :]
        return pltpu.make_async_remote_copy(
            src_ref=half, dst_ref=self._win(t, origin, c), send_sem=self.send_sems.at[t * 3 + kk],
            recv_sem=self.recv_sems.at[t * 3 + kk], device_id=(px, py, c), device_id_type=MESH)

    def _d2d(self, t, kk, origin, core):
        x, y, c, _ = _place()
        k2 = 3 * self.nt + t * 3 + kk
        return pltpu.make_async_remote_copy(
            src_ref=self._win(t, origin, core), dst_ref=self._win(t, origin, core), send_sem=self.send_sems.at[k2],
            recv_sem=self.recv_sems.at[k2], device_id=(x, y, 1 - c), device_id_type=MESH)

    def _local(self, t):
        x, y, _, _ = _place()
        return pltpu.make_async_copy(self.src[t].at[self.layers[t]], self._win(t, 2 * x + y), self.local_sems.at[t])

    def _each(self):
        _, _, _, chips = _place()
        for t in range(self.nt):
            for kk in range(3):
                px, py = chips[kk]
                yield t, kk, 2 * px + py

    def start(self):
        x, y, _, _ = _place()
        for t in range(self.nt):
            self._local(t).start()
        for t, kk, _ in self._each():
            self._ici(t, kk, 2 * x + y).start()

    def forward(self):
        _, _, c, _ = _place()
        for t, kk, origin in self._each():
            self._ici(t, kk, origin).wait_recv()
            self._d2d(t, kk, origin, c).start()

    def finish(self):
        x, y, c, _ = _place()
        for t, kk, origin in self._each():
            self._d2d(t, kk, origin, 1 - c).wait_recv()
        for t, kk, origin in self._each():
            self._ici(t, kk, 2 * x + y).wait_send()
            self._d2d(t, kk, origin, c).wait_send()
        for t in range(self.nt):
            self._local(t).wait()


class _Scatter:
    def __init__(self, grads, axes):
        self.nt, self.axes = len(grads), axes
        self.out_shape, self.sizes = [], []
        for gr, ax in zip(grads, axes):
            l, r, cc = gr.shape
            self.out_shape.append(jax.ShapeDtypeStruct((4, l, r // 4, cc) if ax == "r" else (4, l, r, cc // 4), gr.dtype))
            self.sizes.append(r // 4 if ax == "r" else cc // 4)
        self.scratch = [pltpu.SemaphoreType.DMA((3 * self.nt,)), pltpu.SemaphoreType.DMA((3 * self.nt,)),
                        pltpu.SemaphoreType.DMA((self.nt,))]

    def bind(self, src, dst, send_sems, recv_sems, local_sems):
        self.src, self.dst, self.send_sems, self.recv_sems, self.local_sems = src, dst, send_sems, recv_sems, local_sems

    def _copy(self, t, kk, slot):
        x, y, c, chips = _place()
        px, py = chips[kk]
        return pltpu.make_async_remote_copy(
            src_ref=_window(self.src[t], self.axes[t], 2 * px + py, self.sizes[t]), dst_ref=self.dst[t].at[slot],
            send_sem=self.send_sems.at[t * 3 + kk], recv_sem=self.recv_sems.at[t * 3 + kk],
            device_id=(px, py, c), device_id_type=MESH)

    def _local(self, t):
        x, y, _, _ = _place()
        me = 2 * x + y
        return pltpu.make_async_copy(_window(self.src[t], self.axes[t], me, self.sizes[t]), self.dst[t].at[me],
                                     self.local_sems.at[t])

    def start(self):
        x, y, _, _ = _place()
        for t in range(self.nt):
            self._local(t).start()
            for kk in range(3):
                self._copy(t, kk, 2 * x + y).start()

    def finish(self):
        _, _, _, chips = _place()
        for t in range(self.nt):
            for kk in range(3):
                px, py = chips[kk]
                self._copy(t, kk, 2 * px + py).wait_recv()
        for t in range(self.nt):
            for kk in range(3):
                px, py = chips[kk]
                self._copy(t, kk, 2 * px + py).wait_send()
            self._local(t).wait()


def _comm_call(plans, operands, name):
    nts = [p.nt for p in plans]
    n_in, n_sem = sum(nts), [len(p.scratch) for p in plans]

    def body(*refs):
        pos_in, pos_out, pos_sem = 0, n_in, 2 * n_in
        for p, nt, ns in zip(plans, nts, n_sem):
            p.bind(refs[pos_in:pos_in + nt], refs[pos_out:pos_out + nt], *refs[pos_sem:pos_sem + ns])
            pos_in, pos_out, pos_sem = pos_in + nt, pos_out + nt, pos_sem + ns
        for p in plans:
            p.start()
        for p in plans:
            if hasattr(p, "forward"):
                p.forward()
        for p in plans:
            p.finish()

    return pl.pallas_call(
        body, out_shape=tuple(s for p in plans for s in p.out_shape), in_specs=[ANY] * n_in,
        out_specs=tuple([ANY] * n_in), scratch_shapes=[s for p in plans for s in p.scratch],
        name=name)(*[a for ops in operands for a in ops])


class _GatherAll:
    FLIPS = [f for f in itertools.product((0, 1), repeat=3) if any(f)]

    def __init__(self, pack):
        self.nt = 1
        self.out_shape = [jax.ShapeDtypeStruct((8,) + pack.shape, pack.dtype)]
        self.scratch = [pltpu.SemaphoreType.DMA((7,)), pltpu.SemaphoreType.DMA((7,)), pltpu.SemaphoreType.DMA((1,))]

    def bind(self, src, dst, send_sems, recv_sems, local_sems):
        self.src, self.dst, self.send_sems, self.recv_sems, self.local_sems = src[0], dst[0], send_sems, recv_sems, local_sems

    def _copy(self, kk, mine):
        x, y, c, _ = _place()
        px, py, pc = (1 - v if fl else v for v, fl in zip((x, y, c), self.FLIPS[kk]))
        slot = 4 * x + 2 * y + c if mine else 4 * px + 2 * py + pc
        return pltpu.make_async_remote_copy(src_ref=self.src, dst_ref=self.dst.at[slot], send_sem=self.send_sems.at[kk],
                                            recv_sem=self.recv_sems.at[kk], device_id=(px, py, pc), device_id_type=MESH)

    def _local(self):
        x, y, c, _ = _place()
        return pltpu.make_async_copy(self.src, self.dst.at[4 * x + 2 * y + c], self.local_sems.at[0])

    def start(self):
        self._local().start()
        for kk in range(7):
            self._copy(kk, True).start()

    def finish(self):
        for kk in range(7):
            self._copy(kk, False).wait_recv()
        for kk in range(7):
            self._copy(kk, True).wait_send()
        self._local().wait()


class _Swap:
    def __init__(self, arrs):
        self.nt = len(arrs)
        self.out_shape = [jax.ShapeDtypeStruct(a.shape, a.dtype) for a in arrs]
        self.scratch = [pltpu.SemaphoreType.DMA((self.nt,)), pltpu.SemaphoreType.DMA((self.nt,))]

    def bind(self, src, dst, send_sems, recv_sems):
        self.src, self.dst, self.send_sems, self.recv_sems = src, dst, send_sems, recv_sems

    def _copy(self, t):
        x, y, c, _ = _place()
        return pltpu.make_async_remote_copy(src_ref=self.src[t], dst_ref=self.dst[t], send_sem=self.send_sems.at[t],
                                            recv_sem=self.recv_sems.at[t], device_id=(x, y, 1 - c), device_id_type=MESH)

    def start(self):
        for t in range(self.nt):
            self._copy(t).start()

    def finish(self):
        for t in range(self.nt):
            self._copy(t).wait()


def _sum8(packs, name):
    def body(p_ref, o_ref):
        acc = p_ref[0]
        for dev in range(1, 8):
            acc = acc + p_ref[dev]
        o_ref[...] = acc

    return pl.pallas_call(body, out_shape=jax.ShapeDtypeStruct(packs.shape[1:], F32), name=name)(packs)


def _pack(arrs):
    flat = jnp.concatenate([a.reshape(-1).astype(F32) for a in arrs])
    rows = -(-flat.shape[0] // (8 * LANES)) * 8
    return jnp.pad(flat, (0, rows * LANES - flat.shape[0])).reshape(rows, LANES)


def _unpack(flat, shapes):
    out, pos = [], 0
    for shp in shapes:
        size = 1
        for s in shp:
            size *= s
        out.append(flat[pos:pos + size].reshape(shp))
        pos += size
    return out


BIG = ("attn_w_qkv", "attn_w_o", "pool_w", "xattn_w_q", "xattn_w_kv", "xattn_w_o", "ffn_w_up", "ffn_w_down")
BIG_AXIS = ("c", "r", "r", "r", "c", "r", "c", "r")
SMALL_REPL = ("attn_norm", "attn_q_gain", "attn_k_gain", "xattn_norm", "mem_norm", "ffn_norm", "ffn_conv_b", "final_norm")
SMALL_SHARD = ("pool_norm", "pool_scale", "ffn_conv_w")
ORDER = ("attn_norm", "attn_w_qkv", "attn_q_gain", "attn_k_gain", "attn_w_o", "pool_norm", "pool_w", "pool_scale",
         "xattn_norm", "mem_norm", "xattn_w_q", "xattn_w_kv", "xattn_w_o", "ffn_norm", "ffn_w_up", "ffn_conv_w",
         "ffn_conv_b", "ffn_w_down", "final_norm")


def _step(x, mem, tgt, w, m, v):
    seq, d = x.shape
    xi, yi, ci = lax.axis_index("x"), lax.axis_index("y"), lax.axis_index("c")
    chip = 2 * xi + yi
    dff = w["ffn_w_down"].shape[1] * 4
    n_layers = w["ffn_norm"].shape[0]

    def as3d(a):
        return a.reshape(a.shape[-3:])
    shards = [as3d(w[nm]).astype(BF16) for nm in BIG]
    small_in = [w[nm] for nm in SMALL_SHARD]
    small_pack = _pack(small_in)
    conv_b = w["ffn_conv_b"].reshape(n_layers, 1, -1)
    tabs = _rope_tables(seq)
    qg2 = jnp.tile(w["attn_q_gain"], (1, 2))
    kg2 = jnp.tile(w["attn_k_gain"], (1, 2))
    mm = functools.partial(_mm)

    saved = {}
    x0 = x
    h0, wq = _rms_fwd(x0, w["attn_norm"], BF16, "rms_attn", _Gather(shards[:1], BIG_AXIS[:1]), shards[:1])
    qkv = mm(h0, wq, "nn", b_l=0, out_dtype=F32, name="mm_qkv")
    q_r, k_r, k_t, v_b, v_t, small_all = _qk_prep(qkv, qg2, kg2, tabs, _GatherAll(small_pack), [small_pack], "qk_prep")
    per_chip = [_unpack(small_all[2 * j].reshape(-1), [a.shape for a in small_in]) for j in range(4)]
    pool_norm, pool_scale, conv_w = (jnp.concatenate([per_chip[j][i] for j in range(4)], axis=-1) for i in range(3))
    first = [slice(None)] * 5 + [slice(0, 1)] * 2
    o_at, lse, wo, wp, wxq, wxkv, wxo, wup, wdn = _flash_fwd(
        q_r, k_r, v_t, _Gather(shards[1:], BIG_AXIS[1:], first), shards[1:], "flash_fwd")
    ffn_w = {"up": wup, "down": wdn}
    x1, hq0 = mm(o_at, wo, "nn", b_l=0, res=x0, out_dtype=F32, norm_out=(w["xattn_norm"][0:1], BF16), name="mm_attn_o")

    def xattn_fwd(l, xin, hq):
        mn = _rms_fwd(mem, w["mem_norm"][l:l + 1], BF16, f"rms_mem{l}")
        xq = mm(hq, wxq, "nn", b_l=l, scale=X_HEAD_DIM ** -0.5, out_dtype=BF16, name=f"mm_xq{l}")
        kv = mm(mn, wxkv, "nn", b_l=l, out_dtype=BF16, name=f"mm_xkv{l}")
        xo = _xattn_fwd(xq, kv, f"xattn_fwd{l}")
        saved[f"x{l}"] = (hq, mn, xq, kv, xo)
        return mm(xo, wxo, "nn", b_l=l, res=xin, out_dtype=F32, norm_out=(w["ffn_norm"][l:l + 1], BF16), name=f"mm_xo{l}")

    def ffn_fwd(l, xin, hf, norm_out):
        ug = mm(hf, ffn_w["up"], "nn", b_l=l, n=dff, out_dtype=BF16, name=f"mm_up_g{l}")
        uv = mm(hf, ffn_w["up"], "nn", b_l=l, n=dff, b_off=(0, dff), out_dtype=BF16, name=f"mm_up_v{l}")
        if l == 0:
            rest = _Gather(shards[6:7], BIG_AXIS[6:7], [slice(1, 2)])
            act, ffn_w["up"] = _conv_gate_fwd(ug, uv, conv_w, conv_b, l, f"conv_gate{l}", rest, shards[6:7], [ffn_w["up"]])
        else:
            act = _conv_gate_fwd(ug, uv, conv_w, conv_b, l, f"conv_gate{l}")
        saved[f"f{l}"] = (hf, ug, uv, act)
        return mm(act, ffn_w["down"], "nn", b_l=l, res=xin, out_dtype=F32, norm_out=norm_out, name=f"mm_down{l}")

    x2, hf0 = xattn_fwd(0, x1, hq0)
    x3, hp = ffn_fwd(0, x2, hf0, (pool_norm, F32))
    x4, ffn_w["down"] = _pool_fwd(hp, x3, wp, pool_scale, "pool_fwd", _Gather(shards[7:], BIG_AXIS[7:], [slice(1, 2)]),
                                  shards[7:], [ffn_w["down"]])
    x5, hf1 = xattn_fwd(1, x4, _rms_fwd(x4, w["xattn_norm"][1:2], BF16, "rms_xq1"))
    xs = [x0, x1, x2, x3, x4, x5, ffn_fwd(1, x5, hf1, None)]
    dres, g_final, loss = _final_loss(xs[6], w["final_norm"].reshape(1, d), tgt, "final_loss")

    grads = {}
    gbuf = {}

    def dw(nm, a, b, layer, full, off=(0, 0), n=None, tn=None):
        gbuf[nm] = _mm(a, b, "tn", out_dtype=BF16, out_full=full, out_l=layer, out_off=off, n=n, tn=tn,
                       alias=gbuf.get(nm), name=f"dw_{nm}{layer}_{off[1]}")

    def ffn_bwd(l, xin, dres):
        hf, ug, uv, act = saved[f"f{l}"]
        wup, wdn = ffn_w["up"], ffn_w["down"]
        dw("ffn_w_down", act, dres, l, wdn.shape)
        dact = _mm(dres, wdn, "nt", b_l=l, out_dtype=BF16, name=f"mm_dact{l}")
        dug, duv, dwg, dwv = _conv_gate_bwd(ug, uv, dact, conv_w, conv_b, l, f"conv_gate_bwd{l}")
        dw("ffn_w_up", hf, dug, l, wup.shape, tn=1408)
        dw("ffn_w_up", hf, duv, l, wup.shape, off=(0, dff), tn=1408)
        dhf = _mm(dug, wup, "nt", b_l=l, n=d, out_dtype=F32, name=f"mm_dhf_g{l}")
        dres, dg = _mm(duv, wup, "nt", b_l=l, n=d, b_off=(0, dff), res=dhf, out_dtype=F32, tm=256,
                       norm_bwd=(xin, w["ffn_norm"][l:l + 1], dres), name=f"mm_dhf_v{l}")
        return dres, dg, jnp.concatenate([dwg[:3], dwv[:3]], axis=1), jnp.concatenate([dwg[3], dwv[3]], axis=0)

    def xattn_bwd(l, xin, dres):
        hq, mn, xq, kv, xo = saved[f"x{l}"]
        dw("xattn_w_o", xo, dres, l, wxo.shape)
        dxo = _mm(dres, wxo, "nt", b_l=l, out_dtype=BF16, name=f"mm_dxo{l}")
        dq, dkv = _xattn_bwd(xq, kv, dxo, f"xattn_bwd{l}")
        dw("xattn_w_q", hq, dq, l, wxq.shape)
        dw("xattn_w_kv", mn, dkv, l, wxkv.shape)
        dmn = _mm(dkv, wxkv, "nt", b_l=l, out_dtype=F32, name=f"mm_dmn{l}")
        _, dg_mem = _rms_bwd(mem, w["mem_norm"][l:l + 1], dmn, None, f"rms_mem_bwd{l}")
        dres, dg = _mm(dq, wxq, "nt", b_l=l, out_dtype=F32, norm_bwd=(xin, w["xattn_norm"][l:l + 1], dres),
                       name=f"mm_dhq{l}")
        return dres, dg, dg_mem

    g_ffn, g_xn, g_mn, g_cw, g_cb = [None] * n_layers, [None] * n_layers, [None] * n_layers, [None] * n_layers, [None] * n_layers
    dres, g_ffn[1], g_cw[1], g_cb[1] = ffn_bwd(1, xs[5], dres)
    dres, g_xn[1], g_mn[1] = xattn_bwd(1, xs[4], dres)
    dhp, g_pw, g_pscale = _pool_bwd(hp, dres, wp, pool_scale, "pool_bwd")
    dres, g_pnorm = _rms_bwd(xs[3], pool_norm, dhp, dres, "rms_pool_bwd")
    dres, g_ffn[0], g_cw[0], g_cb[0] = ffn_bwd(0, xs[2], dres)
    dres, g_xn[0], g_mn[0] = xattn_bwd(0, xs[1], dres)
    dw("attn_w_o", o_at, dres, 0, wo.shape)
    do = _mm(dres, wo, "nt", b_l=0, out_dtype=BF16, name="mm_do")
    gbuf["pool_w"] = g_pw.astype(BF16)
    early = [gbuf[nm] for nm in BIG[1:]]
    dq_r, dk_r, dv, *recv_early = _flash_bwd(q_r, k_r, k_t, v_b, do, o_at, lse, _Scatter(early, BIG_AXIS[1:]), early,
                                             "flash_bwd")
    def sum4(nm, rc):
        return _sum4(rc.reshape(4, -1, rc.shape[-1]), f"sum4_{nm}")
    sums_early = [sum4(nm, rc) for nm, rc in zip(BIG[1:], recv_early)]
    dqkv, dqg, dkg, *others_early = _qk_prep_bwd(qkv, dq_r, dk_r, dv, qg2, kg2, tabs, _Swap(sums_early), sums_early,
                                                 "qk_prep_bwd")
    dw("attn_w_qkv", h0, dqkv, 0, wq.shape)
    grad_x, g_an = _mm(dqkv, wq, "nt", b_l=0, out_dtype=F32, norm_bwd=(x0, w["attn_norm"], dres), name="mm_dh0")

    small_g = {
        "attn_norm": g_an, "attn_q_gain": dqg[:, :HEAD_DIM] + dqg[:, HEAD_DIM:], "attn_k_gain": dkg[:, :HEAD_DIM] + dkg[:, HEAD_DIM:],
        "xattn_norm": jnp.concatenate(g_xn, axis=0), "mem_norm": jnp.concatenate(g_mn, axis=0),
        "ffn_norm": jnp.concatenate(g_ffn, axis=0), "ffn_conv_b": jnp.stack(g_cb, axis=0), "final_norm": g_final.reshape(d),
        "pool_norm": g_pnorm, "pool_scale": g_pscale, "ffn_conv_w": jnp.stack(g_cw, axis=0)}
    names = SMALL_REPL + SMALL_SHARD
    small_pack = _pack([loss[0, :1]] + [small_g[nm] for nm in names])
    late = [gbuf[nm] for nm in BIG[:1]]
    small_all, recv_late = _comm_call([_GatherAll(small_pack), _Scatter(late, BIG_AXIS[:1])], [[small_pack], late],
                                      "reduce_small_scatter_qkv")
    total = _sum8(small_all, "sum_small")
    parts = _unpack(total.reshape(-1), [(1,)] + [small_g[nm].shape for nm in names])
    loss_out = parts[0][0]
    for nm, g in zip(names, parts[1:]):
        if nm in SMALL_SHARD:
            size = w[nm].shape[-1]
            g = lax.dynamic_slice_in_dim(g, chip * size, size, axis=g.ndim - 1)
        grads[nm] = g.reshape(w[nm].shape)

    packed = [_pack([src[nm] for nm in names]) for src in (w, grads, m, v)]
    _, sd, sm, sv = _adamw(packed[0], packed[1], None, packed[2], packed[3], "adamw_small")
    shapes = [w[nm].shape for nm in names]
    delta = dict(zip(names, _unpack(sd.reshape(-1), shapes)))
    new_m = dict(zip(names, _unpack(sm.reshape(-1), shapes)))
    new_v = dict(zip(names, _unpack(sv.reshape(-1), shapes)))

    sums_late = [sum4(BIG[0], recv_late)]
    others_late = _comm_call([_Swap(sums_late)], [sums_late], "swap_qkv")
    for nm, mine, other in zip(BIG, sums_late + sums_early, list(others_late) + others_early):
        cols = mine.shape[-1]
        outs = _adamw(w[nm].reshape(-1, cols), mine, other, m[nm].reshape(-1, cols), v[nm].reshape(-1, cols), f"adamw_{nm}")
        grads[nm], delta[nm], new_m[nm], new_v[nm] = (o.reshape(w[nm].shape) for o in outs)

    return loss_out, grad_x, grads, delta, new_m, new_v


def kernel(x, mem, attn_norm, attn_w_qkv, attn_q_gain, attn_k_gain, attn_w_o, pool_norm, pool_w, pool_scale, xattn_norm, mem_norm, xattn_w_q, xattn_w_kv, xattn_w_o, ffn_norm, ffn_w_up, ffn_conv_w, ffn_conv_b, ffn_w_down, final_norm, loss_target, m_attn_norm, m_attn_w_qkv, m_attn_q_gain, m_attn_k_gain, m_attn_w_o, m_pool_norm, m_pool_w, m_pool_scale, m_xattn_norm, m_mem_norm, m_xattn_w_q, m_xattn_w_kv, m_xattn_w_o, m_ffn_norm, m_ffn_w_up, m_ffn_conv_w, m_ffn_conv_b, m_ffn_w_down, m_final_norm, v_attn_norm, v_attn_w_qkv, v_attn_q_gain, v_attn_k_gain, v_attn_w_o, v_pool_norm, v_pool_w, v_pool_scale, v_xattn_norm, v_mem_norm, v_xattn_w_q, v_xattn_w_kv, v_xattn_w_o, v_ffn_norm, v_ffn_w_up, v_ffn_conv_w, v_ffn_conv_b, v_ffn_w_down, v_final_norm):
    given = dict(locals())
    w = {nm: given[nm] for nm in ORDER}
    m = {nm: given["m_" + nm] for nm in ORDER}
    v = {nm: given["v_" + nm] for nm in ORDER}
    seq, d = x.shape[1], x.shape[2]
    loss, grad_x, grads, delta, new_m, new_v = _step(
        x.reshape(seq, d), mem.reshape(mem.shape[1], d), loss_target.reshape(seq, d), w, m, v)
    return (loss, grad_x.reshape(x.shape), *[grads[nm] for nm in ORDER], *[delta[nm] for nm in ORDER],
            *[new_m[nm] for nm in ORDER], *[new_v[nm] for nm in ORDER])
```

```python
import functools
import itertools

import jax
import jax.numpy as jnp
from jax import lax
from jax.experimental import pallas as pl
from jax.experimental.pallas import tpu as pltpu

F32, BF16 = jnp.float32, jnp.bfloat16
EPS = 1e-6
GRID_W = 64
ROPE_THETA = 10000.0
HEAD_DIM = 64
N_HEADS = 16
N_KV = 4
X_HEADS = 4
X_HEAD_DIM = 256
POOL_GROUPS = 4
POOL_GROUP_W = 256
HALO = 16
HALO_TR = 1024
LANES = 128
ADAM_LR, ADAM_B1, ADAM_B2, ADAM_EPS, ADAM_WD, ADAM_STEP = 0.001, 0.9, 0.999, 1e-08, 0.01, 10
VMEM_LIMIT = 48 * 1024 * 1024
MESH = pl.DeviceIdType.MESH
NEG = -1e30
LOG2E = 1.4426950408889634
FLASH_TQ, FLASH_TK = 512, 4096
FLASH_SUB = 512
ANY = pl.BlockSpec(memory_space=pl.ANY)


def _cp(sem=None):
    return pltpu.CompilerParams(dimension_semantics=sem, vmem_limit_bytes=VMEM_LIMIT)


def _pick(n, cands):
    for c in cands:
        if c <= n and n % c == 0:
            return c
    return n


def _mm(a, b, mode, *, name, out_dtype, tm=None, tn=None, tk=None, n=None, k=None, b_l=None, b_off=(0, 0),
        res=None, scale=None, out_full=None, out_l=None, out_off=(0, 0), alias=None, norm_out=None, norm_bwd=None):
    if mode == "tn":
        K, M = a.shape
    else:
        M, K = a.shape
    bs = b.shape[-2:]
    if mode == "nn":
        K = k or K
        N = n or bs[1]
    elif mode == "nt":
        N = n or bs[0]
    else:
        N = n or bs[1]
    wide = (1408, 1024, 512, 256, 128)
    if mode == "tn":
        tm = tm or (M if M <= 1024 else _pick(M, wide))
        tk = tk or _pick(K, (2048, 1024, 512, 256, 128))
    else:
        tm = _pick(M, (tm or 512, 256, 128))
        tk = tk or (K if K <= 2816 else _pick(K, wide))
    tn = tn or (N if N <= 1536 else _pick(N, wide))
    assert M % tm == 0 and N % tn == 0 and K % tk == 0, (name, M, N, K, tm, tn, tk)
    nk = K // tk
    dims = {"nn": ((1,), (0,)), "nt": ((1,), (1,)), "tn": ((0,), (0,))}[mode]

    j_outer = nk == 1 and mode != "tn"

    def at(f):
        return (lambda j, i, kk: f(i, j, kk)) if j_outer else f

    if mode == "tn":
        a_spec = pl.BlockSpec((tk, tm), at(lambda i, j, kk: (kk, i)))
    else:
        a_spec = pl.BlockSpec((tm, tk), at(lambda i, j, kk: (i, kk)))
    if mode == "nt":
        bb, (d0, d1) = (tn, tk), (b_off[0] // tn, b_off[1] // tk)
        assert b_off[0] % tn == 0 and b_off[1] % tk == 0
        bidx = lambda i, j, kk: (j + d0, kk + d1)
    else:
        bb, (d0, d1) = (tk, tn), (b_off[0] // tk, b_off[1] // tn)
        assert b_off[0] % tk == 0 and b_off[1] % tn == 0
        bidx = lambda i, j, kk: (kk + d0, j + d1)
    if b.ndim == 3:
        b_spec = pl.BlockSpec((None,) + bb, at(lambda i, j, kk: (b_l,) + bidx(i, j, kk)))
    else:
        b_spec = pl.BlockSpec(bb, at(bidx))
    in_specs, operands = [a_spec, b_spec], [a, b]
    if res is not None:
        in_specs.append(pl.BlockSpec((tm, tn), at(lambda i, j, kk: (i, j))))
        operands.append(res)
    aliases = {}
    if alias is not None:
        aliases = {len(operands): 0}
        in_specs.append(ANY)
        operands.append(alias)
    if out_full is None:
        out_shape = jax.ShapeDtypeStruct((M, N), out_dtype)
        out_spec = pl.BlockSpec((tm, tn), at(lambda i, j, kk: (i, j)))
    else:
        assert out_off[0] % tm == 0 and out_off[1] % tn == 0
        o0, o1 = out_off[0] // tm, out_off[1] // tn
        out_shape = jax.ShapeDtypeStruct(out_full, out_dtype)
        out_spec = pl.BlockSpec((None, tm, tn), at(lambda i, j, kk: (out_l, i + o0, j + o1)))
    has_res, has_alias = res is not None, alias is not None
    grid = (N // tn, M // tm, nk) if j_outer else (M // tm, N // tn, nk)
    n_extra = 0
    if norm_out is not None or norm_bwd is not None:
        assert j_outer and tn == N and out_full is None, name
        row = pl.BlockSpec((tm, tn), at(lambda i, j, kk: (i, 0)))
        vec = pl.BlockSpec((1, tn), at(lambda i, j, kk: (0, 0)))
        if norm_out is not None:
            in_specs.append(vec)
            operands.append(norm_out[0])
            n_extra = 1
            out_shape = (out_shape, jax.ShapeDtypeStruct((M, N), norm_out[1]))
            out_spec = (out_spec, row)
        else:
            in_specs += [row, vec, row]
            operands += list(norm_bwd)
            n_extra = 3
            out_shape = (out_shape, jax.ShapeDtypeStruct((1, N), F32))
            out_spec = (out_spec, vec)
    n_out = 1 if n_extra == 0 else 2

    def body(*refs):
        a_ref, b_ref = refs[0], refs[1]
        pos = 2
        res_ref = None
        if has_res:
            res_ref = refs[pos]
            pos += 1
        if has_alias:
            pos += 1
        extra = refs[pos:pos + n_extra]
        pos += n_extra
        o_ref, acc_ref = refs[pos], refs[pos + n_out]
        kk = pl.program_id(2)
        part = lax.dot_general(a_ref[...].astype(BF16), b_ref[...].astype(BF16), (dims, ((), ())),
                               preferred_element_type=F32)

        def finish(acc):
            if scale is not None:
                acc = acc * scale
            if res_ref is not None:
                acc = acc + res_ref[...]
            if norm_out is not None:
                r = lax.rsqrt(jnp.mean(acc * acc, axis=-1, keepdims=True) + EPS)
                refs[pos + 1][...] = (acc * r * extra[0][...]).astype(refs[pos + 1].dtype)
            if norm_bwd is not None:
                x_ref, g_ref, dres_ref = extra
                dg_ref, step = refs[pos + 1], pl.program_id(1)
                xv = x_ref[...]
                r = lax.rsqrt(jnp.mean(xv * xv, axis=-1, keepdims=True) + EPS)
                nv = xv * r
                dgp = jnp.sum(acc * nv, axis=0, keepdims=True)

                @pl.when(step == 0)
                def _():
                    dg_ref[...] = dgp

                @pl.when(step > 0)
                def _():
                    dg_ref[...] += dgp

                dn = acc * g_ref[...]
                acc = dres_ref[...] + r * (dn - nv * jnp.mean(dn * nv, axis=-1, keepdims=True))
            o_ref[...] = acc.astype(o_ref.dtype)

        if nk == 1:
            finish(part)
        else:
            @pl.when(kk == 0)
            def _():
                acc_ref[...] = part

            @pl.when(jnp.logical_and(kk > 0, kk < nk - 1))
            def _():
                acc_ref[...] += part

            @pl.when(kk == nk - 1)
            def _():
                finish(acc_ref[...] + part)

    return pl.pallas_call(
        body, out_shape=out_shape, grid=grid, in_specs=in_specs, out_specs=out_spec,
        scratch_shapes=[pltpu.VMEM((tm, tn) if nk > 1 else (8, 128), F32)], input_output_aliases=aliases,
        compiler_params=_cp(("arbitrary",) * 3 if norm_bwd is not None else ("parallel", "parallel", "arbitrary")),
        name=name)(*operands)


def _hosted(plan, nsteps, step, compute):
    if plan is None:
        return compute()

    @pl.when(step == 0)
    def _():
        plan.start()

    compute()

    @pl.when(step == nsteps - 1)
    def _():
        if hasattr(plan, "forward"):
            plan.forward()
        plan.finish()


def _rms_fwd(x, gain, out_dtype, name, plan=None, sends=()):
    rows, d = x.shape
    tr = _pick(rows, (512, 256))
    nt, nsteps = (plan.nt if plan is not None else 0), rows // tr

    def body(x_ref, g_ref, *rest):
        o_ref = rest[nt]
        if plan is not None:
            plan.bind(rest[:nt], rest[nt + 1:2 * nt + 1], *rest[2 * nt + 1:])

        def compute():
            xv = x_ref[...]
            r = lax.rsqrt(jnp.mean(xv * xv, axis=-1, keepdims=True) + EPS)
            o_ref[...] = (xv * r * g_ref[...]).astype(o_ref.dtype)

        _hosted(plan, nsteps, pl.program_id(0), compute)

    out = jax.ShapeDtypeStruct((rows, d), out_dtype)
    row = pl.BlockSpec((tr, d), lambda i: (i, 0))
    in_specs = [row, pl.BlockSpec((1, d), lambda i: (0, 0))]
    if plan is None:
        return pl.pallas_call(body, out_shape=out, grid=(nsteps,), in_specs=in_specs, out_specs=row,
                              compiler_params=_cp(("parallel",)), name=name)(x, gain)
    return pl.pallas_call(
        body, out_shape=(out, *plan.out_shape), grid=(nsteps,), in_specs=in_specs + [ANY] * nt,
        out_specs=(row, *([ANY] * nt)), scratch_shapes=plan.scratch, compiler_params=_cp(("arbitrary",)),
        name=name)(x, gain, *sends)


def _rms_bwd(x, gain, dh, dres, name):
    rows, d = x.shape
    tr = _pick(rows, (512, 256))
    need_dx = dres is not None

    def body(*refs):
        if need_dx:
            x_ref, g_ref, dh_ref, dres_ref, o_ref, dg_ref = refs
        else:
            x_ref, g_ref, dh_ref, dg_ref = refs
        i = pl.program_id(0)
        xv = x_ref[...]
        dhv = dh_ref[...].astype(F32)
        r = lax.rsqrt(jnp.mean(xv * xv, axis=-1, keepdims=True) + EPS)
        nv = xv * r
        part = jnp.sum(dhv * nv, axis=0, keepdims=True)

        @pl.when(i == 0)
        def _():
            dg_ref[...] = part

        @pl.when(i > 0)
        def _():
            dg_ref[...] += part

        if need_dx:
            dn = dhv * g_ref[...]
            dx = r * (dn - nv * jnp.mean(dn * nv, axis=-1, keepdims=True))
            o_ref[...] = dres_ref[...] + dx

    row_spec = pl.BlockSpec((tr, d), lambda i: (i, 0))
    vec_spec = pl.BlockSpec((1, d), lambda i: (0, 0))
    if need_dx:
        return pl.pallas_call(
            body, out_shape=(jax.ShapeDtypeStruct((rows, d), F32), jax.ShapeDtypeStruct((1, d), F32)),
            grid=(rows // tr,), in_specs=[row_spec, vec_spec, row_spec, row_spec], out_specs=(row_spec, vec_spec),
            compiler_params=_cp(("arbitrary",)), name=name)(x, gain, dh, dres)
    return None, pl.pallas_call(
        body, out_shape=jax.ShapeDtypeStruct((1, d), F32), grid=(rows // tr,),
        in_specs=[row_spec, vec_spec, row_spec], out_specs=vec_spec,
        compiler_params=_cp(("arbitrary",)), name=name)(x, gain, dh)


def _final_loss(x, gain, target, name):
    rows, d = x.shape
    tr = _pick(rows, (512, 256))
    nsteps = rows // tr

    def body(x_ref, g_ref, t_ref, dx_ref, dg_ref, loss_ref, acc_ref):
        i = pl.program_id(0)
        xv = x_ref[...]
        g = g_ref[...]
        r = lax.rsqrt(jnp.mean(xv * xv, axis=-1, keepdims=True) + EPS)
        nv = xv * r
        err = nv * g - t_ref[...]
        dy = err * (1.0 / d)
        dn = dy * g
        dx_ref[...] = r * (dn - nv * jnp.mean(dn * nv, axis=-1, keepdims=True))
        dgp = jnp.sum(dy * nv, axis=0, keepdims=True)
        lp = jnp.sum(err * err, axis=0, keepdims=True)

        @pl.when(i == 0)
        def _():
            dg_ref[...] = dgp
            acc_ref[...] = lp

        @pl.when(i > 0)
        def _():
            dg_ref[...] += dgp
            acc_ref[...] += lp

        @pl.when(i == nsteps - 1)
        def _():
            tot = jnp.sum(acc_ref[...], axis=1, keepdims=True) * (0.5 / d)
            loss_ref[...] = jnp.broadcast_to(tot, loss_ref.shape)

    row_spec = pl.BlockSpec((tr, d), lambda i: (i, 0))
    vec_spec = pl.BlockSpec((1, d), lambda i: (0, 0))
    return pl.pallas_call(
        body, out_shape=(jax.ShapeDtypeStruct((rows, d), F32), jax.ShapeDtypeStruct((1, d), F32),
                         jax.ShapeDtypeStruct((1, LANES), F32)),
        grid=(nsteps,), in_specs=[row_spec, vec_spec, row_spec],
        out_specs=(row_spec, vec_spec, pl.BlockSpec((1, LANES), lambda i: (0, 0))),
        scratch_shapes=[pltpu.VMEM((1, d), F32)], compiler_params=_cp(("arbitrary",)), name=name)(x, gain, target)


def _rope_tables(seq):
    pairs = HEAD_DIM // 4
    lane = jnp.arange(LANES, dtype=jnp.int32) % HEAD_DIM
    by_col, second, pair = lane // (2 * pairs) == 1, (lane % (2 * pairs)) // pairs == 1, lane % pairs
    inv_freq = ROPE_THETA ** (-pair.astype(F32) / pairs)
    t = jnp.arange(seq, dtype=jnp.int32)[:, None]
    pos = jnp.where(by_col[None, :], t % GRID_W, t // GRID_W).astype(F32)
    ang = pos * inv_freq[None, :]
    cos, sin = jnp.cos(ang), jnp.sin(ang)
    return cos, jnp.where(second[None, :], sin, 0.0), jnp.where(second[None, :], 0.0, -sin)


def _pair_norm(xv, lo):
    sq = xv * xv
    s_lo = jnp.sum(jnp.where(lo, sq, 0.0), axis=1, keepdims=True)
    s_hi = jnp.sum(jnp.where(lo, 0.0, sq), axis=1, keepdims=True)
    return lax.rsqrt(jnp.where(lo, s_lo, s_hi) * (1.0 / HEAD_DIM) + EPS)


def _rope(y, c, sp, sm):
    return y * c + pltpu.roll(y, 16, axis=1) * sp + pltpu.roll(y, LANES - 16, axis=1) * sm


def _rope_t(dz, c, sp, sm):
    return dz * c + pltpu.roll(dz * sp, LANES - 16, axis=1) + pltpu.roll(dz * sm, 16, axis=1)


def _qk_prep(qkv, qg2, kg2, tabs, plan, sends, name):
    seq = qkv.shape[0]
    ts = _pick(seq, (256, 128))
    nq, nkp = N_HEADS // 2, N_KV // 2
    qw, kw = N_HEADS * HEAD_DIM, N_KV * HEAD_DIM
    nt, nsteps = plan.nt, seq // ts

    def body(x_ref, qg_ref, kg_ref, c_ref, sp_ref, sm_ref, *rest):
        q_ref, k_ref, kt_ref, v_ref, vt_ref = rest[nt:nt + 5]
        plan.bind(rest[:nt], rest[nt + 5:2 * nt + 5], *rest[2 * nt + 5:])
        _hosted(plan, nsteps, pl.program_id(0), lambda: compute(x_ref, qg_ref, kg_ref, c_ref, sp_ref, sm_ref,
                                                                q_ref, k_ref, kt_ref, v_ref, vt_ref))

    def compute(x_ref, qg_ref, kg_ref, c_ref, sp_ref, sm_ref, q_ref, k_ref, kt_ref, v_ref, vt_ref):
        lo = lax.broadcasted_iota(jnp.int32, (ts, LANES), 1) < HEAD_DIM
        top = lax.broadcasted_iota(jnp.int32, (LANES, ts), 0) < HEAD_DIM
        c, sp, sm = c_ref[...], sp_ref[...], sm_ref[...]
        for i in range(nq):
            xv = x_ref[:, i * LANES:(i + 1) * LANES]
            y = xv * _pair_norm(xv, lo) * qg_ref[...]
            q_ref[:, i * LANES:(i + 1) * LANES] = (_rope(y, c, sp, sm) * (LOG2E * HEAD_DIM ** -0.5)).astype(BF16)
        for i in range(nkp):
            xv = x_ref[:, qw + i * LANES:qw + (i + 1) * LANES]
            z = _rope(xv * _pair_norm(xv, lo) * kg_ref[...], c, sp, sm)
            k_ref[:, i * LANES:(i + 1) * LANES] = z.astype(BF16)
            kt_ref[i * LANES:(i + 1) * LANES, :] = z.T.astype(BF16)
            vv = x_ref[:, qw + kw + i * LANES:qw + kw + (i + 1) * LANES]
            v_ref[:, i * LANES:(i + 1) * LANES] = vv.astype(BF16)
            vvt = vv.T
            vt_ref[(2 * i) * LANES:(2 * i + 1) * LANES, :] = jnp.where(top, vvt, 1.0).astype(BF16)
            vt_ref[(2 * i + 1) * LANES:(2 * i + 2) * LANES, :] = jnp.where(top, 1.0, vvt).astype(BF16)

    tab = pl.BlockSpec((ts, LANES), lambda i: (i, 0))
    vec = pl.BlockSpec((1, LANES), lambda i: (0, 0))
    return pl.pallas_call(
        body,
        out_shape=(jax.ShapeDtypeStruct((seq, qw), BF16), jax.ShapeDtypeStruct((seq, kw), BF16),
                   jax.ShapeDtypeStruct((kw, seq), BF16), jax.ShapeDtypeStruct((seq, kw), BF16),
                   jax.ShapeDtypeStruct((N_KV * LANES, seq), BF16), *plan.out_shape),
        grid=(nsteps,),
        in_specs=[pl.BlockSpec((ts, qw + 2 * kw), lambda i: (i, 0)), vec, vec, tab, tab, tab] + [ANY] * nt,
        out_specs=(pl.BlockSpec((ts, qw), lambda i: (i, 0)), pl.BlockSpec((ts, kw), lambda i: (i, 0)),
                   pl.BlockSpec((kw, ts), lambda i: (0, i)), pl.BlockSpec((ts, kw), lambda i: (i, 0)),
                   pl.BlockSpec((N_KV * LANES, ts), lambda i: (0, i)), *([ANY] * nt)),
        scratch_shapes=plan.scratch, compiler_params=_cp(("arbitrary",)), name=name)(qkv, qg2, kg2, *tabs, *sends)


def _qk_prep_bwd(qkv, dq, dk, dv, qg2, kg2, tabs, plan, sends, name):
    seq = qkv.shape[0]
    ts = _pick(seq, (256, 128))
    nq, nkp = N_HEADS // 2, N_KV // 2
    qw, kw = N_HEADS * HEAD_DIM, N_KV * HEAD_DIM
    nt, nsteps = plan.nt, seq // ts

    def body(x_ref, dq_ref, dk_ref, dv_ref, qg_ref, kg_ref, c_ref, sp_ref, sm_ref, *rest):
        o_ref, dqg_ref, dkg_ref = rest[nt:nt + 3]
        plan.bind(rest[:nt], rest[nt + 3:2 * nt + 3], *rest[2 * nt + 3:])
        step = pl.program_id(0)

        @pl.when(step == 0)
        def _():
            plan.start()

        lo = lax.broadcasted_iota(jnp.int32, (ts, LANES), 1) < HEAD_DIM
        c, sp, sm = c_ref[...], sp_ref[...], sm_ref[...]

        def one(xv, dz, gain):
            r = _pair_norm(xv, lo)
            nv = xv * r
            dy = _rope_t(dz, c, sp, sm)
            dgp = jnp.sum(dy * nv, axis=0, keepdims=True)
            dn = dy * gain
            t = dn * nv
            m_lo = jnp.sum(jnp.where(lo, t, 0.0), axis=1, keepdims=True)
            m_hi = jnp.sum(jnp.where(lo, 0.0, t), axis=1, keepdims=True)
            m = jnp.where(lo, m_lo, m_hi) * (1.0 / HEAD_DIM)
            return r * (dn - nv * m), dgp

        dqg = jnp.zeros((1, LANES), F32)
        for i in range(nq):
            sl = slice(i * LANES, (i + 1) * LANES)
            dx, dgp = one(x_ref[:, sl], dq_ref[:, sl] * (HEAD_DIM ** -0.5), qg_ref[...])
            o_ref[:, sl] = dx.astype(BF16)
            dqg = dqg + dgp
        dkg = jnp.zeros((1, LANES), F32)
        for i in range(nkp):
            sl = slice(i * LANES, (i + 1) * LANES)
            dx, dgp = one(x_ref[:, qw + i * LANES:qw + (i + 1) * LANES], dk_ref[:, sl], kg_ref[...])
            o_ref[:, qw + i * LANES:qw + (i + 1) * LANES] = dx.astype(BF16)
            dkg = dkg + dgp
            o_ref[:, qw + kw + i * LANES:qw + kw + (i + 1) * LANES] = dv_ref[:, sl].astype(BF16)

        @pl.when(step == 0)
        def _():
            dqg_ref[...] = dqg
            dkg_ref[...] = dkg

        @pl.when(step > 0)
        def _():
            dqg_ref[...] += dqg
            dkg_ref[...] += dkg

        @pl.when(step == nsteps - 1)
        def _():
            plan.finish()

    tab = pl.BlockSpec((ts, LANES), lambda i: (i, 0))
    vec = pl.BlockSpec((1, LANES), lambda i: (0, 0))
    return pl.pallas_call(
        body,
        out_shape=(jax.ShapeDtypeStruct((seq, qw + 2 * kw), BF16), jax.ShapeDtypeStruct((1, LANES), F32),
                   jax.ShapeDtypeStruct((1, LANES), F32), *plan.out_shape),
        grid=(nsteps,),
        in_specs=[pl.BlockSpec((ts, qw + 2 * kw), lambda i: (i, 0)), pl.BlockSpec((ts, qw), lambda i: (i, 0)),
                  pl.BlockSpec((ts, kw), lambda i: (i, 0)), pl.BlockSpec((ts, kw), lambda i: (i, 0)),
                  vec, vec, tab, tab, tab] + [ANY] * nt,
        out_specs=(pl.BlockSpec((ts, qw + 2 * kw), lambda i: (i, 0)), vec, vec, *([ANY] * nt)),
        scratch_shapes=plan.scratch, compiler_params=_cp(("arbitrary",)), name=name)(qkv, dq, dk, dv, qg2, kg2, *tabs, *sends)


def _slot(blk, off0, tq):
    half = lax.broadcasted_iota(jnp.int32, (tq, LANES), 1) // HEAD_DIM
    keep = half == jnp.where(off0, 0, 1)
    parts = []
    for p in range(2):
        pair = blk[:, p * LANES:(p + 1) * LANES].astype(F32)
        rolled = pltpu.roll(pair, HEAD_DIM, axis=1)
        parts.append(jnp.where(keep, jnp.where(off0, pair, rolled), 0.0))
        parts.append(jnp.where(keep, jnp.where(off0, rolled, pair), 0.0))
    return jnp.concatenate(parts, axis=0)


def _unslot(x4, off0, tq):
    lo = lax.broadcasted_iota(jnp.int32, (tq, LANES), 1) < HEAD_DIM
    pairs = []
    for p in range(2):
        h0 = x4[(2 * p) * tq:(2 * p + 1) * tq]
        h1 = x4[(2 * p + 1) * tq:(2 * p + 2) * tq]
        a = jnp.where(off0, h0, pltpu.roll(h0, HEAD_DIM, axis=1))
        b = jnp.where(off0, pltpu.roll(h1, HEAD_DIM, axis=1), h1)
        pairs.append(jnp.where(lo, a, b))
    return jnp.concatenate(pairs, axis=1)


def _flash_fwd(q, k, vt, plan, shards, name):
    seq = q.shape[0]
    tq = _pick(seq, (FLASH_TQ, 128))
    tk = _pick(seq, (FLASH_TK, 2048, 512, 256, 128))
    sub = _pick(tk, (FLASH_SUB, 256, 128))
    nq, nkv, nsub = seq // tq, seq // tk, tk // sub
    gw = 4 * HEAD_DIM
    nt = plan.nt

    def body(q_ref, k_ref, vt_ref, *rest):
        o_ref, lse_ref = rest[nt:nt + 2]
        q4_ref, m_ref, acc_ref, st_ref = rest[2 * nt + 2:2 * nt + 6]
        plan.bind(rest[:nt], rest[nt + 2:2 * nt + 2], *rest[2 * nt + 6:])
        g, qi, ki = pl.program_id(0), pl.program_id(1), pl.program_id(2)
        off0 = (g % 2) == 0
        @pl.when(jnp.logical_and(g == 0, jnp.logical_and(qi == 0, ki == 0)))
        def _():
            plan.start()

        @pl.when(jnp.logical_and(g == N_KV - 1, jnp.logical_and(qi == nq - 1, ki == 0)))
        def _():
            plan.forward()

        @pl.when(ki == 0)
        def _():
            q4_ref[...] = _slot(q_ref[...], off0, tq).astype(BF16)
            m_ref[...] = jnp.full(m_ref.shape, NEG, F32)
            acc_ref[...] = jnp.zeros(acc_ref.shape, F32)

        q4 = q4_ref[...]

        def scores(c):
            st_ref[c % 2] = lax.dot_general(k_ref[c * sub:(c + 1) * sub, :], q4, (((1,), (1,)), ((), ())),
                                            preferred_element_type=F32)

        m, acc = m_ref[...], acc_ref[...]
        scores(0)
        for c in range(nsub):
            if c + 1 < nsub:
                scores(c + 1)
            st = st_ref[c % 2]
            m_new = jnp.maximum(m, jnp.max(st, axis=0, keepdims=True))
            pt = jnp.exp2(st - m_new).astype(BF16)
            acc = jnp.exp2(m - m_new) * acc + jnp.dot(vt_ref[:, c * sub:(c + 1) * sub], pt, preferred_element_type=F32)
            m = m_new
        m_ref[...] = m
        acc_ref[...] = acc

        @pl.when(ki == nkv - 1)
        def _():
            acc = acc_ref[...]
            l = jnp.where(off0, acc[HEAD_DIM:HEAD_DIM + 1], acc[0:1])
            o4 = acc.T
            o4 = o4 / pltpu.roll(o4, HEAD_DIM, axis=1)
            o_ref[...] = _unslot(o4, off0, tq).astype(o_ref.dtype)
            lse_ref[...] = jnp.broadcast_to(m_ref[...] + jnp.log2(l), lse_ref.shape)

        @pl.when(jnp.logical_and(g == N_KV - 1, jnp.logical_and(qi == nq - 1, ki == nkv - 1)))
        def _():
            plan.finish()

    return pl.pallas_call(
        body,
        out_shape=(jax.ShapeDtypeStruct((seq, N_HEADS * HEAD_DIM), BF16),
                   jax.ShapeDtypeStruct((N_KV * nq * 8, 4 * tq), F32), *plan.out_shape),
        grid=(N_KV, nq, nkv),
        in_specs=[pl.BlockSpec((tq, gw), lambda g, qi, ki: (qi, g)),
                  pl.BlockSpec((tk, LANES), lambda g, qi, ki: (ki, g // 2)),
                  pl.BlockSpec((LANES, tk), lambda g, qi, ki: (g, ki))] + [ANY] * nt,
        out_specs=(pl.BlockSpec((tq, gw), lambda g, qi, ki: (qi, g)),
                   pl.BlockSpec((8, 4 * tq), lambda g, qi, ki: (g * nq + qi, 0)), *([ANY] * nt)),
        scratch_shapes=[pltpu.VMEM((4 * tq, LANES), BF16), pltpu.VMEM((1, 4 * tq), F32),
                        pltpu.VMEM((LANES, 4 * tq), F32), pltpu.VMEM((2, sub, 4 * tq), F32)] + plan.scratch,
        compiler_params=_cp(("arbitrary", "arbitrary", "arbitrary")), name=name)(q, k, vt, *shards)


def _flash_bwd(q, k, kt, v, do, o, lse, plan, grads, name):
    seq = q.shape[0]
    tq = _pick(seq, (FLASH_TQ, 128))
    tk = _pick(seq, (FLASH_TK, 2048, 512, 256, 128))
    sub = _pick(tk, (FLASH_SUB, 256, 128))
    nq, nkv, nsub = seq // tq, seq // tk, tk // sub
    gw = 4 * HEAD_DIM
    nt = plan.nt

    def body(q_ref, k_ref, kt_ref, v_ref, do_ref, o_ref, lse_ref, *rest):
        dq_ref, dk_ref, dv_ref = rest[nt:nt + 3]
        q4_ref, do4_ref, delta_ref, dqt_ref, st_ref, dpt_ref = rest[2 * nt + 3:2 * nt + 9]
        plan.bind(rest[:nt], rest[nt + 3:2 * nt + 3], *rest[2 * nt + 9:])
        g, qi, ki = pl.program_id(0), pl.program_id(1), pl.program_id(2)
        off0 = (g % 2) == 0

        @pl.when(jnp.logical_and(g == 0, jnp.logical_and(qi == 0, ki == 0)))
        def _():
            plan.start()

        @pl.when(jnp.logical_and(g % 2 == 0, jnp.logical_and(qi == 0, ki == 0)))
        def _():
            dk_ref[...] = jnp.zeros(dk_ref.shape, F32)
            dv_ref[...] = jnp.zeros(dv_ref.shape, F32)

        @pl.when(ki == 0)
        def _():
            q4_ref[...] = _slot(q_ref[...], off0, tq).astype(BF16)
            do4 = _slot(do_ref[...], off0, tq)
            do4_ref[...] = do4.astype(BF16)
            o4 = _slot(o_ref[...], off0, tq)
            delta_ref[...] = jnp.sum((do4 * o4).T, axis=0, keepdims=True)
            dqt_ref[...] = jnp.zeros(dqt_ref.shape, F32)

        q4, do4 = q4_ref[...], do4_ref[...]
        lse_row, delta = lse_ref[0:1, :], delta_ref[...]

        def products(c):
            rows = slice(c * sub, (c + 1) * sub)
            st_ref[c % 2] = lax.dot_general(k_ref[rows, :], q4, (((1,), (1,)), ((), ())), preferred_element_type=F32)
            dpt_ref[c % 2] = lax.dot_general(v_ref[rows, :], do4, (((1,), (1,)), ((), ())), preferred_element_type=F32)

        dqt = dqt_ref[...]
        products(0)
        for c in range(nsub):
            if c + 1 < nsub:
                products(c + 1)
            pt = jnp.exp2(st_ref[c % 2] - lse_row)
            dst = (pt * (dpt_ref[c % 2] - delta)).astype(BF16)
            rows = pl.ds(pl.multiple_of(ki * tk + c * sub, sub), sub)
            dv_ref[rows, :] += jnp.dot(pt.astype(BF16), do4, preferred_element_type=F32)
            dk_ref[rows, :] += jnp.dot(dst, q4, preferred_element_type=F32) * (1.0 / LOG2E)
            dqt = dqt + jnp.dot(kt_ref[:, c * sub:(c + 1) * sub], dst, preferred_element_type=F32)
        dqt_ref[...] = dqt

        @pl.when(ki == nkv - 1)
        def _():
            dq_ref[...] = _unslot(dqt_ref[...].T, off0, tq)

        @pl.when(jnp.logical_and(g == N_KV - 1, jnp.logical_and(qi == nq - 1, ki == nkv - 1)))
        def _():
            plan.finish()

    return pl.pallas_call(
        body,
        out_shape=(jax.ShapeDtypeStruct((seq, N_HEADS * HEAD_DIM), F32),
                   jax.ShapeDtypeStruct((seq, N_KV * HEAD_DIM), F32), jax.ShapeDtypeStruct((seq, N_KV * HEAD_DIM), F32),
                   *plan.out_shape),
        grid=(N_KV, nq, nkv),
        in_specs=[pl.BlockSpec((tq, gw), lambda g, qi, ki: (qi, g)),
                  pl.BlockSpec((tk, LANES), lambda g, qi, ki: (ki, g // 2)),
                  pl.BlockSpec((LANES, tk), lambda g, qi, ki: (g // 2, ki)),
                  pl.BlockSpec((tk, LANES), lambda g, qi, ki: (ki, g // 2)),
                  pl.BlockSpec((tq, gw), lambda g, qi, ki: (qi, g)),
                  pl.BlockSpec((tq, gw), lambda g, qi, ki: (qi, g)),
                  pl.BlockSpec((8, 4 * tq), lambda g, qi, ki: (g * nq + qi, 0))] + [ANY] * nt,
        out_specs=(pl.BlockSpec((tq, gw), lambda g, qi, ki: (qi, g)),
                   pl.BlockSpec((seq, LANES), lambda g, qi, ki: (0, g // 2)),
                   pl.BlockSpec((seq, LANES), lambda g, qi, ki: (0, g // 2)), *([ANY] * nt)),
        scratch_shapes=[pltpu.VMEM((4 * tq, LANES), BF16), pltpu.VMEM((4 * tq, LANES), BF16),
                        pltpu.VMEM((1, 4 * tq), F32), pltpu.VMEM((LANES, 4 * tq), F32),
                        pltpu.VMEM((2, sub, 4 * tq), F32), pltpu.VMEM((2, sub, 4 * tq), F32)] + plan.scratch,
        compiler_params=_cp(("arbitrary", "arbitrary", "arbitrary")), name=name)(q, k, kt, v, do, o, lse, *grads)


def _xattn_fwd(q, kv, name):
    seq, d = q.shape
    mlen = kv.shape[0]
    tq = _pick(seq, (512, 256))

    def body(q_ref, k_ref, v_ref, o_ref):
        for h in range(X_HEADS):
            sl = slice(h * X_HEAD_DIM, (h + 1) * X_HEAD_DIM)
            s = lax.dot_general(q_ref[:, sl], k_ref[:, sl], (((1,), (1,)), ((), ())), preferred_element_type=F32)
            e = jnp.exp(s - jnp.max(s, axis=-1, keepdims=True))
            p = e / jnp.sum(e, axis=-1, keepdims=True)
            o_ref[:, sl] = jnp.dot(p.astype(BF16), v_ref[:, sl], preferred_element_type=F32).astype(o_ref.dtype)

    return pl.pallas_call(
        body, out_shape=jax.ShapeDtypeStruct((seq, d), BF16), grid=(seq // tq,),
        in_specs=[pl.BlockSpec((tq, d), lambda i: (i, 0)), pl.BlockSpec((mlen, d), lambda i: (0, 0)),
                  pl.BlockSpec((mlen, d), lambda i: (0, 1))],
        out_specs=pl.BlockSpec((tq, d), lambda i: (i, 0)), compiler_params=_cp(("parallel",)), name=name)(q, kv, kv)


def _xattn_bwd(q, kv, do, name):
    seq, d = q.shape
    mlen = kv.shape[0]
    tq = _pick(seq, (512, 256))
    scale = X_HEAD_DIM ** -0.5

    def body(q_ref, k_ref, v_ref, do_ref, dq_ref, dkv_ref):
        i = pl.program_id(0)

        @pl.when(i == 0)
        def _():
            dkv_ref[...] = jnp.zeros(dkv_ref.shape, F32)

        for h in range(X_HEADS):
            sl = slice(h * X_HEAD_DIM, (h + 1) * X_HEAD_DIM)
            qh, kh, vh = q_ref[:, sl], k_ref[:, sl], v_ref[:, sl]
            doh = do_ref[:, sl].astype(BF16)
            st = lax.dot_general(kh, qh, (((1,), (1,)), ((), ())), preferred_element_type=F32)
            e = jnp.exp(st - jnp.max(st, axis=0, keepdims=True))
            pt = e / jnp.sum(e, axis=0, keepdims=True)
            dpt = lax.dot_general(vh, doh, (((1,), (1,)), ((), ())), preferred_element_type=F32)
            dst = (pt * (dpt - jnp.sum(pt * dpt, axis=0, keepdims=True))).astype(BF16)
            dkv_ref[:, sl] += jnp.dot(dst, qh, preferred_element_type=F32)
            dkv_ref[:, d + h * X_HEAD_DIM:d + (h + 1) * X_HEAD_DIM] += jnp.dot(pt.astype(BF16), doh,
                                                                                 preferred_element_type=F32)
            dqh = lax.dot_general(dst, kh, (((0,), (0,)), ((), ())), preferred_element_type=F32)
            dq_ref[:, sl] = (dqh * scale).astype(dq_ref.dtype)

    return pl.pallas_call(
        body, out_shape=(jax.ShapeDtypeStruct((seq, d), BF16), jax.ShapeDtypeStruct((mlen, 2 * d), F32)),
        grid=(seq // tq,),
        in_specs=[pl.BlockSpec((tq, d), lambda i: (i, 0)), pl.BlockSpec((mlen, d), lambda i: (0, 0)),
                  pl.BlockSpec((mlen, d), lambda i: (0, 1)), pl.BlockSpec((tq, d), lambda i: (i, 0))],
        out_specs=(pl.BlockSpec((tq, d), lambda i: (i, 0)), pl.BlockSpec((mlen, 2 * d), lambda i: (0, 0))),
        compiler_params=_cp(("arbitrary",)), name=name)(q, kv, kv, do)


def _halo_specs(tr, tc, seq, col):
    per, last = tr // HALO, seq // HALO - 1
    return [pl.BlockSpec((tr, tc), lambda j, r: (r, col(j))),
            pl.BlockSpec((HALO, tc), lambda j, r: (jnp.maximum(r * per - 1, 0), col(j))),
            pl.BlockSpec((HALO, tc), lambda j, r: (jnp.minimum((r + 1) * per, last), col(j)))]


def _extend(main_ref, prev_ref, next_ref, r, nr):
    pv = (r > 0).astype(F32)
    nv = (r < nr - 1).astype(F32)
    return jnp.concatenate([prev_ref[...].astype(F32) * pv, main_ref[...].astype(F32),
                            next_ref[...].astype(F32) * nv], axis=0)


def _conv3(e, w_ref, n):
    return pltpu.roll(e, 1, axis=0) * w_ref[0:1, :] + e * w_ref[1:2, :] + pltpu.roll(e, n - 1, axis=0) * w_ref[2:3, :]


def _conv_gate_fwd(ug, uv, cw, cb, layer, name, plan=None, shards=(), fulls=()):
    seq, f = ug.shape
    tc = 256
    tr = _pick(seq, (HALO_TR, 512, 256))
    nc, nr = f // tc, seq // tr
    n = tr + 2 * HALO
    nt = plan.nt if plan is not None else 0

    def body(g_ref, gp_ref, gn_ref, v_ref, vp_ref, vn_ref, wg_ref, wv_ref, bg_ref, bv_ref, *rest):
        o_ref = rest[2 * nt]
        j, r = pl.program_id(0), pl.program_id(1)
        if plan is not None:
            plan.bind(rest[:nt], rest[2 * nt + 1:3 * nt + 1], *rest[3 * nt + 1:])

            @pl.when(jnp.logical_and(j == 0, r == 0))
            def _():
                plan.start()

        cg = _conv3(_extend(g_ref, gp_ref, gn_ref, r, nr), wg_ref, n)[HALO:HALO + tr] + bg_ref[...]
        cv = _conv3(_extend(v_ref, vp_ref, vn_ref, r, nr), wv_ref, n)[HALO:HALO + tr] + bv_ref[...]
        o_ref[...] = (cg * jax.nn.sigmoid(cg) * cv).astype(o_ref.dtype)

        if plan is not None:
            @pl.when(jnp.logical_and(j == nc - 1, r == nr - 1))
            def _():
                plan.forward()
                plan.finish()

    w_spec = lambda shift: pl.BlockSpec((None, 3, tc), lambda j, r: (layer, 0, j + shift))
    b_spec = lambda shift: pl.BlockSpec((None, 1, tc), lambda j, r: (layer, 0, j + shift))
    act_shape = jax.ShapeDtypeStruct((seq, f), BF16)
    act_spec = pl.BlockSpec((tr, tc), lambda j, r: (r, j))
    in_specs = _halo_specs(tr, tc, seq, lambda j: j) * 2 + [w_spec(0), w_spec(nc), b_spec(0), b_spec(nc)]
    operands = (ug, ug, ug, uv, uv, uv, cw, cw, cb, cb)
    if plan is None:
        return pl.pallas_call(body, out_shape=act_shape, grid=(nc, nr), in_specs=in_specs, out_specs=act_spec,
                              compiler_params=_cp(("parallel", "parallel")), name=name)(*operands)
    return pl.pallas_call(
        body, out_shape=(act_shape, *plan.out_shape), grid=(nc, nr), in_specs=in_specs + [ANY] * (2 * nt),
        out_specs=(act_spec, *([ANY] * nt)), scratch_shapes=plan.scratch,
        input_output_aliases={len(operands) + nt + t: 1 + t for t in range(nt)},
        compiler_params=_cp(("arbitrary", "arbitrary")), name=name)(*operands, *shards, *fulls)


def _conv_gate_bwd(ug, uv, dact, cw, cb, layer, name):
    seq, f = ug.shape
    tc = 256
    tr = _pick(seq, (HALO_TR, 512, 256))
    nc, nr = f // tc, seq // tr
    n = tr + 2 * HALO

    def body(g_ref, gp_ref, gn_ref, v_ref, vp_ref, vn_ref, d_ref, dp_ref, dn_ref, wg_ref, wv_ref, bg_ref, bv_ref,
             dug_ref, duv_ref, dwg_ref, dwv_ref):
        r = pl.program_id(1)
        eg = _extend(g_ref, gp_ref, gn_ref, r, nr)
        ev = _extend(v_ref, vp_ref, vn_ref, r, nr)
        da = _extend(d_ref, dp_ref, dn_ref, r, nr)
        eg3 = (pltpu.roll(eg, 1, axis=0), eg, pltpu.roll(eg, n - 1, axis=0))
        ev3 = (pltpu.roll(ev, 1, axis=0), ev, pltpu.roll(ev, n - 1, axis=0))
        cg = eg3[0] * wg_ref[0:1, :] + eg3[1] * wg_ref[1:2, :] + eg3[2] * wg_ref[2:3, :] + bg_ref[...]
        cv = ev3[0] * wv_ref[0:1, :] + ev3[1] * wv_ref[1:2, :] + ev3[2] * wv_ref[2:3, :] + bv_ref[...]
        sg = jax.nn.sigmoid(cg)
        dcv = da * (cg * sg)
        dcg = da * cv * (sg * (1.0 + cg * (1.0 - sg)))

        def back(dc, e3, w_ref, du_ref, dw_ref):
            du = (pltpu.roll(dc, n - 1, axis=0) * w_ref[0:1, :] + dc * w_ref[1:2, :]
                  + pltpu.roll(dc, 1, axis=0) * w_ref[2:3, :])
            du_ref[...] = du[HALO:HALO + tr].astype(du_ref.dtype)
            dcm = dc[HALO:HALO + tr]
            taps = [jnp.sum(dcm * e[HALO:HALO + tr], axis=0, keepdims=True) for e in e3] + [
                    jnp.sum(dcm, axis=0, keepdims=True)]
            part = jnp.concatenate(taps + [jnp.zeros((4, tc), F32)], axis=0)

            @pl.when(r == 0)
            def _():
                dw_ref[...] = part

            @pl.when(r > 0)
            def _():
                dw_ref[...] += part

        back(dcg, eg3, wg_ref, dug_ref, dwg_ref)
        back(dcv, ev3, wv_ref, duv_ref, dwv_ref)

    w_spec = lambda shift: pl.BlockSpec((None, 3, tc), lambda j, r: (layer, 0, j + shift))
    b_spec = lambda shift: pl.BlockSpec((None, 1, tc), lambda j, r: (layer, 0, j + shift))
    out_rows = pl.BlockSpec((tr, tc), lambda j, r: (r, j))
    out_acc = pl.BlockSpec((8, tc), lambda j, r: (0, j))
    return pl.pallas_call(
        body,
        out_shape=(jax.ShapeDtypeStruct((seq, f), BF16), jax.ShapeDtypeStruct((seq, f), BF16),
                   jax.ShapeDtypeStruct((8, f), F32), jax.ShapeDtypeStruct((8, f), F32)),
        grid=(nc, nr),
        in_specs=_halo_specs(tr, tc, seq, lambda j: j) * 3 + [w_spec(0), w_spec(nc), b_spec(0), b_spec(nc)],
        out_specs=(out_rows, out_rows, out_acc, out_acc),
        compiler_params=_cp(("parallel", "arbitrary")), name=name)(ug, ug, ug, uv, uv, uv, dact, dact, dact, cw, cw, cb, cb)


def _pool_count(g, r, tr, n, seq):
    half = jnp.left_shift(1, g)
    t = r * tr - HALO + lax.broadcasted_iota(jnp.int32, (n, 1), 0)
    cnt = jnp.minimum(t + half, seq) - jnp.maximum(t - half, 0)
    return jnp.maximum(cnt, 1).astype(F32)


def _by_group(g, levels):
    out = levels[3]
    for i in (2, 1, 0):
        out = jnp.where(g == i, levels[i], out)
    return out


def _pool_mixed(e, g, cnt, n):
    w2 = e + pltpu.roll(e, 1, axis=0)
    w4 = pltpu.roll(w2, 1, axis=0) + pltpu.roll(w2, n - 1, axis=0)
    w8 = pltpu.roll(w4, 2, axis=0) + pltpu.roll(w4, n - 2, axis=0)
    w16 = pltpu.roll(w8, 4, axis=0) + pltpu.roll(w8, n - 4, axis=0)
    return _by_group(g, (w2, w4, w8, w16)) / cnt - e


def _pool_fwd(hp, xres, pw, scale, name, plan, shards, fulls):
    seq, d = hp.shape
    tc = POOL_GROUP_W
    tr = _pick(seq, (HALO_TR, 512, 256))
    nr = seq // tr
    n = tr + 2 * HALO
    nt = plan.nt

    def body(h_ref, hp_ref, hn_ref, x_ref, w_ref, s_ref, *rest):
        o_ref = rest[2 * nt]
        plan.bind(rest[:nt], rest[2 * nt + 1:3 * nt + 1], *rest[3 * nt + 1:])
        g, r = pl.program_id(0), pl.program_id(1)

        def compute():
            e = _extend(h_ref, hp_ref, hn_ref, r, nr)
            mixed = _pool_mixed(e, g, _pool_count(g, r, tr, n, seq), n)[HALO:HALO + tr]
            y = jnp.dot(mixed.astype(BF16), w_ref[...], preferred_element_type=F32)
            o_ref[...] = x_ref[...] + y * s_ref[...]

        _hosted(plan, POOL_GROUPS * nr, g * nr + r, compute)

    operands = (hp, hp, hp, xres, pw, scale)
    return pl.pallas_call(
        body, out_shape=(jax.ShapeDtypeStruct((seq, d), F32), *plan.out_shape), grid=(POOL_GROUPS, nr),
        in_specs=_halo_specs(tr, tc, seq, lambda j: j) + [
            pl.BlockSpec((tr, tc), lambda j, r: (r, j)), pl.BlockSpec((None, tc, tc), lambda j, r: (j, 0, 0)),
            pl.BlockSpec((1, tc), lambda j, r: (0, j))] + [ANY] * (2 * nt),
        out_specs=(pl.BlockSpec((tr, tc), lambda j, r: (r, j)), *([ANY] * nt)), scratch_shapes=plan.scratch,
        input_output_aliases={len(operands) + nt + t: 1 + t for t in range(nt)},
        compiler_params=_cp(("arbitrary", "arbitrary")), name=name)(*operands, *shards, *fulls)


def _pool_bwd(hp, dy, pw, scale, name):
    seq, d = hp.shape
    tc = POOL_GROUP_W
    tr = _pick(seq, (HALO_TR, 512, 256))
    nr = seq // tr
    n = tr + 2 * HALO

    def body(h_ref, hp_ref, hn_ref, d_ref, dp_ref, dn_ref, w_ref, s_ref, dh_ref, dw_ref, ds_ref):
        g, r = pl.program_id(0), pl.program_id(1)
        cnt = _pool_count(g, r, tr, n, seq)
        e = _extend(h_ref, hp_ref, hn_ref, r, nr)
        mixed = _pool_mixed(e, g, cnt, n)[HALO:HALO + tr].astype(BF16)
        dye = _extend(d_ref, dp_ref, dn_ref, r, nr)
        dyp = (dye * s_ref[...]).astype(BF16)
        dmixed = lax.dot_general(dyp, w_ref[...], (((1,), (1,)), ((), ())), preferred_element_type=F32)
        dwin = dmixed / cnt
        m2 = dwin + pltpu.roll(dwin, n - 1, axis=0)
        m4 = pltpu.roll(m2, 1, axis=0) + pltpu.roll(m2, n - 1, axis=0)
        m8 = pltpu.roll(m4, 2, axis=0) + pltpu.roll(m4, n - 2, axis=0)
        m16 = pltpu.roll(m8, 4, axis=0) + pltpu.roll(m8, n - 4, axis=0)
        dh_ref[...] = (_by_group(g, (m2, m4, m8, m16)) - dmixed)[HALO:HALO + tr]
        ypre = jnp.dot(mixed, w_ref[...], preferred_element_type=F32)
        dsp = jnp.sum(d_ref[...] * ypre, axis=0, keepdims=True)
        dwp = lax.dot_general(mixed, dyp[HALO:HALO + tr], (((0,), (0,)), ((), ())), preferred_element_type=F32)

        @pl.when(r == 0)
        def _():
            dw_ref[...] = dwp
            ds_ref[...] = dsp

        @pl.when(r > 0)
        def _():
            dw_ref[...] += dwp
            ds_ref[...] += dsp

    return pl.pallas_call(
        body,
        out_shape=(jax.ShapeDtypeStruct((seq, d), F32), jax.ShapeDtypeStruct((POOL_GROUPS, tc, tc), F32),
                   jax.ShapeDtypeStruct((1, d), F32)),
        grid=(POOL_GROUPS, nr),
        in_specs=_halo_specs(tr, tc, seq, lambda j: j) * 2 + [
            pl.BlockSpec((None, tc, tc), lambda j, r: (j, 0, 0)), pl.BlockSpec((1, tc), lambda j, r: (0, j))],
        out_specs=(pl.BlockSpec((tr, tc), lambda j, r: (r, j)), pl.BlockSpec((None, tc, tc), lambda j, r: (j, 0, 0)),
                   pl.BlockSpec((1, tc), lambda j, r: (0, j))),
        compiler_params=_cp(("parallel", "arbitrary")), name=name)(hp, hp, hp, dy, dy, dy, pw, scale)


def _adamw_math(w, g, m, v):
    m = ADAM_B1 * m + (1.0 - ADAM_B1) * g
    v = ADAM_B2 * v + (1.0 - ADAM_B2) * (g * g)
    m_hat = m / (1.0 - ADAM_B1 ** ADAM_STEP)
    v_hat = v / (1.0 - ADAM_B2 ** ADAM_STEP)
    delta = -ADAM_LR * (m_hat / (jnp.sqrt(v_hat) + ADAM_EPS) + ADAM_WD * w)
    return delta, m, v


def _adamw(w, ga, gb, m, v, name):
    rows, cols = w.shape
    tr = _pick(rows, (256, 128, 64, 32, 16, 8))
    two = gb is not None

    def body(*refs):
        if two:
            w_ref, ga_ref, gb_ref, m_ref, v_ref, g_out, d_out, m_out, v_out = refs
            g = ga_ref[...] + gb_ref[...]
        else:
            w_ref, ga_ref, m_ref, v_ref, g_out, d_out, m_out, v_out = refs
            g = ga_ref[...]
        delta, m, v = _adamw_math(w_ref[...], g, m_ref[...], v_ref[...])
        g_out[...] = g
        d_out[...] = delta
        m_out[...] = m
        v_out[...] = v

    spec = pl.BlockSpec((tr, cols), lambda i: (i, 0))
    ops = [w, ga] + ([gb] if two else []) + [m, v]
    return pl.pallas_call(
        body, out_shape=tuple(jax.ShapeDtypeStruct((rows, cols), F32) for _ in range(4)), grid=(rows // tr,),
        in_specs=[spec] * len(ops), out_specs=(spec,) * 4, compiler_params=_cp(("parallel",)), name=name)(*ops)


def _sum4(parts, name):
    _, rows, cols = parts.shape
    tr = _pick(rows, (256, 128, 64, 32, 16))

    def body(p_ref, o_ref):
        acc = p_ref[0].astype(F32)
        for kk in range(1, 4):
            acc = acc + p_ref[kk].astype(F32)
        o_ref[...] = acc

    return pl.pallas_call(
        body, out_shape=jax.ShapeDtypeStruct((rows, cols), F32), grid=(rows // tr,),
        in_specs=[pl.BlockSpec((4, tr, cols), lambda i: (0, i, 0))], out_specs=pl.BlockSpec((tr, cols), lambda i: (i, 0)),
        compiler_params=_cp(("parallel",)), name=name)(parts)


def _place():
    x, y, c = lax.axis_index("x"), lax.axis_index("y"), lax.axis_index("c")
    chips = [(1 - x, y), (x, 1 - y), (1 - x, 1 - y)]
    return x, y, c, chips


def _window(ref, axis, j, size, c=None, half=None, lead=(), layers=slice(None)):
    if axis == "r":
        if c is None:
            return ref.at[lead + (layers, pl.ds(pl.multiple_of(j * size, 32), size), slice(None))]
        return ref.at[lead + (layers, pl.ds(pl.multiple_of(j * size + c * half, 32), half), slice(None))]
    cols = pl.ds(pl.multiple_of(j * size, LANES), size)
    if c is None:
        return ref.at[lead + (layers, slice(None), cols)]
    return ref.at[lead + (layers, pl.ds(pl.multiple_of(c * half, 32), half), cols)]


class _Gather:
    def __init__(self, shards, axes, layers=None):
        self.nt, self.axes = len(shards), axes
        self.layers = layers or [slice(None)] * self.nt
        self.out_shape, self.sizes, self.halves = [], [], []
        for s, ax in zip(shards, axes):
            l, rs, cs = s.shape
            self.out_shape.append(jax.ShapeDtypeStruct((l, 4 * rs, cs) if ax == "r" else (l, rs, 4 * cs), s.dtype))
            self.sizes.append(rs if ax == "r" else cs)
            self.halves.append(rs // 2)
        self.scratch = [pltpu.SemaphoreType.DMA((6 * self.nt,)), pltpu.SemaphoreType.DMA((6 * self.nt,)),
                        pltpu.SemaphoreType.DMA((self.nt,))]

    def bind(self, src, dst, send_sems, recv_sems, local_sems):
        self.src, self.dst, self.send_sems, self.recv_sems, self.local_sems = src, dst, send_sems, recv_sems, local_sems

    def _win(self, t, j, core=None):
        return _window(self.dst[t], self.axes[t], j, self.sizes[t], core, self.halves[t], layers=self.layers[t])

    def _ici(self, t, kk, origin):
        _, _, c, chips = _place()
        px, py = chips[kk]
        half = self.src[t].at[self.layers[t], pl.ds(pl.multiple_of(c * self.halves[t], 16), self.halves[t]), :]
        return pltpu.make_async_remote_copy(
            src_ref=half, dst_ref=self._win(t, origin, c), send_sem=self.send_sems.at[t * 3 + kk],
            recv_sem=self.recv_sems.at[t * 3 + kk], device_id=(px, py, c), device_id_type=MESH)

    def _d2d(self, t, kk, origin, core):
        x, y, c, _ = _place()
        k2 = 3 * self.nt + t * 3 + kk
        return pltpu.make_async_remote_copy(
            src_ref=self._win(t, origin, core), dst_ref=self._win(t, origin, core), send_sem=self.send_sems.at[k2],
            recv_sem=self.recv_sems.at[k2], device_id=(x, y, 1 - c), device_id_type=MESH)

    def _local(self, t):
        x, y, _, _ = _place()
        return pltpu.make_async_copy(self.src[t].at[self.layers[t]], self._win(t, 2 * x + y), self.local_sems.at[t])

    def _each(self):
        _, _, _, chips = _place()
        for t in range(self.nt):
            for kk in range(3):
                px, py = chips[kk]
                yield t, kk, 2 * px + py

    def start(self):
        x, y, _, _ = _place()
        for t in range(self.nt):
            self._local(t).start()
        for t, kk, _ in self._each():
            self._ici(t, kk, 2 * x + y).start()

    def forward(self):
        _, _, c, _ = _place()
        for t, kk, origin in self._each():
            self._ici(t, kk, origin).wait_recv()
            self._d2d(t, kk, origin, c).start()

    def finish(self):
        x, y, c, _ = _place()
        for t, kk, origin in self._each():
            self._d2d(t, kk, origin, 1 - c).wait_recv()
        for t, kk, origin in self._each():
            self._ici(t, kk, 2 * x + y).wait_send()
            self._d2d(t, kk, origin, c).wait_send()
        for t in range(self.nt):
            self._local(t).wait()


class _Scatter:
    def __init__(self, grads, axes):
        self.nt, self.axes = len(grads), axes
        self.out_shape, self.sizes = [], []
        for gr, ax in zip(grads, axes):
            l, r, cc = gr.shape
            self.out_shape.append(jax.ShapeDtypeStruct((4, l, r // 4, cc) if ax == "r" else (4, l, r, cc // 4), gr.dtype))
            self.sizes.append(r // 4 if ax == "r" else cc // 4)
        self.scratch = [pltpu.SemaphoreType.DMA((3 * self.nt,)), pltpu.SemaphoreType.DMA((3 * self.nt,)),
                        pltpu.SemaphoreType.DMA((self.nt,))]

    def bind(self, src, dst, send_sems, recv_sems, local_sems):
        self.src, self.dst, self.send_sems, self.recv_sems, self.local_sems = src, dst, send_sems, recv_sems, local_sems

    def _copy(self, t, kk, slot):
        x, y, c, chips = _place()
        px, py = chips[kk]
        return pltpu.make_async_remote_copy(
            src_ref=_window(self.src[t], self.axes[t], 2 * px + py, self.sizes[t]), dst_ref=self.dst[t].at[slot],
            send_sem=self.send_sems.at[t * 3 + kk], recv_sem=self.recv_sems.at[t * 3 + kk],
            device_id=(px, py, c), device_id_type=MESH)

    def _local(self, t):
        x, y, _, _ = _place()
        me = 2 * x + y
        return pltpu.make_async_copy(_window(self.src[t], self.axes[t], me, self.sizes[t]), self.dst[t].at[me],
                                     self.local_sems.at[t])

    def start(self):
        x, y, _, _ = _place()
        for t in range(self.nt):
            self._local(t).start()
            for kk in range(3):
                self._copy(t, kk, 2 * x + y).start()

    def finish(self):
        _, _, _, chips = _place()
        for t in range(self.nt):
            for kk in range(3):
                px, py = chips[kk]
                self._copy(t, kk, 2 * px + py).wait_recv()
        for t in range(self.nt):
            for kk in range(3):
                px, py = chips[kk]
                self._copy(t, kk, 2 * px + py).wait_send()
            self._local(t).wait()


def _comm_call(plans, operands, name):
    nts = [p.nt for p in plans]
    n_in, n_sem = sum(nts), [len(p.scratch) for p in plans]

    def body(*refs):
        pos_in, pos_out, pos_sem = 0, n_in, 2 * n_in
        for p, nt, ns in zip(plans, nts, n_sem):
            p.bind(refs[pos_in:pos_in + nt], refs[pos_out:pos_out + nt], *refs[pos_sem:pos_sem + ns])
            pos_in, pos_out, pos_sem = pos_in + nt, pos_out + nt, pos_sem + ns
        for p in plans:
            p.start()
        for p in plans:
            if hasattr(p, "forward"):
                p.forward()
        for p in plans:
            p.finish()

    return pl.pallas_call(
        body, out_shape=tuple(s for p in plans for s in p.out_shape), in_specs=[ANY] * n_in,
        out_specs=tuple([ANY] * n_in), scratch_shapes=[s for p in plans for s in p.scratch],
        name=name)(*[a for ops in operands for a in ops])


class _GatherAll:
    FLIPS = [f for f in itertools.product((0, 1), repeat=3) if any(f)]

    def __init__(self, pack):
        self.nt = 1
        self.out_shape = [jax.ShapeDtypeStruct((8,) + pack.shape, pack.dtype)]
        self.scratch = [pltpu.SemaphoreType.DMA((7,)), pltpu.SemaphoreType.DMA((7,)), pltpu.SemaphoreType.DMA((1,))]

    def bind(self, src, dst, send_sems, recv_sems, local_sems):
        self.src, self.dst, self.send_sems, self.recv_sems, self.local_sems = src[0], dst[0], send_sems, recv_sems, local_sems

    def _copy(self, kk, mine):
        x, y, c, _ = _place()
        px, py, pc = (1 - v if fl else v for v, fl in zip((x, y, c), self.FLIPS[kk]))
        slot = 4 * x + 2 * y + c if mine else 4 * px + 2 * py + pc
        return pltpu.make_async_remote_copy(src_ref=self.src, dst_ref=self.dst.at[slot], send_sem=self.send_sems.at[kk],
                                            recv_sem=self.recv_sems.at[kk], device_id=(px, py, pc), device_id_type=MESH)

    def _local(self):
        x, y, c, _ = _place()
        return pltpu.make_async_copy(self.src, self.dst.at[4 * x + 2 * y + c], self.local_sems.at[0])

    def start(self):
        self._local().start()
        for kk in range(7):
            self._copy(kk, True).start()

    def finish(self):
        for kk in range(7):
            self._copy(kk, False).wait_recv()
        for kk in range(7):
            self._copy(kk, True).wait_send()
        self._local().wait()


class _Swap:
    def __init__(self, arrs):
        self.nt = len(arrs)
        self.out_shape = [jax.ShapeDtypeStruct(a.shape, a.dtype) for a in arrs]
        self.scratch = [pltpu.SemaphoreType.DMA((self.nt,)), pltpu.SemaphoreType.DMA((self.nt,))]

    def bind(self, src, dst, send_sems, recv_sems):
        self.src, self.dst, self.send_sems, self.recv_sems = src, dst, send_sems, recv_sems

    def _copy(self, t):
        x, y, c, _ = _place()
        return pltpu.make_async_remote_copy(src_ref=self.src[t], dst_ref=self.dst[t], send_sem=self.send_sems.at[t],
                                            recv_sem=self.recv_sems.at[t], device_id=(x, y, 1 - c), device_id_type=MESH)

    def start(self):
        for t in range(self.nt):
            self._copy(t).start()

    def finish(self):
        for t in range(self.nt):
            self._copy(t).wait()


def _sum8(packs, name):
    def body(p_ref, o_ref):
        acc = p_ref[0]
        for dev in range(1, 8):
            acc = acc + p_ref[dev]
        o_ref[...] = acc

    return pl.pallas_call(body, out_shape=jax.ShapeDtypeStruct(packs.shape[1:], F32), name=name)(packs)


def _pack(arrs):
    flat = jnp.concatenate([a.reshape(-1).astype(F32) for a in arrs])
    rows = -(-flat.shape[0] // (8 * LANES)) * 8
    return jnp.pad(flat, (0, rows * LANES - flat.shape[0])).reshape(rows, LANES)


def _unpack(flat, shapes):
    out, pos = [], 0
    for shp in shapes:
        size = 1
        for s in shp:
            size *= s
        out.append(flat[pos:pos + size].reshape(shp))
        pos += size
    return out


BIG = ("attn_w_qkv", "attn_w_o", "pool_w", "xattn_w_q", "xattn_w_kv", "xattn_w_o", "ffn_w_up", "ffn_w_down")
BIG_AXIS = ("c", "r", "r", "r", "c", "r", "c", "r")
SMALL_REPL = ("attn_norm", "attn_q_gain", "attn_k_gain", "xattn_norm", "mem_norm", "ffn_norm", "ffn_conv_b", "final_norm")
SMALL_SHARD = ("pool_norm", "pool_scale", "ffn_conv_w")
ORDER = ("attn_norm", "attn_w_qkv", "attn_q_gain", "attn_k_gain", "attn_w_o", "pool_norm", "pool_w", "pool_scale",
         "xattn_norm", "mem_norm", "xattn_w_q", "xattn_w_kv", "xattn_w_o", "ffn_norm", "ffn_w_up", "ffn_conv_w",
         "ffn_conv_b", "ffn_w_down", "final_norm")


def _step(x, mem, tgt, w, m, v):
    seq, d = x.shape
    xi, yi, ci = lax.axis_index("x"), lax.axis_index("y"), lax.axis_index("c")
    chip = 2 * xi + yi
    dff = w["ffn_w_down"].shape[1] * 4
    n_layers = w["ffn_norm"].shape[0]

    def as3d(a):
        return a.reshape(a.shape[-3:])
    shards = [as3d(w[nm]).astype(BF16) for nm in BIG]
    small_in = [w[nm] for nm in SMALL_SHARD]
    small_pack = _pack(small_in)
    conv_b = w["ffn_conv_b"].reshape(n_layers, 1, -1)
    tabs = _rope_tables(seq)
    qg2 = jnp.tile(w["attn_q_gain"], (1, 2))
    kg2 = jnp.tile(w["attn_k_gain"], (1, 2))
    mm = functools.partial(_mm)

    saved = {}
    x0 = x
    h0, wq = _rms_fwd(x0, w["attn_norm"], BF16, "rms_attn", _Gather(shards[:1], BIG_AXIS[:1]), shards[:1])
    qkv = mm(h0, wq, "nn", b_l=0, out_dtype=F32, name="mm_qkv")
    q_r, k_r, k_t, v_b, v_t, small_all = _qk_prep(qkv, qg2, kg2, tabs, _GatherAll(small_pack), [small_pack], "qk_prep")
    per_chip = [_unpack(small_all[2 * j].reshape(-1), [a.shape for a in small_in]) for j in range(4)]
    pool_norm, pool_scale, conv_w = (jnp.concatenate([per_chip[j][i] for j in range(4)], axis=-1) for i in range(3))
    first = [slice(None)] * 5 + [slice(0, 1)] * 2
    o_at, lse, wo, wp, wxq, wxkv, wxo, wup, wdn = _flash_fwd(
        q_r, k_r, v_t, _Gather(shards[1:], BIG_AXIS[1:], first), shards[1:], "flash_fwd")
    ffn_w = {"up": wup, "down": wdn}
    x1, hq0 = mm(o_at, wo, "nn", b_l=0, res=x0, out_dtype=F32, norm_out=(w["xattn_norm"][0:1], BF16), name="mm_attn_o")

    def xattn_fwd(l, xin, hq):
        mn = _rms_fwd(mem, w["mem_norm"][l:l + 1], BF16, f"rms_mem{l}")
        xq = mm(hq, wxq, "nn", b_l=l, scale=X_HEAD_DIM ** -0.5, out_dtype=BF16, name=f"mm_xq{l}")
        kv = mm(mn, wxkv, "nn", b_l=l, out_dtype=BF16, name=f"mm_xkv{l}")
        xo = _xattn_fwd(xq, kv, f"xattn_fwd{l}")
        saved[f"x{l}"] = (hq, mn, xq, kv, xo)
        return mm(xo, wxo, "nn", b_l=l, res=xin, out_dtype=F32, norm_out=(w["ffn_norm"][l:l + 1], BF16), name=f"mm_xo{l}")

    def ffn_fwd(l, xin, hf, norm_out):
        ug = mm(hf, ffn_w["up"], "nn", b_l=l, n=dff, out_dtype=BF16, name=f"mm_up_g{l}")
        uv = mm(hf, ffn_w["up"], "nn", b_l=l, n=dff, b_off=(0, dff), out_dtype=BF16, name=f"mm_up_v{l}")
        if l == 0:
            rest = _Gather(shards[6:7], BIG_AXIS[6:7], [slice(1, 2)])
            act, ffn_w["up"] = _conv_gate_fwd(ug, uv, conv_w, conv_b, l, f"conv_gate{l}", rest, shards[6:7], [ffn_w["up"]])
        else:
            act = _conv_gate_fwd(ug, uv, conv_w, conv_b, l, f"conv_gate{l}")
        saved[f"f{l}"] = (hf, ug, uv, act)
        return mm(act, ffn_w["down"], "nn", b_l=l, res=xin, out_dtype=F32, norm_out=norm_out, name=f"mm_down{l}")

    x2, hf0 = xattn_fwd(0, x1, hq0)
    x3, hp = ffn_fwd(0, x2, hf0, (pool_norm, F32))
    x4, ffn_w["down"] = _pool_fwd(hp, x3, wp, pool_scale, "pool_fwd", _Gather(shards[7:], BIG_AXIS[7:], [slice(1, 2)]),
                                  shards[7:], [ffn_w["down"]])
    x5, hf1 = xattn_fwd(1, x4, _rms_fwd(x4, w["xattn_norm"][1:2], BF16, "rms_xq1"))
    xs = [x0, x1, x2, x3, x4, x5, ffn_fwd(1, x5, hf1, None)]
    dres, g_final, loss = _final_loss(xs[6], w["final_norm"].reshape(1, d), tgt, "final_loss")

    grads = {}
    gbuf = {}

    def dw(nm, a, b, layer, full, off=(0, 0), n=None, tn=None):
        gbuf[nm] = _mm(a, b, "tn", out_dtype=BF16, out_full=full, out_l=layer, out_off=off, n=n, tn=tn,
                       alias=gbuf.get(nm), name=f"dw_{nm}{layer}_{off[1]}")

    def ffn_bwd(l, xin, dres):
        hf, ug, uv, act = saved[f"f{l}"]
        wup, wdn = ffn_w["up"], ffn_w["down"]
        dw("ffn_w_down", act, dres, l, wdn.shape)
        dact = _mm(dres, wdn, "nt", b_l=l, out_dtype=BF16, name=f"mm_dact{l}")
        dug, duv, dwg, dwv = _conv_gate_bwd(ug, uv, dact, conv_w, conv_b, l, f"conv_gate_bwd{l}")
        dw("ffn_w_up", hf, dug, l, wup.shape, tn=1408)
        dw("ffn_w_up", hf, duv, l, wup.shape, off=(0, dff), tn=1408)
        dhf = _mm(dug, wup, "nt", b_l=l, n=d, out_dtype=F32, name=f"mm_dhf_g{l}")
        dres, dg = _mm(duv, wup, "nt", b_l=l, n=d, b_off=(0, dff), res=dhf, out_dtype=F32, tm=256,
                       norm_bwd=(xin, w["ffn_norm"][l:l + 1], dres), name=f"mm_dhf_v{l}")
        return dres, dg, jnp.concatenate([dwg[:3], dwv[:3]], axis=1), jnp.concatenate([dwg[3], dwv[3]], axis=0)

    def xattn_bwd(l, xin, dres):
        hq, mn, xq, kv, xo = saved[f"x{l}"]
        dw("xattn_w_o", xo, dres, l, wxo.shape)
        dxo = _mm(dres, wxo, "nt", b_l=l, out_dtype=BF16, name=f"mm_dxo{l}")
        dq, dkv = _xattn_bwd(xq, kv, dxo, f"xattn_bwd{l}")
        dw("xattn_w_q", hq, dq, l, wxq.shape)
        dw("xattn_w_kv", mn, dkv, l, wxkv.shape)
        dmn = _mm(dkv, wxkv, "nt", b_l=l, out_dtype=F32, name=f"mm_dmn{l}")
        _, dg_mem = _rms_bwd(mem, w["mem_norm"][l:l + 1], dmn, None, f"rms_mem_bwd{l}")
        dres, dg = _mm(dq, wxq, "nt", b_l=l, out_dtype=F32, norm_bwd=(xin, w["xattn_norm"][l:l + 1], dres),
                       name=f"mm_dhq{l}")
        return dres, dg, dg_mem

    g_ffn, g_xn, g_mn, g_cw, g_cb = [None] * n_layers, [None] * n_layers, [None] * n_layers, [None] * n_layers, [None] * n_layers
    dres, g_ffn[1], g_cw[1], g_cb[1] = ffn_bwd(1, xs[5], dres)
    dres, g_xn[1], g_mn[1] = xattn_bwd(1, xs[4], dres)
    dhp, g_pw, g_pscale = _pool_bwd(hp, dres, wp, pool_scale, "pool_bwd")
    dres, g_pnorm = _rms_bwd(xs[3], pool_norm, dhp, dres, "rms_pool_bwd")
    dres, g_ffn[0], g_cw[0], g_cb[0] = ffn_bwd(0, xs[2], dres)
    dres, g_xn[0], g_mn[0] = xattn_bwd(0, xs[1], dres)
    dw("attn_w_o", o_at, dres, 0, wo.shape)
    do = _mm(dres, wo, "nt", b_l=0, out_dtype=BF16, name="mm_do")
    gbuf["pool_w"] = g_pw.astype(BF16)
    early = [gbuf[nm] for nm in BIG[1:]]
    dq_r, dk_r, dv, *recv_early = _flash_bwd(q_r, k_r, k_t, v_b, do, o_at, lse, _Scatter(early, BIG_AXIS[1:]), early,
                                             "flash_bwd")
    def sum4(nm, rc):
        return _sum4(rc.reshape(4, -1, rc.shape[-1]), f"sum4_{nm}")
    sums_early = [sum4(nm, rc) for nm, rc in zip(BIG[1:], recv_early)]
    dqkv, dqg, dkg, *others_early = _qk_prep_bwd(qkv, dq_r, dk_r, dv, qg2, kg2, tabs, _Swap(sums_early), sums_early,
                                                 "qk_prep_bwd")
    dw("attn_w_qkv", h0, dqkv, 0, wq.shape)
    grad_x, g_an = _mm(dqkv, wq, "nt", b_l=0, out_dtype=F32, norm_bwd=(x0, w["attn_norm"], dres), name="mm_dh0")

    small_g = {
        "attn_norm": g_an, "attn_q_gain": dqg[:, :HEAD_DIM] + dqg[:, HEAD_DIM:], "attn_k_gain": dkg[:, :HEAD_DIM] + dkg[:, HEAD_DIM:],
        "xattn_norm": jnp.concatenate(g_xn, axis=0), "mem_norm": jnp.concatenate(g_mn, axis=0),
        "ffn_norm": jnp.concatenate(g_ffn, axis=0), "ffn_conv_b": jnp.stack(g_cb, axis=0), "final_norm": g_final.reshape(d),
        "pool_norm": g_pnorm, "pool_scale": g_pscale, "ffn_conv_w": jnp.stack(g_cw, axis=0)}
    names = SMALL_REPL + SMALL_SHARD
    small_pack = _pack([loss[0, :1]] + [small_g[nm] for nm in names])
    late = [gbuf[nm] for nm in BIG[:1]]
    small_all, recv_late = _comm_call([_GatherAll(small_pack), _Scatter(late, BIG_AXIS[:1])], [[small_pack], late],
                                      "reduce_small_scatter_qkv")
    total = _sum8(small_all, "sum_small")
    parts = _unpack(total.reshape(-1), [(1,)] + [small_g[nm].shape for nm in names])
    loss_out = parts[0][0]
    for nm, g in zip(names, parts[1:]):
        if nm in SMALL_SHARD:
            size = w[nm].shape[-1]
            g = lax.dynamic_slice_in_dim(g, chip * size, size, axis=g.ndim - 1)
        grads[nm] = g.reshape(w[nm].shape)

    packed = [_pack([src[nm] for nm in names]) for src in (w, grads, m, v)]
    _, sd, sm, sv = _adamw(packed[0], packed[1], None, packed[2], packed[3], "adamw_small")
    shapes = [w[nm].shape for nm in names]
    delta = dict(zip(names, _unpack(sd.reshape(-1), shapes)))
    new_m = dict(zip(names, _unpack(sm.reshape(-1), shapes)))
    new_v = dict(zip(names, _unpack(sv.reshape(-1), shapes)))

    sums_late = [sum4(BIG[0], recv_late)]
    others_late = _comm_call([_Swap(sums_late)], [sums_late], "swap_qkv")
    for nm, mine, other in zip(BIG, sums_late + sums_early, list(others_late) + others_early):
        cols = mine.shape[-1]
        outs = _adamw(w[nm].reshape(-1, cols), mine, other, m[nm].reshape(-1, cols), v[nm].reshape(-1, cols), f"adamw_{nm}")
        grads[nm], delta[nm], new_m[nm], new_v[nm] = (o.reshape(w[nm].shape) for o in outs)

    return loss_out, grad_x, grads, delta, new_m, new_v


def kernel(x, mem, attn_norm, attn_w_qkv, attn_q_gain, attn_k_gain, attn_w_o, pool_norm, pool_w, pool_scale, xattn_norm, mem_norm, xattn_w_q, xattn_w_kv, xattn_w_o, ffn_norm, ffn_w_up, ffn_conv_w, ffn_conv_b, ffn_w_down, final_norm, loss_target, m_attn_norm, m_attn_w_qkv, m_attn_q_gain, m_attn_k_gain, m_attn_w_o, m_pool_norm, m_pool_w, m_pool_scale, m_xattn_norm, m_mem_norm, m_xattn_w_q, m_xattn_w_kv, m_xattn_w_o, m_ffn_norm, m_ffn_w_up, m_ffn_conv_w, m_ffn_conv_b, m_ffn_w_down, m_final_norm, v_attn_norm, v_attn_w_qkv, v_attn_q_gain, v_attn_k_gain, v_attn_w_o, v_pool_norm, v_pool_w, v_pool_scale, v_xattn_norm, v_mem_norm, v_xattn_w_q, v_xattn_w_kv, v_xattn_w_o, v_ffn_norm, v_ffn_w_up, v_ffn_conv_w, v_ffn_conv_b, v_ffn_w_down, v_final_norm):
    given = dict(locals())
    w = {nm: given[nm] for nm in ORDER}
    m = {nm: given["m_" + nm] for nm in ORDER}
    v = {nm: given["v_" + nm] for nm in ORDER}
    seq, d = x.shape[1], x.shape[2]
    loss, grad_x, grads, delta, new_m, new_v = _step(
        x.reshape(seq, d), mem.reshape(mem.shape[1], d), loss_target.reshape(seq, d), w, m, v)
    return (loss, grad_x.reshape(x.shape), *[grads[nm] for nm in ORDER], *[delta[nm] for nm in ORDER],
            *[new_m[nm] for nm in ORDER], *[new_v[nm] for nm in ORDER])
```

```python
import functools
import itertools

import jax
import jax.numpy as jnp
from jax import lax
from jax.experimental import pallas as pl
from jax.experimental.pallas import tpu as pltpu

F32, BF16 = jnp.float32, jnp.bfloat16
EPS = 1e-6
GRID_W = 64
ROPE_THETA = 10000.0
HEAD_DIM = 64
N_HEADS = 16
N_KV = 4
X_HEADS = 4
X_HEAD_DIM = 256
POOL_GROUPS = 4
POOL_GROUP_W = 256
HALO = 16
HALO_TR = 2048
LANES = 128
ADAM_LR, ADAM_B1, ADAM_B2, ADAM_EPS, ADAM_WD, ADAM_STEP = 0.001, 0.9, 0.999, 1e-08, 0.01, 10
VMEM_LIMIT = 48 * 1024 * 1024
MESH = pl.DeviceIdType.MESH
NEG = -1e30
LOG2E = 1.4426950408889634
FLASH_TQ, FLASH_TK = 512, 4096
FLASH_SUB = 512
ANY = pl.BlockSpec(memory_space=pl.ANY)


def _cp(sem=None):
    return pltpu.CompilerParams(dimension_semantics=sem, vmem_limit_bytes=VMEM_LIMIT)


def _pick(n, cands):
    for c in cands:
        if c <= n and n % c == 0:
            return c
    return n


def _mm(a, b, mode, *, name, out_dtype, tm=None, tn=None, tk=None, n=None, k=None, b_l=None, b_off=(0, 0),
        res=None, scale=None, out_full=None, out_l=None, out_off=(0, 0), alias=None, norm_out=None, norm_bwd=None):
    if mode == "tn":
        K, M = a.shape
    else:
        M, K = a.shape
    bs = b.shape[-2:]
    if mode == "nn":
        K = k or K
        N = n or bs[1]
    elif mode == "nt":
        N = n or bs[0]
    else:
        N = n or bs[1]
    wide = (1408, 1024, 512, 256, 128)
    if mode == "tn":
        tm = tm or (M if M <= 1024 else _pick(M, wide))
        tk = tk or _pick(K, (2048, 1024, 512, 256, 128))
    else:
        tm = _pick(M, (tm or 512, 256, 128))
        tk = tk or (K if K <= 2816 else _pick(K, wide))
    tn = tn or (N if N <= 1536 else _pick(N, wide))
    assert M % tm == 0 and N % tn == 0 and K % tk == 0, (name, M, N, K, tm, tn, tk)
    nk = K // tk
    dims = {"nn": ((1,), (0,)), "nt": ((1,), (1,)), "tn": ((0,), (0,))}[mode]

    j_outer = nk == 1 and mode != "tn"

    def at(f):
        return (lambda j, i, kk: f(i, j, kk)) if j_outer else f

    if mode == "tn":
        a_spec = pl.BlockSpec((tk, tm), at(lambda i, j, kk: (kk, i)))
    else:
        a_spec = pl.BlockSpec((tm, tk), at(lambda i, j, kk: (i, kk)))
    if mode == "nt":
        bb, (d0, d1) = (tn, tk), (b_off[0] // tn, b_off[1] // tk)
        assert b_off[0] % tn == 0 and b_off[1] % tk == 0
        bidx = lambda i, j, kk: (j + d0, kk + d1)
    else:
        bb, (d0, d1) = (tk, tn), (b_off[0] // tk, b_off[1] // tn)
        assert b_off[0] % tk == 0 and b_off[1] % tn == 0
        bidx = lambda i, j, kk: (kk + d0, j + d1)
    if b.ndim == 3:
        b_spec = pl.BlockSpec((None,) + bb, at(lambda i, j, kk: (b_l,) + bidx(i, j, kk)))
    else:
        b_spec = pl.BlockSpec(bb, at(bidx))
    in_specs, operands = [a_spec, b_spec], [a, b]
    if res is not None:
        in_specs.append(pl.BlockSpec((tm, tn), at(lambda i, j, kk: (i, j))))
        operands.append(res)
    aliases = {}
    if alias is not None:
        aliases = {len(operands): 0}
        in_specs.append(ANY)
        operands.append(alias)
    if out_full is None:
        out_shape = jax.ShapeDtypeStruct((M, N), out_dtype)
        out_spec = pl.BlockSpec((tm, tn), at(lambda i, j, kk: (i, j)))
    else:
        assert out_off[0] % tm == 0 and out_off[1] % tn == 0
        o0, o1 = out_off[0] // tm, out_off[1] // tn
        out_shape = jax.ShapeDtypeStruct(out_full, out_dtype)
        out_spec = pl.BlockSpec((None, tm, tn), at(lambda i, j, kk: (out_l, i + o0, j + o1)))
    has_res, has_alias = res is not None, alias is not None
    grid = (N // tn, M // tm, nk) if j_outer else (M // tm, N // tn, nk)
    n_extra = 0
    if norm_out is not None or norm_bwd is not None:
        assert j_outer and tn == N and out_full is None, name
        row = pl.BlockSpec((tm, tn), at(lambda i, j, kk: (i, 0)))
        vec = pl.BlockSpec((1, tn), at(lambda i, j, kk: (0, 0)))
        if norm_out is not None:
            in_specs.append(vec)
            operands.append(norm_out[0])
            n_extra = 1
            out_shape = (out_shape, jax.ShapeDtypeStruct((M, N), norm_out[1]))
            out_spec = (out_spec, row)
        else:
            in_specs += [row, vec, row]
            operands += list(norm_bwd)
            n_extra = 3
            out_shape = (out_shape, jax.ShapeDtypeStruct((1, N), F32))
            out_spec = (out_spec, vec)
    n_out = 1 if n_extra == 0 else 2

    def body(*refs):
        a_ref, b_ref = refs[0], refs[1]
        pos = 2
        res_ref = None
        if has_res:
            res_ref = refs[pos]
            pos += 1
        if has_alias:
            pos += 1
        extra = refs[pos:pos + n_extra]
        pos += n_extra
        o_ref, acc_ref = refs[pos], refs[pos + n_out]
        kk = pl.program_id(2)
        part = lax.dot_general(a_ref[...].astype(BF16), b_ref[...].astype(BF16), (dims, ((), ())),
                               preferred_element_type=F32)

        def finish(acc):
            if scale is not None:
                acc = acc * scale
            if res_ref is not None:
                acc = acc + res_ref[...]
            if norm_out is not None:
                r = lax.rsqrt(jnp.mean(acc * acc, axis=-1, keepdims=True) + EPS)
                refs[pos + 1][...] = (acc * r * extra[0][...]).astype(refs[pos + 1].dtype)
            if norm_bwd is not None:
                x_ref, g_ref, dres_ref = extra
                dg_ref, step = refs[pos + 1], pl.program_id(1)
                xv = x_ref[...]
                r = lax.rsqrt(jnp.mean(xv * xv, axis=-1, keepdims=True) + EPS)
                nv = xv * r
                dgp = jnp.sum(acc * nv, axis=0, keepdims=True)

                @pl.when(step == 0)
                def _():
                    dg_ref[...] = dgp

                @pl.when(step > 0)
                def _():
                    dg_ref[...] += dgp

                dn = acc * g_ref[...]
                acc = dres_ref[...] + r * (dn - nv * jnp.mean(dn * nv, axis=-1, keepdims=True))
            o_ref[...] = acc.astype(o_ref.dtype)

        if nk == 1:
            finish(part)
        else:
            @pl.when(kk == 0)
            def _():
                acc_ref[...] = part

            @pl.when(jnp.logical_and(kk > 0, kk < nk - 1))
            def _():
                acc_ref[...] += part

            @pl.when(kk == nk - 1)
            def _():
                finish(acc_ref[...] + part)

    return pl.pallas_call(
        body, out_shape=out_shape, grid=grid, in_specs=in_specs, out_specs=out_spec,
        scratch_shapes=[pltpu.VMEM((tm, tn) if nk > 1 else (8, 128), F32)], input_output_aliases=aliases,
        compiler_params=_cp(("arbitrary",) * 3 if norm_bwd is not None else ("parallel", "parallel", "arbitrary")),
        name=name)(*operands)


def _hosted(plan, nsteps, step, compute):
    if plan is None:
        return compute()

    @pl.when(step == 0)
    def _():
        plan.start()

    compute()

    @pl.when(step == nsteps - 1)
    def _():
        if hasattr(plan, "forward"):
            plan.forward()
        plan.finish()


def _rms_fwd(x, gain, out_dtype, name, plan=None, sends=()):
    rows, d = x.shape
    tr = _pick(rows, (512, 256))
    nt, nsteps = (plan.nt if plan is not None else 0), rows // tr

    def body(x_ref, g_ref, *rest):
        o_ref = rest[nt]
        if plan is not None:
            plan.bind(rest[:nt], rest[nt + 1:2 * nt + 1], *rest[2 * nt + 1:])

        def compute():
            xv = x_ref[...]
            r = lax.rsqrt(jnp.mean(xv * xv, axis=-1, keepdims=True) + EPS)
            o_ref[...] = (xv * r * g_ref[...]).astype(o_ref.dtype)

        _hosted(plan, nsteps, pl.program_id(0), compute)

    out = jax.ShapeDtypeStruct((rows, d), out_dtype)
    row = pl.BlockSpec((tr, d), lambda i: (i, 0))
    in_specs = [row, pl.BlockSpec((1, d), lambda i: (0, 0))]
    if plan is None:
        return pl.pallas_call(body, out_shape=out, grid=(nsteps,), in_specs=in_specs, out_specs=row,
                              compiler_params=_cp(("parallel",)), name=name)(x, gain)
    return pl.pallas_call(
        body, out_shape=(out, *plan.out_shape), grid=(nsteps,), in_specs=in_specs + [ANY] * nt,
        out_specs=(row, *([ANY] * nt)), scratch_shapes=plan.scratch, compiler_params=_cp(("arbitrary",)),
        name=name)(x, gain, *sends)


def _rms_bwd(x, gain, dh, dres, name):
    rows, d = x.shape
    tr = _pick(rows, (512, 256))
    need_dx = dres is not None

    def body(*refs):
        if need_dx:
            x_ref, g_ref, dh_ref, dres_ref, o_ref, dg_ref = refs
        else:
            x_ref, g_ref, dh_ref, dg_ref = refs
        i = pl.program_id(0)
        xv = x_ref[...]
        dhv = dh_ref[...].astype(F32)
        r = lax.rsqrt(jnp.mean(xv * xv, axis=-1, keepdims=True) + EPS)
        nv = xv * r
        part = jnp.sum(dhv * nv, axis=0, keepdims=True)

        @pl.when(i == 0)
        def _():
            dg_ref[...] = part

        @pl.when(i > 0)
        def _():
            dg_ref[...] += part

        if need_dx:
            dn = dhv * g_ref[...]
            dx = r * (dn - nv * jnp.mean(dn * nv, axis=-1, keepdims=True))
            o_ref[...] = dres_ref[...] + dx

    row_spec = pl.BlockSpec((tr, d), lambda i: (i, 0))
    vec_spec = pl.BlockSpec((1, d), lambda i: (0, 0))
    if need_dx:
        return pl.pallas_call(
            body, out_shape=(jax.ShapeDtypeStruct((rows, d), F32), jax.ShapeDtypeStruct((1, d), F32)),
            grid=(rows // tr,), in_specs=[row_spec, vec_spec, row_spec, row_spec], out_specs=(row_spec, vec_spec),
            compiler_params=_cp(("arbitrary",)), name=name)(x, gain, dh, dres)
    return None, pl.pallas_call(
        body, out_shape=jax.ShapeDtypeStruct((1, d), F32), grid=(rows // tr,),
        in_specs=[row_spec, vec_spec, row_spec], out_specs=vec_spec,
        compiler_params=_cp(("arbitrary",)), name=name)(x, gain, dh)


def _final_loss(x, gain, target, name):
    rows, d = x.shape
    tr = _pick(rows, (512, 256))
    nsteps = rows // tr

    def body(x_ref, g_ref, t_ref, dx_ref, dg_ref, loss_ref, acc_ref):
        i = pl.program_id(0)
        xv = x_ref[...]
        g = g_ref[...]
        r = lax.rsqrt(jnp.mean(xv * xv, axis=-1, keepdims=True) + EPS)
        nv = xv * r
        err = nv * g - t_ref[...]
        dy = err * (1.0 / d)
        dn = dy * g
        dx_ref[...] = r * (dn - nv * jnp.mean(dn * nv, axis=-1, keepdims=True))
        dgp = jnp.sum(dy * nv, axis=0, keepdims=True)
        lp = jnp.sum(err * err, axis=0, keepdims=True)

        @pl.when(i == 0)
        def _():
            dg_ref[...] = dgp
            acc_ref[...] = lp

        @pl.when(i > 0)
        def _():
            dg_ref[...] += dgp
            acc_ref[...] += lp

        @pl.when(i == nsteps - 1)
        def _():
            tot = jnp.sum(acc_ref[...], axis=1, keepdims=True) * (0.5 / d)
            loss_ref[...] = jnp.broadcast_to(tot, loss_ref.shape)

    row_spec = pl.BlockSpec((tr, d), lambda i: (i, 0))
    vec_spec = pl.BlockSpec((1, d), lambda i: (0, 0))
    return pl.pallas_call(
        body, out_shape=(jax.ShapeDtypeStruct((rows, d), F32), jax.ShapeDtypeStruct((1, d), F32),
                         jax.ShapeDtypeStruct((1, LANES), F32)),
        grid=(nsteps,), in_specs=[row_spec, vec_spec, row_spec],
        out_specs=(row_spec, vec_spec, pl.BlockSpec((1, LANES), lambda i: (0, 0))),
        scratch_shapes=[pltpu.VMEM((1, d), F32)], compiler_params=_cp(("arbitrary",)), name=name)(x, gain, target)


def _rope_tables(seq):
    pairs = HEAD_DIM // 4
    lane = jnp.arange(LANES, dtype=jnp.int32) % HEAD_DIM
    by_col, second, pair = lane // (2 * pairs) == 1, (lane % (2 * pairs)) // pairs == 1, lane % pairs
    inv_freq = ROPE_THETA ** (-pair.astype(F32) / pairs)
    t = jnp.arange(seq, dtype=jnp.int32)[:, None]
    pos = jnp.where(by_col[None, :], t % GRID_W, t // GRID_W).astype(F32)
    ang = pos * inv_freq[None, :]
    cos, sin = jnp.cos(ang), jnp.sin(ang)
    return cos, jnp.where(second[None, :], sin, 0.0), jnp.where(second[None, :], 0.0, -sin)


def _pair_norm(xv, lo):
    sq = xv * xv
    s_lo = jnp.sum(jnp.where(lo, sq, 0.0), axis=1, keepdims=True)
    s_hi = jnp.sum(jnp.where(lo, 0.0, sq), axis=1, keepdims=True)
    return lax.rsqrt(jnp.where(lo, s_lo, s_hi) * (1.0 / HEAD_DIM) + EPS)


def _rope(y, c, sp, sm):
    return y * c + pltpu.roll(y, 16, axis=1) * sp + pltpu.roll(y, LANES - 16, axis=1) * sm


def _rope_t(dz, c, sp, sm):
    return dz * c + pltpu.roll(dz * sp, LANES - 16, axis=1) + pltpu.roll(dz * sm, 16, axis=1)


def _qk_prep(qkv, qg2, kg2, tabs, plan, sends, name):
    seq = qkv.shape[0]
    ts = _pick(seq, (256, 128))
    nq, nkp = N_HEADS // 2, N_KV // 2
    qw, kw = N_HEADS * HEAD_DIM, N_KV * HEAD_DIM
    nt, nsteps = plan.nt, seq // ts

    def body(x_ref, qg_ref, kg_ref, c_ref, sp_ref, sm_ref, *rest):
        q_ref, k_ref, kt_ref, v_ref, vt_ref = rest[nt:nt + 5]
        plan.bind(rest[:nt], rest[nt + 5:2 * nt + 5], *rest[2 * nt + 5:])
        _hosted(plan, nsteps, pl.program_id(0), lambda: compute(x_ref, qg_ref, kg_ref, c_ref, sp_ref, sm_ref,
                                                                q_ref, k_ref, kt_ref, v_ref, vt_ref))

    def compute(x_ref, qg_ref, kg_ref, c_ref, sp_ref, sm_ref, q_ref, k_ref, kt_ref, v_ref, vt_ref):
        lo = lax.broadcasted_iota(jnp.int32, (ts, LANES), 1) < HEAD_DIM
        top = lax.broadcasted_iota(jnp.int32, (LANES, ts), 0) < HEAD_DIM
        c, sp, sm = c_ref[...], sp_ref[...], sm_ref[...]
        for i in range(nq):
            xv = x_ref[:, i * LANES:(i + 1) * LANES]
            y = xv * _pair_norm(xv, lo) * qg_ref[...]
            q_ref[:, i * LANES:(i + 1) * LANES] = (_rope(y, c, sp, sm) * (LOG2E * HEAD_DIM ** -0.5)).astype(BF16)
        for i in range(nkp):
            xv = x_ref[:, qw + i * LANES:qw + (i + 1) * LANES]
            z = _rope(xv * _pair_norm(xv, lo) * kg_ref[...], c, sp, sm)
            k_ref[:, i * LANES:(i + 1) * LANES] = z.astype(BF16)
            kt_ref[i * LANES:(i + 1) * LANES, :] = z.T.astype(BF16)
            vv = x_ref[:, qw + kw + i * LANES:qw + kw + (i + 1) * LANES]
            v_ref[:, i * LANES:(i + 1) * LANES] = vv.astype(BF16)
            vvt = vv.T
            vt_ref[(2 * i) * LANES:(2 * i + 1) * LANES, :] = jnp.where(top, vvt, 1.0).astype(BF16)
            vt_ref[(2 * i + 1) * LANES:(2 * i + 2) * LANES, :] = jnp.where(top, 1.0, vvt).astype(BF16)

    tab = pl.BlockSpec((ts, LANES), lambda i: (i, 0))
    vec = pl.BlockSpec((1, LANES), lambda i: (0, 0))
    return pl.pallas_call(
        body,
        out_shape=(jax.ShapeDtypeStruct((seq, qw), BF16), jax.ShapeDtypeStruct((seq, kw), BF16),
                   jax.ShapeDtypeStruct((kw, seq), BF16), jax.ShapeDtypeStruct((seq, kw), BF16),
                   jax.ShapeDtypeStruct((N_KV * LANES, seq), BF16), *plan.out_shape),
        grid=(nsteps,),
        in_specs=[pl.BlockSpec((ts, qw + 2 * kw), lambda i: (i, 0)), vec, vec, tab, tab, tab] + [ANY] * nt,
        out_specs=(pl.BlockSpec((ts, qw), lambda i: (i, 0)), pl.BlockSpec((ts, kw), lambda i: (i, 0)),
                   pl.BlockSpec((kw, ts), lambda i: (0, i)), pl.BlockSpec((ts, kw), lambda i: (i, 0)),
                   pl.BlockSpec((N_KV * LANES, ts), lambda i: (0, i)), *([ANY] * nt)),
        scratch_shapes=plan.scratch, compiler_params=_cp(("arbitrary",)), name=name)(qkv, qg2, kg2, *tabs, *sends)


def _qk_prep_bwd(qkv, dq, dk, dv, qg2, kg2, tabs, plan, sends, name):
    seq = qkv.shape[0]
    ts = _pick(seq, (256, 128))
    nq, nkp = N_HEADS // 2, N_KV // 2
    qw, kw = N_HEADS * HEAD_DIM, N_KV * HEAD_DIM
    nt, nsteps = plan.nt, seq // ts

    def body(x_ref, dq_ref, dk_ref, dv_ref, qg_ref, kg_ref, c_ref, sp_ref, sm_ref, *rest):
        o_ref, dqg_ref, dkg_ref = rest[nt:nt + 3]
        plan.bind(rest[:nt], rest[nt + 3:2 * nt + 3], *rest[2 * nt + 3:])
        step = pl.program_id(0)

        @pl.when(step == 0)
        def _():
            plan.start()

        lo = lax.broadcasted_iota(jnp.int32, (ts, LANES), 1) < HEAD_DIM
        c, sp, sm = c_ref[...], sp_ref[...], sm_ref[...]

        def one(xv, dz, gain):
            r = _pair_norm(xv, lo)
            nv = xv * r
            dy = _rope_t(dz, c, sp, sm)
            dgp = jnp.sum(dy * nv, axis=0, keepdims=True)
            dn = dy * gain
            t = dn * nv
            m_lo = jnp.sum(jnp.where(lo, t, 0.0), axis=1, keepdims=True)
            m_hi = jnp.sum(jnp.where(lo, 0.0, t), axis=1, keepdims=True)
            m = jnp.where(lo, m_lo, m_hi) * (1.0 / HEAD_DIM)
            return r * (dn - nv * m), dgp

        dqg = jnp.zeros((1, LANES), F32)
        for i in range(nq):
            sl = slice(i * LANES, (i + 1) * LANES)
            dx, dgp = one(x_ref[:, sl], dq_ref[:, sl] * (HEAD_DIM ** -0.5), qg_ref[...])
            o_ref[:, sl] = dx.astype(BF16)
            dqg = dqg + dgp
        dkg = jnp.zeros((1, LANES), F32)
        for i in range(nkp):
            sl = slice(i * LANES, (i + 1) * LANES)
            dx, dgp = one(x_ref[:, qw + i * LANES:qw + (i + 1) * LANES], dk_ref[:, sl], kg_ref[...])
            o_ref[:, qw + i * LANES:qw + (i + 1) * LANES] = dx.astype(BF16)
            dkg = dkg + dgp
            o_ref[:, qw + kw + i * LANES:qw + kw + (i + 1) * LANES] = dv_ref[:, sl].astype(BF16)

        @pl.when(step == 0)
        def _():
            dqg_ref[...] = dqg
            dkg_ref[...] = dkg

        @pl.when(step > 0)
        def _():
            dqg_ref[...] += dqg
            dkg_ref[...] += dkg

        @pl.when(step == nsteps - 1)
        def _():
            plan.finish()

    tab = pl.BlockSpec((ts, LANES), lambda i: (i, 0))
    vec = pl.BlockSpec((1, LANES), lambda i: (0, 0))
    return pl.pallas_call(
        body,
        out_shape=(jax.ShapeDtypeStruct((seq, qw + 2 * kw), BF16), jax.ShapeDtypeStruct((1, LANES), F32),
                   jax.ShapeDtypeStruct((1, LANES), F32), *plan.out_shape),
        grid=(nsteps,),
        in_specs=[pl.BlockSpec((ts, qw + 2 * kw), lambda i: (i, 0)), pl.BlockSpec((ts, qw), lambda i: (i, 0)),
                  pl.BlockSpec((ts, kw), lambda i: (i, 0)), pl.BlockSpec((ts, kw), lambda i: (i, 0)),
                  vec, vec, tab, tab, tab] + [ANY] * nt,
        out_specs=(pl.BlockSpec((ts, qw + 2 * kw), lambda i: (i, 0)), vec, vec, *([ANY] * nt)),
        scratch_shapes=plan.scratch, compiler_params=_cp(("arbitrary",)), name=name)(qkv, dq, dk, dv, qg2, kg2, *tabs, *sends)


def _slot(blk, off0, tq):
    half = lax.broadcasted_iota(jnp.int32, (tq, LANES), 1) // HEAD_DIM
    keep = half == jnp.where(off0, 0, 1)
    parts = []
    for p in range(2):
        pair = blk[:, p * LANES:(p + 1) * LANES].astype(F32)
        rolled = pltpu.roll(pair, HEAD_DIM, axis=1)
        parts.append(jnp.where(keep, jnp.where(off0, pair, rolled), 0.0))
        parts.append(jnp.where(keep, jnp.where(off0, rolled, pair), 0.0))
    return jnp.concatenate(parts, axis=0)


def _unslot(x4, off0, tq):
    lo = lax.broadcasted_iota(jnp.int32, (tq, LANES), 1) < HEAD_DIM
    pairs = []
    for p in range(2):
        h0 = x4[(2 * p) * tq:(2 * p + 1) * tq]
        h1 = x4[(2 * p + 1) * tq:(2 * p + 2) * tq]
        a = jnp.where(off0, h0, pltpu.roll(h0, HEAD_DIM, axis=1))
        b = jnp.where(off0, pltpu.roll(h1, HEAD_DIM, axis=1), h1)
        pairs.append(jnp.where(lo, a, b))
    return jnp.concatenate(pairs, axis=1)


def _flash_fwd(q, k, vt, plan, shards, name):
    seq = q.shape[0]
    tq = _pick(seq, (FLASH_TQ, 128))
    tk = _pick(seq, (FLASH_TK, 2048, 512, 256, 128))
    sub = _pick(tk, (FLASH_SUB, 256, 128))
    nq, nkv, nsub = seq // tq, seq // tk, tk // sub
    gw = 4 * HEAD_DIM
    nt = plan.nt

    def body(q_ref, k_ref, vt_ref, *rest):
        o_ref, lse_ref = rest[nt:nt + 2]
        q4_ref, m_ref, acc_ref, st_ref = rest[2 * nt + 2:2 * nt + 6]
        plan.bind(rest[:nt], rest[nt + 2:2 * nt + 2], *rest[2 * nt + 6:])
        g, qi, ki = pl.program_id(0), pl.program_id(1), pl.program_id(2)
        off0 = (g % 2) == 0
        @pl.when(jnp.logical_and(g == 0, jnp.logical_and(qi == 0, ki == 0)))
        def _():
            plan.start()

        @pl.when(jnp.logical_and(g == N_KV - 1, jnp.logical_and(qi == nq - 1, ki == 0)))
        def _():
            plan.forward()

        @pl.when(ki == 0)
        def _():
            q4_ref[...] = _slot(q_ref[...], off0, tq).astype(BF16)
            m_ref[...] = jnp.full(m_ref.shape, NEG, F32)
            acc_ref[...] = jnp.zeros(acc_ref.shape, F32)

        q4 = q4_ref[...]

        def scores(c):
            st_ref[c % 2] = lax.dot_general(k_ref[c * sub:(c + 1) * sub, :], q4, (((1,), (1,)), ((), ())),
                                            preferred_element_type=F32)

        m, acc = m_ref[...], acc_ref[...]
        scores(0)
        for c in range(nsub):
            if c + 1 < nsub:
                scores(c + 1)
            st = st_ref[c % 2]
            m_new = jnp.maximum(m, jnp.max(st, axis=0, keepdims=True))
            pt = jnp.exp2(st - m_new).astype(BF16)
            acc = jnp.exp2(m - m_new) * acc + jnp.dot(vt_ref[:, c * sub:(c + 1) * sub], pt, preferred_element_type=F32)
            m = m_new
        m_ref[...] = m
        acc_ref[...] = acc

        @pl.when(ki == nkv - 1)
        def _():
            acc = acc_ref[...]
            l = jnp.where(off0, acc[HEAD_DIM:HEAD_DIM + 1], acc[0:1])
            o4 = acc.T
            o4 = o4 / pltpu.roll(o4, HEAD_DIM, axis=1)
            o_ref[...] = _unslot(o4, off0, tq).astype(o_ref.dtype)
            lse_ref[...] = jnp.broadcast_to(m_ref[...] + jnp.log2(l), lse_ref.shape)

        @pl.when(jnp.logical_and(g == N_KV - 1, jnp.logical_and(qi == nq - 1, ki == nkv - 1)))
        def _():
            plan.finish()

    return pl.pallas_call(
        body,
        out_shape=(jax.ShapeDtypeStruct((seq, N_HEADS * HEAD_DIM), BF16),
                   jax.ShapeDtypeStruct((N_KV * nq * 8, 4 * tq), F32), *plan.out_shape),
        grid=(N_KV, nq, nkv),
        in_specs=[pl.BlockSpec((tq, gw), lambda g, qi, ki: (qi, g)),
                  pl.BlockSpec((tk, LANES), lambda g, qi, ki: (ki, g // 2)),
                  pl.BlockSpec((LANES, tk), lambda g, qi, ki: (g, ki))] + [ANY] * nt,
        out_specs=(pl.BlockSpec((tq, gw), lambda g, qi, ki: (qi, g)),
                   pl.BlockSpec((8, 4 * tq), lambda g, qi, ki: (g * nq + qi, 0)), *([ANY] * nt)),
        scratch_shapes=[pltpu.VMEM((4 * tq, LANES), BF16), pltpu.VMEM((1, 4 * tq), F32),
                        pltpu.VMEM((LANES, 4 * tq), F32), pltpu.VMEM((2, sub, 4 * tq), F32)] + plan.scratch,
        compiler_params=_cp(("arbitrary", "arbitrary", "arbitrary")), name=name)(q, k, vt, *shards)


def _flash_bwd(q, k, kt, v, do, o, lse, plan, grads, name):
    seq = q.shape[0]
    tq = _pick(seq, (FLASH_TQ, 128))
    tk = _pick(seq, (FLASH_TK, 2048, 512, 256, 128))
    sub = _pick(tk, (FLASH_SUB, 256, 128))
    nq, nkv, nsub = seq // tq, seq // tk, tk // sub
    gw = 4 * HEAD_DIM
    nt = plan.nt

    def body(q_ref, k_ref, kt_ref, v_ref, do_ref, o_ref, lse_ref, *rest):
        dq_ref, dk_ref, dv_ref = rest[nt:nt + 3]
        q4_ref, do4_ref, delta_ref, dqt_ref, st_ref, dpt_ref = rest[2 * nt + 3:2 * nt + 9]
        plan.bind(rest[:nt], rest[nt + 3:2 * nt + 3], *rest[2 * nt + 9:])
        g, qi, ki = pl.program_id(0), pl.program_id(1), pl.program_id(2)
        off0 = (g % 2) == 0

        @pl.when(jnp.logical_and(g == 0, jnp.logical_and(qi == 0, ki == 0)))
        def _():
            plan.start()

        @pl.when(jnp.logical_and(g % 2 == 0, jnp.logical_and(qi == 0, ki == 0)))
        def _():
            dk_ref[...] = jnp.zeros(dk_ref.shape, F32)
            dv_ref[...] = jnp.zeros(dv_ref.shape, F32)

        @pl.when(ki == 0)
        def _():
            q4_ref[...] = _slot(q_ref[...], off0, tq).astype(BF16)
            do4 = _slot(do_ref[...], off0, tq)
            do4_ref[...] = do4.astype(BF16)
            o4 = _slot(o_ref[...], off0, tq)
            delta_ref[...] = jnp.sum((do4 * o4).T, axis=0, keepdims=True)
            dqt_ref[...] = jnp.zeros(dqt_ref.shape, F32)

        q4, do4 = q4_ref[...], do4_ref[...]
        lse_row, delta = lse_ref[0:1, :], delta_ref[...]

        def products(c):
            rows = slice(c * sub, (c + 1) * sub)
            st_ref[c % 2] = lax.dot_general(k_ref[rows, :], q4, (((1,), (1,)), ((), ())), preferred_element_type=F32)
            dpt_ref[c % 2] = lax.dot_general(v_ref[rows, :], do4, (((1,), (1,)), ((), ())), preferred_element_type=F32)

        dqt = dqt_ref[...]
        products(0)
        for c in range(nsub):
            if c + 1 < nsub:
                products(c + 1)
            pt = jnp.exp2(st_ref[c % 2] - lse_row)
            dst = (pt * (dpt_ref[c % 2] - delta)).astype(BF16)
            rows = pl.ds(pl.multiple_of(ki * tk + c * sub, sub), sub)
            dv_ref[rows, :] += jnp.dot(pt.astype(BF16), do4, preferred_element_type=F32)
            dk_ref[rows, :] += jnp.dot(dst, q4, preferred_element_type=F32) * (1.0 / LOG2E)
            dqt = dqt + jnp.dot(kt_ref[:, c * sub:(c + 1) * sub], dst, preferred_element_type=F32)
        dqt_ref[...] = dqt

        @pl.when(ki == nkv - 1)
        def _():
            dq_ref[...] = _unslot(dqt_ref[...].T, off0, tq)

        @pl.when(jnp.logical_and(g == N_KV - 1, jnp.logical_and(qi == nq - 1, ki == nkv - 1)))
        def _():
            plan.finish()

    return pl.pallas_call(
        body,
        out_shape=(jax.ShapeDtypeStruct((seq, N_HEADS * HEAD_DIM), F32),
                   jax.ShapeDtypeStruct((seq, N_KV * HEAD_DIM), F32), jax.ShapeDtypeStruct((seq, N_KV * HEAD_DIM), F32),
                   *plan.out_shape),
        grid=(N_KV, nq, nkv),
        in_specs=[pl.BlockSpec((tq, gw), lambda g, qi, ki: (qi, g)),
                  pl.BlockSpec((tk, LANES), lambda g, qi, ki: (ki, g // 2)),
                  pl.BlockSpec((LANES, tk), lambda g, qi, ki: (g // 2, ki)),
                  pl.BlockSpec((tk, LANES), lambda g, qi, ki: (ki, g // 2)),
                  pl.BlockSpec((tq, gw), lambda g, qi, ki: (qi, g)),
                  pl.BlockSpec((tq, gw), lambda g, qi, ki: (qi, g)),
                  pl.BlockSpec((8, 4 * tq), lambda g, qi, ki: (g * nq + qi, 0))] + [ANY] * nt,
        out_specs=(pl.BlockSpec((tq, gw), lambda g, qi, ki: (qi, g)),
                   pl.BlockSpec((seq, LANES), lambda g, qi, ki: (0, g // 2)),
                   pl.BlockSpec((seq, LANES), lambda g, qi, ki: (0, g // 2)), *([ANY] * nt)),
        scratch_shapes=[pltpu.VMEM((4 * tq, LANES), BF16), pltpu.VMEM((4 * tq, LANES), BF16),
                        pltpu.VMEM((1, 4 * tq), F32), pltpu.VMEM((LANES, 4 * tq), F32),
                        pltpu.VMEM((2, sub, 4 * tq), F32), pltpu.VMEM((2, sub, 4 * tq), F32)] + plan.scratch,
        compiler_params=_cp(("arbitrary", "arbitrary", "arbitrary")), name=name)(q, k, kt, v, do, o, lse, *grads)


def _xattn_fwd(q, kv, name):
    seq, d = q.shape
    mlen = kv.shape[0]
    tq = _pick(seq, (512, 256))

    def body(q_ref, k_ref, v_ref, o_ref):
        for h in range(X_HEADS):
            sl = slice(h * X_HEAD_DIM, (h + 1) * X_HEAD_DIM)
            s = lax.dot_general(q_ref[:, sl], k_ref[:, sl], (((1,), (1,)), ((), ())), preferred_element_type=F32)
            e = jnp.exp(s - jnp.max(s, axis=-1, keepdims=True))
            p = e / jnp.sum(e, axis=-1, keepdims=True)
            o_ref[:, sl] = jnp.dot(p.astype(BF16), v_ref[:, sl], preferred_element_type=F32).astype(o_ref.dtype)

    return pl.pallas_call(
        body, out_shape=jax.ShapeDtypeStruct((seq, d), BF16), grid=(seq // tq,),
        in_specs=[pl.BlockSpec((tq, d), lambda i: (i, 0)), pl.BlockSpec((mlen, d), lambda i: (0, 0)),
                  pl.BlockSpec((mlen, d), lambda i: (0, 1))],
        out_specs=pl.BlockSpec((tq, d), lambda i: (i, 0)), compiler_params=_cp(("parallel",)), name=name)(q, kv, kv)


def _xattn_bwd(q, kv, do, name):
    seq, d = q.shape
    mlen = kv.shape[0]
    tq = _pick(seq, (512, 256))
    scale = X_HEAD_DIM ** -0.5

    def body(q_ref, k_ref, v_ref, do_ref, dq_ref, dkv_ref):
        i = pl.program_id(0)

        @pl.when(i == 0)
        def _():
            dkv_ref[...] = jnp.zeros(dkv_ref.shape, F32)

        for h in range(X_HEADS):
            sl = slice(h * X_HEAD_DIM, (h + 1) * X_HEAD_DIM)
            qh, kh, vh = q_ref[:, sl], k_ref[:, sl], v_ref[:, sl]
            doh = do_ref[:, sl].astype(BF16)
            st = lax.dot_general(kh, qh, (((1,), (1,)), ((), ())), preferred_element_type=F32)
            e = jnp.exp(st - jnp.max(st, axis=0, keepdims=True))
            pt = e / jnp.sum(e, axis=0, keepdims=True)
            dpt = lax.dot_general(vh, doh, (((1,), (1,)), ((), ())), preferred_element_type=F32)
            dst = (pt * (dpt - jnp.sum(pt * dpt, axis=0, keepdims=True))).astype(BF16)
            dkv_ref[:, sl] += jnp.dot(dst, qh, preferred_element_type=F32)
            dkv_ref[:, d + h * X_HEAD_DIM:d + (h + 1) * X_HEAD_DIM] += jnp.dot(pt.astype(BF16), doh,
                                                                                 preferred_element_type=F32)
            dqh = lax.dot_general(dst, kh, (((0,), (0,)), ((), ())), preferred_element_type=F32)
            dq_ref[:, sl] = (dqh * scale).astype(dq_ref.dtype)

    return pl.pallas_call(
        body, out_shape=(jax.ShapeDtypeStruct((seq, d), BF16), jax.ShapeDtypeStruct((mlen, 2 * d), F32)),
        grid=(seq // tq,),
        in_specs=[pl.BlockSpec((tq, d), lambda i: (i, 0)), pl.BlockSpec((mlen, d), lambda i: (0, 0)),
                  pl.BlockSpec((mlen, d), lambda i: (0, 1)), pl.BlockSpec((tq, d), lambda i: (i, 0))],
        out_specs=(pl.BlockSpec((tq, d), lambda i: (i, 0)), pl.BlockSpec((mlen, 2 * d), lambda i: (0, 0))),
        compiler_params=_cp(("arbitrary",)), name=name)(q, kv, kv, do)


def _halo_specs(tr, tc, seq, col):
    per, last = tr // HALO, seq // HALO - 1
    return [pl.BlockSpec((tr, tc), lambda j, r: (r, col(j))),
            pl.BlockSpec((HALO, tc), lambda j, r: (jnp.maximum(r * per - 1, 0), col(j))),
            pl.BlockSpec((HALO, tc), lambda j, r: (jnp.minimum((r + 1) * per, last), col(j)))]


def _extend(main_ref, prev_ref, next_ref, r, nr):
    pv = (r > 0).astype(F32)
    nv = (r < nr - 1).astype(F32)
    return jnp.concatenate([prev_ref[...].astype(F32) * pv, main_ref[...].astype(F32),
                            next_ref[...].astype(F32) * nv], axis=0)


def _conv3(e, w_ref, n):
    return pltpu.roll(e, 1, axis=0) * w_ref[0:1, :] + e * w_ref[1:2, :] + pltpu.roll(e, n - 1, axis=0) * w_ref[2:3, :]


def _conv_gate_fwd(ug, uv, cw, cb, layer, name, plan=None, shards=(), fulls=()):
    seq, f = ug.shape
    tc = 256
    tr = _pick(seq, (HALO_TR, 512, 256))
    nc, nr = f // tc, seq // tr
    n = tr + 2 * HALO
    nt = plan.nt if plan is not None else 0

    def body(g_ref, gp_ref, gn_ref, v_ref, vp_ref, vn_ref, wg_ref, wv_ref, bg_ref, bv_ref, *rest):
        o_ref = rest[2 * nt]
        j, r = pl.program_id(0), pl.program_id(1)
        if plan is not None:
            plan.bind(rest[:nt], rest[2 * nt + 1:3 * nt + 1], *rest[3 * nt + 1:])

            @pl.when(jnp.logical_and(j == 0, r == 0))
            def _():
                plan.start()

        cg = _conv3(_extend(g_ref, gp_ref, gn_ref, r, nr), wg_ref, n)[HALO:HALO + tr] + bg_ref[...]
        cv = _conv3(_extend(v_ref, vp_ref, vn_ref, r, nr), wv_ref, n)[HALO:HALO + tr] + bv_ref[...]
        o_ref[...] = (cg * jax.nn.sigmoid(cg) * cv).astype(o_ref.dtype)

        if plan is not None:
            @pl.when(jnp.logical_and(j == nc - 1, r == nr - 1))
            def _():
                plan.forward()
                plan.finish()

    w_spec = lambda shift: pl.BlockSpec((None, 3, tc), lambda j, r: (layer, 0, j + shift))
    b_spec = lambda shift: pl.BlockSpec((None, 1, tc), lambda j, r: (layer, 0, j + shift))
    act_shape = jax.ShapeDtypeStruct((seq, f), BF16)
    act_spec = pl.BlockSpec((tr, tc), lambda j, r: (r, j))
    in_specs = _halo_specs(tr, tc, seq, lambda j: j) * 2 + [w_spec(0), w_spec(nc), b_spec(0), b_spec(nc)]
    operands = (ug, ug, ug, uv, uv, uv, cw, cw, cb, cb)
    if plan is None:
        return pl.pallas_call(body, out_shape=act_shape, grid=(nc, nr), in_specs=in_specs, out_specs=act_spec,
                              compiler_params=_cp(("parallel", "parallel")), name=name)(*operands)
    return pl.pallas_call(
        body, out_shape=(act_shape, *plan.out_shape), grid=(nc, nr), in_specs=in_specs + [ANY] * (2 * nt),
        out_specs=(act_spec, *([ANY] * nt)), scratch_shapes=plan.scratch,
        input_output_aliases={len(operands) + nt + t: 1 + t for t in range(nt)},
        compiler_params=_cp(("arbitrary", "arbitrary")), name=name)(*operands, *shards, *fulls)


def _conv_gate_bwd(ug, uv, dact, cw, cb, layer, name):
    seq, f = ug.shape
    tc = 256
    tr = _pick(seq, (512, 256))
    nc, nr = f // tc, seq // tr
    n = tr + 2 * HALO

    def body(g_ref, gp_ref, gn_ref, v_ref, vp_ref, vn_ref, d_ref, dp_ref, dn_ref, wg_ref, wv_ref, bg_ref, bv_ref,
             dug_ref, duv_ref, dwg_ref, dwv_ref):
        r = pl.program_id(1)
        eg = _extend(g_ref, gp_ref, gn_ref, r, nr)
        ev = _extend(v_ref, vp_ref, vn_ref, r, nr)
        da = _extend(d_ref, dp_ref, dn_ref, r, nr)
        eg3 = (pltpu.roll(eg, 1, axis=0), eg, pltpu.roll(eg, n - 1, axis=0))
        ev3 = (pltpu.roll(ev, 1, axis=0), ev, pltpu.roll(ev, n - 1, axis=0))
        cg = eg3[0] * wg_ref[0:1, :] + eg3[1] * wg_ref[1:2, :] + eg3[2] * wg_ref[2:3, :] + bg_ref[...]
        cv = ev3[0] * wv_ref[0:1, :] + ev3[1] * wv_ref[1:2, :] + ev3[2] * wv_ref[2:3, :] + bv_ref[...]
        sg = jax.nn.sigmoid(cg)
        dcv = da * (cg * sg)
        dcg = da * cv * (sg * (1.0 + cg * (1.0 - sg)))

        def back(dc, e3, w_ref, du_ref, dw_ref):
            du = (pltpu.roll(dc, n - 1, axis=0) * w_ref[0:1, :] + dc * w_ref[1:2, :]
                  + pltpu.roll(dc, 1, axis=0) * w_ref[2:3, :])
            du_ref[...] = du[HALO:HALO + tr].astype(du_ref.dtype)
            dcm = dc[HALO:HALO + tr]
            taps = [jnp.sum(dcm * e[HALO:HALO + tr], axis=0, keepdims=True) for e in e3] + [
                    jnp.sum(dcm, axis=0, keepdims=True)]
            part = jnp.concatenate(taps + [jnp.zeros((4, tc), F32)], axis=0)

            @pl.when(r == 0)
            def _():
                dw_ref[...] = part

            @pl.when(r > 0)
            def _():
                dw_ref[...] += part

        back(dcg, eg3, wg_ref, dug_ref, dwg_ref)
        back(dcv, ev3, wv_ref, duv_ref, dwv_ref)

    w_spec = lambda shift: pl.BlockSpec((None, 3, tc), lambda j, r: (layer, 0, j + shift))
    b_spec = lambda shift: pl.BlockSpec((None, 1, tc), lambda j, r: (layer, 0, j + shift))
    out_rows = pl.BlockSpec((tr, tc), lambda j, r: (r, j))
    out_acc = pl.BlockSpec((8, tc), lambda j, r: (0, j))
    return pl.pallas_call(
        body,
        out_shape=(jax.ShapeDtypeStruct((seq, f), BF16), jax.ShapeDtypeStruct((seq, f), BF16),
                   jax.ShapeDtypeStruct((8, f), F32), jax.ShapeDtypeStruct((8, f), F32)),
        grid=(nc, nr),
        in_specs=_halo_specs(tr, tc, seq, lambda j: j) * 3 + [w_spec(0), w_spec(nc), b_spec(0), b_spec(nc)],
        out_specs=(out_rows, out_rows, out_acc, out_acc),
        compiler_params=_cp(("parallel", "arbitrary")), name=name)(ug, ug, ug, uv, uv, uv, dact, dact, dact, cw, cw, cb, cb)


def _pool_count(g, r, tr, n, seq):
    half = jnp.left_shift(1, g)
    t = r * tr - HALO + lax.broadcasted_iota(jnp.int32, (n, 1), 0)
    cnt = jnp.minimum(t + half, seq) - jnp.maximum(t - half, 0)
    return jnp.maximum(cnt, 1).astype(F32)


def _by_group(g, levels):
    out = levels[3]
    for i in (2, 1, 0):
        out = jnp.where(g == i, levels[i], out)
    return out


def _pool_mixed(e, g, cnt, n):
    w2 = e + pltpu.roll(e, 1, axis=0)
    w4 = pltpu.roll(w2, 1, axis=0) + pltpu.roll(w2, n - 1, axis=0)
    w8 = pltpu.roll(w4, 2, axis=0) + pltpu.roll(w4, n - 2, axis=0)
    w16 = pltpu.roll(w8, 4, axis=0) + pltpu.roll(w8, n - 4, axis=0)
    return _by_group(g, (w2, w4, w8, w16)) / cnt - e


def _pool_fwd(hp, xres, pw, scale, name, plan, shards, fulls):
    seq, d = hp.shape
    tc = POOL_GROUP_W
    tr = _pick(seq, (HALO_TR, 512, 256))
    nr = seq // tr
    n = tr + 2 * HALO
    nt = plan.nt

    def body(h_ref, hp_ref, hn_ref, x_ref, w_ref, s_ref, *rest):
        o_ref = rest[2 * nt]
        plan.bind(rest[:nt], rest[2 * nt + 1:3 * nt + 1], *rest[3 * nt + 1:])
        g, r = pl.program_id(0), pl.program_id(1)

        def compute():
            e = _extend(h_ref, hp_ref, hn_ref, r, nr)
            mixed = _pool_mixed(e, g, _pool_count(g, r, tr, n, seq), n)[HALO:HALO + tr]
            y = jnp.dot(mixed.astype(BF16), w_ref[...], preferred_element_type=F32)
            o_ref[...] = x_ref[...] + y * s_ref[...]

        _hosted(plan, POOL_GROUPS * nr, g * nr + r, compute)

    operands = (hp, hp, hp, xres, pw, scale)
    return pl.pallas_call(
        body, out_shape=(jax.ShapeDtypeStruct((seq, d), F32), *plan.out_shape), grid=(POOL_GROUPS, nr),
        in_specs=_halo_specs(tr, tc, seq, lambda j: j) + [
            pl.BlockSpec((tr, tc), lambda j, r: (r, j)), pl.BlockSpec((None, tc, tc), lambda j, r: (j, 0, 0)),
            pl.BlockSpec((1, tc), lambda j, r: (0, j))] + [ANY] * (2 * nt),
        out_specs=(pl.BlockSpec((tr, tc), lambda j, r: (r, j)), *([ANY] * nt)), scratch_shapes=plan.scratch,
        input_output_aliases={len(operands) + nt + t: 1 + t for t in range(nt)},
        compiler_params=_cp(("arbitrary", "arbitrary")), name=name)(*operands, *shards, *fulls)


def _pool_bwd(hp, dy, pw, scale, name):
    seq, d = hp.shape
    tc = POOL_GROUP_W
    tr = _pick(seq, (HALO_TR, 512, 256))
    nr = seq // tr
    n = tr + 2 * HALO

    def body(h_ref, hp_ref, hn_ref, d_ref, dp_ref, dn_ref, w_ref, s_ref, dh_ref, dw_ref, ds_ref):
        g, r = pl.program_id(0), pl.program_id(1)
        cnt = _pool_count(g, r, tr, n, seq)
        e = _extend(h_ref, hp_ref, hn_ref, r, nr)
        mixed = _pool_mixed(e, g, cnt, n)[HALO:HALO + tr].astype(BF16)
        dye = _extend(d_ref, dp_ref, dn_ref, r, nr)
        dyp = (dye * s_ref[...]).astype(BF16)
        dmixed = lax.dot_general(dyp, w_ref[...], (((1,), (1,)), ((), ())), preferred_element_type=F32)
        dwin = dmixed / cnt
        m2 = dwin + pltpu.roll(dwin, n - 1, axis=0)
        m4 = pltpu.roll(m2, 1, axis=0) + pltpu.roll(m2, n - 1, axis=0)
        m8 = pltpu.roll(m4, 2, axis=0) + pltpu.roll(m4, n - 2, axis=0)
        m16 = pltpu.roll(m8, 4, axis=0) + pltpu.roll(m8, n - 4, axis=0)
        dh_ref[...] = (_by_group(g, (m2, m4, m8, m16)) - dmixed)[HALO:HALO + tr]
        ypre = jnp.dot(mixed, w_ref[...], preferred_element_type=F32)
        dsp = jnp.sum(d_ref[...] * ypre, axis=0, keepdims=True)
        dwp = lax.dot_general(mixed, dyp[HALO:HALO + tr], (((0,), (0,)), ((), ())), preferred_element_type=F32)

        @pl.when(r == 0)
        def _():
            dw_ref[...] = dwp
            ds_ref[...] = dsp

        @pl.when(r > 0)
        def _():
            dw_ref[...] += dwp
            ds_ref[...] += dsp

    return pl.pallas_call(
        body,
        out_shape=(jax.ShapeDtypeStruct((seq, d), F32), jax.ShapeDtypeStruct((POOL_GROUPS, tc, tc), F32),
                   jax.ShapeDtypeStruct((1, d), F32)),
        grid=(POOL_GROUPS, nr),
        in_specs=_halo_specs(tr, tc, seq, lambda j: j) * 2 + [
            pl.BlockSpec((None, tc, tc), lambda j, r: (j, 0, 0)), pl.BlockSpec((1, tc), lambda j, r: (0, j))],
        out_specs=(pl.BlockSpec((tr, tc), lambda j, r: (r, j)), pl.BlockSpec((None, tc, tc), lambda j, r: (j, 0, 0)),
                   pl.BlockSpec((1, tc), lambda j, r: (0, j))),
        compiler_params=_cp(("parallel", "arbitrary")), name=name)(hp, hp, hp, dy, dy, dy, pw, scale)


def _adamw_math(w, g, m, v):
    m = ADAM_B1 * m + (1.0 - ADAM_B1) * g
    v = ADAM_B2 * v + (1.0 - ADAM_B2) * (g * g)
    m_hat = m / (1.0 - ADAM_B1 ** ADAM_STEP)
    v_hat = v / (1.0 - ADAM_B2 ** ADAM_STEP)
    delta = -ADAM_LR * (m_hat / (jnp.sqrt(v_hat) + ADAM_EPS) + ADAM_WD * w)
    return delta, m, v


def _adamw(w, ga, gb, m, v, name):
    rows, cols = w.shape
    tr = _pick(rows, (256, 128, 64, 32, 16, 8))
    two = gb is not None

    def body(*refs):
        if two:
            w_ref, ga_ref, gb_ref, m_ref, v_ref, g_out, d_out, m_out, v_out = refs
            g = ga_ref[...] + gb_ref[...]
        else:
            w_ref, ga_ref, m_ref, v_ref, g_out, d_out, m_out, v_out = refs
            g = ga_ref[...]
        delta, m, v = _adamw_math(w_ref[...], g, m_ref[...], v_ref[...])
        g_out[...] = g
        d_out[...] = delta
        m_out[...] = m
        v_out[...] = v

    spec = pl.BlockSpec((tr, cols), lambda i: (i, 0))
    ops = [w, ga] + ([gb] if two else []) + [m, v]
    return pl.pallas_call(
        body, out_shape=tuple(jax.ShapeDtypeStruct((rows, cols), F32) for _ in range(4)), grid=(rows // tr,),
        in_specs=[spec] * len(ops), out_specs=(spec,) * 4, compiler_params=_cp(("parallel",)), name=name)(*ops)


def _sum4(parts, name):
    _, rows, cols = parts.shape
    tr = _pick(rows, (256, 128, 64, 32, 16))

    def body(p_ref, o_ref):
        acc = p_ref[0].astype(F32)
        for kk in range(1, 4):
            acc = acc + p_ref[kk].astype(F32)
        o_ref[...] = acc

    return pl.pallas_call(
        body, out_shape=jax.ShapeDtypeStruct((rows, cols), F32), grid=(rows // tr,),
        in_specs=[pl.BlockSpec((4, tr, cols), lambda i: (0, i, 0))], out_specs=pl.BlockSpec((tr, cols), lambda i: (i, 0)),
        compiler_params=_cp(("parallel",)), name=name)(parts)


def _place():
    x, y, c = lax.axis_index("x"), lax.axis_index("y"), lax.axis_index("c")
    chips = [(1 - x, y), (x, 1 - y), (1 - x, 1 - y)]
    return x, y, c, chips


def _window(ref, axis, j, size, c=None, half=None, lead=(), layers=slice(None)):
    if axis == "r":
        if c is None:
            return ref.at[lead + (layers, pl.ds(pl.multiple_of(j * size, 32), size), slice(None))]
        return ref.at[lead + (layers, pl.ds(pl.multiple_of(j * size + c * half, 32), half), slice(None))]
    cols = pl.ds(pl.multiple_of(j * size, LANES), size)
    if c is None:
        return ref.at[lead + (layers, slice(None), cols)]
    return ref.at[lead + (layers, pl.ds(pl.multiple_of(c * half, 32), half), cols)]


class _Gather:
    def __init__(self, shards, axes, layers=None):
        self.nt, self.axes = len(shards), axes
        self.layers = layers or [slice(None)] * self.nt
        self.out_shape, self.sizes, self.halves = [], [], []
        for s, ax in zip(shards, axes):
            l, rs, cs = s.shape
            self.out_shape.append(jax.ShapeDtypeStruct((l, 4 * rs, cs) if ax == "r" else (l, rs, 4 * cs), s.dtype))
            self.sizes.append(rs if ax == "r" else cs)
            self.halves.append(rs // 2)
        self.scratch = [pltpu.SemaphoreType.DMA((6 * self.nt,)), pltpu.SemaphoreType.DMA((6 * self.nt,)),
                        pltpu.SemaphoreType.DMA((self.nt,))]

    def bind(self, src, dst, send_sems, recv_sems, local_sems):
        self.src, self.dst, self.send_sems, self.recv_sems, self.local_sems = src, dst, send_sems, recv_sems, local_sems

    def _win(self, t, j, core=None):
        return _window(self.dst[t], self.axes[t], j, self.sizes[t], core, self.halves[t], layers=self.layers[t])

    def _ici(self, t, kk, origin):
        _, _, c, chips = _place()
        px, py = chips[kk]
        half = self.src[t].at[self.layers[t], pl.ds(pl.multiple_of(c * self.halves[t], 16), self.halves[t]), :]
        return pltpu.make_async_remote_copy(
            src_ref=half, dst_ref=self._win(t, origin, c), send_sem=self.send_sems.at[t * 3 + kk],
            recv_sem=self.recv_sems.at[t * 3 + kk], device_id=(px, py, c), device_id_type=MESH)

    def _d2d(self, t, kk, origin, core):
        x, y, c, _ = _place()
        k2 = 3 * self.nt + t * 3 + kk
        return pltpu.make_async_remote_copy(
            src_ref=self._win(t, origin, core), dst_ref=self._win(t, origin, core), send_sem=self.send_sems.at[k2],
            recv_sem=self.recv_sems.at[k2], device_id=(x, y, 1 - c), device_id_type=MESH)

    def _local(self, t):
        x, y, _, _ = _place()
        return pltpu.make_async_copy(self.src[t].at[self.layers[t]], self._win(t, 2 * x + y), self.local_sems.at[t])

    def _each(self):
        _, _, _, chips = _place()
        for t in range(self.nt):
            for kk in range(3):
                px, py = chips[kk]
                yield t, kk, 2 * px + py

    def start(self):
        x, y, _, _ = _place()
        for t in range(self.nt):
            self._local(t).start()
        for t, kk, _ in self._each():
            self._ici(t, kk, 2 * x + y).start()

    def forward(self):
        _, _, c, _ = _place()
        for t, kk, origin in self._each():
            self._ici(t, kk, origin).wait_recv()
            self._d2d(t, kk, origin, c).start()

    def finish(self):
        x, y, c, _ = _place()
        for t, kk, origin in self._each():
            self._d2d(t, kk, origin, 1 - c).wait_recv()
        for t, kk, origin in self._each():
            self._ici(t, kk, 2 * x + y).wait_send()
            self._d2d(t, kk, origin, c).wait_send()
        for t in range(self.nt):
            self._local(t).wait()


class _Scatter:
    def __init__(self, grads, axes):
        self.nt, self.axes = len(grads), axes
        self.out_shape, self.sizes = [], []
        for gr, ax in zip(grads, axes):
            l, r, cc = gr.shape
            self.out_shape.append(jax.ShapeDtypeStruct((4, l, r // 4, cc) if ax == "r" else (4, l, r, cc // 4), gr.dtype))
            self.sizes.append(r // 4 if ax == "r" else cc // 4)
        self.scratch = [pltpu.SemaphoreType.DMA((3 * self.nt,)), pltpu.SemaphoreType.DMA((3 * self.nt,)),
                        pltpu.SemaphoreType.DMA((self.nt,))]

    def bind(self, src, dst, send_sems, recv_sems, local_sems):
        self.src, self.dst, self.send_sems, self.recv_sems, self.local_sems = src, dst, send_sems, recv_sems, local_sems

    def _copy(self, t, kk, slot):
        x, y, c, chips = _place()
        px, py = chips[kk]
        return pltpu.make_async_remote_copy(
            src_ref=_window(self.src[t], self.axes[t], 2 * px + py, self.sizes[t]), dst_ref=self.dst[t].at[slot],
            send_sem=self.send_sems.at[t * 3 + kk], recv_sem=self.recv_sems.at[t * 3 + kk],
            device_id=(px, py, c), device_id_type=MESH)

    def _local(self, t):
        x, y, _, _ = _place()
        me = 2 * x + y
        return pltpu.make_async_copy(_window(self.src[t], self.axes[t], me, self.sizes[t]), self.dst[t].at[me],
                                     self.local_sems.at[t])

    def start(self):
        x, y, _, _ = _place()
        for t in range(self.nt):
            self._local(t).start()
            for kk in range(3):
                self._copy(t, kk, 2 * x + y).start()

    def finish(self):
        _, _, _, chips = _place()
        for t in range(self.nt):
            for kk in range(3):
                px, py = chips[kk]
                self._copy(t, kk, 2 * px + py).wait_recv()
        for t in range(self.nt):
            for kk in range(3):
                px, py = chips[kk]
                self._copy(t, kk, 2 * px + py).wait_send()
            self._local(t).wait()


def _comm_call(plans, operands, name):
    nts = [p.nt for p in plans]
    n_in, n_sem = sum(nts), [len(p.scratch) for p in plans]

    def body(*refs):
        pos_in, pos_out, pos_sem = 0, n_in, 2 * n_in
        for p, nt, ns in zip(plans, nts, n_sem):
            p.bind(refs[pos_in:pos_in + nt], refs[pos_out:pos_out + nt], *refs[pos_sem:pos_sem + ns])
            pos_in, pos_out, pos_sem = pos_in + nt, pos_out + nt, pos_sem + ns
        for p in plans:
            p.start()
        for p in plans:
            if hasattr(p, "forward"):
                p.forward()
        for p in plans:
            p.finish()

    return pl.pallas_call(
        body, out_shape=tuple(s for p in plans for s in p.out_shape), in_specs=[ANY] * n_in,
        out_specs=tuple([ANY] * n_in), scratch_shapes=[s for p in plans for s in p.scratch],
        name=name)(*[a for ops in operands for a in ops])


class _GatherAll:
    FLIPS = [f for f in itertools.product((0, 1), repeat=3) if any(f)]

    def __init__(self, pack):
        self.nt = 1
        self.out_shape = [jax.ShapeDtypeStruct((8,) + pack.shape, pack.dtype)]
        self.scratch = [pltpu.SemaphoreType.DMA((7,)), pltpu.SemaphoreType.DMA((7,)), pltpu.SemaphoreType.DMA((1,))]

    def bind(self, src, dst, send_sems, recv_sems, local_sems):
        self.src, self.dst, self.send_sems, self.recv_sems, self.local_sems = src[0], dst[0], send_sems, recv_sems, local_sems

    def _copy(self, kk, mine):
        x, y, c, _ = _place()
        px, py, pc = (1 - v if fl else v for v, fl in zip((x, y, c), self.FLIPS[kk]))
        slot = 4 * x + 2 * y + c if mine else 4 * px + 2 * py + pc
        return pltpu.make_async_remote_copy(src_ref=self.src, dst_ref=self.dst.at[slot], send_sem=self.send_sems.at[kk],
                                            recv_sem=self.recv_sems.at[kk], device_id=(px, py, pc), device_id_type=MESH)

    def _local(self):
        x, y, c, _ = _place()
        return pltpu.make_async_copy(self.src, self.dst.at[4 * x + 2 * y + c], self.local_sems.at[0])

    def start(self):
        self._local().start()
        for kk in range(7):
            self._copy(kk, True).start()

    def finish(self):
        for kk in range(7):
            self._copy(kk, False).wait_recv()
        for kk in range(7):
            self._copy(kk, True).wait_send()
        self._local().wait()


class _Swap:
    def __init__(self, arrs):
        self.nt = len(arrs)
        self.out_shape = [jax.ShapeDtypeStruct(a.shape, a.dtype) for a in arrs]
        self.scratch = [pltpu.SemaphoreType.DMA((self.nt,)), pltpu.SemaphoreType.DMA((self.nt,))]

    def bind(self, src, dst, send_sems, recv_sems):
        self.src, self.dst, self.send_sems, self.recv_sems = src, dst, send_sems, recv_sems

    def _copy(self, t):
        x, y, c, _ = _place()
        return pltpu.make_async_remote_copy(src_ref=self.src[t], dst_ref=self.dst[t], send_sem=self.send_sems.at[t],
                                            recv_sem=self.recv_sems.at[t], device_id=(x, y, 1 - c), device_id_type=MESH)

    def start(self):
        for t in range(self.nt):
            self._copy(t).start()

    def finish(self):
        for t in range(self.nt):
            self._copy(t).wait()


def _sum8(packs, name):
    def body(p_ref, o_ref):
        acc = p_ref[0]
        for dev in range(1, 8):
            acc = acc + p_ref[dev]
        o_ref[...] = acc

    return pl.pallas_call(body, out_shape=jax.ShapeDtypeStruct(packs.shape[1:], F32), name=name)(packs)


def _pack(arrs):
    flat = jnp.concatenate([a.reshape(-1).astype(F32) for a in arrs])
    rows = -(-flat.shape[0] // (8 * LANES)) * 8
    return jnp.pad(flat, (0, rows * LANES - flat.shape[0])).reshape(rows, LANES)


def _unpack(flat, shapes):
    out, pos = [], 0
    for shp in shapes:
        size = 1
        for s in shp:
            size *= s
        out.append(flat[pos:pos + size].reshape(shp))
        pos += size
    return out


BIG = ("attn_w_qkv", "attn_w_o", "pool_w", "xattn_w_q", "xattn_w_kv", "xattn_w_o", "ffn_w_up", "ffn_w_down")
BIG_AXIS = ("c", "r", "r", "r", "c", "r", "c", "r")
SMALL_REPL = ("attn_norm", "attn_q_gain", "attn_k_gain", "xattn_norm", "mem_norm", "ffn_norm", "ffn_conv_b", "final_norm")
SMALL_SHARD = ("pool_norm", "pool_scale", "ffn_conv_w")
ORDER = ("attn_norm", "attn_w_qkv", "attn_q_gain", "attn_k_gain", "attn_w_o", "pool_norm", "pool_w", "pool_scale",
         "xattn_norm", "mem_norm", "xattn_w_q", "xattn_w_kv", "xattn_w_o", "ffn_norm", "ffn_w_up", "ffn_conv_w",
         "ffn_conv_b", "ffn_w_down", "final_norm")


def _step(x, mem, tgt, w, m, v):
    seq, d = x.shape
    xi, yi, ci = lax.axis_index("x"), lax.axis_index("y"), lax.axis_index("c")
    chip = 2 * xi + yi
    dff = w["ffn_w_down"].shape[1] * 4
    n_layers = w["ffn_norm"].shape[0]

    def as3d(a):
        return a.reshape(a.shape[-3:])
    shards = [as3d(w[nm]).astype(BF16) for nm in BIG]
    small_in = [w[nm] for nm in SMALL_SHARD]
    small_pack = _pack(small_in)
    conv_b = w["ffn_conv_b"].reshape(n_layers, 1, -1)
    tabs = _rope_tables(seq)
    qg2 = jnp.tile(w["attn_q_gain"], (1, 2))
    kg2 = jnp.tile(w["attn_k_gain"], (1, 2))
    mm = functools.partial(_mm)

    saved = {}
    x0 = x
    h0, wq = _rms_fwd(x0, w["attn_norm"], BF16, "rms_attn", _Gather(shards[:1], BIG_AXIS[:1]), shards[:1])
    qkv = mm(h0, wq, "nn", b_l=0, out_dtype=F32, name="mm_qkv")
    q_r, k_r, k_t, v_b, v_t, small_all = _qk_prep(qkv, qg2, kg2, tabs, _GatherAll(small_pack), [small_pack], "qk_prep")
    per_chip = [_unpack(small_all[2 * j].reshape(-1), [a.shape for a in small_in]) for j in range(4)]
    pool_norm, pool_scale, conv_w = (jnp.concatenate([per_chip[j][i] for j in range(4)], axis=-1) for i in range(3))
    first = [slice(None)] * 5 + [slice(0, 1)] * 2
    o_at, lse, wo, wp, wxq, wxkv, wxo, wup, wdn = _flash_fwd(
        q_r, k_r, v_t, _Gather(shards[1:], BIG_AXIS[1:], first), shards[1:], "flash_fwd")
    ffn_w = {"up": wup, "down": wdn}
    x1, hq0 = mm(o_at, wo, "nn", b_l=0, res=x0, out_dtype=F32, norm_out=(w["xattn_norm"][0:1], BF16), name="mm_attn_o")

    def xattn_fwd(l, xin, hq):
        mn = _rms_fwd(mem, w["mem_norm"][l:l + 1], BF16, f"rms_mem{l}")
        xq = mm(hq, wxq, "nn", b_l=l, scale=X_HEAD_DIM ** -0.5, out_dtype=BF16, name=f"mm_xq{l}")
        kv = mm(mn, wxkv, "nn", b_l=l, out_dtype=BF16, name=f"mm_xkv{l}")
        xo = _xattn_fwd(xq, kv, f"xattn_fwd{l}")
        saved[f"x{l}"] = (hq, mn, xq, kv, xo)
        return mm(xo, wxo, "nn", b_l=l, res=xin, out_dtype=F32, norm_out=(w["ffn_norm"][l:l + 1], BF16), name=f"mm_xo{l}")

    def ffn_fwd(l, xin, hf, norm_out):
        ug = mm(hf, ffn_w["up"], "nn", b_l=l, n=dff, out_dtype=BF16, name=f"mm_up_g{l}")
        uv = mm(hf, ffn_w["up"], "nn", b_l=l, n=dff, b_off=(0, dff), out_dtype=BF16, name=f"mm_up_v{l}")
        if l == 0:
            rest = _Gather(shards[6:7], BIG_AXIS[6:7], [slice(1, 2)])
            act, ffn_w["up"] = _conv_gate_fwd(ug, uv, conv_w, conv_b, l, f"conv_gate{l}", rest, shards[6:7], [ffn_w["up"]])
        else:
            act = _conv_gate_fwd(ug, uv, conv_w, conv_b, l, f"conv_gate{l}")
        saved[f"f{l}"] = (hf, ug, uv, act)
        return mm(act, ffn_w["down"], "nn", b_l=l, res=xin, out_dtype=F32, norm_out=norm_out, name=f"mm_down{l}")

    x2, hf0 = xattn_fwd(0, x1, hq0)
    x3, hp = ffn_fwd(0, x2, hf0, (pool_norm, F32))
    x4, ffn_w["down"] = _pool_fwd(hp, x3, wp, pool_scale, "pool_fwd", _Gather(shards[7:], BIG_AXIS[7:], [slice(1, 2)]),
                                  shards[7:], [ffn_w["down"]])
    x5, hf1 = xattn_fwd(1, x4, _rms_fwd(x4, w["xattn_norm"][1:2], BF16, "rms_xq1"))
    xs = [x0, x1, x2, x3, x4, x5, ffn_fwd(1, x5, hf1, None)]
    dres, g_final, loss = _final_loss(xs[6], w["final_norm"].reshape(1, d), tgt, "final_loss")

    grads = {}
    gbuf = {}

    def dw(nm, a, b, layer, full, off=(0, 0), n=None, tn=None):
        gbuf[nm] = _mm(a, b, "tn", out_dtype=BF16, out_full=full, out_l=layer, out_off=off, n=n, tn=tn,
                       alias=gbuf.get(nm), name=f"dw_{nm}{layer}_{off[1]}")

    def ffn_bwd(l, xin, dres):
        hf, ug, uv, act = saved[f"f{l}"]
        wup, wdn = ffn_w["up"], ffn_w["down"]
        dw("ffn_w_down", act, dres, l, wdn.shape)
        dact = _mm(dres, wdn, "nt", b_l=l, out_dtype=BF16, name=f"mm_dact{l}")
        dug, duv, dwg, dwv = _conv_gate_bwd(ug, uv, dact, conv_w, conv_b, l, f"conv_gate_bwd{l}")
        dw("ffn_w_up", hf, dug, l, wup.shape, tn=1408)
        dw("ffn_w_up", hf, duv, l, wup.shape, off=(0, dff), tn=1408)
        dhf = _mm(dug, wup, "nt", b_l=l, n=d, out_dtype=F32, name=f"mm_dhf_g{l}")
        dres, dg = _mm(duv, wup, "nt", b_l=l, n=d, b_off=(0, dff), res=dhf, out_dtype=F32, tm=256,
                       norm_bwd=(xin, w["ffn_norm"][l:l + 1], dres), name=f"mm_dhf_v{l}")
        return dres, dg, jnp.concatenate([dwg[:3], dwv[:3]], axis=1), jnp.concatenate([dwg[3], dwv[3]], axis=0)

    def xattn_bwd(l, xin, dres):
        hq, mn, xq, kv, xo = saved[f"x{l}"]
        dw("xattn_w_o", xo, dres, l, wxo.shape)
        dxo = _mm(dres, wxo, "nt", b_l=l, out_dtype=BF16, name=f"mm_dxo{l}")
        dq, dkv = _xattn_bwd(xq, kv, dxo, f"xattn_bwd{l}")
        dw("xattn_w_q", hq, dq, l, wxq.shape)
        dw("xattn_w_kv", mn, dkv, l, wxkv.shape)
        dmn = _mm(dkv, wxkv, "nt", b_l=l, out_dtype=F32, name=f"mm_dmn{l}")
        _, dg_mem = _rms_bwd(mem, w["mem_norm"][l:l + 1], dmn, None, f"rms_mem_bwd{l}")
        dres, dg = _mm(dq, wxq, "nt", b_l=l, out_dtype=F32, norm_bwd=(xin, w["xattn_norm"][l:l + 1], dres),
                       name=f"mm_dhq{l}")
        return dres, dg, dg_mem

    g_ffn, g_xn, g_mn, g_cw, g_cb = [None] * n_layers, [None] * n_layers, [None] * n_layers, [None] * n_layers, [None] * n_layers
    dres, g_ffn[1], g_cw[1], g_cb[1] = ffn_bwd(1, xs[5], dres)
    dres, g_xn[1], g_mn[1] = xattn_bwd(1, xs[4], dres)
    dhp, g_pw, g_pscale = _pool_bwd(hp, dres, wp, pool_scale, "pool_bwd")
    dres, g_pnorm = _rms_bwd(xs[3], pool_norm, dhp, dres, "rms_pool_bwd")
    dres, g_ffn[0], g_cw[0], g_cb[0] = ffn_bwd(0, xs[2], dres)
    dres, g_xn[0], g_mn[0] = xattn_bwd(0, xs[1], dres)
    dw("attn_w_o", o_at, dres, 0, wo.shape)
    do = _mm(dres, wo, "nt", b_l=0, out_dtype=BF16, name="mm_do")
    gbuf["pool_w"] = g_pw.astype(BF16)
    early = [gbuf[nm] for nm in BIG[1:]]
    dq_r, dk_r, dv, *recv_early = _flash_bwd(q_r, k_r, k_t, v_b, do, o_at, lse, _Scatter(early, BIG_AXIS[1:]), early,
                                             "flash_bwd")
    def sum4(nm, rc):
        return _sum4(rc.reshape(4, -1, rc.shape[-1]), f"sum4_{nm}")
    sums_early = [sum4(nm, rc) for nm, rc in zip(BIG[1:], recv_early)]
    dqkv, dqg, dkg, *others_early = _qk_prep_bwd(qkv, dq_r, dk_r, dv, qg2, kg2, tabs, _Swap(sums_early), sums_early,
                                                 "qk_prep_bwd")
    dw("attn_w_qkv", h0, dqkv, 0, wq.shape)
    grad_x, g_an = _mm(dqkv, wq, "nt", b_l=0, out_dtype=F32, norm_bwd=(x0, w["attn_norm"], dres), name="mm_dh0")

    small_g = {
        "attn_norm": g_an, "attn_q_gain": dqg[:, :HEAD_DIM] + dqg[:, HEAD_DIM:], "attn_k_gain": dkg[:, :HEAD_DIM] + dkg[:, HEAD_DIM:],
        "xattn_norm": jnp.concatenate(g_xn, axis=0), "mem_norm": jnp.concatenate(g_mn, axis=0),
        "ffn_norm": jnp.concatenate(g_ffn, axis=0), "ffn_conv_b": jnp.stack(g_cb, axis=0), "final_norm": g_final.reshape(d),
        "pool_norm": g_pnorm, "pool_scale": g_pscale, "ffn_conv_w": jnp.stack(g_cw, axis=0)}
    names = SMALL_REPL + SMALL_SHARD
    small_pack = _pack([loss[0, :1]] + [small_g[nm] for nm in names])
    late = [gbuf[nm] for nm in BIG[:1]]
    small_all, recv_late = _comm_call([_GatherAll(small_pack), _Scatter(late, BIG_AXIS[:1])], [[small_pack], late],
                                      "reduce_small_scatter_qkv")
    total = _sum8(small_all, "sum_small")
    parts = _unpack(total.reshape(-1), [(1,)] + [small_g[nm].shape for nm in names])
    loss_out = parts[0][0]
    for nm, g in zip(names, parts[1:]):
        if nm in SMALL_SHARD:
            size = w[nm].shape[-1]
            g = lax.dynamic_slice_in_dim(g, chip * size, size, axis=g.ndim - 1)
        grads[nm] = g.reshape(w[nm].shape)

    packed = [_pack([src[nm] for nm in names]) for src in (w, grads, m, v)]
    _, sd, sm, sv = _adamw(packed[0], packed[1], None, packed[2], packed[3], "adamw_small")
    shapes = [w[nm].shape for nm in names]
    delta = dict(zip(names, _unpack(sd.reshape(-1), shapes)))
    new_m = dict(zip(names, _unpack(sm.reshape(-1), shapes)))
    new_v = dict(zip(names, _unpack(sv.reshape(-1), shapes)))

    sums_late = [sum4(BIG[0], recv_late)]
    others_late = _comm_call([_Swap(sums_late)], [sums_late], "swap_qkv")
    for nm, mine, other in zip(BIG, sums_late + sums_early, list(others_late) + others_early):
        cols = mine.shape[-1]
        outs = _adamw(w[nm].reshape(-1, cols), mine, other, m[nm].reshape(-1, cols), v[nm].reshape(-1, cols), f"adamw_{nm}")
        grads[nm], delta[nm], new_m[nm], new_v[nm] = (o.reshape(w[nm].shape) for o in outs)

    return loss_out, grad_x, grads, delta, new_m, new_v


def kernel(x, mem, attn_norm, attn_w_qkv, attn_q_gain, attn_k_gain, attn_w_o, pool_norm, pool_w, pool_scale, xattn_norm, mem_norm, xattn_w_q, xattn_w_kv, xattn_w_o, ffn_norm, ffn_w_up, ffn_conv_w, ffn_conv_b, ffn_w_down, final_norm, loss_target, m_attn_norm, m_attn_w_qkv, m_attn_q_gain, m_attn_k_gain, m_attn_w_o, m_pool_norm, m_pool_w, m_pool_scale, m_xattn_norm, m_mem_norm, m_xattn_w_q, m_xattn_w_kv, m_xattn_w_o, m_ffn_norm, m_ffn_w_up, m_ffn_conv_w, m_ffn_conv_b, m_ffn_w_down, m_final_norm, v_attn_norm, v_attn_w_qkv, v_attn_q_gain, v_attn_k_gain, v_attn_w_o, v_pool_norm, v_pool_w, v_pool_scale, v_xattn_norm, v_mem_norm, v_xattn_w_q, v_xattn_w_kv, v_xattn_w_o, v_ffn_norm, v_ffn_w_up, v_ffn_conv_w, v_ffn_conv_b, v_ffn_w_down, v_final_norm):
    given = dict(locals())
    w = {nm: given[nm] for nm in ORDER}
    m = {nm: given["m_" + nm] for nm in ORDER}
    v = {nm: given["v_" + nm] for nm in ORDER}
    seq, d = x.shape[1], x.shape[2]
    loss, grad_x, grads, delta, new_m, new_v = _step(
        x.reshape(seq, d), mem.reshape(mem.shape[1], d), loss_target.reshape(seq, d), w, m, v)
    return (loss, grad_x.reshape(x.shape), *[grads[nm] for nm in ORDER], *[delta[nm] for nm in ORDER],
            *[new_m[nm] for nm in ORDER], *[new_v[nm] for nm in ORDER])
```

```python
import functools
import itertools

import jax
import jax.numpy as jnp
from jax import lax
from jax.experimental import pallas as pl
from jax.experimental.pallas import tpu as pltpu

F32, BF16 = jnp.float32, jnp.bfloat16
EPS = 1e-6
GRID_W = 64
ROPE_THETA = 10000.0
HEAD_DIM = 64
N_HEADS = 16
N_KV = 4
X_HEADS = 4
X_HEAD_DIM = 256
POOL_GROUPS = 4
POOL_GROUP_W = 256
HALO = 16
HALO_TR = 2048
LANES = 128
ADAM_LR, ADAM_B1, ADAM_B2, ADAM_EPS, ADAM_WD, ADAM_STEP = 0.001, 0.9, 0.999, 1e-08, 0.01, 10
VMEM_LIMIT = 48 * 1024 * 1024
MESH = pl.DeviceIdType.MESH
NEG = -1e30
LOG2E = 1.4426950408889634
FLASH_TQ, FLASH_TK = 512, 4096
FLASH_SUB = 512
ANY = pl.BlockSpec(memory_space=pl.ANY)


def _cp(sem=None):
    return pltpu.CompilerParams(dimension_semantics=sem, vmem_limit_bytes=VMEM_LIMIT)


def _pick(n, cands):
    for c in cands:
        if c <= n and n % c == 0:
            return c
    return n


def _mm(a, b, mode, *, name, out_dtype, tm=None, tn=None, tk=None, n=None, k=None, b_l=None, b_off=(0, 0),
        res=None, scale=None, out_full=None, out_l=None, out_off=(0, 0), alias=None, norm_out=None, norm_bwd=None):
    if mode == "tn":
        K, M = a.shape
    else:
        M, K = a.shape
    bs = b.shape[-2:]
    if mode == "nn":
        K = k or K
        N = n or bs[1]
    elif mode == "nt":
        N = n or bs[0]
    else:
        N = n or bs[1]
    wide = (1408, 1024, 512, 256, 128)
    if mode == "tn":
        tm = tm or (M if M <= 1024 else _pick(M, wide))
        tk = tk or _pick(K, (2048, 1024, 512, 256, 128))
    else:
        small = K <= 1024 and N <= 1024 and norm_bwd is None
        tm = _pick(M, (tm or (1024 if small else 512), 512, 256, 128))
        tk = tk or (K if K <= 2816 else _pick(K, wide))
    tn = tn or (N if N <= 1536 else _pick(N, wide))
    assert M % tm == 0 and N % tn == 0 and K % tk == 0, (name, M, N, K, tm, tn, tk)
    nk = K // tk
    dims = {"nn": ((1,), (0,)), "nt": ((1,), (1,)), "tn": ((0,), (0,))}[mode]

    j_outer = nk == 1 and mode != "tn"

    def at(f):
        return (lambda j, i, kk: f(i, j, kk)) if j_outer else f

    if mode == "tn":
        a_spec = pl.BlockSpec((tk, tm), at(lambda i, j, kk: (kk, i)))
    else:
        a_spec = pl.BlockSpec((tm, tk), at(lambda i, j, kk: (i, kk)))
    if mode == "nt":
        bb, (d0, d1) = (tn, tk), (b_off[0] // tn, b_off[1] // tk)
        assert b_off[0] % tn == 0 and b_off[1] % tk == 0
        bidx = lambda i, j, kk: (j + d0, kk + d1)
    else:
        bb, (d0, d1) = (tk, tn), (b_off[0] // tk, b_off[1] // tn)
        assert b_off[0] % tk == 0 and b_off[1] % tn == 0
        bidx = lambda i, j, kk: (kk + d0, j + d1)
    if b.ndim == 3:
        b_spec = pl.BlockSpec((None,) + bb, at(lambda i, j, kk: (b_l,) + bidx(i, j, kk)))
    else:
        b_spec = pl.BlockSpec(bb, at(bidx))
    in_specs, operands = [a_spec, b_spec], [a, b]
    if res is not None:
        in_specs.append(pl.BlockSpec((tm, tn), at(lambda i, j, kk: (i, j))))
        operands.append(res)
    aliases = {}
    if alias is not None:
        aliases = {len(operands): 0}
        in_specs.append(ANY)
        operands.append(alias)
    if out_full is None:
        out_shape = jax.ShapeDtypeStruct((M, N), out_dtype)
        out_spec = pl.BlockSpec((tm, tn), at(lambda i, j, kk: (i, j)))
    else:
        assert out_off[0] % tm == 0 and out_off[1] % tn == 0
        o0, o1 = out_off[0] // tm, out_off[1] // tn
        out_shape = jax.ShapeDtypeStruct(out_full, out_dtype)
        out_spec = pl.BlockSpec((None, tm, tn), at(lambda i, j, kk: (out_l, i + o0, j + o1)))
    has_res, has_alias = res is not None, alias is not None
    grid = (N // tn, M // tm, nk) if j_outer else (M // tm, N // tn, nk)
    n_extra = 0
    if norm_out is not None or norm_bwd is not None:
        assert j_outer and tn == N and out_full is None, name
        row = pl.BlockSpec((tm, tn), at(lambda i, j, kk: (i, 0)))
        vec = pl.BlockSpec((1, tn), at(lambda i, j, kk: (0, 0)))
        if norm_out is not None:
            in_specs.append(vec)
            operands.append(norm_out[0])
            n_extra = 1
            out_shape = (out_shape, jax.ShapeDtypeStruct((M, N), norm_out[1]))
            out_spec = (out_spec, row)
        else:
            in_specs += [row, vec, row]
            operands += list(norm_bwd)
            n_extra = 3
            out_shape = (out_shape, jax.ShapeDtypeStruct((1, N), F32))
            out_spec = (out_spec, vec)
    n_out = 1 if n_extra == 0 else 2

    def body(*refs):
        a_ref, b_ref = refs[0], refs[1]
        pos = 2
        res_ref = None
        if has_res:
            res_ref = refs[pos]
            pos += 1
        if has_alias:
            pos += 1
        extra = refs[pos:pos + n_extra]
        pos += n_extra
        o_ref, acc_ref = refs[pos], refs[pos + n_out]
        kk = pl.program_id(2)
        part = lax.dot_general(a_ref[...].astype(BF16), b_ref[...].astype(BF16), (dims, ((), ())),
                               preferred_element_type=F32)

        def finish(acc):
            if scale is not None:
                acc = acc * scale
            if res_ref is not None:
                acc = acc + res_ref[...]
            if norm_out is not None:
                r = lax.rsqrt(jnp.mean(acc * acc, axis=-1, keepdims=True) + EPS)
                refs[pos + 1][...] = (acc * r * extra[0][...]).astype(refs[pos + 1].dtype)
            if norm_bwd is not None:
                x_ref, g_ref, dres_ref = extra
                dg_ref, step = refs[pos + 1], pl.program_id(1)
                xv = x_ref[...]
                r = lax.rsqrt(jnp.mean(xv * xv, axis=-1, keepdims=True) + EPS)
                nv = xv * r
                dgp = jnp.sum(acc * nv, axis=0, keepdims=True)

                @pl.when(step == 0)
                def _():
                    dg_ref[...] = dgp

                @pl.when(step > 0)
                def _():
                    dg_ref[...] += dgp

                dn = acc * g_ref[...]
                acc = dres_ref[...] + r * (dn - nv * jnp.mean(dn * nv, axis=-1, keepdims=True))
            o_ref[...] = acc.astype(o_ref.dtype)

        if nk == 1:
            finish(part)
        else:
            @pl.when(kk == 0)
            def _():
                acc_ref[...] = part

            @pl.when(jnp.logical_and(kk > 0, kk < nk - 1))
            def _():
                acc_ref[...] += part

            @pl.when(kk == nk - 1)
            def _():
                finish(acc_ref[...] + part)

    return pl.pallas_call(
        body, out_shape=out_shape, grid=grid, in_specs=in_specs, out_specs=out_spec,
        scratch_shapes=[pltpu.VMEM((tm, tn) if nk > 1 else (8, 128), F32)], input_output_aliases=aliases,
        compiler_params=_cp(("arbitrary",) * 3 if norm_bwd is not None else ("parallel", "parallel", "arbitrary")),
        name=name)(*operands)


def _hosted(plan, nsteps, step, compute):
    if plan is None:
        return compute()

    @pl.when(step == 0)
    def _():
        plan.start()

    compute()

    @pl.when(step == nsteps - 1)
    def _():
        if hasattr(plan, "forward"):
            plan.forward()
        plan.finish()


def _rms_fwd(x, gain, out_dtype, name, plan=None, sends=()):
    rows, d = x.shape
    tr = _pick(rows, (512, 256))
    nt, nsteps = (plan.nt if plan is not None else 0), rows // tr

    def body(x_ref, g_ref, *rest):
        o_ref = rest[nt]
        if plan is not None:
            plan.bind(rest[:nt], rest[nt + 1:2 * nt + 1], *rest[2 * nt + 1:])

        def compute():
            xv = x_ref[...]
            r = lax.rsqrt(jnp.mean(xv * xv, axis=-1, keepdims=True) + EPS)
            o_ref[...] = (xv * r * g_ref[...]).astype(o_ref.dtype)

        _hosted(plan, nsteps, pl.program_id(0), compute)

    out = jax.ShapeDtypeStruct((rows, d), out_dtype)
    row = pl.BlockSpec((tr, d), lambda i: (i, 0))
    in_specs = [row, pl.BlockSpec((1, d), lambda i: (0, 0))]
    if plan is None:
        return pl.pallas_call(body, out_shape=out, grid=(nsteps,), in_specs=in_specs, out_specs=row,
                              compiler_params=_cp(("parallel",)), name=name)(x, gain)
    return pl.pallas_call(
        body, out_shape=(out, *plan.out_shape), grid=(nsteps,), in_specs=in_specs + [ANY] * nt,
        out_specs=(row, *([ANY] * nt)), scratch_shapes=plan.scratch, compiler_params=_cp(("arbitrary",)),
        name=name)(x, gain, *sends)


def _rms_bwd(x, gain, dh, dres, name):
    rows, d = x.shape
    tr = _pick(rows, (512, 256))
    need_dx = dres is not None

    def body(*refs):
        if need_dx:
            x_ref, g_ref, dh_ref, dres_ref, o_ref, dg_ref = refs
        else:
            x_ref, g_ref, dh_ref, dg_ref = refs
        i = pl.program_id(0)
        xv = x_ref[...]
        dhv = dh_ref[...].astype(F32)
        r = lax.rsqrt(jnp.mean(xv * xv, axis=-1, keepdims=True) + EPS)
        nv = xv * r
        part = jnp.sum(dhv * nv, axis=0, keepdims=True)

        @pl.when(i == 0)
        def _():
            dg_ref[...] = part

        @pl.when(i > 0)
        def _():
            dg_ref[...] += part

        if need_dx:
            dn = dhv * g_ref[...]
            dx = r * (dn - nv * jnp.mean(dn * nv, axis=-1, keepdims=True))
            o_ref[...] = dres_ref[...] + dx

    row_spec = pl.BlockSpec((tr, d), lambda i: (i, 0))
    vec_spec = pl.BlockSpec((1, d), lambda i: (0, 0))
    if need_dx:
        return pl.pallas_call(
            body, out_shape=(jax.ShapeDtypeStruct((rows, d), F32), jax.ShapeDtypeStruct((1, d), F32)),
            grid=(rows // tr,), in_specs=[row_spec, vec_spec, row_spec, row_spec], out_specs=(row_spec, vec_spec),
            compiler_params=_cp(("arbitrary",)), name=name)(x, gain, dh, dres)
    return None, pl.pallas_call(
        body, out_shape=jax.ShapeDtypeStruct((1, d), F32), grid=(rows // tr,),
        in_specs=[row_spec, vec_spec, row_spec], out_specs=vec_spec,
        compiler_params=_cp(("arbitrary",)), name=name)(x, gain, dh)


def _final_loss(x, gain, target, name):
    rows, d = x.shape
    tr = _pick(rows, (512, 256))
    nsteps = rows // tr

    def body(x_ref, g_ref, t_ref, dx_ref, dg_ref, loss_ref, acc_ref):
        i = pl.program_id(0)
        xv = x_ref[...]
        g = g_ref[...]
        r = lax.rsqrt(jnp.mean(xv * xv, axis=-1, keepdims=True) + EPS)
        nv = xv * r
        err = nv * g - t_ref[...]
        dy = err * (1.0 / d)
        dn = dy * g
        dx_ref[...] = r * (dn - nv * jnp.mean(dn * nv, axis=-1, keepdims=True))
        dgp = jnp.sum(dy * nv, axis=0, keepdims=True)
        lp = jnp.sum(err * err, axis=0, keepdims=True)

        @pl.when(i == 0)
        def _():
            dg_ref[...] = dgp
            acc_ref[...] = lp

        @pl.when(i > 0)
        def _():
            dg_ref[...] += dgp
            acc_ref[...] += lp

        @pl.when(i == nsteps - 1)
        def _():
            tot = jnp.sum(acc_ref[...], axis=1, keepdims=True) * (0.5 / d)
            loss_ref[...] = jnp.broadcast_to(tot, loss_ref.shape)

    row_spec = pl.BlockSpec((tr, d), lambda i: (i, 0))
    vec_spec = pl.BlockSpec((1, d), lambda i: (0, 0))
    return pl.pallas_call(
        body, out_shape=(jax.ShapeDtypeStruct((rows, d), F32), jax.ShapeDtypeStruct((1, d), F32),
                         jax.ShapeDtypeStruct((1, LANES), F32)),
        grid=(nsteps,), in_specs=[row_spec, vec_spec, row_spec],
        out_specs=(row_spec, vec_spec, pl.BlockSpec((1, LANES), lambda i: (0, 0))),
        scratch_shapes=[pltpu.VMEM((1, d), F32)], compiler_params=_cp(("arbitrary",)), name=name)(x, gain, target)


def _rope_tables(seq):
    pairs = HEAD_DIM // 4
    lane = jnp.arange(LANES, dtype=jnp.int32) % HEAD_DIM
    by_col, second, pair = lane // (2 * pairs) == 1, (lane % (2 * pairs)) // pairs == 1, lane % pairs
    inv_freq = ROPE_THETA ** (-pair.astype(F32) / pairs)
    t = jnp.arange(seq, dtype=jnp.int32)[:, None]
    pos = jnp.where(by_col[None, :], t % GRID_W, t // GRID_W).astype(F32)
    ang = pos * inv_freq[None, :]
    cos, sin = jnp.cos(ang), jnp.sin(ang)
    return cos, jnp.where(second[None, :], sin, 0.0), jnp.where(second[None, :], 0.0, -sin)


def _pair_norm(xv, lo):
    sq = xv * xv
    s_lo = jnp.sum(jnp.where(lo, sq, 0.0), axis=1, keepdims=True)
    s_hi = jnp.sum(jnp.where(lo, 0.0, sq), axis=1, keepdims=True)
    return lax.rsqrt(jnp.where(lo, s_lo, s_hi) * (1.0 / HEAD_DIM) + EPS)


def _rope(y, c, sp, sm):
    return y * c + pltpu.roll(y, 16, axis=1) * sp + pltpu.roll(y, LANES - 16, axis=1) * sm


def _rope_t(dz, c, sp, sm):
    return dz * c + pltpu.roll(dz * sp, LANES - 16, axis=1) + pltpu.roll(dz * sm, 16, axis=1)


def _qk_prep(qkv, qg2, kg2, tabs, plan, sends, name):
    seq = qkv.shape[0]
    ts = _pick(seq, (256, 128))
    nq, nkp = N_HEADS // 2, N_KV // 2
    qw, kw = N_HEADS * HEAD_DIM, N_KV * HEAD_DIM
    nt, nsteps = plan.nt, seq // ts

    def body(x_ref, qg_ref, kg_ref, c_ref, sp_ref, sm_ref, *rest):
        q_ref, k_ref, kt_ref, v_ref, vt_ref = rest[nt:nt + 5]
        plan.bind(rest[:nt], rest[nt + 5:2 * nt + 5], *rest[2 * nt + 5:])
        _hosted(plan, nsteps, pl.program_id(0), lambda: compute(x_ref, qg_ref, kg_ref, c_ref, sp_ref, sm_ref,
                                                                q_ref, k_ref, kt_ref, v_ref, vt_ref))

    def compute(x_ref, qg_ref, kg_ref, c_ref, sp_ref, sm_ref, q_ref, k_ref, kt_ref, v_ref, vt_ref):
        lo = lax.broadcasted_iota(jnp.int32, (ts, LANES), 1) < HEAD_DIM
        top = lax.broadcasted_iota(jnp.int32, (LANES, ts), 0) < HEAD_DIM
        c, sp, sm = c_ref[...], sp_ref[...], sm_ref[...]
        for i in range(nq):
            xv = x_ref[:, i * LANES:(i + 1) * LANES]
            y = xv * _pair_norm(xv, lo) * qg_ref[...]
            q_ref[:, i * LANES:(i + 1) * LANES] = (_rope(y, c, sp, sm) * (LOG2E * HEAD_DIM ** -0.5)).astype(BF16)
        for i in range(nkp):
            xv = x_ref[:, qw + i * LANES:qw + (i + 1) * LANES]
            z = _rope(xv * _pair_norm(xv, lo) * kg_ref[...], c, sp, sm)
            k_ref[:, i * LANES:(i + 1) * LANES] = z.astype(BF16)
            kt_ref[i * LANES:(i + 1) * LANES, :] = z.T.astype(BF16)
            vv = x_ref[:, qw + kw + i * LANES:qw + kw + (i + 1) * LANES]
            v_ref[:, i * LANES:(i + 1) * LANES] = vv.astype(BF16)
            vvt = vv.T
            vt_ref[(2 * i) * LANES:(2 * i + 1) * LANES, :] = jnp.where(top, vvt, 1.0).astype(BF16)
            vt_ref[(2 * i + 1) * LANES:(2 * i + 2) * LANES, :] = jnp.where(top, 1.0, vvt).astype(BF16)

    tab = pl.BlockSpec((ts, LANES), lambda i: (i, 0))
    vec = pl.BlockSpec((1, LANES), lambda i: (0, 0))
    return pl.pallas_call(
        body,
        out_shape=(jax.ShapeDtypeStruct((seq, qw), BF16), jax.ShapeDtypeStruct((seq, kw), BF16),
                   jax.ShapeDtypeStruct((kw, seq), BF16), jax.ShapeDtypeStruct((seq, kw), BF16),
                   jax.ShapeDtypeStruct((N_KV * LANES, seq), BF16), *plan.out_shape),
        grid=(nsteps,),
        in_specs=[pl.BlockSpec((ts, qw + 2 * kw), lambda i: (i, 0)), vec, vec, tab, tab, tab] + [ANY] * nt,
        out_specs=(pl.BlockSpec((ts, qw), lambda i: (i, 0)), pl.BlockSpec((ts, kw), lambda i: (i, 0)),
                   pl.BlockSpec((kw, ts), lambda i: (0, i)), pl.BlockSpec((ts, kw), lambda i: (i, 0)),
                   pl.BlockSpec((N_KV * LANES, ts), lambda i: (0, i)), *([ANY] * nt)),
        scratch_shapes=plan.scratch, compiler_params=_cp(("arbitrary",)), name=name)(qkv, qg2, kg2, *tabs, *sends)


def _qk_prep_bwd(qkv, dq, dk, dv, qg2, kg2, tabs, plan, sends, name):
    seq = qkv.shape[0]
    ts = _pick(seq, (256, 128))
    nq, nkp = N_HEADS // 2, N_KV // 2
    qw, kw = N_HEADS * HEAD_DIM, N_KV * HEAD_DIM
    nt, nsteps = plan.nt, seq // ts

    def body(x_ref, dq_ref, dk_ref, dv_ref, qg_ref, kg_ref, c_ref, sp_ref, sm_ref, *rest):
        o_ref, dqg_ref, dkg_ref = rest[nt:nt + 3]
        plan.bind(rest[:nt], rest[nt + 3:2 * nt + 3], *rest[2 * nt + 3:])
        step = pl.program_id(0)

        @pl.when(step == 0)
        def _():
            plan.start()

        lo = lax.broadcasted_iota(jnp.int32, (ts, LANES), 1) < HEAD_DIM
        c, sp, sm = c_ref[...], sp_ref[...], sm_ref[...]

        def one(xv, dz, gain):
            r = _pair_norm(xv, lo)
            nv = xv * r
            dy = _rope_t(dz, c, sp, sm)
            dgp = jnp.sum(dy * nv, axis=0, keepdims=True)
            dn = dy * gain
            t = dn * nv
            m_lo = jnp.sum(jnp.where(lo, t, 0.0), axis=1, keepdims=True)
            m_hi = jnp.sum(jnp.where(lo, 0.0, t), axis=1, keepdims=True)
            m = jnp.where(lo, m_lo, m_hi) * (1.0 / HEAD_DIM)
            return r * (dn - nv * m), dgp

        dqg = jnp.zeros((1, LANES), F32)
        for i in range(nq):
            sl = slice(i * LANES, (i + 1) * LANES)
            dx, dgp = one(x_ref[:, sl], dq_ref[:, sl] * (HEAD_DIM ** -0.5), qg_ref[...])
            o_ref[:, sl] = dx.astype(BF16)
            dqg = dqg + dgp
        dkg = jnp.zeros((1, LANES), F32)
        for i in range(nkp):
            sl = slice(i * LANES, (i + 1) * LANES)
            dx, dgp = one(x_ref[:, qw + i * LANES:qw + (i + 1) * LANES], dk_ref[:, sl], kg_ref[...])
            o_ref[:, qw + i * LANES:qw + (i + 1) * LANES] = dx.astype(BF16)
            dkg = dkg + dgp
            o_ref[:, qw + kw + i * LANES:qw + kw + (i + 1) * LANES] = dv_ref[:, sl].astype(BF16)

        @pl.when(step == 0)
        def _():
            dqg_ref[...] = dqg
            dkg_ref[...] = dkg

        @pl.when(step > 0)
        def _():
            dqg_ref[...] += dqg
            dkg_ref[...] += dkg

        @pl.when(step == nsteps - 1)
        def _():
            plan.finish()

    tab = pl.BlockSpec((ts, LANES), lambda i: (i, 0))
    vec = pl.BlockSpec((1, LANES), lambda i: (0, 0))
    return pl.pallas_call(
        body,
        out_shape=(jax.ShapeDtypeStruct((seq, qw + 2 * kw), BF16), jax.ShapeDtypeStruct((1, LANES), F32),
                   jax.ShapeDtypeStruct((1, LANES), F32), *plan.out_shape),
        grid=(nsteps,),
        in_specs=[pl.BlockSpec((ts, qw + 2 * kw), lambda i: (i, 0)), pl.BlockSpec((ts, qw), lambda i: (i, 0)),
                  pl.BlockSpec((ts, kw), lambda i: (i, 0)), pl.BlockSpec((ts, kw), lambda i: (i, 0)),
                  vec, vec, tab, tab, tab] + [ANY] * nt,
        out_specs=(pl.BlockSpec((ts, qw + 2 * kw), lambda i: (i, 0)), vec, vec, *([ANY] * nt)),
        scratch_shapes=plan.scratch, compiler_params=_cp(("arbitrary",)), name=name)(qkv, dq, dk, dv, qg2, kg2, *tabs, *sends)


def _slot(blk, off0, tq):
    half = lax.broadcasted_iota(jnp.int32, (tq, LANES), 1) // HEAD_DIM
    keep = half == jnp.where(off0, 0, 1)
    parts = []
    for p in range(2):
        pair = blk[:, p * LANES:(p + 1) * LANES].astype(F32)
        rolled = pltpu.roll(pair, HEAD_DIM, axis=1)
        parts.append(jnp.where(keep, jnp.where(off0, pair, rolled), 0.0))
        parts.append(jnp.where(keep, jnp.where(off0, rolled, pair), 0.0))
    return jnp.concatenate(parts, axis=0)


def _unslot(x4, off0, tq):
    lo = lax.broadcasted_iota(jnp.int32, (tq, LANES), 1) < HEAD_DIM
    pairs = []
    for p in range(2):
        h0 = x4[(2 * p) * tq:(2 * p + 1) * tq]
        h1 = x4[(2 * p + 1) * tq:(2 * p + 2) * tq]
        a = jnp.where(off0, h0, pltpu.roll(h0, HEAD_DIM, axis=1))
        b = jnp.where(off0, pltpu.roll(h1, HEAD_DIM, axis=1), h1)
        pairs.append(jnp.where(lo, a, b))
    return jnp.concatenate(pairs, axis=1)


def _flash_fwd(q, k, vt, plan, shards, name):
    seq = q.shape[0]
    tq = _pick(seq, (FLASH_TQ, 128))
    tk = _pick(seq, (FLASH_TK, 2048, 512, 256, 128))
    sub = _pick(tk, (FLASH_SUB, 256, 128))
    nq, nkv, nsub = seq // tq, seq // tk, tk // sub
    gw = 4 * HEAD_DIM
    nt = plan.nt

    def body(q_ref, k_ref, vt_ref, *rest):
        o_ref, lse_ref = rest[nt:nt + 2]
        q4_ref, m_ref, acc_ref, st_ref = rest[2 * nt + 2:2 * nt + 6]
        plan.bind(rest[:nt], rest[nt + 2:2 * nt + 2], *rest[2 * nt + 6:])
        g, qi, ki = pl.program_id(0), pl.program_id(1), pl.program_id(2)
        off0 = (g % 2) == 0
        @pl.when(jnp.logical_and(g == 0, jnp.logical_and(qi == 0, ki == 0)))
        def _():
            plan.start()

        @pl.when(jnp.logical_and(g == N_KV - 1, jnp.logical_and(qi == nq - 1, ki == 0)))
        def _():
            plan.forward()

        @pl.when(ki == 0)
        def _():
            q4_ref[...] = _slot(q_ref[...], off0, tq).astype(BF16)
            m_ref[...] = jnp.full(m_ref.shape, NEG, F32)
            acc_ref[...] = jnp.zeros(acc_ref.shape, F32)

        q4 = q4_ref[...]

        def scores(c):
            st_ref[c % 2] = lax.dot_general(k_ref[c * sub:(c + 1) * sub, :], q4, (((1,), (1,)), ((), ())),
                                            preferred_element_type=F32)

        m, acc = m_ref[...], acc_ref[...]
        scores(0)
        for c in range(nsub):
            if c + 1 < nsub:
                scores(c + 1)
            st = st_ref[c % 2]
            m_new = jnp.maximum(m, jnp.max(st, axis=0, keepdims=True))
            pt = jnp.exp2(st - m_new).astype(BF16)
            acc = jnp.exp2(m - m_new) * acc + jnp.dot(vt_ref[:, c * sub:(c + 1) * sub], pt, preferred_element_type=F32)
            m = m_new
        m_ref[...] = m
        acc_ref[...] = acc

        @pl.when(ki == nkv - 1)
        def _():
            acc = acc_ref[...]
            l = jnp.where(off0, acc[HEAD_DIM:HEAD_DIM + 1], acc[0:1])
            o4 = acc.T
            o4 = o4 / pltpu.roll(o4, HEAD_DIM, axis=1)
            o_ref[...] = _unslot(o4, off0, tq).astype(o_ref.dtype)
            lse_ref[...] = jnp.broadcast_to(m_ref[...] + jnp.log2(l), lse_ref.shape)

        @pl.when(jnp.logical_and(g == N_KV - 1, jnp.logical_and(qi == nq - 1, ki == nkv - 1)))
        def _():
            plan.finish()

    return pl.pallas_call(
        body,
        out_shape=(jax.ShapeDtypeStruct((seq, N_HEADS * HEAD_DIM), BF16),
                   jax.ShapeDtypeStruct((N_KV * nq * 8, 4 * tq), F32), *plan.out_shape),
        grid=(N_KV, nq, nkv),
        in_specs=[pl.BlockSpec((tq, gw), lambda g, qi, ki: (qi, g)),
                  pl.BlockSpec((tk, LANES), lambda g, qi, ki: (ki, g // 2)),
                  pl.BlockSpec((LANES, tk), lambda g, qi, ki: (g, ki))] + [ANY] * nt,
        out_specs=(pl.BlockSpec((tq, gw), lambda g, qi, ki: (qi, g)),
                   pl.BlockSpec((8, 4 * tq), lambda g, qi, ki: (g * nq + qi, 0)), *([ANY] * nt)),
        scratch_shapes=[pltpu.VMEM((4 * tq, LANES), BF16), pltpu.VMEM((1, 4 * tq), F32),
                        pltpu.VMEM((LANES, 4 * tq), F32), pltpu.VMEM((2, sub, 4 * tq), F32)] + plan.scratch,
        compiler_params=_cp(("arbitrary", "arbitrary", "arbitrary")), name=name)(q, k, vt, *shards)


def _flash_bwd(q, k, kt, v, do, o, lse, plan, grads, name):
    seq = q.shape[0]
    tq = _pick(seq, (FLASH_TQ, 128))
    tk = _pick(seq, (FLASH_TK, 2048, 512, 256, 128))
    sub = _pick(tk, (FLASH_SUB, 256, 128))
    nq, nkv, nsub = seq // tq, seq // tk, tk // sub
    gw = 4 * HEAD_DIM
    nt = plan.nt

    def body(q_ref, k_ref, kt_ref, v_ref, do_ref, o_ref, lse_ref, *rest):
        dq_ref, dk_ref, dv_ref = rest[nt:nt + 3]
        q4_ref, do4_ref, delta_ref, dqt_ref, st_ref, dpt_ref = rest[2 * nt + 3:2 * nt + 9]
        plan.bind(rest[:nt], rest[nt + 3:2 * nt + 3], *rest[2 * nt + 9:])
        g, qi, ki = pl.program_id(0), pl.program_id(1), pl.program_id(2)
        off0 = (g % 2) == 0

        @pl.when(jnp.logical_and(g == 0, jnp.logical_and(qi == 0, ki == 0)))
        def _():
            plan.start()

        @pl.when(jnp.logical_and(g % 2 == 0, jnp.logical_and(qi == 0, ki == 0)))
        def _():
            dk_ref[...] = jnp.zeros(dk_ref.shape, F32)
            dv_ref[...] = jnp.zeros(dv_ref.shape, F32)

        @pl.when(ki == 0)
        def _():
            q4_ref[...] = _slot(q_ref[...], off0, tq).astype(BF16)
            do4 = _slot(do_ref[...], off0, tq)
            do4_ref[...] = do4.astype(BF16)
            o4 = _slot(o_ref[...], off0, tq)
            delta_ref[...] = jnp.sum((do4 * o4).T, axis=0, keepdims=True)
            dqt_ref[...] = jnp.zeros(dqt_ref.shape, F32)

        q4, do4 = q4_ref[...], do4_ref[...]
        lse_row, delta = lse_ref[0:1, :], delta_ref[...]

        def products(c):
            rows = slice(c * sub, (c + 1) * sub)
            st_ref[c % 2] = lax.dot_general(k_ref[rows, :], q4, (((1,), (1,)), ((), ())), preferred_element_type=F32)
            dpt_ref[c % 2] = lax.dot_general(v_ref[rows, :], do4, (((1,), (1,)), ((), ())), preferred_element_type=F32)

        dqt = dqt_ref[...]
        products(0)
        for c in range(nsub):
            if c + 1 < nsub:
                products(c + 1)
            pt = jnp.exp2(st_ref[c % 2] - lse_row)
            dst = (pt * (dpt_ref[c % 2] - delta)).astype(BF16)
            rows = pl.ds(pl.multiple_of(ki * tk + c * sub, sub), sub)
            dv_ref[rows, :] += jnp.dot(pt.astype(BF16), do4, preferred_element_type=F32)
            dk_ref[rows, :] += jnp.dot(dst, q4, preferred_element_type=F32) * (1.0 / LOG2E)
            dqt = dqt + jnp.dot(kt_ref[:, c * sub:(c + 1) * sub], dst, preferred_element_type=F32)
        dqt_ref[...] = dqt

        @pl.when(ki == nkv - 1)
        def _():
            dq_ref[...] = _unslot(dqt_ref[...].T, off0, tq)

        @pl.when(jnp.logical_and(g == N_KV - 1, jnp.logical_and(qi == nq - 1, ki == nkv - 1)))
        def _():
            plan.finish()

    return pl.pallas_call(
        body,
        out_shape=(jax.ShapeDtypeStruct((seq, N_HEADS * HEAD_DIM), F32),
                   jax.ShapeDtypeStruct((seq, N_KV * HEAD_DIM), F32), jax.ShapeDtypeStruct((seq, N_KV * HEAD_DIM), F32),
                   *plan.out_shape),
        grid=(N_KV, nq, nkv),
        in_specs=[pl.BlockSpec((tq, gw), lambda g, qi, ki: (qi, g)),
                  pl.BlockSpec((tk, LANES), lambda g, qi, ki: (ki, g // 2)),
                  pl.BlockSpec((LANES, tk), lambda g, qi, ki: (g // 2, ki)),
                  pl.BlockSpec((tk, LANES), lambda g, qi, ki: (ki, g // 2)),
                  pl.BlockSpec((tq, gw), lambda g, qi, ki: (qi, g)),
                  pl.BlockSpec((tq, gw), lambda g, qi, ki: (qi, g)),
                  pl.BlockSpec((8, 4 * tq), lambda g, qi, ki: (g * nq + qi, 0))] + [ANY] * nt,
        out_specs=(pl.BlockSpec((tq, gw), lambda g, qi, ki: (qi, g)),
                   pl.BlockSpec((seq, LANES), lambda g, qi, ki: (0, g // 2)),
                   pl.BlockSpec((seq, LANES), lambda g, qi, ki: (0, g // 2)), *([ANY] * nt)),
        scratch_shapes=[pltpu.VMEM((4 * tq, LANES), BF16), pltpu.VMEM((4 * tq, LANES), BF16),
                        pltpu.VMEM((1, 4 * tq), F32), pltpu.VMEM((LANES, 4 * tq), F32),
                        pltpu.VMEM((2, sub, 4 * tq), F32), pltpu.VMEM((2, sub, 4 * tq), F32)] + plan.scratch,
        compiler_params=_cp(("arbitrary", "arbitrary", "arbitrary")), name=name)(q, k, kt, v, do, o, lse, *grads)


def _xattn_fwd(q, kv, name):
    seq, d = q.shape
    mlen = kv.shape[0]
    tq = _pick(seq, (1024, 512, 256))

    def body(q_ref, k_ref, v_ref, o_ref):
        for h in range(X_HEADS):
            sl = slice(h * X_HEAD_DIM, (h + 1) * X_HEAD_DIM)
            s = lax.dot_general(q_ref[:, sl], k_ref[:, sl], (((1,), (1,)), ((), ())), preferred_element_type=F32)
            e = jnp.exp(s - jnp.max(s, axis=-1, keepdims=True))
            p = e / jnp.sum(e, axis=-1, keepdims=True)
            o_ref[:, sl] = jnp.dot(p.astype(BF16), v_ref[:, sl], preferred_element_type=F32).astype(o_ref.dtype)

    return pl.pallas_call(
        body, out_shape=jax.ShapeDtypeStruct((seq, d), BF16), grid=(seq // tq,),
        in_specs=[pl.BlockSpec((tq, d), lambda i: (i, 0)), pl.BlockSpec((mlen, d), lambda i: (0, 0)),
                  pl.BlockSpec((mlen, d), lambda i: (0, 1))],
        out_specs=pl.BlockSpec((tq, d), lambda i: (i, 0)), compiler_params=_cp(("parallel",)), name=name)(q, kv, kv)


def _xattn_bwd(q, kv, do, name):
    seq, d = q.shape
    mlen = kv.shape[0]
    tq = _pick(seq, (1024, 512, 256))
    scale = X_HEAD_DIM ** -0.5

    def body(q_ref, k_ref, v_ref, do_ref, dq_ref, dkv_ref):
        i = pl.program_id(0)

        @pl.when(i == 0)
        def _():
            dkv_ref[...] = jnp.zeros(dkv_ref.shape, F32)

        for h in range(X_HEADS):
            sl = slice(h * X_HEAD_DIM, (h + 1) * X_HEAD_DIM)
            qh, kh, vh = q_ref[:, sl], k_ref[:, sl], v_ref[:, sl]
            doh = do_ref[:, sl].astype(BF16)
            st = lax.dot_general(kh, qh, (((1,), (1,)), ((), ())), preferred_element_type=F32)
            e = jnp.exp(st - jnp.max(st, axis=0, keepdims=True))
            pt = e / jnp.sum(e, axis=0, keepdims=True)
            dpt = lax.dot_general(vh, doh, (((1,), (1,)), ((), ())), preferred_element_type=F32)
            dst = (pt * (dpt - jnp.sum(pt * dpt, axis=0, keepdims=True))).astype(BF16)
            dkv_ref[:, sl] += jnp.dot(dst, qh, preferred_element_type=F32)
            dkv_ref[:, d + h * X_HEAD_DIM:d + (h + 1) * X_HEAD_DIM] += jnp.dot(pt.astype(BF16), doh,
                                                                                 preferred_element_type=F32)
            dqh = lax.dot_general(dst, kh, (((0,), (0,)), ((), ())), preferred_element_type=F32)
            dq_ref[:, sl] = (dqh * scale).astype(dq_ref.dtype)

    return pl.pallas_call(
        body, out_shape=(jax.ShapeDtypeStruct((seq, d), BF16), jax.ShapeDtypeStruct((mlen, 2 * d), F32)),
        grid=(seq // tq,),
        in_specs=[pl.BlockSpec((tq, d), lambda i: (i, 0)), pl.BlockSpec((mlen, d), lambda i: (0, 0)),
                  pl.BlockSpec((mlen, d), lambda i: (0, 1)), pl.BlockSpec((tq, d), lambda i: (i, 0))],
        out_specs=(pl.BlockSpec((tq, d), lambda i: (i, 0)), pl.BlockSpec((mlen, 2 * d), lambda i: (0, 0))),
        compiler_params=_cp(("arbitrary",)), name=name)(q, kv, kv, do)


def _halo_specs(tr, tc, seq, col):
    per, last = tr // HALO, seq // HALO - 1
    return [pl.BlockSpec((tr, tc), lambda j, r: (r, col(j))),
            pl.BlockSpec((HALO, tc), lambda j, r: (jnp.maximum(r * per - 1, 0), col(j))),
            pl.BlockSpec((HALO, tc), lambda j, r: (jnp.minimum((r + 1) * per, last), col(j)))]


def _extend(main_ref, prev_ref, next_ref, r, nr):
    pv = (r > 0).astype(F32)
    nv = (r < nr - 1).astype(F32)
    return jnp.concatenate([prev_ref[...].astype(F32) * pv, main_ref[...].astype(F32),
                            next_ref[...].astype(F32) * nv], axis=0)


def _conv3(e, w_ref, n):
    return pltpu.roll(e, 1, axis=0) * w_ref[0:1, :] + e * w_ref[1:2, :] + pltpu.roll(e, n - 1, axis=0) * w_ref[2:3, :]


def _conv_gate_fwd(ug, uv, cw, cb, layer, name, plan=None, shards=(), fulls=()):
    seq, f = ug.shape
    tc = 256
    tr = _pick(seq, (HALO_TR, 512, 256))
    nc, nr = f // tc, seq // tr
    n = tr + 2 * HALO
    nt = plan.nt if plan is not None else 0

    def body(g_ref, gp_ref, gn_ref, v_ref, vp_ref, vn_ref, wg_ref, wv_ref, bg_ref, bv_ref, *rest):
        o_ref = rest[2 * nt]
        j, r = pl.program_id(0), pl.program_id(1)
        if plan is not None:
            plan.bind(rest[:nt], rest[2 * nt + 1:3 * nt + 1], *rest[3 * nt + 1:])

            @pl.when(jnp.logical_and(j == 0, r == 0))
            def _():
                plan.start()

        cg = _conv3(_extend(g_ref, gp_ref, gn_ref, r, nr), wg_ref, n)[HALO:HALO + tr] + bg_ref[...]
        cv = _conv3(_extend(v_ref, vp_ref, vn_ref, r, nr), wv_ref, n)[HALO:HALO + tr] + bv_ref[...]
        o_ref[...] = (cg * jax.nn.sigmoid(cg) * cv).astype(o_ref.dtype)

        if plan is not None:
            @pl.when(jnp.logical_and(j == nc - 1, r == nr - 1))
            def _():
                plan.forward()
                plan.finish()

    w_spec = lambda shift: pl.BlockSpec((None, 3, tc), lambda j, r: (layer, 0, j + shift))
    b_spec = lambda shift: pl.BlockSpec((None, 1, tc), lambda j, r: (layer, 0, j + shift))
    act_shape = jax.ShapeDtypeStruct((seq, f), BF16)
    act_spec = pl.BlockSpec((tr, tc), lambda j, r: (r, j))
    in_specs = _halo_specs(tr, tc, seq, lambda j: j) * 2 + [w_spec(0), w_spec(nc), b_spec(0), b_spec(nc)]
    operands = (ug, ug, ug, uv, uv, uv, cw, cw, cb, cb)
    if plan is None:
        return pl.pallas_call(body, out_shape=act_shape, grid=(nc, nr), in_specs=in_specs, out_specs=act_spec,
                              compiler_params=_cp(("parallel", "parallel")), name=name)(*operands)
    return pl.pallas_call(
        body, out_shape=(act_shape, *plan.out_shape), grid=(nc, nr), in_specs=in_specs + [ANY] * (2 * nt),
        out_specs=(act_spec, *([ANY] * nt)), scratch_shapes=plan.scratch,
        input_output_aliases={len(operands) + nt + t: 1 + t for t in range(nt)},
        compiler_params=_cp(("arbitrary", "arbitrary")), name=name)(*operands, *shards, *fulls)


def _conv_gate_bwd(ug, uv, dact, cw, cb, layer, name):
    seq, f = ug.shape
    tc = 256
    tr = _pick(seq, (512, 256))
    nc, nr = f // tc, seq // tr
    n = tr + 2 * HALO

    def body(g_ref, gp_ref, gn_ref, v_ref, vp_ref, vn_ref, d_ref, dp_ref, dn_ref, wg_ref, wv_ref, bg_ref, bv_ref,
             dug_ref, duv_ref, dwg_ref, dwv_ref):
        r = pl.program_id(1)
        eg = _extend(g_ref, gp_ref, gn_ref, r, nr)
        ev = _extend(v_ref, vp_ref, vn_ref, r, nr)
        da = _extend(d_ref, dp_ref, dn_ref, r, nr)
        eg3 = (pltpu.roll(eg, 1, axis=0), eg, pltpu.roll(eg, n - 1, axis=0))
        ev3 = (pltpu.roll(ev, 1, axis=0), ev, pltpu.roll(ev, n - 1, axis=0))
        cg = eg3[0] * wg_ref[0:1, :] + eg3[1] * wg_ref[1:2, :] + eg3[2] * wg_ref[2:3, :] + bg_ref[...]
        cv = ev3[0] * wv_ref[0:1, :] + ev3[1] * wv_ref[1:2, :] + ev3[2] * wv_ref[2:3, :] + bv_ref[...]
        sg = jax.nn.sigmoid(cg)
        dcv = da * (cg * sg)
        dcg = da * cv * (sg * (1.0 + cg * (1.0 - sg)))

        def back(dc, e3, w_ref, du_ref, dw_ref):
            du = (pltpu.roll(dc, n - 1, axis=0) * w_ref[0:1, :] + dc * w_ref[1:2, :]
                  + pltpu.roll(dc, 1, axis=0) * w_ref[2:3, :])
            du_ref[...] = du[HALO:HALO + tr].astype(du_ref.dtype)
            dcm = dc[HALO:HALO + tr]
            taps = [jnp.sum(dcm * e[HALO:HALO + tr], axis=0, keepdims=True) for e in e3] + [
                    jnp.sum(dcm, axis=0, keepdims=True)]
            part = jnp.concatenate(taps + [jnp.zeros((4, tc), F32)], axis=0)

            @pl.when(r == 0)
            def _():
                dw_ref[...] = part

            @pl.when(r > 0)
            def _():
                dw_ref[...] += part

        back(dcg, eg3, wg_ref, dug_ref, dwg_ref)
        back(dcv, ev3, wv_ref, duv_ref, dwv_ref)

    w_spec = lambda shift: pl.BlockSpec((None, 3, tc), lambda j, r: (layer, 0, j + shift))
    b_spec = lambda shift: pl.BlockSpec((None, 1, tc), lambda j, r: (layer, 0, j + shift))
    out_rows = pl.BlockSpec((tr, tc), lambda j, r: (r, j))
    out_acc = pl.BlockSpec((8, tc), lambda j, r: (0, j))
    return pl.pallas_call(
        body,
        out_shape=(jax.ShapeDtypeStruct((seq, f), BF16), jax.ShapeDtypeStruct((seq, f), BF16),
                   jax.ShapeDtypeStruct((8, f), F32), jax.ShapeDtypeStruct((8, f), F32)),
        grid=(nc, nr),
        in_specs=_halo_specs(tr, tc, seq, lambda j: j) * 3 + [w_spec(0), w_spec(nc), b_spec(0), b_spec(nc)],
        out_specs=(out_rows, out_rows, out_acc, out_acc),
        compiler_params=_cp(("parallel", "arbitrary")), name=name)(ug, ug, ug, uv, uv, uv, dact, dact, dact, cw, cw, cb, cb)


def _pool_count(g, r, tr, n, seq):
    half = jnp.left_shift(1, g)
    t = r * tr - HALO + lax.broadcasted_iota(jnp.int32, (n, 1), 0)
    cnt = jnp.minimum(t + half, seq) - jnp.maximum(t - half, 0)
    return jnp.maximum(cnt, 1).astype(F32)


def _by_group(g, levels):
    out = levels[3]
    for i in (2, 1, 0):
        out = jnp.where(g == i, levels[i], out)
    return out


def _pool_mixed(e, g, cnt, n):
    w2 = e + pltpu.roll(e, 1, axis=0)
    w4 = pltpu.roll(w2, 1, axis=0) + pltpu.roll(w2, n - 1, axis=0)
    w8 = pltpu.roll(w4, 2, axis=0) + pltpu.roll(w4, n - 2, axis=0)
    w16 = pltpu.roll(w8, 4, axis=0) + pltpu.roll(w8, n - 4, axis=0)
    return _by_group(g, (w2, w4, w8, w16)) / cnt - e


def _pool_fwd(hp, xres, pw, scale, name, plan, shards, fulls):
    seq, d = hp.shape
    tc = POOL_GROUP_W
    tr = _pick(seq, (HALO_TR, 512, 256))
    nr = seq // tr
    n = tr + 2 * HALO
    nt = plan.nt

    def body(h_ref, hp_ref, hn_ref, x_ref, w_ref, s_ref, *rest):
        o_ref = rest[2 * nt]
        plan.bind(rest[:nt], rest[2 * nt + 1:3 * nt + 1], *rest[3 * nt + 1:])
        g, r = pl.program_id(0), pl.program_id(1)

        def compute():
            e = _extend(h_ref, hp_ref, hn_ref, r, nr)
            mixed = _pool_mixed(e, g, _pool_count(g, r, tr, n, seq), n)[HALO:HALO + tr]
            y = jnp.dot(mixed.astype(BF16), w_ref[...], preferred_element_type=F32)
            o_ref[...] = x_ref[...] + y * s_ref[...]

        _hosted(plan, POOL_GROUPS * nr, g * nr + r, compute)

    operands = (hp, hp, hp, xres, pw, scale)
    return pl.pallas_call(
        body, out_shape=(jax.ShapeDtypeStruct((seq, d), F32), *plan.out_shape), grid=(POOL_GROUPS, nr),
        in_specs=_halo_specs(tr, tc, seq, lambda j: j) + [
            pl.BlockSpec((tr, tc), lambda j, r: (r, j)), pl.BlockSpec((None, tc, tc), lambda j, r: (j, 0, 0)),
            pl.BlockSpec((1, tc), lambda j, r: (0, j))] + [ANY] * (2 * nt),
        out_specs=(pl.BlockSpec((tr, tc), lambda j, r: (r, j)), *([ANY] * nt)), scratch_shapes=plan.scratch,
        input_output_aliases={len(operands) + nt + t: 1 + t for t in range(nt)},
        compiler_params=_cp(("arbitrary", "arbitrary")), name=name)(*operands, *shards, *fulls)


def _pool_bwd(hp, dy, pw, scale, name):
    seq, d = hp.shape
    tc = POOL_GROUP_W
    tr = _pick(seq, (HALO_TR, 512, 256))
    nr = seq // tr
    n = tr + 2 * HALO

    def body(h_ref, hp_ref, hn_ref, d_ref, dp_ref, dn_ref, w_ref, s_ref, dh_ref, dw_ref, ds_ref):
        g, r = pl.program_id(0), pl.program_id(1)
        cnt = _pool_count(g, r, tr, n, seq)
        e = _extend(h_ref, hp_ref, hn_ref, r, nr)
        mixed = _pool_mixed(e, g, cnt, n)[HALO:HALO + tr].astype(BF16)
        dye = _extend(d_ref, dp_ref, dn_ref, r, nr)
        dyp = (dye * s_ref[...]).astype(BF16)
        dmixed = lax.dot_general(dyp, w_ref[...], (((1,), (1,)), ((), ())), preferred_element_type=F32)
        dwin = dmixed / cnt
        m2 = dwin + pltpu.roll(dwin, n - 1, axis=0)
        m4 = pltpu.roll(m2, 1, axis=0) + pltpu.roll(m2, n - 1, axis=0)
        m8 = pltpu.roll(m4, 2, axis=0) + pltpu.roll(m4, n - 2, axis=0)
        m16 = pltpu.roll(m8, 4, axis=0) + pltpu.roll(m8, n - 4, axis=0)
        dh_ref[...] = (_by_group(g, (m2, m4, m8, m16)) - dmixed)[HALO:HALO + tr]
        ypre = jnp.dot(mixed, w_ref[...], preferred_element_type=F32)
        dsp = jnp.sum(d_ref[...] * ypre, axis=0, keepdims=True)
        dwp = lax.dot_general(mixed, dyp[HALO:HALO + tr], (((0,), (0,)), ((), ())), preferred_element_type=F32)

        @pl.when(r == 0)
        def _():
            dw_ref[...] = dwp
            ds_ref[...] = dsp

        @pl.when(r > 0)
        def _():
            dw_ref[...] += dwp
            ds_ref[...] += dsp

    return pl.pallas_call(
        body,
        out_shape=(jax.ShapeDtypeStruct((seq, d), F32), jax.ShapeDtypeStruct((POOL_GROUPS, tc, tc), F32),
                   jax.ShapeDtypeStruct((1, d), F32)),
        grid=(POOL_GROUPS, nr),
        in_specs=_halo_specs(tr, tc, seq, lambda j: j) * 2 + [
            pl.BlockSpec((None, tc, tc), lambda j, r: (j, 0, 0)), pl.BlockSpec((1, tc), lambda j, r: (0, j))],
        out_specs=(pl.BlockSpec((tr, tc), lambda j, r: (r, j)), pl.BlockSpec((None, tc, tc), lambda j, r: (j, 0, 0)),
                   pl.BlockSpec((1, tc), lambda j, r: (0, j))),
        compiler_params=_cp(("parallel", "arbitrary")), name=name)(hp, hp, hp, dy, dy, dy, pw, scale)


def _adamw_math(w, g, m, v):
    m = ADAM_B1 * m + (1.0 - ADAM_B1) * g
    v = ADAM_B2 * v + (1.0 - ADAM_B2) * (g * g)
    m_hat = m / (1.0 - ADAM_B1 ** ADAM_STEP)
    v_hat = v / (1.0 - ADAM_B2 ** ADAM_STEP)
    delta = -ADAM_LR * (m_hat / (jnp.sqrt(v_hat) + ADAM_EPS) + ADAM_WD * w)
    return delta, m, v


def _adamw(w, ga, gb, m, v, name):
    rows, cols = w.shape
    tr = _pick(rows, (256, 128, 64, 32, 16, 8))
    two = gb is not None

    def body(*refs):
        if two:
            w_ref, ga_ref, gb_ref, m_ref, v_ref, g_out, d_out, m_out, v_out = refs
            g = ga_ref[...] + gb_ref[...]
        else:
            w_ref, ga_ref, m_ref, v_ref, g_out, d_out, m_out, v_out = refs
            g = ga_ref[...]
        delta, m, v = _adamw_math(w_ref[...], g, m_ref[...], v_ref[...])
        g_out[...] = g
        d_out[...] = delta
        m_out[...] = m
        v_out[...] = v

    spec = pl.BlockSpec((tr, cols), lambda i: (i, 0))
    ops = [w, ga] + ([gb] if two else []) + [m, v]
    return pl.pallas_call(
        body, out_shape=tuple(jax.ShapeDtypeStruct((rows, cols), F32) for _ in range(4)), grid=(rows // tr,),
        in_specs=[spec] * len(ops), out_specs=(spec,) * 4, compiler_params=_cp(("parallel",)), name=name)(*ops)


def _sum4(parts, name):
    _, rows, cols = parts.shape
    tr = _pick(rows, (256, 128, 64, 32, 16))

    def body(p_ref, o_ref):
        acc = p_ref[0].astype(F32)
        for kk in range(1, 4):
            acc = acc + p_ref[kk].astype(F32)
        o_ref[...] = acc

    return pl.pallas_call(
        body, out_shape=jax.ShapeDtypeStruct((rows, cols), F32), grid=(rows // tr,),
        in_specs=[pl.BlockSpec((4, tr, cols), lambda i: (0, i, 0))], out_specs=pl.BlockSpec((tr, cols), lambda i: (i, 0)),
        compiler_params=_cp(("parallel",)), name=name)(parts)


def _place():
    x, y, c = lax.axis_index("x"), lax.axis_index("y"), lax.axis_index("c")
    chips = [(1 - x, y), (x, 1 - y), (1 - x, 1 - y)]
    return x, y, c, chips


def _window(ref, axis, j, size, c=None, half=None, lead=(), layers=slice(None)):
    if axis == "r":
        if c is None:
            return ref.at[lead + (layers, pl.ds(pl.multiple_of(j * size, 32), size), slice(None))]
        return ref.at[lead + (layers, pl.ds(pl.multiple_of(j * size + c * half, 32), half), slice(None))]
    cols = pl.ds(pl.multiple_of(j * size, LANES), size)
    if c is None:
        return ref.at[lead + (layers, slice(None), cols)]
    return ref.at[lead + (layers, pl.ds(pl.multiple_of(c * half, 32), half), cols)]


class _Gather:
    def __init__(self, shards, axes, layers=None):
        self.nt, self.axes = len(shards), axes
        self.layers = layers or [slice(None)] * self.nt
        self.out_shape, self.sizes, self.halves = [], [], []
        for s, ax in zip(shards, axes):
            l, rs, cs = s.shape
            self.out_shape.append(jax.ShapeDtypeStruct((l, 4 * rs, cs) if ax == "r" else (l, rs, 4 * cs), s.dtype))
            self.sizes.append(rs if ax == "r" else cs)
            self.halves.append(rs // 2)
        self.scratch = [pltpu.SemaphoreType.DMA((6 * self.nt,)), pltpu.SemaphoreType.DMA((6 * self.nt,)),
                        pltpu.SemaphoreType.DMA((self.nt,))]

    def bind(self, src, dst, send_sems, recv_sems, local_sems):
        self.src, self.dst, self.send_sems, self.recv_sems, self.local_sems = src, dst, send_sems, recv_sems, local_sems

    def _win(self, t, j, core=None):
        return _window(self.dst[t], self.axes[t], j, self.sizes[t], core, self.halves[t], layers=self.layers[t])

    def _ici(self, t, kk, origin):
        _, _, c, chips = _place()
        px, py = chips[kk]
        half = self.src[t].at[self.layers[t], pl.ds(pl.multiple_of(c * self.halves[t], 16), self.halves[t]), :]
        return pltpu.make_async_remote_copy(
            src_ref=half, dst_ref=self._win(t, origin, c), send_sem=self.send_sems.at[t * 3 + kk],
            recv_sem=self.recv_sems.at[t * 3 + kk], device_id=(px, py, c), device_id_type=MESH)

    def _d2d(self, t, kk, origin, core):
        x, y, c, _ = _place()
        k2 = 3 * self.nt + t * 3 + kk
        return pltpu.make_async_remote_copy(
            src_ref=self._win(t, origin, core), dst_ref=self._win(t, origin, core), send_sem=self.send_sems.at[k2],
            recv_sem=self.recv_sems.at[k2], device_id=(x, y, 1 - c), device_id_type=MESH)

    def _local(self, t):
        x, y, _, _ = _place()
        return pltpu.make_async_copy(self.src[t].at[self.layers[t]], self._win(t, 2 * x + y), self.local_sems.at[t])

    def _each(self):
        _, _, _, chips = _place()
        for t in range(self.nt):
            for kk in range(3):
                px, py = chips[kk]
                yield t, kk, 2 * px + py

    def start(self):
        x, y, _, _ = _place()
        for t in range(self.nt):
            self._local(t).start()
        for t, kk, _ in self._each():
            self._ici(t, kk, 2 * x + y).start()

    def forward(self):
        _, _, c, _ = _place()
        for t, kk, origin in self._each():
            self._ici(t, kk, origin).wait_recv()
            self._d2d(t, kk, origin, c).start()

    def finish(self):
        x, y, c, _ = _place()
        for t, kk, origin in self._each():
            self._d2d(t, kk, origin, 1 - c).wait_recv()
        for t, kk, origin in self._each():
            self._ici(t, kk, 2 * x + y).wait_send()
            self._d2d(t, kk, origin, c).wait_send()
        for t in range(self.nt):
            self._local(t).wait()


class _Scatter:
    def __init__(self, grads, axes):
        self.nt, self.axes = len(grads), axes
        self.out_shape, self.sizes = [], []
        for gr, ax in zip(grads, axes):
            l, r, cc = gr.shape
            self.out_shape.append(jax.ShapeDtypeStruct((4, l, r // 4, cc) if ax == "r" else (4, l, r, cc // 4), gr.dtype))
            self.sizes.append(r // 4 if ax == "r" else cc // 4)
        self.scratch = [pltpu.SemaphoreType.DMA((3 * self.nt,)), pltpu.SemaphoreType.DMA((3 * self.nt,)),
                        pltpu.SemaphoreType.DMA((self.nt,))]

    def bind(self, src, dst, send_sems, recv_sems, local_sems):
        self.src, self.dst, self.send_sems, self.recv_sems, self.local_sems = src, dst, send_sems, recv_sems, local_sems

    def _copy(self, t, kk, slot):
        x, y, c, chips = _place()
        px, py = chips[kk]
        return pltpu.make_async_remote_copy(
            src_ref=_window(self.src[t], self.axes[t], 2 * px + py, self.sizes[t]), dst_ref=self.dst[t].at[slot],
            send_sem=self.send_sems.at[t * 3 + kk], recv_sem=self.recv_sems.at[t * 3 + kk],
            device_id=(px, py, c), device_id_type=MESH)

    def _local(self, t):
        x, y, _, _ = _place()
        me = 2 * x + y
        return pltpu.make_async_copy(_window(self.src[t], self.axes[t], me, self.sizes[t]), self.dst[t].at[me],
                                     self.local_sems.at[t])

    def start(self):
        x, y, _, _ = _place()
        for t in range(self.nt):
            self._local(t).start()
            for kk in range(3):
                self._copy(t, kk, 2 * x + y).start()

    def finish(self):
        _, _, _, chips = _place()
        for t in range(self.nt):
            for kk in range(3):
                px, py = chips[kk]
                self._copy(t, kk, 2 * px + py).wait_recv()
        for t in range(self.nt):
            for kk in range(3):
                px, py = chips[kk]
                self._copy(t, kk, 2 * px + py).wait_send()
            self._local(t).wait()


def _comm_call(plans, operands, name):
    nts = [p.nt for p in plans]
    n_in, n_sem = sum(nts), [len(p.scratch) for p in plans]

    def body(*refs):
        pos_in, pos_out, pos_sem = 0, n_in, 2 * n_in
        for p, nt, ns in zip(plans, nts, n_sem):
            p.bind(refs[pos_in:pos_in + nt], refs[pos_out:pos_out + nt], *refs[pos_sem:pos_sem + ns])
            pos_in, pos_out, pos_sem = pos_in + nt, pos_out + nt, pos_sem + ns
        for p in plans:
            p.start()
        for p in plans:
            if hasattr(p, "forward"):
                p.forward()
        for p in plans:
            p.finish()

    return pl.pallas_call(
        body, out_shape=tuple(s for p in plans for s in p.out_shape), in_specs=[ANY] * n_in,
        out_specs=tuple([ANY] * n_in), scratch_shapes=[s for p in plans for s in p.scratch],
        name=name)(*[a for ops in operands for a in ops])


class _GatherAll:
    FLIPS = [f for f in itertools.product((0, 1), repeat=3) if any(f)]

    def __init__(self, pack):
        self.nt = 1
        self.out_shape = [jax.ShapeDtypeStruct((8,) + pack.shape, pack.dtype)]
        self.scratch = [pltpu.SemaphoreType.DMA((7,)), pltpu.SemaphoreType.DMA((7,)), pltpu.SemaphoreType.DMA((1,))]

    def bind(self, src, dst, send_sems, recv_sems, local_sems):
        self.src, self.dst, self.send_sems, self.recv_sems, self.local_sems = src[0], dst[0], send_sems, recv_sems, local_sems

    def _copy(self, kk, mine):
        x, y, c, _ = _place()
        px, py, pc = (1 - v if fl else v for v, fl in zip((x, y, c), self.FLIPS[kk]))
        slot = 4 * x + 2 * y + c if mine else 4 * px + 2 * py + pc
        return pltpu.make_async_remote_copy(src_ref=self.src, dst_ref=self.dst.at[slot], send_sem=self.send_sems.at[kk],
                                            recv_sem=self.recv_sems.at[kk], device_id=(px, py, pc), device_id_type=MESH)

    def _local(self):
        x, y, c, _ = _place()
        return pltpu.make_async_copy(self.src, self.dst.at[4 * x + 2 * y + c], self.local_sems.at[0])

    def start(self):
        self._local().start()
        for kk in range(7):
            self._copy(kk, True).start()

    def finish(self):
        for kk in range(7):
            self._copy(kk, False).wait_recv()
        for kk in range(7):
            self._copy(kk, True).wait_send()
        self._local().wait()


class _Swap:
    def __init__(self, arrs):
        self.nt = len(arrs)
        self.out_shape = [jax.ShapeDtypeStruct(a.shape, a.dtype) for a in arrs]
        self.scratch = [pltpu.SemaphoreType.DMA((self.nt,)), pltpu.SemaphoreType.DMA((self.nt,))]

    def bind(self, src, dst, send_sems, recv_sems):
        self.src, self.dst, self.send_sems, self.recv_sems = src, dst, send_sems, recv_sems

    def _copy(self, t):
        x, y, c, _ = _place()
        return pltpu.make_async_remote_copy(src_ref=self.src[t], dst_ref=self.dst[t], send_sem=self.send_sems.at[t],
                                            recv_sem=self.recv_sems.at[t], device_id=(x, y, 1 - c), device_id_type=MESH)

    def start(self):
        for t in range(self.nt):
            self._copy(t).start()

    def finish(self):
        for t in range(self.nt):
            self._copy(t).wait()


def _sum8(packs, name):
    def body(p_ref, o_ref):
        acc = p_ref[0]
        for dev in range(1, 8):
            acc = acc + p_ref[dev]
        o_ref[...] = acc

    return pl.pallas_call(body, out_shape=jax.ShapeDtypeStruct(packs.shape[1:], F32), name=name)(packs)


def _pack(arrs):
    flat = jnp.concatenate([a.reshape(-1).astype(F32) for a in arrs])
    rows = -(-flat.shape[0] // (8 * LANES)) * 8
    return jnp.pad(flat, (0, rows * LANES - flat.shape[0])).reshape(rows, LANES)


def _unpack(flat, shapes):
    out, pos = [], 0
    for shp in shapes:
        size = 1
        for s in shp:
            size *= s
        out.append(flat[pos:pos + size].reshape(shp))
        pos += size
    return out


BIG = ("attn_w_qkv", "attn_w_o", "pool_w", "xattn_w_q", "xattn_w_kv", "xattn_w_o", "ffn_w_up", "ffn_w_down")
BIG_AXIS = ("c", "r", "r", "r", "c", "r", "c", "r")
SMALL_REPL = ("attn_norm", "attn_q_gain", "attn_k_gain", "xattn_norm", "mem_norm", "ffn_norm", "ffn_conv_b", "final_norm")
SMALL_SHARD = ("pool_norm", "pool_scale", "ffn_conv_w")
ORDER = ("attn_norm", "attn_w_qkv", "attn_q_gain", "attn_k_gain", "attn_w_o", "pool_norm", "pool_w", "pool_scale",
         "xattn_norm", "mem_norm", "xattn_w_q", "xattn_w_kv", "xattn_w_o", "ffn_norm", "ffn_w_up", "ffn_conv_w",
         "ffn_conv_b", "ffn_w_down", "final_norm")


def _step(x, mem, tgt, w, m, v):
    seq, d = x.shape
    xi, yi, ci = lax.axis_index("x"), lax.axis_index("y"), lax.axis_index("c")
    chip = 2 * xi + yi
    dff = w["ffn_w_down"].shape[1] * 4
    n_layers = w["ffn_norm"].shape[0]

    def as3d(a):
        return a.reshape(a.shape[-3:])
    shards = [as3d(w[nm]).astype(BF16) for nm in BIG]
    small_in = [w[nm] for nm in SMALL_SHARD]
    small_pack = _pack(small_in)
    conv_b = w["ffn_conv_b"].reshape(n_layers, 1, -1)
    tabs = _rope_tables(seq)
    qg2 = jnp.tile(w["attn_q_gain"], (1, 2))
    kg2 = jnp.tile(w["attn_k_gain"], (1, 2))
    mm = functools.partial(_mm)

    saved = {}
    x0 = x
    h0, wq = _rms_fwd(x0, w["attn_norm"], BF16, "rms_attn", _Gather(shards[:1], BIG_AXIS[:1]), shards[:1])
    qkv = mm(h0, wq, "nn", b_l=0, out_dtype=F32, name="mm_qkv")
    q_r, k_r, k_t, v_b, v_t, small_all = _qk_prep(qkv, qg2, kg2, tabs, _GatherAll(small_pack), [small_pack], "qk_prep")
    per_chip = [_unpack(small_all[2 * j].reshape(-1), [a.shape for a in small_in]) for j in range(4)]
    pool_norm, pool_scale, conv_w = (jnp.concatenate([per_chip[j][i] for j in range(4)], axis=-1) for i in range(3))
    first = [slice(None)] * 5 + [slice(0, 1)] * 2
    o_at, lse, wo, wp, wxq, wxkv, wxo, wup, wdn = _flash_fwd(
        q_r, k_r, v_t, _Gather(shards[1:], BIG_AXIS[1:], first), shards[1:], "flash_fwd")
    ffn_w = {"up": wup, "down": wdn}
    x1, hq0 = mm(o_at, wo, "nn", b_l=0, res=x0, out_dtype=F32, norm_out=(w["xattn_norm"][0:1], BF16), name="mm_attn_o")

    def xattn_fwd(l, xin, hq):
        mn = _rms_fwd(mem, w["mem_norm"][l:l + 1], BF16, f"rms_mem{l}")
        xq = mm(hq, wxq, "nn", b_l=l, scale=X_HEAD_DIM ** -0.5, out_dtype=BF16, name=f"mm_xq{l}")
        kv = mm(mn, wxkv, "nn", b_l=l, out_dtype=BF16, name=f"mm_xkv{l}")
        xo = _xattn_fwd(xq, kv, f"xattn_fwd{l}")
        saved[f"x{l}"] = (hq, mn, xq, kv, xo)
        return mm(xo, wxo, "nn", b_l=l, res=xin, out_dtype=F32, norm_out=(w["ffn_norm"][l:l + 1], BF16), name=f"mm_xo{l}")

    def ffn_fwd(l, xin, hf, norm_out):
        ug = mm(hf, ffn_w["up"], "nn", b_l=l, n=dff, out_dtype=BF16, name=f"mm_up_g{l}")
        uv = mm(hf, ffn_w["up"], "nn", b_l=l, n=dff, b_off=(0, dff), out_dtype=BF16, name=f"mm_up_v{l}")
        if l == 0:
            rest = _Gather(shards[6:7], BIG_AXIS[6:7], [slice(1, 2)])
            act, ffn_w["up"] = _conv_gate_fwd(ug, uv, conv_w, conv_b, l, f"conv_gate{l}", rest, shards[6:7], [ffn_w["up"]])
        else:
            act = _conv_gate_fwd(ug, uv, conv_w, conv_b, l, f"conv_gate{l}")
        saved[f"f{l}"] = (hf, ug, uv, act)
        return mm(act, ffn_w["down"], "nn", b_l=l, res=xin, out_dtype=F32, norm_out=norm_out, name=f"mm_down{l}")

    x2, hf0 = xattn_fwd(0, x1, hq0)
    x3, hp = ffn_fwd(0, x2, hf0, (pool_norm, F32))
    x4, ffn_w["down"] = _pool_fwd(hp, x3, wp, pool_scale, "pool_fwd", _Gather(shards[7:], BIG_AXIS[7:], [slice(1, 2)]),
                                  shards[7:], [ffn_w["down"]])
    x5, hf1 = xattn_fwd(1, x4, _rms_fwd(x4, w["xattn_norm"][1:2], BF16, "rms_xq1"))
    xs = [x0, x1, x2, x3, x4, x5, ffn_fwd(1, x5, hf1, None)]
    dres, g_final, loss = _final_loss(xs[6], w["final_norm"].reshape(1, d), tgt, "final_loss")

    grads = {}
    gbuf = {}

    def dw(nm, a, b, layer, full, off=(0, 0), n=None, tn=None):
        gbuf[nm] = _mm(a, b, "tn", out_dtype=BF16, out_full=full, out_l=layer, out_off=off, n=n, tn=tn,
                       alias=gbuf.get(nm), name=f"dw_{nm}{layer}_{off[1]}")

    def ffn_bwd(l, xin, dres):
        hf, ug, uv, act = saved[f"f{l}"]
        wup, wdn = ffn_w["up"], ffn_w["down"]
        dw("ffn_w_down", act, dres, l, wdn.shape)
        dact = _mm(dres, wdn, "nt", b_l=l, out_dtype=BF16, name=f"mm_dact{l}")
        dug, duv, dwg, dwv = _conv_gate_bwd(ug, uv, dact, conv_w, conv_b, l, f"conv_gate_bwd{l}")
        dw("ffn_w_up", hf, dug, l, wup.shape, tn=1408)
        dw("ffn_w_up", hf, duv, l, wup.shape, off=(0, dff), tn=1408)
        dhf = _mm(dug, wup, "nt", b_l=l, n=d, out_dtype=F32, name=f"mm_dhf_g{l}")
        dres, dg = _mm(duv, wup, "nt", b_l=l, n=d, b_off=(0, dff), res=dhf, out_dtype=F32, tm=256,
                       norm_bwd=(xin, w["ffn_norm"][l:l + 1], dres), name=f"mm_dhf_v{l}")
        return dres, dg, jnp.concatenate([dwg[:3], dwv[:3]], axis=1), jnp.concatenate([dwg[3], dwv[3]], axis=0)

    def xattn_bwd(l, xin, dres):
        hq, mn, xq, kv, xo = saved[f"x{l}"]
        dw("xattn_w_o", xo, dres, l, wxo.shape)
        dxo = _mm(dres, wxo, "nt", b_l=l, out_dtype=BF16, name=f"mm_dxo{l}")
        dq, dkv = _xattn_bwd(xq, kv, dxo, f"xattn_bwd{l}")
        dw("xattn_w_q", hq, dq, l, wxq.shape)
        dw("xattn_w_kv", mn, dkv, l, wxkv.shape)
        dmn = _mm(dkv, wxkv, "nt", b_l=l, out_dtype=F32, name=f"mm_dmn{l}")
        _, dg_mem = _rms_bwd(mem, w["mem_norm"][l:l + 1], dmn, None, f"rms_mem_bwd{l}")
        dres, dg = _mm(dq, wxq, "nt", b_l=l, out_dtype=F32, norm_bwd=(xin, w["xattn_norm"][l:l + 1], dres),
                       name=f"mm_dhq{l}")
        return dres, dg, dg_mem

    g_ffn, g_xn, g_mn, g_cw, g_cb = [None] * n_layers, [None] * n_layers, [None] * n_layers, [None] * n_layers, [None] * n_layers
    dres, g_ffn[1], g_cw[1], g_cb[1] = ffn_bwd(1, xs[5], dres)
    dres, g_xn[1], g_mn[1] = xattn_bwd(1, xs[4], dres)
    dhp, g_pw, g_pscale = _pool_bwd(hp, dres, wp, pool_scale, "pool_bwd")
    dres, g_pnorm = _rms_bwd(xs[3], pool_norm, dhp, dres, "rms_pool_bwd")
    dres, g_ffn[0], g_cw[0], g_cb[0] = ffn_bwd(0, xs[2], dres)
    dres, g_xn[0], g_mn[0] = xattn_bwd(0, xs[1], dres)
    dw("attn_w_o", o_at, dres, 0, wo.shape)
    do = _mm(dres, wo, "nt", b_l=0, out_dtype=BF16, name="mm_do")
    gbuf["pool_w"] = g_pw.astype(BF16)
    early = [gbuf[nm] for nm in BIG[1:]]
    dq_r, dk_r, dv, *recv_early = _flash_bwd(q_r, k_r, k_t, v_b, do, o_at, lse, _Scatter(early, BIG_AXIS[1:]), early,
                                             "flash_bwd")
    def sum4(nm, rc):
        return _sum4(rc.reshape(4, -1, rc.shape[-1]), f"sum4_{nm}")
    sums_early = [sum4(nm, rc) for nm, rc in zip(BIG[1:], recv_early)]
    dqkv, dqg, dkg, *others_early = _qk_prep_bwd(qkv, dq_r, dk_r, dv, qg2, kg2, tabs, _Swap(sums_early), sums_early,
                                                 "qk_prep_bwd")
    dw("attn_w_qkv", h0, dqkv, 0, wq.shape)
    grad_x, g_an = _mm(dqkv, wq, "nt", b_l=0, out_dtype=F32, norm_bwd=(x0, w["attn_norm"], dres), name="mm_dh0")

    small_g = {
        "attn_norm": g_an, "attn_q_gain": dqg[:, :HEAD_DIM] + dqg[:, HEAD_DIM:], "attn_k_gain": dkg[:, :HEAD_DIM] + dkg[:, HEAD_DIM:],
        "xattn_norm": jnp.concatenate(g_xn, axis=0), "mem_norm": jnp.concatenate(g_mn, axis=0),
        "ffn_norm": jnp.concatenate(g_ffn, axis=0), "ffn_conv_b": jnp.stack(g_cb, axis=0), "final_norm": g_final.reshape(d),
        "pool_norm": g_pnorm, "pool_scale": g_pscale, "ffn_conv_w": jnp.stack(g_cw, axis=0)}
    names = SMALL_REPL + SMALL_SHARD
    small_pack = _pack([loss[0, :1]] + [small_g[nm] for nm in names])
    late = [gbuf[nm] for nm in BIG[:1]]
    small_all, recv_late = _comm_call([_GatherAll(small_pack), _Scatter(late, BIG_AXIS[:1])], [[small_pack], late],
                                      "reduce_small_scatter_qkv")
    total = _sum8(small_all, "sum_small")
    parts = _unpack(total.reshape(-1), [(1,)] + [small_g[nm].shape for nm in names])
    loss_out = parts[0][0]
    for nm, g in zip(names, parts[1:]):
        if nm in SMALL_SHARD:
            size = w[nm].shape[-1]
            g = lax.dynamic_slice_in_dim(g, chip * size, size, axis=g.ndim - 1)
        grads[nm] = g.reshape(w[nm].shape)

    packed = [_pack([src[nm] for nm in names]) for src in (w, grads, m, v)]
    _, sd, sm, sv = _adamw(packed[0], packed[1], None, packed[2], packed[3], "adamw_small")
    shapes = [w[nm].shape for nm in names]
    delta = dict(zip(names, _unpack(sd.reshape(-1), shapes)))
    new_m = dict(zip(names, _unpack(sm.reshape(-1), shapes)))
    new_v = dict(zip(names, _unpack(sv.reshape(-1), shapes)))

    sums_late = [sum4(BIG[0], recv_late)]
    others_late = _comm_call([_Swap(sums_late)], [sums_late], "swap_qkv")
    for nm, mine, other in zip(BIG, sums_late + sums_early, list(others_late) + others_early):
        cols = mine.shape[-1]
        outs = _adamw(w[nm].reshape(-1, cols), mine, other, m[nm].reshape(-1, cols), v[nm].reshape(-1, cols), f"adamw_{nm}")
        grads[nm], delta[nm], new_m[nm], new_v[nm] = (o.reshape(w[nm].shape) for o in outs)

    return loss_out, grad_x, grads, delta, new_m, new_v


def kernel(x, mem, attn_norm, attn_w_qkv, attn_q_gain, attn_k_gain, attn_w_o, pool_norm, pool_w, pool_scale, xattn_norm, mem_norm, xattn_w_q, xattn_w_kv, xattn_w_o, ffn_norm, ffn_w_up, ffn_conv_w, ffn_conv_b, ffn_w_down, final_norm, loss_target, m_attn_norm, m_attn_w_qkv, m_attn_q_gain, m_attn_k_gain, m_attn_w_o, m_pool_norm, m_pool_w, m_pool_scale, m_xattn_norm, m_mem_norm, m_xattn_w_q, m_xattn_w_kv, m_xattn_w_o, m_ffn_norm, m_ffn_w_up, m_ffn_conv_w, m_ffn_conv_b, m_ffn_w_down, m_final_norm, v_attn_norm, v_attn_w_qkv, v_attn_q_gain, v_attn_k_gain, v_attn_w_o, v_pool_norm, v_pool_w, v_pool_scale, v_xattn_norm, v_mem_norm, v_xattn_w_q, v_xattn_w_kv, v_xattn_w_o, v_ffn_norm, v_ffn_w_up, v_ffn_conv_w, v_ffn_conv_b, v_ffn_w_down, v_final_norm):
    given = dict(locals())
    w = {nm: given[nm] for nm in ORDER}
    m = {nm: given["m_" + nm] for nm in ORDER}
    v = {nm: given["v_" + nm] for nm in ORDER}
    seq, d = x.shape[1], x.shape[2]
    loss, grad_x, grads, delta, new_m, new_v = _step(
        x.reshape(seq, d), mem.reshape(mem.shape[1], d), loss_target.reshape(seq, d), w, m, v)
    return (loss, grad_x.reshape(x.shape), *[grads[nm] for nm in ORDER], *[delta[nm] for nm in ORDER],
            *[new_m[nm] for nm in ORDER], *[new_v[nm] for nm in ORDER])
```

```python
import functools
import itertools

import jax
import jax.numpy as jnp
from jax import lax
from jax.experimental import pallas as pl
from jax.experimental.pallas import tpu as pltpu

F32, BF16 = jnp.float32, jnp.bfloat16
EPS = 1e-6
GRID_W = 64
ROPE_THETA = 10000.0
HEAD_DIM = 64
N_HEADS = 16
N_KV = 4
X_HEADS = 4
X_HEAD_DIM = 256
POOL_GROUPS = 4
POOL_GROUP_W = 256
HALO = 16
HALO_TR = 2048
LANES = 128
ADAM_LR, ADAM_B1, ADAM_B2, ADAM_EPS, ADAM_WD, ADAM_STEP = 0.001, 0.9, 0.999, 1e-08, 0.01, 10
VMEM_LIMIT = 48 * 1024 * 1024
MESH = pl.DeviceIdType.MESH
NEG = -1e30
LOG2E = 1.4426950408889634
FLASH_TQ, FLASH_TK = 512, 4096
FLASH_SUB = 512
ANY = pl.BlockSpec(memory_space=pl.ANY)


def _cp(sem=None):
    return pltpu.CompilerParams(dimension_semantics=sem, vmem_limit_bytes=VMEM_LIMIT)


def _pick(n, cands):
    for c in cands:
        if c <= n and n % c == 0:
            return c
    return n


def _mm(a, b, mode, *, name, out_dtype, tm=None, tn=None, tk=None, n=None, k=None, b_l=None, b_off=(0, 0),
        res=None, scale=None, out_full=None, out_l=None, out_off=(0, 0), alias=None, norm_out=None, norm_bwd=None):
    if mode == "tn":
        K, M = a.shape
    else:
        M, K = a.shape
    bs = b.shape[-2:]
    if mode == "nn":
        K = k or K
        N = n or bs[1]
    elif mode == "nt":
        N = n or bs[0]
    else:
        N = n or bs[1]
    wide = (1408, 1024, 512, 256, 128)
    if mode == "tn":
        tm = tm or (M if M <= 1024 else _pick(M, wide))
        tk = tk or _pick(K, (2048, 1024, 512, 256, 128))
    else:
        small = K <= 1024 and N <= 1536 and norm_bwd is None
        tm = _pick(M, (tm or (1024 if small else 512), 512, 256, 128))
        tk = tk or (K if K <= 2816 else _pick(K, wide))
    tn = tn or (N if N <= 1536 else _pick(N, wide))
    assert M % tm == 0 and N % tn == 0 and K % tk == 0, (name, M, N, K, tm, tn, tk)
    nk = K // tk
    dims = {"nn": ((1,), (0,)), "nt": ((1,), (1,)), "tn": ((0,), (0,))}[mode]

    j_outer = nk == 1 and mode != "tn"

    def at(f):
        return (lambda j, i, kk: f(i, j, kk)) if j_outer else f

    if mode == "tn":
        a_spec = pl.BlockSpec((tk, tm), at(lambda i, j, kk: (kk, i)))
    else:
        a_spec = pl.BlockSpec((tm, tk), at(lambda i, j, kk: (i, kk)))
    if mode == "nt":
        bb, (d0, d1) = (tn, tk), (b_off[0] // tn, b_off[1] // tk)
        assert b_off[0] % tn == 0 and b_off[1] % tk == 0
        bidx = lambda i, j, kk: (j + d0, kk + d1)
    else:
        bb, (d0, d1) = (tk, tn), (b_off[0] // tk, b_off[1] // tn)
        assert b_off[0] % tk == 0 and b_off[1] % tn == 0
        bidx = lambda i, j, kk: (kk + d0, j + d1)
    if b.ndim == 3:
        b_spec = pl.BlockSpec((None,) + bb, at(lambda i, j, kk: (b_l,) + bidx(i, j, kk)))
    else:
        b_spec = pl.BlockSpec(bb, at(bidx))
    in_specs, operands = [a_spec, b_spec], [a, b]
    if res is not None:
        in_specs.append(pl.BlockSpec((tm, tn), at(lambda i, j, kk: (i, j))))
        operands.append(res)
    aliases = {}
    if alias is not None:
        aliases = {len(operands): 0}
        in_specs.append(ANY)
        operands.append(alias)
    if out_full is None:
        out_shape = jax.ShapeDtypeStruct((M, N), out_dtype)
        out_spec = pl.BlockSpec((tm, tn), at(lambda i, j, kk: (i, j)))
    else:
        assert out_off[0] % tm == 0 and out_off[1] % tn == 0
        o0, o1 = out_off[0] // tm, out_off[1] // tn
        out_shape = jax.ShapeDtypeStruct(out_full, out_dtype)
        out_spec = pl.BlockSpec((None, tm, tn), at(lambda i, j, kk: (out_l, i + o0, j + o1)))
    has_res, has_alias = res is not None, alias is not None
    grid = (N // tn, M // tm, nk) if j_outer else (M // tm, N // tn, nk)
    n_extra = 0
    if norm_out is not None or norm_bwd is not None:
        assert j_outer and tn == N and out_full is None, name
        row = pl.BlockSpec((tm, tn), at(lambda i, j, kk: (i, 0)))
        vec = pl.BlockSpec((1, tn), at(lambda i, j, kk: (0, 0)))
        if norm_out is not None:
            in_specs.append(vec)
            operands.append(norm_out[0])
            n_extra = 1
            out_shape = (out_shape, jax.ShapeDtypeStruct((M, N), norm_out[1]))
            out_spec = (out_spec, row)
        else:
            in_specs += [row, vec, row]
            operands += list(norm_bwd)
            n_extra = 3
            out_shape = (out_shape, jax.ShapeDtypeStruct((1, N), F32))
            out_spec = (out_spec, vec)
    n_out = 1 if n_extra == 0 else 2

    def body(*refs):
        a_ref, b_ref = refs[0], refs[1]
        pos = 2
        res_ref = None
        if has_res:
            res_ref = refs[pos]
            pos += 1
        if has_alias:
            pos += 1
        extra = refs[pos:pos + n_extra]
        pos += n_extra
        o_ref, acc_ref = refs[pos], refs[pos + n_out]
        kk = pl.program_id(2)
        part = lax.dot_general(a_ref[...].astype(BF16), b_ref[...].astype(BF16), (dims, ((), ())),
                               preferred_element_type=F32)

        def finish(acc):
            if scale is not None:
                acc = acc * scale
            if res_ref is not None:
                acc = acc + res_ref[...]
            if norm_out is not None:
                r = lax.rsqrt(jnp.mean(acc * acc, axis=-1, keepdims=True) + EPS)
                refs[pos + 1][...] = (acc * r * extra[0][...]).astype(refs[pos + 1].dtype)
            if norm_bwd is not None:
                x_ref, g_ref, dres_ref = extra
                dg_ref, step = refs[pos + 1], pl.program_id(1)
                xv = x_ref[...]
                r = lax.rsqrt(jnp.mean(xv * xv, axis=-1, keepdims=True) + EPS)
                nv = xv * r
                dgp = jnp.sum(acc * nv, axis=0, keepdims=True)

                @pl.when(step == 0)
                def _():
                    dg_ref[...] = dgp

                @pl.when(step > 0)
                def _():
                    dg_ref[...] += dgp

                dn = acc * g_ref[...]
                acc = dres_ref[...] + r * (dn - nv * jnp.mean(dn * nv, axis=-1, keepdims=True))
            o_ref[...] = acc.astype(o_ref.dtype)

        if nk == 1:
            finish(part)
        else:
            @pl.when(kk == 0)
            def _():
                acc_ref[...] = part

            @pl.when(jnp.logical_and(kk > 0, kk < nk - 1))
            def _():
                acc_ref[...] += part

            @pl.when(kk == nk - 1)
            def _():
                finish(acc_ref[...] + part)

    return pl.pallas_call(
        body, out_shape=out_shape, grid=grid, in_specs=in_specs, out_specs=out_spec,
        scratch_shapes=[pltpu.VMEM((tm, tn) if nk > 1 else (8, 128), F32)], input_output_aliases=aliases,
        compiler_params=_cp(("arbitrary",) * 3 if norm_bwd is not None else ("parallel", "parallel", "arbitrary")),
        name=name)(*operands)


def _hosted(plan, nsteps, step, compute):
    if plan is None:
        return compute()

    @pl.when(step == 0)
    def _():
        plan.start()

    compute()

    @pl.when(step == nsteps - 1)
    def _():
        if hasattr(plan, "forward"):
            plan.forward()
        plan.finish()


def _rms_fwd(x, gain, out_dtype, name, plan=None, sends=()):
    rows, d = x.shape
    tr = _pick(rows, (512, 256))
    nt, nsteps = (plan.nt if plan is not None else 0), rows // tr

    def body(x_ref, g_ref, *rest):
        o_ref = rest[nt]
        if plan is not None:
            plan.bind(rest[:nt], rest[nt + 1:2 * nt + 1], *rest[2 * nt + 1:])

        def compute():
            xv = x_ref[...]
            r = lax.rsqrt(jnp.mean(xv * xv, axis=-1, keepdims=True) + EPS)
            o_ref[...] = (xv * r * g_ref[...]).astype(o_ref.dtype)

        _hosted(plan, nsteps, pl.program_id(0), compute)

    out = jax.ShapeDtypeStruct((rows, d), out_dtype)
    row = pl.BlockSpec((tr, d), lambda i: (i, 0))
    in_specs = [row, pl.BlockSpec((1, d), lambda i: (0, 0))]
    if plan is None:
        return pl.pallas_call(body, out_shape=out, grid=(nsteps,), in_specs=in_specs, out_specs=row,
                              compiler_params=_cp(("parallel",)), name=name)(x, gain)
    return pl.pallas_call(
        body, out_shape=(out, *plan.out_shape), grid=(nsteps,), in_specs=in_specs + [ANY] * nt,
        out_specs=(row, *([ANY] * nt)), scratch_shapes=plan.scratch, compiler_params=_cp(("arbitrary",)),
        name=name)(x, gain, *sends)


def _rms_bwd(x, gain, dh, dres, name):
    rows, d = x.shape
    tr = _pick(rows, (512, 256))
    need_dx = dres is not None

    def body(*refs):
        if need_dx:
            x_ref, g_ref, dh_ref, dres_ref, o_ref, dg_ref = refs
        else:
            x_ref, g_ref, dh_ref, dg_ref = refs
        i = pl.program_id(0)
        xv = x_ref[...]
        dhv = dh_ref[...].astype(F32)
        r = lax.rsqrt(jnp.mean(xv * xv, axis=-1, keepdims=True) + EPS)
        nv = xv * r
        part = jnp.sum(dhv * nv, axis=0, keepdims=True)

        @pl.when(i == 0)
        def _():
            dg_ref[...] = part

        @pl.when(i > 0)
        def _():
            dg_ref[...] += part

        if need_dx:
            dn = dhv * g_ref[...]
            dx = r * (dn - nv * jnp.mean(dn * nv, axis=-1, keepdims=True))
            o_ref[...] = dres_ref[...] + dx

    row_spec = pl.BlockSpec((tr, d), lambda i: (i, 0))
    vec_spec = pl.BlockSpec((1, d), lambda i: (0, 0))
    if need_dx:
        return pl.pallas_call(
            body, out_shape=(jax.ShapeDtypeStruct((rows, d), F32), jax.ShapeDtypeStruct((1, d), F32)),
            grid=(rows // tr,), in_specs=[row_spec, vec_spec, row_spec, row_spec], out_specs=(row_spec, vec_spec),
            compiler_params=_cp(("arbitrary",)), name=name)(x, gain, dh, dres)
    return None, pl.pallas_call(
        body, out_shape=jax.ShapeDtypeStruct((1, d), F32), grid=(rows // tr,),
        in_specs=[row_spec, vec_spec, row_spec], out_specs=vec_spec,
        compiler_params=_cp(("arbitrary",)), name=name)(x, gain, dh)


def _final_loss(x, gain, target, name):
    rows, d = x.shape
    tr = _pick(rows, (512, 256))
    nsteps = rows // tr

    def body(x_ref, g_ref, t_ref, dx_ref, dg_ref, loss_ref, acc_ref):
        i = pl.program_id(0)
        xv = x_ref[...]
        g = g_ref[...]
        r = lax.rsqrt(jnp.mean(xv * xv, axis=-1, keepdims=True) + EPS)
        nv = xv * r
        err = nv * g - t_ref[...]
        dy = err * (1.0 / d)
        dn = dy * g
        dx_ref[...] = r * (dn - nv * jnp.mean(dn * nv, axis=-1, keepdims=True))
        dgp = jnp.sum(dy * nv, axis=0, keepdims=True)
        lp = jnp.sum(err * err, axis=0, keepdims=True)

        @pl.when(i == 0)
        def _():
            dg_ref[...] = dgp
            acc_ref[...] = lp

        @pl.when(i > 0)
        def _():
            dg_ref[...] += dgp
            acc_ref[...] += lp

        @pl.when(i == nsteps - 1)
        def _():
            tot = jnp.sum(acc_ref[...], axis=1, keepdims=True) * (0.5 / d)
            loss_ref[...] = jnp.broadcast_to(tot, loss_ref.shape)

    row_spec = pl.BlockSpec((tr, d), lambda i: (i, 0))
    vec_spec = pl.BlockSpec((1, d), lambda i: (0, 0))
    return pl.pallas_call(
        body, out_shape=(jax.ShapeDtypeStruct((rows, d), F32), jax.ShapeDtypeStruct((1, d), F32),
                         jax.ShapeDtypeStruct((1, LANES), F32)),
        grid=(nsteps,), in_specs=[row_spec, vec_spec, row_spec],
        out_specs=(row_spec, vec_spec, pl.BlockSpec((1, LANES), lambda i: (0, 0))),
        scratch_shapes=[pltpu.VMEM((1, d), F32)], compiler_params=_cp(("arbitrary",)), name=name)(x, gain, target)


def _rope_tables(seq):
    pairs = HEAD_DIM // 4
    lane = jnp.arange(LANES, dtype=jnp.int32) % HEAD_DIM
    by_col, second, pair = lane // (2 * pairs) == 1, (lane % (2 * pairs)) // pairs == 1, lane % pairs
    inv_freq = ROPE_THETA ** (-pair.astype(F32) / pairs)
    t = jnp.arange(seq, dtype=jnp.int32)[:, None]
    pos = jnp.where(by_col[None, :], t % GRID_W, t // GRID_W).astype(F32)
    ang = pos * inv_freq[None, :]
    cos, sin = jnp.cos(ang), jnp.sin(ang)
    return cos, jnp.where(second[None, :], sin, 0.0), jnp.where(second[None, :], 0.0, -sin)


def _pair_norm(xv, lo):
    sq = xv * xv
    s_lo = jnp.sum(jnp.where(lo, sq, 0.0), axis=1, keepdims=True)
    s_hi = jnp.sum(jnp.where(lo, 0.0, sq), axis=1, keepdims=True)
    return lax.rsqrt(jnp.where(lo, s_lo, s_hi) * (1.0 / HEAD_DIM) + EPS)


def _rope(y, c, sp, sm):
    return y * c + pltpu.roll(y, 16, axis=1) * sp + pltpu.roll(y, LANES - 16, axis=1) * sm


def _rope_t(dz, c, sp, sm):
    return dz * c + pltpu.roll(dz * sp, LANES - 16, axis=1) + pltpu.roll(dz * sm, 16, axis=1)


def _qk_prep(qkv, qg2, kg2, tabs, plan, sends, name):
    seq = qkv.shape[0]
    ts = _pick(seq, (512, 256, 128))
    nq, nkp = N_HEADS // 2, N_KV // 2
    qw, kw = N_HEADS * HEAD_DIM, N_KV * HEAD_DIM
    nt, nsteps = plan.nt, seq // ts

    def body(x_ref, qg_ref, kg_ref, c_ref, sp_ref, sm_ref, *rest):
        q_ref, k_ref, kt_ref, v_ref, vt_ref = rest[nt:nt + 5]
        plan.bind(rest[:nt], rest[nt + 5:2 * nt + 5], *rest[2 * nt + 5:])
        _hosted(plan, nsteps, pl.program_id(0), lambda: compute(x_ref, qg_ref, kg_ref, c_ref, sp_ref, sm_ref,
                                                                q_ref, k_ref, kt_ref, v_ref, vt_ref))

    def compute(x_ref, qg_ref, kg_ref, c_ref, sp_ref, sm_ref, q_ref, k_ref, kt_ref, v_ref, vt_ref):
        lo = lax.broadcasted_iota(jnp.int32, (ts, LANES), 1) < HEAD_DIM
        top = lax.broadcasted_iota(jnp.int32, (LANES, ts), 0) < HEAD_DIM
        c, sp, sm = c_ref[...], sp_ref[...], sm_ref[...]
        for i in range(nq):
            xv = x_ref[:, i * LANES:(i + 1) * LANES]
            y = xv * _pair_norm(xv, lo) * qg_ref[...]
            q_ref[:, i * LANES:(i + 1) * LANES] = (_rope(y, c, sp, sm) * (LOG2E * HEAD_DIM ** -0.5)).astype(BF16)
        for i in range(nkp):
            xv = x_ref[:, qw + i * LANES:qw + (i + 1) * LANES]
            z = _rope(xv * _pair_norm(xv, lo) * kg_ref[...], c, sp, sm)
            k_ref[:, i * LANES:(i + 1) * LANES] = z.astype(BF16)
            kt_ref[i * LANES:(i + 1) * LANES, :] = z.T.astype(BF16)
            vv = x_ref[:, qw + kw + i * LANES:qw + kw + (i + 1) * LANES]
            v_ref[:, i * LANES:(i + 1) * LANES] = vv.astype(BF16)
            vvt = vv.T
            vt_ref[(2 * i) * LANES:(2 * i + 1) * LANES, :] = jnp.where(top, vvt, 1.0).astype(BF16)
            vt_ref[(2 * i + 1) * LANES:(2 * i + 2) * LANES, :] = jnp.where(top, 1.0, vvt).astype(BF16)

    tab = pl.BlockSpec((ts, LANES), lambda i: (i, 0))
    vec = pl.BlockSpec((1, LANES), lambda i: (0, 0))
    return pl.pallas_call(
        body,
        out_shape=(jax.ShapeDtypeStruct((seq, qw), BF16), jax.ShapeDtypeStruct((seq, kw), BF16),
                   jax.ShapeDtypeStruct((kw, seq), BF16), jax.ShapeDtypeStruct((seq, kw), BF16),
                   jax.ShapeDtypeStruct((N_KV * LANES, seq), BF16), *plan.out_shape),
        grid=(nsteps,),
        in_specs=[pl.BlockSpec((ts, qw + 2 * kw), lambda i: (i, 0)), vec, vec, tab, tab, tab] + [ANY] * nt,
        out_specs=(pl.BlockSpec((ts, qw), lambda i: (i, 0)), pl.BlockSpec((ts, kw), lambda i: (i, 0)),
                   pl.BlockSpec((kw, ts), lambda i: (0, i)), pl.BlockSpec((ts, kw), lambda i: (i, 0)),
                   pl.BlockSpec((N_KV * LANES, ts), lambda i: (0, i)), *([ANY] * nt)),
        scratch_shapes=plan.scratch, compiler_params=_cp(("arbitrary",)), name=name)(qkv, qg2, kg2, *tabs, *sends)


def _qk_prep_bwd(qkv, dq, dk, dv, qg2, kg2, tabs, plan, sends, name):
    seq = qkv.shape[0]
    ts = _pick(seq, (512, 256, 128))
    nq, nkp = N_HEADS // 2, N_KV // 2
    qw, kw = N_HEADS * HEAD_DIM, N_KV * HEAD_DIM
    nt, nsteps = plan.nt, seq // ts

    def body(x_ref, dq_ref, dk_ref, dv_ref, qg_ref, kg_ref, c_ref, sp_ref, sm_ref, *rest):
        o_ref, dqg_ref, dkg_ref = rest[nt:nt + 3]
        plan.bind(rest[:nt], rest[nt + 3:2 * nt + 3], *rest[2 * nt + 3:])
        step = pl.program_id(0)

        @pl.when(step == 0)
        def _():
            plan.start()

        lo = lax.broadcasted_iota(jnp.int32, (ts, LANES), 1) < HEAD_DIM
        c, sp, sm = c_ref[...], sp_ref[...], sm_ref[...]

        def one(xv, dz, gain):
            r = _pair_norm(xv, lo)
            nv = xv * r
            dy = _rope_t(dz, c, sp, sm)
            dgp = jnp.sum(dy * nv, axis=0, keepdims=True)
            dn = dy * gain
            t = dn * nv
            m_lo = jnp.sum(jnp.where(lo, t, 0.0), axis=1, keepdims=True)
            m_hi = jnp.sum(jnp.where(lo, 0.0, t), axis=1, keepdims=True)
            m = jnp.where(lo, m_lo, m_hi) * (1.0 / HEAD_DIM)
            return r * (dn - nv * m), dgp

        dqg = jnp.zeros((1, LANES), F32)
        for i in range(nq):
            sl = slice(i * LANES, (i + 1) * LANES)
            dx, dgp = one(x_ref[:, sl], dq_ref[:, sl] * (HEAD_DIM ** -0.5), qg_ref[...])
            o_ref[:, sl] = dx.astype(BF16)
            dqg = dqg + dgp
        dkg = jnp.zeros((1, LANES), F32)
        for i in range(nkp):
            sl = slice(i * LANES, (i + 1) * LANES)
            dx, dgp = one(x_ref[:, qw + i * LANES:qw + (i + 1) * LANES], dk_ref[:, sl], kg_ref[...])
            o_ref[:, qw + i * LANES:qw + (i + 1) * LANES] = dx.astype(BF16)
            dkg = dkg + dgp
            o_ref[:, qw + kw + i * LANES:qw + kw + (i + 1) * LANES] = dv_ref[:, sl].astype(BF16)

        @pl.when(step == 0)
        def _():
            dqg_ref[...] = dqg
            dkg_ref[...] = dkg

        @pl.when(step > 0)
        def _():
            dqg_ref[...] += dqg
            dkg_ref[...] += dkg

        @pl.when(step == nsteps - 1)
        def _():
            plan.finish()

    tab = pl.BlockSpec((ts, LANES), lambda i: (i, 0))
    vec = pl.BlockSpec((1, LANES), lambda i: (0, 0))
    return pl.pallas_call(
        body,
        out_shape=(jax.ShapeDtypeStruct((seq, qw + 2 * kw), BF16), jax.ShapeDtypeStruct((1, LANES), F32),
                   jax.ShapeDtypeStruct((1, LANES), F32), *plan.out_shape),
        grid=(nsteps,),
        in_specs=[pl.BlockSpec((ts, qw + 2 * kw), lambda i: (i, 0)), pl.BlockSpec((ts, qw), lambda i: (i, 0)),
                  pl.BlockSpec((ts, kw), lambda i: (i, 0)), pl.BlockSpec((ts, kw), lambda i: (i, 0)),
                  vec, vec, tab, tab, tab] + [ANY] * nt,
        out_specs=(pl.BlockSpec((ts, qw + 2 * kw), lambda i: (i, 0)), vec, vec, *([ANY] * nt)),
        scratch_shapes=plan.scratch, compiler_params=_cp(("arbitrary",)), name=name)(qkv, dq, dk, dv, qg2, kg2, *tabs, *sends)


def _slot(blk, off0, tq):
    half = lax.broadcasted_iota(jnp.int32, (tq, LANES), 1) // HEAD_DIM
    keep = half == jnp.where(off0, 0, 1)
    parts = []
    for p in range(2):
        pair = blk[:, p * LANES:(p + 1) * LANES].astype(F32)
        rolled = pltpu.roll(pair, HEAD_DIM, axis=1)
        parts.append(jnp.where(keep, jnp.where(off0, pair, rolled), 0.0))
        parts.append(jnp.where(keep, jnp.where(off0, rolled, pair), 0.0))
    return jnp.concatenate(parts, axis=0)


def _unslot(x4, off0, tq):
    lo = lax.broadcasted_iota(jnp.int32, (tq, LANES), 1) < HEAD_DIM
    pairs = []
    for p in range(2):
        h0 = x4[(2 * p) * tq:(2 * p + 1) * tq]
        h1 = x4[(2 * p + 1) * tq:(2 * p + 2) * tq]
        a = jnp.where(off0, h0, pltpu.roll(h0, HEAD_DIM, axis=1))
        b = jnp.where(off0, pltpu.roll(h1, HEAD_DIM, axis=1), h1)
        pairs.append(jnp.where(lo, a, b))
    return jnp.concatenate(pairs, axis=1)


def _flash_fwd(q, k, vt, plan, shards, name):
    seq = q.shape[0]
    tq = _pick(seq, (FLASH_TQ, 128))
    tk = _pick(seq, (FLASH_TK, 2048, 512, 256, 128))
    sub = _pick(tk, (FLASH_SUB, 256, 128))
    nq, nkv, nsub = seq // tq, seq // tk, tk // sub
    gw = 4 * HEAD_DIM
    nt = plan.nt

    def body(q_ref, k_ref, vt_ref, *rest):
        o_ref, lse_ref = rest[nt:nt + 2]
        q4_ref, m_ref, acc_ref, st_ref = rest[2 * nt + 2:2 * nt + 6]
        plan.bind(rest[:nt], rest[nt + 2:2 * nt + 2], *rest[2 * nt + 6:])
        g, qi, ki = pl.program_id(0), pl.program_id(1), pl.program_id(2)
        off0 = (g % 2) == 0
        @pl.when(jnp.logical_and(g == 0, jnp.logical_and(qi == 0, ki == 0)))
        def _():
            plan.start()

        @pl.when(jnp.logical_and(g == N_KV - 1, jnp.logical_and(qi == nq - 1, ki == 0)))
        def _():
            plan.forward()

        @pl.when(ki == 0)
        def _():
            q4_ref[...] = _slot(q_ref[...], off0, tq).astype(BF16)
            m_ref[...] = jnp.full(m_ref.shape, NEG, F32)
            acc_ref[...] = jnp.zeros(acc_ref.shape, F32)

        q4 = q4_ref[...]

        def scores(c):
            st_ref[c % 2] = lax.dot_general(k_ref[c * sub:(c + 1) * sub, :], q4, (((1,), (1,)), ((), ())),
                                            preferred_element_type=F32)

        m, acc = m_ref[...], acc_ref[...]
        scores(0)
        for c in range(nsub):
            if c + 1 < nsub:
                scores(c + 1)
            st = st_ref[c % 2]
            m_new = jnp.maximum(m, jnp.max(st, axis=0, keepdims=True))
            pt = jnp.exp2(st - m_new).astype(BF16)
            acc = jnp.exp2(m - m_new) * acc + jnp.dot(vt_ref[:, c * sub:(c + 1) * sub], pt, preferred_element_type=F32)
            m = m_new
        m_ref[...] = m
        acc_ref[...] = acc

        @pl.when(ki == nkv - 1)
        def _():
            acc = acc_ref[...]
            l = jnp.where(off0, acc[HEAD_DIM:HEAD_DIM + 1], acc[0:1])
            o4 = acc.T
            o4 = o4 / pltpu.roll(o4, HEAD_DIM, axis=1)
            o_ref[...] = _unslot(o4, off0, tq).astype(o_ref.dtype)
            lse_ref[...] = jnp.broadcast_to(m_ref[...] + jnp.log2(l), lse_ref.shape)

        @pl.when(jnp.logical_and(g == N_KV - 1, jnp.logical_and(qi == nq - 1, ki == nkv - 1)))
        def _():
            plan.finish()

    return pl.pallas_call(
        body,
        out_shape=(jax.ShapeDtypeStruct((seq, N_HEADS * HEAD_DIM), BF16),
                   jax.ShapeDtypeStruct((N_KV * nq * 8, 4 * tq), F32), *plan.out_shape),
        grid=(N_KV, nq, nkv),
        in_specs=[pl.BlockSpec((tq, gw), lambda g, qi, ki: (qi, g)),
                  pl.BlockSpec((tk, LANES), lambda g, qi, ki: (ki, g // 2)),
                  pl.BlockSpec((LANES, tk), lambda g, qi, ki: (g, ki))] + [ANY] * nt,
        out_specs=(pl.BlockSpec((tq, gw), lambda g, qi, ki: (qi, g)),
                   pl.BlockSpec((8, 4 * tq), lambda g, qi, ki: (g * nq + qi, 0)), *([ANY] * nt)),
        scratch_shapes=[pltpu.VMEM((4 * tq, LANES), BF16), pltpu.VMEM((1, 4 * tq), F32),
                        pltpu.VMEM((LANES, 4 * tq), F32), pltpu.VMEM((2, sub, 4 * tq), F32)] + plan.scratch,
        compiler_params=_cp(("arbitrary", "arbitrary", "arbitrary")), name=name)(q, k, vt, *shards)


def _flash_bwd(q, k, kt, v, do, o, lse, plan, grads, name):
    seq = q.shape[0]
    tq = _pick(seq, (FLASH_TQ, 128))
    tk = _pick(seq, (FLASH_TK, 2048, 512, 256, 128))
    sub = _pick(tk, (FLASH_SUB, 256, 128))
    nq, nkv, nsub = seq // tq, seq // tk, tk // sub
    gw = 4 * HEAD_DIM
    nt = plan.nt

    def body(q_ref, k_ref, kt_ref, v_ref, do_ref, o_ref, lse_ref, *rest):
        dq_ref, dk_ref, dv_ref = rest[nt:nt + 3]
        q4_ref, do4_ref, delta_ref, dqt_ref, st_ref, dpt_ref = rest[2 * nt + 3:2 * nt + 9]
        plan.bind(rest[:nt], rest[nt + 3:2 * nt + 3], *rest[2 * nt + 9:])
        g, qi, ki = pl.program_id(0), pl.program_id(1), pl.program_id(2)
        off0 = (g % 2) == 0

        @pl.when(jnp.logical_and(g == 0, jnp.logical_and(qi == 0, ki == 0)))
        def _():
            plan.start()

        @pl.when(jnp.logical_and(g % 2 == 0, jnp.logical_and(qi == 0, ki == 0)))
        def _():
            dk_ref[...] = jnp.zeros(dk_ref.shape, F32)
            dv_ref[...] = jnp.zeros(dv_ref.shape, F32)

        @pl.when(ki == 0)
        def _():
            q4_ref[...] = _slot(q_ref[...], off0, tq).astype(BF16)
            do4 = _slot(do_ref[...], off0, tq)
            do4_ref[...] = do4.astype(BF16)
            o4 = _slot(o_ref[...], off0, tq)
            delta_ref[...] = jnp.sum((do4 * o4).T, axis=0, keepdims=True)
            dqt_ref[...] = jnp.zeros(dqt_ref.shape, F32)

        q4, do4 = q4_ref[...], do4_ref[...]
        lse_row, delta = lse_ref[0:1, :], delta_ref[...]

        def products(c):
            rows = slice(c * sub, (c + 1) * sub)
            st_ref[c % 2] = lax.dot_general(k_ref[rows, :], q4, (((1,), (1,)), ((), ())), preferred_element_type=F32)
            dpt_ref[c % 2] = lax.dot_general(v_ref[rows, :], do4, (((1,), (1,)), ((), ())), preferred_element_type=F32)

        dqt = dqt_ref[...]
        products(0)
        for c in range(nsub):
            if c + 1 < nsub:
                products(c + 1)
            pt = jnp.exp2(st_ref[c % 2] - lse_row)
            dst = (pt * (dpt_ref[c % 2] - delta)).astype(BF16)
            rows = pl.ds(pl.multiple_of(ki * tk + c * sub, sub), sub)
            dv_ref[rows, :] += jnp.dot(pt.astype(BF16), do4, preferred_element_type=F32)
            dk_ref[rows, :] += jnp.dot(dst, q4, preferred_element_type=F32) * (1.0 / LOG2E)
            dqt = dqt + jnp.dot(kt_ref[:, c * sub:(c + 1) * sub], dst, preferred_element_type=F32)
        dqt_ref[...] = dqt

        @pl.when(ki == nkv - 1)
        def _():
            dq_ref[...] = _unslot(dqt_ref[...].T, off0, tq)

        @pl.when(jnp.logical_and(g == N_KV - 1, jnp.logical_and(qi == nq - 1, ki == nkv - 1)))
        def _():
            plan.finish()

    return pl.pallas_call(
        body,
        out_shape=(jax.ShapeDtypeStruct((seq, N_HEADS * HEAD_DIM), F32),
                   jax.ShapeDtypeStruct((seq, N_KV * HEAD_DIM), F32), jax.ShapeDtypeStruct((seq, N_KV * HEAD_DIM), F32),
                   *plan.out_shape),
        grid=(N_KV, nq, nkv),
        in_specs=[pl.BlockSpec((tq, gw), lambda g, qi, ki: (qi, g)),
                  pl.BlockSpec((tk, LANES), lambda g, qi, ki: (ki, g // 2)),
                  pl.BlockSpec((LANES, tk), lambda g, qi, ki: (g // 2, ki)),
                  pl.BlockSpec((tk, LANES), lambda g, qi, ki: (ki, g // 2)),
                  pl.BlockSpec((tq, gw), lambda g, qi, ki: (qi, g)),
                  pl.BlockSpec((tq, gw), lambda g, qi, ki: (qi, g)),
                  pl.BlockSpec((8, 4 * tq), lambda g, qi, ki: (g * nq + qi, 0))] + [ANY] * nt,
        out_specs=(pl.BlockSpec((tq, gw), lambda g, qi, ki: (qi, g)),
                   pl.BlockSpec((seq, LANES), lambda g, qi, ki: (0, g // 2)),
                   pl.BlockSpec((seq, LANES), lambda g, qi, ki: (0, g // 2)), *([ANY] * nt)),
        scratch_shapes=[pltpu.VMEM((4 * tq, LANES), BF16), pltpu.VMEM((4 * tq, LANES), BF16),
                        pltpu.VMEM((1, 4 * tq), F32), pltpu.VMEM((LANES, 4 * tq), F32),
                        pltpu.VMEM((2, sub, 4 * tq), F32), pltpu.VMEM((2, sub, 4 * tq), F32)] + plan.scratch,
        compiler_params=_cp(("arbitrary", "arbitrary", "arbitrary")), name=name)(q, k, kt, v, do, o, lse, *grads)


def _xattn_fwd(q, kv, name):
    seq, d = q.shape
    mlen = kv.shape[0]
    tq = _pick(seq, (1024, 512, 256))

    def body(q_ref, k_ref, v_ref, o_ref):
        for h in range(X_HEADS):
            sl = slice(h * X_HEAD_DIM, (h + 1) * X_HEAD_DIM)
            s = lax.dot_general(q_ref[:, sl], k_ref[:, sl], (((1,), (1,)), ((), ())), preferred_element_type=F32)
            e = jnp.exp(s - jnp.max(s, axis=-1, keepdims=True))
            p = e / jnp.sum(e, axis=-1, keepdims=True)
            o_ref[:, sl] = jnp.dot(p.astype(BF16), v_ref[:, sl], preferred_element_type=F32).astype(o_ref.dtype)

    return pl.pallas_call(
        body, out_shape=jax.ShapeDtypeStruct((seq, d), BF16), grid=(seq // tq,),
        in_specs=[pl.BlockSpec((tq, d), lambda i: (i, 0)), pl.BlockSpec((mlen, d), lambda i: (0, 0)),
                  pl.BlockSpec((mlen, d), lambda i: (0, 1))],
        out_specs=pl.BlockSpec((tq, d), lambda i: (i, 0)), compiler_params=_cp(("parallel",)), name=name)(q, kv, kv)


def _xattn_bwd(q, kv, do, name):
    seq, d = q.shape
    mlen = kv.shape[0]
    tq = _pick(seq, (1024, 512, 256))
    scale = X_HEAD_DIM ** -0.5

    def body(q_ref, k_ref, v_ref, do_ref, dq_ref, dkv_ref):
        i = pl.program_id(0)

        @pl.when(i == 0)
        def _():
            dkv_ref[...] = jnp.zeros(dkv_ref.shape, F32)

        for h in range(X_HEADS):
            sl = slice(h * X_HEAD_DIM, (h + 1) * X_HEAD_DIM)
            qh, kh, vh = q_ref[:, sl], k_ref[:, sl], v_ref[:, sl]
            doh = do_ref[:, sl].astype(BF16)
            st = lax.dot_general(kh, qh, (((1,), (1,)), ((), ())), preferred_element_type=F32)
            e = jnp.exp(st - jnp.max(st, axis=0, keepdims=True))
            pt = e / jnp.sum(e, axis=0, keepdims=True)
            dpt = lax.dot_general(vh, doh, (((1,), (1,)), ((), ())), preferred_element_type=F32)
            dst = (pt * (dpt - jnp.sum(pt * dpt, axis=0, keepdims=True))).astype(BF16)
            dkv_ref[:, sl] += jnp.dot(dst, qh, preferred_element_type=F32)
            dkv_ref[:, d + h * X_HEAD_DIM:d + (h + 1) * X_HEAD_DIM] += jnp.dot(pt.astype(BF16), doh,
                                                                                 preferred_element_type=F32)
            dqh = lax.dot_general(dst, kh, (((0,), (0,)), ((), ())), preferred_element_type=F32)
            dq_ref[:, sl] = (dqh * scale).astype(dq_ref.dtype)

    return pl.pallas_call(
        body, out_shape=(jax.ShapeDtypeStruct((seq, d), BF16), jax.ShapeDtypeStruct((mlen, 2 * d), F32)),
        grid=(seq // tq,),
        in_specs=[pl.BlockSpec((tq, d), lambda i: (i, 0)), pl.BlockSpec((mlen, d), lambda i: (0, 0)),
                  pl.BlockSpec((mlen, d), lambda i: (0, 1)), pl.BlockSpec((tq, d), lambda i: (i, 0))],
        out_specs=(pl.BlockSpec((tq, d), lambda i: (i, 0)), pl.BlockSpec((mlen, 2 * d), lambda i: (0, 0))),
        compiler_params=_cp(("arbitrary",)), name=name)(q, kv, kv, do)


def _halo_specs(tr, tc, seq, col):
    per, last = tr // HALO, seq // HALO - 1
    return [pl.BlockSpec((tr, tc), lambda j, r: (r, col(j))),
            pl.BlockSpec((HALO, tc), lambda j, r: (jnp.maximum(r * per - 1, 0), col(j))),
            pl.BlockSpec((HALO, tc), lambda j, r: (jnp.minimum((r + 1) * per, last), col(j)))]


def _extend(main_ref, prev_ref, next_ref, r, nr):
    pv = (r > 0).astype(F32)
    nv = (r < nr - 1).astype(F32)
    return jnp.concatenate([prev_ref[...].astype(F32) * pv, main_ref[...].astype(F32),
                            next_ref[...].astype(F32) * nv], axis=0)


def _conv3(e, w_ref, n):
    return pltpu.roll(e, 1, axis=0) * w_ref[0:1, :] + e * w_ref[1:2, :] + pltpu.roll(e, n - 1, axis=0) * w_ref[2:3, :]


def _conv_gate_fwd(ug, uv, cw, cb, layer, name, plan=None, shards=(), fulls=()):
    seq, f = ug.shape
    tc = 256
    tr = _pick(seq, (HALO_TR, 512, 256))
    nc, nr = f // tc, seq // tr
    n = tr + 2 * HALO
    nt = plan.nt if plan is not None else 0

    def body(g_ref, gp_ref, gn_ref, v_ref, vp_ref, vn_ref, wg_ref, wv_ref, bg_ref, bv_ref, *rest):
        o_ref = rest[2 * nt]
        j, r = pl.program_id(0), pl.program_id(1)
        if plan is not None:
            plan.bind(rest[:nt], rest[2 * nt + 1:3 * nt + 1], *rest[3 * nt + 1:])

            @pl.when(jnp.logical_and(j == 0, r == 0))
            def _():
                plan.start()

        cg = _conv3(_extend(g_ref, gp_ref, gn_ref, r, nr), wg_ref, n)[HALO:HALO + tr] + bg_ref[...]
        cv = _conv3(_extend(v_ref, vp_ref, vn_ref, r, nr), wv_ref, n)[HALO:HALO + tr] + bv_ref[...]
        o_ref[...] = (cg * jax.nn.sigmoid(cg) * cv).astype(o_ref.dtype)

        if plan is not None:
            @pl.when(jnp.logical_and(j == nc - 1, r == nr - 1))
            def _():
                plan.forward()
                plan.finish()

    w_spec = lambda shift: pl.BlockSpec((None, 3, tc), lambda j, r: (layer, 0, j + shift))
    b_spec = lambda shift: pl.BlockSpec((None, 1, tc), lambda j, r: (layer, 0, j + shift))
    act_shape = jax.ShapeDtypeStruct((seq, f), BF16)
    act_spec = pl.BlockSpec((tr, tc), lambda j, r: (r, j))
    in_specs = _halo_specs(tr, tc, seq, lambda j: j) * 2 + [w_spec(0), w_spec(nc), b_spec(0), b_spec(nc)]
    operands = (ug, ug, ug, uv, uv, uv, cw, cw, cb, cb)
    if plan is None:
        return pl.pallas_call(body, out_shape=act_shape, grid=(nc, nr), in_specs=in_specs, out_specs=act_spec,
                              compiler_params=_cp(("parallel", "parallel")), name=name)(*operands)
    return pl.pallas_call(
        body, out_shape=(act_shape, *plan.out_shape), grid=(nc, nr), in_specs=in_specs + [ANY] * (2 * nt),
        out_specs=(act_spec, *([ANY] * nt)), scratch_shapes=plan.scratch,
        input_output_aliases={len(operands) + nt + t: 1 + t for t in range(nt)},
        compiler_params=_cp(("arbitrary", "arbitrary")), name=name)(*operands, *shards, *fulls)


def _conv_gate_bwd(ug, uv, dact, cw, cb, layer, name):
    seq, f = ug.shape
    tc = 256
    tr = _pick(seq, (512, 256))
    nc, nr = f // tc, seq // tr
    n = tr + 2 * HALO

    def body(g_ref, gp_ref, gn_ref, v_ref, vp_ref, vn_ref, d_ref, dp_ref, dn_ref, wg_ref, wv_ref, bg_ref, bv_ref,
             dug_ref, duv_ref, dwg_ref, dwv_ref):
        r = pl.program_id(1)
        eg = _extend(g_ref, gp_ref, gn_ref, r, nr)
        ev = _extend(v_ref, vp_ref, vn_ref, r, nr)
        da = _extend(d_ref, dp_ref, dn_ref, r, nr)
        eg3 = (pltpu.roll(eg, 1, axis=0), eg, pltpu.roll(eg, n - 1, axis=0))
        ev3 = (pltpu.roll(ev, 1, axis=0), ev, pltpu.roll(ev, n - 1, axis=0))
        cg = eg3[0] * wg_ref[0:1, :] + eg3[1] * wg_ref[1:2, :] + eg3[2] * wg_ref[2:3, :] + bg_ref[...]
        cv = ev3[0] * wv_ref[0:1, :] + ev3[1] * wv_ref[1:2, :] + ev3[2] * wv_ref[2:3, :] + bv_ref[...]
        sg = jax.nn.sigmoid(cg)
        dcv = da * (cg * sg)
        dcg = da * cv * (sg * (1.0 + cg * (1.0 - sg)))

        def back(dc, e3, w_ref, du_ref, dw_ref):
            du = (pltpu.roll(dc, n - 1, axis=0) * w_ref[0:1, :] + dc * w_ref[1:2, :]
                  + pltpu.roll(dc, 1, axis=0) * w_ref[2:3, :])
            du_ref[...] = du[HALO:HALO + tr].astype(du_ref.dtype)
            dcm = dc[HALO:HALO + tr]
            taps = [jnp.sum(dcm * e[HALO:HALO + tr], axis=0, keepdims=True) for e in e3] + [
                    jnp.sum(dcm, axis=0, keepdims=True)]
            part = jnp.concatenate(taps + [jnp.zeros((4, tc), F32)], axis=0)

            @pl.when(r == 0)
            def _():
                dw_ref[...] = part

            @pl.when(r > 0)
            def _():
                dw_ref[...] += part

        back(dcg, eg3, wg_ref, dug_ref, dwg_ref)
        back(dcv, ev3, wv_ref, duv_ref, dwv_ref)

    w_spec = lambda shift: pl.BlockSpec((None, 3, tc), lambda j, r: (layer, 0, j + shift))
    b_spec = lambda shift: pl.BlockSpec((None, 1, tc), lambda j, r: (layer, 0, j + shift))
    out_rows = pl.BlockSpec((tr, tc), lambda j, r: (r, j))
    out_acc = pl.BlockSpec((8, tc), lambda j, r: (0, j))
    return pl.pallas_call(
        body,
        out_shape=(jax.ShapeDtypeStruct((seq, f), BF16), jax.ShapeDtypeStruct((seq, f), BF16),
                   jax.ShapeDtypeStruct((8, f), F32), jax.ShapeDtypeStruct((8, f), F32)),
        grid=(nc, nr),
        in_specs=_halo_specs(tr, tc, seq, lambda j: j) * 3 + [w_spec(0), w_spec(nc), b_spec(0), b_spec(nc)],
        out_specs=(out_rows, out_rows, out_acc, out_acc),
        compiler_params=_cp(("parallel", "arbitrary")), name=name)(ug, ug, ug, uv, uv, uv, dact, dact, dact, cw, cw, cb, cb)


def _pool_count(g, r, tr, n, seq):
    half = jnp.left_shift(1, g)
    t = r * tr - HALO + lax.broadcasted_iota(jnp.int32, (n, 1), 0)
    cnt = jnp.minimum(t + half, seq) - jnp.maximum(t - half, 0)
    return jnp.maximum(cnt, 1).astype(F32)


def _by_group(g, levels):
    out = levels[3]
    for i in (2, 1, 0):
        out = jnp.where(g == i, levels[i], out)
    return out


def _pool_mixed(e, g, cnt, n):
    w2 = e + pltpu.roll(e, 1, axis=0)
    w4 = pltpu.roll(w2, 1, axis=0) + pltpu.roll(w2, n - 1, axis=0)
    w8 = pltpu.roll(w4, 2, axis=0) + pltpu.roll(w4, n - 2, axis=0)
    w16 = pltpu.roll(w8, 4, axis=0) + pltpu.roll(w8, n - 4, axis=0)
    return _by_group(g, (w2, w4, w8, w16)) / cnt - e


def _pool_fwd(hp, xres, pw, scale, name, plan, shards, fulls):
    seq, d = hp.shape
    tc = POOL_GROUP_W
    tr = _pick(seq, (HALO_TR, 512, 256))
    nr = seq // tr
    n = tr + 2 * HALO
    nt = plan.nt

    def body(h_ref, hp_ref, hn_ref, x_ref, w_ref, s_ref, *rest):
        o_ref = rest[2 * nt]
        plan.bind(rest[:nt], rest[2 * nt + 1:3 * nt + 1], *rest[3 * nt + 1:])
        g, r = pl.program_id(0), pl.program_id(1)

        def compute():
            e = _extend(h_ref, hp_ref, hn_ref, r, nr)
            mixed = _pool_mixed(e, g, _pool_count(g, r, tr, n, seq), n)[HALO:HALO + tr]
            y = jnp.dot(mixed.astype(BF16), w_ref[...], preferred_element_type=F32)
            o_ref[...] = x_ref[...] + y * s_ref[...]

        _hosted(plan, POOL_GROUPS * nr, g * nr + r, compute)

    operands = (hp, hp, hp, xres, pw, scale)
    return pl.pallas_call(
        body, out_shape=(jax.ShapeDtypeStruct((seq, d), F32), *plan.out_shape), grid=(POOL_GROUPS, nr),
        in_specs=_halo_specs(tr, tc, seq, lambda j: j) + [
            pl.BlockSpec((tr, tc), lambda j, r: (r, j)), pl.BlockSpec((None, tc, tc), lambda j, r: (j, 0, 0)),
            pl.BlockSpec((1, tc), lambda j, r: (0, j))] + [ANY] * (2 * nt),
        out_specs=(pl.BlockSpec((tr, tc), lambda j, r: (r, j)), *([ANY] * nt)), scratch_shapes=plan.scratch,
        input_output_aliases={len(operands) + nt + t: 1 + t for t in range(nt)},
        compiler_params=_cp(("arbitrary", "arbitrary")), name=name)(*operands, *shards, *fulls)


def _pool_bwd(hp, dy, pw, scale, name):
    seq, d = hp.shape
    tc = POOL_GROUP_W
    tr = _pick(seq, (HALO_TR, 512, 256))
    nr = seq // tr
    n = tr + 2 * HALO

    def body(h_ref, hp_ref, hn_ref, d_ref, dp_ref, dn_ref, w_ref, s_ref, dh_ref, dw_ref, ds_ref):
        g, r = pl.program_id(0), pl.program_id(1)
        cnt = _pool_count(g, r, tr, n, seq)
        e = _extend(h_ref, hp_ref, hn_ref, r, nr)
        mixed = _pool_mixed(e, g, cnt, n)[HALO:HALO + tr].astype(BF16)
        dye = _extend(d_ref, dp_ref, dn_ref, r, nr)
        dyp = (dye * s_ref[...]).astype(BF16)
        dmixed = lax.dot_general(dyp, w_ref[...], (((1,), (1,)), ((), ())), preferred_element_type=F32)
        dwin = dmixed / cnt
        m2 = dwin + pltpu.roll(dwin, n - 1, axis=0)
        m4 = pltpu.roll(m2, 1, axis=0) + pltpu.roll(m2, n - 1, axis=0)
        m8 = pltpu.roll(m4, 2, axis=0) + pltpu.roll(m4, n - 2, axis=0)
        m16 = pltpu.roll(m8, 4, axis=0) + pltpu.roll(m8, n - 4, axis=0)
        dh_ref[...] = (_by_group(g, (m2, m4, m8, m16)) - dmixed)[HALO:HALO + tr]
        ypre = jnp.dot(mixed, w_ref[...], preferred_element_type=F32)
        dsp = jnp.sum(d_ref[...] * ypre, axis=0, keepdims=True)
        dwp = lax.dot_general(mixed, dyp[HALO:HALO + tr], (((0,), (0,)), ((), ())), preferred_element_type=F32)

        @pl.when(r == 0)
        def _():
            dw_ref[...] = dwp
            ds_ref[...] = dsp

        @pl.when(r > 0)
        def _():
            dw_ref[...] += dwp
            ds_ref[...] += dsp

    return pl.pallas_call(
        body,
        out_shape=(jax.ShapeDtypeStruct((seq, d), F32), jax.ShapeDtypeStruct((POOL_GROUPS, tc, tc), F32),
                   jax.ShapeDtypeStruct((1, d), F32)),
        grid=(POOL_GROUPS, nr),
        in_specs=_halo_specs(tr, tc, seq, lambda j: j) * 2 + [
            pl.BlockSpec((None, tc, tc), lambda j, r: (j, 0, 0)), pl.BlockSpec((1, tc), lambda j, r: (0, j))],
        out_specs=(pl.BlockSpec((tr, tc), lambda j, r: (r, j)), pl.BlockSpec((None, tc, tc), lambda j, r: (j, 0, 0)),
                   pl.BlockSpec((1, tc), lambda j, r: (0, j))),
        compiler_params=_cp(("parallel", "arbitrary")), name=name)(hp, hp, hp, dy, dy, dy, pw, scale)


def _adamw_math(w, g, m, v):
    m = ADAM_B1 * m + (1.0 - ADAM_B1) * g
    v = ADAM_B2 * v + (1.0 - ADAM_B2) * (g * g)
    m_hat = m / (1.0 - ADAM_B1 ** ADAM_STEP)
    v_hat = v / (1.0 - ADAM_B2 ** ADAM_STEP)
    delta = -ADAM_LR * (m_hat / (jnp.sqrt(v_hat) + ADAM_EPS) + ADAM_WD * w)
    return delta, m, v


def _adamw(w, ga, gb, m, v, name):
    rows, cols = w.shape
    tr = _pick(rows, (256, 128, 64, 32, 16, 8))
    two = gb is not None

    def body(*refs):
        if two:
            w_ref, ga_ref, gb_ref, m_ref, v_ref, g_out, d_out, m_out, v_out = refs
            g = ga_ref[...] + gb_ref[...]
        else:
            w_ref, ga_ref, m_ref, v_ref, g_out, d_out, m_out, v_out = refs
            g = ga_ref[...]
        delta, m, v = _adamw_math(w_ref[...], g, m_ref[...], v_ref[...])
        g_out[...] = g
        d_out[...] = delta
        m_out[...] = m
        v_out[...] = v

    spec = pl.BlockSpec((tr, cols), lambda i: (i, 0))
    ops = [w, ga] + ([gb] if two else []) + [m, v]
    return pl.pallas_call(
        body, out_shape=tuple(jax.ShapeDtypeStruct((rows, cols), F32) for _ in range(4)), grid=(rows // tr,),
        in_specs=[spec] * len(ops), out_specs=(spec,) * 4, compiler_params=_cp(("parallel",)), name=name)(*ops)


def _sum4(parts, name):
    _, rows, cols = parts.shape
    tr = _pick(rows, (256, 128, 64, 32, 16))

    def body(p_ref, o_ref):
        acc = p_ref[0].astype(F32)
        for kk in range(1, 4):
            acc = acc + p_ref[kk].astype(F32)
        o_ref[...] = acc

    return pl.pallas_call(
        body, out_shape=jax.ShapeDtypeStruct((rows, cols), F32), grid=(rows // tr,),
        in_specs=[pl.BlockSpec((4, tr, cols), lambda i: (0, i, 0))], out_specs=pl.BlockSpec((tr, cols), lambda i: (i, 0)),
        compiler_params=_cp(("parallel",)), name=name)(parts)


def _place():
    x, y, c = lax.axis_index("x"), lax.axis_index("y"), lax.axis_index("c")
    chips = [(1 - x, y), (x, 1 - y), (1 - x, 1 - y)]
    return x, y, c, chips


def _window(ref, axis, j, size, c=None, half=None, lead=(), layers=slice(None)):
    if axis == "r":
        if c is None:
            return ref.at[lead + (layers, pl.ds(pl.multiple_of(j * size, 32), size), slice(None))]
        return ref.at[lead + (layers, pl.ds(pl.multiple_of(j * size + c * half, 32), half), slice(None))]
    cols = pl.ds(pl.multiple_of(j * size, LANES), size)
    if c is None:
        return ref.at[lead + (layers, slice(None), cols)]
    return ref.at[lead + (layers, pl.ds(pl.multiple_of(c * half, 32), half), cols)]


class _Gather:
    def __init__(self, shards, axes, layers=None):
        self.nt, self.axes = len(shards), axes
        self.layers = layers or [slice(None)] * self.nt
        self.out_shape, self.sizes, self.halves = [], [], []
        for s, ax in zip(shards, axes):
            l, rs, cs = s.shape
            self.out_shape.append(jax.ShapeDtypeStruct((l, 4 * rs, cs) if ax == "r" else (l, rs, 4 * cs), s.dtype))
            self.sizes.append(rs if ax == "r" else cs)
            self.halves.append(rs // 2)
        self.scratch = [pltpu.SemaphoreType.DMA((6 * self.nt,)), pltpu.SemaphoreType.DMA((6 * self.nt,)),
                        pltpu.SemaphoreType.DMA((self.nt,))]

    def bind(self, src, dst, send_sems, recv_sems, local_sems):
        self.src, self.dst, self.send_sems, self.recv_sems, self.local_sems = src, dst, send_sems, recv_sems, local_sems

    def _win(self, t, j, core=None):
        return _window(self.dst[t], self.axes[t], j, self.sizes[t], core, self.halves[t], layers=self.layers[t])

    def _ici(self, t, kk, origin):
        _, _, c, chips = _place()
        px, py = chips[kk]
        half = self.src[t].at[self.layers[t], pl.ds(pl.multiple_of(c * self.halves[t], 16), self.halves[t]), :]
        return pltpu.make_async_remote_copy(
            src_ref=half, dst_ref=self._win(t, origin, c), send_sem=self.send_sems.at[t * 3 + kk],
            recv_sem=self.recv_sems.at[t * 3 + kk], device_id=(px, py, c), device_id_type=MESH)

    def _d2d(self, t, kk, origin, core):
        x, y, c, _ = _place()
        k2 = 3 * self.nt + t * 3 + kk
        return pltpu.make_async_remote_copy(
            src_ref=self._win(t, origin, core), dst_ref=self._win(t, origin, core), send_sem=self.send_sems.at[k2],
            recv_sem=self.recv_sems.at[k2], device_id=(x, y, 1 - c), device_id_type=MESH)

    def _local(self, t):
        x, y, _, _ = _place()
        return pltpu.make_async_copy(self.src[t].at[self.layers[t]], self._win(t, 2 * x + y), self.local_sems.at[t])

    def _each(self):
        _, _, _, chips = _place()
        for t in range(self.nt):
            for kk in range(3):
                px, py = chips[kk]
                yield t, kk, 2 * px + py

    def start(self):
        x, y, _, _ = _place()
        for t in range(self.nt):
            self._local(t).start()
        for t, kk, _ in self._each():
            self._ici(t, kk, 2 * x + y).start()

    def forward(self):
        _, _, c, _ = _place()
        for t, kk, origin in self._each():
            self._ici(t, kk, origin).wait_recv()
            self._d2d(t, kk, origin, c).start()

    def finish(self):
        x, y, c, _ = _place()
        for t, kk, origin in self._each():
            self._d2d(t, kk, origin, 1 - c).wait_recv()
        for t, kk, origin in self._each():
            self._ici(t, kk, 2 * x + y).wait_send()
            self._d2d(t, kk, origin, c).wait_send()
        for t in range(self.nt):
            self._local(t).wait()


class _Scatter:
    def __init__(self, grads, axes):
        self.nt, self.axes = len(grads), axes
        self.out_shape, self.sizes = [], []
        for gr, ax in zip(grads, axes):
            l, r, cc = gr.shape
            self.out_shape.append(jax.ShapeDtypeStruct((4, l, r // 4, cc) if ax == "r" else (4, l, r, cc // 4), gr.dtype))
            self.sizes.append(r // 4 if ax == "r" else cc // 4)
        self.scratch = [pltpu.SemaphoreType.DMA((3 * self.nt,)), pltpu.SemaphoreType.DMA((3 * self.nt,)),
                        pltpu.SemaphoreType.DMA((self.nt,))]

    def bind(self, src, dst, send_sems, recv_sems, local_sems):
        self.src, self.dst, self.send_sems, self.recv_sems, self.local_sems = src, dst, send_sems, recv_sems, local_sems

    def _copy(self, t, kk, slot):
        x, y, c, chips = _place()
        px, py = chips[kk]
        return pltpu.make_async_remote_copy(
            src_ref=_window(self.src[t], self.axes[t], 2 * px + py, self.sizes[t]), dst_ref=self.dst[t].at[slot],
            send_sem=self.send_sems.at[t * 3 + kk], recv_sem=self.recv_sems.at[t * 3 + kk],
            device_id=(px, py, c), device_id_type=MESH)

    def _local(self, t):
        x, y, _, _ = _place()
        me = 2 * x + y
        return pltpu.make_async_copy(_window(self.src[t], self.axes[t], me, self.sizes[t]), self.dst[t].at[me],
                                     self.local_sems.at[t])

    def start(self):
        x, y, _, _ = _place()
        for t in range(self.nt):
            self._local(t).start()
            for kk in range(3):
                self._copy(t, kk, 2 * x + y).start()

    def finish(self):
        _, _, _, chips = _place()
        for t in range(self.nt):
            for kk in range(3):
                px, py = chips[kk]
                self._copy(t, kk, 2 * px + py).wait_recv()
        for t in range(self.nt):
            for kk in range(3):
                px, py = chips[kk]
                self._copy(t, kk, 2 * px + py).wait_send()
            self._local(t).wait()


def _comm_call(plans, operands, name):
    nts = [p.nt for p in plans]
    n_in, n_sem = sum(nts), [len(p.scratch) for p in plans]

    def body(*refs):
        pos_in, pos_out, pos_sem = 0, n_in, 2 * n_in
        for p, nt, ns in zip(plans, nts, n_sem):
            p.bind(refs[pos_in:pos_in + nt], refs[pos_out:pos_out + nt], *refs[pos_sem:pos_sem + ns])
            pos_in, pos_out, pos_sem = pos_in + nt, pos_out + nt, pos_sem + ns
        for p in plans:
            p.start()
        for p in plans:
            if hasattr(p, "forward"):
                p.forward()
        for p in plans:
            p.finish()

    return pl.pallas_call(
        body, out_shape=tuple(s for p in plans for s in p.out_shape), in_specs=[ANY] * n_in,
        out_specs=tuple([ANY] * n_in), scratch_shapes=[s for p in plans for s in p.scratch],
        name=name)(*[a for ops in operands for a in ops])


class _GatherAll:
    FLIPS = [f for f in itertools.product((0, 1), repeat=3) if any(f)]

    def __init__(self, pack):
        self.nt = 1
        self.out_shape = [jax.ShapeDtypeStruct((8,) + pack.shape, pack.dtype)]
        self.scratch = [pltpu.SemaphoreType.DMA((7,)), pltpu.SemaphoreType.DMA((7,)), pltpu.SemaphoreType.DMA((1,))]

    def bind(self, src, dst, send_sems, recv_sems, local_sems):
        self.src, self.dst, self.send_sems, self.recv_sems, self.local_sems = src[0], dst[0], send_sems, recv_sems, local_sems

    def _copy(self, kk, mine):
        x, y, c, _ = _place()
        px, py, pc = (1 - v if fl else v for v, fl in zip((x, y, c), self.FLIPS[kk]))
        slot = 4 * x + 2 * y + c if mine else 4 * px + 2 * py + pc
        return pltpu.make_async_remote_copy(src_ref=self.src, dst_ref=self.dst.at[slot], send_sem=self.send_sems.at[kk],
                                            recv_sem=self.recv_sems.at[kk], device_id=(px, py, pc), device_id_type=MESH)

    def _local(self):
        x, y, c, _ = _place()
        return pltpu.make_async_copy(self.src, self.dst.at[4 * x + 2 * y + c], self.local_sems.at[0])

    def start(self):
        self._local().start()
        for kk in range(7):
            self._copy(kk, True).start()

    def finish(self):
        for kk in range(7):
            self._copy(kk, False).wait_recv()
        for kk in range(7):
            self._copy(kk, True).wait_send()
        self._local().wait()


class _Swap:
    def __init__(self, arrs):
        self.nt = len(arrs)
        self.out_shape = [jax.ShapeDtypeStruct(a.shape, a.dtype) for a in arrs]
        self.scratch = [pltpu.SemaphoreType.DMA((self.nt,)), pltpu.SemaphoreType.DMA((self.nt,))]

    def bind(self, src, dst, send_sems, recv_sems):
        self.src, self.dst, self.send_sems, self.recv_sems = src, dst, send_sems, recv_sems

    def _copy(self, t):
        x, y, c, _ = _place()
        return pltpu.make_async_remote_copy(src_ref=self.src[t], dst_ref=self.dst[t], send_sem=self.send_sems.at[t],
                                            recv_sem=self.recv_sems.at[t], device_id=(x, y, 1 - c), device_id_type=MESH)

    def start(self):
        for t in range(self.nt):
            self._copy(t).start()

    def finish(self):
        for t in range(self.nt):
            self._copy(t).wait()


def _sum8(packs, name):
    def body(p_ref, o_ref):
        acc = p_ref[0]
        for dev in range(1, 8):
            acc = acc + p_ref[dev]
        o_ref[...] = acc

    return pl.pallas_call(body, out_shape=jax.ShapeDtypeStruct(packs.shape[1:], F32), name=name)(packs)


def _pack(arrs):
    flat = jnp.concatenate([a.reshape(-1).astype(F32) for a in arrs])
    rows = -(-flat.shape[0] // (8 * LANES)) * 8
    return jnp.pad(flat, (0, rows * LANES - flat.shape[0])).reshape(rows, LANES)


def _unpack(flat, shapes):
    out, pos = [], 0
    for shp in shapes:
        size = 1
        for s in shp:
            size *= s
        out.append(flat[pos:pos + size].reshape(shp))
        pos += size
    return out


BIG = ("attn_w_qkv", "attn_w_o", "pool_w", "xattn_w_q", "xattn_w_kv", "xattn_w_o", "ffn_w_up", "ffn_w_down")
BIG_AXIS = ("c", "r", "r", "r", "c", "r", "c", "r")
SMALL_REPL = ("attn_norm", "attn_q_gain", "attn_k_gain", "xattn_norm", "mem_norm", "ffn_norm", "ffn_conv_b", "final_norm")
SMALL_SHARD = ("pool_norm", "pool_scale", "ffn_conv_w")
ORDER = ("attn_norm", "attn_w_qkv", "attn_q_gain", "attn_k_gain", "attn_w_o", "pool_norm", "pool_w", "pool_scale",
         "xattn_norm", "mem_norm", "xattn_w_q", "xattn_w_kv", "xattn_w_o", "ffn_norm", "ffn_w_up", "ffn_conv_w",
         "ffn_conv_b", "ffn_w_down", "final_norm")


def _step(x, mem, tgt, w, m, v):
    seq, d = x.shape
    xi, yi, ci = lax.axis_index("x"), lax.axis_index("y"), lax.axis_index("c")
    chip = 2 * xi + yi
    dff = w["ffn_w_down"].shape[1] * 4
    n_layers = w["ffn_norm"].shape[0]

    def as3d(a):
        return a.reshape(a.shape[-3:])
    shards = [as3d(w[nm]).astype(BF16) for nm in BIG]
    small_in = [w[nm] for nm in SMALL_SHARD]
    small_pack = _pack(small_in)
    conv_b = w["ffn_conv_b"].reshape(n_layers, 1, -1)
    tabs = _rope_tables(seq)
    qg2 = jnp.tile(w["attn_q_gain"], (1, 2))
    kg2 = jnp.tile(w["attn_k_gain"], (1, 2))
    mm = functools.partial(_mm)

    saved = {}
    x0 = x
    h0, wq = _rms_fwd(x0, w["attn_norm"], BF16, "rms_attn", _Gather(shards[:1], BIG_AXIS[:1]), shards[:1])
    qkv = mm(h0, wq, "nn", b_l=0, out_dtype=F32, name="mm_qkv")
    q_r, k_r, k_t, v_b, v_t, small_all = _qk_prep(qkv, qg2, kg2, tabs, _GatherAll(small_pack), [small_pack], "qk_prep")
    per_chip = [_unpack(small_all[2 * j].reshape(-1), [a.shape for a in small_in]) for j in range(4)]
    pool_norm, pool_scale, conv_w = (jnp.concatenate([per_chip[j][i] for j in range(4)], axis=-1) for i in range(3))
    first = [slice(None)] * 5 + [slice(0, 1)] * 2
    o_at, lse, wo, wp, wxq, wxkv, wxo, wup, wdn = _flash_fwd(
        q_r, k_r, v_t, _Gather(shards[1:], BIG_AXIS[1:], first), shards[1:], "flash_fwd")
    ffn_w = {"up": wup, "down": wdn}
    x1, hq0 = mm(o_at, wo, "nn", b_l=0, res=x0, out_dtype=F32, norm_out=(w["xattn_norm"][0:1], BF16), name="mm_attn_o")

    def xattn_fwd(l, xin, hq):
        mn = _rms_fwd(mem, w["mem_norm"][l:l + 1], BF16, f"rms_mem{l}")
        xq = mm(hq, wxq, "nn", b_l=l, scale=X_HEAD_DIM ** -0.5, out_dtype=BF16, name=f"mm_xq{l}")
        kv = mm(mn, wxkv, "nn", b_l=l, out_dtype=BF16, name=f"mm_xkv{l}")
        xo = _xattn_fwd(xq, kv, f"xattn_fwd{l}")
        saved[f"x{l}"] = (hq, mn, xq, kv, xo)
        return mm(xo, wxo, "nn", b_l=l, res=xin, out_dtype=F32, norm_out=(w["ffn_norm"][l:l + 1], BF16), name=f"mm_xo{l}")

    def ffn_fwd(l, xin, hf, norm_out):
        ug = mm(hf, ffn_w["up"], "nn", b_l=l, n=dff, out_dtype=BF16, name=f"mm_up_g{l}")
        uv = mm(hf, ffn_w["up"], "nn", b_l=l, n=dff, b_off=(0, dff), out_dtype=BF16, name=f"mm_up_v{l}")
        if l == 0:
            rest = _Gather(shards[6:7], BIG_AXIS[6:7], [slice(1, 2)])
            act, ffn_w["up"] = _conv_gate_fwd(ug, uv, conv_w, conv_b, l, f"conv_gate{l}", rest, shards[6:7], [ffn_w["up"]])
        else:
            act = _conv_gate_fwd(ug, uv, conv_w, conv_b, l, f"conv_gate{l}")
        saved[f"f{l}"] = (hf, ug, uv, act)
        return mm(act, ffn_w["down"], "nn", b_l=l, res=xin, out_dtype=F32, norm_out=norm_out, name=f"mm_down{l}")

    x2, hf0 = xattn_fwd(0, x1, hq0)
    x3, hp = ffn_fwd(0, x2, hf0, (pool_norm, F32))
    x4, ffn_w["down"] = _pool_fwd(hp, x3, wp, pool_scale, "pool_fwd", _Gather(shards[7:], BIG_AXIS[7:], [slice(1, 2)]),
                                  shards[7:], [ffn_w["down"]])
    x5, hf1 = xattn_fwd(1, x4, _rms_fwd(x4, w["xattn_norm"][1:2], BF16, "rms_xq1"))
    xs = [x0, x1, x2, x3, x4, x5, ffn_fwd(1, x5, hf1, None)]
    dres, g_final, loss = _final_loss(xs[6], w["final_norm"].reshape(1, d), tgt, "final_loss")

    grads = {}
    gbuf = {}

    def dw(nm, a, b, layer, full, off=(0, 0), n=None, tn=None):
        gbuf[nm] = _mm(a, b, "tn", out_dtype=BF16, out_full=full, out_l=layer, out_off=off, n=n, tn=tn,
                       alias=gbuf.get(nm), name=f"dw_{nm}{layer}_{off[1]}")

    def ffn_bwd(l, xin, dres):
        hf, ug, uv, act = saved[f"f{l}"]
        wup, wdn = ffn_w["up"], ffn_w["down"]
        dw("ffn_w_down", act, dres, l, wdn.shape)
        dact = _mm(dres, wdn, "nt", b_l=l, out_dtype=BF16, name=f"mm_dact{l}")
        dug, duv, dwg, dwv = _conv_gate_bwd(ug, uv, dact, conv_w, conv_b, l, f"conv_gate_bwd{l}")
        dw("ffn_w_up", hf, dug, l, wup.shape, tn=1408)
        dw("ffn_w_up", hf, duv, l, wup.shape, off=(0, dff), tn=1408)
        dhf = _mm(dug, wup, "nt", b_l=l, n=d, out_dtype=F32, name=f"mm_dhf_g{l}")
        dres, dg = _mm(duv, wup, "nt", b_l=l, n=d, b_off=(0, dff), res=dhf, out_dtype=F32, tm=256,
                       norm_bwd=(xin, w["ffn_norm"][l:l + 1], dres), name=f"mm_dhf_v{l}")
        return dres, dg, jnp.concatenate([dwg[:3], dwv[:3]], axis=1), jnp.concatenate([dwg[3], dwv[3]], axis=0)

    def xattn_bwd(l, xin, dres):
        hq, mn, xq, kv, xo = saved[f"x{l}"]
        dw("xattn_w_o", xo, dres, l, wxo.shape)
        dxo = _mm(dres, wxo, "nt", b_l=l, out_dtype=BF16, name=f"mm_dxo{l}")
        dq, dkv = _xattn_bwd(xq, kv, dxo, f"xattn_bwd{l}")
        dw("xattn_w_q", hq, dq, l, wxq.shape)
        dw("xattn_w_kv", mn, dkv, l, wxkv.shape)
        dmn = _mm(dkv, wxkv, "nt", b_l=l, out_dtype=F32, name=f"mm_dmn{l}")
        _, dg_mem = _rms_bwd(mem, w["mem_norm"][l:l + 1], dmn, None, f"rms_mem_bwd{l}")
        dres, dg = _mm(dq, wxq, "nt", b_l=l, out_dtype=F32, norm_bwd=(xin, w["xattn_norm"][l:l + 1], dres),
                       name=f"mm_dhq{l}")
        return dres, dg, dg_mem

    g_ffn, g_xn, g_mn, g_cw, g_cb = [None] * n_layers, [None] * n_layers, [None] * n_layers, [None] * n_layers, [None] * n_layers
    dres, g_ffn[1], g_cw[1], g_cb[1] = ffn_bwd(1, xs[5], dres)
    dres, g_xn[1], g_mn[1] = xattn_bwd(1, xs[4], dres)
    dhp, g_pw, g_pscale = _pool_bwd(hp, dres, wp, pool_scale, "pool_bwd")
    dres, g_pnorm = _rms_bwd(xs[3], pool_norm, dhp, dres, "rms_pool_bwd")
    dres, g_ffn[0], g_cw[0], g_cb[0] = ffn_bwd(0, xs[2], dres)
    dres, g_xn[0], g_mn[0] = xattn_bwd(0, xs[1], dres)
    dw("attn_w_o", o_at, dres, 0, wo.shape)
    do = _mm(dres, wo, "nt", b_l=0, out_dtype=BF16, name="mm_do")
    gbuf["pool_w"] = g_pw.astype(BF16)
    early = [gbuf[nm] for nm in BIG[1:]]
    dq_r, dk_r, dv, *recv_early = _flash_bwd(q_r, k_r, k_t, v_b, do, o_at, lse, _Scatter(early, BIG_AXIS[1:]), early,
                                             "flash_bwd")
    def sum4(nm, rc):
        return _sum4(rc.reshape(4, -1, rc.shape[-1]), f"sum4_{nm}")
    sums_early = [sum4(nm, rc) for nm, rc in zip(BIG[1:], recv_early)]
    dqkv, dqg, dkg, *others_early = _qk_prep_bwd(qkv, dq_r, dk_r, dv, qg2, kg2, tabs, _Swap(sums_early), sums_early,
                                                 "qk_prep_bwd")
    dw("attn_w_qkv", h0, dqkv, 0, wq.shape)
    grad_x, g_an = _mm(dqkv, wq, "nt", b_l=0, out_dtype=F32, norm_bwd=(x0, w["attn_norm"], dres), name="mm_dh0")

    small_g = {
        "attn_norm": g_an, "attn_q_gain": dqg[:, :HEAD_DIM] + dqg[:, HEAD_DIM:], "attn_k_gain": dkg[:, :HEAD_DIM] + dkg[:, HEAD_DIM:],
        "xattn_norm": jnp.concatenate(g_xn, axis=0), "mem_norm": jnp.concatenate(g_mn, axis=0),
        "ffn_norm": jnp.concatenate(g_ffn, axis=0), "ffn_conv_b": jnp.stack(g_cb, axis=0), "final_norm": g_final.reshape(d),
        "pool_norm": g_pnorm, "pool_scale": g_pscale, "ffn_conv_w": jnp.stack(g_cw, axis=0)}
    names = SMALL_REPL + SMALL_SHARD
    small_pack = _pack([loss[0, :1]] + [small_g[nm] for nm in names])
    late = [gbuf[nm] for nm in BIG[:1]]
    small_all, recv_late = _comm_call([_GatherAll(small_pack), _Scatter(late, BIG_AXIS[:1])], [[small_pack], late],
                                      "reduce_small_scatter_qkv")
    total = _sum8(small_all, "sum_small")
    parts = _unpack(total.reshape(-1), [(1,)] + [small_g[nm].shape for nm in names])
    loss_out = parts[0][0]
    for nm, g in zip(names, parts[1:]):
        if nm in SMALL_SHARD:
            size = w[nm].shape[-1]
            g = lax.dynamic_slice_in_dim(g, chip * size, size, axis=g.ndim - 1)
        grads[nm] = g.reshape(w[nm].shape)

    packed = [_pack([src[nm] for nm in names]) for src in (w, grads, m, v)]
    _, sd, sm, sv = _adamw(packed[0], packed[1], None, packed[2], packed[3], "adamw_small")
    shapes = [w[nm].shape for nm in names]
    delta = dict(zip(names, _unpack(sd.reshape(-1), shapes)))
    new_m = dict(zip(names, _unpack(sm.reshape(-1), shapes)))
    new_v = dict(zip(names, _unpack(sv.reshape(-1), shapes)))

    sums_late = [sum4(BIG[0], recv_late)]
    others_late = _comm_call([_Swap(sums_late)], [sums_late], "swap_qkv")
    for nm, mine, other in zip(BIG, sums_late + sums_early, list(others_late) + others_early):
        cols = mine.shape[-1]
        outs = _adamw(w[nm].reshape(-1, cols), mine, other, m[nm].reshape(-1, cols), v[nm].reshape(-1, cols), f"adamw_{nm}")
        grads[nm], delta[nm], new_m[nm], new_v[nm] = (o.reshape(w[nm].shape) for o in outs)

    return loss_out, grad_x, grads, delta, new_m, new_v


def kernel(x, mem, attn_norm, attn_w_qkv, attn_q_gain, attn_k_gain, attn_w_o, pool_norm, pool_w, pool_scale, xattn_norm, mem_norm, xattn_w_q, xattn_w_kv, xattn_w_o, ffn_norm, ffn_w_up, ffn_conv_w, ffn_conv_b, ffn_w_down, final_norm, loss_target, m_attn_norm, m_attn_w_qkv, m_attn_q_gain, m_attn_k_gain, m_attn_w_o, m_pool_norm, m_pool_w, m_pool_scale, m_xattn_norm, m_mem_norm, m_xattn_w_q, m_xattn_w_kv, m_xattn_w_o, m_ffn_norm, m_ffn_w_up, m_ffn_conv_w, m_ffn_conv_b, m_ffn_w_down, m_final_norm, v_attn_norm, v_attn_w_qkv, v_attn_q_gain, v_attn_k_gain, v_attn_w_o, v_pool_norm, v_pool_w, v_pool_scale, v_xattn_norm, v_mem_norm, v_xattn_w_q, v_xattn_w_kv, v_xattn_w_o, v_ffn_norm, v_ffn_w_up, v_ffn_conv_w, v_ffn_conv_b, v_ffn_w_down, v_final_norm):
    given = dict(locals())
    w = {nm: given[nm] for nm in ORDER}
    m = {nm: given["m_" + nm] for nm in ORDER}
    v = {nm: given["v_" + nm] for nm in ORDER}
    seq, d = x.shape[1], x.shape[2]
    loss, grad_x, grads, delta, new_m, new_v = _step(
        x.reshape(seq, d), mem.reshape(mem.shape[1], d), loss_target.reshape(seq, d), w, m, v)
    return (loss, grad_x.reshape(x.shape), *[grads[nm] for nm in ORDER], *[delta[nm] for nm in ORDER],
            *[new_m[nm] for nm in ORDER], *[new_v[nm] for nm in ORDER])
```

```python
import functools
import itertools

import jax
import jax.numpy as jnp
from jax import lax
from jax.experimental import pallas as pl
from jax.experimental.pallas import tpu as pltpu

F32, BF16 = jnp.float32, jnp.bfloat16
EPS = 1e-6
GRID_W = 64
ROPE_THETA = 10000.0
HEAD_DIM = 64
N_HEADS = 16
N_KV = 4
X_HEADS = 4
X_HEAD_DIM = 256
POOL_GROUPS = 4
POOL_GROUP_W = 256
HALO = 16
HALO_TR = 2048
LANES = 128
ADAM_LR, ADAM_B1, ADAM_B2, ADAM_EPS, ADAM_WD, ADAM_STEP = 0.001, 0.9, 0.999, 1e-08, 0.01, 10
VMEM_LIMIT = 48 * 1024 * 1024
MESH = pl.DeviceIdType.MESH
NEG = -1e30
LOG2E = 1.4426950408889634
FLASH_TQ, FLASH_TK = 512, 4096
FLASH_SUB = 512
ANY = pl.BlockSpec(memory_space=pl.ANY)


def _cp(sem=None):
    return pltpu.CompilerParams(dimension_semantics=sem, vmem_limit_bytes=VMEM_LIMIT)


def _pick(n, cands):
    for c in cands:
        if c <= n and n % c == 0:
            return c
    return n


def _mm(a, b, mode, *, name, out_dtype, tm=None, tn=None, tk=None, n=None, k=None, b_l=None, b_off=(0, 0),
        res=None, scale=None, out_full=None, out_l=None, out_off=(0, 0), alias=None, norm_out=None, norm_bwd=None):
    if mode == "tn":
        K, M = a.shape
    else:
        M, K = a.shape
    bs = b.shape[-2:]
    if mode == "nn":
        K = k or K
        N = n or bs[1]
    elif mode == "nt":
        N = n or bs[0]
    else:
        N = n or bs[1]
    wide = (1408, 1024, 512, 256, 128)
    if mode == "tn":
        tm = tm or (M if M <= 1024 else _pick(M, wide))
        tk = tk or _pick(K, (2048, 1024, 512, 256, 128))
    else:
        small = K <= 1024 and N <= 1536 and norm_bwd is None
        tm = _pick(M, (tm or (1024 if small else 512), 512, 256, 128))
        tk = tk or (K if K <= 2816 else _pick(K, wide))
    tn = tn or (N if N <= 1536 else _pick(N, wide))
    assert M % tm == 0 and N % tn == 0 and K % tk == 0, (name, M, N, K, tm, tn, tk)
    nk = K // tk
    dims = {"nn": ((1,), (0,)), "nt": ((1,), (1,)), "tn": ((0,), (0,))}[mode]

    j_outer = nk == 1 and mode != "tn"

    def at(f):
        return (lambda j, i, kk: f(i, j, kk)) if j_outer else f

    if mode == "tn":
        a_spec = pl.BlockSpec((tk, tm), at(lambda i, j, kk: (kk, i)))
    else:
        a_spec = pl.BlockSpec((tm, tk), at(lambda i, j, kk: (i, kk)))
    if mode == "nt":
        bb, (d0, d1) = (tn, tk), (b_off[0] // tn, b_off[1] // tk)
        assert b_off[0] % tn == 0 and b_off[1] % tk == 0
        bidx = lambda i, j, kk: (j + d0, kk + d1)
    else:
        bb, (d0, d1) = (tk, tn), (b_off[0] // tk, b_off[1] // tn)
        assert b_off[0] % tk == 0 and b_off[1] % tn == 0
        bidx = lambda i, j, kk: (kk + d0, j + d1)
    if b.ndim == 3:
        b_spec = pl.BlockSpec((None,) + bb, at(lambda i, j, kk: (b_l,) + bidx(i, j, kk)))
    else:
        b_spec = pl.BlockSpec(bb, at(bidx))
    in_specs, operands = [a_spec, b_spec], [a, b]
    if res is not None:
        in_specs.append(pl.BlockSpec((tm, tn), at(lambda i, j, kk: (i, j))))
        operands.append(res)
    aliases = {}
    if alias is not None:
        aliases = {len(operands): 0}
        in_specs.append(ANY)
        operands.append(alias)
    if out_full is None:
        out_shape = jax.ShapeDtypeStruct((M, N), out_dtype)
        out_spec = pl.BlockSpec((tm, tn), at(lambda i, j, kk: (i, j)))
    else:
        assert out_off[0] % tm == 0 and out_off[1] % tn == 0
        o0, o1 = out_off[0] // tm, out_off[1] // tn
        out_shape = jax.ShapeDtypeStruct(out_full, out_dtype)
        out_spec = pl.BlockSpec((None, tm, tn), at(lambda i, j, kk: (out_l, i + o0, j + o1)))
    has_res, has_alias = res is not None, alias is not None
    grid = (N // tn, M // tm, nk) if j_outer else (M // tm, N // tn, nk)
    n_extra = 0
    if norm_out is not None or norm_bwd is not None:
        assert j_outer and tn == N and out_full is None, name
        row = pl.BlockSpec((tm, tn), at(lambda i, j, kk: (i, 0)))
        vec = pl.BlockSpec((1, tn), at(lambda i, j, kk: (0, 0)))
        if norm_out is not None:
            in_specs.append(vec)
            operands.append(norm_out[0])
            n_extra = 1
            out_shape = (out_shape, jax.ShapeDtypeStruct((M, N), norm_out[1]))
            out_spec = (out_spec, row)
        else:
            in_specs += [row, vec, row]
            operands += list(norm_bwd)
            n_extra = 3
            out_shape = (out_shape, jax.ShapeDtypeStruct((1, N), F32))
            out_spec = (out_spec, vec)
    n_out = 1 if n_extra == 0 else 2

    def body(*refs):
        a_ref, b_ref = refs[0], refs[1]
        pos = 2
        res_ref = None
        if has_res:
            res_ref = refs[pos]
            pos += 1
        if has_alias:
            pos += 1
        extra = refs[pos:pos + n_extra]
        pos += n_extra
        o_ref, acc_ref = refs[pos], refs[pos + n_out]
        kk = pl.program_id(2)
        part = lax.dot_general(a_ref[...].astype(BF16), b_ref[...].astype(BF16), (dims, ((), ())),
                               preferred_element_type=F32)

        def finish(acc):
            if scale is not None:
                acc = acc * scale
            if res_ref is not None:
                acc = acc + res_ref[...]
            if norm_out is not None:
                r = lax.rsqrt(jnp.mean(acc * acc, axis=-1, keepdims=True) + EPS)
                refs[pos + 1][...] = (acc * r * extra[0][...]).astype(refs[pos + 1].dtype)
            if norm_bwd is not None:
                x_ref, g_ref, dres_ref = extra
                dg_ref, step = refs[pos + 1], pl.program_id(1)
                xv = x_ref[...]
                r = lax.rsqrt(jnp.mean(xv * xv, axis=-1, keepdims=True) + EPS)
                nv = xv * r
                dgp = jnp.sum(acc * nv, axis=0, keepdims=True)

                @pl.when(step == 0)
                def _():
                    dg_ref[...] = dgp

                @pl.when(step > 0)
                def _():
                    dg_ref[...] += dgp

                dn = acc * g_ref[...]
                acc = dres_ref[...] + r * (dn - nv * jnp.mean(dn * nv, axis=-1, keepdims=True))
            o_ref[...] = acc.astype(o_ref.dtype)

        if nk == 1:
            finish(part)
        else:
            @pl.when(kk == 0)
            def _():
                acc_ref[...] = part

            @pl.when(jnp.logical_and(kk > 0, kk < nk - 1))
            def _():
                acc_ref[...] += part

            @pl.when(kk == nk - 1)
            def _():
                finish(acc_ref[...] + part)

    return pl.pallas_call(
        body, out_shape=out_shape, grid=grid, in_specs=in_specs, out_specs=out_spec,
        scratch_shapes=[pltpu.VMEM((tm, tn) if nk > 1 else (8, 128), F32)], input_output_aliases=aliases,
        compiler_params=_cp(("arbitrary",) * 3 if norm_bwd is not None else ("parallel", "parallel", "arbitrary")),
        name=name)(*operands)


def _hosted(plan, nsteps, step, compute):
    if plan is None:
        return compute()

    @pl.when(step == 0)
    def _():
        plan.start()

    compute()

    @pl.when(step == nsteps - 1)
    def _():
        if hasattr(plan, "forward"):
            plan.forward()
        plan.finish()


def _rms_fwd(x, gain, out_dtype, name, plan=None, sends=()):
    rows, d = x.shape
    tr = _pick(rows, (512, 256))
    nt, nsteps = (plan.nt if plan is not None else 0), rows // tr

    def body(x_ref, g_ref, *rest):
        o_ref = rest[nt]
        if plan is not None:
            plan.bind(rest[:nt], rest[nt + 1:2 * nt + 1], *rest[2 * nt + 1:])

        def compute():
            xv = x_ref[...]
            r = lax.rsqrt(jnp.mean(xv * xv, axis=-1, keepdims=True) + EPS)
            o_ref[...] = (xv * r * g_ref[...]).astype(o_ref.dtype)

        _hosted(plan, nsteps, pl.program_id(0), compute)

    out = jax.ShapeDtypeStruct((rows, d), out_dtype)
    row = pl.BlockSpec((tr, d), lambda i: (i, 0))
    in_specs = [row, pl.BlockSpec((1, d), lambda i: (0, 0))]
    if plan is None:
        return pl.pallas_call(body, out_shape=out, grid=(nsteps,), in_specs=in_specs, out_specs=row,
                              compiler_params=_cp(("parallel",)), name=name)(x, gain)
    return pl.pallas_call(
        body, out_shape=(out, *plan.out_shape), grid=(nsteps,), in_specs=in_specs + [ANY] * nt,
        out_specs=(row, *([ANY] * nt)), scratch_shapes=plan.scratch, compiler_params=_cp(("arbitrary",)),
        name=name)(x, gain, *sends)


def _rms_bwd(x, gain, dh, dres, name):
    rows, d = x.shape
    tr = _pick(rows, (512, 256))
    need_dx = dres is not None

    def body(*refs):
        if need_dx:
            x_ref, g_ref, dh_ref, dres_ref, o_ref, dg_ref = refs
        else:
            x_ref, g_ref, dh_ref, dg_ref = refs
        i = pl.program_id(0)
        xv = x_ref[...]
        dhv = dh_ref[...].astype(F32)
        r = lax.rsqrt(jnp.mean(xv * xv, axis=-1, keepdims=True) + EPS)
        nv = xv * r
        part = jnp.sum(dhv * nv, axis=0, keepdims=True)

        @pl.when(i == 0)
        def _():
            dg_ref[...] = part

        @pl.when(i > 0)
        def _():
            dg_ref[...] += part

        if need_dx:
            dn = dhv * g_ref[...]
            dx = r * (dn - nv * jnp.mean(dn * nv, axis=-1, keepdims=True))
            o_ref[...] = dres_ref[...] + dx

    row_spec = pl.BlockSpec((tr, d), lambda i: (i, 0))
    vec_spec = pl.BlockSpec((1, d), lambda i: (0, 0))
    if need_dx:
        return pl.pallas_call(
            body, out_shape=(jax.ShapeDtypeStruct((rows, d), F32), jax.ShapeDtypeStruct((1, d), F32)),
            grid=(rows // tr,), in_specs=[row_spec, vec_spec, row_spec, row_spec], out_specs=(row_spec, vec_spec),
            compiler_params=_cp(("arbitrary",)), name=name)(x, gain, dh, dres)
    return None, pl.pallas_call(
        body, out_shape=jax.ShapeDtypeStruct((1, d), F32), grid=(rows // tr,),
        in_specs=[row_spec, vec_spec, row_spec], out_specs=vec_spec,
        compiler_params=_cp(("arbitrary",)), name=name)(x, gain, dh)


def _final_loss(x, gain, target, name):
    rows, d = x.shape
    tr = _pick(rows, (512, 256))
    nsteps = rows // tr

    def body(x_ref, g_ref, t_ref, dx_ref, dg_ref, loss_ref, acc_ref):
        i = pl.program_id(0)
        xv = x_ref[...]
        g = g_ref[...]
        r = lax.rsqrt(jnp.mean(xv * xv, axis=-1, keepdims=True) + EPS)
        nv = xv * r
        err = nv * g - t_ref[...]
        dy = err * (1.0 / d)
        dn = dy * g
        dx_ref[...] = r * (dn - nv * jnp.mean(dn * nv, axis=-1, keepdims=True))
        dgp = jnp.sum(dy * nv, axis=0, keepdims=True)
        lp = jnp.sum(err * err, axis=0, keepdims=True)

        @pl.when(i == 0)
        def _():
            dg_ref[...] = dgp
            acc_ref[...] = lp

        @pl.when(i > 0)
        def _():
            dg_ref[...] += dgp
            acc_ref[...] += lp

        @pl.when(i == nsteps - 1)
        def _():
            tot = jnp.sum(acc_ref[...], axis=1, keepdims=True) * (0.5 / d)
            loss_ref[...] = jnp.broadcast_to(tot, loss_ref.shape)

    row_spec = pl.BlockSpec((tr, d), lambda i: (i, 0))
    vec_spec = pl.BlockSpec((1, d), lambda i: (0, 0))
    return pl.pallas_call(
        body, out_shape=(jax.ShapeDtypeStruct((rows, d), F32), jax.ShapeDtypeStruct((1, d), F32),
                         jax.ShapeDtypeStruct((1, LANES), F32)),
        grid=(nsteps,), in_specs=[row_spec, vec_spec, row_spec],
        out_specs=(row_spec, vec_spec, pl.BlockSpec((1, LANES), lambda i: (0, 0))),
        scratch_shapes=[pltpu.VMEM((1, d), F32)], compiler_params=_cp(("arbitrary",)), name=name)(x, gain, target)


def _rope_tables(seq):
    pairs = HEAD_DIM // 4
    lane = jnp.arange(LANES, dtype=jnp.int32) % HEAD_DIM
    by_col, second, pair = lane // (2 * pairs) == 1, (lane % (2 * pairs)) // pairs == 1, lane % pairs
    inv_freq = ROPE_THETA ** (-pair.astype(F32) / pairs)
    t = jnp.arange(seq, dtype=jnp.int32)[:, None]
    pos = jnp.where(by_col[None, :], t % GRID_W, t // GRID_W).astype(F32)
    ang = pos * inv_freq[None, :]
    cos, sin = jnp.cos(ang), jnp.sin(ang)
    return cos, jnp.where(second[None, :], sin, 0.0), jnp.where(second[None, :], 0.0, -sin)


def _pair_norm(xv, lo):
    sq = xv * xv
    s_lo = jnp.sum(jnp.where(lo, sq, 0.0), axis=1, keepdims=True)
    s_hi = jnp.sum(jnp.where(lo, 0.0, sq), axis=1, keepdims=True)
    return lax.rsqrt(jnp.where(lo, s_lo, s_hi) * (1.0 / HEAD_DIM) + EPS)


def _rope(y, c, sp, sm):
    return y * c + pltpu.roll(y, 16, axis=1) * sp + pltpu.roll(y, LANES - 16, axis=1) * sm


def _rope_t(dz, c, sp, sm):
    return dz * c + pltpu.roll(dz * sp, LANES - 16, axis=1) + pltpu.roll(dz * sm, 16, axis=1)


def _qk_prep(qkv, qg2, kg2, tabs, plan, sends, name):
    seq = qkv.shape[0]
    ts = _pick(seq, (512, 256, 128))
    nq, nkp = N_HEADS // 2, N_KV // 2
    qw, kw = N_HEADS * HEAD_DIM, N_KV * HEAD_DIM
    nt, nsteps = plan.nt, seq // ts

    def body(x_ref, qg_ref, kg_ref, c_ref, sp_ref, sm_ref, *rest):
        q_ref, k_ref, kt_ref, v_ref, vt_ref = rest[nt:nt + 5]
        plan.bind(rest[:nt], rest[nt + 5:2 * nt + 5], *rest[2 * nt + 5:])
        _hosted(plan, nsteps, pl.program_id(0), lambda: compute(x_ref, qg_ref, kg_ref, c_ref, sp_ref, sm_ref,
                                                                q_ref, k_ref, kt_ref, v_ref, vt_ref))

    def compute(x_ref, qg_ref, kg_ref, c_ref, sp_ref, sm_ref, q_ref, k_ref, kt_ref, v_ref, vt_ref):
        lo = lax.broadcasted_iota(jnp.int32, (ts, LANES), 1) < HEAD_DIM
        top = lax.broadcasted_iota(jnp.int32, (LANES, ts), 0) < HEAD_DIM
        c, sp, sm = c_ref[...], sp_ref[...], sm_ref[...]
        for i in range(nq):
            xv = x_ref[:, i * LANES:(i + 1) * LANES]
            y = xv * _pair_norm(xv, lo) * qg_ref[...]
            q_ref[:, i * LANES:(i + 1) * LANES] = (_rope(y, c, sp, sm) * (LOG2E * HEAD_DIM ** -0.5)).astype(BF16)
        for i in range(nkp):
            xv = x_ref[:, qw + i * LANES:qw + (i + 1) * LANES]
            z = _rope(xv * _pair_norm(xv, lo) * kg_ref[...], c, sp, sm)
            k_ref[:, i * LANES:(i + 1) * LANES] = z.astype(BF16)
            kt_ref[i * LANES:(i + 1) * LANES, :] = z.T.astype(BF16)
            vv = x_ref[:, qw + kw + i * LANES:qw + kw + (i + 1) * LANES]
            v_ref[:, i * LANES:(i + 1) * LANES] = vv.astype(BF16)
            vvt = vv.T
            vt_ref[(2 * i) * LANES:(2 * i + 1) * LANES, :] = jnp.where(top, vvt, 1.0).astype(BF16)
            vt_ref[(2 * i + 1) * LANES:(2 * i + 2) * LANES, :] = jnp.where(top, 1.0, vvt).astype(BF16)

    tab = pl.BlockSpec((ts, LANES), lambda i: (i, 0))
    vec = pl.BlockSpec((1, LANES), lambda i: (0, 0))
    return pl.pallas_call(
        body,
        out_shape=(jax.ShapeDtypeStruct((seq, qw), BF16), jax.ShapeDtypeStruct((seq, kw), BF16),
                   jax.ShapeDtypeStruct((kw, seq), BF16), jax.ShapeDtypeStruct((seq, kw), BF16),
                   jax.ShapeDtypeStruct((N_KV * LANES, seq), BF16), *plan.out_shape),
        grid=(nsteps,),
        in_specs=[pl.BlockSpec((ts, qw + 2 * kw), lambda i: (i, 0)), vec, vec, tab, tab, tab] + [ANY] * nt,
        out_specs=(pl.BlockSpec((ts, qw), lambda i: (i, 0)), pl.BlockSpec((ts, kw), lambda i: (i, 0)),
                   pl.BlockSpec((kw, ts), lambda i: (0, i)), pl.BlockSpec((ts, kw), lambda i: (i, 0)),
                   pl.BlockSpec((N_KV * LANES, ts), lambda i: (0, i)), *([ANY] * nt)),
        scratch_shapes=plan.scratch, compiler_params=_cp(("arbitrary",)), name=name)(qkv, qg2, kg2, *tabs, *sends)


def _qk_prep_bwd(qkv, dq, dk, dv, qg2, kg2, tabs, plan, sends, name):
    seq = qkv.shape[0]
    ts = _pick(seq, (512, 256, 128))
    nq, nkp = N_HEADS // 2, N_KV // 2
    qw, kw = N_HEADS * HEAD_DIM, N_KV * HEAD_DIM
    nt, nsteps = plan.nt, seq // ts

    def body(x_ref, dq_ref, dk_ref, dv_ref, qg_ref, kg_ref, c_ref, sp_ref, sm_ref, *rest):
        o_ref, dqg_ref, dkg_ref = rest[nt:nt + 3]
        plan.bind(rest[:nt], rest[nt + 3:2 * nt + 3], *rest[2 * nt + 3:])
        step = pl.program_id(0)

        @pl.when(step == 0)
        def _():
            plan.start()

        lo = lax.broadcasted_iota(jnp.int32, (ts, LANES), 1) < HEAD_DIM
        c, sp, sm = c_ref[...], sp_ref[...], sm_ref[...]

        def one(xv, dz, gain):
            r = _pair_norm(xv, lo)
            nv = xv * r
            dy = _rope_t(dz, c, sp, sm)
            dgp = jnp.sum(dy * nv, axis=0, keepdims=True)
            dn = dy * gain
            t = dn * nv
            m_lo = jnp.sum(jnp.where(lo, t, 0.0), axis=1, keepdims=True)
            m_hi = jnp.sum(jnp.where(lo, 0.0, t), axis=1, keepdims=True)
            m = jnp.where(lo, m_lo, m_hi) * (1.0 / HEAD_DIM)
            return r * (dn - nv * m), dgp

        dqg = jnp.zeros((1, LANES), F32)
        for i in range(nq):
            sl = slice(i * LANES, (i + 1) * LANES)
            dx, dgp = one(x_ref[:, sl], dq_ref[:, sl] * (HEAD_DIM ** -0.5), qg_ref[...])
            o_ref[:, sl] = dx.astype(BF16)
            dqg = dqg + dgp
        dkg = jnp.zeros((1, LANES), F32)
        for i in range(nkp):
            sl = slice(i * LANES, (i + 1) * LANES)
            dx, dgp = one(x_ref[:, qw + i * LANES:qw + (i + 1) * LANES], dk_ref[:, sl], kg_ref[...])
            o_ref[:, qw + i * LANES:qw + (i + 1) * LANES] = dx.astype(BF16)
            dkg = dkg + dgp
            o_ref[:, qw + kw + i * LANES:qw + kw + (i + 1) * LANES] = dv_ref[:, sl].astype(BF16)

        @pl.when(step == 0)
        def _():
            dqg_ref[...] = dqg
            dkg_ref[...] = dkg

        @pl.when(step > 0)
        def _():
            dqg_ref[...] += dqg
            dkg_ref[...] += dkg

        @pl.when(step == nsteps - 1)
        def _():
            plan.finish()

    tab = pl.BlockSpec((ts, LANES), lambda i: (i, 0))
    vec = pl.BlockSpec((1, LANES), lambda i: (0, 0))
    return pl.pallas_call(
        body,
        out_shape=(jax.ShapeDtypeStruct((seq, qw + 2 * kw), BF16), jax.ShapeDtypeStruct((1, LANES), F32),
                   jax.ShapeDtypeStruct((1, LANES), F32), *plan.out_shape),
        grid=(nsteps,),
        in_specs=[pl.BlockSpec((ts, qw + 2 * kw), lambda i: (i, 0)), pl.BlockSpec((ts, qw), lambda i: (i, 0)),
                  pl.BlockSpec((ts, kw), lambda i: (i, 0)), pl.BlockSpec((ts, kw), lambda i: (i, 0)),
                  vec, vec, tab, tab, tab] + [ANY] * nt,
        out_specs=(pl.BlockSpec((ts, qw + 2 * kw), lambda i: (i, 0)), vec, vec, *([ANY] * nt)),
        scratch_shapes=plan.scratch, compiler_params=_cp(("arbitrary",)), name=name)(qkv, dq, dk, dv, qg2, kg2, *tabs, *sends)


def _slot(blk, off0, tq):
    half = lax.broadcasted_iota(jnp.int32, (tq, LANES), 1) // HEAD_DIM
    keep = half == jnp.where(off0, 0, 1)
    parts = []
    for p in range(2):
        pair = blk[:, p * LANES:(p + 1) * LANES].astype(F32)
        rolled = pltpu.roll(pair, HEAD_DIM, axis=1)
        parts.append(jnp.where(keep, jnp.where(off0, pair, rolled), 0.0))
        parts.append(jnp.where(keep, jnp.where(off0, rolled, pair), 0.0))
    return jnp.concatenate(parts, axis=0)


def _unslot(x4, off0, tq):
    lo = lax.broadcasted_iota(jnp.int32, (tq, LANES), 1) < HEAD_DIM
    pairs = []
    for p in range(2):
        h0 = x4[(2 * p) * tq:(2 * p + 1) * tq]
        h1 = x4[(2 * p + 1) * tq:(2 * p + 2) * tq]
        a = jnp.where(off0, h0, pltpu.roll(h0, HEAD_DIM, axis=1))
        b = jnp.where(off0, pltpu.roll(h1, HEAD_DIM, axis=1), h1)
        pairs.append(jnp.where(lo, a, b))
    return jnp.concatenate(pairs, axis=1)


def _flash_fwd(q, k, vt, plan, shards, name):
    seq = q.shape[0]
    tq = _pick(seq, (FLASH_TQ, 128))
    tk = _pick(seq, (FLASH_TK, 2048, 512, 256, 128))
    sub = _pick(tk, (FLASH_SUB, 256, 128))
    nq, nkv, nsub = seq // tq, seq // tk, tk // sub
    gw = 4 * HEAD_DIM
    nt = plan.nt

    def body(q_ref, k_ref, vt_ref, *rest):
        o_ref, lse_ref = rest[nt:nt + 2]
        q4_ref, m_ref, acc_ref, st_ref = rest[2 * nt + 2:2 * nt + 6]
        plan.bind(rest[:nt], rest[nt + 2:2 * nt + 2], *rest[2 * nt + 6:])
        g, qi, ki = pl.program_id(0), pl.program_id(1), pl.program_id(2)
        off0 = (g % 2) == 0
        @pl.when(jnp.logical_and(g == 0, jnp.logical_and(qi == 0, ki == 0)))
        def _():
            plan.start()

        @pl.when(jnp.logical_and(g == N_KV - 1, jnp.logical_and(qi == nq // 2, ki == 0)))
        def _():
            plan.forward()

        @pl.when(ki == 0)
        def _():
            q4_ref[...] = _slot(q_ref[...], off0, tq).astype(BF16)
            m_ref[...] = jnp.full(m_ref.shape, NEG, F32)
            acc_ref[...] = jnp.zeros(acc_ref.shape, F32)

        q4 = q4_ref[...]

        def scores(c):
            st_ref[c % 2] = lax.dot_general(k_ref[c * sub:(c + 1) * sub, :], q4, (((1,), (1,)), ((), ())),
                                            preferred_element_type=F32)

        m, acc = m_ref[...], acc_ref[...]
        scores(0)
        for c in range(nsub):
            if c + 1 < nsub:
                scores(c + 1)
            st = st_ref[c % 2]
            m_new = jnp.maximum(m, jnp.max(st, axis=0, keepdims=True))
            pt = jnp.exp2(st - m_new).astype(BF16)
            acc = jnp.exp2(m - m_new) * acc + jnp.dot(vt_ref[:, c * sub:(c + 1) * sub], pt, preferred_element_type=F32)
            m = m_new
        m_ref[...] = m
        acc_ref[...] = acc

        @pl.when(ki == nkv - 1)
        def _():
            acc = acc_ref[...]
            l = jnp.where(off0, acc[HEAD_DIM:HEAD_DIM + 1], acc[0:1])
            o4 = acc.T
            o4 = o4 / pltpu.roll(o4, HEAD_DIM, axis=1)
            o_ref[...] = _unslot(o4, off0, tq).astype(o_ref.dtype)
            lse_ref[...] = jnp.broadcast_to(m_ref[...] + jnp.log2(l), lse_ref.shape)

        @pl.when(jnp.logical_and(g == N_KV - 1, jnp.logical_and(qi == nq - 1, ki == nkv - 1)))
        def _():
            plan.finish()

    return pl.pallas_call(
        body,
        out_shape=(jax.ShapeDtypeStruct((seq, N_HEADS * HEAD_DIM), BF16),
                   jax.ShapeDtypeStruct((N_KV * nq * 8, 4 * tq), F32), *plan.out_shape),
        grid=(N_KV, nq, nkv),
        in_specs=[pl.BlockSpec((tq, gw), lambda g, qi, ki: (qi, g)),
                  pl.BlockSpec((tk, LANES), lambda g, qi, ki: (ki, g // 2)),
                  pl.BlockSpec((LANES, tk), lambda g, qi, ki: (g, ki))] + [ANY] * nt,
        out_specs=(pl.BlockSpec((tq, gw), lambda g, qi, ki: (qi, g)),
                   pl.BlockSpec((8, 4 * tq), lambda g, qi, ki: (g * nq + qi, 0)), *([ANY] * nt)),
        scratch_shapes=[pltpu.VMEM((4 * tq, LANES), BF16), pltpu.VMEM((1, 4 * tq), F32),
                        pltpu.VMEM((LANES, 4 * tq), F32), pltpu.VMEM((2, sub, 4 * tq), F32)] + plan.scratch,
        compiler_params=_cp(("arbitrary", "arbitrary", "arbitrary")), name=name)(q, k, vt, *shards)


def _flash_bwd(q, k, kt, v, do, o, lse, plan, grads, name):
    seq = q.shape[0]
    tq = _pick(seq, (FLASH_TQ, 128))
    tk = _pick(seq, (FLASH_TK, 2048, 512, 256, 128))
    sub = _pick(tk, (FLASH_SUB, 256, 128))
    nq, nkv, nsub = seq // tq, seq // tk, tk // sub
    gw = 4 * HEAD_DIM
    nt = plan.nt

    def body(q_ref, k_ref, kt_ref, v_ref, do_ref, o_ref, lse_ref, *rest):
        dq_ref, dk_ref, dv_ref = rest[nt:nt + 3]
        q4_ref, do4_ref, delta_ref, dqt_ref, st_ref, dpt_ref = rest[2 * nt + 3:2 * nt + 9]
        plan.bind(rest[:nt], rest[nt + 3:2 * nt + 3], *rest[2 * nt + 9:])
        g, qi, ki = pl.program_id(0), pl.program_id(1), pl.program_id(2)
        off0 = (g % 2) == 0

        @pl.when(jnp.logical_and(g == 0, jnp.logical_and(qi == 0, ki == 0)))
        def _():
            plan.start()

        @pl.when(jnp.logical_and(g % 2 == 0, jnp.logical_and(qi == 0, ki == 0)))
        def _():
            dk_ref[...] = jnp.zeros(dk_ref.shape, F32)
            dv_ref[...] = jnp.zeros(dv_ref.shape, F32)

        @pl.when(ki == 0)
        def _():
            q4_ref[...] = _slot(q_ref[...], off0, tq).astype(BF16)
            do4 = _slot(do_ref[...], off0, tq)
            do4_ref[...] = do4.astype(BF16)
            o4 = _slot(o_ref[...], off0, tq)
            delta_ref[...] = jnp.sum((do4 * o4).T, axis=0, keepdims=True)
            dqt_ref[...] = jnp.zeros(dqt_ref.shape, F32)

        q4, do4 = q4_ref[...], do4_ref[...]
        lse_row, delta = lse_ref[0:1, :], delta_ref[...]

        def products(c):
            rows = slice(c * sub, (c + 1) * sub)
            st_ref[c % 2] = lax.dot_general(k_ref[rows, :], q4, (((1,), (1,)), ((), ())), preferred_element_type=F32)
            dpt_ref[c % 2] = lax.dot_general(v_ref[rows, :], do4, (((1,), (1,)), ((), ())), preferred_element_type=F32)

        dqt = dqt_ref[...]
        products(0)
        for c in range(nsub):
            if c + 1 < nsub:
                products(c + 1)
            pt = jnp.exp2(st_ref[c % 2] - lse_row)
            dst = (pt * (dpt_ref[c % 2] - delta)).astype(BF16)
            rows = pl.ds(pl.multiple_of(ki * tk + c * sub, sub), sub)
            dv_ref[rows, :] += jnp.dot(pt.astype(BF16), do4, preferred_element_type=F32)
            dk_ref[rows, :] += jnp.dot(dst, q4, preferred_element_type=F32) * (1.0 / LOG2E)
            dqt = dqt + jnp.dot(kt_ref[:, c * sub:(c + 1) * sub], dst, preferred_element_type=F32)
        dqt_ref[...] = dqt

        @pl.when(ki == nkv - 1)
        def _():
            dq_ref[...] = _unslot(dqt_ref[...].T, off0, tq)

        @pl.when(jnp.logical_and(g == N_KV - 1, jnp.logical_and(qi == nq - 1, ki == nkv - 1)))
        def _():
            plan.finish()

    return pl.pallas_call(
        body,
        out_shape=(jax.ShapeDtypeStruct((seq, N_HEADS * HEAD_DIM), F32),
                   jax.ShapeDtypeStruct((seq, N_KV * HEAD_DIM), F32), jax.ShapeDtypeStruct((seq, N_KV * HEAD_DIM), F32),
                   *plan.out_shape),
        grid=(N_KV, nq, nkv),
        in_specs=[pl.BlockSpec((tq, gw), lambda g, qi, ki: (qi, g)),
                  pl.BlockSpec((tk, LANES), lambda g, qi, ki: (ki, g // 2)),
                  pl.BlockSpec((LANES, tk), lambda g, qi, ki: (g // 2, ki)),
                  pl.BlockSpec((tk, LANES), lambda g, qi, ki: (ki, g // 2)),
                  pl.BlockSpec((tq, gw), lambda g, qi, ki: (qi, g)),
                  pl.BlockSpec((tq, gw), lambda g, qi, ki: (qi, g)),
                  pl.BlockSpec((8, 4 * tq), lambda g, qi, ki: (g * nq + qi, 0))] + [ANY] * nt,
        out_specs=(pl.BlockSpec((tq, gw), lambda g, qi, ki: (qi, g)),
                   pl.BlockSpec((seq, LANES), lambda g, qi, ki: (0, g // 2)),
                   pl.BlockSpec((seq, LANES), lambda g, qi, ki: (0, g // 2)), *([ANY] * nt)),
        scratch_shapes=[pltpu.VMEM((4 * tq, LANES), BF16), pltpu.VMEM((4 * tq, LANES), BF16),
                        pltpu.VMEM((1, 4 * tq), F32), pltpu.VMEM((LANES, 4 * tq), F32),
                        pltpu.VMEM((2, sub, 4 * tq), F32), pltpu.VMEM((2, sub, 4 * tq), F32)] + plan.scratch,
        compiler_params=_cp(("arbitrary", "arbitrary", "arbitrary")), name=name)(q, k, kt, v, do, o, lse, *grads)


def _xattn_fwd(q, kv, name):
    seq, d = q.shape
    mlen = kv.shape[0]
    tq = _pick(seq, (1024, 512, 256))

    def body(q_ref, k_ref, v_ref, o_ref):
        for h in range(X_HEADS):
            sl = slice(h * X_HEAD_DIM, (h + 1) * X_HEAD_DIM)
            s = lax.dot_general(q_ref[:, sl], k_ref[:, sl], (((1,), (1,)), ((), ())), preferred_element_type=F32)
            e = jnp.exp(s - jnp.max(s, axis=-1, keepdims=True))
            p = e / jnp.sum(e, axis=-1, keepdims=True)
            o_ref[:, sl] = jnp.dot(p.astype(BF16), v_ref[:, sl], preferred_element_type=F32).astype(o_ref.dtype)

    return pl.pallas_call(
        body, out_shape=jax.ShapeDtypeStruct((seq, d), BF16), grid=(seq // tq,),
        in_specs=[pl.BlockSpec((tq, d), lambda i: (i, 0)), pl.BlockSpec((mlen, d), lambda i: (0, 0)),
                  pl.BlockSpec((mlen, d), lambda i: (0, 1))],
        out_specs=pl.BlockSpec((tq, d), lambda i: (i, 0)), compiler_params=_cp(("parallel",)), name=name)(q, kv, kv)


def _xattn_bwd(q, kv, do, name):
    seq, d = q.shape
    mlen = kv.shape[0]
    tq = _pick(seq, (1024, 512, 256))
    scale = X_HEAD_DIM ** -0.5

    def body(q_ref, k_ref, v_ref, do_ref, dq_ref, dkv_ref):
        i = pl.program_id(0)

        @pl.when(i == 0)
        def _():
            dkv_ref[...] = jnp.zeros(dkv_ref.shape, F32)

        for h in range(X_HEADS):
            sl = slice(h * X_HEAD_DIM, (h + 1) * X_HEAD_DIM)
            qh, kh, vh = q_ref[:, sl], k_ref[:, sl], v_ref[:, sl]
            doh = do_ref[:, sl].astype(BF16)
            st = lax.dot_general(kh, qh, (((1,), (1,)), ((), ())), preferred_element_type=F32)
            e = jnp.exp(st - jnp.max(st, axis=0, keepdims=True))
            pt = e / jnp.sum(e, axis=0, keepdims=True)
            dpt = lax.dot_general(vh, doh, (((1,), (1,)), ((), ())), preferred_element_type=F32)
            dst = (pt * (dpt - jnp.sum(pt * dpt, axis=0, keepdims=True))).astype(BF16)
            dkv_ref[:, sl] += jnp.dot(dst, qh, preferred_element_type=F32)
            dkv_ref[:, d + h * X_HEAD_DIM:d + (h + 1) * X_HEAD_DIM] += jnp.dot(pt.astype(BF16), doh,
                                                                                 preferred_element_type=F32)
            dqh = lax.dot_general(dst, kh, (((0,), (0,)), ((), ())), preferred_element_type=F32)
            dq_ref[:, sl] = (dqh * scale).astype(dq_ref.dtype)

    return pl.pallas_call(
        body, out_shape=(jax.ShapeDtypeStruct((seq, d), BF16), jax.ShapeDtypeStruct((mlen, 2 * d), F32)),
        grid=(seq // tq,),
        in_specs=[pl.BlockSpec((tq, d), lambda i: (i, 0)), pl.BlockSpec((mlen, d), lambda i: (0, 0)),
                  pl.BlockSpec((mlen, d), lambda i: (0, 1)), pl.BlockSpec((tq, d), lambda i: (i, 0))],
        out_specs=(pl.BlockSpec((tq, d), lambda i: (i, 0)), pl.BlockSpec((mlen, 2 * d), lambda i: (0, 0))),
        compiler_params=_cp(("arbitrary",)), name=name)(q, kv, kv, do)


def _halo_specs(tr, tc, seq, col):
    per, last = tr // HALO, seq // HALO - 1
    return [pl.BlockSpec((tr, tc), lambda j, r: (r, col(j))),
            pl.BlockSpec((HALO, tc), lambda j, r: (jnp.maximum(r * per - 1, 0), col(j))),
            pl.BlockSpec((HALO, tc), lambda j, r: (jnp.minimum((r + 1) * per, last), col(j)))]


def _extend(main_ref, prev_ref, next_ref, r, nr):
    pv = (r > 0).astype(F32)
    nv = (r < nr - 1).astype(F32)
    return jnp.concatenate([prev_ref[...].astype(F32) * pv, main_ref[...].astype(F32),
                            next_ref[...].astype(F32) * nv], axis=0)


def _conv3(e, w_ref, n):
    return pltpu.roll(e, 1, axis=0) * w_ref[0:1, :] + e * w_ref[1:2, :] + pltpu.roll(e, n - 1, axis=0) * w_ref[2:3, :]


def _conv_gate_fwd(ug, uv, cw, cb, layer, name, plan=None, shards=(), fulls=()):
    seq, f = ug.shape
    tc = 256
    tr = _pick(seq, (HALO_TR, 512, 256))
    nc, nr = f // tc, seq // tr
    n = tr + 2 * HALO
    nt = plan.nt if plan is not None else 0

    def body(g_ref, gp_ref, gn_ref, v_ref, vp_ref, vn_ref, wg_ref, wv_ref, bg_ref, bv_ref, *rest):
        o_ref = rest[2 * nt]
        j, r = pl.program_id(0), pl.program_id(1)
        if plan is not None:
            plan.bind(rest[:nt], rest[2 * nt + 1:3 * nt + 1], *rest[3 * nt + 1:])

            @pl.when(jnp.logical_and(j == 0, r == 0))
            def _():
                plan.start()

        cg = _conv3(_extend(g_ref, gp_ref, gn_ref, r, nr), wg_ref, n)[HALO:HALO + tr] + bg_ref[...]
        cv = _conv3(_extend(v_ref, vp_ref, vn_ref, r, nr), wv_ref, n)[HALO:HALO + tr] + bv_ref[...]
        o_ref[...] = (cg * jax.nn.sigmoid(cg) * cv).astype(o_ref.dtype)

        if plan is not None:
            @pl.when(jnp.logical_and(j == nc - 1, r == nr - 1))
            def _():
                plan.forward()
                plan.finish()

    w_spec = lambda shift: pl.BlockSpec((None, 3, tc), lambda j, r: (layer, 0, j + shift))
    b_spec = lambda shift: pl.BlockSpec((None, 1, tc), lambda j, r: (layer, 0, j + shift))
    act_shape = jax.ShapeDtypeStruct((seq, f), BF16)
    act_spec = pl.BlockSpec((tr, tc), lambda j, r: (r, j))
    in_specs = _halo_specs(tr, tc, seq, lambda j: j) * 2 + [w_spec(0), w_spec(nc), b_spec(0), b_spec(nc)]
    operands = (ug, ug, ug, uv, uv, uv, cw, cw, cb, cb)
    if plan is None:
        return pl.pallas_call(body, out_shape=act_shape, grid=(nc, nr), in_specs=in_specs, out_specs=act_spec,
                              compiler_params=_cp(("parallel", "parallel")), name=name)(*operands)
    return pl.pallas_call(
        body, out_shape=(act_shape, *plan.out_shape), grid=(nc, nr), in_specs=in_specs + [ANY] * (2 * nt),
        out_specs=(act_spec, *([ANY] * nt)), scratch_shapes=plan.scratch,
        input_output_aliases={len(operands) + nt + t: 1 + t for t in range(nt)},
        compiler_params=_cp(("arbitrary", "arbitrary")), name=name)(*operands, *shards, *fulls)


def _conv_gate_bwd(ug, uv, dact, cw, cb, layer, name):
    seq, f = ug.shape
    tc = 256
    tr = _pick(seq, (512, 256))
    nc, nr = f // tc, seq // tr
    n = tr + 2 * HALO

    def body(g_ref, gp_ref, gn_ref, v_ref, vp_ref, vn_ref, d_ref, dp_ref, dn_ref, wg_ref, wv_ref, bg_ref, bv_ref,
             dug_ref, duv_ref, dwg_ref, dwv_ref):
        r = pl.program_id(1)
        eg = _extend(g_ref, gp_ref, gn_ref, r, nr)
        ev = _extend(v_ref, vp_ref, vn_ref, r, nr)
        da = _extend(d_ref, dp_ref, dn_ref, r, nr)
        eg3 = (pltpu.roll(eg, 1, axis=0), eg, pltpu.roll(eg, n - 1, axis=0))
        ev3 = (pltpu.roll(ev, 1, axis=0), ev, pltpu.roll(ev, n - 1, axis=0))
        cg = eg3[0] * wg_ref[0:1, :] + eg3[1] * wg_ref[1:2, :] + eg3[2] * wg_ref[2:3, :] + bg_ref[...]
        cv = ev3[0] * wv_ref[0:1, :] + ev3[1] * wv_ref[1:2, :] + ev3[2] * wv_ref[2:3, :] + bv_ref[...]
        sg = jax.nn.sigmoid(cg)
        dcv = da * (cg * sg)
        dcg = da * cv * (sg * (1.0 + cg * (1.0 - sg)))

        def back(dc, e3, w_ref, du_ref, dw_ref):
            du = (pltpu.roll(dc, n - 1, axis=0) * w_ref[0:1, :] + dc * w_ref[1:2, :]
                  + pltpu.roll(dc, 1, axis=0) * w_ref[2:3, :])
            du_ref[...] = du[HALO:HALO + tr].astype(du_ref.dtype)
            dcm = dc[HALO:HALO + tr]
            taps = [jnp.sum(dcm * e[HALO:HALO + tr], axis=0, keepdims=True) for e in e3] + [
                    jnp.sum(dcm, axis=0, keepdims=True)]
            part = jnp.concatenate(taps + [jnp.zeros((4, tc), F32)], axis=0)

            @pl.when(r == 0)
            def _():
                dw_ref[...] = part

            @pl.when(r > 0)
            def _():
                dw_ref[...] += part

        back(dcg, eg3, wg_ref, dug_ref, dwg_ref)
        back(dcv, ev3, wv_ref, duv_ref, dwv_ref)

    w_spec = lambda shift: pl.BlockSpec((None, 3, tc), lambda j, r: (layer, 0, j + shift))
    b_spec = lambda shift: pl.BlockSpec((None, 1, tc), lambda j, r: (layer, 0, j + shift))
    out_rows = pl.BlockSpec((tr, tc), lambda j, r: (r, j))
    out_acc = pl.BlockSpec((8, tc), lambda j, r: (0, j))
    return pl.pallas_call(
        body,
        out_shape=(jax.ShapeDtypeStruct((seq, f), BF16), jax.ShapeDtypeStruct((seq, f), BF16),
                   jax.ShapeDtypeStruct((8, f), F32), jax.ShapeDtypeStruct((8, f), F32)),
        grid=(nc, nr),
        in_specs=_halo_specs(tr, tc, seq, lambda j: j) * 3 + [w_spec(0), w_spec(nc), b_spec(0), b_spec(nc)],
        out_specs=(out_rows, out_rows, out_acc, out_acc),
        compiler_params=_cp(("parallel", "arbitrary")), name=name)(ug, ug, ug, uv, uv, uv, dact, dact, dact, cw, cw, cb, cb)


def _pool_count(g, r, tr, n, seq):
    half = jnp.left_shift(1, g)
    t = r * tr - HALO + lax.broadcasted_iota(jnp.int32, (n, 1), 0)
    cnt = jnp.minimum(t + half, seq) - jnp.maximum(t - half, 0)
    return jnp.maximum(cnt, 1).astype(F32)


def _by_group(g, levels):
    out = levels[3]
    for i in (2, 1, 0):
        out = jnp.where(g == i, levels[i], out)
    return out


def _pool_mixed(e, g, cnt, n):
    w2 = e + pltpu.roll(e, 1, axis=0)
    w4 = pltpu.roll(w2, 1, axis=0) + pltpu.roll(w2, n - 1, axis=0)
    w8 = pltpu.roll(w4, 2, axis=0) + pltpu.roll(w4, n - 2, axis=0)
    w16 = pltpu.roll(w8, 4, axis=0) + pltpu.roll(w8, n - 4, axis=0)
    return _by_group(g, (w2, w4, w8, w16)) / cnt - e


def _pool_fwd(hp, xres, pw, scale, name, plan, shards, fulls):
    seq, d = hp.shape
    tc = POOL_GROUP_W
    tr = _pick(seq, (HALO_TR, 512, 256))
    nr = seq // tr
    n = tr + 2 * HALO
    nt = plan.nt

    def body(h_ref, hp_ref, hn_ref, x_ref, w_ref, s_ref, *rest):
        o_ref = rest[2 * nt]
        plan.bind(rest[:nt], rest[2 * nt + 1:3 * nt + 1], *rest[3 * nt + 1:])
        g, r = pl.program_id(0), pl.program_id(1)

        def compute():
            e = _extend(h_ref, hp_ref, hn_ref, r, nr)
            mixed = _pool_mixed(e, g, _pool_count(g, r, tr, n, seq), n)[HALO:HALO + tr]
            y = jnp.dot(mixed.astype(BF16), w_ref[...], preferred_element_type=F32)
            o_ref[...] = x_ref[...] + y * s_ref[...]

        _hosted(plan, POOL_GROUPS * nr, g * nr + r, compute)

    operands = (hp, hp, hp, xres, pw, scale)
    return pl.pallas_call(
        body, out_shape=(jax.ShapeDtypeStruct((seq, d), F32), *plan.out_shape), grid=(POOL_GROUPS, nr),
        in_specs=_halo_specs(tr, tc, seq, lambda j: j) + [
            pl.BlockSpec((tr, tc), lambda j, r: (r, j)), pl.BlockSpec((None, tc, tc), lambda j, r: (j, 0, 0)),
            pl.BlockSpec((1, tc), lambda j, r: (0, j))] + [ANY] * (2 * nt),
        out_specs=(pl.BlockSpec((tr, tc), lambda j, r: (r, j)), *([ANY] * nt)), scratch_shapes=plan.scratch,
        input_output_aliases={len(operands) + nt + t: 1 + t for t in range(nt)},
        compiler_params=_cp(("arbitrary", "arbitrary")), name=name)(*operands, *shards, *fulls)


def _pool_bwd(hp, dy, pw, scale, name):
    seq, d = hp.shape
    tc = POOL_GROUP_W
    tr = _pick(seq, (HALO_TR, 512, 256))
    nr = seq // tr
    n = tr + 2 * HALO

    def body(h_ref, hp_ref, hn_ref, d_ref, dp_ref, dn_ref, w_ref, s_ref, dh_ref, dw_ref, ds_ref):
        g, r = pl.program_id(0), pl.program_id(1)
        cnt = _pool_count(g, r, tr, n, seq)
        e = _extend(h_ref, hp_ref, hn_ref, r, nr)
        mixed = _pool_mixed(e, g, cnt, n)[HALO:HALO + tr].astype(BF16)
        dye = _extend(d_ref, dp_ref, dn_ref, r, nr)
        dyp = (dye * s_ref[...]).astype(BF16)
        dmixed = lax.dot_general(dyp, w_ref[...], (((1,), (1,)), ((), ())), preferred_element_type=F32)
        dwin = dmixed / cnt
        m2 = dwin + pltpu.roll(dwin, n - 1, axis=0)
        m4 = pltpu.roll(m2, 1, axis=0) + pltpu.roll(m2, n - 1, axis=0)
        m8 = pltpu.roll(m4, 2, axis=0) + pltpu.roll(m4, n - 2, axis=0)
        m16 = pltpu.roll(m8, 4, axis=0) + pltpu.roll(m8, n - 4, axis=0)
        dh_ref[...] = (_by_group(g, (m2, m4, m8, m16)) - dmixed)[HALO:HALO + tr]
        ypre = jnp.dot(mixed, w_ref[...], preferred_element_type=F32)
        dsp = jnp.sum(d_ref[...] * ypre, axis=0, keepdims=True)
        dwp = lax.dot_general(mixed, dyp[HALO:HALO + tr], (((0,), (0,)), ((), ())), preferred_element_type=F32)

        @pl.when(r == 0)
        def _():
            dw_ref[...] = dwp
            ds_ref[...] = dsp

        @pl.when(r > 0)
        def _():
            dw_ref[...] += dwp
            ds_ref[...] += dsp

    return pl.pallas_call(
        body,
        out_shape=(jax.ShapeDtypeStruct((seq, d), F32), jax.ShapeDtypeStruct((POOL_GROUPS, tc, tc), F32),
                   jax.ShapeDtypeStruct((1, d), F32)),
        grid=(POOL_GROUPS, nr),
        in_specs=_halo_specs(tr, tc, seq, lambda j: j) * 2 + [
            pl.BlockSpec((None, tc, tc), lambda j, r: (j, 0, 0)), pl.BlockSpec((1, tc), lambda j, r: (0, j))],
        out_specs=(pl.BlockSpec((tr, tc), lambda j, r: (r, j)), pl.BlockSpec((None, tc, tc), lambda j, r: (j, 0, 0)),
                   pl.BlockSpec((1, tc), lambda j, r: (0, j))),
        compiler_params=_cp(("parallel", "arbitrary")), name=name)(hp, hp, hp, dy, dy, dy, pw, scale)


def _adamw_math(w, g, m, v):
    m = ADAM_B1 * m + (1.0 - ADAM_B1) * g
    v = ADAM_B2 * v + (1.0 - ADAM_B2) * (g * g)
    m_hat = m / (1.0 - ADAM_B1 ** ADAM_STEP)
    v_hat = v / (1.0 - ADAM_B2 ** ADAM_STEP)
    delta = -ADAM_LR * (m_hat / (jnp.sqrt(v_hat) + ADAM_EPS) + ADAM_WD * w)
    return delta, m, v


def _adamw(w, ga, gb, m, v, name):
    rows, cols = w.shape
    tr = _pick(rows, (256, 128, 64, 32, 16, 8))
    two = gb is not None

    def body(*refs):
        if two:
            w_ref, ga_ref, gb_ref, m_ref, v_ref, g_out, d_out, m_out, v_out = refs
            g = ga_ref[...] + gb_ref[...]
        else:
            w_ref, ga_ref, m_ref, v_ref, g_out, d_out, m_out, v_out = refs
            g = ga_ref[...]
        delta, m, v = _adamw_math(w_ref[...], g, m_ref[...], v_ref[...])
        g_out[...] = g
        d_out[...] = delta
        m_out[...] = m
        v_out[...] = v

    spec = pl.BlockSpec((tr, cols), lambda i: (i, 0))
    ops = [w, ga] + ([gb] if two else []) + [m, v]
    return pl.pallas_call(
        body, out_shape=tuple(jax.ShapeDtypeStruct((rows, cols), F32) for _ in range(4)), grid=(rows // tr,),
        in_specs=[spec] * len(ops), out_specs=(spec,) * 4, compiler_params=_cp(("parallel",)), name=name)(*ops)


def _sum4(parts, name):
    _, rows, cols = parts.shape
    tr = _pick(rows, (256, 128, 64, 32, 16))

    def body(p_ref, o_ref):
        acc = p_ref[0].astype(F32)
        for kk in range(1, 4):
            acc = acc + p_ref[kk].astype(F32)
        o_ref[...] = acc

    return pl.pallas_call(
        body, out_shape=jax.ShapeDtypeStruct((rows, cols), F32), grid=(rows // tr,),
        in_specs=[pl.BlockSpec((4, tr, cols), lambda i: (0, i, 0))], out_specs=pl.BlockSpec((tr, cols), lambda i: (i, 0)),
        compiler_params=_cp(("parallel",)), name=name)(parts)


def _place():
    x, y, c = lax.axis_index("x"), lax.axis_index("y"), lax.axis_index("c")
    chips = [(1 - x, y), (x, 1 - y), (1 - x, 1 - y)]
    return x, y, c, chips


def _window(ref, axis, j, size, c=None, half=None, lead=(), layers=slice(None)):
    if axis == "r":
        if c is None:
            return ref.at[lead + (layers, pl.ds(pl.multiple_of(j * size, 32), size), slice(None))]
        return ref.at[lead + (layers, pl.ds(pl.multiple_of(j * size + c * half, 32), half), slice(None))]
    cols = pl.ds(pl.multiple_of(j * size, LANES), size)
    if c is None:
        return ref.at[lead + (layers, slice(None), cols)]
    return ref.at[lead + (layers, pl.ds(pl.multiple_of(c * half, 32), half), cols)]


class _Gather:
    def __init__(self, shards, axes, layers=None):
        self.nt, self.axes = len(shards), axes
        self.layers = layers or [slice(None)] * self.nt
        self.out_shape, self.sizes, self.halves = [], [], []
        for s, ax in zip(shards, axes):
            l, rs, cs = s.shape
            self.out_shape.append(jax.ShapeDtypeStruct((l, 4 * rs, cs) if ax == "r" else (l, rs, 4 * cs), s.dtype))
            self.sizes.append(rs if ax == "r" else cs)
            self.halves.append(rs // 2)
        self.scratch = [pltpu.SemaphoreType.DMA((6 * self.nt,)), pltpu.SemaphoreType.DMA((6 * self.nt,)),
                        pltpu.SemaphoreType.DMA((self.nt,))]

    def bind(self, src, dst, send_sems, recv_sems, local_sems):
        self.src, self.dst, self.send_sems, self.recv_sems, self.local_sems = src, dst, send_sems, recv_sems, local_sems

    def _win(self, t, j, core=None):
        return _window(self.dst[t], self.axes[t], j, self.sizes[t], core, self.halves[t], layers=self.layers[t])

    def _ici(self, t, kk, origin):
        _, _, c, chips = _place()
        px, py = chips[kk]
        half = self.src[t].at[self.layers[t], pl.ds(pl.multiple_of(c * self.halves[t], 16), self.halves[t]), :]
        return pltpu.make_async_remote_copy(
            src_ref=half, dst_ref=self._win(t, origin, c), send_sem=self.send_sems.at[t * 3 + kk],
            recv_sem=self.recv_sems.at[t * 3 + kk], device_id=(px, py, c), device_id_type=MESH)

    def _d2d(self, t, kk, origin, core):
        x, y, c, _ = _place()
        k2 = 3 * self.nt + t * 3 + kk
        return pltpu.make_async_remote_copy(
            src_ref=self._win(t, origin, core), dst_ref=self._win(t, origin, core), send_sem=self.send_sems.at[k2],
            recv_sem=self.recv_sems.at[k2], device_id=(x, y, 1 - c), device_id_type=MESH)

    def _local(self, t):
        x, y, _, _ = _place()
        return pltpu.make_async_copy(self.src[t].at[self.layers[t]], self._win(t, 2 * x + y), self.local_sems.at[t])

    def _each(self):
        _, _, _, chips = _place()
        for t in range(self.nt):
            for kk in range(3):
                px, py = chips[kk]
                yield t, kk, 2 * px + py

    def start(self):
        x, y, _, _ = _place()
        for t in range(self.nt):
            self._local(t).start()
        for t, kk, _ in self._each():
            self._ici(t, kk, 2 * x + y).start()

    def forward(self):
        _, _, c, _ = _place()
        for t, kk, origin in self._each():
            self._ici(t, kk, origin).wait_recv()
            self._d2d(t, kk, origin, c).start()

    def finish(self):
        x, y, c, _ = _place()
        for t, kk, origin in self._each():
            self._d2d(t, kk, origin, 1 - c).wait_recv()
        for t, kk, origin in self._each():
            self._ici(t, kk, 2 * x + y).wait_send()
            self._d2d(t, kk, origin, c).wait_send()
        for t in range(self.nt):
            self._local(t).wait()


class _Scatter:
    def __init__(self, grads, axes):
        self.nt, self.axes = len(grads), axes
        self.out_shape, self.sizes = [], []
        for gr, ax in zip(grads, axes):
            l, r, cc = gr.shape
            self.out_shape.append(jax.ShapeDtypeStruct((4, l, r // 4, cc) if ax == "r" else (4, l, r, cc // 4), gr.dtype))
            self.sizes.append(r // 4 if ax == "r" else cc // 4)
        self.scratch = [pltpu.SemaphoreType.DMA((3 * self.nt,)), pltpu.SemaphoreType.DMA((3 * self.nt,)),
                        pltpu.SemaphoreType.DMA((self.nt,))]

    def bind(self, src, dst, send_sems, recv_sems, local_sems):
        self.src, self.dst, self.send_sems, self.recv_sems, self.local_sems = src, dst, send_sems, recv_sems, local_sems

    def _copy(self, t, kk, slot):
        x, y, c, chips = _place()
        px, py = chips[kk]
        return pltpu.make_async_remote_copy(
            src_ref=_window(self.src[t], self.axes[t], 2 * px + py, self.sizes[t]), dst_ref=self.dst[t].at[slot],
            send_sem=self.send_sems.at[t * 3 + kk], recv_sem=self.recv_sems.at[t * 3 + kk],
            device_id=(px, py, c), device_id_type=MESH)

    def _local(self, t):
        x, y, _, _ = _place()
        me = 2 * x + y
        return pltpu.make_async_copy(_window(self.src[t], self.axes[t], me, self.sizes[t]), self.dst[t].at[me],
                                     self.local_sems.at[t])

    def start(self):
        x, y, _, _ = _place()
        for t in range(self.nt):
            self._local(t).start()
            for kk in range(3):
                self._copy(t, kk, 2 * x + y).start()

    def finish(self):
        _, _, _, chips = _place()
        for t in range(self.nt):
            for kk in range(3):
                px, py = chips[kk]
                self._copy(t, kk, 2 * px + py).wait_recv()
        for t in range(self.nt):
            for kk in range(3):
                px, py = chips[kk]
                self._copy(t, kk, 2 * px + py).wait_send()
            self._local(t).wait()


def _comm_call(plans, operands, name):
    nts = [p.nt for p in plans]
    n_in, n_sem = sum(nts), [len(p.scratch) for p in plans]

    def body(*refs):
        pos_in, pos_out, pos_sem = 0, n_in, 2 * n_in
        for p, nt, ns in zip(plans, nts, n_sem):
            p.bind(refs[pos_in:pos_in + nt], refs[pos_out:pos_out + nt], *refs[pos_sem:pos_sem + ns])
            pos_in, pos_out, pos_sem = pos_in + nt, pos_out + nt, pos_sem + ns
        for p in plans:
            p.start()
        for p in plans:
            if hasattr(p, "forward"):
                p.forward()
        for p in plans:
            p.finish()

    return pl.pallas_call(
        body, out_shape=tuple(s for p in plans for s in p.out_shape), in_specs=[ANY] * n_in,
        out_specs=tuple([ANY] * n_in), scratch_shapes=[s for p in plans for s in p.scratch],
        name=name)(*[a for ops in operands for a in ops])


class _GatherAll:
    FLIPS = [f for f in itertools.product((0, 1), repeat=3) if any(f)]

    def __init__(self, pack):
        self.nt = 1
        self.out_shape = [jax.ShapeDtypeStruct((8,) + pack.shape, pack.dtype)]
        self.scratch = [pltpu.SemaphoreType.DMA((7,)), pltpu.SemaphoreType.DMA((7,)), pltpu.SemaphoreType.DMA((1,))]

    def bind(self, src, dst, send_sems, recv_sems, local_sems):
        self.src, self.dst, self.send_sems, self.recv_sems, self.local_sems = src[0], dst[0], send_sems, recv_sems, local_sems

    def _copy(self, kk, mine):
        x, y, c, _ = _place()
        px, py, pc = (1 - v if fl else v for v, fl in zip((x, y, c), self.FLIPS[kk]))
        slot = 4 * x + 2 * y + c if mine else 4 * px + 2 * py + pc
        return pltpu.make_async_remote_copy(src_ref=self.src, dst_ref=self.dst.at[slot], send_sem=self.send_sems.at[kk],
                                            recv_sem=self.recv_sems.at[kk], device_id=(px, py, pc), device_id_type=MESH)

    def _local(self):
        x, y, c, _ = _place()
        return pltpu.make_async_copy(self.src, self.dst.at[4 * x + 2 * y + c], self.local_sems.at[0])

    def start(self):
        self._local().start()
        for kk in range(7):
            self._copy(kk, True).start()

    def finish(self):
        for kk in range(7):
            self._copy(kk, False).wait_recv()
        for kk in range(7):
            self._copy(kk, True).wait_send()
        self._local().wait()


class _Swap:
    def __init__(self, arrs):
        self.nt = len(arrs)
        self.out_shape = [jax.ShapeDtypeStruct(a.shape, a.dtype) for a in arrs]
        self.scratch = [pltpu.SemaphoreType.DMA((self.nt,)), pltpu.SemaphoreType.DMA((self.nt,))]

    def bind(self, src, dst, send_sems, recv_sems):
        self.src, self.dst, self.send_sems, self.recv_sems = src, dst, send_sems, recv_sems

    def _copy(self, t):
        x, y, c, _ = _place()
        return pltpu.make_async_remote_copy(src_ref=self.src[t], dst_ref=self.dst[t], send_sem=self.send_sems.at[t],
                                            recv_sem=self.recv_sems.at[t], device_id=(x, y, 1 - c), device_id_type=MESH)

    def start(self):
        for t in range(self.nt):
            self._copy(t).start()

    def finish(self):
        for t in range(self.nt):
            self._copy(t).wait()


def _sum8(packs, name):
    def body(p_ref, o_ref):
        acc = p_ref[0]
        for dev in range(1, 8):
            acc = acc + p_ref[dev]
        o_ref[...] = acc

    return pl.pallas_call(body, out_shape=jax.ShapeDtypeStruct(packs.shape[1:], F32), name=name)(packs)


def _pack(arrs):
    flat = jnp.concatenate([a.reshape(-1).astype(F32) for a in arrs])
    rows = -(-flat.shape[0] // (8 * LANES)) * 8
    return jnp.pad(flat, (0, rows * LANES - flat.shape[0])).reshape(rows, LANES)


def _unpack(flat, shapes):
    out, pos = [], 0
    for shp in shapes:
        size = 1
        for s in shp:
            size *= s
        out.append(flat[pos:pos + size].reshape(shp))
        pos += size
    return out


BIG = ("attn_w_qkv", "attn_w_o", "pool_w", "xattn_w_q", "xattn_w_kv", "xattn_w_o", "ffn_w_up", "ffn_w_down")
BIG_AXIS = ("c", "r", "r", "r", "c", "r", "c", "r")
SMALL_REPL = ("attn_norm", "attn_q_gain", "attn_k_gain", "xattn_norm", "mem_norm", "ffn_norm", "ffn_conv_b", "final_norm")
SMALL_SHARD = ("pool_norm", "pool_scale", "ffn_conv_w")
ORDER = ("attn_norm", "attn_w_qkv", "attn_q_gain", "attn_k_gain", "attn_w_o", "pool_norm", "pool_w", "pool_scale",
         "xattn_norm", "mem_norm", "xattn_w_q", "xattn_w_kv", "xattn_w_o", "ffn_norm", "ffn_w_up", "ffn_conv_w",
         "ffn_conv_b", "ffn_w_down", "final_norm")


def _step(x, mem, tgt, w, m, v):
    seq, d = x.shape
    xi, yi, ci = lax.axis_index("x"), lax.axis_index("y"), lax.axis_index("c")
    chip = 2 * xi + yi
    dff = w["ffn_w_down"].shape[1] * 4
    n_layers = w["ffn_norm"].shape[0]

    def as3d(a):
        return a.reshape(a.shape[-3:])
    shards = [as3d(w[nm]).astype(BF16) for nm in BIG]
    small_in = [w[nm] for nm in SMALL_SHARD]
    small_pack = _pack(small_in)
    conv_b = w["ffn_conv_b"].reshape(n_layers, 1, -1)
    tabs = _rope_tables(seq)
    qg2 = jnp.tile(w["attn_q_gain"], (1, 2))
    kg2 = jnp.tile(w["attn_k_gain"], (1, 2))
    mm = functools.partial(_mm)

    saved = {}
    x0 = x
    h0, wq = _rms_fwd(x0, w["attn_norm"], BF16, "rms_attn", _Gather(shards[:1], BIG_AXIS[:1]), shards[:1])
    qkv = mm(h0, wq, "nn", b_l=0, out_dtype=F32, name="mm_qkv")
    q_r, k_r, k_t, v_b, v_t, small_all = _qk_prep(qkv, qg2, kg2, tabs, _GatherAll(small_pack), [small_pack], "qk_prep")
    per_chip = [_unpack(small_all[2 * j].reshape(-1), [a.shape for a in small_in]) for j in range(4)]
    pool_norm, pool_scale, conv_w = (jnp.concatenate([per_chip[j][i] for j in range(4)], axis=-1) for i in range(3))
    first = [slice(None)] * 5 + [slice(0, 1)] * 2
    o_at, lse, wo, wp, wxq, wxkv, wxo, wup, wdn = _flash_fwd(
        q_r, k_r, v_t, _Gather(shards[1:], BIG_AXIS[1:], first), shards[1:], "flash_fwd")
    ffn_w = {"up": wup, "down": wdn}
    x1, hq0 = mm(o_at, wo, "nn", b_l=0, res=x0, out_dtype=F32, norm_out=(w["xattn_norm"][0:1], BF16), name="mm_attn_o")

    def xattn_fwd(l, xin, hq):
        mn = _rms_fwd(mem, w["mem_norm"][l:l + 1], BF16, f"rms_mem{l}")
        xq = mm(hq, wxq, "nn", b_l=l, scale=X_HEAD_DIM ** -0.5, out_dtype=BF16, name=f"mm_xq{l}")
        kv = mm(mn, wxkv, "nn", b_l=l, out_dtype=BF16, name=f"mm_xkv{l}")
        xo = _xattn_fwd(xq, kv, f"xattn_fwd{l}")
        saved[f"x{l}"] = (hq, mn, xq, kv, xo)
        return mm(xo, wxo, "nn", b_l=l, res=xin, out_dtype=F32, norm_out=(w["ffn_norm"][l:l + 1], BF16), name=f"mm_xo{l}")

    def ffn_fwd(l, xin, hf, norm_out):
        ug = mm(hf, ffn_w["up"], "nn", b_l=l, n=dff, out_dtype=BF16, name=f"mm_up_g{l}")
        uv = mm(hf, ffn_w["up"], "nn", b_l=l, n=dff, b_off=(0, dff), out_dtype=BF16, name=f"mm_up_v{l}")
        if l == 0:
            rest = _Gather(shards[6:7], BIG_AXIS[6:7], [slice(1, 2)])
            act, ffn_w["up"] = _conv_gate_fwd(ug, uv, conv_w, conv_b, l, f"conv_gate{l}", rest, shards[6:7], [ffn_w["up"]])
        else:
            act = _conv_gate_fwd(ug, uv, conv_w, conv_b, l, f"conv_gate{l}")
        saved[f"f{l}"] = (hf, ug, uv, act)
        return mm(act, ffn_w["down"], "nn", b_l=l, res=xin, out_dtype=F32, norm_out=norm_out, name=f"mm_down{l}")

    x2, hf0 = xattn_fwd(0, x1, hq0)
    x3, hp = ffn_fwd(0, x2, hf0, (pool_norm, F32))
    x4, ffn_w["down"] = _pool_fwd(hp, x3, wp, pool_scale, "pool_fwd", _Gather(shards[7:], BIG_AXIS[7:], [slice(1, 2)]),
                                  shards[7:], [ffn_w["down"]])
    x5, hf1 = xattn_fwd(1, x4, _rms_fwd(x4, w["xattn_norm"][1:2], BF16, "rms_xq1"))
    xs = [x0, x1, x2, x3, x4, x5, ffn_fwd(1, x5, hf1, None)]
    dres, g_final, loss = _final_loss(xs[6], w["final_norm"].reshape(1, d), tgt, "final_loss")

    grads = {}
    gbuf = {}

    def dw(nm, a, b, layer, full, off=(0, 0), n=None, tn=None):
        gbuf[nm] = _mm(a, b, "tn", out_dtype=BF16, out_full=full, out_l=layer, out_off=off, n=n, tn=tn,
                       alias=gbuf.get(nm), name=f"dw_{nm}{layer}_{off[1]}")

    def ffn_bwd(l, xin, dres):
        hf, ug, uv, act = saved[f"f{l}"]
        wup, wdn = ffn_w["up"], ffn_w["down"]
        dw("ffn_w_down", act, dres, l, wdn.shape)
        dact = _mm(dres, wdn, "nt", b_l=l, out_dtype=BF16, name=f"mm_dact{l}")
        dug, duv, dwg, dwv = _conv_gate_bwd(ug, uv, dact, conv_w, conv_b, l, f"conv_gate_bwd{l}")
        dw("ffn_w_up", hf, dug, l, wup.shape, tn=1408)
        dw("ffn_w_up", hf, duv, l, wup.shape, off=(0, dff), tn=1408)
        dhf = _mm(dug, wup, "nt", b_l=l, n=d, out_dtype=F32, name=f"mm_dhf_g{l}")
        dres, dg = _mm(duv, wup, "nt", b_l=l, n=d, b_off=(0, dff), res=dhf, out_dtype=F32, tm=256,
                       norm_bwd=(xin, w["ffn_norm"][l:l + 1], dres), name=f"mm_dhf_v{l}")
        return dres, dg, jnp.concatenate([dwg[:3], dwv[:3]], axis=1), jnp.concatenate([dwg[3], dwv[3]], axis=0)

    def xattn_bwd(l, xin, dres):
        hq, mn, xq, kv, xo = saved[f"x{l}"]
        dw("xattn_w_o", xo, dres, l, wxo.shape)
        dxo = _mm(dres, wxo, "nt", b_l=l, out_dtype=BF16, name=f"mm_dxo{l}")
        dq, dkv = _xattn_bwd(xq, kv, dxo, f"xattn_bwd{l}")
        dw("xattn_w_q", hq, dq, l, wxq.shape)
        dw("xattn_w_kv", mn, dkv, l, wxkv.shape)
        dmn = _mm(dkv, wxkv, "nt", b_l=l, out_dtype=F32, name=f"mm_dmn{l}")
        _, dg_mem = _rms_bwd(mem, w["mem_norm"][l:l + 1], dmn, None, f"rms_mem_bwd{l}")
        dres, dg = _mm(dq, wxq, "nt", b_l=l, out_dtype=F32, norm_bwd=(xin, w["xattn_norm"][l:l + 1], dres),
                       name=f"mm_dhq{l}")
        return dres, dg, dg_mem

    g_ffn, g_xn, g_mn, g_cw, g_cb = [None] * n_layers, [None] * n_layers, [None] * n_layers, [None] * n_layers, [None] * n_layers
    dres, g_ffn[1], g_cw[1], g_cb[1] = ffn_bwd(1, xs[5], dres)
    dres, g_xn[1], g_mn[1] = xattn_bwd(1, xs[4], dres)
    dhp, g_pw, g_pscale = _pool_bwd(hp, dres, wp, pool_scale, "pool_bwd")
    dres, g_pnorm = _rms_bwd(xs[3], pool_norm, dhp, dres, "rms_pool_bwd")
    dres, g_ffn[0], g_cw[0], g_cb[0] = ffn_bwd(0, xs[2], dres)
    dres, g_xn[0], g_mn[0] = xattn_bwd(0, xs[1], dres)
    dw("attn_w_o", o_at, dres, 0, wo.shape)
    do = _mm(dres, wo, "nt", b_l=0, out_dtype=BF16, name="mm_do")
    gbuf["pool_w"] = g_pw.astype(BF16)
    early = [gbuf[nm] for nm in BIG[1:]]
    dq_r, dk_r, dv, *recv_early = _flash_bwd(q_r, k_r, k_t, v_b, do, o_at, lse, _Scatter(early, BIG_AXIS[1:]), early,
                                             "flash_bwd")
    def sum4(nm, rc):
        return _sum4(rc.reshape(4, -1, rc.shape[-1]), f"sum4_{nm}")
    sums_early = [sum4(nm, rc) for nm, rc in zip(BIG[1:], recv_early)]
    dqkv, dqg, dkg, *others_early = _qk_prep_bwd(qkv, dq_r, dk_r, dv, qg2, kg2, tabs, _Swap(sums_early), sums_early,
                                                 "qk_prep_bwd")
    dw("attn_w_qkv", h0, dqkv, 0, wq.shape)
    grad_x, g_an = _mm(dqkv, wq, "nt", b_l=0, out_dtype=F32, norm_bwd=(x0, w["attn_norm"], dres), name="mm_dh0")

    small_g = {
        "attn_norm": g_an, "attn_q_gain": dqg[:, :HEAD_DIM] + dqg[:, HEAD_DIM:], "attn_k_gain": dkg[:, :HEAD_DIM] + dkg[:, HEAD_DIM:],
        "xattn_norm": jnp.concatenate(g_xn, axis=0), "mem_norm": jnp.concatenate(g_mn, axis=0),
        "ffn_norm": jnp.concatenate(g_ffn, axis=0), "ffn_conv_b": jnp.stack(g_cb, axis=0), "final_norm": g_final.reshape(d),
        "pool_norm": g_pnorm, "pool_scale": g_pscale, "ffn_conv_w": jnp.stack(g_cw, axis=0)}
    names = SMALL_REPL + SMALL_SHARD
    small_pack = _pack([loss[0, :1]] + [small_g[nm] for nm in names])
    late = [gbuf[nm] for nm in BIG[:1]]
    small_all, recv_late = _comm_call([_GatherAll(small_pack), _Scatter(late, BIG_AXIS[:1])], [[small_pack], late],
                                      "reduce_small_scatter_qkv")
    total = _sum8(small_all, "sum_small")
    parts = _unpack(total.reshape(-1), [(1,)] + [small_g[nm].shape for nm in names])
    loss_out = parts[0][0]
    for nm, g in zip(names, parts[1:]):
        if nm in SMALL_SHARD:
            size = w[nm].shape[-1]
            g = lax.dynamic_slice_in_dim(g, chip * size, size, axis=g.ndim - 1)
        grads[nm] = g.reshape(w[nm].shape)

    packed = [_pack([src[nm] for nm in names]) for src in (w, grads, m, v)]
    _, sd, sm, sv = _adamw(packed[0], packed[1], None, packed[2], packed[3], "adamw_small")
    shapes = [w[nm].shape for nm in names]
    delta = dict(zip(names, _unpack(sd.reshape(-1), shapes)))
    new_m = dict(zip(names, _unpack(sm.reshape(-1), shapes)))
    new_v = dict(zip(names, _unpack(sv.reshape(-1), shapes)))

    sums_late = [sum4(BIG[0], recv_late)]
    others_late = _comm_call([_Swap(sums_late)], [sums_late], "swap_qkv")
    for nm, mine, other in zip(BIG, sums_late + sums_early, list(others_late) + others_early):
        cols = mine.shape[-1]
        outs = _adamw(w[nm].reshape(-1, cols), mine, other, m[nm].reshape(-1, cols), v[nm].reshape(-1, cols), f"adamw_{nm}")
        grads[nm], delta[nm], new_m[nm], new_v[nm] = (o.reshape(w[nm].shape) for o in outs)

    return loss_out, grad_x, grads, delta, new_m, new_v


def kernel(x, mem, attn_norm, attn_w_qkv, attn_q_gain, attn_k_gain, attn_w_o, pool_norm, pool_w, pool_scale, xattn_norm, mem_norm, xattn_w_q, xattn_w_kv, xattn_w_o, ffn_norm, ffn_w_up, ffn_conv_w, ffn_conv_b, ffn_w_down, final_norm, loss_target, m_attn_norm, m_attn_w_qkv, m_attn_q_gain, m_attn_k_gain, m_attn_w_o, m_pool_norm, m_pool_w, m_pool_scale, m_xattn_norm, m_mem_norm, m_xattn_w_q, m_xattn_w_kv, m_xattn_w_o, m_ffn_norm, m_ffn_w_up, m_ffn_conv_w, m_ffn_conv_b, m_ffn_w_down, m_final_norm, v_attn_norm, v_attn_w_qkv, v_attn_q_gain, v_attn_k_gain, v_attn_w_o, v_pool_norm, v_pool_w, v_pool_scale, v_xattn_norm, v_mem_norm, v_xattn_w_q, v_xattn_w_kv, v_xattn_w_o, v_ffn_norm, v_ffn_w_up, v_ffn_conv_w, v_ffn_conv_b, v_ffn_w_down, v_final_norm):
    given = dict(locals())
    w = {nm: given[nm] for nm in ORDER}
    m = {nm: given["m_" + nm] for nm in ORDER}
    v = {nm: given["v_" + nm] for nm in ORDER}
    seq, d = x.shape[1], x.shape[2]
    loss, grad_x, grads, delta, new_m, new_v = _step(
        x.reshape(seq, d), mem.reshape(mem.shape[1], d), loss_target.reshape(seq, d), w, m, v)
    return (loss, grad_x.reshape(x.shape), *[grads[nm] for nm in ORDER], *[delta[nm] for nm in ORDER],
            *[new_m[nm] for nm in ORDER], *[new_v[nm] for nm in ORDER])
```

```python
import functools
import itertools

import jax
import jax.numpy as jnp
from jax import lax
from jax.experimental import pallas as pl
from jax.experimental.pallas import tpu as pltpu

F32, BF16 = jnp.float32, jnp.bfloat16
EPS = 1e-6
GRID_W = 64
ROPE_THETA = 10000.0
HEAD_DIM = 64
N_HEADS = 16
N_KV = 4
X_HEADS = 4
X_HEAD_DIM = 256
POOL_GROUPS = 4
POOL_GROUP_W = 256
HALO = 16
HALO_TR = 2048
LANES = 128
ADAM_LR, ADAM_B1, ADAM_B2, ADAM_EPS, ADAM_WD, ADAM_STEP = 0.001, 0.9, 0.999, 1e-08, 0.01, 10
VMEM_LIMIT = 48 * 1024 * 1024
MESH = pl.DeviceIdType.MESH
NEG = -1e30
LOG2E = 1.4426950408889634
FLASH_TQ, FLASH_TK = 512, 4096
FLASH_SUB = 512
ANY = pl.BlockSpec(memory_space=pl.ANY)


def _cp(sem=None):
    return pltpu.CompilerParams(dimension_semantics=sem, vmem_limit_bytes=VMEM_LIMIT)


def _pick(n, cands):
    for c in cands:
        if c <= n and n % c == 0:
            return c
    return n


def _mm(a, b, mode, *, name, out_dtype, tm=None, tn=None, tk=None, n=None, k=None, b_l=None, b_off=(0, 0),
        res=None, scale=None, out_full=None, out_l=None, out_off=(0, 0), alias=None, norm_out=None, norm_bwd=None):
    if mode == "tn":
        K, M = a.shape
    else:
        M, K = a.shape
    bs = b.shape[-2:]
    if mode == "nn":
        K = k or K
        N = n or bs[1]
    elif mode == "nt":
        N = n or bs[0]
    else:
        N = n or bs[1]
    wide = (1408, 1024, 512, 256, 128)
    if mode == "tn":
        tm = tm or (M if M <= 1024 else _pick(M, wide))
        tk = tk or _pick(K, (2048, 1024, 512, 256, 128))
    else:
        small = K <= 1024 and N <= 1536 and norm_bwd is None
        tm = _pick(M, (tm or (1024 if small else 512), 512, 256, 128))
        tk = tk or (K if K <= 2816 else _pick(K, wide))
    tn = tn or (N if N <= 1536 else _pick(N, wide))
    assert M % tm == 0 and N % tn == 0 and K % tk == 0, (name, M, N, K, tm, tn, tk)
    nk = K // tk
    dims = {"nn": ((1,), (0,)), "nt": ((1,), (1,)), "tn": ((0,), (0,))}[mode]

    j_outer = nk == 1 and mode != "tn"

    def at(f):
        return (lambda j, i, kk: f(i, j, kk)) if j_outer else f

    if mode == "tn":
        a_spec = pl.BlockSpec((tk, tm), at(lambda i, j, kk: (kk, i)))
    else:
        a_spec = pl.BlockSpec((tm, tk), at(lambda i, j, kk: (i, kk)))
    if mode == "nt":
        bb, (d0, d1) = (tn, tk), (b_off[0] // tn, b_off[1] // tk)
        assert b_off[0] % tn == 0 and b_off[1] % tk == 0
        bidx = lambda i, j, kk: (j + d0, kk + d1)
    else:
        bb, (d0, d1) = (tk, tn), (b_off[0] // tk, b_off[1] // tn)
        assert b_off[0] % tk == 0 and b_off[1] % tn == 0
        bidx = lambda i, j, kk: (kk + d0, j + d1)
    if b.ndim == 3:
        b_spec = pl.BlockSpec((None,) + bb, at(lambda i, j, kk: (b_l,) + bidx(i, j, kk)))
    else:
        b_spec = pl.BlockSpec(bb, at(bidx))
    in_specs, operands = [a_spec, b_spec], [a, b]
    if res is not None:
        in_specs.append(pl.BlockSpec((tm, tn), at(lambda i, j, kk: (i, j))))
        operands.append(res)
    aliases = {}
    if alias is not None:
        aliases = {len(operands): 0}
        in_specs.append(ANY)
        operands.append(alias)
    if out_full is None:
        out_shape = jax.ShapeDtypeStruct((M, N), out_dtype)
        out_spec = pl.BlockSpec((tm, tn), at(lambda i, j, kk: (i, j)))
    else:
        assert out_off[0] % tm == 0 and out_off[1] % tn == 0
        o0, o1 = out_off[0] // tm, out_off[1] // tn
        out_shape = jax.ShapeDtypeStruct(out_full, out_dtype)
        out_spec = pl.BlockSpec((None, tm, tn), at(lambda i, j, kk: (out_l, i + o0, j + o1)))
    has_res, has_alias = res is not None, alias is not None
    grid = (N // tn, M // tm, nk) if j_outer else (M // tm, N // tn, nk)
    n_extra = 0
    if norm_out is not None or norm_bwd is not None:
        assert j_outer and tn == N and out_full is None, name
        row = pl.BlockSpec((tm, tn), at(lambda i, j, kk: (i, 0)))
        vec = pl.BlockSpec((1, tn), at(lambda i, j, kk: (0, 0)))
        if norm_out is not None:
            in_specs.append(vec)
            operands.append(norm_out[0])
            n_extra = 1
            out_shape = (out_shape, jax.ShapeDtypeStruct((M, N), norm_out[1]))
            out_spec = (out_spec, row)
        else:
            in_specs += [row, vec, row]
            operands += list(norm_bwd)
            n_extra = 3
            out_shape = (out_shape, jax.ShapeDtypeStruct((1, N), F32), jax.ShapeDtypeStruct((M, N), BF16))
            out_spec = (out_spec, vec, row)
    n_out = 1 if n_extra == 0 else (2 if norm_out is not None else 3)

    def body(*refs):
        a_ref, b_ref = refs[0], refs[1]
        pos = 2
        res_ref = None
        if has_res:
            res_ref = refs[pos]
            pos += 1
        if has_alias:
            pos += 1
        extra = refs[pos:pos + n_extra]
        pos += n_extra
        o_ref, acc_ref = refs[pos], refs[pos + n_out]
        kk = pl.program_id(2)
        part = lax.dot_general(a_ref[...].astype(BF16), b_ref[...].astype(BF16), (dims, ((), ())),
                               preferred_element_type=F32)

        def finish(acc):
            if scale is not None:
                acc = acc * scale
            if res_ref is not None:
                acc = acc + res_ref[...]
            if norm_out is not None:
                r = lax.rsqrt(jnp.mean(acc * acc, axis=-1, keepdims=True) + EPS)
                refs[pos + 1][...] = (acc * r * extra[0][...]).astype(refs[pos + 1].dtype)
            if norm_bwd is not None:
                x_ref, g_ref, dres_ref = extra
                dg_ref, step = refs[pos + 1], pl.program_id(1)
                xv = x_ref[...]
                r = lax.rsqrt(jnp.mean(xv * xv, axis=-1, keepdims=True) + EPS)
                nv = xv * r
                dgp = jnp.sum(acc * nv, axis=0, keepdims=True)

                @pl.when(step == 0)
                def _():
                    dg_ref[...] = dgp

                @pl.when(step > 0)
                def _():
                    dg_ref[...] += dgp

                dn = acc * g_ref[...]
                acc = dres_ref[...] + r * (dn - nv * jnp.mean(dn * nv, axis=-1, keepdims=True))
                refs[pos + 2][...] = acc.astype(BF16)
            o_ref[...] = acc.astype(o_ref.dtype)

        if nk == 1:
            finish(part)
        else:
            @pl.when(kk == 0)
            def _():
                acc_ref[...] = part

            @pl.when(jnp.logical_and(kk > 0, kk < nk - 1))
            def _():
                acc_ref[...] += part

            @pl.when(kk == nk - 1)
            def _():
                finish(acc_ref[...] + part)

    return pl.pallas_call(
        body, out_shape=out_shape, grid=grid, in_specs=in_specs, out_specs=out_spec,
        scratch_shapes=[pltpu.VMEM((tm, tn) if nk > 1 else (8, 128), F32)], input_output_aliases=aliases,
        compiler_params=_cp(("arbitrary",) * 3 if norm_bwd is not None else ("parallel", "parallel", "arbitrary")),
        name=name)(*operands)


def _hosted(plan, nsteps, step, compute):
    if plan is None:
        return compute()

    @pl.when(step == 0)
    def _():
        plan.start()

    compute()

    @pl.when(step == nsteps - 1)
    def _():
        if hasattr(plan, "forward"):
            plan.forward()
        plan.finish()


def _rms_fwd(x, gain, out_dtype, name, plan=None, sends=()):
    rows, d = x.shape
    tr = _pick(rows, (512, 256))
    nt, nsteps = (plan.nt if plan is not None else 0), rows // tr

    def body(x_ref, g_ref, *rest):
        o_ref = rest[nt]
        if plan is not None:
            plan.bind(rest[:nt], rest[nt + 1:2 * nt + 1], *rest[2 * nt + 1:])

        def compute():
            xv = x_ref[...]
            r = lax.rsqrt(jnp.mean(xv * xv, axis=-1, keepdims=True) + EPS)
            o_ref[...] = (xv * r * g_ref[...]).astype(o_ref.dtype)

        _hosted(plan, nsteps, pl.program_id(0), compute)

    out = jax.ShapeDtypeStruct((rows, d), out_dtype)
    row = pl.BlockSpec((tr, d), lambda i: (i, 0))
    in_specs = [row, pl.BlockSpec((1, d), lambda i: (0, 0))]
    if plan is None:
        return pl.pallas_call(body, out_shape=out, grid=(nsteps,), in_specs=in_specs, out_specs=row,
                              compiler_params=_cp(("parallel",)), name=name)(x, gain)
    return pl.pallas_call(
        body, out_shape=(out, *plan.out_shape), grid=(nsteps,), in_specs=in_specs + [ANY] * nt,
        out_specs=(row, *([ANY] * nt)), scratch_shapes=plan.scratch, compiler_params=_cp(("arbitrary",)),
        name=name)(x, gain, *sends)


def _rms_bwd(x, gain, dh, dres, name):
    rows, d = x.shape
    tr = _pick(rows, (512, 256))
    need_dx = dres is not None

    def body(*refs):
        if need_dx:
            x_ref, g_ref, dh_ref, dres_ref, o_ref, dg_ref = refs
        else:
            x_ref, g_ref, dh_ref, dg_ref = refs
        i = pl.program_id(0)
        xv = x_ref[...]
        dhv = dh_ref[...].astype(F32)
        r = lax.rsqrt(jnp.mean(xv * xv, axis=-1, keepdims=True) + EPS)
        nv = xv * r
        part = jnp.sum(dhv * nv, axis=0, keepdims=True)

        @pl.when(i == 0)
        def _():
            dg_ref[...] = part

        @pl.when(i > 0)
        def _():
            dg_ref[...] += part

        if need_dx:
            dn = dhv * g_ref[...]
            dx = r * (dn - nv * jnp.mean(dn * nv, axis=-1, keepdims=True))
            o_ref[...] = dres_ref[...] + dx

    row_spec = pl.BlockSpec((tr, d), lambda i: (i, 0))
    vec_spec = pl.BlockSpec((1, d), lambda i: (0, 0))
    if need_dx:
        return pl.pallas_call(
            body, out_shape=(jax.ShapeDtypeStruct((rows, d), F32), jax.ShapeDtypeStruct((1, d), F32)),
            grid=(rows // tr,), in_specs=[row_spec, vec_spec, row_spec, row_spec], out_specs=(row_spec, vec_spec),
            compiler_params=_cp(("arbitrary",)), name=name)(x, gain, dh, dres)
    return None, pl.pallas_call(
        body, out_shape=jax.ShapeDtypeStruct((1, d), F32), grid=(rows // tr,),
        in_specs=[row_spec, vec_spec, row_spec], out_specs=vec_spec,
        compiler_params=_cp(("arbitrary",)), name=name)(x, gain, dh)


def _final_loss(x, gain, target, name):
    rows, d = x.shape
    tr = _pick(rows, (512, 256))
    nsteps = rows // tr

    def body(x_ref, g_ref, t_ref, dx_ref, dg_ref, loss_ref, dxb_ref, acc_ref):
        i = pl.program_id(0)
        xv = x_ref[...]
        g = g_ref[...]
        r = lax.rsqrt(jnp.mean(xv * xv, axis=-1, keepdims=True) + EPS)
        nv = xv * r
        err = nv * g - t_ref[...]
        dy = err * (1.0 / d)
        dn = dy * g
        dx = r * (dn - nv * jnp.mean(dn * nv, axis=-1, keepdims=True))
        dx_ref[...] = dx
        dxb_ref[...] = dx.astype(BF16)
        dgp = jnp.sum(dy * nv, axis=0, keepdims=True)
        lp = jnp.sum(err * err, axis=0, keepdims=True)

        @pl.when(i == 0)
        def _():
            dg_ref[...] = dgp
            acc_ref[...] = lp

        @pl.when(i > 0)
        def _():
            dg_ref[...] += dgp
            acc_ref[...] += lp

        @pl.when(i == nsteps - 1)
        def _():
            tot = jnp.sum(acc_ref[...], axis=1, keepdims=True) * (0.5 / d)
            loss_ref[...] = jnp.broadcast_to(tot, loss_ref.shape)

    row_spec = pl.BlockSpec((tr, d), lambda i: (i, 0))
    vec_spec = pl.BlockSpec((1, d), lambda i: (0, 0))
    return pl.pallas_call(
        body, out_shape=(jax.ShapeDtypeStruct((rows, d), F32), jax.ShapeDtypeStruct((1, d), F32),
                         jax.ShapeDtypeStruct((1, LANES), F32), jax.ShapeDtypeStruct((rows, d), BF16)),
        grid=(nsteps,), in_specs=[row_spec, vec_spec, row_spec],
        out_specs=(row_spec, vec_spec, pl.BlockSpec((1, LANES), lambda i: (0, 0)), row_spec),
        scratch_shapes=[pltpu.VMEM((1, d), F32)], compiler_params=_cp(("arbitrary",)), name=name)(x, gain, target)


def _rope_tables(seq):
    pairs = HEAD_DIM // 4
    lane = jnp.arange(LANES, dtype=jnp.int32) % HEAD_DIM
    by_col, second, pair = lane // (2 * pairs) == 1, (lane % (2 * pairs)) // pairs == 1, lane % pairs
    inv_freq = ROPE_THETA ** (-pair.astype(F32) / pairs)
    t = jnp.arange(seq, dtype=jnp.int32)[:, None]
    pos = jnp.where(by_col[None, :], t % GRID_W, t // GRID_W).astype(F32)
    ang = pos * inv_freq[None, :]
    cos, sin = jnp.cos(ang), jnp.sin(ang)
    return cos, jnp.where(second[None, :], sin, 0.0), jnp.where(second[None, :], 0.0, -sin)


def _pair_norm(xv, lo):
    sq = xv * xv
    s_lo = jnp.sum(jnp.where(lo, sq, 0.0), axis=1, keepdims=True)
    s_hi = jnp.sum(jnp.where(lo, 0.0, sq), axis=1, keepdims=True)
    return lax.rsqrt(jnp.where(lo, s_lo, s_hi) * (1.0 / HEAD_DIM) + EPS)


def _rope(y, c, sp, sm):
    return y * c + pltpu.roll(y, 16, axis=1) * sp + pltpu.roll(y, LANES - 16, axis=1) * sm


def _rope_t(dz, c, sp, sm):
    return dz * c + pltpu.roll(dz * sp, LANES - 16, axis=1) + pltpu.roll(dz * sm, 16, axis=1)


def _qk_prep(qkv, qg2, kg2, tabs, plan, sends, name):
    seq = qkv.shape[0]
    ts = _pick(seq, (512, 256, 128))
    nq, nkp = N_HEADS // 2, N_KV // 2
    qw, kw = N_HEADS * HEAD_DIM, N_KV * HEAD_DIM
    nt, nsteps = plan.nt, seq // ts

    def body(x_ref, qg_ref, kg_ref, c_ref, sp_ref, sm_ref, *rest):
        q_ref, k_ref, kt_ref, v_ref, vt_ref = rest[nt:nt + 5]
        plan.bind(rest[:nt], rest[nt + 5:2 * nt + 5], *rest[2 * nt + 5:])
        _hosted(plan, nsteps, pl.program_id(0), lambda: compute(x_ref, qg_ref, kg_ref, c_ref, sp_ref, sm_ref,
                                                                q_ref, k_ref, kt_ref, v_ref, vt_ref))

    def compute(x_ref, qg_ref, kg_ref, c_ref, sp_ref, sm_ref, q_ref, k_ref, kt_ref, v_ref, vt_ref):
        lo = lax.broadcasted_iota(jnp.int32, (ts, LANES), 1) < HEAD_DIM
        top = lax.broadcasted_iota(jnp.int32, (LANES, ts), 0) < HEAD_DIM
        c, sp, sm = c_ref[...], sp_ref[...], sm_ref[...]
        for i in range(nq):
            xv = x_ref[:, i * LANES:(i + 1) * LANES]
            y = xv * _pair_norm(xv, lo) * qg_ref[...]
            q_ref[:, i * LANES:(i + 1) * LANES] = (_rope(y, c, sp, sm) * (LOG2E * HEAD_DIM ** -0.5)).astype(BF16)
        for i in range(nkp):
            xv = x_ref[:, qw + i * LANES:qw + (i + 1) * LANES]
            z = _rope(xv * _pair_norm(xv, lo) * kg_ref[...], c, sp, sm)
            k_ref[:, i * LANES:(i + 1) * LANES] = z.astype(BF16)
            kt_ref[i * LANES:(i + 1) * LANES, :] = z.T.astype(BF16)
            vv = x_ref[:, qw + kw + i * LANES:qw + kw + (i + 1) * LANES]
            v_ref[:, i * LANES:(i + 1) * LANES] = vv.astype(BF16)
            vvt = vv.T
            vt_ref[(2 * i) * LANES:(2 * i + 1) * LANES, :] = jnp.where(top, vvt, 1.0).astype(BF16)
            vt_ref[(2 * i + 1) * LANES:(2 * i + 2) * LANES, :] = jnp.where(top, 1.0, vvt).astype(BF16)

    tab = pl.BlockSpec((ts, LANES), lambda i: (i, 0))
    vec = pl.BlockSpec((1, LANES), lambda i: (0, 0))
    return pl.pallas_call(
        body,
        out_shape=(jax.ShapeDtypeStruct((seq, qw), BF16), jax.ShapeDtypeStruct((seq, kw), BF16),
                   jax.ShapeDtypeStruct((kw, seq), BF16), jax.ShapeDtypeStruct((seq, kw), BF16),
                   jax.ShapeDtypeStruct((N_KV * LANES, seq), BF16), *plan.out_shape),
        grid=(nsteps,),
        in_specs=[pl.BlockSpec((ts, qw + 2 * kw), lambda i: (i, 0)), vec, vec, tab, tab, tab] + [ANY] * nt,
        out_specs=(pl.BlockSpec((ts, qw), lambda i: (i, 0)), pl.BlockSpec((ts, kw), lambda i: (i, 0)),
                   pl.BlockSpec((kw, ts), lambda i: (0, i)), pl.BlockSpec((ts, kw), lambda i: (i, 0)),
                   pl.BlockSpec((N_KV * LANES, ts), lambda i: (0, i)), *([ANY] * nt)),
        scratch_shapes=plan.scratch, compiler_params=_cp(("arbitrary",)), name=name)(qkv, qg2, kg2, *tabs, *sends)


def _qk_prep_bwd(qkv, dq, dk, dv, qg2, kg2, tabs, plan, sends, name):
    seq = qkv.shape[0]
    ts = _pick(seq, (512, 256, 128))
    nq, nkp = N_HEADS // 2, N_KV // 2
    qw, kw = N_HEADS * HEAD_DIM, N_KV * HEAD_DIM
    nt, nsteps = plan.nt, seq // ts

    def body(x_ref, dq_ref, dk_ref, dv_ref, qg_ref, kg_ref, c_ref, sp_ref, sm_ref, *rest):
        o_ref, dqg_ref, dkg_ref = rest[nt:nt + 3]
        plan.bind(rest[:nt], rest[nt + 3:2 * nt + 3], *rest[2 * nt + 3:])
        step = pl.program_id(0)

        @pl.when(step == 0)
        def _():
            plan.start()

        lo = lax.broadcasted_iota(jnp.int32, (ts, LANES), 1) < HEAD_DIM
        c, sp, sm = c_ref[...], sp_ref[...], sm_ref[...]

        def one(xv, dz, gain):
            r = _pair_norm(xv, lo)
            nv = xv * r
            dy = _rope_t(dz, c, sp, sm)
            dgp = jnp.sum(dy * nv, axis=0, keepdims=True)
            dn = dy * gain
            t = dn * nv
            m_lo = jnp.sum(jnp.where(lo, t, 0.0), axis=1, keepdims=True)
            m_hi = jnp.sum(jnp.where(lo, 0.0, t), axis=1, keepdims=True)
            m = jnp.where(lo, m_lo, m_hi) * (1.0 / HEAD_DIM)
            return r * (dn - nv * m), dgp

        dqg = jnp.zeros((1, LANES), F32)
        for i in range(nq):
            sl = slice(i * LANES, (i + 1) * LANES)
            dx, dgp = one(x_ref[:, sl], dq_ref[:, sl] * (HEAD_DIM ** -0.5), qg_ref[...])
            o_ref[:, sl] = dx.astype(BF16)
            dqg = dqg + dgp
        dkg = jnp.zeros((1, LANES), F32)
        for i in range(nkp):
            sl = slice(i * LANES, (i + 1) * LANES)
            dx, dgp = one(x_ref[:, qw + i * LANES:qw + (i + 1) * LANES], dk_ref[:, sl], kg_ref[...])
            o_ref[:, qw + i * LANES:qw + (i + 1) * LANES] = dx.astype(BF16)
            dkg = dkg + dgp
            o_ref[:, qw + kw + i * LANES:qw + kw + (i + 1) * LANES] = dv_ref[:, sl].astype(BF16)

        @pl.when(step == 0)
        def _():
            dqg_ref[...] = dqg
            dkg_ref[...] = dkg

        @pl.when(step > 0)
        def _():
            dqg_ref[...] += dqg
            dkg_ref[...] += dkg

        @pl.when(step == nsteps - 1)
        def _():
            plan.finish()

    tab = pl.BlockSpec((ts, LANES), lambda i: (i, 0))
    vec = pl.BlockSpec((1, LANES), lambda i: (0, 0))
    return pl.pallas_call(
        body,
        out_shape=(jax.ShapeDtypeStruct((seq, qw + 2 * kw), BF16), jax.ShapeDtypeStruct((1, LANES), F32),
                   jax.ShapeDtypeStruct((1, LANES), F32), *plan.out_shape),
        grid=(nsteps,),
        in_specs=[pl.BlockSpec((ts, qw + 2 * kw), lambda i: (i, 0)), pl.BlockSpec((ts, qw), lambda i: (i, 0)),
                  pl.BlockSpec((ts, kw), lambda i: (i, 0)), pl.BlockSpec((ts, kw), lambda i: (i, 0)),
                  vec, vec, tab, tab, tab] + [ANY] * nt,
        out_specs=(pl.BlockSpec((ts, qw + 2 * kw), lambda i: (i, 0)), vec, vec, *([ANY] * nt)),
        scratch_shapes=plan.scratch, compiler_params=_cp(("arbitrary",)), name=name)(qkv, dq, dk, dv, qg2, kg2, *tabs, *sends)


def _slot(blk, off0, tq):
    half = lax.broadcasted_iota(jnp.int32, (tq, LANES), 1) // HEAD_DIM
    keep = half == jnp.where(off0, 0, 1)
    parts = []
    for p in range(2):
        pair = blk[:, p * LANES:(p + 1) * LANES].astype(F32)
        rolled = pltpu.roll(pair, HEAD_DIM, axis=1)
        parts.append(jnp.where(keep, jnp.where(off0, pair, rolled), 0.0))
        parts.append(jnp.where(keep, jnp.where(off0, rolled, pair), 0.0))
    return jnp.concatenate(parts, axis=0)


def _unslot(x4, off0, tq):
    lo = lax.broadcasted_iota(jnp.int32, (tq, LANES), 1) < HEAD_DIM
    pairs = []
    for p in range(2):
        h0 = x4[(2 * p) * tq:(2 * p + 1) * tq]
        h1 = x4[(2 * p + 1) * tq:(2 * p + 2) * tq]
        a = jnp.where(off0, h0, pltpu.roll(h0, HEAD_DIM, axis=1))
        b = jnp.where(off0, pltpu.roll(h1, HEAD_DIM, axis=1), h1)
        pairs.append(jnp.where(lo, a, b))
    return jnp.concatenate(pairs, axis=1)


def _flash_fwd(q, k, vt, plan, shards, name):
    seq = q.shape[0]
    tq = _pick(seq, (FLASH_TQ, 128))
    tk = _pick(seq, (FLASH_TK, 2048, 512, 256, 128))
    sub = _pick(tk, (FLASH_SUB, 256, 128))
    nq, nkv, nsub = seq // tq, seq // tk, tk // sub
    gw = 4 * HEAD_DIM
    nt = plan.nt

    def body(q_ref, k_ref, vt_ref, *rest):
        o_ref, lse_ref = rest[nt:nt + 2]
        q4_ref, m_ref, acc_ref, st_ref = rest[2 * nt + 2:2 * nt + 6]
        plan.bind(rest[:nt], rest[nt + 2:2 * nt + 2], *rest[2 * nt + 6:])
        g, qi, ki = pl.program_id(0), pl.program_id(1), pl.program_id(2)
        off0 = (g % 2) == 0
        @pl.when(jnp.logical_and(g == 0, jnp.logical_and(qi == 0, ki == 0)))
        def _():
            plan.start()

        @pl.when(jnp.logical_and(g == N_KV - 1, jnp.logical_and(qi == nq // 2, ki == 0)))
        def _():
            plan.forward()

        @pl.when(ki == 0)
        def _():
            q4_ref[...] = _slot(q_ref[...], off0, tq).astype(BF16)
            m_ref[...] = jnp.full(m_ref.shape, NEG, F32)
            acc_ref[...] = jnp.zeros(acc_ref.shape, F32)

        q4 = q4_ref[...]

        def scores(c):
            st_ref[c % 2] = lax.dot_general(k_ref[c * sub:(c + 1) * sub, :], q4, (((1,), (1,)), ((), ())),
                                            preferred_element_type=F32)

        m, acc = m_ref[...], acc_ref[...]
        scores(0)
        for c in range(nsub):
            if c + 1 < nsub:
                scores(c + 1)
            st = st_ref[c % 2]
            m_new = jnp.maximum(m, jnp.max(st, axis=0, keepdims=True))
            pt = jnp.exp2(st - m_new).astype(BF16)
            acc = jnp.exp2(m - m_new) * acc + jnp.dot(vt_ref[:, c * sub:(c + 1) * sub], pt, preferred_element_type=F32)
            m = m_new
        m_ref[...] = m
        acc_ref[...] = acc

        @pl.when(ki == nkv - 1)
        def _():
            acc = acc_ref[...]
            l = jnp.where(off0, acc[HEAD_DIM:HEAD_DIM + 1], acc[0:1])
            o4 = acc.T
            o4 = o4 / pltpu.roll(o4, HEAD_DIM, axis=1)
            o_ref[...] = _unslot(o4, off0, tq).astype(o_ref.dtype)
            lse_ref[...] = jnp.broadcast_to(m_ref[...] + jnp.log2(l), lse_ref.shape)

        @pl.when(jnp.logical_and(g == N_KV - 1, jnp.logical_and(qi == nq - 1, ki == nkv - 1)))
        def _():
            plan.finish()

    return pl.pallas_call(
        body,
        out_shape=(jax.ShapeDtypeStruct((seq, N_HEADS * HEAD_DIM), BF16),
                   jax.ShapeDtypeStruct((N_KV * nq * 8, 4 * tq), F32), *plan.out_shape),
        grid=(N_KV, nq, nkv),
        in_specs=[pl.BlockSpec((tq, gw), lambda g, qi, ki: (qi, g)),
                  pl.BlockSpec((tk, LANES), lambda g, qi, ki: (ki, g // 2)),
                  pl.BlockSpec((LANES, tk), lambda g, qi, ki: (g, ki))] + [ANY] * nt,
        out_specs=(pl.BlockSpec((tq, gw), lambda g, qi, ki: (qi, g)),
                   pl.BlockSpec((8, 4 * tq), lambda g, qi, ki: (g * nq + qi, 0)), *([ANY] * nt)),
        scratch_shapes=[pltpu.VMEM((4 * tq, LANES), BF16), pltpu.VMEM((1, 4 * tq), F32),
                        pltpu.VMEM((LANES, 4 * tq), F32), pltpu.VMEM((2, sub, 4 * tq), F32)] + plan.scratch,
        compiler_params=_cp(("arbitrary", "arbitrary", "arbitrary")), name=name)(q, k, vt, *shards)


def _flash_bwd(q, k, kt, v, do, o, lse, plan, grads, name):
    seq = q.shape[0]
    tq = _pick(seq, (FLASH_TQ, 128))
    tk = _pick(seq, (FLASH_TK, 2048, 512, 256, 128))
    sub = _pick(tk, (FLASH_SUB, 256, 128))
    nq, nkv, nsub = seq // tq, seq // tk, tk // sub
    gw = 4 * HEAD_DIM
    nt = plan.nt

    def body(q_ref, k_ref, kt_ref, v_ref, do_ref, o_ref, lse_ref, *rest):
        dq_ref, dk_ref, dv_ref = rest[nt:nt + 3]
        q4_ref, do4_ref, delta_ref, dqt_ref, st_ref, dpt_ref = rest[2 * nt + 3:2 * nt + 9]
        plan.bind(rest[:nt], rest[nt + 3:2 * nt + 3], *rest[2 * nt + 9:])
        g, qi, ki = pl.program_id(0), pl.program_id(1), pl.program_id(2)
        off0 = (g % 2) == 0

        @pl.when(jnp.logical_and(g == 0, jnp.logical_and(qi == 0, ki == 0)))
        def _():
            plan.start()

        @pl.when(jnp.logical_and(g % 2 == 0, jnp.logical_and(qi == 0, ki == 0)))
        def _():
            dk_ref[...] = jnp.zeros(dk_ref.shape, F32)
            dv_ref[...] = jnp.zeros(dv_ref.shape, F32)

        @pl.when(ki == 0)
        def _():
            q4_ref[...] = _slot(q_ref[...], off0, tq).astype(BF16)
            do4 = _slot(do_ref[...], off0, tq)
            do4_ref[...] = do4.astype(BF16)
            o4 = _slot(o_ref[...], off0, tq)
            delta_ref[...] = jnp.sum((do4 * o4).T, axis=0, keepdims=True)
            dqt_ref[...] = jnp.zeros(dqt_ref.shape, F32)

        q4, do4 = q4_ref[...], do4_ref[...]
        lse_row, delta = lse_ref[0:1, :], delta_ref[...]

        def products(c):
            rows = slice(c * sub, (c + 1) * sub)
            st_ref[c % 2] = lax.dot_general(k_ref[rows, :], q4, (((1,), (1,)), ((), ())), preferred_element_type=F32)
            dpt_ref[c % 2] = lax.dot_general(v_ref[rows, :], do4, (((1,), (1,)), ((), ())), preferred_element_type=F32)

        dqt = dqt_ref[...]
        products(0)
        for c in range(nsub):
            if c + 1 < nsub:
                products(c + 1)
            pt = jnp.exp2(st_ref[c % 2] - lse_row)
            dst = (pt * (dpt_ref[c % 2] - delta)).astype(BF16)
            rows = pl.ds(pl.multiple_of(ki * tk + c * sub, sub), sub)
            dv_ref[rows, :] += jnp.dot(pt.astype(BF16), do4, preferred_element_type=F32)
            dk_ref[rows, :] += jnp.dot(dst, q4, preferred_element_type=F32) * (1.0 / LOG2E)
            dqt = dqt + jnp.dot(kt_ref[:, c * sub:(c + 1) * sub], dst, preferred_element_type=F32)
        dqt_ref[...] = dqt

        @pl.when(ki == nkv - 1)
        def _():
            dq_ref[...] = _unslot(dqt_ref[...].T, off0, tq)

        @pl.when(jnp.logical_and(g == N_KV - 1, jnp.logical_and(qi == nq - 1, ki == nkv - 1)))
        def _():
            plan.finish()

    return pl.pallas_call(
        body,
        out_shape=(jax.ShapeDtypeStruct((seq, N_HEADS * HEAD_DIM), F32),
                   jax.ShapeDtypeStruct((seq, N_KV * HEAD_DIM), F32), jax.ShapeDtypeStruct((seq, N_KV * HEAD_DIM), F32),
                   *plan.out_shape),
        grid=(N_KV, nq, nkv),
        in_specs=[pl.BlockSpec((tq, gw), lambda g, qi, ki: (qi, g)),
                  pl.BlockSpec((tk, LANES), lambda g, qi, ki: (ki, g // 2)),
                  pl.BlockSpec((LANES, tk), lambda g, qi, ki: (g // 2, ki)),
                  pl.BlockSpec((tk, LANES), lambda g, qi, ki: (ki, g // 2)),
                  pl.BlockSpec((tq, gw), lambda g, qi, ki: (qi, g)),
                  pl.BlockSpec((tq, gw), lambda g, qi, ki: (qi, g)),
                  pl.BlockSpec((8, 4 * tq), lambda g, qi, ki: (g * nq + qi, 0))] + [ANY] * nt,
        out_specs=(pl.BlockSpec((tq, gw), lambda g, qi, ki: (qi, g)),
                   pl.BlockSpec((seq, LANES), lambda g, qi, ki: (0, g // 2)),
                   pl.BlockSpec((seq, LANES), lambda g, qi, ki: (0, g // 2)), *([ANY] * nt)),
        scratch_shapes=[pltpu.VMEM((4 * tq, LANES), BF16), pltpu.VMEM((4 * tq, LANES), BF16),
                        pltpu.VMEM((1, 4 * tq), F32), pltpu.VMEM((LANES, 4 * tq), F32),
                        pltpu.VMEM((2, sub, 4 * tq), F32), pltpu.VMEM((2, sub, 4 * tq), F32)] + plan.scratch,
        compiler_params=_cp(("arbitrary", "arbitrary", "arbitrary")), name=name)(q, k, kt, v, do, o, lse, *grads)


def _xattn_fwd(q, kv, name):
    seq, d = q.shape
    mlen = kv.shape[0]
    tq = _pick(seq, (1024, 512, 256))

    def body(q_ref, k_ref, v_ref, o_ref):
        for h in range(X_HEADS):
            sl = slice(h * X_HEAD_DIM, (h + 1) * X_HEAD_DIM)
            s = lax.dot_general(q_ref[:, sl], k_ref[:, sl], (((1,), (1,)), ((), ())), preferred_element_type=F32)
            e = jnp.exp(s - jnp.max(s, axis=-1, keepdims=True))
            p = e / jnp.sum(e, axis=-1, keepdims=True)
            o_ref[:, sl] = jnp.dot(p.astype(BF16), v_ref[:, sl], preferred_element_type=F32).astype(o_ref.dtype)

    return pl.pallas_call(
        body, out_shape=jax.ShapeDtypeStruct((seq, d), BF16), grid=(seq // tq,),
        in_specs=[pl.BlockSpec((tq, d), lambda i: (i, 0)), pl.BlockSpec((mlen, d), lambda i: (0, 0)),
                  pl.BlockSpec((mlen, d), lambda i: (0, 1))],
        out_specs=pl.BlockSpec((tq, d), lambda i: (i, 0)), compiler_params=_cp(("parallel",)), name=name)(q, kv, kv)


def _xattn_bwd(q, kv, do, name):
    seq, d = q.shape
    mlen = kv.shape[0]
    tq = _pick(seq, (1024, 512, 256))
    scale = X_HEAD_DIM ** -0.5

    def body(q_ref, k_ref, v_ref, do_ref, dq_ref, dkv_ref):
        i = pl.program_id(0)

        @pl.when(i == 0)
        def _():
            dkv_ref[...] = jnp.zeros(dkv_ref.shape, F32)

        for h in range(X_HEADS):
            sl = slice(h * X_HEAD_DIM, (h + 1) * X_HEAD_DIM)
            qh, kh, vh = q_ref[:, sl], k_ref[:, sl], v_ref[:, sl]
            doh = do_ref[:, sl].astype(BF16)
            st = lax.dot_general(kh, qh, (((1,), (1,)), ((), ())), preferred_element_type=F32)
            e = jnp.exp(st - jnp.max(st, axis=0, keepdims=True))
            pt = e / jnp.sum(e, axis=0, keepdims=True)
            dpt = lax.dot_general(vh, doh, (((1,), (1,)), ((), ())), preferred_element_type=F32)
            dst = (pt * (dpt - jnp.sum(pt * dpt, axis=0, keepdims=True))).astype(BF16)
            dkv_ref[:, sl] += jnp.dot(dst, qh, preferred_element_type=F32)
            dkv_ref[:, d + h * X_HEAD_DIM:d + (h + 1) * X_HEAD_DIM] += jnp.dot(pt.astype(BF16), doh,
                                                                                 preferred_element_type=F32)
            dqh = lax.dot_general(dst, kh, (((0,), (0,)), ((), ())), preferred_element_type=F32)
            dq_ref[:, sl] = (dqh * scale).astype(dq_ref.dtype)

    return pl.pallas_call(
        body, out_shape=(jax.ShapeDtypeStruct((seq, d), BF16), jax.ShapeDtypeStruct((mlen, 2 * d), F32)),
        grid=(seq // tq,),
        in_specs=[pl.BlockSpec((tq, d), lambda i: (i, 0)), pl.BlockSpec((mlen, d), lambda i: (0, 0)),
                  pl.BlockSpec((mlen, d), lambda i: (0, 1)), pl.BlockSpec((tq, d), lambda i: (i, 0))],
        out_specs=(pl.BlockSpec((tq, d), lambda i: (i, 0)), pl.BlockSpec((mlen, 2 * d), lambda i: (0, 0))),
        compiler_params=_cp(("arbitrary",)), name=name)(q, kv, kv, do)


def _halo_specs(tr, tc, seq, col):
    per, last = tr // HALO, seq // HALO - 1
    return [pl.BlockSpec((tr, tc), lambda j, r: (r, col(j))),
            pl.BlockSpec((HALO, tc), lambda j, r: (jnp.maximum(r * per - 1, 0), col(j))),
            pl.BlockSpec((HALO, tc), lambda j, r: (jnp.minimum((r + 1) * per, last), col(j)))]


def _extend(main_ref, prev_ref, next_ref, r, nr):
    pv = (r > 0).astype(F32)
    nv = (r < nr - 1).astype(F32)
    return jnp.concatenate([prev_ref[...].astype(F32) * pv, main_ref[...].astype(F32),
                            next_ref[...].astype(F32) * nv], axis=0)


def _conv3(e, w_ref, n):
    return pltpu.roll(e, 1, axis=0) * w_ref[0:1, :] + e * w_ref[1:2, :] + pltpu.roll(e, n - 1, axis=0) * w_ref[2:3, :]


def _conv_gate_fwd(ug, uv, cw, cb, layer, name, plan=None, shards=(), fulls=()):
    seq, f = ug.shape
    tc = 256
    tr = _pick(seq, (HALO_TR, 512, 256))
    nc, nr = f // tc, seq // tr
    n = tr + 2 * HALO
    nt = plan.nt if plan is not None else 0

    def body(g_ref, gp_ref, gn_ref, v_ref, vp_ref, vn_ref, wg_ref, wv_ref, bg_ref, bv_ref, *rest):
        o_ref = rest[2 * nt]
        j, r = pl.program_id(0), pl.program_id(1)
        if plan is not None:
            plan.bind(rest[:nt], rest[2 * nt + 1:3 * nt + 1], *rest[3 * nt + 1:])

            @pl.when(jnp.logical_and(j == 0, r == 0))
            def _():
                plan.start()

        cg = _conv3(_extend(g_ref, gp_ref, gn_ref, r, nr), wg_ref, n)[HALO:HALO + tr] + bg_ref[...]
        cv = _conv3(_extend(v_ref, vp_ref, vn_ref, r, nr), wv_ref, n)[HALO:HALO + tr] + bv_ref[...]
        o_ref[...] = (cg * jax.nn.sigmoid(cg) * cv).astype(o_ref.dtype)

        if plan is not None:
            @pl.when(jnp.logical_and(j == nc - 1, r == nr - 1))
            def _():
                plan.forward()
                plan.finish()

    w_spec = lambda shift: pl.BlockSpec((None, 3, tc), lambda j, r: (layer, 0, j + shift))
    b_spec = lambda shift: pl.BlockSpec((None, 1, tc), lambda j, r: (layer, 0, j + shift))
    act_shape = jax.ShapeDtypeStruct((seq, f), BF16)
    act_spec = pl.BlockSpec((tr, tc), lambda j, r: (r, j))
    in_specs = _halo_specs(tr, tc, seq, lambda j: j) * 2 + [w_spec(0), w_spec(nc), b_spec(0), b_spec(nc)]
    operands = (ug, ug, ug, uv, uv, uv, cw, cw, cb, cb)
    if plan is None:
        return pl.pallas_call(body, out_shape=act_shape, grid=(nc, nr), in_specs=in_specs, out_specs=act_spec,
                              compiler_params=_cp(("parallel", "parallel")), name=name)(*operands)
    return pl.pallas_call(
        body, out_shape=(act_shape, *plan.out_shape), grid=(nc, nr), in_specs=in_specs + [ANY] * (2 * nt),
        out_specs=(act_spec, *([ANY] * nt)), scratch_shapes=plan.scratch,
        input_output_aliases={len(operands) + nt + t: 1 + t for t in range(nt)},
        compiler_params=_cp(("arbitrary", "arbitrary")), name=name)(*operands, *shards, *fulls)


def _conv_gate_bwd(ug, uv, dact, cw, cb, layer, name):
    seq, f = ug.shape
    tc = 256
    tr = _pick(seq, (512, 256))
    nc, nr = f // tc, seq // tr
    n = tr + 2 * HALO

    def body(g_ref, gp_ref, gn_ref, v_ref, vp_ref, vn_ref, d_ref, dp_ref, dn_ref, wg_ref, wv_ref, bg_ref, bv_ref,
             dug_ref, duv_ref, dwg_ref, dwv_ref):
        r = pl.program_id(1)
        eg = _extend(g_ref, gp_ref, gn_ref, r, nr)
        ev = _extend(v_ref, vp_ref, vn_ref, r, nr)
        da = _extend(d_ref, dp_ref, dn_ref, r, nr)
        eg3 = (pltpu.roll(eg, 1, axis=0), eg, pltpu.roll(eg, n - 1, axis=0))
        ev3 = (pltpu.roll(ev, 1, axis=0), ev, pltpu.roll(ev, n - 1, axis=0))
        cg = eg3[0] * wg_ref[0:1, :] + eg3[1] * wg_ref[1:2, :] + eg3[2] * wg_ref[2:3, :] + bg_ref[...]
        cv = ev3[0] * wv_ref[0:1, :] + ev3[1] * wv_ref[1:2, :] + ev3[2] * wv_ref[2:3, :] + bv_ref[...]
        sg = jax.nn.sigmoid(cg)
        dcv = da * (cg * sg)
        dcg = da * cv * (sg * (1.0 + cg * (1.0 - sg)))

        def back(dc, e3, w_ref, du_ref, dw_ref):
            du = (pltpu.roll(dc, n - 1, axis=0) * w_ref[0:1, :] + dc * w_ref[1:2, :]
                  + pltpu.roll(dc, 1, axis=0) * w_ref[2:3, :])
            du_ref[...] = du[HALO:HALO + tr].astype(du_ref.dtype)
            dcm = dc[HALO:HALO + tr]
            taps = [jnp.sum(dcm * e[HALO:HALO + tr], axis=0, keepdims=True) for e in e3] + [
                    jnp.sum(dcm, axis=0, keepdims=True)]
            part = jnp.concatenate(taps + [jnp.zeros((4, tc), F32)], axis=0)

            @pl.when(r == 0)
            def _():
                dw_ref[...] = part

            @pl.when(r > 0)
            def _():
                dw_ref[...] += part

        back(dcg, eg3, wg_ref, dug_ref, dwg_ref)
        back(dcv, ev3, wv_ref, duv_ref, dwv_ref)

    w_spec = lambda shift: pl.BlockSpec((None, 3, tc), lambda j, r: (layer, 0, j + shift))
    b_spec = lambda shift: pl.BlockSpec((None, 1, tc), lambda j, r: (layer, 0, j + shift))
    out_rows = pl.BlockSpec((tr, tc), lambda j, r: (r, j))
    out_acc = pl.BlockSpec((8, tc), lambda j, r: (0, j))
    return pl.pallas_call(
        body,
        out_shape=(jax.ShapeDtypeStruct((seq, f), BF16), jax.ShapeDtypeStruct((seq, f), BF16),
                   jax.ShapeDtypeStruct((8, f), F32), jax.ShapeDtypeStruct((8, f), F32)),
        grid=(nc, nr),
        in_specs=_halo_specs(tr, tc, seq, lambda j: j) * 3 + [w_spec(0), w_spec(nc), b_spec(0), b_spec(nc)],
        out_specs=(out_rows, out_rows, out_acc, out_acc),
        compiler_params=_cp(("parallel", "arbitrary")), name=name)(ug, ug, ug, uv, uv, uv, dact, dact, dact, cw, cw, cb, cb)


def _pool_count(g, r, tr, n, seq):
    half = jnp.left_shift(1, g)
    t = r * tr - HALO + lax.broadcasted_iota(jnp.int32, (n, 1), 0)
    cnt = jnp.minimum(t + half, seq) - jnp.maximum(t - half, 0)
    return jnp.maximum(cnt, 1).astype(F32)


def _by_group(g, levels):
    out = levels[3]
    for i in (2, 1, 0):
        out = jnp.where(g == i, levels[i], out)
    return out


def _pool_mixed(e, g, cnt, n):
    w2 = e + pltpu.roll(e, 1, axis=0)
    w4 = pltpu.roll(w2, 1, axis=0) + pltpu.roll(w2, n - 1, axis=0)
    w8 = pltpu.roll(w4, 2, axis=0) + pltpu.roll(w4, n - 2, axis=0)
    w16 = pltpu.roll(w8, 4, axis=0) + pltpu.roll(w8, n - 4, axis=0)
    return _by_group(g, (w2, w4, w8, w16)) / cnt - e


def _pool_fwd(hp, xres, pw, scale, name, plan, shards, fulls):
    seq, d = hp.shape
    tc = POOL_GROUP_W
    tr = _pick(seq, (HALO_TR, 512, 256))
    nr = seq // tr
    n = tr + 2 * HALO
    nt = plan.nt

    def body(h_ref, hp_ref, hn_ref, x_ref, w_ref, s_ref, *rest):
        o_ref = rest[2 * nt]
        plan.bind(rest[:nt], rest[2 * nt + 1:3 * nt + 1], *rest[3 * nt + 1:])
        g, r = pl.program_id(0), pl.program_id(1)

        def compute():
            e = _extend(h_ref, hp_ref, hn_ref, r, nr)
            mixed = _pool_mixed(e, g, _pool_count(g, r, tr, n, seq), n)[HALO:HALO + tr]
            y = jnp.dot(mixed.astype(BF16), w_ref[...], preferred_element_type=F32)
            o_ref[...] = x_ref[...] + y * s_ref[...]

        _hosted(plan, POOL_GROUPS * nr, g * nr + r, compute)

    operands = (hp, hp, hp, xres, pw, scale)
    return pl.pallas_call(
        body, out_shape=(jax.ShapeDtypeStruct((seq, d), F32), *plan.out_shape), grid=(POOL_GROUPS, nr),
        in_specs=_halo_specs(tr, tc, seq, lambda j: j) + [
            pl.BlockSpec((tr, tc), lambda j, r: (r, j)), pl.BlockSpec((None, tc, tc), lambda j, r: (j, 0, 0)),
            pl.BlockSpec((1, tc), lambda j, r: (0, j))] + [ANY] * (2 * nt),
        out_specs=(pl.BlockSpec((tr, tc), lambda j, r: (r, j)), *([ANY] * nt)), scratch_shapes=plan.scratch,
        input_output_aliases={len(operands) + nt + t: 1 + t for t in range(nt)},
        compiler_params=_cp(("arbitrary", "arbitrary")), name=name)(*operands, *shards, *fulls)


def _pool_bwd(hp, dy, pw, scale, name):
    seq, d = hp.shape
    tc = POOL_GROUP_W
    tr = _pick(seq, (HALO_TR, 512, 256))
    nr = seq // tr
    n = tr + 2 * HALO

    def body(h_ref, hp_ref, hn_ref, d_ref, dp_ref, dn_ref, w_ref, s_ref, dh_ref, dw_ref, ds_ref):
        g, r = pl.program_id(0), pl.program_id(1)
        cnt = _pool_count(g, r, tr, n, seq)
        e = _extend(h_ref, hp_ref, hn_ref, r, nr)
        mixed = _pool_mixed(e, g, cnt, n)[HALO:HALO + tr].astype(BF16)
        dye = _extend(d_ref, dp_ref, dn_ref, r, nr)
        dyp = (dye * s_ref[...]).astype(BF16)
        dmixed = lax.dot_general(dyp, w_ref[...], (((1,), (1,)), ((), ())), preferred_element_type=F32)
        dwin = dmixed / cnt
        m2 = dwin + pltpu.roll(dwin, n - 1, axis=0)
        m4 = pltpu.roll(m2, 1, axis=0) + pltpu.roll(m2, n - 1, axis=0)
        m8 = pltpu.roll(m4, 2, axis=0) + pltpu.roll(m4, n - 2, axis=0)
        m16 = pltpu.roll(m8, 4, axis=0) + pltpu.roll(m8, n - 4, axis=0)
        dh_ref[...] = (_by_group(g, (m2, m4, m8, m16)) - dmixed)[HALO:HALO + tr]
        ypre = jnp.dot(mixed, w_ref[...], preferred_element_type=F32)
        dsp = jnp.sum(d_ref[...] * ypre, axis=0, keepdims=True)
        dwp = lax.dot_general(mixed, dyp[HALO:HALO + tr], (((0,), (0,)), ((), ())), preferred_element_type=F32)

        @pl.when(r == 0)
        def _():
            dw_ref[...] = dwp
            ds_ref[...] = dsp

        @pl.when(r > 0)
        def _():
            dw_ref[...] += dwp
            ds_ref[...] += dsp

    return pl.pallas_call(
        body,
        out_shape=(jax.ShapeDtypeStruct((seq, d), F32), jax.ShapeDtypeStruct((POOL_GROUPS, tc, tc), F32),
                   jax.ShapeDtypeStruct((1, d), F32)),
        grid=(POOL_GROUPS, nr),
        in_specs=_halo_specs(tr, tc, seq, lambda j: j) * 2 + [
            pl.BlockSpec((None, tc, tc), lambda j, r: (j, 0, 0)), pl.BlockSpec((1, tc), lambda j, r: (0, j))],
        out_specs=(pl.BlockSpec((tr, tc), lambda j, r: (r, j)), pl.BlockSpec((None, tc, tc), lambda j, r: (j, 0, 0)),
                   pl.BlockSpec((1, tc), lambda j, r: (0, j))),
        compiler_params=_cp(("parallel", "arbitrary")), name=name)(hp, hp, hp, dy, dy, dy, pw, scale)


def _adamw_math(w, g, m, v):
    m = ADAM_B1 * m + (1.0 - ADAM_B1) * g
    v = ADAM_B2 * v + (1.0 - ADAM_B2) * (g * g)
    m_hat = m / (1.0 - ADAM_B1 ** ADAM_STEP)
    v_hat = v / (1.0 - ADAM_B2 ** ADAM_STEP)
    delta = -ADAM_LR * (m_hat / (jnp.sqrt(v_hat) + ADAM_EPS) + ADAM_WD * w)
    return delta, m, v


def _adamw(w, ga, gb, m, v, name):
    rows, cols = w.shape
    tr = _pick(rows, (256, 128, 64, 32, 16, 8))
    two = gb is not None

    def body(*refs):
        if two:
            w_ref, ga_ref, gb_ref, m_ref, v_ref, g_out, d_out, m_out, v_out = refs
            g = ga_ref[...] + gb_ref[...]
        else:
            w_ref, ga_ref, m_ref, v_ref, g_out, d_out, m_out, v_out = refs
            g = ga_ref[...]
        delta, m, v = _adamw_math(w_ref[...], g, m_ref[...], v_ref[...])
        g_out[...] = g
        d_out[...] = delta
        m_out[...] = m
        v_out[...] = v

    spec = pl.BlockSpec((tr, cols), lambda i: (i, 0))
    ops = [w, ga] + ([gb] if two else []) + [m, v]
    return pl.pallas_call(
        body, out_shape=tuple(jax.ShapeDtypeStruct((rows, cols), F32) for _ in range(4)), grid=(rows // tr,),
        in_specs=[spec] * len(ops), out_specs=(spec,) * 4, compiler_params=_cp(("parallel",)), name=name)(*ops)


def _sum4(parts, name):
    _, rows, cols = parts.shape
    tr = _pick(rows, (256, 128, 64, 32, 16))

    def body(p_ref, o_ref):
        acc = p_ref[0].astype(F32)
        for kk in range(1, 4):
            acc = acc + p_ref[kk].astype(F32)
        o_ref[...] = acc

    return pl.pallas_call(
        body, out_shape=jax.ShapeDtypeStruct((rows, cols), F32), grid=(rows // tr,),
        in_specs=[pl.BlockSpec((4, tr, cols), lambda i: (0, i, 0))], out_specs=pl.BlockSpec((tr, cols), lambda i: (i, 0)),
        compiler_params=_cp(("parallel",)), name=name)(parts)


def _place():
    x, y, c = lax.axis_index("x"), lax.axis_index("y"), lax.axis_index("c")
    chips = [(1 - x, y), (x, 1 - y), (1 - x, 1 - y)]
    return x, y, c, chips


def _window(ref, axis, j, size, c=None, half=None, lead=(), layers=slice(None)):
    if axis == "r":
        if c is None:
            return ref.at[lead + (layers, pl.ds(pl.multiple_of(j * size, 32), size), slice(None))]
        return ref.at[lead + (layers, pl.ds(pl.multiple_of(j * size + c * half, 32), half), slice(None))]
    cols = pl.ds(pl.multiple_of(j * size, LANES), size)
    if c is None:
        return ref.at[lead + (layers, slice(None), cols)]
    return ref.at[lead + (layers, pl.ds(pl.multiple_of(c * half, 32), half), cols)]


class _Gather:
    def __init__(self, shards, axes, layers=None):
        self.nt, self.axes = len(shards), axes
        self.layers = layers or [slice(None)] * self.nt
        self.out_shape, self.sizes, self.halves = [], [], []
        for s, ax in zip(shards, axes):
            l, rs, cs = s.shape
            self.out_shape.append(jax.ShapeDtypeStruct((l, 4 * rs, cs) if ax == "r" else (l, rs, 4 * cs), s.dtype))
            self.sizes.append(rs if ax == "r" else cs)
            self.halves.append(rs // 2)
        self.scratch = [pltpu.SemaphoreType.DMA((6 * self.nt,)), pltpu.SemaphoreType.DMA((6 * self.nt,)),
                        pltpu.SemaphoreType.DMA((self.nt,))]

    def bind(self, src, dst, send_sems, recv_sems, local_sems):
        self.src, self.dst, self.send_sems, self.recv_sems, self.local_sems = src, dst, send_sems, recv_sems, local_sems

    def _win(self, t, j, core=None):
        return _window(self.dst[t], self.axes[t], j, self.sizes[t], core, self.halves[t], layers=self.layers[t])

    def _ici(self, t, kk, origin):
        _, _, c, chips = _place()
        px, py = chips[kk]
        half = self.src[t].at[self.layers[t], pl.ds(pl.multiple_of(c * self.halves[t], 16), self.halves[t]), :]
        return pltpu.make_async_remote_copy(
            src_ref=half, dst_ref=self._win(t, origin, c), send_sem=self.send_sems.at[t * 3 + kk],
            recv_sem=self.recv_sems.at[t * 3 + kk], device_id=(px, py, c), device_id_type=MESH)

    def _d2d(self, t, kk, origin, core):
        x, y, c, _ = _place()
        k2 = 3 * self.nt + t * 3 + kk
        return pltpu.make_async_remote_copy(
            src_ref=self._win(t, origin, core), dst_ref=self._win(t, origin, core), send_sem=self.send_sems.at[k2],
            recv_sem=self.recv_sems.at[k2], device_id=(x, y, 1 - c), device_id_type=MESH)

    def _local(self, t):
        x, y, _, _ = _place()
        return pltpu.make_async_copy(self.src[t].at[self.layers[t]], self._win(t, 2 * x + y), self.local_sems.at[t])

    def _each(self):
        _, _, _, chips = _place()
        for t in range(self.nt):
            for kk in range(3):
                px, py = chips[kk]
                yield t, kk, 2 * px + py

    def start(self):
        x, y, _, _ = _place()
        for t in range(self.nt):
            self._local(t).start()
        for t, kk, _ in self._each():
            self._ici(t, kk, 2 * x + y).start()

    def forward(self):
        _, _, c, _ = _place()
        for t, kk, origin in self._each():
            self._ici(t, kk, origin).wait_recv()
            self._d2d(t, kk, origin, c).start()

    def finish(self):
        x, y, c, _ = _place()
        for t, kk, origin in self._each():
            self._d2d(t, kk, origin, 1 - c).wait_recv()
        for t, kk, origin in self._each():
            self._ici(t, kk, 2 * x + y).wait_send()
            self._d2d(t, kk, origin, c).wait_send()
        for t in range(self.nt):
            self._local(t).wait()


class _Scatter:
    def __init__(self, grads, axes):
        self.nt, self.axes = len(grads), axes
        self.out_shape, self.sizes = [], []
        for gr, ax in zip(grads, axes):
            l, r, cc = gr.shape
            self.out_shape.append(jax.ShapeDtypeStruct((4, l, r // 4, cc) if ax == "r" else (4, l, r, cc // 4), gr.dtype))
            self.sizes.append(r // 4 if ax == "r" else cc // 4)
        self.scratch = [pltpu.SemaphoreType.DMA((3 * self.nt,)), pltpu.SemaphoreType.DMA((3 * self.nt,)),
                        pltpu.SemaphoreType.DMA((self.nt,))]

    def bind(self, src, dst, send_sems, recv_sems, local_sems):
        self.src, self.dst, self.send_sems, self.recv_sems, self.local_sems = src, dst, send_sems, recv_sems, local_sems

    def _copy(self, t, kk, slot):
        x, y, c, chips = _place()
        px, py = chips[kk]
        return pltpu.make_async_remote_copy(
            src_ref=_window(self.src[t], self.axes[t], 2 * px + py, self.sizes[t]), dst_ref=self.dst[t].at[slot],
            send_sem=self.send_sems.at[t * 3 + kk], recv_sem=self.recv_sems.at[t * 3 + kk],
            device_id=(px, py, c), device_id_type=MESH)

    def _local(self, t):
        x, y, _, _ = _place()
        me = 2 * x + y
        return pltpu.make_async_copy(_window(self.src[t], self.axes[t], me, self.sizes[t]), self.dst[t].at[me],
                                     self.local_sems.at[t])

    def start(self):
        x, y, _, _ = _place()
        for t in range(self.nt):
            self._local(t).start()
            for kk in range(3):
                self._copy(t, kk, 2 * x + y).start()

    def finish(self):
        _, _, _, chips = _place()
        for t in range(self.nt):
            for kk in range(3):
                px, py = chips[kk]
                self._copy(t, kk, 2 * px + py).wait_recv()
        for t in range(self.nt):
            for kk in range(3):
                px, py = chips[kk]
                self._copy(t, kk, 2 * px + py).wait_send()
            self._local(t).wait()


def _comm_call(plans, operands, name):
    nts = [p.nt for p in plans]
    n_in, n_sem = sum(nts), [len(p.scratch) for p in plans]

    def body(*refs):
        pos_in, pos_out, pos_sem = 0, n_in, 2 * n_in
        for p, nt, ns in zip(plans, nts, n_sem):
            p.bind(refs[pos_in:pos_in + nt], refs[pos_out:pos_out + nt], *refs[pos_sem:pos_sem + ns])
            pos_in, pos_out, pos_sem = pos_in + nt, pos_out + nt, pos_sem + ns
        for p in plans:
            p.start()
        for p in plans:
            if hasattr(p, "forward"):
                p.forward()
        for p in plans:
            p.finish()

    return pl.pallas_call(
        body, out_shape=tuple(s for p in plans for s in p.out_shape), in_specs=[ANY] * n_in,
        out_specs=tuple([ANY] * n_in), scratch_shapes=[s for p in plans for s in p.scratch],
        name=name)(*[a for ops in operands for a in ops])


class _GatherAll:
    FLIPS = [f for f in itertools.product((0, 1), repeat=3) if any(f)]

    def __init__(self, pack):
        self.nt = 1
        self.out_shape = [jax.ShapeDtypeStruct((8,) + pack.shape, pack.dtype)]
        self.scratch = [pltpu.SemaphoreType.DMA((7,)), pltpu.SemaphoreType.DMA((7,)), pltpu.SemaphoreType.DMA((1,))]

    def bind(self, src, dst, send_sems, recv_sems, local_sems):
        self.src, self.dst, self.send_sems, self.recv_sems, self.local_sems = src[0], dst[0], send_sems, recv_sems, local_sems

    def _copy(self, kk, mine):
        x, y, c, _ = _place()
        px, py, pc = (1 - v if fl else v for v, fl in zip((x, y, c), self.FLIPS[kk]))
        slot = 4 * x + 2 * y + c if mine else 4 * px + 2 * py + pc
        return pltpu.make_async_remote_copy(src_ref=self.src, dst_ref=self.dst.at[slot], send_sem=self.send_sems.at[kk],
                                            recv_sem=self.recv_sems.at[kk], device_id=(px, py, pc), device_id_type=MESH)

    def _local(self):
        x, y, c, _ = _place()
        return pltpu.make_async_copy(self.src, self.dst.at[4 * x + 2 * y + c], self.local_sems.at[0])

    def start(self):
        self._local().start()
        for kk in range(7):
            self._copy(kk, True).start()

    def finish(self):
        for kk in range(7):
            self._copy(kk, False).wait_recv()
        for kk in range(7):
            self._copy(kk, True).wait_send()
        self._local().wait()


class _Swap:
    def __init__(self, arrs):
        self.nt = len(arrs)
        self.out_shape = [jax.ShapeDtypeStruct(a.shape, a.dtype) for a in arrs]
        self.scratch = [pltpu.SemaphoreType.DMA((self.nt,)), pltpu.SemaphoreType.DMA((self.nt,))]

    def bind(self, src, dst, send_sems, recv_sems):
        self.src, self.dst, self.send_sems, self.recv_sems = src, dst, send_sems, recv_sems

    def _copy(self, t):
        x, y, c, _ = _place()
        return pltpu.make_async_remote_copy(src_ref=self.src[t], dst_ref=self.dst[t], send_sem=self.send_sems.at[t],
                                            recv_sem=self.recv_sems.at[t], device_id=(x, y, 1 - c), device_id_type=MESH)

    def start(self):
        for t in range(self.nt):
            self._copy(t).start()

    def finish(self):
        for t in range(self.nt):
            self._copy(t).wait()


def _sum8(packs, name):
    def body(p_ref, o_ref):
        acc = p_ref[0]
        for dev in range(1, 8):
            acc = acc + p_ref[dev]
        o_ref[...] = acc

    return pl.pallas_call(body, out_shape=jax.ShapeDtypeStruct(packs.shape[1:], F32), name=name)(packs)


def _pack(arrs):
    flat = jnp.concatenate([a.reshape(-1).astype(F32) for a in arrs])
    rows = -(-flat.shape[0] // (8 * LANES)) * 8
    return jnp.pad(flat, (0, rows * LANES - flat.shape[0])).reshape(rows, LANES)


def _unpack(flat, shapes):
    out, pos = [], 0
    for shp in shapes:
        size = 1
        for s in shp:
            size *= s
        out.append(flat[pos:pos + size].reshape(shp))
        pos += size
    return out


BIG = ("attn_w_qkv", "attn_w_o", "pool_w", "xattn_w_q", "xattn_w_kv", "xattn_w_o", "ffn_w_up", "ffn_w_down")
BIG_AXIS = ("c", "r", "r", "r", "c", "r", "c", "r")
SMALL_REPL = ("attn_norm", "attn_q_gain", "attn_k_gain", "xattn_norm", "mem_norm", "ffn_norm", "ffn_conv_b", "final_norm")
SMALL_SHARD = ("pool_norm", "pool_scale", "ffn_conv_w")
ORDER = ("attn_norm", "attn_w_qkv", "attn_q_gain", "attn_k_gain", "attn_w_o", "pool_norm", "pool_w", "pool_scale",
         "xattn_norm", "mem_norm", "xattn_w_q", "xattn_w_kv", "xattn_w_o", "ffn_norm", "ffn_w_up", "ffn_conv_w",
         "ffn_conv_b", "ffn_w_down", "final_norm")


def _step(x, mem, tgt, w, m, v):
    seq, d = x.shape
    xi, yi, ci = lax.axis_index("x"), lax.axis_index("y"), lax.axis_index("c")
    chip = 2 * xi + yi
    dff = w["ffn_w_down"].shape[1] * 4
    n_layers = w["ffn_norm"].shape[0]

    def as3d(a):
        return a.reshape(a.shape[-3:])
    shards = [as3d(w[nm]).astype(BF16) for nm in BIG]
    small_in = [w[nm] for nm in SMALL_SHARD]
    small_pack = _pack(small_in)
    conv_b = w["ffn_conv_b"].reshape(n_layers, 1, -1)
    tabs = _rope_tables(seq)
    qg2 = jnp.tile(w["attn_q_gain"], (1, 2))
    kg2 = jnp.tile(w["attn_k_gain"], (1, 2))
    mm = functools.partial(_mm)

    saved = {}
    x0 = x
    h0, wq = _rms_fwd(x0, w["attn_norm"], BF16, "rms_attn", _Gather(shards[:1], BIG_AXIS[:1]), shards[:1])
    qkv = mm(h0, wq, "nn", b_l=0, out_dtype=F32, name="mm_qkv")
    q_r, k_r, k_t, v_b, v_t, small_all = _qk_prep(qkv, qg2, kg2, tabs, _GatherAll(small_pack), [small_pack], "qk_prep")
    per_chip = [_unpack(small_all[2 * j].reshape(-1), [a.shape for a in small_in]) for j in range(4)]
    pool_norm, pool_scale, conv_w = (jnp.concatenate([per_chip[j][i] for j in range(4)], axis=-1) for i in range(3))
    first = [slice(None)] * 5 + [slice(0, 1)] * 2
    o_at, lse, wo, wp, wxq, wxkv, wxo, wup, wdn = _flash_fwd(
        q_r, k_r, v_t, _Gather(shards[1:], BIG_AXIS[1:], first), shards[1:], "flash_fwd")
    ffn_w = {"up": wup, "down": wdn}
    x1, hq0 = mm(o_at, wo, "nn", b_l=0, res=x0, out_dtype=F32, norm_out=(w["xattn_norm"][0:1], BF16), name="mm_attn_o")

    def xattn_fwd(l, xin, hq):
        mn = _rms_fwd(mem, w["mem_norm"][l:l + 1], BF16, f"rms_mem{l}")
        xq = mm(hq, wxq, "nn", b_l=l, scale=X_HEAD_DIM ** -0.5, out_dtype=BF16, name=f"mm_xq{l}")
        kv = mm(mn, wxkv, "nn", b_l=l, out_dtype=BF16, name=f"mm_xkv{l}")
        xo = _xattn_fwd(xq, kv, f"xattn_fwd{l}")
        saved[f"x{l}"] = (hq, mn, xq, kv, xo)
        return mm(xo, wxo, "nn", b_l=l, res=xin, out_dtype=F32, norm_out=(w["ffn_norm"][l:l + 1], BF16), name=f"mm_xo{l}")

    def ffn_fwd(l, xin, hf, norm_out):
        ug = mm(hf, ffn_w["up"], "nn", b_l=l, n=dff, out_dtype=BF16, name=f"mm_up_g{l}")
        uv = mm(hf, ffn_w["up"], "nn", b_l=l, n=dff, b_off=(0, dff), out_dtype=BF16, name=f"mm_up_v{l}")
        if l == 0:
            rest = _Gather(shards[6:7], BIG_AXIS[6:7], [slice(1, 2)])
            act, ffn_w["up"] = _conv_gate_fwd(ug, uv, conv_w, conv_b, l, f"conv_gate{l}", rest, shards[6:7], [ffn_w["up"]])
        else:
            act = _conv_gate_fwd(ug, uv, conv_w, conv_b, l, f"conv_gate{l}")
        saved[f"f{l}"] = (hf, ug, uv, act)
        return mm(act, ffn_w["down"], "nn", b_l=l, res=xin, out_dtype=F32, norm_out=norm_out, name=f"mm_down{l}")

    x2, hf0 = xattn_fwd(0, x1, hq0)
    x3, hp = ffn_fwd(0, x2, hf0, (pool_norm, F32))
    x4, ffn_w["down"] = _pool_fwd(hp, x3, wp, pool_scale, "pool_fwd", _Gather(shards[7:], BIG_AXIS[7:], [slice(1, 2)]),
                                  shards[7:], [ffn_w["down"]])
    x5, hf1 = xattn_fwd(1, x4, _rms_fwd(x4, w["xattn_norm"][1:2], BF16, "rms_xq1"))
    xs = [x0, x1, x2, x3, x4, x5, ffn_fwd(1, x5, hf1, None)]
    dres, g_final, loss, dres_b = _final_loss(xs[6], w["final_norm"].reshape(1, d), tgt, "final_loss")

    grads = {}
    gbuf = {}

    def dw(nm, a, b, layer, full, off=(0, 0), n=None, tn=None):
        gbuf[nm] = _mm(a, b, "tn", out_dtype=BF16, out_full=full, out_l=layer, out_off=off, n=n, tn=tn,
                       alias=gbuf.get(nm), name=f"dw_{nm}{layer}_{off[1]}")

    def ffn_bwd(l, xin, dres, dres_b):
        hf, ug, uv, act = saved[f"f{l}"]
        wup, wdn = ffn_w["up"], ffn_w["down"]
        dw("ffn_w_down", act, dres_b, l, wdn.shape)
        dact = _mm(dres_b, wdn, "nt", b_l=l, out_dtype=BF16, name=f"mm_dact{l}")
        dug, duv, dwg, dwv = _conv_gate_bwd(ug, uv, dact, conv_w, conv_b, l, f"conv_gate_bwd{l}")
        dw("ffn_w_up", hf, dug, l, wup.shape, tn=1408)
        dw("ffn_w_up", hf, duv, l, wup.shape, off=(0, dff), tn=1408)
        dhf = _mm(dug, wup, "nt", b_l=l, n=d, out_dtype=F32, name=f"mm_dhf_g{l}")
        dres, dg, dres_b = _mm(duv, wup, "nt", b_l=l, n=d, b_off=(0, dff), res=dhf, out_dtype=F32, tm=256,
                               norm_bwd=(xin, w["ffn_norm"][l:l + 1], dres), name=f"mm_dhf_v{l}")
        return (dres, dres_b), dg, jnp.concatenate([dwg[:3], dwv[:3]], axis=1), jnp.concatenate([dwg[3], dwv[3]], axis=0)

    def xattn_bwd(l, xin, dres, dres_b):
        hq, mn, xq, kv, xo = saved[f"x{l}"]
        dw("xattn_w_o", xo, dres_b, l, wxo.shape)
        dxo = _mm(dres_b, wxo, "nt", b_l=l, out_dtype=BF16, name=f"mm_dxo{l}")
        dq, dkv = _xattn_bwd(xq, kv, dxo, f"xattn_bwd{l}")
        dw("xattn_w_q", hq, dq, l, wxq.shape)
        dw("xattn_w_kv", mn, dkv, l, wxkv.shape)
        dmn = _mm(dkv, wxkv, "nt", b_l=l, out_dtype=F32, name=f"mm_dmn{l}")
        _, dg_mem = _rms_bwd(mem, w["mem_norm"][l:l + 1], dmn, None, f"rms_mem_bwd{l}")
        dres, dg, dres_b = _mm(dq, wxq, "nt", b_l=l, out_dtype=F32, norm_bwd=(xin, w["xattn_norm"][l:l + 1], dres),
                               name=f"mm_dhq{l}")
        return (dres, dres_b), dg, dg_mem

    g_ffn, g_xn, g_mn, g_cw, g_cb = [None] * n_layers, [None] * n_layers, [None] * n_layers, [None] * n_layers, [None] * n_layers
    (dres, dres_b), g_ffn[1], g_cw[1], g_cb[1] = ffn_bwd(1, xs[5], dres, dres_b)
    (dres, _), g_xn[1], g_mn[1] = xattn_bwd(1, xs[4], dres, dres_b)
    dhp, g_pw, g_pscale = _pool_bwd(hp, dres, wp, pool_scale, "pool_bwd")
    dres, g_pnorm = _rms_bwd(xs[3], pool_norm, dhp, dres, "rms_pool_bwd")
    (dres, dres_b), g_ffn[0], g_cw[0], g_cb[0] = ffn_bwd(0, xs[2], dres, dres)
    (dres, dres_b), g_xn[0], g_mn[0] = xattn_bwd(0, xs[1], dres, dres_b)
    dw("attn_w_o", o_at, dres_b, 0, wo.shape)
    do = _mm(dres_b, wo, "nt", b_l=0, out_dtype=BF16, name="mm_do")
    gbuf["pool_w"] = g_pw.astype(BF16)
    early = [gbuf[nm] for nm in BIG[1:]]
    dq_r, dk_r, dv, *recv_early = _flash_bwd(q_r, k_r, k_t, v_b, do, o_at, lse, _Scatter(early, BIG_AXIS[1:]), early,
                                             "flash_bwd")
    def sum4(nm, rc):
        return _sum4(rc.reshape(4, -1, rc.shape[-1]), f"sum4_{nm}")
    sums_early = [sum4(nm, rc) for nm, rc in zip(BIG[1:], recv_early)]
    dqkv, dqg, dkg, *others_early = _qk_prep_bwd(qkv, dq_r, dk_r, dv, qg2, kg2, tabs, _Swap(sums_early), sums_early,
                                                 "qk_prep_bwd")
    dw("attn_w_qkv", h0, dqkv, 0, wq.shape)
    grad_x, g_an, _ = _mm(dqkv, wq, "nt", b_l=0, out_dtype=F32, norm_bwd=(x0, w["attn_norm"], dres), name="mm_dh0")

    small_g = {
        "attn_norm": g_an, "attn_q_gain": dqg[:, :HEAD_DIM] + dqg[:, HEAD_DIM:], "attn_k_gain": dkg[:, :HEAD_DIM] + dkg[:, HEAD_DIM:],
        "xattn_norm": jnp.concatenate(g_xn, axis=0), "mem_norm": jnp.concatenate(g_mn, axis=0),
        "ffn_norm": jnp.concatenate(g_ffn, axis=0), "ffn_conv_b": jnp.stack(g_cb, axis=0), "final_norm": g_final.reshape(d),
        "pool_norm": g_pnorm, "pool_scale": g_pscale, "ffn_conv_w": jnp.stack(g_cw, axis=0)}
    names = SMALL_REPL + SMALL_SHARD
    small_pack = _pack([loss[0, :1]] + [small_g[nm] for nm in names])
    late = [gbuf[nm] for nm in BIG[:1]]
    small_all, recv_late = _comm_call([_GatherAll(small_pack), _Scatter(late, BIG_AXIS[:1])], [[small_pack], late],
                                      "reduce_small_scatter_qkv")
    total = _sum8(small_all, "sum_small")
    parts = _unpack(total.reshape(-1), [(1,)] + [small_g[nm].shape for nm in names])
    loss_out = parts[0][0]
    for nm, g in zip(names, parts[1:]):
        if nm in SMALL_SHARD:
            size = w[nm].shape[-1]
            g = lax.dynamic_slice_in_dim(g, chip * size, size, axis=g.ndim - 1)
        grads[nm] = g.reshape(w[nm].shape)

    packed = [_pack([src[nm] for nm in names]) for src in (w, grads, m, v)]
    _, sd, sm, sv = _adamw(packed[0], packed[1], None, packed[2], packed[3], "adamw_small")
    shapes = [w[nm].shape for nm in names]
    delta = dict(zip(names, _unpack(sd.reshape(-1), shapes)))
    new_m = dict(zip(names, _unpack(sm.reshape(-1), shapes)))
    new_v = dict(zip(names, _unpack(sv.reshape(-1), shapes)))

    sums_late = [sum4(BIG[0], recv_late)]
    others_late = _comm_call([_Swap(sums_late)], [sums_late], "swap_qkv")
    for nm, mine, other in zip(BIG, sums_late + sums_early, list(others_late) + others_early):
        cols = mine.shape[-1]
        outs = _adamw(w[nm].reshape(-1, cols), mine, other, m[nm].reshape(-1, cols), v[nm].reshape(-1, cols), f"adamw_{nm}")
        grads[nm], delta[nm], new_m[nm], new_v[nm] = (o.reshape(w[nm].shape) for o in outs)

    return loss_out, grad_x, grads, delta, new_m, new_v


def kernel(x, mem, attn_norm, attn_w_qkv, attn_q_gain, attn_k_gain, attn_w_o, pool_norm, pool_w, pool_scale, xattn_norm, mem_norm, xattn_w_q, xattn_w_kv, xattn_w_o, ffn_norm, ffn_w_up, ffn_conv_w, ffn_conv_b, ffn_w_down, final_norm, loss_target, m_attn_norm, m_attn_w_qkv, m_attn_q_gain, m_attn_k_gain, m_attn_w_o, m_pool_norm, m_pool_w, m_pool_scale, m_xattn_norm, m_mem_norm, m_xattn_w_q, m_xattn_w_kv, m_xattn_w_o, m_ffn_norm, m_ffn_w_up, m_ffn_conv_w, m_ffn_conv_b, m_ffn_w_down, m_final_norm, v_attn_norm, v_attn_w_qkv, v_attn_q_gain, v_attn_k_gain, v_attn_w_o, v_pool_norm, v_pool_w, v_pool_scale, v_xattn_norm, v_mem_norm, v_xattn_w_q, v_xattn_w_kv, v_xattn_w_o, v_ffn_norm, v_ffn_w_up, v_ffn_conv_w, v_ffn_conv_b, v_ffn_w_down, v_final_norm):
    given = dict(locals())
    w = {nm: given[nm] for nm in ORDER}
    m = {nm: given["m_" + nm] for nm in ORDER}
    v = {nm: given["v_" + nm] for nm in ORDER}
    seq, d = x.shape[1], x.shape[2]
    loss, grad_x, grads, delta, new_m, new_v = _step(
        x.reshape(seq, d), mem.reshape(mem.shape[1], d), loss_target.reshape(seq, d), w, m, v)
    return (loss, grad_x.reshape(x.shape), *[grads[nm] for nm in ORDER], *[delta[nm] for nm in ORDER],
            *[new_m[nm] for nm in ORDER], *[new_v[nm] for nm in ORDER])
```
